```python
import jax, jax.numpy as jnp
from jax import lax
import numpy as np

D_MODEL = 1024
BATCH = 8
SEQ = 4096
DEPTH = 4

N_MIXERS = 3
N_META = 16
NORM_EPS = 1e-6

FOX_HEADS = 16
FOX_HEAD_DIM = D_MODEL // FOX_HEADS
FOX_Q_BLOCK = 128
FOX_IN = 4 * D_MODEL + FOX_HEADS

GLA_HEADS = 4
GLA_DK = D_MODEL // 2 // GLA_HEADS
GLA_DV = D_MODEL // GLA_HEADS
GLA_GATE_RANK = 16
GLA_GATE_NORMALIZER = 16.0
GLA_CHUNK = 64
GLA_QK = GLA_HEADS * GLA_DK
GLA_V = GLA_HEADS * GLA_DV
GLA_IN = 2 * GLA_QK + 2 * GLA_V + GLA_GATE_RANK

GDN_HEADS = 8
GDN_DK = 128
GDN_DV = 128
GDN_CONV = 4
GDN_CHUNK = 64
GDN_CONV_DIM = 2 * GDN_HEADS * GDN_DK + GDN_HEADS * GDN_DV
GDN_IN = GDN_CONV_DIM + GDN_HEADS * GDN_DV + 2 * GDN_HEADS

D_FF = ((-(-8 * D_MODEL // 3) + 255) // 256) * 256

N_FOX = (DEPTH + 2) // 3
N_GLA = (DEPTH + 1) // 3
N_GDN = DEPTH // 3

kernel_name = "fox_gla_gdn_interleaved_hybrid"


def rmsnorm(x, g):
    xf = x.astype(jnp.float32)
    y = xf * lax.rsqrt(jnp.mean(xf * xf, axis=-1, keepdims=True) + NORM_EPS)
    return (y * g.astype(jnp.float32)).astype(x.dtype)


def l2norm(x):
    xf = x.astype(jnp.float32)
    return (xf * lax.rsqrt(jnp.sum(xf * xf, axis=-1, keepdims=True) + NORM_EPS)).astype(x.dtype)


def to_chunks(a, chunk):
    b, t = a.shape[0], a.shape[1]
    return jnp.moveaxis(a.reshape((b, t // chunk, chunk) + a.shape[2:]), 1, 0)


def from_chunks(a):
    a = jnp.moveaxis(a, 0, 1)
    return a.reshape((a.shape[0], a.shape[1] * a.shape[2]) + a.shape[3:])


def chunked_scan(chunk_fn, state0, seqs, chunk):
    state, o_meta = chunk_fn(state0, tuple(a[:, :N_META] for a in seqs))
    real = tuple(to_chunks(a[:, N_META:], chunk) for a in seqs)
    _, o_real = lax.scan(chunk_fn, state, real)
    return jnp.concatenate([o_meta, from_chunks(o_real)], axis=1)


def fox_attend(q, cq, qpos, k, v, ck, kpos):
    s = jnp.einsum('bqhd,bkhd->bhqk', q, k).astype(jnp.float32)
    bias = cq[..., :, None] - ck[..., None, :]
    mask = kpos[None, :] <= qpos[:, None]
    s = jnp.where(mask, s + bias, -jnp.inf)
    p = jax.nn.softmax(s, axis=-1)
    return jnp.einsum('bhqk,bkhd->bqhd', p.astype(v.dtype), v)


def fox_mixer(h, w_in, b_f, q_gain, k_gain, w_out):
    B, L, _ = h.shape
    H, Dh = FOX_HEADS, FOX_HEAD_DIM
    proj = h @ w_in
    q, k, v, gate, f = jnp.split(proj, [D_MODEL, 2 * D_MODEL, 3 * D_MODEL, 4 * D_MODEL], axis=-1)
    q = rmsnorm(q.reshape(B, L, H, Dh), q_gain) * (Dh ** -0.5)
    k = rmsnorm(k.reshape(B, L, H, Dh), k_gain)
    v = v.reshape(B, L, H, Dh)
    log_f = jax.nn.log_sigmoid((f + b_f).astype(jnp.float32))
    c = jnp.cumsum(log_f, axis=1).transpose(0, 2, 1)
    kpos = jnp.arange(L)

    o_meta = fox_attend(q[:, :N_META], c[:, :, :N_META], kpos[:N_META],
                        k[:, :N_META], v[:, :N_META], c[:, :, :N_META], kpos[:N_META])

    n_blk = (L - N_META) // FOX_Q_BLOCK
    q_blocks = q[:, N_META:].reshape(B, n_blk, FOX_Q_BLOCK, H, Dh).swapaxes(0, 1)
    cq_blocks = c[:, :, N_META:].reshape(B, H, n_blk, FOX_Q_BLOCK).transpose(2, 0, 1, 3)
    qpos_blocks = (N_META + jnp.arange(n_blk * FOX_Q_BLOCK)).reshape(n_blk, FOX_Q_BLOCK)

    def block(args):
        qb, cqb, qp = args
        return fox_attend(qb, cqb, qp, k, v, c, kpos)

    o_real = from_chunks(lax.map(block, (q_blocks, cq_blocks, qpos_blocks)))
    o = jnp.concatenate([o_meta, o_real], axis=1).reshape(B, L, H * Dh)
    o = o * jax.nn.sigmoid(gate)
    return (o @ w_out).astype(h.dtype)


def gla_chunk(state, inputs):
    q, k, v, g = inputs
    C = q.shape[1]
    b = jnp.cumsum(g, axis=1)
    causal = jnp.tril(jnp.ones((C, C), dtype=bool))
    diff = b[:, :, None] - b[:, None, :]
    decay = jnp.exp(jnp.where(causal[None, :, :, None, None], diff, -jnp.inf))
    A = jnp.einsum('bthd,bshd,btshd->bhts', q, k, decay)
    o_intra = jnp.einsum('bhts,bshv->bthv', A, v)
    o_inter = jnp.einsum('bthd,bhdv->bthv', q * jnp.exp(b), state)
    b_last = b[:, -1]
    k_dec = k * jnp.exp(b_last[:, None] - b)
    new_state = state * jnp.exp(b_last)[..., None] + jnp.einsum('bshd,bshv->bhdv', k_dec, v)
    return new_state, (o_intra + o_inter).astype(v.dtype)


def gla_mixer(h, w_in, w_alpha2, b_alpha, o_gain, w_out):
    B, L, _ = h.shape
    H = GLA_HEADS
    proj = h @ w_in
    q, k, v, r, a_lr = jnp.split(proj, [GLA_QK, 2 * GLA_QK, 2 * GLA_QK + GLA_V, 2 * GLA_QK + 2 * GLA_V], axis=-1)
    q = q.reshape(B, L, H, GLA_DK) * (GLA_DK ** -0.5)
    k = k.reshape(B, L, H, GLA_DK)
    v = v.reshape(B, L, H, GLA_DV)
    g = jax.nn.log_sigmoid((a_lr @ w_alpha2 + b_alpha).astype(jnp.float32)) / GLA_GATE_NORMALIZER
    g = g.reshape(B, L, H, GLA_DK)
    state0 = jnp.zeros((B, H, GLA_DK, GLA_DV), jnp.float32)
    o = chunked_scan(gla_chunk, state0, (q, k, v, g), GLA_CHUNK)
    o = rmsnorm(o, o_gain) * jax.nn.silu(r.reshape(B, L, H, GLA_DV))
    return (o.reshape(B, L, GLA_V) @ w_out).astype(h.dtype)


def causal_depthwise_conv(x, w):
    return lax.conv_general_dilated(
        x, w.astype(x.dtype), window_strides=(1,), padding=[(GDN_CONV - 1, 0)],
        dimension_numbers=('NWC', 'WIO', 'NWC'), feature_group_count=x.shape[-1])


def gdn_chunk(state, inputs):
    q, k, v, g, beta = inputs
    C = q.shape[1]
    qh = q.transpose(0, 2, 1, 3)
    kh = k.transpose(0, 2, 1, 3)
    vh = v.transpose(0, 2, 1, 3)
    bt = beta.transpose(0, 2, 1).astype(jnp.float32)[..., None]
    b = jnp.cumsum(g, axis=1).transpose(0, 2, 1)
    diff = b[..., :, None] - b[..., None, :]
    incl = jnp.tril(jnp.ones((C, C), dtype=bool))
    strict = jnp.tril(jnp.ones((C, C), dtype=bool), k=-1)
    kb = kh.astype(jnp.float32) * bt
    vb = vh.astype(jnp.float32) * bt
    lower = jnp.einsum('bhtd,bhsd->bhts', kb, kh) * jnp.exp(jnp.where(strict, diff, -jnp.inf))
    t_mat = jnp.eye(C, dtype=jnp.float32) + lower
    rhs = jnp.concatenate([vb, kb * jnp.exp(b)[..., None]], axis=-1)
    sol = lax.linalg.triangular_solve(t_mat, rhs, left_side=True, lower=True, unit_diagonal=True)
    u, w = sol[..., :GDN_DV], sol[..., GDN_DV:]
    v_new = u - jnp.einsum('bhcd,bhdv->bhcv', w, state)
    attn = jnp.einsum('bhtd,bhsd->bhts', qh, kh) * jnp.exp(jnp.where(incl, diff, -jnp.inf))
    o = jnp.einsum('bhts,bhsv->bhtv', attn, v_new) + \
        jnp.einsum('bhtd,bhdv->bhtv', qh * jnp.exp(b)[..., None], state)
    b_last = b[..., -1]
    k_dec = kh * jnp.exp(b_last[..., None] - b)[..., None]
    new_state = state * jnp.exp(b_last)[..., None, None] + jnp.einsum('bhsd,bhsv->bhdv', k_dec, v_new)
    return new_state, o.transpose(0, 2, 1, 3).astype(v.dtype)


def gdn_mixer(h, w_in, conv_w, a_log, dt_bias, o_gain, w_out):
    B, L, _ = h.shape
    H = GDN_HEADS
    proj = h @ w_in
    qkv, gate, a, beta_logit = jnp.split(
        proj, [GDN_CONV_DIM, GDN_CONV_DIM + H * GDN_DV, GDN_CONV_DIM + H * GDN_DV + H], axis=-1)
    qkv = jax.nn.silu(causal_depthwise_conv(qkv, conv_w))
    q, k, v = jnp.split(qkv, [H * GDN_DK, 2 * H * GDN_DK], axis=-1)
    q = l2norm(q.reshape(B, L, H, GDN_DK)) * (GDN_DK ** -0.5)
    k = l2norm(k.reshape(B, L, H, GDN_DK))
    v = v.reshape(B, L, H, GDN_DV)
    beta = jax.nn.sigmoid(beta_logit)
    g = -jnp.exp(a_log.astype(jnp.float32)) * jax.nn.softplus((a + dt_bias).astype(jnp.float32))
    state0 = jnp.zeros((B, H, GDN_DK, GDN_DV), jnp.float32)
    o = chunked_scan(gdn_chunk, state0, (q, k, v, g, beta), GDN_CHUNK)
    o = rmsnorm(o, o_gain) * jax.nn.silu(gate.reshape(B, L, H, GDN_DV))
    return (o.reshape(B, L, H * GDN_DV) @ w_out).astype(h.dtype)


def swiglu(h, w_gate_up, w_down):
    gu = h @ w_gate_up
    gt, up = gu[..., :D_FF], gu[..., D_FF:]
    return ((jax.nn.silu(gt) * up) @ w_down).astype(h.dtype)


def _fwd_setup_inputs(seed: int = 0) -> dict:
    key = jax.random.key(seed)
    ks = jax.random.split(key, 24)

    def nrm(k, shape, fan_in):
        return jax.random.normal(k, shape, jnp.float32) * (fan_in ** -0.5)

    def gain(k, shape):
        return 1.0 + 0.02 * jax.random.normal(k, shape, jnp.float32)

    dt = jnp.exp(jax.random.uniform(ks[20], (N_GDN, GDN_HEADS), jnp.float32, np.log(1e-3), np.log(1e-1)))
    return {
        "x": jax.random.normal(ks[0], (BATCH, SEQ, D_MODEL), jnp.float32),
        "meta_tokens": jax.random.normal(ks[1], (N_META, D_MODEL), jnp.float32),
        "norm_mix": gain(ks[2], (DEPTH, D_MODEL)),
        "norm_ffn": gain(ks[3], (DEPTH, D_MODEL)),
        "w_gate_up": nrm(ks[4], (DEPTH, D_MODEL, 2 * D_FF), D_MODEL),
        "w_down": nrm(ks[5], (DEPTH, D_FF, D_MODEL), D_FF),
        "fox_w_in": nrm(ks[6], (N_FOX, D_MODEL, FOX_IN), D_MODEL),
        "fox_b_f": 2.0 + 0.5 * jax.random.normal(ks[7], (N_FOX, FOX_HEADS), jnp.float32),
        "fox_q_gain": gain(ks[8], (N_FOX, FOX_HEAD_DIM)),
        "fox_k_gain": gain(ks[9], (N_FOX, FOX_HEAD_DIM)),
        "fox_w_out": nrm(ks[10], (N_FOX, D_MODEL, D_MODEL), D_MODEL),
        "gla_w_in": nrm(ks[11], (N_GLA, D_MODEL, GLA_IN), D_MODEL),
        "gla_w_alpha2": nrm(ks[12], (N_GLA, GLA_GATE_RANK, GLA_QK), GLA_GATE_RANK),
        "gla_b_alpha": 0.1 * jax.random.normal(ks[13], (N_GLA, GLA_QK), jnp.float32),
        "gla_o_gain": gain(ks[14], (N_GLA, GLA_DV)),
        "gla_w_out": nrm(ks[15], (N_GLA, GLA_V, D_MODEL), GLA_V),
        "gdn_w_in": nrm(ks[16], (N_GDN, D_MODEL, GDN_IN), D_MODEL),
        "gdn_conv_w": nrm(ks[17], (N_GDN, GDN_CONV, 1, GDN_CONV_DIM), GDN_CONV),
        "gdn_a_log": jnp.log(jax.random.uniform(ks[18], (N_GDN, GDN_HEADS), jnp.float32, 1.0, 16.0)),
        "gdn_dt_bias": dt + jnp.log(-jnp.expm1(-dt)),
        "gdn_o_gain": gain(ks[19], (N_GDN, GDN_DV)),
        "gdn_w_out": nrm(ks[21], (N_GDN, GDN_HEADS * GDN_DV, D_MODEL), GDN_HEADS * GDN_DV),
    }


def _fwd_reference(x, meta_tokens, norm_mix, norm_ffn, w_gate_up, w_down,
              fox_w_in, fox_b_f, fox_q_gain, fox_k_gain, fox_w_out,
              gla_w_in, gla_w_alpha2, gla_b_alpha, gla_o_gain, gla_w_out,
              gdn_w_in, gdn_conv_w, gdn_a_log, gdn_dt_bias, gdn_o_gain, gdn_w_out):
    B = x.shape[0]
    meta = jnp.broadcast_to(meta_tokens[None].astype(x.dtype), (B, N_META, D_MODEL))
    h = jnp.concatenate([meta, x], axis=1)
    for i in range(DEPTH):
        kind, j = i % N_MIXERS, i // N_MIXERS
        y = rmsnorm(h, norm_mix[i])
        if kind == 0:
            mix = fox_mixer(y, fox_w_in[j], fox_b_f[j], fox_q_gain[j], fox_k_gain[j], fox_w_out[j])
        elif kind == 1:
            mix = gla_mixer(y, gla_w_in[j], gla_w_alpha2[j], gla_b_alpha[j], gla_o_gain[j], gla_w_out[j])
        else:
            mix = gdn_mixer(y, gdn_w_in[j], gdn_conv_w[j], gdn_a_log[j], gdn_dt_bias[j], gdn_o_gain[j], gdn_w_out[j])
        h = h + mix
        h = h + swiglu(rmsnorm(h, norm_ffn[i]), w_gate_up[i], w_down[i])
    return h[:, N_META:]


import jax as _jax
import jax.numpy as _jnp

TWIN_FORMAT = 'train_step'
FWD_PARAMS = ['x', 'meta_tokens', 'norm_mix', 'norm_ffn', 'w_gate_up', 'w_down', 'fox_w_in', 'fox_b_f', 'fox_q_gain', 'fox_k_gain', 'fox_w_out', 'gla_w_in', 'gla_w_alpha2', 'gla_b_alpha', 'gla_o_gain', 'gla_w_out', 'gdn_w_in', 'gdn_conv_w', 'gdn_a_log', 'gdn_dt_bias', 'gdn_o_gain', 'gdn_w_out']
TWIN_WEIGHTS = ['meta_tokens', 'norm_mix', 'norm_ffn', 'w_gate_up', 'w_down', 'fox_w_in', 'fox_b_f', 'fox_q_gain', 'fox_k_gain', 'fox_w_out', 'gla_w_in', 'gla_w_alpha2', 'gla_b_alpha', 'gla_o_gain', 'gla_w_out', 'gdn_w_in', 'gdn_conv_w', 'gdn_a_log', 'gdn_dt_bias', 'gdn_o_gain', 'gdn_w_out']
TWIN_DIFF_INPUT = 'x'
TWIN_INPUTS = ['x', 'meta_tokens', 'norm_mix', 'norm_ffn', 'w_gate_up', 'w_down', 'fox_w_in', 'fox_b_f', 'fox_q_gain', 'fox_k_gain', 'fox_w_out', 'gla_w_in', 'gla_w_alpha2', 'gla_b_alpha', 'gla_o_gain', 'gla_w_out', 'gdn_w_in', 'gdn_conv_w', 'gdn_a_log', 'gdn_dt_bias', 'gdn_o_gain', 'gdn_w_out', 'loss_target', 'm_meta_tokens', 'm_norm_mix', 'm_norm_ffn', 'm_w_gate_up', 'm_w_down', 'm_fox_w_in', 'm_fox_b_f', 'm_fox_q_gain', 'm_fox_k_gain', 'm_fox_w_out', 'm_gla_w_in', 'm_gla_w_alpha2', 'm_gla_b_alpha', 'm_gla_o_gain', 'm_gla_w_out', 'm_gdn_w_in', 'm_gdn_conv_w', 'm_gdn_a_log', 'm_gdn_dt_bias', 'm_gdn_o_gain', 'm_gdn_w_out', 'v_meta_tokens', 'v_norm_mix', 'v_norm_ffn', 'v_w_gate_up', 'v_w_down', 'v_fox_w_in', 'v_fox_b_f', 'v_fox_q_gain', 'v_fox_k_gain', 'v_fox_w_out', 'v_gla_w_in', 'v_gla_w_alpha2', 'v_gla_b_alpha', 'v_gla_o_gain', 'v_gla_w_out', 'v_gdn_w_in', 'v_gdn_conv_w', 'v_gdn_a_log', 'v_gdn_dt_bias', 'v_gdn_o_gain', 'v_gdn_w_out']
TWIN_OUTPUTS = ['loss', 'grad_x', 'grad_meta_tokens', 'grad_norm_mix', 'grad_norm_ffn', 'grad_w_gate_up', 'grad_w_down', 'grad_fox_w_in', 'grad_fox_b_f', 'grad_fox_q_gain', 'grad_fox_k_gain', 'grad_fox_w_out', 'grad_gla_w_in', 'grad_gla_w_alpha2', 'grad_gla_b_alpha', 'grad_gla_o_gain', 'grad_gla_w_out', 'grad_gdn_w_in', 'grad_gdn_conv_w', 'grad_gdn_a_log', 'grad_gdn_dt_bias', 'grad_gdn_o_gain', 'grad_gdn_w_out', 'delta_meta_tokens', 'delta_norm_mix', 'delta_norm_ffn', 'delta_w_gate_up', 'delta_w_down', 'delta_fox_w_in', 'delta_fox_b_f', 'delta_fox_q_gain', 'delta_fox_k_gain', 'delta_fox_w_out', 'delta_gla_w_in', 'delta_gla_w_alpha2', 'delta_gla_b_alpha', 'delta_gla_o_gain', 'delta_gla_w_out', 'delta_gdn_w_in', 'delta_gdn_conv_w', 'delta_gdn_a_log', 'delta_gdn_dt_bias', 'delta_gdn_o_gain', 'delta_gdn_w_out', 'new_m_meta_tokens', 'new_m_norm_mix', 'new_m_norm_ffn', 'new_m_w_gate_up', 'new_m_w_down', 'new_m_fox_w_in', 'new_m_fox_b_f', 'new_m_fox_q_gain', 'new_m_fox_k_gain', 'new_m_fox_w_out', 'new_m_gla_w_in', 'new_m_gla_w_alpha2', 'new_m_gla_b_alpha', 'new_m_gla_o_gain', 'new_m_gla_w_out', 'new_m_gdn_w_in', 'new_m_gdn_conv_w', 'new_m_gdn_a_log', 'new_m_gdn_dt_bias', 'new_m_gdn_o_gain', 'new_m_gdn_w_out', 'new_v_meta_tokens', 'new_v_norm_mix', 'new_v_norm_ffn', 'new_v_w_gate_up', 'new_v_w_down', 'new_v_fox_w_in', 'new_v_fox_b_f', 'new_v_fox_q_gain', 'new_v_fox_k_gain', 'new_v_fox_w_out', 'new_v_gla_w_in', 'new_v_gla_w_alpha2', 'new_v_gla_b_alpha', 'new_v_gla_o_gain', 'new_v_gla_w_out', 'new_v_gdn_w_in', 'new_v_gdn_conv_w', 'new_v_gdn_a_log', 'new_v_gdn_dt_bias', 'new_v_gdn_o_gain', 'new_v_gdn_w_out']
TWIN_LEAF_KINDS = {'loss': 'loss', 'grad_x': 'grad_x', 'grad_meta_tokens': 'grad_w', 'grad_norm_mix': 'grad_w', 'grad_norm_ffn': 'grad_w', 'grad_w_gate_up': 'grad_w', 'grad_w_down': 'grad_w', 'grad_fox_w_in': 'grad_w', 'grad_fox_b_f': 'grad_w', 'grad_fox_q_gain': 'grad_w', 'grad_fox_k_gain': 'grad_w', 'grad_fox_w_out': 'grad_w', 'grad_gla_w_in': 'grad_w', 'grad_gla_w_alpha2': 'grad_w', 'grad_gla_b_alpha': 'grad_w', 'grad_gla_o_gain': 'grad_w', 'grad_gla_w_out': 'grad_w', 'grad_gdn_w_in': 'grad_w', 'grad_gdn_conv_w': 'grad_w', 'grad_gdn_a_log': 'grad_w', 'grad_gdn_dt_bias': 'grad_w', 'grad_gdn_o_gain': 'grad_w', 'grad_gdn_w_out': 'grad_w', 'delta_meta_tokens': 'delta_w', 'delta_norm_mix': 'delta_w', 'delta_norm_ffn': 'delta_w', 'delta_w_gate_up': 'delta_w', 'delta_w_down': 'delta_w', 'delta_fox_w_in': 'delta_w', 'delta_fox_b_f': 'delta_w', 'delta_fox_q_gain': 'delta_w', 'delta_fox_k_gain': 'delta_w', 'delta_fox_w_out': 'delta_w', 'delta_gla_w_in': 'delta_w', 'delta_gla_w_alpha2': 'delta_w', 'delta_gla_b_alpha': 'delta_w', 'delta_gla_o_gain': 'delta_w', 'delta_gla_w_out': 'delta_w', 'delta_gdn_w_in': 'delta_w', 'delta_gdn_conv_w': 'delta_w', 'delta_gdn_a_log': 'delta_w', 'delta_gdn_dt_bias': 'delta_w', 'delta_gdn_o_gain': 'delta_w', 'delta_gdn_w_out': 'delta_w', 'new_m_meta_tokens': 'new_m', 'new_m_norm_mix': 'new_m', 'new_m_norm_ffn': 'new_m', 'new_m_w_gate_up': 'new_m', 'new_m_w_down': 'new_m', 'new_m_fox_w_in': 'new_m', 'new_m_fox_b_f': 'new_m', 'new_m_fox_q_gain': 'new_m', 'new_m_fox_k_gain': 'new_m', 'new_m_fox_w_out': 'new_m', 'new_m_gla_w_in': 'new_m', 'new_m_gla_w_alpha2': 'new_m', 'new_m_gla_b_alpha': 'new_m', 'new_m_gla_o_gain': 'new_m', 'new_m_gla_w_out': 'new_m', 'new_m_gdn_w_in': 'new_m', 'new_m_gdn_conv_w': 'new_m', 'new_m_gdn_a_log': 'new_m', 'new_m_gdn_dt_bias': 'new_m', 'new_m_gdn_o_gain': 'new_m', 'new_m_gdn_w_out': 'new_m', 'new_v_meta_tokens': 'new_v', 'new_v_norm_mix': 'new_v', 'new_v_norm_ffn': 'new_v', 'new_v_w_gate_up': 'new_v', 'new_v_w_down': 'new_v', 'new_v_fox_w_in': 'new_v', 'new_v_fox_b_f': 'new_v', 'new_v_fox_q_gain': 'new_v', 'new_v_fox_k_gain': 'new_v', 'new_v_fox_w_out': 'new_v', 'new_v_gla_w_in': 'new_v', 'new_v_gla_w_alpha2': 'new_v', 'new_v_gla_b_alpha': 'new_v', 'new_v_gla_o_gain': 'new_v', 'new_v_gla_w_out': 'new_v', 'new_v_gdn_w_in': 'new_v', 'new_v_gdn_conv_w': 'new_v', 'new_v_gdn_a_log': 'new_v', 'new_v_gdn_dt_bias': 'new_v', 'new_v_gdn_o_gain': 'new_v', 'new_v_gdn_w_out': 'new_v'}


def _forward(args):
    return _fwd_reference(*[args[k] for k in FWD_PARAMS])


def _output_shape():
    out = _jax.eval_shape(lambda: _forward(_fwd_setup_inputs(0)))
    return out.shape, out.dtype

N_MICROBATCH = 1
ADAM_LR = 0.001
ADAM_B1 = 0.9
ADAM_B2 = 0.999
ADAM_EPS = 1e-08
ADAM_WD = 0.01
ADAM_STEP = 10
PER_EXAMPLE_BATCH_AXIS = {'x': 0, 'loss_target': 0}
SHARED_INPUTS = []
_WEIGHT_DTYPES = {'meta_tokens': _jnp.float32, 'norm_mix': _jnp.float32, 'norm_ffn': _jnp.float32, 'w_gate_up': _jnp.float32, 'w_down': _jnp.float32, 'fox_w_in': _jnp.float32, 'fox_b_f': _jnp.float32, 'fox_q_gain': _jnp.float32, 'fox_k_gain': _jnp.float32, 'fox_w_out': _jnp.float32, 'gla_w_in': _jnp.float32, 'gla_w_alpha2': _jnp.float32, 'gla_b_alpha': _jnp.float32, 'gla_o_gain': _jnp.float32, 'gla_w_out': _jnp.float32, 'gdn_w_in': _jnp.float32, 'gdn_conv_w': _jnp.float32, 'gdn_a_log': _jnp.float32, 'gdn_dt_bias': _jnp.float32, 'gdn_o_gain': _jnp.float32, 'gdn_w_out': _jnp.float32}
MOMENT_SCALE = {'meta_tokens': 4.992306e-02, 'norm_mix': 9.112872e+00, 'norm_ffn': 2.444450e+01, 'w_gate_up': 3.082980e-01, 'w_down': 4.982642e-01, 'fox_w_in': 2.170367e-01, 'fox_b_f': 3.368728e+01, 'fox_q_gain': 8.653431e+00, 'fox_k_gain': 8.660228e+00, 'fox_w_out': 2.873296e-01, 'gla_w_in': 6.134576e-01, 'gla_w_alpha2': 8.338273e-02, 'gla_b_alpha': 3.194512e-01, 'gla_o_gain': 4.540937e+01, 'gla_w_out': 5.288811e-01, 'gdn_w_in': 3.819213e-01, 'gdn_conv_w': 4.710399e-01, 'gdn_a_log': 5.520108e+01, 'gdn_dt_bias': 5.277389e+01, 'gdn_o_gain': 8.683390e+01, 'gdn_w_out': 7.386375e-01}


def _to_microbatches(a, axis):
    t = _jnp.moveaxis(a, axis, 0)
    t = t.reshape((N_MICROBATCH, t.shape[0] // N_MICROBATCH) + t.shape[1:])
    return _jnp.moveaxis(t, 1, axis + 1)


def setup_inputs(seed: int = 0) -> dict:
    inp = _fwd_setup_inputs(seed)
    key = _jax.random.fold_in(_jax.random.key(seed), 7919)
    shape, _ = _output_shape()
    out = dict(inp)
    out["loss_target"] = _jax.random.normal(_jax.random.fold_in(key, 0), shape, _jnp.float32)
    for i, name in enumerate(TWIN_WEIGHTS):
        w = inp[name].astype(_jnp.float32)
        if MOMENT_SCALE is None:
            s = _jnp.sqrt(_jnp.mean(_jnp.square(w)) + 1e-30)
        else:
            s = MOMENT_SCALE[name]
        km, kv = _jax.random.split(_jax.random.fold_in(key, i + 1))
        out[name] = w
        out["m_" + name] = s * _jax.random.normal(km, w.shape, _jnp.float32)
        out["v_" + name] = (s * s) * _jax.random.uniform(kv, w.shape, _jnp.float32, 0.5, 1.5)
    if N_MICROBATCH > 1:
        for name, axis in PER_EXAMPLE_BATCH_AXIS.items():
            out[name] = _to_microbatches(out[name], axis)
    return {'x': out['x'], 'meta_tokens': out['meta_tokens'], 'norm_mix': out['norm_mix'], 'norm_ffn': out['norm_ffn'], 'w_gate_up': out['w_gate_up'], 'w_down': out['w_down'], 'fox_w_in': out['fox_w_in'], 'fox_b_f': out['fox_b_f'], 'fox_q_gain': out['fox_q_gain'], 'fox_k_gain': out['fox_k_gain'], 'fox_w_out': out['fox_w_out'], 'gla_w_in': out['gla_w_in'], 'gla_w_alpha2': out['gla_w_alpha2'], 'gla_b_alpha': out['gla_b_alpha'], 'gla_o_gain': out['gla_o_gain'], 'gla_w_out': out['gla_w_out'], 'gdn_w_in': out['gdn_w_in'], 'gdn_conv_w': out['gdn_conv_w'], 'gdn_a_log': out['gdn_a_log'], 'gdn_dt_bias': out['gdn_dt_bias'], 'gdn_o_gain': out['gdn_o_gain'], 'gdn_w_out': out['gdn_w_out'], 'loss_target': out['loss_target'], 'm_meta_tokens': out['m_meta_tokens'], 'm_norm_mix': out['m_norm_mix'], 'm_norm_ffn': out['m_norm_ffn'], 'm_w_gate_up': out['m_w_gate_up'], 'm_w_down': out['m_w_down'], 'm_fox_w_in': out['m_fox_w_in'], 'm_fox_b_f': out['m_fox_b_f'], 'm_fox_q_gain': out['m_fox_q_gain'], 'm_fox_k_gain': out['m_fox_k_gain'], 'm_fox_w_out': out['m_fox_w_out'], 'm_gla_w_in': out['m_gla_w_in'], 'm_gla_w_alpha2': out['m_gla_w_alpha2'], 'm_gla_b_alpha': out['m_gla_b_alpha'], 'm_gla_o_gain': out['m_gla_o_gain'], 'm_gla_w_out': out['m_gla_w_out'], 'm_gdn_w_in': out['m_gdn_w_in'], 'm_gdn_conv_w': out['m_gdn_conv_w'], 'm_gdn_a_log': out['m_gdn_a_log'], 'm_gdn_dt_bias': out['m_gdn_dt_bias'], 'm_gdn_o_gain': out['m_gdn_o_gain'], 'm_gdn_w_out': out['m_gdn_w_out'], 'v_meta_tokens': out['v_meta_tokens'], 'v_norm_mix': out['v_norm_mix'], 'v_norm_ffn': out['v_norm_ffn'], 'v_w_gate_up': out['v_w_gate_up'], 'v_w_down': out['v_w_down'], 'v_fox_w_in': out['v_fox_w_in'], 'v_fox_b_f': out['v_fox_b_f'], 'v_fox_q_gain': out['v_fox_q_gain'], 'v_fox_k_gain': out['v_fox_k_gain'], 'v_fox_w_out': out['v_fox_w_out'], 'v_gla_w_in': out['v_gla_w_in'], 'v_gla_w_alpha2': out['v_gla_w_alpha2'], 'v_gla_b_alpha': out['v_gla_b_alpha'], 'v_gla_o_gain': out['v_gla_o_gain'], 'v_gla_w_out': out['v_gla_w_out'], 'v_gdn_w_in': out['v_gdn_w_in'], 'v_gdn_conv_w': out['v_gdn_conv_w'], 'v_gdn_a_log': out['v_gdn_a_log'], 'v_gdn_dt_bias': out['v_gdn_dt_bias'], 'v_gdn_o_gain': out['v_gdn_o_gain'], 'v_gdn_w_out': out['v_gdn_w_out']}


def _loss(weights, diff, rest, loss_target):
    with _jax.named_scope("forward"):
        args = {**rest, TWIN_DIFF_INPUT: diff, **{k: w.astype(_WEIGHT_DTYPES[k]) for k, w in weights.items()}}
        y = _forward(args)
    with _jax.named_scope("loss_head"):
        err = _jnp.square(y.astype(_jnp.float32) - loss_target)
        return 0.5 * _jnp.sum(_jnp.mean(err, axis=-1)) if err.ndim else 0.5 * err


def _adamw(w, g, m, v):
    m = ADAM_B1 * m + (1.0 - ADAM_B1) * g
    v = ADAM_B2 * v + (1.0 - ADAM_B2) * _jnp.square(g)
    m_hat = m / (1.0 - ADAM_B1 ** ADAM_STEP)
    v_hat = v / (1.0 - ADAM_B2 ** ADAM_STEP)
    delta = -ADAM_LR * (m_hat / (_jnp.sqrt(v_hat) + ADAM_EPS) + ADAM_WD * w)
    return delta, m, v


def reference(x, meta_tokens, norm_mix, norm_ffn, w_gate_up, w_down, fox_w_in, fox_b_f, fox_q_gain, fox_k_gain, fox_w_out, gla_w_in, gla_w_alpha2, gla_b_alpha, gla_o_gain, gla_w_out, gdn_w_in, gdn_conv_w, gdn_a_log, gdn_dt_bias, gdn_o_gain, gdn_w_out, loss_target, m_meta_tokens, m_norm_mix, m_norm_ffn, m_w_gate_up, m_w_down, m_fox_w_in, m_fox_b_f, m_fox_q_gain, m_fox_k_gain, m_fox_w_out, m_gla_w_in, m_gla_w_alpha2, m_gla_b_alpha, m_gla_o_gain, m_gla_w_out, m_gdn_w_in, m_gdn_conv_w, m_gdn_a_log, m_gdn_dt_bias, m_gdn_o_gain, m_gdn_w_out, v_meta_tokens, v_norm_mix, v_norm_ffn, v_w_gate_up, v_w_down, v_fox_w_in, v_fox_b_f, v_fox_q_gain, v_fox_k_gain, v_fox_w_out, v_gla_w_in, v_gla_w_alpha2, v_gla_b_alpha, v_gla_o_gain, v_gla_w_out, v_gdn_w_in, v_gdn_conv_w, v_gdn_a_log, v_gdn_dt_bias, v_gdn_o_gain, v_gdn_w_out):
    given = dict(x=x, meta_tokens=meta_tokens, norm_mix=norm_mix, norm_ffn=norm_ffn, w_gate_up=w_gate_up, w_down=w_down, fox_w_in=fox_w_in, fox_b_f=fox_b_f, fox_q_gain=fox_q_gain, fox_k_gain=fox_k_gain, fox_w_out=fox_w_out, gla_w_in=gla_w_in, gla_w_alpha2=gla_w_alpha2, gla_b_alpha=gla_b_alpha, gla_o_gain=gla_o_gain, gla_w_out=gla_w_out, gdn_w_in=gdn_w_in, gdn_conv_w=gdn_conv_w, gdn_a_log=gdn_a_log, gdn_dt_bias=gdn_dt_bias, gdn_o_gain=gdn_o_gain, gdn_w_out=gdn_w_out, loss_target=loss_target, m_meta_tokens=m_meta_tokens, m_norm_mix=m_norm_mix, m_norm_ffn=m_norm_ffn, m_w_gate_up=m_w_gate_up, m_w_down=m_w_down, m_fox_w_in=m_fox_w_in, m_fox_b_f=m_fox_b_f, m_fox_q_gain=m_fox_q_gain, m_fox_k_gain=m_fox_k_gain, m_fox_w_out=m_fox_w_out, m_gla_w_in=m_gla_w_in, m_gla_w_alpha2=m_gla_w_alpha2, m_gla_b_alpha=m_gla_b_alpha, m_gla_o_gain=m_gla_o_gain, m_gla_w_out=m_gla_w_out, m_gdn_w_in=m_gdn_w_in, m_gdn_conv_w=m_gdn_conv_w, m_gdn_a_log=m_gdn_a_log, m_gdn_dt_bias=m_gdn_dt_bias, m_gdn_o_gain=m_gdn_o_gain, m_gdn_w_out=m_gdn_w_out, v_meta_tokens=v_meta_tokens, v_norm_mix=v_norm_mix, v_norm_ffn=v_norm_ffn, v_w_gate_up=v_w_gate_up, v_w_down=v_w_down, v_fox_w_in=v_fox_w_in, v_fox_b_f=v_fox_b_f, v_fox_q_gain=v_fox_q_gain, v_fox_k_gain=v_fox_k_gain, v_fox_w_out=v_fox_w_out, v_gla_w_in=v_gla_w_in, v_gla_w_alpha2=v_gla_w_alpha2, v_gla_b_alpha=v_gla_b_alpha, v_gla_o_gain=v_gla_o_gain, v_gla_w_out=v_gla_w_out, v_gdn_w_in=v_gdn_w_in, v_gdn_conv_w=v_gdn_conv_w, v_gdn_a_log=v_gdn_a_log, v_gdn_dt_bias=v_gdn_dt_bias, v_gdn_o_gain=v_gdn_o_gain, v_gdn_w_out=v_gdn_w_out)
    weights = {n: given[n] for n in TWIN_WEIGHTS}
    shared = {n: given[n] for n in SHARED_INPUTS}
    per_example = {n: given[n] for n in ['x']}
    grad_fn = _jax.value_and_grad(_loss, argnums=(0, 1))

    def one_microbatch(ex, loss_target):
        ex = dict(ex)
        diff = ex.pop(TWIN_DIFF_INPUT)
        return grad_fn(weights, diff, {**shared, **ex}, loss_target)

    if N_MICROBATCH == 1:
        loss, (grad_w, grad_x) = one_microbatch(per_example, given["loss_target"])
    else:
        def body(carry, xs):
            loss_sum, grad_sum = carry
            l_k, (gw_k, gx_k) = one_microbatch(xs[0], xs[1])
            with _jax.named_scope("update"):
                return (loss_sum + l_k, _jax.tree.map(_jnp.add, grad_sum, gw_k)), gx_k

        init = (_jnp.zeros((), _jnp.float32), _jax.tree.map(_jnp.zeros_like, weights))
        (loss, grad_w), grad_x = _jax.lax.scan(body, init, (per_example, given["loss_target"]))
    with _jax.named_scope("update"):
        delta_w, new_m, new_v = {}, {}, {}
        for n in TWIN_WEIGHTS:
            delta_w[n], new_m[n], new_v[n] = _adamw(weights[n], grad_w[n], given["m_" + n], given["v_" + n])
    return (loss, grad_x, *[grad_w[n] for n in TWIN_WEIGHTS], *[delta_w[n] for n in TWIN_WEIGHTS],
            *[new_m[n] for n in TWIN_WEIGHTS], *[new_v[n] for n in TWIN_WEIGHTS])
```

```python
import functools

import jax
import jax.numpy as jnp
from jax import lax
from jax.experimental import pallas as pl
from jax.experimental.pallas import tpu as pltpu

F32, BF16 = jnp.float32, jnp.bfloat16
HIGHEST = lax.Precision.HIGHEST

D_MODEL = 1024
SEQ = 4096
N_META = 16
DEPTH = 4
NORM_EPS = 1e-6
LANE = 128
PAD_ROWS = 128
FIRST_ROW = PAD_ROWS - N_META
ROWS = PAD_ROWS + SEQ
ROW_TILE = 128
CHUNK = 64
NEG = -1e30
N_DEV = 8
VMEM_LIMIT = 56 * 1024 * 1024
VMEM_BLOCK_BUDGET = 36 * 1024 * 1024

FOX_HEADS, FOX_DH = 16, 64
FOX_IN, FOX_IN_PAD = 4 * D_MODEL + FOX_HEADS, 4 * D_MODEL + LANE
GLA_HEADS, GLA_DK, GLA_DV, GLA_RANK = 4, 128, 256, 16
GLA_QK, GLA_V = GLA_HEADS * GLA_DK, GLA_HEADS * GLA_DV
GLA_IN, GLA_IN_PAD = 2 * GLA_QK + 2 * GLA_V + GLA_RANK, 2 * GLA_QK + 2 * GLA_V + LANE
GDN_HEADS, GDN_DK, GDN_DV, GDN_CONV = 8, 128, 128, 4
GDN_CONV_DIM = 3 * GDN_HEADS * GDN_DK
GDN_IN, GDN_IN_PAD = GDN_CONV_DIM + GDN_HEADS * GDN_DV + 2 * GDN_HEADS, GDN_CONV_DIM + GDN_HEADS * GDN_DV + LANE
D_FF = 2816

ADAM_LR, ADAM_B1, ADAM_B2, ADAM_EPS, ADAM_WD, ADAM_STEP = 0.001, 0.9, 0.999, 1e-08, 0.01, 10

PACK_COLS = 1024
PACK_ROW_TILE = 256

SHARDED = (
    ("meta_tokens", (16, 128), 1),
    ("w_gate_up", (4, 1024, 704), 2),
    ("w_down", (4, 352, 1024), 1),
    ("fox_w_in", (2, 1024, 514), 2),
    ("fox_w_out", (2, 128, 1024), 1),
    ("gla_w_in", (1, 1024, 386), 2),
    ("gla_w_alpha2", (1, 16, 64), 2),
    ("gla_w_out", (1, 128, 1024), 1),
    ("gdn_w_in", (1, 1024, 514), 2),
    ("gdn_conv_w", (1, 4, 1, 384), 3),
    ("gdn_w_out", (1, 128, 1024), 1),
)
REPLICATED = (
    ("norm_mix", (4, 1024)),
    ("norm_ffn", (4, 1024)),
    ("fox_b_f", (2, 16)),
    ("fox_q_gain", (2, 64)),
    ("fox_k_gain", (2, 64)),
    ("gla_b_alpha", (1, 512)),
    ("gla_o_gain", (1, 256)),
    ("gdn_a_log", (1, 8)),
    ("gdn_dt_bias", (1, 8)),
    ("gdn_o_gain", (1, 128)),
)
WEIGHT_ORDER = (
    "meta_tokens", "norm_mix", "norm_ffn", "w_gate_up", "w_down", "fox_w_in", "fox_b_f", "fox_q_gain",
    "fox_k_gain", "fox_w_out", "gla_w_in", "gla_w_alpha2", "gla_b_alpha", "gla_o_gain", "gla_w_out",
    "gdn_w_in", "gdn_conv_w", "gdn_a_log", "gdn_dt_bias", "gdn_o_gain", "gdn_w_out",
)

_DN = {
    "nn": (((1,), (0,)), ((), ())),
    "nt": (((1,), (1,)), ((), ())),
    "tn": (((0,), (0,)), ((), ())),
}


def _dot(a, b, kind):
    return lax.dot_general(a.astype(BF16), b.astype(BF16), _DN[kind], preferred_element_type=F32)


@functools.partial(jax.custom_vjp, nondiff_argnums=(2,))
def _dot_ad(a, b, kind):
    return _dot(a, b, kind)


def _dot_ad_fwd(a, b, kind):
    return _dot(a, b, kind), (a, b)


def _dot_ad_bwd(kind, res, ct):
    a, b = res
    if kind == "nn":
        return _dot(ct, b, "nt"), _dot(a, ct, "tn")
    if kind == "nt":
        return _dot(ct, b, "nn"), _dot(ct, a, "tn")
    return _dot(b, ct, "nt"), _dot(a, ct, "nn")


_dot_ad.defvjp(_dot_ad_fwd, _dot_ad_bwd)


def _dot_hi(a, b, kind):
    return lax.dot_general(a, b, _DN[kind], precision=HIGHEST, preferred_element_type=F32)


def _iota(shape, dim):
    return lax.broadcasted_iota(jnp.int32, shape, dim)


def _row_mask(i, tm):
    return ((i * tm + _iota((tm, 1), 0)) >= FIRST_ROW).astype(F32)


def _log_sigmoid(z):
    return jnp.minimum(z, 0.0) - jnp.log(1.0 + jnp.exp(-jnp.abs(z)))


def _softplus(z):
    return jnp.maximum(z, 0.0) + jnp.log(1.0 + jnp.exp(-jnp.abs(z)))


def _silu(z):
    return z * jax.nn.sigmoid(z)


def _lane_col(x, lane):
    return jnp.sum(jnp.where(_iota(x.shape, 1) == lane, x, 0.0), axis=1, keepdims=True)


def _last_row(x):
    return jnp.sum(jnp.where(_iota(x.shape, 0) == x.shape[0] - 1, x, 0.0), axis=0, keepdims=True)


def _params(sem, limit=VMEM_LIMIT):
    return pltpu.CompilerParams(dimension_semantics=sem, vmem_limit_bytes=limit)


def _pick(n, options):
    for t in options:
        if n % t == 0:
            return t
    return n


def _matmul(a, b, kind, out_dtype, name, add=None):
    if kind == "nn":
        (m, k), n = a.shape, b.shape[1]
    elif kind == "nt":
        (m, k), n = a.shape, b.shape[0]
    else:
        (k, m), n = a.shape, b.shape[1]
    tn = _pick(n, (512, 384, 640, 256, 128))
    tm_options = (1408, 384, 128) if kind != "tn" else (256, 128)
    tm = None
    for t in tm_options:
        if m % t:
            continue
        need = 2 * (t * k * a.dtype.itemsize + tn * k * b.dtype.itemsize + t * tn * 8)
        if need <= VMEM_BLOCK_BUDGET:
            tm = t
            break
    assert tm is not None, (name, a.shape, b.shape)

    def body(*refs):
        a_ref, b_ref = refs[0], refs[1]
        o_ref = refs[-1]
        acc = _dot(a_ref[...], b_ref[...], kind)
        if add is not None:
            acc = acc + refs[2][...]
        o_ref[...] = acc.astype(o_ref.dtype)

    if kind == "tn":
        a_spec = pl.BlockSpec((k, tm), lambda i, j: (0, i))
    else:
        a_spec = pl.BlockSpec((tm, k), lambda i, j: (i, 0))
    if kind == "nt":
        b_spec = pl.BlockSpec((tn, k), lambda i, j: (j, 0))
    else:
        b_spec = pl.BlockSpec((k, tn), lambda i, j: (0, j))
    o_spec = pl.BlockSpec((tm, tn), lambda i, j: (i, j))
    operands, in_specs = [a, b], [a_spec, b_spec]
    if add is not None:
        operands.append(add)
        in_specs.append(o_spec)
    return pl.pallas_call(
        body, grid=(m // tm, n // tn), in_specs=in_specs, out_specs=o_spec,
        out_shape=jax.ShapeDtypeStruct((m, n), out_dtype), name=name,
        compiler_params=_params(("parallel", "parallel")),
    )(*operands)


def _x_spec(tm, width, cb0, moves):
    return pl.BlockSpec((tm, width), lambda i, j: (i, cb0 + j * moves))


def _full_spec(arr):
    nd = arr.ndim
    return pl.BlockSpec(arr.shape, lambda i, j: (0,) * nd)


def _row_fwd(fn, xs, ps, cs, outs, name, tm=ROW_TILE, ncol=1, rows=None):
    rows = ROWS if rows is None else rows
    nx, npar, nc = len(xs), len(ps), len(cs)

    def body(*refs):
        i = pl.program_id(0)
        xv = [r[...].astype(F32) for r in refs[:nx]]
        pv = [r[...] for r in refs[nx:nx + npar + nc]]
        res = fn(i, *xv, *pv)
        for r, val in zip(refs[nx + npar + nc:], res):
            r[...] = val.astype(r.dtype)

    in_specs = [_x_spec(tm, w, cb0, mv) for (_, w, cb0, mv) in xs]
    in_specs += [_full_spec(p) for p in list(ps) + list(cs)]
    out_specs = [_x_spec(tm, w, 0, 1) for (w, _, _) in outs]
    out_shape = [jax.ShapeDtypeStruct((rows, tot), dt) for (_, tot, dt) in outs]
    return pl.pallas_call(
        body, grid=(rows // tm, ncol), in_specs=in_specs, out_specs=out_specs, out_shape=out_shape,
        name=name, compiler_params=_params(("parallel", "parallel")),
    )(*[x[0] for x in xs], *ps, *cs)


def _row_bwd(fn, xs, ps, cs, cts, dxs, name, adds=None, tm=ROW_TILE, ncol=1, rows=None):
    rows = ROWS if rows is None else rows
    nx, npar, nc, nct = len(xs), len(ps), len(cs), len(cts)
    adds = [None] * nx if adds is None else adds
    add_idx = [k for k in range(nx) if adds[k] is not None]

    def body(*refs):
        i, j = pl.program_id(0), pl.program_id(1)
        pos = 0
        x_refs = refs[pos:pos + nx]; pos += nx
        p_refs = refs[pos:pos + npar]; pos += npar
        c_refs = refs[pos:pos + nc]; pos += nc
        ct_refs = refs[pos:pos + nct]; pos += nct
        add_refs = refs[pos:pos + len(add_idx)]; pos += len(add_idx)
        dx_refs = refs[pos:pos + nx]; pos += nx
        dp_refs = refs[pos:pos + npar]
        xv = [r[...].astype(F32) for r in x_refs]
        pv = [r[...] for r in p_refs]
        cv = [r[...] for r in c_refs]
        _, vjp = jax.vjp(lambda *args: fn(i, *args, *cv), *xv, *pv)
        grads = vjp(tuple(r[...].astype(F32) for r in ct_refs))
        for k in range(nx):
            val = grads[k]
            if adds[k] is not None:
                val = val + add_refs[add_idx.index(k)][...]
            dx_refs[k][...] = val.astype(dx_refs[k].dtype)
        if npar:
            @pl.when((i == 0) & (j == 0))
            def _():
                for r in dp_refs:
                    r[...] = jnp.zeros(r.shape, r.dtype)

            for k in range(npar):
                dp_refs[k][...] += grads[nx + k]

    in_specs = [_x_spec(tm, w, cb0, mv) for (_, w, cb0, mv) in xs]
    in_specs += [_full_spec(p) for p in list(ps) + list(cs)]
    in_specs += [_x_spec(tm, w, cb0, mv) for (_, w, cb0, mv) in cts]
    in_specs += [_x_spec(tm, xs[k][1], 0, 1) for k in add_idx]
    out_specs = [_x_spec(tm, xs[k][1], 0, 1) for k in range(nx)] + [_full_spec(p) for p in ps]
    out_shape = [jax.ShapeDtypeStruct((rows, tot), dt) for (tot, dt) in dxs]
    out_shape += [jax.ShapeDtypeStruct(p.shape, F32) for p in ps]
    sem = ("arbitrary", "arbitrary") if npar else ("parallel", "parallel")
    return pl.pallas_call(
        body, grid=(rows // tm, ncol), in_specs=in_specs, out_specs=out_specs, out_shape=out_shape,
        name=name, compiler_params=_params(sem),
    )(*[x[0] for x in xs], *ps, *cs, *[c[0] for c in cts], *[adds[k] for k in add_idx])


def _rms_fn(i, h, g):
    return (h * lax.rsqrt(jnp.mean(h * h, axis=1, keepdims=True) + NORM_EPS) * g,)


def _swiglu_fn(i, gt, up):
    return (_silu(gt) * up,)


def _fox_pre_fn(i, q, k, f, qg, kg, bf, e_mean, e_expand):
    tm = q.shape[0]

    def headnorm(x, gain, scale):
        msq = _dot_hi(x * x, e_mean, "nn")
        r = _dot_hi(lax.rsqrt(msq + NORM_EPS), e_expand, "nn")
        return x * r * gain * scale

    qn = headnorm(q, qg, FOX_DH ** -0.5)
    kn = headnorm(k, kg, 1.0)
    keep = (_iota((tm, LANE), 1) < FOX_HEADS).astype(F32) * _row_mask(i, tm)
    return qn, kn, _log_sigmoid(f + bf) * keep


def _fox_post_fn(i, o, gate):
    return (o * jax.nn.sigmoid(gate) * _row_mask(i, o.shape[0]),)


def _gla_pre_fn(i, alr, wa, ba):
    z = _dot_ad(alr, wa, "nn") + ba
    return (_log_sigmoid(z) * (1.0 / 16.0) * _row_mask(i, alr.shape[0]),)


def _headwise_norm_gate(o, gate, gain, dv, mask):
    pieces = []
    for h in range(o.shape[1] // dv):
        oh = o[:, h * dv:(h + 1) * dv]
        pieces.append(oh * lax.rsqrt(jnp.mean(oh * oh, axis=1, keepdims=True) + NORM_EPS) * gain)
    return jnp.concatenate(pieces, axis=1) * _silu(gate) * mask


def _gla_post_fn(i, o, r, gain):
    return (_headwise_norm_gate(o, r, gain, GLA_DV, _row_mask(i, o.shape[0])),)


def _gdn_post_fn(i, o, gate, gain):
    return (_headwise_norm_gate(o, gate, gain, GDN_DV, _row_mask(i, o.shape[0])),)


def _gdn_gates_fn(i, sm, alog, dtb):
    tm = sm.shape[0]
    lane = _iota((tm, LANE), 1)
    g = -jnp.exp(alog) * _softplus(sm + dtb)
    beta = jax.nn.sigmoid(sm)
    out = jnp.where(lane < GDN_HEADS, g, jnp.where(lane < 2 * GDN_HEADS, beta, 0.0))
    return (out * _row_mask(i, tm),)


def _gdn_conv_fn(kind):
    def fn(i, prev, cur, w):
        tm = cur.shape[0]
        xw = jnp.concatenate([prev * jnp.where(i > 0, 1.0, 0.0), cur], axis=0)
        t = _iota((tm, 2 * tm), 0)
        s = _iota((tm, 2 * tm), 1)
        y = jnp.zeros(cur.shape, F32)
        for j in range(GDN_CONV):
            sel = (s == t + (tm - (GDN_CONV - 1) + j)).astype(F32)
            y = y + _dot_hi(sel, xw, "nn") * jnp.sum(jnp.where(_iota(w.shape, 0) == j, w, 0.0), axis=0, keepdims=True)
        a = _silu(y * _row_mask(i, tm))
        if kind == "v":
            return (a,)
        scale = GDN_DK ** -0.5 if kind == "q" else 1.0
        pieces = []
        for h in range(GDN_HEADS):
            ah = a[:, h * GDN_DK:(h + 1) * GDN_DK]
            pieces.append(ah * lax.rsqrt(jnp.sum(ah * ah, axis=1, keepdims=True) + NORM_EPS) * scale)
        return (jnp.concatenate(pieces, axis=1),)

    return fn


def _seq_cumsum(x, reverse, name):
    rows, width = x.shape
    nt = rows // ROW_TILE

    def body(x_ref, o_ref, carry):
        @pl.when(pl.program_id(0) == 0)
        def _():
            carry[...] = jnp.zeros(carry.shape, F32)

        r = _iota((ROW_TILE, ROW_TILE), 0)
        c = _iota((ROW_TILE, ROW_TILE), 1)
        tri = ((c >= r) if reverse else (c <= r)).astype(F32)
        acc = _dot_hi(tri, x_ref[...], "nn") + carry[...]
        o_ref[...] = acc
        edge = 0 if reverse else ROW_TILE - 1
        carry[...] = jnp.sum(jnp.where(_iota(acc.shape, 0) == edge, acc, 0.0), axis=0, keepdims=True)

    idx = (lambda i: (nt - 1 - i, 0)) if reverse else (lambda i: (i, 0))
    return pl.pallas_call(
        body, grid=(nt,), in_specs=[pl.BlockSpec((ROW_TILE, width), idx)],
        out_specs=pl.BlockSpec((ROW_TILE, width), idx), out_shape=jax.ShapeDtypeStruct(x.shape, F32),
        scratch_shapes=[pltpu.VMEM((1, width), F32)], name=name, compiler_params=_params(("arbitrary",)),
    )(x)


FOX_PAIRS = FOX_HEADS // 2
FOX_V_BLOCK = 2 * D_MODEL // LANE


def _pair_cols(x):
    return _lane_col(x, 0), _lane_col(x, FOX_DH)


def _fox_scores(qa, qb, kj, ca, cb, cra, crb, mask):
    sa = jnp.where(mask, _dot(qa, kj, "nt") + (ca - cra), NEG)
    sb = jnp.where(mask, _dot(qb, kj, "nt") + (cb - crb), NEG)
    return sa, sb


def _fox_mask(i, j):
    rows = i * ROW_TILE + _iota((ROW_TILE, ROW_TILE), 0)
    cols = j * ROW_TILE + _iota((ROW_TILE, ROW_TILE), 1)
    return (cols <= rows) & (cols >= FIRST_ROW)


def _fox_attn_fwd(qn, kn, proj, c2, crow):
    nt = ROWS // ROW_TILE

    def body(q_ref, k_ref, v_ref, c2_ref, crow_ref, o_ref, lse_ref):
        i = pl.program_id(1)
        q = q_ref[...]
        is_a = _iota((ROW_TILE, LANE), 1) < FOX_DH
        qa, qb = jnp.where(is_a, q, 0), jnp.where(is_a, 0, q)
        ca, cb = _pair_cols(c2_ref[0])

        def step(j, carry):
            ma, la, acca, mb, lb, accb = carry
            off = pl.multiple_of(j * ROW_TILE, ROW_TILE)
            kj = k_ref[pl.ds(off, ROW_TILE), :]
            vj = v_ref[pl.ds(off, ROW_TILE), :].astype(BF16)
            sa, sb = _fox_scores(qa, qb, kj, ca, cb, crow_ref[0, 0:1, pl.ds(off, ROW_TILE)],
                                 crow_ref[0, 1:2, pl.ds(off, ROW_TILE)], _fox_mask(i, j))

            def online(s, m, l, acc):
                m_new = jnp.maximum(m, jnp.max(s, axis=1, keepdims=True))
                p = jnp.exp(s - m_new)
                alpha = jnp.exp(m - m_new)
                return m_new, alpha * l + jnp.sum(p, axis=1, keepdims=True), alpha * acc + _dot(p, vj, "nn")

            return online(sa, ma, la, acca) + online(sb, mb, lb, accb)

        m0 = jnp.full((ROW_TILE, 1), NEG, F32)
        l0 = jnp.zeros((ROW_TILE, 1), F32)
        a0 = jnp.zeros((ROW_TILE, LANE), F32)
        ma, la, acca, mb, lb, accb = lax.fori_loop(0, i + 1, step, (m0, l0, a0, m0, l0, a0))
        o_ref[...] = jnp.where(is_a, acca / la, accb / lb)
        lse_ref[0] = jnp.where(is_a, ma + jnp.log(la), mb + jnp.log(lb))

    return pl.pallas_call(
        body, grid=(FOX_PAIRS, nt),
        in_specs=[
            pl.BlockSpec((ROW_TILE, LANE), lambda g, i: (i, g)),
            pl.BlockSpec((ROWS, LANE), lambda g, i: (0, g)),
            pl.BlockSpec((ROWS, LANE), lambda g, i: (0, FOX_V_BLOCK + g)),
            pl.BlockSpec((1, ROW_TILE, LANE), lambda g, i: (g, i, 0)),
            pl.BlockSpec((1, 2, ROWS), lambda g, i: (g, 0, 0)),
        ],
        out_specs=[
            pl.BlockSpec((ROW_TILE, LANE), lambda g, i: (i, g)),
            pl.BlockSpec((1, ROW_TILE, LANE), lambda g, i: (g, i, 0)),
        ],
        out_shape=[jax.ShapeDtypeStruct((ROWS, D_MODEL), F32), jax.ShapeDtypeStruct((FOX_PAIRS, ROWS, LANE), F32)],
        name="fox_attn_fwd", compiler_params=_params(("parallel", "arbitrary")),
    )(qn, kn, proj, c2, crow)


def _fox_attn_dq(qn, kn, proj, c2, crow, lse2, o, do, blockdiag):
    nt = ROWS // ROW_TILE

    def body(q_ref, k_ref, v_ref, c2_ref, crow_ref, lse_ref, o_ref, do_ref, bd_ref, dq_ref, delta_ref, dcq_ref):
        i = pl.program_id(1)
        q = q_ref[...]
        is_a = _iota((ROW_TILE, LANE), 1) < FOX_DH
        qa, qb = jnp.where(is_a, q, 0), jnp.where(is_a, 0, q)
        ca, cb = _pair_cols(c2_ref[0])
        lsa, lsb = _pair_cols(lse_ref[0])
        do_t = do_ref[...]
        delta2 = _dot_hi(do_t * o_ref[...], bd_ref[...], "nn")
        delta_ref[0] = delta2
        dla, dlb = _pair_cols(delta2)
        doa, dob = jnp.where(is_a, do_t, 0.0).astype(BF16), jnp.where(is_a, 0.0, do_t).astype(BF16)

        def step(j, carry):
            dqa, dqb, rsa, rsb = carry
            off = pl.multiple_of(j * ROW_TILE, ROW_TILE)
            kj = k_ref[pl.ds(off, ROW_TILE), :]
            vj = v_ref[pl.ds(off, ROW_TILE), :].astype(BF16)
            mask = _fox_mask(i, j)
            sa, sb = _fox_scores(qa, qb, kj, ca, cb, crow_ref[0, 0:1, pl.ds(off, ROW_TILE)],
                                 crow_ref[0, 1:2, pl.ds(off, ROW_TILE)], mask)
            pa = jnp.where(mask, jnp.exp(sa - lsa), 0.0)
            pb = jnp.where(mask, jnp.exp(sb - lsb), 0.0)
            dsa = pa * (_dot(doa, vj, "nt") - dla)
            dsb = pb * (_dot(dob, vj, "nt") - dlb)
            return (dqa + _dot(dsa, kj, "nn"), dqb + _dot(dsb, kj, "nn"),
                    rsa + jnp.sum(dsa, axis=1, keepdims=True), rsb + jnp.sum(dsb, axis=1, keepdims=True))

        z = jnp.zeros((ROW_TILE, LANE), F32)
        zc = jnp.zeros((ROW_TILE, 1), F32)
        dqa, dqb, rsa, rsb = lax.fori_loop(0, i + 1, step, (z, z, zc, zc))
        dq_ref[...] = jnp.where(is_a, dqa, dqb)
        dcq_ref[0] = jnp.where(is_a, rsa, rsb)

    tile = pl.BlockSpec((ROW_TILE, LANE), lambda g, i: (i, g))
    pair_tile = pl.BlockSpec((1, ROW_TILE, LANE), lambda g, i: (g, i, 0))
    return pl.pallas_call(
        body, grid=(FOX_PAIRS, nt),
        in_specs=[
            tile,
            pl.BlockSpec((ROWS, LANE), lambda g, i: (0, g)),
            pl.BlockSpec((ROWS, LANE), lambda g, i: (0, FOX_V_BLOCK + g)),
            pair_tile,
            pl.BlockSpec((1, 2, ROWS), lambda g, i: (g, 0, 0)),
            pair_tile, tile, tile,
            pl.BlockSpec((LANE, LANE), lambda g, i: (0, 0)),
        ],
        out_specs=[tile, pair_tile, pair_tile],
        out_shape=[jax.ShapeDtypeStruct((ROWS, D_MODEL), F32), jax.ShapeDtypeStruct((FOX_PAIRS, ROWS, LANE), F32),
                   jax.ShapeDtypeStruct((FOX_PAIRS, ROWS, LANE), F32)],
        name="fox_attn_dq", compiler_params=_params(("parallel", "arbitrary")),
    )(qn, kn, proj, c2, crow, lse2, o, do, blockdiag)


def _fox_attn_dkv(qn, kn, proj, c2, crow, lse2, delta2, do):
    nt = ROWS // ROW_TILE

    def body(q_ref, k_ref, v_ref, c2_ref, crow_ref, lse_ref, delta_ref, do_ref, dk_ref, dv_ref, dc_ref):
        j = pl.program_id(1)
        kj = k_ref[...]
        vj = v_ref[...].astype(BF16)
        cra, crb = crow_ref[0, 0:1, :], crow_ref[0, 1:2, :]
        is_a = _iota((ROW_TILE, LANE), 1) < FOX_DH

        def step(i, carry):
            dk, dv, dca, dcb = carry
            off = pl.multiple_of(i * ROW_TILE, ROW_TILE)
            q = q_ref[pl.ds(off, ROW_TILE), :]
            qa, qb = jnp.where(is_a, q, 0), jnp.where(is_a, 0, q)
            do_t = do_ref[pl.ds(off, ROW_TILE), :]
            doa, dob = jnp.where(is_a, do_t, 0.0).astype(BF16), jnp.where(is_a, 0.0, do_t).astype(BF16)
            ca, cb = _pair_cols(c2_ref[0, pl.ds(off, ROW_TILE), :])
            lsa, lsb = _pair_cols(lse_ref[0, pl.ds(off, ROW_TILE), :])
            dla, dlb = _pair_cols(delta_ref[0, pl.ds(off, ROW_TILE), :])
            mask = _fox_mask(i, j)
            sa, sb = _fox_scores(qa, qb, kj, ca, cb, cra, crb, mask)
            pa = jnp.where(mask, jnp.exp(sa - lsa), 0.0)
            pb = jnp.where(mask, jnp.exp(sb - lsb), 0.0)
            dsa = pa * (_dot(doa, vj, "nt") - dla)
            dsb = pb * (_dot(dob, vj, "nt") - dlb)
            dv = dv + _dot(pa, doa, "tn") + _dot(pb, dob, "tn")
            dk = dk + _dot(dsa, qa, "tn") + _dot(dsb, qb, "tn")
            dca = dca - jnp.sum(dsa, axis=0, keepdims=True)
            dcb = dcb - jnp.sum(dsb, axis=0, keepdims=True)
            return dk, dv, dca, dcb

        z = jnp.zeros((ROW_TILE, LANE), F32)
        zr = jnp.zeros((1, ROW_TILE), F32)
        dk, dv, dca, dcb = lax.fori_loop(j, nt, step, (z, z, zr, zr))
        dk_ref[...] = dk
        dv_ref[...] = dv.astype(dv_ref.dtype)
        dc_ref[0, 0:1, :] = dca
        dc_ref[0, 1:2, :] = dcb

    whole = pl.BlockSpec((ROWS, LANE), lambda g, j: (0, g))
    pair_whole = pl.BlockSpec((1, ROWS, LANE), lambda g, j: (g, 0, 0))
    tile = pl.BlockSpec((ROW_TILE, LANE), lambda g, j: (j, g))
    return pl.pallas_call(
        body, grid=(FOX_PAIRS, nt),
        in_specs=[
            whole, tile,
            pl.BlockSpec((ROW_TILE, LANE), lambda g, j: (j, FOX_V_BLOCK + g)),
            pair_whole,
            pl.BlockSpec((1, 2, ROW_TILE), lambda g, j: (g, 0, j)),
            pair_whole, pair_whole, whole,
        ],
        out_specs=[tile, tile, pl.BlockSpec((1, 2, ROW_TILE), lambda g, j: (g, 0, j))],
        out_shape=[jax.ShapeDtypeStruct((ROWS, D_MODEL), F32), jax.ShapeDtypeStruct((ROWS, D_MODEL), BF16),
                   jax.ShapeDtypeStruct((FOX_PAIRS, 2, ROWS), F32)],
        name="fox_attn_dkv", compiler_params=_params(("parallel", "arbitrary")),
    )(qn, kn, proj, c2, crow, lse2, delta2, do)


def _gla_chunk(q, k, v, g, st, dot):
    c = q.shape[0]
    incl = _iota((c, c), 1) <= _iota((c, c), 0)
    b = _dot_hi(incl.astype(F32), g, "nn")
    qe = q * (GLA_DK ** -0.5) * jnp.exp(b)
    ke = k * jnp.exp(-b)
    att = jnp.where(incl, dot(qe, ke, "nt"), 0.0)
    o = dot(att, v, "nn") + dot(qe, st, "nt")
    b_last = _last_row(b)
    kd = k * jnp.exp(b_last - b)
    return o, st * jnp.exp(b_last) + dot(v, kd, "tn")


def _tri_inverse(low):
    c = low.shape[0]
    eye = (_iota((c, c), 0) == _iota((c, c), 1)).astype(F32)
    x, p = eye - low, low
    span = 2
    while span < c:
        p = _dot_hi(p, p, "nn")
        x = _dot_hi(x, eye + p, "nn")
        span *= 2
    return x


def _solve(low, rhs):
    return _dot_hi(_tri_inverse(low), rhs, "nn")


@jax.custom_vjp
def _solve_ad(low, rhs):
    return _solve(low, rhs)


def _solve_ad_fwd(low, rhs):
    inv = _tri_inverse(low)
    sol = _dot_hi(inv, rhs, "nn")
    return sol, (inv, sol)


def _solve_ad_bwd(res, ct):
    inv, sol = res
    drhs = _dot_hi(inv, ct, "tn")
    return -_dot_hi(drhs, sol, "nt"), drhs


_solve_ad.defvjp(_solve_ad_fwd, _solve_ad_bwd)


def _gdn_chunk(q, k, v, gates, st, h, dot, solve):
    c = q.shape[0]
    gcol = _lane_col(gates, h)
    beta = _lane_col(gates, h + GDN_HEADS)
    incl = _iota((c, c), 1) <= _iota((c, c), 0)
    strict = _iota((c, c), 1) < _iota((c, c), 0)
    incl_f = incl.astype(F32)
    b = _dot_hi(incl_f, jnp.broadcast_to(gcol, (c, LANE)), "nn")
    g_sq = jnp.broadcast_to(gcol, (c, c))
    b_col = _dot_hi(incl_f, g_sq, "nn")
    b_row = _dot_hi(g_sq, (_iota((c, c), 0) <= _iota((c, c), 1)).astype(F32), "tn")
    diff = b_col - b_row
    kb, vb = k * beta, v * beta
    low = dot(kb, k, "nt") * jnp.exp(jnp.where(strict, diff, NEG))
    eb = jnp.exp(b)
    sol = solve(low, jnp.concatenate([vb, kb * eb], axis=1))
    v_new = sol[:, :GDN_DV] - dot(sol[:, GDN_DV:], st, "nn")
    att = dot(q, k, "nt") * jnp.exp(jnp.where(incl, diff, NEG))
    o = dot(att, v_new, "nn") + dot(q * eb, st, "nn")
    b_last = _last_row(b)
    kd = k * jnp.exp(b_last - b)
    return o, st * jnp.exp(b_last) + dot(kd, v_new, "tn")


N_CHUNKS = ROWS // CHUNK


def _gla_scan_fwd(proj, garr):
    def body(q_ref, k_ref, v_ref, g_ref, o_ref, states_ref, st):
        @pl.when(pl.program_id(1) == 0)
        def _():
            st[...] = jnp.zeros(st.shape, F32)

        s_in = st[...]
        states_ref[0, 0] = s_in
        o, s_out = _gla_chunk(q_ref[...], k_ref[...], v_ref[...], g_ref[...], s_in, _dot)
        o_ref[...] = o
        st[...] = s_out

    qk = GLA_QK // GLA_DK
    return pl.pallas_call(
        body, grid=(GLA_HEADS, N_CHUNKS),
        in_specs=[
            pl.BlockSpec((CHUNK, GLA_DK), lambda h, c: (c, h)),
            pl.BlockSpec((CHUNK, GLA_DK), lambda h, c: (c, qk + h)),
            pl.BlockSpec((CHUNK, GLA_DV), lambda h, c: (c, qk + h)),
            pl.BlockSpec((CHUNK, GLA_DK), lambda h, c: (c, h)),
        ],
        out_specs=[
            pl.BlockSpec((CHUNK, GLA_DV), lambda h, c: (c, h)),
            pl.BlockSpec((1, 1, GLA_DV, GLA_DK), lambda h, c: (h, c, 0, 0)),
        ],
        out_shape=[jax.ShapeDtypeStruct((ROWS, GLA_V), F32),
                   jax.ShapeDtypeStruct((GLA_HEADS, N_CHUNKS, GLA_DV, GLA_DK), F32)],
        scratch_shapes=[pltpu.VMEM((GLA_DV, GLA_DK), F32)],
        name="gla_scan_fwd", compiler_params=_params(("parallel", "arbitrary")),
    )(proj, proj, proj, garr)


def _gla_scan_bwd(proj, garr, states, do):
    last = N_CHUNKS - 1

    def body(q_ref, k_ref, v_ref, g_ref, s_ref, do_ref, dq_ref, dk_ref, dv_ref, dg_ref, dst):
        @pl.when(pl.program_id(1) == 0)
        def _():
            dst[...] = jnp.zeros(dst.shape, F32)

        _, vjp = jax.vjp(lambda q, k, v, g, s: _gla_chunk(q, k, v, g, s, _dot_ad),
                         q_ref[...], k_ref[...], v_ref[...], g_ref[...], s_ref[0, 0])
        dq, dk, dv, dg, ds = vjp((do_ref[...], dst[...]))
        dq_ref[...] = dq.astype(dq_ref.dtype)
        dk_ref[...] = dk.astype(dk_ref.dtype)
        dv_ref[...] = dv.astype(dv_ref.dtype)
        dg_ref[...] = dg
        dst[...] = ds

    qk = GLA_QK // GLA_DK
    kspec = pl.BlockSpec((CHUNK, GLA_DK), lambda h, c: (last - c, h))
    vspec = pl.BlockSpec((CHUNK, GLA_DV), lambda h, c: (last - c, h))
    return pl.pallas_call(
        body, grid=(GLA_HEADS, N_CHUNKS),
        in_specs=[
            kspec,
            pl.BlockSpec((CHUNK, GLA_DK), lambda h, c: (last - c, qk + h)),
            pl.BlockSpec((CHUNK, GLA_DV), lambda h, c: (last - c, qk + h)),
            kspec,
            pl.BlockSpec((1, 1, GLA_DV, GLA_DK), lambda h, c: (h, last - c, 0, 0)),
            vspec,
        ],
        out_specs=[kspec, kspec, vspec, kspec],
        out_shape=[jax.ShapeDtypeStruct((ROWS, GLA_QK), BF16), jax.ShapeDtypeStruct((ROWS, GLA_QK), BF16),
                   jax.ShapeDtypeStruct((ROWS, GLA_V), BF16), jax.ShapeDtypeStruct((ROWS, GLA_QK), F32)],
        scratch_shapes=[pltpu.VMEM((GLA_DV, GLA_DK), F32)],
        name="gla_scan_bwd", compiler_params=_params(("parallel", "arbitrary")),
    )(proj, proj, proj, garr, states, do)


def _gdn_scan_fwd(qn, kn, vn, gates):
    def body(q_ref, k_ref, v_ref, g_ref, o_ref, states_ref, st):
        h = pl.program_id(1)

        @pl.when(pl.program_id(0) == 0)
        def _():
            st[h] = jnp.zeros((GDN_DK, GDN_DV), F32)

        s_in = st[h]
        states_ref[0, 0] = s_in
        o, s_out = _gdn_chunk(q_ref[...], k_ref[...], v_ref[...], g_ref[...], s_in, h, _dot, _solve)
        o_ref[...] = o
        st[h] = s_out

    hspec = pl.BlockSpec((CHUNK, GDN_DK), lambda c, h: (c, h))
    return pl.pallas_call(
        body, grid=(N_CHUNKS, GDN_HEADS),
        in_specs=[hspec, hspec, hspec, pl.BlockSpec((CHUNK, LANE), lambda c, h: (c, 0))],
        out_specs=[hspec, pl.BlockSpec((1, 1, GDN_DK, GDN_DV), lambda c, h: (h, c, 0, 0))],
        out_shape=[jax.ShapeDtypeStruct((ROWS, D_MODEL), F32),
                   jax.ShapeDtypeStruct((GDN_HEADS, N_CHUNKS, GDN_DK, GDN_DV), F32)],
        scratch_shapes=[pltpu.VMEM((GDN_HEADS, GDN_DK, GDN_DV), F32)],
        name="gdn_scan_fwd", compiler_params=_params(("arbitrary", "arbitrary")),
    )(qn, kn, vn, gates)


def _gdn_scan_bwd(qn, kn, vn, gates, states, do):
    last = N_CHUNKS - 1

    def body(q_ref, k_ref, v_ref, g_ref, s_ref, do_ref, dq_ref, dk_ref, dv_ref, dg_ref, dst):
        h = pl.program_id(1)

        @pl.when(pl.program_id(0) == 0)
        def _():
            dst[h] = jnp.zeros((GDN_DK, GDN_DV), F32)

        @pl.when(h == 0)
        def _():
            dg_ref[...] = jnp.zeros(dg_ref.shape, F32)

        _, vjp = jax.vjp(lambda q, k, v, g, s: _gdn_chunk(q, k, v, g, s, h, _dot_ad, _solve_ad),
                         q_ref[...], k_ref[...], v_ref[...], g_ref[...], s_ref[0, 0])
        dq, dk, dv, dg, ds = vjp((do_ref[...], dst[h]))
        dq_ref[...] = dq
        dk_ref[...] = dk
        dv_ref[...] = dv
        dg_ref[...] += dg
        dst[h] = ds

    hspec = pl.BlockSpec((CHUNK, GDN_DK), lambda c, h: (last - c, h))
    gspec = pl.BlockSpec((CHUNK, LANE), lambda c, h: (last - c, 0))
    return pl.pallas_call(
        body, grid=(N_CHUNKS, GDN_HEADS),
        in_specs=[hspec, hspec, hspec, gspec,
                  pl.BlockSpec((1, 1, GDN_DK, GDN_DV), lambda c, h: (h, last - c, 0, 0)), hspec],
        out_specs=[hspec, hspec, hspec, gspec],
        out_shape=[jax.ShapeDtypeStruct((ROWS, D_MODEL), F32)] * 3 + [jax.ShapeDtypeStruct((ROWS, LANE), F32)],
        scratch_shapes=[pltpu.VMEM((GDN_HEADS, GDN_DK, GDN_DV), F32)],
        name="gdn_scan_bwd", compiler_params=_params(("arbitrary", "arbitrary")),
    )(qn, kn, vn, gates, states, do)


def _gdn_conv_fwd(proj, w, kind, cb):
    fn = _gdn_conv_fn(kind)
    nt = ROWS // ROW_TILE

    def body(prev_ref, cur_ref, w_ref, o_ref):
        (out,) = fn(pl.program_id(0), prev_ref[...], cur_ref[...], w_ref[...])
        o_ref[...] = out

    return pl.pallas_call(
        body, grid=(nt,),
        in_specs=[pl.BlockSpec((ROW_TILE, D_MODEL), lambda i: (jnp.maximum(i - 1, 0), cb)),
                  pl.BlockSpec((ROW_TILE, D_MODEL), lambda i: (i, cb)),
                  pl.BlockSpec(w.shape, lambda i: (0, 0))],
        out_specs=pl.BlockSpec((ROW_TILE, D_MODEL), lambda i: (i, 0)),
        out_shape=jax.ShapeDtypeStruct((ROWS, D_MODEL), F32),
        name="gdn_conv_fwd_" + kind, compiler_params=_params(("parallel",)),
    )(proj, proj, w)


def _gdn_conv_bwd(proj, w, kind, cb, ct):
    fn = _gdn_conv_fn(kind)
    nt = ROWS // ROW_TILE

    def body(prev_ref, cur_ref, w_ref, ct_ref, dx_ref, dw_ref, carry):
        step = pl.program_id(0)
        i = nt - 1 - step

        @pl.when(step == 0)
        def _():
            carry[...] = jnp.zeros(carry.shape, F32)
            dw_ref[...] = jnp.zeros(dw_ref.shape, F32)

        _, vjp = jax.vjp(lambda p, c, ww: fn(i, p, c, ww), prev_ref[...], cur_ref[...], w_ref[...])
        dprev, dcur, dw = vjp((ct_ref[...],))
        dx_ref[...] = (dcur + carry[...]).astype(dx_ref.dtype)
        carry[...] = dprev
        dw_ref[...] += dw

    return pl.pallas_call(
        body, grid=(nt,),
        in_specs=[pl.BlockSpec((ROW_TILE, D_MODEL), lambda s: (jnp.maximum(nt - 2 - s, 0), cb)),
                  pl.BlockSpec((ROW_TILE, D_MODEL), lambda s: (nt - 1 - s, cb)),
                  pl.BlockSpec(w.shape, lambda s: (0, 0)),
                  pl.BlockSpec((ROW_TILE, D_MODEL), lambda s: (nt - 1 - s, 0))],
        out_specs=[pl.BlockSpec((ROW_TILE, D_MODEL), lambda s: (nt - 1 - s, 0)),
                   pl.BlockSpec(w.shape, lambda s: (0, 0))],
        out_shape=[jax.ShapeDtypeStruct((ROWS, D_MODEL), BF16), jax.ShapeDtypeStruct(w.shape, F32)],
        scratch_shapes=[pltpu.VMEM((ROW_TILE, D_MODEL), F32)],
        name="gdn_conv_bwd_" + kind, compiler_params=_params(("arbitrary",)),
    )(proj, proj, w, ct)


def _loss_head(h, target):
    nt = ROWS // ROW_TILE
    lead = PAD_ROWS // ROW_TILE

    def body(h_ref, t_ref, dh_ref, acc_ref):
        i = pl.program_id(0)

        @pl.when(i == 0)
        def _():
            acc_ref[...] = jnp.zeros(acc_ref.shape, F32)

        err = (h_ref[...] - t_ref[...]) * jnp.where(i >= lead, 1.0, 0.0)
        dh_ref[...] = err * (1.0 / D_MODEL)
        acc_ref[...] += jnp.sum(err * err, axis=0, keepdims=True)

    return pl.pallas_call(
        body, grid=(nt,),
        in_specs=[pl.BlockSpec((ROW_TILE, D_MODEL), lambda i: (i, 0)),
                  pl.BlockSpec((ROW_TILE, D_MODEL), lambda i: (jnp.maximum(i - lead, 0), 0))],
        out_specs=[pl.BlockSpec((ROW_TILE, D_MODEL), lambda i: (i, 0)),
                   pl.BlockSpec((1, D_MODEL), lambda i: (0, 0))],
        out_shape=[jax.ShapeDtypeStruct((ROWS, D_MODEL), F32), jax.ShapeDtypeStruct((1, D_MODEL), F32)],
        name="loss_head", compiler_params=_params(("arbitrary",)),
    )(h, target)


MESH = pl.DeviceIdType.MESH
ANY = pl.BlockSpec(memory_space=pl.ANY)


def _all_gather(block, name):
    m_per, n = block.shape

    def body(x_ref, out_ref, send_sems, recv_sems, local_sem):
        x, y, c = lax.axis_index("x"), lax.axis_index("y"), lax.axis_index("c")
        me, sibling = (x, y, c), (x, y, 1 - c)
        chips = [(1 - x, y), (x, 1 - y), (1 - x, 1 - y)]

        def rows(px, py, pc):
            return out_ref.at[pl.ds((4 * px + 2 * py + pc) * m_per, m_per), :]

        def copy(k, blk, to, src=None):
            return pltpu.make_async_remote_copy(
                src_ref=rows(*blk) if src is None else src, dst_ref=rows(*blk),
                send_sem=send_sems.at[k], recv_sem=recv_sems.at[k], device_id=to, device_id_type=MESH)

        mine = pltpu.make_async_copy(x_ref, rows(*me), local_sem)
        mine.start()
        first = [copy(0, me, sibling, src=x_ref)]
        first += [copy(1 + j, me, (*chip, c), src=x_ref) for j, chip in enumerate(chips)]
        for cp in first:
            cp.start()
        passed = [copy(4 + j, (*chip, c), sibling) for j, chip in enumerate(chips)]
        for j, chip in enumerate(chips):
            copy(1 + j, (*chip, c), me).wait_recv()
            passed[j].start()
        copy(0, sibling, me).wait_recv()
        for j, chip in enumerate(chips):
            copy(4 + j, (*chip, 1 - c), me).wait_recv()
        for cp in first + passed:
            cp.wait_send()
        mine.wait()

    return pl.pallas_call(
        body, out_shape=jax.ShapeDtypeStruct((N_DEV * m_per, n), block.dtype),
        in_specs=[ANY], out_specs=ANY,
        scratch_shapes=[pltpu.SemaphoreType.DMA((7,)), pltpu.SemaphoreType.DMA((7,)), pltpu.SemaphoreType.DMA],
        name=name,
    )(block)


def _exchange(slabs, name):
    def body(x_ref, out_ref, send_sems, recv_sems, local_sem):
        x, y, c = lax.axis_index("x"), lax.axis_index("y"), lax.axis_index("c")
        me = 4 * x + 2 * y + c
        local = pltpu.make_async_copy(x_ref.at[me], out_ref.at[me], local_sem)
        local.start()
        sends, peers = [], []
        for k in range(1, N_DEV):
            px = 1 - x if k & 4 else x
            py = 1 - y if k & 2 else y
            pc = 1 - c if k & 1 else c
            peer = 4 * px + 2 * py + pc
            cp = pltpu.make_async_remote_copy(
                src_ref=x_ref.at[peer], dst_ref=out_ref.at[me], send_sem=send_sems.at[k - 1],
                recv_sem=recv_sems.at[k - 1], device_id=(px, py, pc), device_id_type=MESH)
            cp.start()
            sends.append(cp)
            peers.append((peer, (px, py, pc)))
        for k, (peer, pid) in enumerate(peers):
            pltpu.make_async_remote_copy(
                src_ref=x_ref.at[me], dst_ref=out_ref.at[peer], send_sem=send_sems.at[k],
                recv_sem=recv_sems.at[k], device_id=pid, device_id_type=MESH).wait_recv()
        for cp in sends:
            cp.wait_send()
        local.wait()

    return pl.pallas_call(
        body, out_shape=jax.ShapeDtypeStruct(slabs.shape, slabs.dtype), in_specs=[ANY], out_specs=ANY,
        scratch_shapes=[pltpu.SemaphoreType.DMA((7,)), pltpu.SemaphoreType.DMA((7,)), pltpu.SemaphoreType.DMA],
        name=name,
    )(slabs)


def _adamw(recv, w, m, v, name):
    rows, cols = w.shape
    tr = min(rows, PACK_ROW_TILE)

    def body(r_ref, w_ref, m_ref, v_ref, g_ref, d_ref, m2_ref, v2_ref):
        g = r_ref[0].astype(F32)
        for k in range(1, N_DEV):
            g = g + r_ref[k].astype(F32)
        m2 = ADAM_B1 * m_ref[...] + (1.0 - ADAM_B1) * g
        v2 = ADAM_B2 * v_ref[...] + (1.0 - ADAM_B2) * (g * g)
        m_hat = m2 / (1.0 - ADAM_B1 ** ADAM_STEP)
        v_hat = v2 / (1.0 - ADAM_B2 ** ADAM_STEP)
        g_ref[...] = g
        d_ref[...] = -ADAM_LR * (m_hat / (jnp.sqrt(v_hat) + ADAM_EPS) + ADAM_WD * w_ref[...])
        m2_ref[...] = m2
        v2_ref[...] = v2

    spec = pl.BlockSpec((tr, cols), lambda i: (i, 0))
    return pl.pallas_call(
        body, grid=(rows // tr,),
        in_specs=[pl.BlockSpec((N_DEV, tr, cols), lambda i: (0, i, 0)), spec, spec, spec],
        out_specs=[spec] * 4, out_shape=[jax.ShapeDtypeStruct((rows, cols), F32)] * 4,
        name=name, compiler_params=_params(("parallel",)),
    )(recv, w, m, v)


def _padded_rows(n, mult):
    rows = -(-n // PACK_COLS)
    return -(-rows // mult) * mult


def _pack(flat_pieces, mult, dtype, lead=()):
    flat = jnp.concatenate([p.astype(dtype) for p in flat_pieces], axis=-1)
    n = flat.shape[-1]
    rows = _padded_rows(n, mult)
    flat = jnp.pad(flat, [(0, 0)] * len(lead) + [(0, rows * PACK_COLS - n)])
    return flat.reshape(lead + (rows, PACK_COLS))


def _unpack(packed, shapes):
    flat = packed.reshape(-1)
    out, off = {}, 0
    for name, shape in shapes:
        size = 1
        for s in shape:
            size *= s
        out[name] = flat[off:off + size].reshape(shape)
        off += size
    return out


def _to_slabs(full, shape, axis):
    split = full.reshape(shape[:axis] + (N_DEV, shape[axis]) + shape[axis + 1:])
    return jnp.moveaxis(split, axis, 0).reshape(N_DEV, -1)


def _from_gathered(seg, shape, axis):
    blocks = jnp.moveaxis(seg.reshape((N_DEV,) + shape), 0, axis)
    return blocks.reshape(shape[:axis] + (N_DEV * shape[axis],) + shape[axis + 1:])


def _pad_cols(w, total):
    return jnp.pad(w, [(0, 0)] * (w.ndim - 1) + [(0, total - w.shape[-1])])


def _pad_lanes(v):
    return _pad_cols(v, LANE)


def _x(arr, width, cb0=0, moves=0):
    return (arr, width, cb0, moves)


def _rms_fwd(h, g):
    return _row_fwd(_rms_fn, [_x(h, D_MODEL)], [g], [], [(D_MODEL, D_MODEL, BF16)], "rms_fwd")[0]


def _rms_bwd(h, g, dy, add):
    dh, dg = _row_bwd(_rms_fn, [_x(h, D_MODEL)], [g], [], [_x(dy, D_MODEL)], [(D_MODEL, F32)], "rms_bwd", adds=[add])
    return dh, dg


SW_TILE, SW_WIDTH = 384, 256
SW_NCOL = D_FF // SW_WIDTH


def _ffn_fwd(h, g, wgu, wd):
    y = _rms_fwd(h, g)
    gu = _matmul(y, wgu, "nn", F32, "ffn_gate_up")
    a = _row_fwd(_swiglu_fn, [_x(gu, SW_WIDTH, 0, 1), _x(gu, SW_WIDTH, SW_NCOL, 1)], [], [],
                 [(SW_WIDTH, D_FF, BF16)], "swiglu_fwd", tm=SW_TILE, ncol=SW_NCOL)[0]
    h2 = _matmul(a, wd, "nn", F32, "ffn_down", add=h)
    return h2, (h, y, gu, a)


def _ffn_bwd(dh2, saved, g, wgu, wd):
    h, y, gu, a = saved
    da = _matmul(dh2, wd, "nt", F32, "ffn_down_dx")
    dwd = _matmul(a, dh2, "tn", F32, "ffn_down_dw")
    dgt, dup = _row_bwd(_swiglu_fn, [_x(gu, SW_WIDTH, 0, 1), _x(gu, SW_WIDTH, SW_NCOL, 1)], [], [],
                        [_x(da, SW_WIDTH, 0, 1)], [(D_FF, BF16), (D_FF, BF16)], "swiglu_bwd",
                        tm=SW_TILE, ncol=SW_NCOL)
    dgu = jnp.concatenate([dgt, dup], axis=1)
    dwgu = _matmul(y, dgu, "tn", F32, "ffn_gate_up_dw")
    dy = _matmul(dgu, wgu, "nt", F32, "ffn_gate_up_dx")
    dh, dg = _rms_bwd(h, g, dy, dh2)
    return dh, dg, dwgu, dwd


def _fox_consts():
    lane = jnp.arange(D_MODEL) // FOX_DH
    e_expand = (jnp.arange(LANE)[:, None] == lane[None, :]).astype(F32)
    e_mean = e_expand.T * (1.0 / FOX_DH)
    half = jnp.arange(LANE) // FOX_DH
    blockdiag = (half[:, None] == half[None, :]).astype(F32)
    return e_mean, e_expand, blockdiag


def _pair_layout(c):
    heads = c[:, :FOX_HEADS].T
    crow = heads.reshape(FOX_PAIRS, 2, ROWS)
    c2 = jnp.broadcast_to(crow.transpose(0, 2, 1)[:, :, :, None], (FOX_PAIRS, ROWS, 2, FOX_DH))
    return c2.reshape(FOX_PAIRS, ROWS, LANE), crow


def _fox_fwd(y, p):
    proj = _matmul(y, p["w_in"], "nn", F32, "fox_in")
    e_mean, e_expand, _ = _fox_consts()
    nb = D_MODEL // LANE
    xs = [_x(proj, D_MODEL, 0), _x(proj, D_MODEL, 1), _x(proj, LANE, 4 * nb)]
    ps = [p["q_gain"], p["k_gain"], p["b_f"]]
    qn, kn, lf = _row_fwd(_fox_pre_fn, xs, ps, [e_mean, e_expand],
                          [(D_MODEL, D_MODEL, BF16), (D_MODEL, D_MODEL, BF16), (LANE, LANE, F32)], "fox_pre_fwd")
    c = _seq_cumsum(lf, False, "fox_cumsum")
    c2, crow = _pair_layout(c)
    o, lse2 = _fox_attn_fwd(qn, kn, proj, c2, crow)
    om = _row_fwd(_fox_post_fn, [_x(o, D_MODEL, 0), _x(proj, D_MODEL, 3)], [], [],
                  [(D_MODEL, D_MODEL, BF16)], "fox_post_fwd")[0]
    return om, (proj, qn, kn, c2, crow, o, lse2)


def _fox_bwd(dom, y, saved, p):
    proj, qn, kn, c2, crow, o, lse2 = saved
    e_mean, e_expand, blockdiag = _fox_consts()
    nb = D_MODEL // LANE
    do, dgate = _row_bwd(_fox_post_fn, [_x(o, D_MODEL, 0), _x(proj, D_MODEL, 3)], [], [], [_x(dom, D_MODEL)],
                         [(D_MODEL, F32), (D_MODEL, BF16)], "fox_post_bwd")
    dq, delta2, dcq2 = _fox_attn_dq(qn, kn, proj, c2, crow, lse2, o, do, blockdiag)
    dk, dv, dcrow = _fox_attn_dkv(qn, kn, proj, c2, crow, lse2, delta2, do)
    dcq = dcq2[:, :, ::FOX_DH].transpose(1, 0, 2).reshape(ROWS, FOX_HEADS)
    dc = _pad_cols(dcq + dcrow.reshape(FOX_HEADS, ROWS).T, LANE)
    dlf = _seq_cumsum(dc, True, "fox_cumsum_rev")
    xs = [_x(proj, D_MODEL, 0), _x(proj, D_MODEL, 1), _x(proj, LANE, 4 * nb)]
    ps = [p["q_gain"], p["k_gain"], p["b_f"]]
    dqr, dkr, df, dqg, dkg, dbf = _row_bwd(
        _fox_pre_fn, xs, ps, [e_mean, e_expand], [_x(dq, D_MODEL), _x(dk, D_MODEL), _x(dlf, LANE)],
        [(D_MODEL, BF16), (D_MODEL, BF16), (LANE, BF16)], "fox_pre_bwd")
    dproj = jnp.concatenate([dqr, dkr, dv, dgate, df], axis=1)
    grads = {
        "w_in": _matmul(y, dproj, "tn", F32, "fox_in_dw")[:, :FOX_IN],
        "q_gain": dqg.reshape(FOX_HEADS, FOX_DH).sum(axis=0),
        "k_gain": dkg.reshape(FOX_HEADS, FOX_DH).sum(axis=0),
        "b_f": dbf[0, :FOX_HEADS],
    }
    return _matmul(dproj, p["w_in"], "nt", F32, "fox_in_dx"), grads


def _gla_fwd(y, p):
    proj = _matmul(y, p["w_in"], "nn", F32, "gla_in")
    small = (2 * GLA_QK + 2 * GLA_V) // LANE
    garr = _row_fwd(_gla_pre_fn, [_x(proj, LANE, small)], [p["w_alpha"], p["b_alpha"]], [],
                    [(GLA_QK, GLA_QK, F32)], "gla_pre_fwd")[0]
    o, states = _gla_scan_fwd(proj, garr)
    om = _row_fwd(_gla_post_fn, [_x(o, GLA_V, 0), _x(proj, GLA_V, 2)], [p["o_gain"]], [],
                  [(GLA_V, GLA_V, BF16)], "gla_post_fwd")[0]
    return om, (proj, garr, states, o)


def _gla_bwd(dom, y, saved, p):
    proj, garr, states, o = saved
    small = (2 * GLA_QK + 2 * GLA_V) // LANE
    do, dr, dgain = _row_bwd(_gla_post_fn, [_x(o, GLA_V, 0), _x(proj, GLA_V, 2)], [p["o_gain"]], [],
                             [_x(dom, GLA_V)], [(GLA_V, F32), (GLA_V, BF16)], "gla_post_bwd")
    dq, dk, dv, dg = _gla_scan_bwd(proj, garr, states, do)
    dalr, dwa, dba = _row_bwd(_gla_pre_fn, [_x(proj, LANE, small)], [p["w_alpha"], p["b_alpha"]], [],
                              [_x(dg, GLA_QK)], [(LANE, BF16)], "gla_pre_bwd")
    dproj = jnp.concatenate([dq, dk, dv, dr, dalr], axis=1)
    grads = {
        "w_in": _matmul(y, dproj, "tn", F32, "gla_in_dw")[:, :GLA_IN],
        "w_alpha": dwa[:GLA_RANK], "b_alpha": dba[0], "o_gain": dgain[0],
    }
    return _matmul(dproj, p["w_in"], "nt", F32, "gla_in_dx"), grads


def _gdn_fwd(y, p):
    proj = _matmul(y, p["w_in"], "nn", F32, "gdn_in")
    nb = D_MODEL // LANE
    qn = _gdn_conv_fwd(proj, p["conv_q"], "q", 0)
    kn = _gdn_conv_fwd(proj, p["conv_k"], "k", 1)
    vn = _gdn_conv_fwd(proj, p["conv_v"], "v", 2)
    gates = _row_fwd(_gdn_gates_fn, [_x(proj, LANE, 4 * nb)], [p["a_log"], p["dt_bias"]], [],
                     [(LANE, LANE, F32)], "gdn_gates_fwd")[0]
    o, states = _gdn_scan_fwd(qn, kn, vn, gates)
    om = _row_fwd(_gdn_post_fn, [_x(o, D_MODEL, 0), _x(proj, D_MODEL, 3)], [p["o_gain"]], [],
                  [(D_MODEL, D_MODEL, BF16)], "gdn_post_fwd")[0]
    return om, (proj, qn, kn, vn, gates, states, o)


def _gdn_bwd(dom, y, saved, p):
    proj, qn, kn, vn, gates, states, o = saved
    nb = D_MODEL // LANE
    do, dgate, dgain = _row_bwd(_gdn_post_fn, [_x(o, D_MODEL, 0), _x(proj, D_MODEL, 3)], [p["o_gain"]], [],
                                [_x(dom, D_MODEL)], [(D_MODEL, F32), (D_MODEL, BF16)], "gdn_post_bwd")
    dqn, dkn, dvn, dgates = _gdn_scan_bwd(qn, kn, vn, gates, states, do)
    dsm, dalog, ddtb = _row_bwd(_gdn_gates_fn, [_x(proj, LANE, 4 * nb)], [p["a_log"], p["dt_bias"]], [],
                                [_x(dgates, LANE)], [(LANE, BF16)], "gdn_gates_bwd")
    dxq, dwq = _gdn_conv_bwd(proj, p["conv_q"], "q", 0, dqn)
    dxk, dwk = _gdn_conv_bwd(proj, p["conv_k"], "k", 1, dkn)
    dxv, dwv = _gdn_conv_bwd(proj, p["conv_v"], "v", 2, dvn)
    dproj = jnp.concatenate([dxq, dxk, dxv, dgate, dsm], axis=1)
    grads = {
        "w_in": _matmul(y, dproj, "tn", F32, "gdn_in_dw")[:, :GDN_IN],
        "conv_w": jnp.concatenate([dwq[:GDN_CONV], dwk[:GDN_CONV], dwv[:GDN_CONV]], axis=1),
        "a_log": dalog[0, :GDN_HEADS], "dt_bias": ddtb[0, :GDN_HEADS], "o_gain": dgain[0],
    }
    return _matmul(dproj, p["w_in"], "nt", F32, "gdn_in_dx"), grads


_MIXERS = ((_fox_fwd, _fox_bwd), (_gla_fwd, _gla_bwd), (_gdn_fwd, _gdn_bwd))


def _mixer_params(i, wts, rep):
    kind, j = i % 3, i // 3
    if kind == 0:
        return {
            "w_in": _pad_cols(wts["fox_w_in"][j], FOX_IN_PAD), "w_out": wts["fox_w_out"][j],
            "q_gain": jnp.tile(rep["fox_q_gain"][j], FOX_HEADS)[None],
            "k_gain": jnp.tile(rep["fox_k_gain"][j], FOX_HEADS)[None],
            "b_f": _pad_lanes(rep["fox_b_f"][j][None]),
        }
    if kind == 1:
        return {
            "w_in": _pad_cols(wts["gla_w_in"][j], GLA_IN_PAD), "w_out": wts["gla_w_out"][j],
            "w_alpha": jnp.pad(wts["gla_w_alpha2"][j], ((0, LANE - GLA_RANK), (0, 0))).astype(F32),
            "b_alpha": rep["gla_b_alpha"][j][None], "o_gain": rep["gla_o_gain"][j][None],
        }
    conv = jnp.pad(wts["gdn_conv_w"][j][:, 0, :].astype(F32), ((0, 8 - GDN_CONV), (0, 0)))
    return {
        "w_in": _pad_cols(wts["gdn_w_in"][j], GDN_IN_PAD), "w_out": wts["gdn_w_out"][j],
        "conv_q": conv[:, :D_MODEL], "conv_k": conv[:, D_MODEL:2 * D_MODEL], "conv_v": conv[:, 2 * D_MODEL:],
        "a_log": _pad_lanes(rep["gdn_a_log"][j][None]), "dt_bias": _pad_lanes(rep["gdn_dt_bias"][j][None]),
        "o_gain": rep["gdn_o_gain"][j][None],
    }


def kernel(x, meta_tokens, norm_mix, norm_ffn, w_gate_up, w_down, fox_w_in, fox_b_f, fox_q_gain, fox_k_gain, fox_w_out, gla_w_in, gla_w_alpha2, gla_b_alpha, gla_o_gain, gla_w_out, gdn_w_in, gdn_conv_w, gdn_a_log, gdn_dt_bias, gdn_o_gain, gdn_w_out, loss_target, m_meta_tokens, m_norm_mix, m_norm_ffn, m_w_gate_up, m_w_down, m_fox_w_in, m_fox_b_f, m_fox_q_gain, m_fox_k_gain, m_fox_w_out, m_gla_w_in, m_gla_w_alpha2, m_gla_b_alpha, m_gla_o_gain, m_gla_w_out, m_gdn_w_in, m_gdn_conv_w, m_gdn_a_log, m_gdn_dt_bias, m_gdn_o_gain, m_gdn_w_out, v_meta_tokens, v_norm_mix, v_norm_ffn, v_w_gate_up, v_w_down, v_fox_w_in, v_fox_b_f, v_fox_q_gain, v_fox_k_gain, v_fox_w_out, v_gla_w_in, v_gla_w_alpha2, v_gla_b_alpha, v_gla_o_gain, v_gla_w_out, v_gdn_w_in, v_gdn_conv_w, v_gdn_a_log, v_gdn_dt_bias, v_gdn_o_gain, v_gdn_w_out):
    given = dict(locals())
    w_loc = {n: given[n] for n in WEIGHT_ORDER}
    m_loc = {n: given["m_" + n] for n in WEIGHT_ORDER}
    v_loc = {n: given["v_" + n] for n in WEIGHT_ORDER}
    sharded_shapes = [(n, s) for n, s, _ in SHARDED]

    packed_w = _pack([w_loc[n].reshape(-1) for n, _, _ in SHARDED], PACK_ROW_TILE, BF16)
    rows_w = packed_w.shape[0]
    gathered = _all_gather(packed_w, "all_gather_weights").reshape(N_DEV, rows_w * PACK_COLS)
    wts, off = {}, 0
    for name, shape, axis in SHARDED:
        size = 1
        for s in shape:
            size *= s
        wts[name] = _from_gathered(gathered[:, off:off + size], shape, axis)
        off += size
    meta_full = _from_gathered(_all_gather(meta_tokens, "all_gather_meta").reshape(N_DEV, -1), (N_META, LANE), 1)
    rep = {n: w_loc[n] for n, _ in REPLICATED}

    h = jnp.concatenate([jnp.zeros((FIRST_ROW, D_MODEL), F32), meta_full, x[0]], axis=0)
    saved = []
    for i in range(DEPTH):
        p = _mixer_params(i, wts, rep)
        fwd, _ = _MIXERS[i % 3]
        g_mix = rep["norm_mix"][i][None]
        y = _rms_fwd(h, g_mix)
        om, mix_saved = fwd(y, p)
        h_mid = _matmul(om, p["w_out"], "nn", F32, "mixer_out", add=h)
        g_ffn = rep["norm_ffn"][i][None]
        h_out, ffn_saved = _ffn_fwd(h_mid, g_ffn, wts["w_gate_up"][i], wts["w_down"][i])
        saved.append((p, h, y, om, mix_saved, ffn_saved))
        h = h_out

    dh, sq = _loss_head(h, loss_target[0])
    loss = lax.psum(0.5 * jnp.sum(sq) * (1.0 / D_MODEL), ("x", "y", "c"))

    gw = {n: [None] * DEPTH for n in ("w_gate_up", "w_down", "norm_mix", "norm_ffn")}
    gm = {}
    for i in reversed(range(DEPTH)):
        p, h_in, y, om, mix_saved, ffn_saved = saved[i]
        kind, j = i % 3, i // 3
        _, bwd = _MIXERS[kind]
        dh_mid, gw["norm_ffn"][i], gw["w_gate_up"][i], gw["w_down"][i] = _ffn_bwd(
            dh, ffn_saved, rep["norm_ffn"][i][None], wts["w_gate_up"][i], wts["w_down"][i])
        dom = _matmul(dh_mid, p["w_out"], "nt", F32, "mixer_out_dx")
        dwo = _matmul(om, dh_mid, "tn", F32, "mixer_out_dw")
        dy, grads = bwd(dom, y, mix_saved, p)
        grads["w_out"] = dwo
        gm[(kind, j)] = grads
        dh, gw["norm_mix"][i] = _rms_bwd(h_in, rep["norm_mix"][i][None], dy, dh_mid)

    full = {
        "meta_tokens": dh[FIRST_ROW:PAD_ROWS],
        "w_gate_up": jnp.stack(gw["w_gate_up"]), "w_down": jnp.stack(gw["w_down"]),
        "fox_w_in": jnp.stack([gm[(0, 0)]["w_in"], gm[(0, 1)]["w_in"]]),
        "fox_w_out": jnp.stack([gm[(0, 0)]["w_out"], gm[(0, 1)]["w_out"]]),
        "gla_w_in": gm[(1, 0)]["w_in"][None], "gla_w_alpha2": gm[(1, 0)]["w_alpha"][None],
        "gla_w_out": gm[(1, 0)]["w_out"][None],
        "gdn_w_in": gm[(2, 0)]["w_in"][None], "gdn_conv_w": gm[(2, 0)]["conv_w"][None, :, None, :],
        "gdn_w_out": gm[(2, 0)]["w_out"][None],
    }
    partial = {
        "norm_mix": jnp.concatenate(gw["norm_mix"], axis=0), "norm_ffn": jnp.concatenate(gw["norm_ffn"], axis=0),
        "fox_b_f": jnp.stack([gm[(0, 0)]["b_f"], gm[(0, 1)]["b_f"]]),
        "fox_q_gain": jnp.stack([gm[(0, 0)]["q_gain"], gm[(0, 1)]["q_gain"]]),
        "fox_k_gain": jnp.stack([gm[(0, 0)]["k_gain"], gm[(0, 1)]["k_gain"]]),
        "gla_b_alpha": gm[(1, 0)]["b_alpha"][None], "gla_o_gain": gm[(1, 0)]["o_gain"][None],
        "gdn_a_log": gm[(2, 0)]["a_log"][None], "gdn_dt_bias": gm[(2, 0)]["dt_bias"][None],
        "gdn_o_gain": gm[(2, 0)]["o_gain"][None],
    }

    full_shapes = {n: s[:a] + (N_DEV * s[a],) + s[a + 1:] for n, s, a in SHARDED}
    slabs = _pack([_to_slabs(full[n].reshape(full_shapes[n]), s, a) for n, s, a in SHARDED],
                  PACK_ROW_TILE, BF16, lead=(N_DEV,))
    recv = _exchange(slabs, "exchange_grads")
    outs = _adamw(recv, *[_pack([d[n].reshape(-1) for n, _, _ in SHARDED], PACK_ROW_TILE, F32)
                          for d in (w_loc, m_loc, v_loc)], "adamw_sharded")
    res_sh = [_unpack(o, sharded_shapes) for o in outs]

    small = _pack([partial[n].reshape(-1) for n, _ in REPLICATED], 8, F32)
    recv_small = _exchange(jnp.broadcast_to(small[None], (N_DEV,) + small.shape), "exchange_small_grads")
    outs_small = _adamw(recv_small, *[_pack([d[n].reshape(-1) for n, _ in REPLICATED], 8, F32)
                                      for d in (w_loc, m_loc, v_loc)], "adamw_replicated")
    res_rep = [_unpack(o, list(REPLICATED)) for o in outs_small]

    rep_names = {n for n, _ in REPLICATED}
    result = [loss, dh[PAD_ROWS:][None]]
    for k in range(4):
        for n in WEIGHT_ORDER:
            result.append(res_rep[k][n] if n in rep_names else res_sh[k][n])
    return tuple(result)
```

```python
import functools

import jax
import jax.numpy as jnp
from jax import lax
from jax.experimental import pallas as pl
from jax.experimental.pallas import tpu as pltpu

F32, BF16 = jnp.float32, jnp.bfloat16
HIGHEST = lax.Precision.HIGHEST

D_MODEL = 1024
SEQ = 4096
N_META = 16
DEPTH = 4
NORM_EPS = 1e-6
LANE = 128
PAD_ROWS = 128
FIRST_ROW = PAD_ROWS - N_META
ROWS = PAD_ROWS + SEQ
ROW_TILE = 128
CHUNK = 64
NEG = -1e30
N_DEV = 8
VMEM_LIMIT = 56 * 1024 * 1024
VMEM_BLOCK_BUDGET = 36 * 1024 * 1024

FOX_HEADS, FOX_DH = 16, 64
FOX_IN, FOX_IN_PAD = 4 * D_MODEL + FOX_HEADS, 4 * D_MODEL + LANE
GLA_HEADS, GLA_DK, GLA_DV, GLA_RANK = 4, 128, 256, 16
GLA_QK, GLA_V = GLA_HEADS * GLA_DK, GLA_HEADS * GLA_DV
GLA_IN, GLA_IN_PAD = 2 * GLA_QK + 2 * GLA_V + GLA_RANK, 2 * GLA_QK + 2 * GLA_V + LANE
GDN_HEADS, GDN_DK, GDN_DV, GDN_CONV = 8, 128, 128, 4
GDN_CONV_DIM = 3 * GDN_HEADS * GDN_DK
GDN_IN, GDN_IN_PAD = GDN_CONV_DIM + GDN_HEADS * GDN_DV + 2 * GDN_HEADS, GDN_CONV_DIM + GDN_HEADS * GDN_DV + LANE
D_FF = 2816

ADAM_LR, ADAM_B1, ADAM_B2, ADAM_EPS, ADAM_WD, ADAM_STEP = 0.001, 0.9, 0.999, 1e-08, 0.01, 10

PACK_COLS = 1024
PACK_ROW_TILE = 256

SHARDED = (
    ("meta_tokens", (16, 128), 1),
    ("w_gate_up", (4, 1024, 704), 2),
    ("w_down", (4, 352, 1024), 1),
    ("fox_w_in", (2, 1024, 514), 2),
    ("fox_w_out", (2, 128, 1024), 1),
    ("gla_w_in", (1, 1024, 386), 2),
    ("gla_w_alpha2", (1, 16, 64), 2),
    ("gla_w_out", (1, 128, 1024), 1),
    ("gdn_w_in", (1, 1024, 514), 2),
    ("gdn_conv_w", (1, 4, 1, 384), 3),
    ("gdn_w_out", (1, 128, 1024), 1),
)
REPLICATED = (
    ("norm_mix", (4, 1024)),
    ("norm_ffn", (4, 1024)),
    ("fox_b_f", (2, 16)),
    ("fox_q_gain", (2, 64)),
    ("fox_k_gain", (2, 64)),
    ("gla_b_alpha", (1, 512)),
    ("gla_o_gain", (1, 256)),
    ("gdn_a_log", (1, 8)),
    ("gdn_dt_bias", (1, 8)),
    ("gdn_o_gain", (1, 128)),
)
WEIGHT_ORDER = (
    "meta_tokens", "norm_mix", "norm_ffn", "w_gate_up", "w_down", "fox_w_in", "fox_b_f", "fox_q_gain",
    "fox_k_gain", "fox_w_out", "gla_w_in", "gla_w_alpha2", "gla_b_alpha", "gla_o_gain", "gla_w_out",
    "gdn_w_in", "gdn_conv_w", "gdn_a_log", "gdn_dt_bias", "gdn_o_gain", "gdn_w_out",
)

_DN = {
    "nn": (((1,), (0,)), ((), ())),
    "nt": (((1,), (1,)), ((), ())),
    "tn": (((0,), (0,)), ((), ())),
}


def _dot(a, b, kind):
    return lax.dot_general(a.astype(BF16), b.astype(BF16), _DN[kind], preferred_element_type=F32)


@functools.partial(jax.custom_vjp, nondiff_argnums=(2,))
def _dot_ad(a, b, kind):
    return _dot(a, b, kind)


def _dot_ad_fwd(a, b, kind):
    return _dot(a, b, kind), (a, b)


def _dot_ad_bwd(kind, res, ct):
    a, b = res
    if kind == "nn":
        return _dot(ct, b, "nt"), _dot(a, ct, "tn")
    if kind == "nt":
        return _dot(ct, b, "nn"), _dot(ct, a, "tn")
    return _dot(b, ct, "nt"), _dot(a, ct, "nn")


_dot_ad.defvjp(_dot_ad_fwd, _dot_ad_bwd)


def _dot_hi(a, b, kind):
    return lax.dot_general(a, b, _DN[kind], precision=HIGHEST, preferred_element_type=F32)


def _iota(shape, dim):
    return lax.broadcasted_iota(jnp.int32, shape, dim)


def _row_mask(i, tm):
    return ((i * tm + _iota((tm, 1), 0)) >= FIRST_ROW).astype(F32)


def _log_sigmoid(z):
    return jnp.minimum(z, 0.0) - jnp.log(1.0 + jnp.exp(-jnp.abs(z)))


def _softplus(z):
    return jnp.maximum(z, 0.0) + jnp.log(1.0 + jnp.exp(-jnp.abs(z)))


def _silu(z):
    return z * jax.nn.sigmoid(z)


def _lane_col(x, lane):
    return jnp.sum(jnp.where(_iota(x.shape, 1) == lane, x, 0.0), axis=1, keepdims=True)


def _last_row(x):
    return jnp.sum(jnp.where(_iota(x.shape, 0) == x.shape[0] - 1, x, 0.0), axis=0, keepdims=True)


def _params(sem, limit=VMEM_LIMIT):
    return pltpu.CompilerParams(dimension_semantics=sem, vmem_limit_bytes=limit)


def _pick(n, options):
    for t in options:
        if n % t == 0:
            return t
    return n


def _matmul(a, b, kind, out_dtype, name, add=None):
    if kind == "nn":
        (m, k), n = a.shape, b.shape[1]
    elif kind == "nt":
        (m, k), n = a.shape, b.shape[0]
    else:
        (k, m), n = a.shape, b.shape[1]
    tn = _pick(n, (512, 384, 640, 256, 128))
    tm_options = (1408, 384, 128) if kind != "tn" else (256, 128)
    tm = None
    for t in tm_options:
        if m % t:
            continue
        need = 2 * (t * k * a.dtype.itemsize + tn * k * b.dtype.itemsize + t * tn * 8)
        if need <= VMEM_BLOCK_BUDGET:
            tm = t
            break
    assert tm is not None, (name, a.shape, b.shape)

    def body(*refs):
        a_ref, b_ref = refs[0], refs[1]
        o_ref = refs[-1]
        acc = _dot(a_ref[...], b_ref[...], kind)
        if add is not None:
            acc = acc + refs[2][...]
        o_ref[...] = acc.astype(o_ref.dtype)

    if kind == "tn":
        a_spec = pl.BlockSpec((k, tm), lambda i, j: (0, i))
    else:
        a_spec = pl.BlockSpec((tm, k), lambda i, j: (i, 0))
    if kind == "nt":
        b_spec = pl.BlockSpec((tn, k), lambda i, j: (j, 0))
    else:
        b_spec = pl.BlockSpec((k, tn), lambda i, j: (0, j))
    o_spec = pl.BlockSpec((tm, tn), lambda i, j: (i, j))
    operands, in_specs = [a, b], [a_spec, b_spec]
    if add is not None:
        operands.append(add)
        in_specs.append(o_spec)
    return pl.pallas_call(
        body, grid=(m // tm, n // tn), in_specs=in_specs, out_specs=o_spec,
        out_shape=jax.ShapeDtypeStruct((m, n), out_dtype), name=name,
        compiler_params=_params(("parallel", "parallel")),
    )(*operands)


def _x_spec(tm, width, cb0, moves):
    return pl.BlockSpec((tm, width), lambda i, j: (i, cb0 + j * moves))


def _full_spec(arr):
    nd = arr.ndim
    return pl.BlockSpec(arr.shape, lambda i, j: (0,) * nd)


def _row_fwd(fn, xs, ps, cs, outs, name, tm=ROW_TILE, ncol=1, rows=None):
    rows = ROWS if rows is None else rows
    nx, npar, nc = len(xs), len(ps), len(cs)

    def body(*refs):
        i = pl.program_id(0)
        xv = [r[...].astype(F32) for r in refs[:nx]]
        pv = [r[...] for r in refs[nx:nx + npar + nc]]
        res = fn(i, *xv, *pv)
        for r, val in zip(refs[nx + npar + nc:], res):
            r[...] = val.astype(r.dtype)

    in_specs = [_x_spec(tm, w, cb0, mv) for (_, w, cb0, mv) in xs]
    in_specs += [_full_spec(p) for p in list(ps) + list(cs)]
    out_specs = [_x_spec(tm, w, 0, 1) for (w, _, _) in outs]
    out_shape = [jax.ShapeDtypeStruct((rows, tot), dt) for (_, tot, dt) in outs]
    return pl.pallas_call(
        body, grid=(rows // tm, ncol), in_specs=in_specs, out_specs=out_specs, out_shape=out_shape,
        name=name, compiler_params=_params(("parallel", "parallel")),
    )(*[x[0] for x in xs], *ps, *cs)


def _row_bwd(fn, xs, ps, cs, cts, dxs, name, adds=None, tm=ROW_TILE, ncol=1, rows=None):
    rows = ROWS if rows is None else rows
    nx, npar, nc, nct = len(xs), len(ps), len(cs), len(cts)
    adds = [None] * nx if adds is None else adds
    add_idx = [k for k in range(nx) if adds[k] is not None]

    def body(*refs):
        i, j = pl.program_id(0), pl.program_id(1)
        pos = 0
        x_refs = refs[pos:pos + nx]; pos += nx
        p_refs = refs[pos:pos + npar]; pos += npar
        c_refs = refs[pos:pos + nc]; pos += nc
        ct_refs = refs[pos:pos + nct]; pos += nct
        add_refs = refs[pos:pos + len(add_idx)]; pos += len(add_idx)
        dx_refs = refs[pos:pos + nx]; pos += nx
        dp_refs = refs[pos:pos + npar]
        xv = [r[...].astype(F32) for r in x_refs]
        pv = [r[...] for r in p_refs]
        cv = [r[...] for r in c_refs]
        _, vjp = jax.vjp(lambda *args: fn(i, *args, *cv), *xv, *pv)
        grads = vjp(tuple(r[...].astype(F32) for r in ct_refs))
        for k in range(nx):
            val = grads[k]
            if adds[k] is not None:
                val = val + add_refs[add_idx.index(k)][...]
            dx_refs[k][...] = val.astype(dx_refs[k].dtype)
        if npar:
            @pl.when((i == 0) & (j == 0))
            def _():
                for r in dp_refs:
                    r[...] = jnp.zeros(r.shape, r.dtype)

            for k in range(npar):
                dp_refs[k][...] += grads[nx + k]

    in_specs = [_x_spec(tm, w, cb0, mv) for (_, w, cb0, mv) in xs]
    in_specs += [_full_spec(p) for p in list(ps) + list(cs)]
    in_specs += [_x_spec(tm, w, cb0, mv) for (_, w, cb0, mv) in cts]
    in_specs += [_x_spec(tm, xs[k][1], 0, 1) for k in add_idx]
    out_specs = [_x_spec(tm, xs[k][1], 0, 1) for k in range(nx)] + [_full_spec(p) for p in ps]
    out_shape = [jax.ShapeDtypeStruct((rows, tot), dt) for (tot, dt) in dxs]
    out_shape += [jax.ShapeDtypeStruct(p.shape, F32) for p in ps]
    sem = ("arbitrary", "arbitrary") if npar else ("parallel", "parallel")
    return pl.pallas_call(
        body, grid=(rows // tm, ncol), in_specs=in_specs, out_specs=out_specs, out_shape=out_shape,
        name=name, compiler_params=_params(sem),
    )(*[x[0] for x in xs], *ps, *cs, *[c[0] for c in cts], *[adds[k] for k in add_idx])


def _rms_fn(i, h, g):
    return (h * lax.rsqrt(jnp.mean(h * h, axis=1, keepdims=True) + NORM_EPS) * g,)


def _swiglu_fn(i, gt, up):
    return (_silu(gt) * up,)


def _fox_pre_fn(i, q, k, f, qg, kg, bf, e_mean, e_expand):
    tm = q.shape[0]

    def headnorm(x, gain, scale):
        msq = _dot_hi(x * x, e_mean, "nn")
        r = _dot_hi(lax.rsqrt(msq + NORM_EPS), e_expand, "nn")
        return x * r * gain * scale

    qn = headnorm(q, qg, FOX_DH ** -0.5)
    kn = headnorm(k, kg, 1.0)
    keep = (_iota((tm, LANE), 1) < FOX_HEADS).astype(F32) * _row_mask(i, tm)
    return qn, kn, _log_sigmoid(f + bf) * keep


def _fox_post_fn(i, o, gate):
    return (o * jax.nn.sigmoid(gate) * _row_mask(i, o.shape[0]),)


def _gla_pre_fn(i, alr, wa, ba):
    z = _dot_ad(alr, wa, "nn") + ba
    return (_log_sigmoid(z) * (1.0 / 16.0) * _row_mask(i, alr.shape[0]),)


def _headwise_norm_gate(o, gate, gain, dv, mask):
    pieces = []
    for h in range(o.shape[1] // dv):
        oh = o[:, h * dv:(h + 1) * dv]
        pieces.append(oh * lax.rsqrt(jnp.mean(oh * oh, axis=1, keepdims=True) + NORM_EPS) * gain)
    return jnp.concatenate(pieces, axis=1) * _silu(gate) * mask


def _gla_post_fn(i, o, r, gain):
    return (_headwise_norm_gate(o, r, gain, GLA_DV, _row_mask(i, o.shape[0])),)


def _gdn_post_fn(i, o, gate, gain):
    return (_headwise_norm_gate(o, gate, gain, GDN_DV, _row_mask(i, o.shape[0])),)


def _gdn_gates_fn(i, sm, alog, dtb):
    tm = sm.shape[0]
    lane = _iota((tm, LANE), 1)
    g = -jnp.exp(alog) * _softplus(sm + dtb)
    beta = jax.nn.sigmoid(sm)
    out = jnp.where(lane < GDN_HEADS, g, jnp.where(lane < 2 * GDN_HEADS, beta, 0.0))
    return (out * _row_mask(i, tm),)


def _gdn_conv_fn(kind):
    def fn(i, prev, cur, w):
        tm = cur.shape[0]
        xw = jnp.concatenate([prev * jnp.where(i > 0, 1.0, 0.0), cur], axis=0)
        t = _iota((tm, 2 * tm), 0)
        s = _iota((tm, 2 * tm), 1)
        y = jnp.zeros(cur.shape, F32)
        for j in range(GDN_CONV):
            sel = (s == t + (tm - (GDN_CONV - 1) + j)).astype(F32)
            y = y + _dot_hi(sel, xw, "nn") * jnp.sum(jnp.where(_iota(w.shape, 0) == j, w, 0.0), axis=0, keepdims=True)
        a = _silu(y * _row_mask(i, tm))
        if kind == "v":
            return (a,)
        scale = GDN_DK ** -0.5 if kind == "q" else 1.0
        pieces = []
        for h in range(GDN_HEADS):
            ah = a[:, h * GDN_DK:(h + 1) * GDN_DK]
            pieces.append(ah * lax.rsqrt(jnp.sum(ah * ah, axis=1, keepdims=True) + NORM_EPS) * scale)
        return (jnp.concatenate(pieces, axis=1),)

    return fn


def _seq_cumsum(x, reverse, name):
    rows, width = x.shape
    nt = rows // ROW_TILE

    def body(x_ref, o_ref, carry):
        @pl.when(pl.program_id(0) == 0)
        def _():
            carry[...] = jnp.zeros(carry.shape, F32)

        r = _iota((ROW_TILE, ROW_TILE), 0)
        c = _iota((ROW_TILE, ROW_TILE), 1)
        tri = ((c >= r) if reverse else (c <= r)).astype(F32)
        acc = _dot_hi(tri, x_ref[...], "nn") + carry[...]
        o_ref[...] = acc
        edge = 0 if reverse else ROW_TILE - 1
        carry[...] = jnp.sum(jnp.where(_iota(acc.shape, 0) == edge, acc, 0.0), axis=0, keepdims=True)

    idx = (lambda i: (nt - 1 - i, 0)) if reverse else (lambda i: (i, 0))
    return pl.pallas_call(
        body, grid=(nt,), in_specs=[pl.BlockSpec((ROW_TILE, width), idx)],
        out_specs=pl.BlockSpec((ROW_TILE, width), idx), out_shape=jax.ShapeDtypeStruct(x.shape, F32),
        scratch_shapes=[pltpu.VMEM((1, width), F32)], name=name, compiler_params=_params(("arbitrary",)),
    )(x)


FOX_PAIRS = FOX_HEADS // 2
FOX_V_BLOCK = 2 * D_MODEL // LANE


def _pair_cols(x):
    return _lane_col(x, 0), _lane_col(x, FOX_DH)


FOX_BIG = 3
FOX_BIG_ROWS = FOX_BIG * ROW_TILE
assert ROWS % FOX_BIG_ROWS == 0
KEY_OFF = 1e30


def _fox_scores(qa, qb, kj, ca, cb, cra, crb, mask):
    sa = _dot(qa, kj, "nt") + (ca - cra)
    sb = _dot(qb, kj, "nt") + (cb - crb)
    if mask is None:
        return sa, sb
    return jnp.where(mask, sa, NEG), jnp.where(mask, sb, NEG)


def _fox_mask(i, j):
    rows = i * ROW_TILE + _iota((ROW_TILE, ROW_TILE), 0)
    cols = j * ROW_TILE + _iota((ROW_TILE, ROW_TILE), 1)
    return cols <= rows


def _fox_attn_fwd(qn, kn, proj, c2, crow):
    nt = ROWS // ROW_TILE

    def body(q_ref, k_ref, v_ref, c2_ref, crow_ref, o_ref, lse_ref):
        i = pl.program_id(1)
        q = q_ref[...]
        is_a = _iota((ROW_TILE, LANE), 1) < FOX_DH
        qa, qb = jnp.where(is_a, q, 0), jnp.where(is_a, 0, q)
        ca, cb = _pair_cols(c2_ref[0])

        def block(off, size, mask, carry):
            ma, la, acca, mb, lb, accb = carry
            kj = k_ref[pl.ds(off, size), :]
            vj = v_ref[pl.ds(off, size), :].astype(BF16)
            sa, sb = _fox_scores(qa, qb, kj, ca, cb, crow_ref[0, 0:1, pl.ds(off, size)],
                                 crow_ref[0, 1:2, pl.ds(off, size)], mask)

            def online(s, m, l, acc):
                m_new = jnp.maximum(m, jnp.max(s, axis=1, keepdims=True))
                p = jnp.exp(s - m_new)
                alpha = jnp.exp(m - m_new)
                return m_new, alpha * l + jnp.sum(p, axis=1, keepdims=True), alpha * acc + _dot(p, vj, "nn")

            return online(sa, ma, la, acca) + online(sb, mb, lb, accb)

        def big(b, carry):
            return block(pl.multiple_of(b * FOX_BIG_ROWS, FOX_BIG_ROWS), FOX_BIG_ROWS, None, carry)

        def small(j, carry):
            return block(pl.multiple_of(j * ROW_TILE, ROW_TILE), ROW_TILE, _fox_mask(i, j), carry)

        m0 = jnp.full((ROW_TILE, 1), NEG, F32)
        l0 = jnp.zeros((ROW_TILE, 1), F32)
        a0 = jnp.zeros((ROW_TILE, LANE), F32)
        n_big = i // FOX_BIG
        carry = lax.fori_loop(0, n_big, big, (m0, l0, a0, m0, l0, a0))
        ma, la, acca, mb, lb, accb = lax.fori_loop(n_big * FOX_BIG, i + 1, small, carry)
        o_ref[...] = jnp.where(is_a, acca / la, accb / lb)
        lse_ref[0] = jnp.where(is_a, ma + jnp.log(la), mb + jnp.log(lb))

    return pl.pallas_call(
        body, grid=(FOX_PAIRS, nt),
        in_specs=[
            pl.BlockSpec((ROW_TILE, LANE), lambda g, i: (i, g)),
            pl.BlockSpec((ROWS, LANE), lambda g, i: (0, g)),
            pl.BlockSpec((ROWS, LANE), lambda g, i: (0, FOX_V_BLOCK + g)),
            pl.BlockSpec((1, ROW_TILE, LANE), lambda g, i: (g, i, 0)),
            pl.BlockSpec((1, 2, ROWS), lambda g, i: (g, 0, 0)),
        ],
        out_specs=[
            pl.BlockSpec((ROW_TILE, LANE), lambda g, i: (i, g)),
            pl.BlockSpec((1, ROW_TILE, LANE), lambda g, i: (g, i, 0)),
        ],
        out_shape=[jax.ShapeDtypeStruct((ROWS, D_MODEL), F32), jax.ShapeDtypeStruct((FOX_PAIRS, ROWS, LANE), F32)],
        name="fox_attn_fwd", compiler_params=_params(("parallel", "arbitrary")),
    )(qn, kn, proj, c2, crow)


def _fox_attn_dq(qn, kn, proj, c2, crow, lse2, o, do, blockdiag):
    nt = ROWS // ROW_TILE

    def body(q_ref, k_ref, v_ref, c2_ref, crow_ref, lse_ref, o_ref, do_ref, bd_ref, dq_ref, delta_ref, dcq_ref):
        i = pl.program_id(1)
        q = q_ref[...]
        is_a = _iota((ROW_TILE, LANE), 1) < FOX_DH
        qa, qb = jnp.where(is_a, q, 0), jnp.where(is_a, 0, q)
        ca, cb = _pair_cols(c2_ref[0])
        lsa, lsb = _pair_cols(lse_ref[0])
        do_t = do_ref[...]
        delta2 = _dot_hi(do_t * o_ref[...], bd_ref[...], "nn")
        delta_ref[0] = delta2
        dla, dlb = _pair_cols(delta2)
        doa, dob = jnp.where(is_a, do_t, 0.0).astype(BF16), jnp.where(is_a, 0.0, do_t).astype(BF16)

        def block(off, size, mask, carry):
            dqa, dqb, rsa, rsb = carry
            kj = k_ref[pl.ds(off, size), :]
            vj = v_ref[pl.ds(off, size), :].astype(BF16)
            sa, sb = _fox_scores(qa, qb, kj, ca, cb, crow_ref[0, 0:1, pl.ds(off, size)],
                                 crow_ref[0, 1:2, pl.ds(off, size)], mask)
            dsa = jnp.exp(sa - lsa) * (_dot(doa, vj, "nt") - dla)
            dsb = jnp.exp(sb - lsb) * (_dot(dob, vj, "nt") - dlb)
            return (dqa + _dot(dsa, kj, "nn"), dqb + _dot(dsb, kj, "nn"),
                    rsa + jnp.sum(dsa, axis=1, keepdims=True), rsb + jnp.sum(dsb, axis=1, keepdims=True))

        def big(b, carry):
            return block(pl.multiple_of(b * FOX_BIG_ROWS, FOX_BIG_ROWS), FOX_BIG_ROWS, None, carry)

        def small(j, carry):
            return block(pl.multiple_of(j * ROW_TILE, ROW_TILE), ROW_TILE, _fox_mask(i, j), carry)

        z = jnp.zeros((ROW_TILE, LANE), F32)
        zc = jnp.zeros((ROW_TILE, 1), F32)
        n_big = i // FOX_BIG
        carry = lax.fori_loop(0, n_big, big, (z, z, zc, zc))
        dqa, dqb, rsa, rsb = lax.fori_loop(n_big * FOX_BIG, i + 1, small, carry)
        dq_ref[...] = jnp.where(is_a, dqa, dqb)
        dcq_ref[0] = jnp.where(is_a, rsa, rsb)

    tile = pl.BlockSpec((ROW_TILE, LANE), lambda g, i: (i, g))
    pair_tile = pl.BlockSpec((1, ROW_TILE, LANE), lambda g, i: (g, i, 0))
    return pl.pallas_call(
        body, grid=(FOX_PAIRS, nt),
        in_specs=[
            tile,
            pl.BlockSpec((ROWS, LANE), lambda g, i: (0, g)),
            pl.BlockSpec((ROWS, LANE), lambda g, i: (0, FOX_V_BLOCK + g)),
            pair_tile,
            pl.BlockSpec((1, 2, ROWS), lambda g, i: (g, 0, 0)),
            pair_tile, tile, tile,
            pl.BlockSpec((LANE, LANE), lambda g, i: (0, 0)),
        ],
        out_specs=[tile, pair_tile, pair_tile],
        out_shape=[jax.ShapeDtypeStruct((ROWS, D_MODEL), F32), jax.ShapeDtypeStruct((FOX_PAIRS, ROWS, LANE), F32),
                   jax.ShapeDtypeStruct((FOX_PAIRS, ROWS, LANE), F32)],
        name="fox_attn_dq", compiler_params=_params(("parallel", "arbitrary")),
    )(qn, kn, proj, c2, crow, lse2, o, do, blockdiag)


def _fox_attn_dkv(qn, kn, proj, c2, crow, lse2, delta2, do):
    nt = ROWS // ROW_TILE

    def body(q_ref, k_ref, v_ref, c2_ref, crow_ref, lse_ref, delta_ref, do_ref, dk_ref, dv_ref, dc_ref):
        j = pl.program_id(1)
        kj = k_ref[...]
        vj = v_ref[...].astype(BF16)
        cra, crb = crow_ref[0, 0:1, :], crow_ref[0, 1:2, :]

        def block(off, size, mask, carry):
            dk, dv, dca, dcb = carry
            is_a = _iota((size, LANE), 1) < FOX_DH
            q = q_ref[pl.ds(off, size), :]
            qa, qb = jnp.where(is_a, q, 0), jnp.where(is_a, 0, q)
            do_t = do_ref[pl.ds(off, size), :]
            doa, dob = jnp.where(is_a, do_t, 0.0).astype(BF16), jnp.where(is_a, 0.0, do_t).astype(BF16)
            ca, cb = _pair_cols(c2_ref[0, pl.ds(off, size), :])
            lsa, lsb = _pair_cols(lse_ref[0, pl.ds(off, size), :])
            dla, dlb = _pair_cols(delta_ref[0, pl.ds(off, size), :])
            sa, sb = _fox_scores(qa, qb, kj, ca, cb, cra, crb, mask)
            pa, pb = jnp.exp(sa - lsa), jnp.exp(sb - lsb)
            dsa = pa * (_dot(doa, vj, "nt") - dla)
            dsb = pb * (_dot(dob, vj, "nt") - dlb)
            dv = dv + _dot(pa, doa, "tn") + _dot(pb, dob, "tn")
            dk = dk + _dot(dsa, qa, "tn") + _dot(dsb, qb, "tn")
            dca = dca - jnp.sum(dsa, axis=0, keepdims=True)
            dcb = dcb - jnp.sum(dsb, axis=0, keepdims=True)
            return dk, dv, dca, dcb

        def small(i, carry):
            return block(pl.multiple_of(i * ROW_TILE, ROW_TILE), ROW_TILE, _fox_mask(i, j), carry)

        def big(b, carry):
            return block(pl.multiple_of(b * FOX_BIG_ROWS, FOX_BIG_ROWS), FOX_BIG_ROWS, None, carry)

        z = jnp.zeros((ROW_TILE, LANE), F32)
        zr = jnp.zeros((1, ROW_TILE), F32)
        first_big = j // FOX_BIG + 1
        carry = lax.fori_loop(j, first_big * FOX_BIG, small, (z, z, zr, zr))
        dk, dv, dca, dcb = lax.fori_loop(first_big, nt // FOX_BIG, big, carry)
        dk_ref[...] = dk
        dv_ref[...] = dv.astype(dv_ref.dtype)
        dc_ref[0, 0:1, :] = dca
        dc_ref[0, 1:2, :] = dcb

    whole = pl.BlockSpec((ROWS, LANE), lambda g, j: (0, g))
    pair_whole = pl.BlockSpec((1, ROWS, LANE), lambda g, j: (g, 0, 0))
    tile = pl.BlockSpec((ROW_TILE, LANE), lambda g, j: (j, g))
    return pl.pallas_call(
        body, grid=(FOX_PAIRS, nt),
        in_specs=[
            whole, tile,
            pl.BlockSpec((ROW_TILE, LANE), lambda g, j: (j, FOX_V_BLOCK + g)),
            pair_whole,
            pl.BlockSpec((1, 2, ROW_TILE), lambda g, j: (g, 0, j)),
            pair_whole, pair_whole, whole,
        ],
        out_specs=[tile, tile, pl.BlockSpec((1, 2, ROW_TILE), lambda g, j: (g, 0, j))],
        out_shape=[jax.ShapeDtypeStruct((ROWS, D_MODEL), F32), jax.ShapeDtypeStruct((ROWS, D_MODEL), BF16),
                   jax.ShapeDtypeStruct((FOX_PAIRS, 2, ROWS), F32)],
        name="fox_attn_dkv", compiler_params=_params(("parallel", "arbitrary")),
    )(qn, kn, proj, c2, crow, lse2, delta2, do)


def _gla_chunk(q, k, v, g, st, dot):
    c = q.shape[0]
    incl = _iota((c, c), 1) <= _iota((c, c), 0)
    b = _dot_hi(incl.astype(F32), g, "nn")
    qe = q * (GLA_DK ** -0.5) * jnp.exp(b)
    ke = k * jnp.exp(-b)
    att = jnp.where(incl, dot(qe, ke, "nt"), 0.0)
    o = dot(att, v, "nn") + dot(qe, st, "nt")
    b_last = _last_row(b)
    kd = k * jnp.exp(b_last - b)
    return o, st * jnp.exp(b_last) + dot(v, kd, "tn")


def _tri_inverse(low):
    c = low.shape[0]
    eye = (_iota((c, c), 0) == _iota((c, c), 1)).astype(F32)
    x, p = eye - low, low
    span = 2
    while span < c:
        p = _dot_hi(p, p, "nn")
        x = _dot_hi(x, eye + p, "nn")
        span *= 2
    return x


def _solve(low, rhs):
    return _dot_hi(_tri_inverse(low), rhs, "nn")


@jax.custom_vjp
def _solve_ad(low, rhs):
    return _solve(low, rhs)


def _solve_ad_fwd(low, rhs):
    inv = _tri_inverse(low)
    sol = _dot_hi(inv, rhs, "nn")
    return sol, (inv, sol)


def _solve_ad_bwd(res, ct):
    inv, sol = res
    drhs = _dot_hi(inv, ct, "tn")
    return -_dot_hi(drhs, sol, "nt"), drhs


_solve_ad.defvjp(_solve_ad_fwd, _solve_ad_bwd)


def _gdn_chunk(q, k, v, gates, st, h, dot, solve):
    c = q.shape[0]
    gcol = _lane_col(gates, h)
    beta = _lane_col(gates, h + GDN_HEADS)
    incl = _iota((c, c), 1) <= _iota((c, c), 0)
    strict = _iota((c, c), 1) < _iota((c, c), 0)
    incl_f = incl.astype(F32)
    b = _dot_hi(incl_f, jnp.broadcast_to(gcol, (c, LANE)), "nn")
    g_sq = jnp.broadcast_to(gcol, (c, c))
    b_col = _dot_hi(incl_f, g_sq, "nn")
    b_row = _dot_hi(g_sq, (_iota((c, c), 0) <= _iota((c, c), 1)).astype(F32), "tn")
    diff = b_col - b_row
    kb, vb = k * beta, v * beta
    low = dot(kb, k, "nt") * jnp.exp(jnp.where(strict, diff, NEG))
    eb = jnp.exp(b)
    sol = solve(low, jnp.concatenate([vb, kb * eb], axis=1))
    v_new = sol[:, :GDN_DV] - dot(sol[:, GDN_DV:], st, "nn")
    att = dot(q, k, "nt") * jnp.exp(jnp.where(incl, diff, NEG))
    o = dot(att, v_new, "nn") + dot(q * eb, st, "nn")
    b_last = _last_row(b)
    kd = k * jnp.exp(b_last - b)
    return o, st * jnp.exp(b_last) + dot(kd, v_new, "tn")


N_CHUNKS = ROWS // CHUNK


def _gla_scan_fwd(proj, garr):
    def body(q_ref, k_ref, v_ref, g_ref, o_ref, states_ref, st):
        @pl.when(pl.program_id(1) == 0)
        def _():
            st[...] = jnp.zeros(st.shape, F32)

        s_in = st[...]
        states_ref[0, 0] = s_in
        o, s_out = _gla_chunk(q_ref[...], k_ref[...], v_ref[...], g_ref[...], s_in, _dot)
        o_ref[...] = o
        st[...] = s_out

    qk = GLA_QK // GLA_DK
    return pl.pallas_call(
        body, grid=(GLA_HEADS, N_CHUNKS),
        in_specs=[
            pl.BlockSpec((CHUNK, GLA_DK), lambda h, c: (c, h)),
            pl.BlockSpec((CHUNK, GLA_DK), lambda h, c: (c, qk + h)),
            pl.BlockSpec((CHUNK, GLA_DV), lambda h, c: (c, qk + h)),
            pl.BlockSpec((CHUNK, GLA_DK), lambda h, c: (c, h)),
        ],
        out_specs=[
            pl.BlockSpec((CHUNK, GLA_DV), lambda h, c: (c, h)),
            pl.BlockSpec((1, 1, GLA_DV, GLA_DK), lambda h, c: (h, c, 0, 0)),
        ],
        out_shape=[jax.ShapeDtypeStruct((ROWS, GLA_V), F32),
                   jax.ShapeDtypeStruct((GLA_HEADS, N_CHUNKS, GLA_DV, GLA_DK), F32)],
        scratch_shapes=[pltpu.VMEM((GLA_DV, GLA_DK), F32)],
        name="gla_scan_fwd", compiler_params=_params(("parallel", "arbitrary")),
    )(proj, proj, proj, garr)


def _gla_scan_bwd(proj, garr, states, do):
    last = N_CHUNKS - 1

    def body(q_ref, k_ref, v_ref, g_ref, s_ref, do_ref, dq_ref, dk_ref, dv_ref, dg_ref, dst):
        @pl.when(pl.program_id(1) == 0)
        def _():
            dst[...] = jnp.zeros(dst.shape, F32)

        _, vjp = jax.vjp(lambda q, k, v, g, s: _gla_chunk(q, k, v, g, s, _dot_ad),
                         q_ref[...], k_ref[...], v_ref[...], g_ref[...], s_ref[0, 0])
        dq, dk, dv, dg, ds = vjp((do_ref[...], dst[...]))
        dq_ref[...] = dq.astype(dq_ref.dtype)
        dk_ref[...] = dk.astype(dk_ref.dtype)
        dv_ref[...] = dv.astype(dv_ref.dtype)
        dg_ref[...] = dg
        dst[...] = ds

    qk = GLA_QK // GLA_DK
    kspec = pl.BlockSpec((CHUNK, GLA_DK), lambda h, c: (last - c, h))
    vspec = pl.BlockSpec((CHUNK, GLA_DV), lambda h, c: (last - c, h))
    return pl.pallas_call(
        body, grid=(GLA_HEADS, N_CHUNKS),
        in_specs=[
            kspec,
            pl.BlockSpec((CHUNK, GLA_DK), lambda h, c: (last - c, qk + h)),
            pl.BlockSpec((CHUNK, GLA_DV), lambda h, c: (last - c, qk + h)),
            kspec,
            pl.BlockSpec((1, 1, GLA_DV, GLA_DK), lambda h, c: (h, last - c, 0, 0)),
            vspec,
        ],
        out_specs=[kspec, kspec, vspec, kspec],
        out_shape=[jax.ShapeDtypeStruct((ROWS, GLA_QK), BF16), jax.ShapeDtypeStruct((ROWS, GLA_QK), BF16),
                   jax.ShapeDtypeStruct((ROWS, GLA_V), BF16), jax.ShapeDtypeStruct((ROWS, GLA_QK), F32)],
        scratch_shapes=[pltpu.VMEM((GLA_DV, GLA_DK), F32)],
        name="gla_scan_bwd", compiler_params=_params(("parallel", "arbitrary")),
    )(proj, proj, proj, garr, states, do)


def _gdn_scan_fwd(qn, kn, vn, gates):
    def body(q_ref, k_ref, v_ref, g_ref, o_ref, states_ref, st):
        h = pl.program_id(1)

        @pl.when(pl.program_id(0) == 0)
        def _():
            st[h] = jnp.zeros((GDN_DK, GDN_DV), F32)

        s_in = st[h]
        states_ref[0, 0] = s_in
        o, s_out = _gdn_chunk(q_ref[...], k_ref[...], v_ref[...], g_ref[...], s_in, h, _dot, _solve)
        o_ref[...] = o
        st[h] = s_out

    hspec = pl.BlockSpec((CHUNK, GDN_DK), lambda c, h: (c, h))
    return pl.pallas_call(
        body, grid=(N_CHUNKS, GDN_HEADS),
        in_specs=[hspec, hspec, hspec, pl.BlockSpec((CHUNK, LANE), lambda c, h: (c, 0))],
        out_specs=[hspec, pl.BlockSpec((1, 1, GDN_DK, GDN_DV), lambda c, h: (h, c, 0, 0))],
        out_shape=[jax.ShapeDtypeStruct((ROWS, D_MODEL), F32),
                   jax.ShapeDtypeStruct((GDN_HEADS, N_CHUNKS, GDN_DK, GDN_DV), F32)],
        scratch_shapes=[pltpu.VMEM((GDN_HEADS, GDN_DK, GDN_DV), F32)],
        name="gdn_scan_fwd", compiler_params=_params(("arbitrary", "arbitrary")),
    )(qn, kn, vn, gates)


def _gdn_scan_bwd(qn, kn, vn, gates, states, do):
    last = N_CHUNKS - 1

    def body(q_ref, k_ref, v_ref, g_ref, s_ref, do_ref, dq_ref, dk_ref, dv_ref, dg_ref, dst):
        h = pl.program_id(1)

        @pl.when(pl.program_id(0) == 0)
        def _():
            dst[h] = jnp.zeros((GDN_DK, GDN_DV), F32)

        @pl.when(h == 0)
        def _():
            dg_ref[...] = jnp.zeros(dg_ref.shape, F32)

        _, vjp = jax.vjp(lambda q, k, v, g, s: _gdn_chunk(q, k, v, g, s, h, _dot_ad, _solve_ad),
                         q_ref[...], k_ref[...], v_ref[...], g_ref[...], s_ref[0, 0])
        dq, dk, dv, dg, ds = vjp((do_ref[...], dst[h]))
        dq_ref[...] = dq
        dk_ref[...] = dk
        dv_ref[...] = dv
        dg_ref[...] += dg
        dst[h] = ds

    hspec = pl.BlockSpec((CHUNK, GDN_DK), lambda c, h: (last - c, h))
    gspec = pl.BlockSpec((CHUNK, LANE), lambda c, h: (last - c, 0))
    return pl.pallas_call(
        body, grid=(N_CHUNKS, GDN_HEADS),
        in_specs=[hspec, hspec, hspec, gspec,
                  pl.BlockSpec((1, 1, GDN_DK, GDN_DV), lambda c, h: (h, last - c, 0, 0)), hspec],
        out_specs=[hspec, hspec, hspec, gspec],
        out_shape=[jax.ShapeDtypeStruct((ROWS, D_MODEL), F32)] * 3 + [jax.ShapeDtypeStruct((ROWS, LANE), F32)],
        scratch_shapes=[pltpu.VMEM((GDN_HEADS, GDN_DK, GDN_DV), F32)],
        name="gdn_scan_bwd", compiler_params=_params(("arbitrary", "arbitrary")),
    )(qn, kn, vn, gates, states, do)


def _gdn_conv_fwd(proj, w, kind, cb):
    fn = _gdn_conv_fn(kind)
    nt = ROWS // ROW_TILE

    def body(prev_ref, cur_ref, w_ref, o_ref):
        (out,) = fn(pl.program_id(0), prev_ref[...], cur_ref[...], w_ref[...])
        o_ref[...] = out

    return pl.pallas_call(
        body, grid=(nt,),
        in_specs=[pl.BlockSpec((ROW_TILE, D_MODEL), lambda i: (jnp.maximum(i - 1, 0), cb)),
                  pl.BlockSpec((ROW_TILE, D_MODEL), lambda i: (i, cb)),
                  pl.BlockSpec(w.shape, lambda i: (0, 0))],
        out_specs=pl.BlockSpec((ROW_TILE, D_MODEL), lambda i: (i, 0)),
        out_shape=jax.ShapeDtypeStruct((ROWS, D_MODEL), F32),
        name="gdn_conv_fwd_" + kind, compiler_params=_params(("parallel",)),
    )(proj, proj, w)


def _gdn_conv_bwd(proj, w, kind, cb, ct):
    fn = _gdn_conv_fn(kind)
    nt = ROWS // ROW_TILE

    def body(prev_ref, cur_ref, w_ref, ct_ref, dx_ref, dw_ref, carry):
        step = pl.program_id(0)
        i = nt - 1 - step

        @pl.when(step == 0)
        def _():
            carry[...] = jnp.zeros(carry.shape, F32)
            dw_ref[...] = jnp.zeros(dw_ref.shape, F32)

        _, vjp = jax.vjp(lambda p, c, ww: fn(i, p, c, ww), prev_ref[...], cur_ref[...], w_ref[...])
        dprev, dcur, dw = vjp((ct_ref[...],))
        dx_ref[...] = (dcur + carry[...]).astype(dx_ref.dtype)
        carry[...] = dprev
        dw_ref[...] += dw

    return pl.pallas_call(
        body, grid=(nt,),
        in_specs=[pl.BlockSpec((ROW_TILE, D_MODEL), lambda s: (jnp.maximum(nt - 2 - s, 0), cb)),
                  pl.BlockSpec((ROW_TILE, D_MODEL), lambda s: (nt - 1 - s, cb)),
                  pl.BlockSpec(w.shape, lambda s: (0, 0)),
                  pl.BlockSpec((ROW_TILE, D_MODEL), lambda s: (nt - 1 - s, 0))],
        out_specs=[pl.BlockSpec((ROW_TILE, D_MODEL), lambda s: (nt - 1 - s, 0)),
                   pl.BlockSpec(w.shape, lambda s: (0, 0))],
        out_shape=[jax.ShapeDtypeStruct((ROWS, D_MODEL), BF16), jax.ShapeDtypeStruct(w.shape, F32)],
        scratch_shapes=[pltpu.VMEM((ROW_TILE, D_MODEL), F32)],
        name="gdn_conv_bwd_" + kind, compiler_params=_params(("arbitrary",)),
    )(proj, proj, w, ct)


def _loss_head(h, target):
    nt = ROWS // ROW_TILE
    lead = PAD_ROWS // ROW_TILE

    def body(h_ref, t_ref, dh_ref, acc_ref):
        i = pl.program_id(0)

        @pl.when(i == 0)
        def _():
            acc_ref[...] = jnp.zeros(acc_ref.shape, F32)

        err = (h_ref[...] - t_ref[...]) * jnp.where(i >= lead, 1.0, 0.0)
        dh_ref[...] = err * (1.0 / D_MODEL)
        acc_ref[...] += jnp.sum(err * err, axis=0, keepdims=True)

    return pl.pallas_call(
        body, grid=(nt,),
        in_specs=[pl.BlockSpec((ROW_TILE, D_MODEL), lambda i: (i, 0)),
                  pl.BlockSpec((ROW_TILE, D_MODEL), lambda i: (jnp.maximum(i - lead, 0), 0))],
        out_specs=[pl.BlockSpec((ROW_TILE, D_MODEL), lambda i: (i, 0)),
                   pl.BlockSpec((1, D_MODEL), lambda i: (0, 0))],
        out_shape=[jax.ShapeDtypeStruct((ROWS, D_MODEL), F32), jax.ShapeDtypeStruct((1, D_MODEL), F32)],
        name="loss_head", compiler_params=_params(("arbitrary",)),
    )(h, target)


MESH = pl.DeviceIdType.MESH
ANY = pl.BlockSpec(memory_space=pl.ANY)


N_PEERS = N_DEV - 1


def _all_gather(blocks, name):
    n = len(blocks)

    def body(*refs):
        x_refs, out_refs = refs[:n], refs[n:2 * n]
        send_sems, recv_sems, local_sems = refs[2 * n:]
        x, y, c = lax.axis_index("x"), lax.axis_index("y"), lax.axis_index("c")
        me, sibling = (x, y, c), (x, y, 1 - c)
        chips = [(1 - x, y), (x, 1 - y), (1 - x, 1 - y)]

        def slab(t, px, py, pc):
            return out_refs[t].at[4 * px + 2 * py + pc]

        def copy(t, k, blk, to, src=None):
            return pltpu.make_async_remote_copy(
                src_ref=slab(t, *blk) if src is None else src, dst_ref=slab(t, *blk),
                send_sem=send_sems.at[N_PEERS * t + k], recv_sem=recv_sems.at[N_PEERS * t + k],
                device_id=to, device_id_type=MESH)

        mine = [pltpu.make_async_copy(x_refs[t], slab(t, *me), local_sems.at[t]) for t in range(n)]
        for cp in mine:
            cp.start()
        first = []
        for t in range(n):
            first.append(copy(t, 0, me, sibling, src=x_refs[t]))
            first += [copy(t, 1 + j, me, (*chip, c), src=x_refs[t]) for j, chip in enumerate(chips)]
        for cp in first:
            cp.start()
        passed = []
        for j, chip in enumerate(chips):
            for t in range(n):
                copy(t, 1 + j, (*chip, c), me).wait_recv()
                fwd = copy(t, 4 + j, (*chip, c), sibling)
                fwd.start()
                passed.append(fwd)
        for t in range(n):
            copy(t, 0, sibling, me).wait_recv()
            for j, chip in enumerate(chips):
                copy(t, 4 + j, (*chip, 1 - c), me).wait_recv()
        for cp in first + passed:
            cp.wait_send()
        for cp in mine:
            cp.wait()

    return pl.pallas_call(
        body, out_shape=[jax.ShapeDtypeStruct((N_DEV,) + b.shape, b.dtype) for b in blocks],
        in_specs=[ANY] * n, out_specs=[ANY] * n,
        scratch_shapes=[pltpu.SemaphoreType.DMA((N_PEERS * n,)), pltpu.SemaphoreType.DMA((N_PEERS * n,)),
                        pltpu.SemaphoreType.DMA((n,))],
        name=name,
    )(*blocks)


def _exchange(slabs, name):
    n = len(slabs)

    def body(*refs):
        x_refs, out_refs = refs[:n], refs[n:2 * n]
        send_sems, recv_sems, local_sems = refs[2 * n:]
        x, y, c = lax.axis_index("x"), lax.axis_index("y"), lax.axis_index("c")
        me = 4 * x + 2 * y + c
        local = [pltpu.make_async_copy(x_refs[t].at[me], out_refs[t].at[me], local_sems.at[t]) for t in range(n)]
        for cp in local:
            cp.start()
        peers = []
        for k in range(1, N_DEV):
            pid = (1 - x if k & 4 else x, 1 - y if k & 2 else y, 1 - c if k & 1 else c)
            peers.append((4 * pid[0] + 2 * pid[1] + pid[2], pid))
        sends = []
        for t in range(n):
            for k, (peer, pid) in enumerate(peers):
                cp = pltpu.make_async_remote_copy(
                    src_ref=x_refs[t].at[peer], dst_ref=out_refs[t].at[me], send_sem=send_sems.at[N_PEERS * t + k],
                    recv_sem=recv_sems.at[N_PEERS * t + k], device_id=pid, device_id_type=MESH)
                cp.start()
                sends.append(cp)
        for t in range(n):
            for k, (peer, pid) in enumerate(peers):
                pltpu.make_async_remote_copy(
                    src_ref=x_refs[t].at[me], dst_ref=out_refs[t].at[peer], send_sem=send_sems.at[N_PEERS * t + k],
                    recv_sem=recv_sems.at[N_PEERS * t + k], device_id=pid, device_id_type=MESH).wait_recv()
        for cp in sends:
            cp.wait_send()
        for cp in local:
            cp.wait()

    return pl.pallas_call(
        body, out_shape=[jax.ShapeDtypeStruct(s.shape, s.dtype) for s in slabs],
        in_specs=[ANY] * n, out_specs=[ANY] * n,
        scratch_shapes=[pltpu.SemaphoreType.DMA((N_PEERS * n,)), pltpu.SemaphoreType.DMA((N_PEERS * n,)),
                        pltpu.SemaphoreType.DMA((n,))],
        name=name,
    )(*slabs)


def _adamw(recv, w, m, v, name):
    rows, cols = w.shape
    tr = _pick(rows, (256, 128, 64, 32, 16))

    def body(r_ref, w_ref, m_ref, v_ref, g_ref, d_ref, m2_ref, v2_ref):
        g = r_ref[0].astype(F32)
        for k in range(1, N_DEV):
            g = g + r_ref[k].astype(F32)
        m2 = ADAM_B1 * m_ref[...] + (1.0 - ADAM_B1) * g
        v2 = ADAM_B2 * v_ref[...] + (1.0 - ADAM_B2) * (g * g)
        m_hat = m2 / (1.0 - ADAM_B1 ** ADAM_STEP)
        v_hat = v2 / (1.0 - ADAM_B2 ** ADAM_STEP)
        g_ref[...] = g
        d_ref[...] = -ADAM_LR * (m_hat / (jnp.sqrt(v_hat) + ADAM_EPS) + ADAM_WD * w_ref[...])
        m2_ref[...] = m2
        v2_ref[...] = v2

    spec = pl.BlockSpec((tr, cols), lambda i: (i, 0))
    return pl.pallas_call(
        body, grid=(rows // tr,),
        in_specs=[pl.BlockSpec((N_DEV, tr, cols), lambda i: (0, i, 0)), spec, spec, spec],
        out_specs=[spec] * 4, out_shape=[jax.ShapeDtypeStruct((rows, cols), F32)] * 4,
        name=name, compiler_params=_params(("parallel",)),
    )(recv, w, m, v)


def _padded_rows(n, mult):
    rows = -(-n // PACK_COLS)
    return -(-rows // mult) * mult


def _pack(flat_pieces, mult, dtype, lead=()):
    flat = jnp.concatenate([p.astype(dtype) for p in flat_pieces], axis=-1)
    n = flat.shape[-1]
    rows = _padded_rows(n, mult)
    flat = jnp.pad(flat, [(0, 0)] * len(lead) + [(0, rows * PACK_COLS - n)])
    return flat.reshape(lead + (rows, PACK_COLS))


def _unpack(packed, shapes):
    flat = packed.reshape(-1)
    out, off = {}, 0
    for name, shape in shapes:
        size = 1
        for s in shape:
            size *= s
        out[name] = flat[off:off + size].reshape(shape)
        off += size
    return out


def _local_2d(arr):
    return arr.reshape(-1, arr.shape[-1])


def _to_slabs(full, shape, axis):
    n = shape[axis]
    if axis == len(shape) - 1:
        full2d = _local_2d(full)
        return jnp.stack([full2d[:, n * d:n * (d + 1)] for d in range(N_DEV)])
    assert axis == 1 and len(shape) == 3
    return jnp.stack([full[:, n * d:n * (d + 1)].reshape(-1, shape[2]) for d in range(N_DEV)])


def _from_gathered(g, shape, axis):
    if axis == len(shape) - 1:
        full2d = jnp.concatenate([g[d] for d in range(N_DEV)], axis=1)
        return full2d.reshape(shape[:-1] + (N_DEV * shape[-1],))
    assert axis == 1 and len(shape) == 3
    g4 = g.reshape((N_DEV,) + shape)
    return jnp.concatenate([g4[d] for d in range(N_DEV)], axis=1)


def _pad_cols(w, total):
    return jnp.pad(w, [(0, 0)] * (w.ndim - 1) + [(0, total - w.shape[-1])])


def _pad_lanes(v):
    return _pad_cols(v, LANE)


def _x(arr, width, cb0=0, moves=0):
    return (arr, width, cb0, moves)


def _rms_fwd(h, g):
    return _row_fwd(_rms_fn, [_x(h, D_MODEL)], [g], [], [(D_MODEL, D_MODEL, BF16)], "rms_fwd")[0]


def _rms_bwd(h, g, dy, add):
    dh, dg = _row_bwd(_rms_fn, [_x(h, D_MODEL)], [g], [], [_x(dy, D_MODEL)], [(D_MODEL, F32)], "rms_bwd", adds=[add])
    return dh, dg


SW_TILE, SW_WIDTH = 384, 256
SW_NCOL = D_FF // SW_WIDTH


def _ffn_fwd(h, g, wgu, wd):
    y = _rms_fwd(h, g)
    gu = _matmul(y, wgu, "nn", F32, "ffn_gate_up")
    a = _row_fwd(_swiglu_fn, [_x(gu, SW_WIDTH, 0, 1), _x(gu, SW_WIDTH, SW_NCOL, 1)], [], [],
                 [(SW_WIDTH, D_FF, BF16)], "swiglu_fwd", tm=SW_TILE, ncol=SW_NCOL)[0]
    h2 = _matmul(a, wd, "nn", F32, "ffn_down", add=h)
    return h2, (h, y, gu, a)


def _ffn_bwd(dh2, saved, g, wgu, wd):
    h, y, gu, a = saved
    da = _matmul(dh2, wd, "nt", F32, "ffn_down_dx")
    dwd = _matmul(a, dh2, "tn", F32, "ffn_down_dw")
    dgt, dup = _row_bwd(_swiglu_fn, [_x(gu, SW_WIDTH, 0, 1), _x(gu, SW_WIDTH, SW_NCOL, 1)], [], [],
                        [_x(da, SW_WIDTH, 0, 1)], [(D_FF, BF16), (D_FF, BF16)], "swiglu_bwd",
                        tm=SW_TILE, ncol=SW_NCOL)
    dgu = jnp.concatenate([dgt, dup], axis=1)
    dwgu = _matmul(y, dgu, "tn", F32, "ffn_gate_up_dw")
    dy = _matmul(dgu, wgu, "nt", F32, "ffn_gate_up_dx")
    dh, dg = _rms_bwd(h, g, dy, dh2)
    return dh, dg, dwgu, dwd


def _fox_consts():
    lane = jnp.arange(D_MODEL) // FOX_DH
    e_expand = (jnp.arange(LANE)[:, None] == lane[None, :]).astype(F32)
    e_mean = e_expand.T * (1.0 / FOX_DH)
    half = jnp.arange(LANE) // FOX_DH
    blockdiag = (half[:, None] == half[None, :]).astype(F32)
    return e_mean, e_expand, blockdiag


def _pair_layout(c):
    heads = c[:, :FOX_HEADS].T
    crow = heads.reshape(FOX_PAIRS, 2, ROWS)
    c2 = jnp.broadcast_to(crow.transpose(0, 2, 1)[:, :, :, None], (FOX_PAIRS, ROWS, 2, FOX_DH))
    crow = jnp.where(jnp.arange(ROWS) >= FIRST_ROW, crow, KEY_OFF)
    return c2.reshape(FOX_PAIRS, ROWS, LANE), crow


def _fox_fwd(y, p):
    proj = _matmul(y, p["w_in"], "nn", F32, "fox_in")
    e_mean, e_expand, _ = _fox_consts()
    nb = D_MODEL // LANE
    xs = [_x(proj, D_MODEL, 0), _x(proj, D_MODEL, 1), _x(proj, LANE, 4 * nb)]
    ps = [p["q_gain"], p["k_gain"], p["b_f"]]
    qn, kn, lf = _row_fwd(_fox_pre_fn, xs, ps, [e_mean, e_expand],
                          [(D_MODEL, D_MODEL, BF16), (D_MODEL, D_MODEL, BF16), (LANE, LANE, F32)], "fox_pre_fwd")
    c = _seq_cumsum(lf, False, "fox_cumsum")
    c2, crow = _pair_layout(c)
    o, lse2 = _fox_attn_fwd(qn, kn, proj, c2, crow)
    om = _row_fwd(_fox_post_fn, [_x(o, D_MODEL, 0), _x(proj, D_MODEL, 3)], [], [],
                  [(D_MODEL, D_MODEL, BF16)], "fox_post_fwd")[0]
    return om, (proj, qn, kn, c2, crow, o, lse2)


def _fox_bwd(dom, y, saved, p):
    proj, qn, kn, c2, crow, o, lse2 = saved
    e_mean, e_expand, blockdiag = _fox_consts()
    nb = D_MODEL // LANE
    do, dgate = _row_bwd(_fox_post_fn, [_x(o, D_MODEL, 0), _x(proj, D_MODEL, 3)], [], [], [_x(dom, D_MODEL)],
                         [(D_MODEL, F32), (D_MODEL, BF16)], "fox_post_bwd")
    dq, delta2, dcq2 = _fox_attn_dq(qn, kn, proj, c2, crow, lse2, o, do, blockdiag)
    dk, dv, dcrow = _fox_attn_dkv(qn, kn, proj, c2, crow, lse2, delta2, do)
    dcq = dcq2[:, :, ::FOX_DH].transpose(1, 0, 2).reshape(ROWS, FOX_HEADS)
    dc = _pad_cols(dcq + dcrow.reshape(FOX_HEADS, ROWS).T, LANE)
    dlf = _seq_cumsum(dc, True, "fox_cumsum_rev")
    xs = [_x(proj, D_MODEL, 0), _x(proj, D_MODEL, 1), _x(proj, LANE, 4 * nb)]
    ps = [p["q_gain"], p["k_gain"], p["b_f"]]
    dqr, dkr, df, dqg, dkg, dbf = _row_bwd(
        _fox_pre_fn, xs, ps, [e_mean, e_expand], [_x(dq, D_MODEL), _x(dk, D_MODEL), _x(dlf, LANE)],
        [(D_MODEL, BF16), (D_MODEL, BF16), (LANE, BF16)], "fox_pre_bwd")
    dproj = jnp.concatenate([dqr, dkr, dv, dgate, df], axis=1)
    grads = {
        "w_in": _matmul(y, dproj, "tn", F32, "fox_in_dw")[:, :FOX_IN],
        "q_gain": dqg.reshape(FOX_HEADS, FOX_DH).sum(axis=0),
        "k_gain": dkg.reshape(FOX_HEADS, FOX_DH).sum(axis=0),
        "b_f": dbf[0, :FOX_HEADS],
    }
    return _matmul(dproj, p["w_in"], "nt", F32, "fox_in_dx"), grads


def _gla_fwd(y, p):
    proj = _matmul(y, p["w_in"], "nn", F32, "gla_in")
    small = (2 * GLA_QK + 2 * GLA_V) // LANE
    garr = _row_fwd(_gla_pre_fn, [_x(proj, LANE, small)], [p["w_alpha"], p["b_alpha"]], [],
                    [(GLA_QK, GLA_QK, F32)], "gla_pre_fwd")[0]
    o, states = _gla_scan_fwd(proj, garr)
    om = _row_fwd(_gla_post_fn, [_x(o, GLA_V, 0), _x(proj, GLA_V, 2)], [p["o_gain"]], [],
                  [(GLA_V, GLA_V, BF16)], "gla_post_fwd")[0]
    return om, (proj, garr, states, o)


def _gla_bwd(dom, y, saved, p):
    proj, garr, states, o = saved
    small = (2 * GLA_QK + 2 * GLA_V) // LANE
    do, dr, dgain = _row_bwd(_gla_post_fn, [_x(o, GLA_V, 0), _x(proj, GLA_V, 2)], [p["o_gain"]], [],
                             [_x(dom, GLA_V)], [(GLA_V, F32), (GLA_V, BF16)], "gla_post_bwd")
    dq, dk, dv, dg = _gla_scan_bwd(proj, garr, states, do)
    dalr, dwa, dba = _row_bwd(_gla_pre_fn, [_x(proj, LANE, small)], [p["w_alpha"], p["b_alpha"]], [],
                              [_x(dg, GLA_QK)], [(LANE, BF16)], "gla_pre_bwd")
    dproj = jnp.concatenate([dq, dk, dv, dr, dalr], axis=1)
    grads = {
        "w_in": _matmul(y, dproj, "tn", F32, "gla_in_dw")[:, :GLA_IN],
        "w_alpha": dwa[:GLA_RANK], "b_alpha": dba[0], "o_gain": dgain[0],
    }
    return _matmul(dproj, p["w_in"], "nt", F32, "gla_in_dx"), grads


def _gdn_fwd(y, p):
    proj = _matmul(y, p["w_in"], "nn", F32, "gdn_in")
    nb = D_MODEL // LANE
    qn = _gdn_conv_fwd(proj, p["conv_q"], "q", 0)
    kn = _gdn_conv_fwd(proj, p["conv_k"], "k", 1)
    vn = _gdn_conv_fwd(proj, p["conv_v"], "v", 2)
    gates = _row_fwd(_gdn_gates_fn, [_x(proj, LANE, 4 * nb)], [p["a_log"], p["dt_bias"]], [],
                     [(LANE, LANE, F32)], "gdn_gates_fwd")[0]
    o, states = _gdn_scan_fwd(qn, kn, vn, gates)
    om = _row_fwd(_gdn_post_fn, [_x(o, D_MODEL, 0), _x(proj, D_MODEL, 3)], [p["o_gain"]], [],
                  [(D_MODEL, D_MODEL, BF16)], "gdn_post_fwd")[0]
    return om, (proj, qn, kn, vn, gates, states, o)


def _gdn_bwd(dom, y, saved, p):
    proj, qn, kn, vn, gates, states, o = saved
    nb = D_MODEL // LANE
    do, dgate, dgain = _row_bwd(_gdn_post_fn, [_x(o, D_MODEL, 0), _x(proj, D_MODEL, 3)], [p["o_gain"]], [],
                                [_x(dom, D_MODEL)], [(D_MODEL, F32), (D_MODEL, BF16)], "gdn_post_bwd")
    dqn, dkn, dvn, dgates = _gdn_scan_bwd(qn, kn, vn, gates, states, do)
    dsm, dalog, ddtb = _row_bwd(_gdn_gates_fn, [_x(proj, LANE, 4 * nb)], [p["a_log"], p["dt_bias"]], [],
                                [_x(dgates, LANE)], [(LANE, BF16)], "gdn_gates_bwd")
    dxq, dwq = _gdn_conv_bwd(proj, p["conv_q"], "q", 0, dqn)
    dxk, dwk = _gdn_conv_bwd(proj, p["conv_k"], "k", 1, dkn)
    dxv, dwv = _gdn_conv_bwd(proj, p["conv_v"], "v", 2, dvn)
    dproj = jnp.concatenate([dxq, dxk, dxv, dgate, dsm], axis=1)
    grads = {
        "w_in": _matmul(y, dproj, "tn", F32, "gdn_in_dw")[:, :GDN_IN],
        "conv_w": jnp.concatenate([dwq[:GDN_CONV], dwk[:GDN_CONV], dwv[:GDN_CONV]], axis=1),
        "a_log": dalog[0, :GDN_HEADS], "dt_bias": ddtb[0, :GDN_HEADS], "o_gain": dgain[0],
    }
    return _matmul(dproj, p["w_in"], "nt", F32, "gdn_in_dx"), grads


_MIXERS = ((_fox_fwd, _fox_bwd), (_gla_fwd, _gla_bwd), (_gdn_fwd, _gdn_bwd))


def _mixer_params(i, wts, rep):
    kind, j = i % 3, i // 3
    if kind == 0:
        return {
            "w_in": _pad_cols(wts["fox_w_in"][j], FOX_IN_PAD), "w_out": wts["fox_w_out"][j],
            "q_gain": jnp.tile(rep["fox_q_gain"][j], FOX_HEADS)[None],
            "k_gain": jnp.tile(rep["fox_k_gain"][j], FOX_HEADS)[None],
            "b_f": _pad_lanes(rep["fox_b_f"][j][None]),
        }
    if kind == 1:
        return {
            "w_in": _pad_cols(wts["gla_w_in"][j], GLA_IN_PAD), "w_out": wts["gla_w_out"][j],
            "w_alpha": jnp.pad(wts["gla_w_alpha2"][j], ((0, LANE - GLA_RANK), (0, 0))).astype(F32),
            "b_alpha": rep["gla_b_alpha"][j][None], "o_gain": rep["gla_o_gain"][j][None],
        }
    conv = jnp.pad(wts["gdn_conv_w"][j][:, 0, :].astype(F32), ((0, 8 - GDN_CONV), (0, 0)))
    return {
        "w_in": _pad_cols(wts["gdn_w_in"][j], GDN_IN_PAD), "w_out": wts["gdn_w_out"][j],
        "conv_q": conv[:, :D_MODEL], "conv_k": conv[:, D_MODEL:2 * D_MODEL], "conv_v": conv[:, 2 * D_MODEL:],
        "a_log": _pad_lanes(rep["gdn_a_log"][j][None]), "dt_bias": _pad_lanes(rep["gdn_dt_bias"][j][None]),
        "o_gain": rep["gdn_o_gain"][j][None],
    }


def kernel(x, meta_tokens, norm_mix, norm_ffn, w_gate_up, w_down, fox_w_in, fox_b_f, fox_q_gain, fox_k_gain, fox_w_out, gla_w_in, gla_w_alpha2, gla_b_alpha, gla_o_gain, gla_w_out, gdn_w_in, gdn_conv_w, gdn_a_log, gdn_dt_bias, gdn_o_gain, gdn_w_out, loss_target, m_meta_tokens, m_norm_mix, m_norm_ffn, m_w_gate_up, m_w_down, m_fox_w_in, m_fox_b_f, m_fox_q_gain, m_fox_k_gain, m_fox_w_out, m_gla_w_in, m_gla_w_alpha2, m_gla_b_alpha, m_gla_o_gain, m_gla_w_out, m_gdn_w_in, m_gdn_conv_w, m_gdn_a_log, m_gdn_dt_bias, m_gdn_o_gain, m_gdn_w_out, v_meta_tokens, v_norm_mix, v_norm_ffn, v_w_gate_up, v_w_down, v_fox_w_in, v_fox_b_f, v_fox_q_gain, v_fox_k_gain, v_fox_w_out, v_gla_w_in, v_gla_w_alpha2, v_gla_b_alpha, v_gla_o_gain, v_gla_w_out, v_gdn_w_in, v_gdn_conv_w, v_gdn_a_log, v_gdn_dt_bias, v_gdn_o_gain, v_gdn_w_out):
    given = dict(locals())
    w_loc = {n: given[n] for n in WEIGHT_ORDER}
    m_loc = {n: given["m_" + n] for n in WEIGHT_ORDER}
    v_loc = {n: given["v_" + n] for n in WEIGHT_ORDER}

    wire = {n: (BF16 if len(s) == 3 and s[1] * s[2] >= 1 << 16 else F32) for n, s, _ in SHARDED}
    gathered = _all_gather([_local_2d(w_loc[n]).astype(wire[n]) for n, _, _ in SHARDED], "all_gather_weights")
    wts = {n: _from_gathered(g, s, a) for (n, s, a), g in zip(SHARDED, gathered)}
    meta_full = wts["meta_tokens"]
    rep = {n: w_loc[n] for n, _ in REPLICATED}

    h = jnp.concatenate([jnp.zeros((FIRST_ROW, D_MODEL), F32), meta_full, x[0]], axis=0)
    saved = []
    for i in range(DEPTH):
        p = _mixer_params(i, wts, rep)
        fwd, _ = _MIXERS[i % 3]
        g_mix = rep["norm_mix"][i][None]
        y = _rms_fwd(h, g_mix)
        om, mix_saved = fwd(y, p)
        h_mid = _matmul(om, p["w_out"], "nn", F32, "mixer_out", add=h)
        g_ffn = rep["norm_ffn"][i][None]
        h_out, ffn_saved = _ffn_fwd(h_mid, g_ffn, wts["w_gate_up"][i], wts["w_down"][i])
        saved.append((p, h, y, om, mix_saved, ffn_saved))
        h = h_out

    dh, sq = _loss_head(h, loss_target[0])
    loss = lax.psum(0.5 * jnp.sum(sq) * (1.0 / D_MODEL), ("x", "y", "c"))

    gw = {n: [None] * DEPTH for n in ("w_gate_up", "w_down", "norm_mix", "norm_ffn")}
    gm = {}
    for i in reversed(range(DEPTH)):
        p, h_in, y, om, mix_saved, ffn_saved = saved[i]
        kind, j = i % 3, i // 3
        _, bwd = _MIXERS[kind]
        dh_mid, gw["norm_ffn"][i], gw["w_gate_up"][i], gw["w_down"][i] = _ffn_bwd(
            dh, ffn_saved, rep["norm_ffn"][i][None], wts["w_gate_up"][i], wts["w_down"][i])
        dom = _matmul(dh_mid, p["w_out"], "nt", F32, "mixer_out_dx")
        dwo = _matmul(om, dh_mid, "tn", F32, "mixer_out_dw")
        dy, grads = bwd(dom, y, mix_saved, p)
        grads["w_out"] = dwo
        gm[(kind, j)] = grads
        dh, gw["norm_mix"][i] = _rms_bwd(h_in, rep["norm_mix"][i][None], dy, dh_mid)

    full = {
        "meta_tokens": dh[FIRST_ROW:PAD_ROWS],
        "w_gate_up": jnp.stack(gw["w_gate_up"]), "w_down": jnp.stack(gw["w_down"]),
        "fox_w_in": jnp.stack([gm[(0, 0)]["w_in"], gm[(0, 1)]["w_in"]]),
        "fox_w_out": jnp.stack([gm[(0, 0)]["w_out"], gm[(0, 1)]["w_out"]]),
        "gla_w_in": gm[(1, 0)]["w_in"][None], "gla_w_alpha2": gm[(1, 0)]["w_alpha"][None],
        "gla_w_out": gm[(1, 0)]["w_out"][None],
        "gdn_w_in": gm[(2, 0)]["w_in"][None], "gdn_conv_w": gm[(2, 0)]["conv_w"][None, :, None, :],
        "gdn_w_out": gm[(2, 0)]["w_out"][None],
    }
    partial = {
        "norm_mix": jnp.concatenate(gw["norm_mix"], axis=0), "norm_ffn": jnp.concatenate(gw["norm_ffn"], axis=0),
        "fox_b_f": jnp.stack([gm[(0, 0)]["b_f"], gm[(0, 1)]["b_f"]]),
        "fox_q_gain": jnp.stack([gm[(0, 0)]["q_gain"], gm[(0, 1)]["q_gain"]]),
        "fox_k_gain": jnp.stack([gm[(0, 0)]["k_gain"], gm[(0, 1)]["k_gain"]]),
        "gla_b_alpha": gm[(1, 0)]["b_alpha"][None], "gla_o_gain": gm[(1, 0)]["o_gain"][None],
        "gdn_a_log": gm[(2, 0)]["a_log"][None], "gdn_dt_bias": gm[(2, 0)]["dt_bias"][None],
        "gdn_o_gain": gm[(2, 0)]["o_gain"][None],
    }

    full_shapes = {n: s[:a] + (N_DEV * s[a],) + s[a + 1:] for n, s, a in SHARDED}
    small = _pack([partial[n].reshape(-1) for n, _ in REPLICATED], 8, F32)
    slabs = [_to_slabs(full[n].reshape(full_shapes[n]), s, a).astype(wire[n]) for n, s, a in SHARDED]
    slabs.append(jnp.broadcast_to(small[None], (N_DEV,) + small.shape))
    recv = _exchange(slabs, "exchange_grads")

    res = [{} for _ in range(4)]
    for (n, s, _), r in zip(SHARDED, recv[:-1]):
        outs = _adamw(r, *[_local_2d(d[n]) for d in (w_loc, m_loc, v_loc)], "adamw_" + n)
        for k in range(4):
            res[k][n] = outs[k].reshape(s)
    outs_small = _adamw(recv[-1], *[_pack([d[n].reshape(-1) for n, _ in REPLICATED], 8, F32)
                                    for d in (w_loc, m_loc, v_loc)], "adamw_replicated")
    for k in range(4):
        res[k].update(_unpack(outs_small[k], list(REPLICATED)))

    result = [loss, dh[PAD_ROWS:][None]]
    for k in range(4):
        result += [res[k][n] for n in WEIGHT_ORDER]
    return tuple(result)
```

```python
import functools

import jax
import jax.numpy as jnp
from jax import lax
from jax.experimental import pallas as pl
from jax.experimental.pallas import tpu as pltpu

F32, BF16 = jnp.float32, jnp.bfloat16
HIGHEST = lax.Precision.HIGHEST

D_MODEL = 1024
SEQ = 4096
N_META = 16
DEPTH = 4
NORM_EPS = 1e-6
LANE = 128
PAD_ROWS = 128
FIRST_ROW = PAD_ROWS - N_META
ROWS = PAD_ROWS + SEQ
ROW_TILE = 128
CHUNK = 64
NEG = -1e30
N_DEV = 8
VMEM_LIMIT = 56 * 1024 * 1024
VMEM_BLOCK_BUDGET = 36 * 1024 * 1024

FOX_HEADS, FOX_DH = 16, 64
FOX_IN, FOX_IN_PAD = 4 * D_MODEL + FOX_HEADS, 4 * D_MODEL + LANE
GLA_HEADS, GLA_DK, GLA_DV, GLA_RANK = 4, 128, 256, 16
GLA_QK, GLA_V = GLA_HEADS * GLA_DK, GLA_HEADS * GLA_DV
GLA_IN, GLA_IN_PAD = 2 * GLA_QK + 2 * GLA_V + GLA_RANK, 2 * GLA_QK + 2 * GLA_V + LANE
GDN_HEADS, GDN_DK, GDN_DV, GDN_CONV = 8, 128, 128, 4
GDN_CONV_DIM = 3 * GDN_HEADS * GDN_DK
GDN_IN, GDN_IN_PAD = GDN_CONV_DIM + GDN_HEADS * GDN_DV + 2 * GDN_HEADS, GDN_CONV_DIM + GDN_HEADS * GDN_DV + LANE
D_FF = 2816

ADAM_LR, ADAM_B1, ADAM_B2, ADAM_EPS, ADAM_WD, ADAM_STEP = 0.001, 0.9, 0.999, 1e-08, 0.01, 10

PACK_COLS = 1024
PACK_ROW_TILE = 256

SHARDED = (
    ("meta_tokens", (16, 128), 1),
    ("w_gate_up", (4, 1024, 704), 2),
    ("w_down", (4, 352, 1024), 1),
    ("fox_w_in", (2, 1024, 514), 2),
    ("fox_w_out", (2, 128, 1024), 1),
    ("gla_w_in", (1, 1024, 386), 2),
    ("gla_w_alpha2", (1, 16, 64), 2),
    ("gla_w_out", (1, 128, 1024), 1),
    ("gdn_w_in", (1, 1024, 514), 2),
    ("gdn_conv_w", (1, 4, 1, 384), 3),
    ("gdn_w_out", (1, 128, 1024), 1),
)
REPLICATED = (
    ("norm_mix", (4, 1024)),
    ("norm_ffn", (4, 1024)),
    ("fox_b_f", (2, 16)),
    ("fox_q_gain", (2, 64)),
    ("fox_k_gain", (2, 64)),
    ("gla_b_alpha", (1, 512)),
    ("gla_o_gain", (1, 256)),
    ("gdn_a_log", (1, 8)),
    ("gdn_dt_bias", (1, 8)),
    ("gdn_o_gain", (1, 128)),
)
WEIGHT_ORDER = (
    "meta_tokens", "norm_mix", "norm_ffn", "w_gate_up", "w_down", "fox_w_in", "fox_b_f", "fox_q_gain",
    "fox_k_gain", "fox_w_out", "gla_w_in", "gla_w_alpha2", "gla_b_alpha", "gla_o_gain", "gla_w_out",
    "gdn_w_in", "gdn_conv_w", "gdn_a_log", "gdn_dt_bias", "gdn_o_gain", "gdn_w_out",
)

_DN = {
    "nn": (((1,), (0,)), ((), ())),
    "nt": (((1,), (1,)), ((), ())),
    "tn": (((0,), (0,)), ((), ())),
}


def _dot(a, b, kind):
    return lax.dot_general(a.astype(BF16), b.astype(BF16), _DN[kind], preferred_element_type=F32)


@functools.partial(jax.custom_vjp, nondiff_argnums=(2,))
def _dot_ad(a, b, kind):
    return _dot(a, b, kind)


def _dot_ad_fwd(a, b, kind):
    return _dot(a, b, kind), (a, b)


def _dot_ad_bwd(kind, res, ct):
    a, b = res
    if kind == "nn":
        return _dot(ct, b, "nt"), _dot(a, ct, "tn")
    if kind == "nt":
        return _dot(ct, b, "nn"), _dot(ct, a, "tn")
    return _dot(b, ct, "nt"), _dot(a, ct, "nn")


_dot_ad.defvjp(_dot_ad_fwd, _dot_ad_bwd)


def _dot_hi(a, b, kind):
    return lax.dot_general(a, b, _DN[kind], precision=HIGHEST, preferred_element_type=F32)


def _dot_mid(a, b, kind):
    return lax.dot_general(a, b, _DN[kind], precision=lax.Precision.HIGH, preferred_element_type=F32)


def _iota(shape, dim):
    return lax.broadcasted_iota(jnp.int32, shape, dim)


def _row_mask(i, tm):
    return ((i * tm + _iota((tm, 1), 0)) >= FIRST_ROW).astype(F32)


def _log_sigmoid(z):
    return jnp.minimum(z, 0.0) - jnp.log(1.0 + jnp.exp(-jnp.abs(z)))


def _softplus(z):
    return jnp.maximum(z, 0.0) + jnp.log(1.0 + jnp.exp(-jnp.abs(z)))


def _silu(z):
    return z * jax.nn.sigmoid(z)


def _lane_col(x, lane):
    return jnp.sum(jnp.where(_iota(x.shape, 1) == lane, x, 0.0), axis=1, keepdims=True)


def _last_row(x):
    return jnp.sum(jnp.where(_iota(x.shape, 0) == x.shape[0] - 1, x, 0.0), axis=0, keepdims=True)


def _params(sem, limit=VMEM_LIMIT):
    return pltpu.CompilerParams(dimension_semantics=sem, vmem_limit_bytes=limit)


def _pick(n, options):
    for t in options:
        if n % t == 0:
            return t
    return n


def _matmul(a, b, kind, out_dtype, name, add=None):
    if kind == "nn":
        (m, k), n = a.shape, b.shape[1]
    elif kind == "nt":
        (m, k), n = a.shape, b.shape[0]
    else:
        (k, m), n = a.shape, b.shape[1]
    tn = _pick(n, (512, 384, 640, 256, 128))
    tm_options = (1408, 384, 128) if kind != "tn" else (256, 128)
    tm = None
    for t in tm_options:
        if m % t:
            continue
        need = 2 * (t * k * a.dtype.itemsize + tn * k * b.dtype.itemsize + t * tn * 8)
        if need <= VMEM_BLOCK_BUDGET:
            tm = t
            break
    assert tm is not None, (name, a.shape, b.shape)

    def body(*refs):
        a_ref, b_ref = refs[0], refs[1]
        o_ref = refs[-1]
        acc = _dot(a_ref[...], b_ref[...], kind)
        if add is not None:
            acc = acc + refs[2][...]
        o_ref[...] = acc.astype(o_ref.dtype)

    if kind == "tn":
        a_spec = pl.BlockSpec((k, tm), lambda i, j: (0, i))
    else:
        a_spec = pl.BlockSpec((tm, k), lambda i, j: (i, 0))
    if kind == "nt":
        b_spec = pl.BlockSpec((tn, k), lambda i, j: (j, 0))
    else:
        b_spec = pl.BlockSpec((k, tn), lambda i, j: (0, j))
    o_spec = pl.BlockSpec((tm, tn), lambda i, j: (i, j))
    operands, in_specs = [a, b], [a_spec, b_spec]
    if add is not None:
        operands.append(add)
        in_specs.append(o_spec)
    return pl.pallas_call(
        body, grid=(m // tm, n // tn), in_specs=in_specs, out_specs=o_spec,
        out_shape=jax.ShapeDtypeStruct((m, n), out_dtype), name=name,
        compiler_params=_params(("parallel", "parallel")),
    )(*operands)


def _x_spec(tm, width, cb0, moves):
    return pl.BlockSpec((tm, width), lambda i, j: (i, cb0 + j * moves))


def _full_spec(arr):
    nd = arr.ndim
    return pl.BlockSpec(arr.shape, lambda i, j: (0,) * nd)


def _row_fwd(fn, xs, ps, cs, outs, name, tm=ROW_TILE, ncol=1, rows=None):
    rows = ROWS if rows is None else rows
    nx, npar, nc = len(xs), len(ps), len(cs)

    def body(*refs):
        i = pl.program_id(0)
        xv = [r[...].astype(F32) for r in refs[:nx]]
        pv = [r[...] for r in refs[nx:nx + npar + nc]]
        res = fn(i, *xv, *pv)
        for r, val in zip(refs[nx + npar + nc:], res):
            r[...] = val.astype(r.dtype)

    in_specs = [_x_spec(tm, w, cb0, mv) for (_, w, cb0, mv) in xs]
    in_specs += [_full_spec(p) for p in list(ps) + list(cs)]
    out_specs = [_x_spec(tm, w, 0, 1) for (w, _, _) in outs]
    out_shape = [jax.ShapeDtypeStruct((rows, tot), dt) for (_, tot, dt) in outs]
    return pl.pallas_call(
        body, grid=(rows // tm, ncol), in_specs=in_specs, out_specs=out_specs, out_shape=out_shape,
        name=name, compiler_params=_params(("parallel", "parallel")),
    )(*[x[0] for x in xs], *ps, *cs)


def _row_bwd(fn, xs, ps, cs, cts, dxs, name, adds=None, tm=ROW_TILE, ncol=1, rows=None):
    rows = ROWS if rows is None else rows
    nx, npar, nc, nct = len(xs), len(ps), len(cs), len(cts)
    adds = [None] * nx if adds is None else adds
    add_idx = [k for k in range(nx) if adds[k] is not None]

    def body(*refs):
        i, j = pl.program_id(0), pl.program_id(1)
        pos = 0
        x_refs = refs[pos:pos + nx]; pos += nx
        p_refs = refs[pos:pos + npar]; pos += npar
        c_refs = refs[pos:pos + nc]; pos += nc
        ct_refs = refs[pos:pos + nct]; pos += nct
        add_refs = refs[pos:pos + len(add_idx)]; pos += len(add_idx)
        dx_refs = refs[pos:pos + nx]; pos += nx
        dp_refs = refs[pos:pos + npar]
        xv = [r[...].astype(F32) for r in x_refs]
        pv = [r[...] for r in p_refs]
        cv = [r[...] for r in c_refs]
        _, vjp = jax.vjp(lambda *args: fn(i, *args, *cv), *xv, *pv)
        grads = vjp(tuple(r[...].astype(F32) for r in ct_refs))
        for k in range(nx):
            val = grads[k]
            if adds[k] is not None:
                val = val + add_refs[add_idx.index(k)][...]
            dx_refs[k][...] = val.astype(dx_refs[k].dtype)
        if npar:
            @pl.when((i == 0) & (j == 0))
            def _():
                for r in dp_refs:
                    r[...] = jnp.zeros(r.shape, r.dtype)

            for k in range(npar):
                dp_refs[k][...] += grads[nx + k]

    in_specs = [_x_spec(tm, w, cb0, mv) for (_, w, cb0, mv) in xs]
    in_specs += [_full_spec(p) for p in list(ps) + list(cs)]
    in_specs += [_x_spec(tm, w, cb0, mv) for (_, w, cb0, mv) in cts]
    in_specs += [_x_spec(tm, xs[k][1], 0, 1) for k in add_idx]
    out_specs = [_x_spec(tm, xs[k][1], 0, 1) for k in range(nx)] + [_full_spec(p) for p in ps]
    out_shape = [jax.ShapeDtypeStruct((rows, tot), dt) for (tot, dt) in dxs]
    out_shape += [jax.ShapeDtypeStruct(p.shape, F32) for p in ps]
    sem = ("arbitrary", "arbitrary") if npar else ("parallel", "parallel")
    return pl.pallas_call(
        body, grid=(rows // tm, ncol), in_specs=in_specs, out_specs=out_specs, out_shape=out_shape,
        name=name, compiler_params=_params(sem),
    )(*[x[0] for x in xs], *ps, *cs, *[c[0] for c in cts], *[adds[k] for k in add_idx])


def _rms_fn(i, h, g):
    return (h * lax.rsqrt(jnp.mean(h * h, axis=1, keepdims=True) + NORM_EPS) * g,)


def _swiglu_fn(i, gt, up):
    return (_silu(gt) * up,)


def _fox_pre_fn(i, q, k, f, qg, kg, bf, e_mean, e_expand):
    tm = q.shape[0]

    def headnorm(x, gain, scale):
        msq = _dot_hi(x * x, e_mean, "nn")
        r = _dot_hi(lax.rsqrt(msq + NORM_EPS), e_expand, "nn")
        return x * r * gain * scale

    qn = headnorm(q, qg, FOX_DH ** -0.5)
    kn = headnorm(k, kg, 1.0)
    keep = (_iota((tm, LANE), 1) < FOX_HEADS).astype(F32) * _row_mask(i, tm)
    return qn, kn, _log_sigmoid(f + bf) * keep


def _fox_post_fn(i, o, gate):
    return (o * jax.nn.sigmoid(gate) * _row_mask(i, o.shape[0]),)


def _gla_pre_fn(i, alr, wa, ba):
    z = _dot_ad(alr, wa, "nn") + ba
    return (_log_sigmoid(z) * (1.0 / 16.0) * _row_mask(i, alr.shape[0]),)


def _headwise_norm_gate(o, gate, gain, dv, mask):
    pieces = []
    for h in range(o.shape[1] // dv):
        oh = o[:, h * dv:(h + 1) * dv]
        pieces.append(oh * lax.rsqrt(jnp.mean(oh * oh, axis=1, keepdims=True) + NORM_EPS) * gain)
    return jnp.concatenate(pieces, axis=1) * _silu(gate) * mask


def _gla_post_fn(i, o, r, gain):
    return (_headwise_norm_gate(o, r, gain, GLA_DV, _row_mask(i, o.shape[0])),)


def _gdn_post_fn(i, o, gate, gain):
    return (_headwise_norm_gate(o, gate, gain, GDN_DV, _row_mask(i, o.shape[0])),)


def _gdn_gates_fn(i, sm, alog, dtb):
    tm = sm.shape[0]
    lane = _iota((tm, LANE), 1)
    g = -jnp.exp(alog) * _softplus(sm + dtb)
    beta = jax.nn.sigmoid(sm)
    out = jnp.where(lane < GDN_HEADS, g, jnp.where(lane < 2 * GDN_HEADS, beta, 0.0))
    return (out * _row_mask(i, tm),)


def _gdn_conv_fn(kind):
    def fn(i, prev, cur, w):
        tm = cur.shape[0]
        xw = jnp.concatenate([prev * jnp.where(i > 0, 1.0, 0.0), cur], axis=0)
        t = _iota((tm, 2 * tm), 0)
        s = _iota((tm, 2 * tm), 1)
        y = jnp.zeros(cur.shape, F32)
        for j in range(GDN_CONV):
            sel = (s == t + (tm - (GDN_CONV - 1) + j)).astype(F32)
            y = y + _dot_mid(sel, xw, "nn") * jnp.sum(jnp.where(_iota(w.shape, 0) == j, w, 0.0), axis=0, keepdims=True)
        a = _silu(y * _row_mask(i, tm))
        if kind == "v":
            return (a,)
        scale = GDN_DK ** -0.5 if kind == "q" else 1.0
        pieces = []
        for h in range(GDN_HEADS):
            ah = a[:, h * GDN_DK:(h + 1) * GDN_DK]
            pieces.append(ah * lax.rsqrt(jnp.sum(ah * ah, axis=1, keepdims=True) + NORM_EPS) * scale)
        return (jnp.concatenate(pieces, axis=1),)

    return fn


def _seq_cumsum(x, reverse, name):
    rows, width = x.shape
    nt = rows // ROW_TILE

    def body(x_ref, o_ref, carry):
        @pl.when(pl.program_id(0) == 0)
        def _():
            carry[...] = jnp.zeros(carry.shape, F32)

        r = _iota((ROW_TILE, ROW_TILE), 0)
        c = _iota((ROW_TILE, ROW_TILE), 1)
        tri = ((c >= r) if reverse else (c <= r)).astype(F32)
        acc = _dot_hi(tri, x_ref[...], "nn") + carry[...]
        o_ref[...] = acc
        edge = 0 if reverse else ROW_TILE - 1
        carry[...] = jnp.sum(jnp.where(_iota(acc.shape, 0) == edge, acc, 0.0), axis=0, keepdims=True)

    idx = (lambda i: (nt - 1 - i, 0)) if reverse else (lambda i: (i, 0))
    return pl.pallas_call(
        body, grid=(nt,), in_specs=[pl.BlockSpec((ROW_TILE, width), idx)],
        out_specs=pl.BlockSpec((ROW_TILE, width), idx), out_shape=jax.ShapeDtypeStruct(x.shape, F32),
        scratch_shapes=[pltpu.VMEM((1, width), F32)], name=name, compiler_params=_params(("arbitrary",)),
    )(x)


FOX_PAIRS = FOX_HEADS // 2
FOX_V_BLOCK = 2 * D_MODEL // LANE


def _pair_cols(x):
    return _lane_col(x, 0), _lane_col(x, FOX_DH)


ATT_TILE = 384
assert ROWS % ATT_TILE == 0
N_ATT = ROWS // ATT_TILE
KEY_OFF = 1e30


def _fox_scores(qa, qb, kj, ca, cb, cra, crb, mask):
    sa = _dot(qa, kj, "nt") + (ca - cra)
    sb = _dot(qb, kj, "nt") + (cb - crb)
    if mask is None:
        return sa, sb
    return jnp.where(mask, sa, NEG), jnp.where(mask, sb, NEG)


def _fox_diag_mask():
    return _iota((ATT_TILE, ATT_TILE), 1) <= _iota((ATT_TILE, ATT_TILE), 0)


def _att_off(j):
    return pl.multiple_of(j * ATT_TILE, ATT_TILE)


def _fox_attn_fwd(qn, kn, proj, c2, crow):
    nt = ROWS // ROW_TILE

    def body(q_ref, k_ref, v_ref, c2_ref, crow_ref, o_ref, lse_ref):
        i = pl.program_id(1)
        q = q_ref[...]
        is_a = _iota((ATT_TILE, LANE), 1) < FOX_DH
        qa, qb = jnp.where(is_a, q, 0), jnp.where(is_a, 0, q)
        ca, cb = _pair_cols(c2_ref[0])

        def block(off, size, mask, carry):
            ma, la, acca, mb, lb, accb = carry
            kj = k_ref[pl.ds(off, size), :]
            vj = v_ref[pl.ds(off, size), :].astype(BF16)
            sa, sb = _fox_scores(qa, qb, kj, ca, cb, crow_ref[0, 0:1, pl.ds(off, size)],
                                 crow_ref[0, 1:2, pl.ds(off, size)], mask)

            def online(s, m, l, acc):
                m_new = jnp.maximum(m, jnp.max(s, axis=1, keepdims=True))
                p = jnp.exp(s - m_new)
                alpha = jnp.exp(m - m_new)
                return m_new, alpha * l + jnp.sum(p, axis=1, keepdims=True), alpha * acc + _dot(p, vj, "nn")

            return online(sa, ma, la, acca) + online(sb, mb, lb, accb)

        m0 = jnp.full((ATT_TILE, 1), NEG, F32)
        l0 = jnp.zeros((ATT_TILE, 1), F32)
        a0 = jnp.zeros((ATT_TILE, LANE), F32)
        carry = lax.fori_loop(0, i, lambda j, cr: block(_att_off(j), ATT_TILE, None, cr), (m0, l0, a0, m0, l0, a0))
        ma, la, acca, mb, lb, accb = block(_att_off(i), ATT_TILE, _fox_diag_mask(), carry)
        o_ref[...] = jnp.where(is_a, acca / la, accb / lb)
        lse_ref[0] = jnp.where(is_a, ma + jnp.log(la), mb + jnp.log(lb))

    return pl.pallas_call(
        body, grid=(FOX_PAIRS, N_ATT),
        in_specs=[
            pl.BlockSpec((ATT_TILE, LANE), lambda g, i: (i, g)),
            pl.BlockSpec((ROWS, LANE), lambda g, i: (0, g)),
            pl.BlockSpec((ROWS, LANE), lambda g, i: (0, FOX_V_BLOCK + g)),
            pl.BlockSpec((1, ATT_TILE, LANE), lambda g, i: (g, i, 0)),
            pl.BlockSpec((1, 2, ROWS), lambda g, i: (g, 0, 0)),
        ],
        out_specs=[
            pl.BlockSpec((ATT_TILE, LANE), lambda g, i: (i, g)),
            pl.BlockSpec((1, ATT_TILE, LANE), lambda g, i: (g, i, 0)),
        ],
        out_shape=[jax.ShapeDtypeStruct((ROWS, D_MODEL), F32), jax.ShapeDtypeStruct((FOX_PAIRS, ROWS, LANE), F32)],
        name="fox_attn_fwd", compiler_params=_params(("parallel", "arbitrary")),
    )(qn, kn, proj, c2, crow)


def _fox_attn_dq(qn, kn, proj, c2, crow, lse2, o, do, blockdiag):
    nt = ROWS // ROW_TILE

    def body(q_ref, k_ref, v_ref, c2_ref, crow_ref, lse_ref, o_ref, do_ref, bd_ref, dq_ref, delta_ref, dcq_ref):
        i = pl.program_id(1)
        q = q_ref[...]
        is_a = _iota((ATT_TILE, LANE), 1) < FOX_DH
        qa, qb = jnp.where(is_a, q, 0), jnp.where(is_a, 0, q)
        ca, cb = _pair_cols(c2_ref[0])
        lsa, lsb = _pair_cols(lse_ref[0])
        do_t = do_ref[...]
        delta2 = _dot_hi(do_t * o_ref[...], bd_ref[...], "nn")
        delta_ref[0] = delta2
        dla, dlb = _pair_cols(delta2)
        doa, dob = jnp.where(is_a, do_t, 0.0).astype(BF16), jnp.where(is_a, 0.0, do_t).astype(BF16)

        def block(off, size, mask, carry):
            dqa, dqb, rsa, rsb = carry
            kj = k_ref[pl.ds(off, size), :]
            vj = v_ref[pl.ds(off, size), :].astype(BF16)
            sa, sb = _fox_scores(qa, qb, kj, ca, cb, crow_ref[0, 0:1, pl.ds(off, size)],
                                 crow_ref[0, 1:2, pl.ds(off, size)], mask)
            dsa = jnp.exp(sa - lsa) * (_dot(doa, vj, "nt") - dla)
            dsb = jnp.exp(sb - lsb) * (_dot(dob, vj, "nt") - dlb)
            return (dqa + _dot(dsa, kj, "nn"), dqb + _dot(dsb, kj, "nn"),
                    rsa + jnp.sum(dsa, axis=1, keepdims=True), rsb + jnp.sum(dsb, axis=1, keepdims=True))

        z = jnp.zeros((ATT_TILE, LANE), F32)
        zc = jnp.zeros((ATT_TILE, 1), F32)
        carry = lax.fori_loop(0, i, lambda j, cr: block(_att_off(j), ATT_TILE, None, cr), (z, z, zc, zc))
        dqa, dqb, rsa, rsb = block(_att_off(i), ATT_TILE, _fox_diag_mask(), carry)
        dq_ref[...] = jnp.where(is_a, dqa, dqb)
        dcq_ref[0] = jnp.where(is_a, rsa, rsb)

    tile = pl.BlockSpec((ATT_TILE, LANE), lambda g, i: (i, g))
    pair_tile = pl.BlockSpec((1, ATT_TILE, LANE), lambda g, i: (g, i, 0))
    return pl.pallas_call(
        body, grid=(FOX_PAIRS, N_ATT),
        in_specs=[
            tile,
            pl.BlockSpec((ROWS, LANE), lambda g, i: (0, g)),
            pl.BlockSpec((ROWS, LANE), lambda g, i: (0, FOX_V_BLOCK + g)),
            pair_tile,
            pl.BlockSpec((1, 2, ROWS), lambda g, i: (g, 0, 0)),
            pair_tile, tile, tile,
            pl.BlockSpec((LANE, LANE), lambda g, i: (0, 0)),
        ],
        out_specs=[tile, pair_tile, pair_tile],
        out_shape=[jax.ShapeDtypeStruct((ROWS, D_MODEL), F32), jax.ShapeDtypeStruct((FOX_PAIRS, ROWS, LANE), F32),
                   jax.ShapeDtypeStruct((FOX_PAIRS, ROWS, LANE), F32)],
        name="fox_attn_dq", compiler_params=_params(("parallel", "arbitrary")),
    )(qn, kn, proj, c2, crow, lse2, o, do, blockdiag)


def _fox_attn_dkv(qn, kn, proj, c2, crow, lse2, delta2, do):
    nt = ROWS // ROW_TILE

    def body(q_ref, k_ref, v_ref, c2_ref, crow_ref, lse_ref, delta_ref, do_ref, dk_ref, dv_ref, dc_ref):
        j = pl.program_id(1)
        kj = k_ref[...]
        vj = v_ref[...].astype(BF16)
        cra, crb = crow_ref[0, 0:1, :], crow_ref[0, 1:2, :]

        def block(off, size, mask, carry):
            dk, dv, dca, dcb = carry
            is_a = _iota((size, LANE), 1) < FOX_DH
            q = q_ref[pl.ds(off, size), :]
            qa, qb = jnp.where(is_a, q, 0), jnp.where(is_a, 0, q)
            do_t = do_ref[pl.ds(off, size), :]
            doa, dob = jnp.where(is_a, do_t, 0.0).astype(BF16), jnp.where(is_a, 0.0, do_t).astype(BF16)
            ca, cb = _pair_cols(c2_ref[0, pl.ds(off, size), :])
            lsa, lsb = _pair_cols(lse_ref[0, pl.ds(off, size), :])
            dla, dlb = _pair_cols(delta_ref[0, pl.ds(off, size), :])
            sa, sb = _fox_scores(qa, qb, kj, ca, cb, cra, crb, mask)
            pa, pb = jnp.exp(sa - lsa), jnp.exp(sb - lsb)
            dsa = pa * (_dot(doa, vj, "nt") - dla)
            dsb = pb * (_dot(dob, vj, "nt") - dlb)
            dv = dv + _dot(pa, doa, "tn") + _dot(pb, dob, "tn")
            dk = dk + _dot(dsa, qa, "tn") + _dot(dsb, qb, "tn")
            dca = dca - jnp.sum(dsa, axis=0, keepdims=True)
            dcb = dcb - jnp.sum(dsb, axis=0, keepdims=True)
            return dk, dv, dca, dcb

        z = jnp.zeros((ATT_TILE, LANE), F32)
        zr = jnp.zeros((1, ATT_TILE), F32)
        carry = block(_att_off(j), ATT_TILE, _fox_diag_mask(), (z, z, zr, zr))
        dk, dv, dca, dcb = lax.fori_loop(j + 1, N_ATT, lambda i, cr: block(_att_off(i), ATT_TILE, None, cr), carry)
        dk_ref[...] = dk
        dv_ref[...] = dv.astype(dv_ref.dtype)
        dc_ref[0, 0:1, :] = dca
        dc_ref[0, 1:2, :] = dcb

    whole = pl.BlockSpec((ROWS, LANE), lambda g, j: (0, g))
    pair_whole = pl.BlockSpec((1, ROWS, LANE), lambda g, j: (g, 0, 0))
    tile = pl.BlockSpec((ATT_TILE, LANE), lambda g, j: (j, g))
    return pl.pallas_call(
        body, grid=(FOX_PAIRS, N_ATT),
        in_specs=[
            whole, tile,
            pl.BlockSpec((ATT_TILE, LANE), lambda g, j: (j, FOX_V_BLOCK + g)),
            pair_whole,
            pl.BlockSpec((1, 2, ATT_TILE), lambda g, j: (g, 0, j)),
            pair_whole, pair_whole, whole,
        ],
        out_specs=[tile, tile, pl.BlockSpec((1, 2, ATT_TILE), lambda g, j: (g, 0, j))],
        out_shape=[jax.ShapeDtypeStruct((ROWS, D_MODEL), F32), jax.ShapeDtypeStruct((ROWS, D_MODEL), BF16),
                   jax.ShapeDtypeStruct((FOX_PAIRS, 2, ROWS), F32)],
        name="fox_attn_dkv", compiler_params=_params(("parallel", "arbitrary")),
    )(qn, kn, proj, c2, crow, lse2, delta2, do)


def _gla_chunk(q, k, v, g, st, dot):
    c = q.shape[0]
    incl = _iota((c, c), 1) <= _iota((c, c), 0)
    b = _dot_hi(incl.astype(F32), g, "nn")
    qe = q * (GLA_DK ** -0.5) * jnp.exp(b)
    ke = k * jnp.exp(-b)
    att = jnp.where(incl, dot(qe, ke, "nt"), 0.0)
    o = dot(att, v, "nn") + dot(qe, st, "nt")
    b_last = _last_row(b)
    kd = k * jnp.exp(b_last - b)
    return o, st * jnp.exp(b_last) + dot(v, kd, "tn")


def _tri_inverse(low):
    c = low.shape[0]
    eye = (_iota((c, c), 0) == _iota((c, c), 1)).astype(F32)
    x, p = eye - low, low
    span = 2
    while span < c:
        p = _dot_mid(p, p, "nn")
        x = _dot_mid(x, eye + p, "nn")
        span *= 2
    return x


def _solve(low, rhs):
    return _dot_mid(_tri_inverse(low), rhs, "nn")


@jax.custom_vjp
def _solve_ad(low, rhs):
    return _solve(low, rhs)


def _solve_ad_fwd(low, rhs):
    inv = _tri_inverse(low)
    sol = _dot_mid(inv, rhs, "nn")
    return sol, (inv, sol)


def _solve_ad_bwd(res, ct):
    inv, sol = res
    drhs = _dot_mid(inv, ct, "tn")
    return -_dot_mid(drhs, sol, "nt"), drhs


_solve_ad.defvjp(_solve_ad_fwd, _solve_ad_bwd)


def _gdn_chunk(q, k, v, gates, st, h, dot, solve):
    c = q.shape[0]
    gcol = _lane_col(gates, h)
    beta = _lane_col(gates, h + GDN_HEADS)
    incl = _iota((c, c), 1) <= _iota((c, c), 0)
    strict = _iota((c, c), 1) < _iota((c, c), 0)
    incl_f = incl.astype(F32)
    b = _dot_hi(incl_f, jnp.broadcast_to(gcol, (c, LANE)), "nn")
    g_sq = jnp.broadcast_to(gcol, (c, c))
    b_col = _dot_hi(incl_f, g_sq, "nn")
    b_row = _dot_hi(g_sq, (_iota((c, c), 0) <= _iota((c, c), 1)).astype(F32), "tn")
    diff = b_col - b_row
    kb, vb = k * beta, v * beta
    low = dot(kb, k, "nt") * jnp.exp(jnp.where(strict, diff, NEG))
    eb = jnp.exp(b)
    sol = solve(low, jnp.concatenate([vb, kb * eb], axis=1))
    v_new = sol[:, :GDN_DV] - dot(sol[:, GDN_DV:], st, "nn")
    att = dot(q, k, "nt") * jnp.exp(jnp.where(incl, diff, NEG))
    o = dot(att, v_new, "nn") + dot(q * eb, st, "nn")
    b_last = _last_row(b)
    kd = k * jnp.exp(b_last - b)
    return o, st * jnp.exp(b_last) + dot(kd, v_new, "tn")


N_CHUNKS = ROWS // CHUNK
GDN_GROUP = 4


def _gla_scan_fwd(proj, garr):
    def body(q_ref, k_ref, v_ref, g_ref, o_ref, states_ref, st):
        @pl.when(pl.program_id(1) == 0)
        def _():
            st[...] = jnp.zeros(st.shape, F32)

        s_in = st[...]
        states_ref[0, 0] = s_in
        o, s_out = _gla_chunk(q_ref[...], k_ref[...], v_ref[...], g_ref[...], s_in, _dot)
        o_ref[...] = o
        st[...] = s_out

    qk = GLA_QK // GLA_DK
    return pl.pallas_call(
        body, grid=(GLA_HEADS, N_CHUNKS),
        in_specs=[
            pl.BlockSpec((CHUNK, GLA_DK), lambda h, c: (c, h)),
            pl.BlockSpec((CHUNK, GLA_DK), lambda h, c: (c, qk + h)),
            pl.BlockSpec((CHUNK, GLA_DV), lambda h, c: (c, qk + h)),
            pl.BlockSpec((CHUNK, GLA_DK), lambda h, c: (c, h)),
        ],
        out_specs=[
            pl.BlockSpec((CHUNK, GLA_DV), lambda h, c: (c, h)),
            pl.BlockSpec((1, 1, GLA_DV, GLA_DK), lambda h, c: (h, c, 0, 0)),
        ],
        out_shape=[jax.ShapeDtypeStruct((ROWS, GLA_V), F32),
                   jax.ShapeDtypeStruct((GLA_HEADS, N_CHUNKS, GLA_DV, GLA_DK), F32)],
        scratch_shapes=[pltpu.VMEM((GLA_DV, GLA_DK), F32)],
        name="gla_scan_fwd", compiler_params=_params(("parallel", "arbitrary")),
    )(proj, proj, proj, garr)


def _gla_scan_bwd(proj, garr, states, do):
    last = N_CHUNKS - 1

    def body(q_ref, k_ref, v_ref, g_ref, s_ref, do_ref, dq_ref, dk_ref, dv_ref, dg_ref, dst):
        @pl.when(pl.program_id(1) == 0)
        def _():
            dst[...] = jnp.zeros(dst.shape, F32)

        _, vjp = jax.vjp(lambda q, k, v, g, s: _gla_chunk(q, k, v, g, s, _dot_ad),
                         q_ref[...], k_ref[...], v_ref[...], g_ref[...], s_ref[0, 0])
        dq, dk, dv, dg, ds = vjp((do_ref[...], dst[...]))
        dq_ref[...] = dq.astype(dq_ref.dtype)
        dk_ref[...] = dk.astype(dk_ref.dtype)
        dv_ref[...] = dv.astype(dv_ref.dtype)
        dg_ref[...] = dg
        dst[...] = ds

    qk = GLA_QK // GLA_DK
    kspec = pl.BlockSpec((CHUNK, GLA_DK), lambda h, c: (last - c, h))
    vspec = pl.BlockSpec((CHUNK, GLA_DV), lambda h, c: (last - c, h))
    return pl.pallas_call(
        body, grid=(GLA_HEADS, N_CHUNKS),
        in_specs=[
            kspec,
            pl.BlockSpec((CHUNK, GLA_DK), lambda h, c: (last - c, qk + h)),
            pl.BlockSpec((CHUNK, GLA_DV), lambda h, c: (last - c, qk + h)),
            kspec,
            pl.BlockSpec((1, 1, GLA_DV, GLA_DK), lambda h, c: (h, last - c, 0, 0)),
            vspec,
        ],
        out_specs=[kspec, kspec, vspec, kspec],
        out_shape=[jax.ShapeDtypeStruct((ROWS, GLA_QK), BF16), jax.ShapeDtypeStruct((ROWS, GLA_QK), BF16),
                   jax.ShapeDtypeStruct((ROWS, GLA_V), BF16), jax.ShapeDtypeStruct((ROWS, GLA_QK), F32)],
        scratch_shapes=[pltpu.VMEM((GLA_DV, GLA_DK), F32)],
        name="gla_scan_bwd", compiler_params=_params(("parallel", "arbitrary")),
    )(proj, proj, proj, garr, states, do)


def _gdn_scan_fwd(qn, kn, vn, gates):
    def body(q_ref, k_ref, v_ref, g_ref, o_ref, states_ref, st):
        @pl.when((pl.program_id(0) == 0) & (pl.program_id(1) == 0))
        def _():
            st[...] = jnp.zeros(st.shape, F32)

        gates = g_ref[...]
        heads = [pl.program_id(1) * GDN_GROUP + hh for hh in range(GDN_GROUP)]
        lanes = [slice(hh * GDN_DK, (hh + 1) * GDN_DK) for hh in range(GDN_GROUP)]
        s_in = [st[h] for h in heads]
        outs = [_gdn_chunk(q_ref[:, ln], k_ref[:, ln], v_ref[:, ln], gates, s, h, _dot, _solve)
                for ln, s, h in zip(lanes, s_in, heads)]
        for hh in range(GDN_GROUP):
            states_ref[hh, 0] = s_in[hh]
            o_ref[:, lanes[hh]] = outs[hh][0]
            st[heads[hh]] = outs[hh][1]

    hspec = pl.BlockSpec((CHUNK, GDN_GROUP * GDN_DK), lambda c, h: (c, h))
    return pl.pallas_call(
        body, grid=(N_CHUNKS, GDN_HEADS // GDN_GROUP),
        in_specs=[hspec, hspec, hspec, pl.BlockSpec((CHUNK, LANE), lambda c, h: (c, 0))],
        out_specs=[hspec, pl.BlockSpec((GDN_GROUP, 1, GDN_DK, GDN_DV), lambda c, h: (h, c, 0, 0))],
        out_shape=[jax.ShapeDtypeStruct((ROWS, D_MODEL), F32),
                   jax.ShapeDtypeStruct((GDN_HEADS, N_CHUNKS, GDN_DK, GDN_DV), F32)],
        scratch_shapes=[pltpu.VMEM((GDN_HEADS, GDN_DK, GDN_DV), F32)],
        name="gdn_scan_fwd", compiler_params=_params(("arbitrary", "arbitrary")),
    )(qn, kn, vn, gates)


def _gdn_scan_bwd(qn, kn, vn, gates, states, do):
    last = N_CHUNKS - 1

    def body(q_ref, k_ref, v_ref, g_ref, s_ref, do_ref, dq_ref, dk_ref, dv_ref, dg_ref, dst):
        @pl.when((pl.program_id(0) == 0) & (pl.program_id(1) == 0))
        def _():
            dst[...] = jnp.zeros(dst.shape, F32)

        @pl.when(pl.program_id(1) == 0)
        def _():
            dg_ref[...] = jnp.zeros(dg_ref.shape, F32)

        gates = g_ref[...]
        heads = [pl.program_id(1) * GDN_GROUP + hh for hh in range(GDN_GROUP)]
        lanes = [slice(hh * GDN_DK, (hh + 1) * GDN_DK) for hh in range(GDN_GROUP)]
        ds_in = [dst[h] for h in heads]
        grads = []
        for hh in range(GDN_GROUP):
            ln = lanes[hh]
            _, vjp = jax.vjp(
                functools.partial(lambda hd, q, k, v, g, s: _gdn_chunk(q, k, v, g, s, hd, _dot_ad, _solve_ad),
                                  heads[hh]),
                q_ref[:, ln], k_ref[:, ln], v_ref[:, ln], gates, s_ref[hh, 0])
            grads.append(vjp((do_ref[:, ln], ds_in[hh])))
        dg_sum = grads[0][3]
        for hh in range(1, GDN_GROUP):
            dg_sum = dg_sum + grads[hh][3]
        for hh in range(GDN_GROUP):
            dq_ref[:, lanes[hh]] = grads[hh][0]
            dk_ref[:, lanes[hh]] = grads[hh][1]
            dv_ref[:, lanes[hh]] = grads[hh][2]
            dst[heads[hh]] = grads[hh][4]
        dg_ref[...] += dg_sum

    hspec = pl.BlockSpec((CHUNK, GDN_GROUP * GDN_DK), lambda c, h: (last - c, h))
    gspec = pl.BlockSpec((CHUNK, LANE), lambda c, h: (last - c, 0))
    return pl.pallas_call(
        body, grid=(N_CHUNKS, GDN_HEADS // GDN_GROUP),
        in_specs=[hspec, hspec, hspec, gspec,
                  pl.BlockSpec((GDN_GROUP, 1, GDN_DK, GDN_DV), lambda c, h: (h, last - c, 0, 0)), hspec],
        out_specs=[hspec, hspec, hspec, gspec],
        out_shape=[jax.ShapeDtypeStruct((ROWS, D_MODEL), F32)] * 3 + [jax.ShapeDtypeStruct((ROWS, LANE), F32)],
        scratch_shapes=[pltpu.VMEM((GDN_HEADS, GDN_DK, GDN_DV), F32)],
        name="gdn_scan_bwd", compiler_params=_params(("arbitrary", "arbitrary")),
    )(qn, kn, vn, gates, states, do)


def _gdn_conv_fwd(proj, w, kind, cb):
    fn = _gdn_conv_fn(kind)
    nt = ROWS // ROW_TILE

    def body(prev_ref, cur_ref, w_ref, o_ref):
        (out,) = fn(pl.program_id(0), prev_ref[...], cur_ref[...], w_ref[...])
        o_ref[...] = out

    return pl.pallas_call(
        body, grid=(nt,),
        in_specs=[pl.BlockSpec((ROW_TILE, D_MODEL), lambda i: (jnp.maximum(i - 1, 0), cb)),
                  pl.BlockSpec((ROW_TILE, D_MODEL), lambda i: (i, cb)),
                  pl.BlockSpec(w.shape, lambda i: (0, 0))],
        out_specs=pl.BlockSpec((ROW_TILE, D_MODEL), lambda i: (i, 0)),
        out_shape=jax.ShapeDtypeStruct((ROWS, D_MODEL), F32),
        name="gdn_conv_fwd_" + kind, compiler_params=_params(("parallel",)),
    )(proj, proj, w)


def _gdn_conv_bwd(proj, w, kind, cb, ct):
    fn = _gdn_conv_fn(kind)
    nt = ROWS // ROW_TILE

    def body(prev_ref, cur_ref, w_ref, ct_ref, dx_ref, dw_ref, carry):
        step = pl.program_id(0)
        i = nt - 1 - step

        @pl.when(step == 0)
        def _():
            carry[...] = jnp.zeros(carry.shape, F32)
            dw_ref[...] = jnp.zeros(dw_ref.shape, F32)

        _, vjp = jax.vjp(lambda p, c, ww: fn(i, p, c, ww), prev_ref[...], cur_ref[...], w_ref[...])
        dprev, dcur, dw = vjp((ct_ref[...],))
        dx_ref[...] = (dcur + carry[...]).astype(dx_ref.dtype)
        carry[...] = dprev
        dw_ref[...] += dw

    return pl.pallas_call(
        body, grid=(nt,),
        in_specs=[pl.BlockSpec((ROW_TILE, D_MODEL), lambda s: (jnp.maximum(nt - 2 - s, 0), cb)),
                  pl.BlockSpec((ROW_TILE, D_MODEL), lambda s: (nt - 1 - s, cb)),
                  pl.BlockSpec(w.shape, lambda s: (0, 0)),
                  pl.BlockSpec((ROW_TILE, D_MODEL), lambda s: (nt - 1 - s, 0))],
        out_specs=[pl.BlockSpec((ROW_TILE, D_MODEL), lambda s: (nt - 1 - s, 0)),
                   pl.BlockSpec(w.shape, lambda s: (0, 0))],
        out_shape=[jax.ShapeDtypeStruct((ROWS, D_MODEL), BF16), jax.ShapeDtypeStruct(w.shape, F32)],
        scratch_shapes=[pltpu.VMEM((ROW_TILE, D_MODEL), F32)],
        name="gdn_conv_bwd_" + kind, compiler_params=_params(("arbitrary",)),
    )(proj, proj, w, ct)


def _loss_head(h, target):
    nt = ROWS // ROW_TILE
    lead = PAD_ROWS // ROW_TILE

    def body(h_ref, t_ref, dh_ref, acc_ref):
        i = pl.program_id(0)

        @pl.when(i == 0)
        def _():
            acc_ref[...] = jnp.zeros(acc_ref.shape, F32)

        err = (h_ref[...] - t_ref[...]) * jnp.where(i >= lead, 1.0, 0.0)
        dh_ref[...] = err * (1.0 / D_MODEL)
        acc_ref[...] += jnp.sum(err * err, axis=0, keepdims=True)

    return pl.pallas_call(
        body, grid=(nt,),
        in_specs=[pl.BlockSpec((ROW_TILE, D_MODEL), lambda i: (i, 0)),
                  pl.BlockSpec((ROW_TILE, D_MODEL), lambda i: (jnp.maximum(i - lead, 0), 0))],
        out_specs=[pl.BlockSpec((ROW_TILE, D_MODEL), lambda i: (i, 0)),
                   pl.BlockSpec((1, D_MODEL), lambda i: (0, 0))],
        out_shape=[jax.ShapeDtypeStruct((ROWS, D_MODEL), F32), jax.ShapeDtypeStruct((1, D_MODEL), F32)],
        name="loss_head", compiler_params=_params(("arbitrary",)),
    )(h, target)


MESH = pl.DeviceIdType.MESH
ANY = pl.BlockSpec(memory_space=pl.ANY)


N_PEERS = N_DEV - 1


def _all_gather(blocks, name):
    n = len(blocks)

    def body(*refs):
        x_refs, out_refs = refs[:n], refs[n:2 * n]
        send_sems, recv_sems, local_sems = refs[2 * n:]
        x, y, c = lax.axis_index("x"), lax.axis_index("y"), lax.axis_index("c")
        me, sibling = (x, y, c), (x, y, 1 - c)
        chips = [(1 - x, y), (x, 1 - y), (1 - x, 1 - y)]

        def slab(t, px, py, pc):
            return out_refs[t].at[4 * px + 2 * py + pc]

        def copy(t, k, blk, to, src=None):
            return pltpu.make_async_remote_copy(
                src_ref=slab(t, *blk) if src is None else src, dst_ref=slab(t, *blk),
                send_sem=send_sems.at[N_PEERS * t + k], recv_sem=recv_sems.at[N_PEERS * t + k],
                device_id=to, device_id_type=MESH)

        mine = [pltpu.make_async_copy(x_refs[t], slab(t, *me), local_sems.at[t]) for t in range(n)]
        for cp in mine:
            cp.start()
        first = []
        for t in range(n):
            first.append(copy(t, 0, me, sibling, src=x_refs[t]))
            first += [copy(t, 1 + j, me, (*chip, c), src=x_refs[t]) for j, chip in enumerate(chips)]
        for cp in first:
            cp.start()
        passed = []
        for j, chip in enumerate(chips):
            for t in range(n):
                copy(t, 1 + j, (*chip, c), me).wait_recv()
                fwd = copy(t, 4 + j, (*chip, c), sibling)
                fwd.start()
                passed.append(fwd)
        for t in range(n):
            copy(t, 0, sibling, me).wait_recv()
            for j, chip in enumerate(chips):
                copy(t, 4 + j, (*chip, 1 - c), me).wait_recv()
        for cp in first + passed:
            cp.wait_send()
        for cp in mine:
            cp.wait()

    return pl.pallas_call(
        body, out_shape=[jax.ShapeDtypeStruct((N_DEV,) + b.shape, b.dtype) for b in blocks],
        in_specs=[ANY] * n, out_specs=[ANY] * n,
        scratch_shapes=[pltpu.SemaphoreType.DMA((N_PEERS * n,)), pltpu.SemaphoreType.DMA((N_PEERS * n,)),
                        pltpu.SemaphoreType.DMA((n,))],
        name=name,
    )(*blocks)


def _exchange(slabs, name):
    n = len(slabs)

    def body(*refs):
        x_refs, out_refs = refs[:n], refs[n:2 * n]
        send_sems, recv_sems, local_sems = refs[2 * n:]
        x, y, c = lax.axis_index("x"), lax.axis_index("y"), lax.axis_index("c")
        me = 4 * x + 2 * y + c
        local = [pltpu.make_async_copy(x_refs[t].at[me], out_refs[t].at[me], local_sems.at[t]) for t in range(n)]
        for cp in local:
            cp.start()
        peers = []
        for k in range(1, N_DEV):
            pid = (1 - x if k & 4 else x, 1 - y if k & 2 else y, 1 - c if k & 1 else c)
            peers.append((4 * pid[0] + 2 * pid[1] + pid[2], pid))
        sends = []
        for t in range(n):
            for k, (peer, pid) in enumerate(peers):
                cp = pltpu.make_async_remote_copy(
                    src_ref=x_refs[t].at[peer], dst_ref=out_refs[t].at[me], send_sem=send_sems.at[N_PEERS * t + k],
                    recv_sem=recv_sems.at[N_PEERS * t + k], device_id=pid, device_id_type=MESH)
                cp.start()
                sends.append(cp)
        for t in range(n):
            for k, (peer, pid) in enumerate(peers):
                pltpu.make_async_remote_copy(
                    src_ref=x_refs[t].at[me], dst_ref=out_refs[t].at[peer], send_sem=send_sems.at[N_PEERS * t + k],
                    recv_sem=recv_sems.at[N_PEERS * t + k], device_id=pid, device_id_type=MESH).wait_recv()
        for cp in sends:
            cp.wait_send()
        for cp in local:
            cp.wait()

    return pl.pallas_call(
        body, out_shape=[jax.ShapeDtypeStruct(s.shape, s.dtype) for s in slabs],
        in_specs=[ANY] * n, out_specs=[ANY] * n,
        scratch_shapes=[pltpu.SemaphoreType.DMA((N_PEERS * n,)), pltpu.SemaphoreType.DMA((N_PEERS * n,)),
                        pltpu.SemaphoreType.DMA((n,))],
        name=name,
    )(*slabs)


def _adamw(recv, w, m, v, name):
    rows, cols = w.shape
    tr = _pick(rows, (256, 128, 64, 32, 16))

    def body(r_ref, w_ref, m_ref, v_ref, g_ref, d_ref, m2_ref, v2_ref):
        g = r_ref[0].astype(F32)
        for k in range(1, N_DEV):
            g = g + r_ref[k].astype(F32)
        m2 = ADAM_B1 * m_ref[...] + (1.0 - ADAM_B1) * g
        v2 = ADAM_B2 * v_ref[...] + (1.0 - ADAM_B2) * (g * g)
        m_hat = m2 / (1.0 - ADAM_B1 ** ADAM_STEP)
        v_hat = v2 / (1.0 - ADAM_B2 ** ADAM_STEP)
        g_ref[...] = g
        d_ref[...] = -ADAM_LR * (m_hat / (jnp.sqrt(v_hat) + ADAM_EPS) + ADAM_WD * w_ref[...])
        m2_ref[...] = m2
        v2_ref[...] = v2

    spec = pl.BlockSpec((tr, cols), lambda i: (i, 0))
    return pl.pallas_call(
        body, grid=(rows // tr,),
        in_specs=[pl.BlockSpec((N_DEV, tr, cols), lambda i: (0, i, 0)), spec, spec, spec],
        out_specs=[spec] * 4, out_shape=[jax.ShapeDtypeStruct((rows, cols), F32)] * 4,
        name=name, compiler_params=_params(("parallel",)),
    )(recv, w, m, v)


def _padded_rows(n, mult):
    rows = -(-n // PACK_COLS)
    return -(-rows // mult) * mult


def _pack(flat_pieces, mult, dtype, lead=()):
    flat = jnp.concatenate([p.astype(dtype) for p in flat_pieces], axis=-1)
    n = flat.shape[-1]
    rows = _padded_rows(n, mult)
    flat = jnp.pad(flat, [(0, 0)] * len(lead) + [(0, rows * PACK_COLS - n)])
    return flat.reshape(lead + (rows, PACK_COLS))


def _unpack(packed, shapes):
    flat = packed.reshape(-1)
    out, off = {}, 0
    for name, shape in shapes:
        size = 1
        for s in shape:
            size *= s
        out[name] = flat[off:off + size].reshape(shape)
        off += size
    return out


def _local_2d(arr):
    return arr.reshape(-1, arr.shape[-1])


def _to_slabs(full, shape, axis):
    n = shape[axis]
    if axis == len(shape) - 1:
        full2d = _local_2d(full)
        return jnp.stack([full2d[:, n * d:n * (d + 1)] for d in range(N_DEV)])
    assert axis == 1 and len(shape) == 3
    return jnp.stack([full[:, n * d:n * (d + 1)].reshape(-1, shape[2]) for d in range(N_DEV)])


def _from_gathered(g, shape, axis):
    if axis == len(shape) - 1:
        full2d = jnp.concatenate([g[d] for d in range(N_DEV)], axis=1)
        return full2d.reshape(shape[:-1] + (N_DEV * shape[-1],))
    assert axis == 1 and len(shape) == 3
    g4 = g.reshape((N_DEV,) + shape)
    return jnp.concatenate([g4[d] for d in range(N_DEV)], axis=1)


def _pad_cols(w, total):
    return jnp.pad(w, [(0, 0)] * (w.ndim - 1) + [(0, total - w.shape[-1])])


def _pad_lanes(v):
    return _pad_cols(v, LANE)


def _x(arr, width, cb0=0, moves=0):
    return (arr, width, cb0, moves)


def _rms_fwd(h, g):
    return _row_fwd(_rms_fn, [_x(h, D_MODEL)], [g], [], [(D_MODEL, D_MODEL, BF16)], "rms_fwd")[0]


def _rms_bwd(h, g, dy, add):
    dh, dg = _row_bwd(_rms_fn, [_x(h, D_MODEL)], [g], [], [_x(dy, D_MODEL)], [(D_MODEL, F32)], "rms_bwd", adds=[add])
    return dh, dg


SW_TILE, SW_WIDTH = 384, 256
SW_NCOL = D_FF // SW_WIDTH


def _ffn_fwd(h, g, wgu, wd):
    y = _rms_fwd(h, g)
    gu = _matmul(y, wgu, "nn", F32, "ffn_gate_up")
    a = _row_fwd(_swiglu_fn, [_x(gu, SW_WIDTH, 0, 1), _x(gu, SW_WIDTH, SW_NCOL, 1)], [], [],
                 [(SW_WIDTH, D_FF, BF16)], "swiglu_fwd", tm=SW_TILE, ncol=SW_NCOL)[0]
    h2 = _matmul(a, wd, "nn", F32, "ffn_down", add=h)
    return h2, (h, y, gu, a)


def _ffn_bwd(dh2, saved, g, wgu, wd):
    h, y, gu, a = saved
    da = _matmul(dh2, wd, "nt", F32, "ffn_down_dx")
    dwd = _matmul(a, dh2, "tn", F32, "ffn_down_dw")
    dgt, dup = _row_bwd(_swiglu_fn, [_x(gu, SW_WIDTH, 0, 1), _x(gu, SW_WIDTH, SW_NCOL, 1)], [], [],
                        [_x(da, SW_WIDTH, 0, 1)], [(D_FF, BF16), (D_FF, BF16)], "swiglu_bwd",
                        tm=SW_TILE, ncol=SW_NCOL)
    dgu = jnp.concatenate([dgt, dup], axis=1)
    dwgu = _matmul(y, dgu, "tn", F32, "ffn_gate_up_dw")
    dy = _matmul(dgu, wgu, "nt", F32, "ffn_gate_up_dx")
    dh, dg = _rms_bwd(h, g, dy, dh2)
    return dh, dg, dwgu, dwd


def _fox_consts():
    lane = jnp.arange(D_MODEL) // FOX_DH
    e_expand = (jnp.arange(LANE)[:, None] == lane[None, :]).astype(F32)
    e_mean = e_expand.T * (1.0 / FOX_DH)
    half = jnp.arange(LANE) // FOX_DH
    blockdiag = (half[:, None] == half[None, :]).astype(F32)
    return e_mean, e_expand, blockdiag


def _pair_layout(c):
    heads = c[:, :FOX_HEADS].T
    crow = heads.reshape(FOX_PAIRS, 2, ROWS)
    c2 = jnp.broadcast_to(crow.transpose(0, 2, 1)[:, :, :, None], (FOX_PAIRS, ROWS, 2, FOX_DH))
    crow = jnp.where(jnp.arange(ROWS) >= FIRST_ROW, crow, KEY_OFF)
    return c2.reshape(FOX_PAIRS, ROWS, LANE), crow


def _fox_fwd(y, p):
    proj = _matmul(y, p["w_in"], "nn", F32, "fox_in")
    e_mean, e_expand, _ = _fox_consts()
    nb = D_MODEL // LANE
    xs = [_x(proj, D_MODEL, 0), _x(proj, D_MODEL, 1), _x(proj, LANE, 4 * nb)]
    ps = [p["q_gain"], p["k_gain"], p["b_f"]]
    qn, kn, lf = _row_fwd(_fox_pre_fn, xs, ps, [e_mean, e_expand],
                          [(D_MODEL, D_MODEL, BF16), (D_MODEL, D_MODEL, BF16), (LANE, LANE, F32)], "fox_pre_fwd")
    c = _seq_cumsum(lf, False, "fox_cumsum")
    c2, crow = _pair_layout(c)
    o, lse2 = _fox_attn_fwd(qn, kn, proj, c2, crow)
    om = _row_fwd(_fox_post_fn, [_x(o, D_MODEL, 0), _x(proj, D_MODEL, 3)], [], [],
                  [(D_MODEL, D_MODEL, BF16)], "fox_post_fwd")[0]
    return om, (proj, qn, kn, c2, crow, o, lse2)


def _fox_bwd(dom, y, saved, p):
    proj, qn, kn, c2, crow, o, lse2 = saved
    e_mean, e_expand, blockdiag = _fox_consts()
    nb = D_MODEL // LANE
    do, dgate = _row_bwd(_fox_post_fn, [_x(o, D_MODEL, 0), _x(proj, D_MODEL, 3)], [], [], [_x(dom, D_MODEL)],
                         [(D_MODEL, F32), (D_MODEL, BF16)], "fox_post_bwd")
    dq, delta2, dcq2 = _fox_attn_dq(qn, kn, proj, c2, crow, lse2, o, do, blockdiag)
    dk, dv, dcrow = _fox_attn_dkv(qn, kn, proj, c2, crow, lse2, delta2, do)
    dcq = dcq2[:, :, ::FOX_DH].transpose(1, 0, 2).reshape(ROWS, FOX_HEADS)
    dc = _pad_cols(dcq + dcrow.reshape(FOX_HEADS, ROWS).T, LANE)
    dlf = _seq_cumsum(dc, True, "fox_cumsum_rev")
    xs = [_x(proj, D_MODEL, 0), _x(proj, D_MODEL, 1), _x(proj, LANE, 4 * nb)]
    ps = [p["q_gain"], p["k_gain"], p["b_f"]]
    dqr, dkr, df, dqg, dkg, dbf = _row_bwd(
        _fox_pre_fn, xs, ps, [e_mean, e_expand], [_x(dq, D_MODEL), _x(dk, D_MODEL), _x(dlf, LANE)],
        [(D_MODEL, BF16), (D_MODEL, BF16), (LANE, BF16)], "fox_pre_bwd")
    dproj = jnp.concatenate([dqr, dkr, dv, dgate, df], axis=1)
    grads = {
        "w_in": _matmul(y, dproj, "tn", F32, "fox_in_dw")[:, :FOX_IN],
        "q_gain": dqg.reshape(FOX_HEADS, FOX_DH).sum(axis=0),
        "k_gain": dkg.reshape(FOX_HEADS, FOX_DH).sum(axis=0),
        "b_f": dbf[0, :FOX_HEADS],
    }
    return _matmul(dproj, p["w_in"], "nt", F32, "fox_in_dx"), grads


def _gla_fwd(y, p):
    proj = _matmul(y, p["w_in"], "nn", F32, "gla_in")
    small = (2 * GLA_QK + 2 * GLA_V) // LANE
    garr = _row_fwd(_gla_pre_fn, [_x(proj, LANE, small)], [p["w_alpha"], p["b_alpha"]], [],
                    [(GLA_QK, GLA_QK, F32)], "gla_pre_fwd")[0]
    o, states = _gla_scan_fwd(proj, garr)
    om = _row_fwd(_gla_post_fn, [_x(o, GLA_V, 0), _x(proj, GLA_V, 2)], [p["o_gain"]], [],
                  [(GLA_V, GLA_V, BF16)], "gla_post_fwd")[0]
    return om, (proj, garr, states, o)


def _gla_bwd(dom, y, saved, p):
    proj, garr, states, o = saved
    small = (2 * GLA_QK + 2 * GLA_V) // LANE
    do, dr, dgain = _row_bwd(_gla_post_fn, [_x(o, GLA_V, 0), _x(proj, GLA_V, 2)], [p["o_gain"]], [],
                             [_x(dom, GLA_V)], [(GLA_V, F32), (GLA_V, BF16)], "gla_post_bwd")
    dq, dk, dv, dg = _gla_scan_bwd(proj, garr, states, do)
    dalr, dwa, dba = _row_bwd(_gla_pre_fn, [_x(proj, LANE, small)], [p["w_alpha"], p["b_alpha"]], [],
                              [_x(dg, GLA_QK)], [(LANE, BF16)], "gla_pre_bwd")
    dproj = jnp.concatenate([dq, dk, dv, dr, dalr], axis=1)
    grads = {
        "w_in": _matmul(y, dproj, "tn", F32, "gla_in_dw")[:, :GLA_IN],
        "w_alpha": dwa[:GLA_RANK], "b_alpha": dba[0], "o_gain": dgain[0],
    }
    return _matmul(dproj, p["w_in"], "nt", F32, "gla_in_dx"), grads


def _gdn_fwd(y, p):
    proj = _matmul(y, p["w_in"], "nn", F32, "gdn_in")
    nb = D_MODEL // LANE
    qn = _gdn_conv_fwd(proj, p["conv_q"], "q", 0)
    kn = _gdn_conv_fwd(proj, p["conv_k"], "k", 1)
    vn = _gdn_conv_fwd(proj, p["conv_v"], "v", 2)
    gates = _row_fwd(_gdn_gates_fn, [_x(proj, LANE, 4 * nb)], [p["a_log"], p["dt_bias"]], [],
                     [(LANE, LANE, F32)], "gdn_gates_fwd")[0]
    o, states = _gdn_scan_fwd(qn, kn, vn, gates)
    om = _row_fwd(_gdn_post_fn, [_x(o, D_MODEL, 0), _x(proj, D_MODEL, 3)], [p["o_gain"]], [],
                  [(D_MODEL, D_MODEL, BF16)], "gdn_post_fwd")[0]
    return om, (proj, qn, kn, vn, gates, states, o)


def _gdn_bwd(dom, y, saved, p):
    proj, qn, kn, vn, gates, states, o = saved
    nb = D_MODEL // LANE
    do, dgate, dgain = _row_bwd(_gdn_post_fn, [_x(o, D_MODEL, 0), _x(proj, D_MODEL, 3)], [p["o_gain"]], [],
                                [_x(dom, D_MODEL)], [(D_MODEL, F32), (D_MODEL, BF16)], "gdn_post_bwd")
    dqn, dkn, dvn, dgates = _gdn_scan_bwd(qn, kn, vn, gates, states, do)
    dsm, dalog, ddtb = _row_bwd(_gdn_gates_fn, [_x(proj, LANE, 4 * nb)], [p["a_log"], p["dt_bias"]], [],
                                [_x(dgates, LANE)], [(LANE, BF16)], "gdn_gates_bwd")
    dxq, dwq = _gdn_conv_bwd(proj, p["conv_q"], "q", 0, dqn)
    dxk, dwk = _gdn_conv_bwd(proj, p["conv_k"], "k", 1, dkn)
    dxv, dwv = _gdn_conv_bwd(proj, p["conv_v"], "v", 2, dvn)
    dproj = jnp.concatenate([dxq, dxk, dxv, dgate, dsm], axis=1)
    grads = {
        "w_in": _matmul(y, dproj, "tn", F32, "gdn_in_dw")[:, :GDN_IN],
        "conv_w": jnp.concatenate([dwq[:GDN_CONV], dwk[:GDN_CONV], dwv[:GDN_CONV]], axis=1),
        "a_log": dalog[0, :GDN_HEADS], "dt_bias": ddtb[0, :GDN_HEADS], "o_gain": dgain[0],
    }
    return _matmul(dproj, p["w_in"], "nt", F32, "gdn_in_dx"), grads


_MIXERS = ((_fox_fwd, _fox_bwd), (_gla_fwd, _gla_bwd), (_gdn_fwd, _gdn_bwd))


def _mixer_params(i, wts, rep):
    kind, j = i % 3, i // 3
    if kind == 0:
        return {
            "w_in": _pad_cols(wts["fox_w_in"][j], FOX_IN_PAD), "w_out": wts["fox_w_out"][j],
            "q_gain": jnp.tile(rep["fox_q_gain"][j], FOX_HEADS)[None],
            "k_gain": jnp.tile(rep["fox_k_gain"][j], FOX_HEADS)[None],
            "b_f": _pad_lanes(rep["fox_b_f"][j][None]),
        }
    if kind == 1:
        return {
            "w_in": _pad_cols(wts["gla_w_in"][j], GLA_IN_PAD), "w_out": wts["gla_w_out"][j],
            "w_alpha": jnp.pad(wts["gla_w_alpha2"][j], ((0, LANE - GLA_RANK), (0, 0))).astype(F32),
            "b_alpha": rep["gla_b_alpha"][j][None], "o_gain": rep["gla_o_gain"][j][None],
        }
    conv = jnp.pad(wts["gdn_conv_w"][j][:, 0, :].astype(F32), ((0, 8 - GDN_CONV), (0, 0)))
    return {
        "w_in": _pad_cols(wts["gdn_w_in"][j], GDN_IN_PAD), "w_out": wts["gdn_w_out"][j],
        "conv_q": conv[:, :D_MODEL], "conv_k": conv[:, D_MODEL:2 * D_MODEL], "conv_v": conv[:, 2 * D_MODEL:],
        "a_log": _pad_lanes(rep["gdn_a_log"][j][None]), "dt_bias": _pad_lanes(rep["gdn_dt_bias"][j][None]),
        "o_gain": rep["gdn_o_gain"][j][None],
    }


def kernel(x, meta_tokens, norm_mix, norm_ffn, w_gate_up, w_down, fox_w_in, fox_b_f, fox_q_gain, fox_k_gain, fox_w_out, gla_w_in, gla_w_alpha2, gla_b_alpha, gla_o_gain, gla_w_out, gdn_w_in, gdn_conv_w, gdn_a_log, gdn_dt_bias, gdn_o_gain, gdn_w_out, loss_target, m_meta_tokens, m_norm_mix, m_norm_ffn, m_w_gate_up, m_w_down, m_fox_w_in, m_fox_b_f, m_fox_q_gain, m_fox_k_gain, m_fox_w_out, m_gla_w_in, m_gla_w_alpha2, m_gla_b_alpha, m_gla_o_gain, m_gla_w_out, m_gdn_w_in, m_gdn_conv_w, m_gdn_a_log, m_gdn_dt_bias, m_gdn_o_gain, m_gdn_w_out, v_meta_tokens, v_norm_mix, v_norm_ffn, v_w_gate_up, v_w_down, v_fox_w_in, v_fox_b_f, v_fox_q_gain, v_fox_k_gain, v_fox_w_out, v_gla_w_in, v_gla_w_alpha2, v_gla_b_alpha, v_gla_o_gain, v_gla_w_out, v_gdn_w_in, v_gdn_conv_w, v_gdn_a_log, v_gdn_dt_bias, v_gdn_o_gain, v_gdn_w_out):
    given = dict(locals())
    w_loc = {n: given[n] for n in WEIGHT_ORDER}
    m_loc = {n: given["m_" + n] for n in WEIGHT_ORDER}
    v_loc = {n: given["v_" + n] for n in WEIGHT_ORDER}

    wire = {n: (BF16 if len(s) == 3 and s[1] * s[2] >= 1 << 16 else F32) for n, s, _ in SHARDED}
    gathered = _all_gather([_local_2d(w_loc[n]).astype(wire[n]) for n, _, _ in SHARDED], "all_gather_weights")
    wts = {n: _from_gathered(g, s, a) for (n, s, a), g in zip(SHARDED, gathered)}
    meta_full = wts["meta_tokens"]
    rep = {n: w_loc[n] for n, _ in REPLICATED}

    h = jnp.concatenate([jnp.zeros((FIRST_ROW, D_MODEL), F32), meta_full, x[0]], axis=0)
    saved = []
    for i in range(DEPTH):
        p = _mixer_params(i, wts, rep)
        fwd, _ = _MIXERS[i % 3]
        g_mix = rep["norm_mix"][i][None]
        y = _rms_fwd(h, g_mix)
        om, mix_saved = fwd(y, p)
        h_mid = _matmul(om, p["w_out"], "nn", F32, "mixer_out", add=h)
        g_ffn = rep["norm_ffn"][i][None]
        h_out, ffn_saved = _ffn_fwd(h_mid, g_ffn, wts["w_gate_up"][i], wts["w_down"][i])
        saved.append((p, h, y, om, mix_saved, ffn_saved))
        h = h_out

    dh, sq = _loss_head(h, loss_target[0])
    loss = lax.psum(0.5 * jnp.sum(sq) * (1.0 / D_MODEL), ("x", "y", "c"))

    gw = {n: [None] * DEPTH for n in ("w_gate_up", "w_down", "norm_mix", "norm_ffn")}
    gm = {}
    for i in reversed(range(DEPTH)):
        p, h_in, y, om, mix_saved, ffn_saved = saved[i]
        kind, j = i % 3, i // 3
        _, bwd = _MIXERS[kind]
        dh_mid, gw["norm_ffn"][i], gw["w_gate_up"][i], gw["w_down"][i] = _ffn_bwd(
            dh, ffn_saved, rep["norm_ffn"][i][None], wts["w_gate_up"][i], wts["w_down"][i])
        dom = _matmul(dh_mid, p["w_out"], "nt", F32, "mixer_out_dx")
        dwo = _matmul(om, dh_mid, "tn", F32, "mixer_out_dw")
        dy, grads = bwd(dom, y, mix_saved, p)
        grads["w_out"] = dwo
        gm[(kind, j)] = grads
        dh, gw["norm_mix"][i] = _rms_bwd(h_in, rep["norm_mix"][i][None], dy, dh_mid)

    full = {
        "meta_tokens": dh[FIRST_ROW:PAD_ROWS],
        "w_gate_up": jnp.stack(gw["w_gate_up"]), "w_down": jnp.stack(gw["w_down"]),
        "fox_w_in": jnp.stack([gm[(0, 0)]["w_in"], gm[(0, 1)]["w_in"]]),
        "fox_w_out": jnp.stack([gm[(0, 0)]["w_out"], gm[(0, 1)]["w_out"]]),
        "gla_w_in": gm[(1, 0)]["w_in"][None], "gla_w_alpha2": gm[(1, 0)]["w_alpha"][None],
        "gla_w_out": gm[(1, 0)]["w_out"][None],
        "gdn_w_in": gm[(2, 0)]["w_in"][None], "gdn_conv_w": gm[(2, 0)]["conv_w"][None, :, None, :],
        "gdn_w_out": gm[(2, 0)]["w_out"][None],
    }
    partial = {
        "norm_mix": jnp.concatenate(gw["norm_mix"], axis=0), "norm_ffn": jnp.concatenate(gw["norm_ffn"], axis=0),
        "fox_b_f": jnp.stack([gm[(0, 0)]["b_f"], gm[(0, 1)]["b_f"]]),
        "fox_q_gain": jnp.stack([gm[(0, 0)]["q_gain"], gm[(0, 1)]["q_gain"]]),
        "fox_k_gain": jnp.stack([gm[(0, 0)]["k_gain"], gm[(0, 1)]["k_gain"]]),
        "gla_b_alpha": gm[(1, 0)]["b_alpha"][None], "gla_o_gain": gm[(1, 0)]["o_gain"][None],
        "gdn_a_log": gm[(2, 0)]["a_log"][None], "gdn_dt_bias": gm[(2, 0)]["dt_bias"][None],
        "gdn_o_gain": gm[(2, 0)]["o_gain"][None],
    }

    full_shapes = {n: s[:a] + (N_DEV * s[a],) + s[a + 1:] for n, s, a in SHARDED}
    small = _pack([partial[n].reshape(-1) for n, _ in REPLICATED], 8, F32)
    slabs = [_to_slabs(full[n].reshape(full_shapes[n]), s, a).astype(wire[n]) for n, s, a in SHARDED]
    slabs.append(jnp.broadcast_to(small[None], (N_DEV,) + small.shape))
    recv = _exchange(slabs, "exchange_grads")

    res = [{} for _ in range(4)]
    for (n, s, _), r in zip(SHARDED, recv[:-1]):
        outs = _adamw(r, *[_local_2d(d[n]) for d in (w_loc, m_loc, v_loc)], "adamw_" + n)
        for k in range(4):
            res[k][n] = outs[k].reshape(s)
    outs_small = _adamw(recv[-1], *[_pack([d[n].reshape(-1) for n, _ in REPLICATED], 8, F32)
                                    for d in (w_loc, m_loc, v_loc)], "adamw_replicated")
    for k in range(4):
        res[k].update(_unpack(outs_small[k], list(REPLICATED)))

    result = [loss, dh[PAD_ROWS:][None]]
    for k in range(4):
        result += [res[k][n] for n in WEIGHT_ORDER]
    return tuple(result)
```

```python
import functools

import jax
import jax.numpy as jnp
from jax import lax
from jax.experimental import pallas as pl
from jax.experimental.pallas import tpu as pltpu

F32, BF16 = jnp.float32, jnp.bfloat16
HIGHEST = lax.Precision.HIGHEST

D_MODEL = 1024
SEQ = 4096
N_META = 16
DEPTH = 4
NORM_EPS = 1e-6
LANE = 128
PAD_ROWS = 128
FIRST_ROW = PAD_ROWS - N_META
ROWS = PAD_ROWS + SEQ
ROW_TILE = 128
CHUNK = 64
NEG = -1e30
N_DEV = 8
VMEM_LIMIT = 56 * 1024 * 1024
VMEM_BLOCK_BUDGET = 36 * 1024 * 1024

FOX_HEADS, FOX_DH = 16, 64
FOX_IN, FOX_IN_PAD = 4 * D_MODEL + FOX_HEADS, 4 * D_MODEL + LANE
GLA_HEADS, GLA_DK, GLA_DV, GLA_RANK = 4, 128, 256, 16
GLA_QK, GLA_V = GLA_HEADS * GLA_DK, GLA_HEADS * GLA_DV
GLA_IN, GLA_IN_PAD = 2 * GLA_QK + 2 * GLA_V + GLA_RANK, 2 * GLA_QK + 2 * GLA_V + LANE
GDN_HEADS, GDN_DK, GDN_DV, GDN_CONV = 8, 128, 128, 4
GDN_CONV_DIM = 3 * GDN_HEADS * GDN_DK
GDN_IN, GDN_IN_PAD = GDN_CONV_DIM + GDN_HEADS * GDN_DV + 2 * GDN_HEADS, GDN_CONV_DIM + GDN_HEADS * GDN_DV + LANE
D_FF = 2816

ADAM_LR, ADAM_B1, ADAM_B2, ADAM_EPS, ADAM_WD, ADAM_STEP = 0.001, 0.9, 0.999, 1e-08, 0.01, 10

PACK_COLS = 1024
PACK_ROW_TILE = 256

SHARDED = (
    ("meta_tokens", (16, 128), 1),
    ("w_gate_up", (4, 1024, 704), 2),
    ("w_down", (4, 352, 1024), 1),
    ("fox_w_in", (2, 1024, 514), 2),
    ("fox_w_out", (2, 128, 1024), 1),
    ("gla_w_in", (1, 1024, 386), 2),
    ("gla_w_alpha2", (1, 16, 64), 2),
    ("gla_w_out", (1, 128, 1024), 1),
    ("gdn_w_in", (1, 1024, 514), 2),
    ("gdn_conv_w", (1, 4, 1, 384), 3),
    ("gdn_w_out", (1, 128, 1024), 1),
)
REPLICATED = (
    ("norm_mix", (4, 1024)),
    ("norm_ffn", (4, 1024)),
    ("fox_b_f", (2, 16)),
    ("fox_q_gain", (2, 64)),
    ("fox_k_gain", (2, 64)),
    ("gla_b_alpha", (1, 512)),
    ("gla_o_gain", (1, 256)),
    ("gdn_a_log", (1, 8)),
    ("gdn_dt_bias", (1, 8)),
    ("gdn_o_gain", (1, 128)),
)
WEIGHT_ORDER = (
    "meta_tokens", "norm_mix", "norm_ffn", "w_gate_up", "w_down", "fox_w_in", "fox_b_f", "fox_q_gain",
    "fox_k_gain", "fox_w_out", "gla_w_in", "gla_w_alpha2", "gla_b_alpha", "gla_o_gain", "gla_w_out",
    "gdn_w_in", "gdn_conv_w", "gdn_a_log", "gdn_dt_bias", "gdn_o_gain", "gdn_w_out",
)

_DN = {
    "nn": (((1,), (0,)), ((), ())),
    "nt": (((1,), (1,)), ((), ())),
    "tn": (((0,), (0,)), ((), ())),
}


def _dot(a, b, kind):
    return lax.dot_general(a.astype(BF16), b.astype(BF16), _DN[kind], preferred_element_type=F32)


@functools.partial(jax.custom_vjp, nondiff_argnums=(2,))
def _dot_ad(a, b, kind):
    return _dot(a, b, kind)


def _dot_ad_fwd(a, b, kind):
    return _dot(a, b, kind), (a, b)


def _dot_ad_bwd(kind, res, ct):
    a, b = res
    if kind == "nn":
        return _dot(ct, b, "nt"), _dot(a, ct, "tn")
    if kind == "nt":
        return _dot(ct, b, "nn"), _dot(ct, a, "tn")
    return _dot(b, ct, "nt"), _dot(a, ct, "nn")


_dot_ad.defvjp(_dot_ad_fwd, _dot_ad_bwd)


def _dot_hi(a, b, kind):
    return lax.dot_general(a, b, _DN[kind], precision=HIGHEST, preferred_element_type=F32)


def _dot_mid(a, b, kind):
    return lax.dot_general(a, b, _DN[kind], precision=lax.Precision.HIGH, preferred_element_type=F32)


def _iota(shape, dim):
    return lax.broadcasted_iota(jnp.int32, shape, dim)


def _row_mask(i, tm):
    return ((i * tm + _iota((tm, 1), 0)) >= FIRST_ROW).astype(F32)


def _log_sigmoid(z):
    return jnp.minimum(z, 0.0) - jnp.log(1.0 + jnp.exp(-jnp.abs(z)))


def _softplus(z):
    return jnp.maximum(z, 0.0) + jnp.log(1.0 + jnp.exp(-jnp.abs(z)))


def _silu(z):
    return z * jax.nn.sigmoid(z)


def _lane_col(x, lane):
    return jnp.sum(jnp.where(_iota(x.shape, 1) == lane, x, 0.0), axis=1, keepdims=True)


def _last_row(x):
    return jnp.sum(jnp.where(_iota(x.shape, 0) == x.shape[0] - 1, x, 0.0), axis=0, keepdims=True)


def _params(sem, limit=VMEM_LIMIT):
    return pltpu.CompilerParams(dimension_semantics=sem, vmem_limit_bytes=limit)


def _pick(n, options):
    for t in options:
        if n % t == 0:
            return t
    return n


def _matmul(a, b, kind, out_dtype, name, add=None):
    if kind == "nn":
        (m, k), n = a.shape, b.shape[1]
    elif kind == "nt":
        (m, k), n = a.shape, b.shape[0]
    else:
        (k, m), n = a.shape, b.shape[1]
    tn = _pick(n, (512, 384, 640, 256, 128))
    tm_options = (1408, 384, 128) if kind != "tn" else (256, 128)
    tm = None
    for t in tm_options:
        if m % t:
            continue
        need = 2 * (t * k * a.dtype.itemsize + tn * k * b.dtype.itemsize + t * tn * 8)
        if need <= VMEM_BLOCK_BUDGET:
            tm = t
            break
    assert tm is not None, (name, a.shape, b.shape)

    def body(*refs):
        a_ref, b_ref = refs[0], refs[1]
        o_ref = refs[-1]
        acc = _dot(a_ref[...], b_ref[...], kind)
        if add is not None:
            acc = acc + refs[2][...]
        o_ref[...] = acc.astype(o_ref.dtype)

    if kind == "tn":
        a_spec = pl.BlockSpec((k, tm), lambda i, j: (0, i))
    else:
        a_spec = pl.BlockSpec((tm, k), lambda i, j: (i, 0))
    if kind == "nt":
        b_spec = pl.BlockSpec((tn, k), lambda i, j: (j, 0))
    else:
        b_spec = pl.BlockSpec((k, tn), lambda i, j: (0, j))
    o_spec = pl.BlockSpec((tm, tn), lambda i, j: (i, j))
    operands, in_specs = [a, b], [a_spec, b_spec]
    if add is not None:
        operands.append(add)
        in_specs.append(o_spec)
    return pl.pallas_call(
        body, grid=(m // tm, n // tn), in_specs=in_specs, out_specs=o_spec,
        out_shape=jax.ShapeDtypeStruct((m, n), out_dtype), name=name,
        compiler_params=_params(("parallel", "parallel")),
    )(*operands)


def _x_spec(tm, width, cb0, moves):
    return pl.BlockSpec((tm, width), lambda i, j: (i, cb0 + j * moves))


def _full_spec(arr):
    nd = arr.ndim
    return pl.BlockSpec(arr.shape, lambda i, j: (0,) * nd)


def _row_fwd(fn, xs, ps, cs, outs, name, tm=ROW_TILE, ncol=1, rows=None):
    rows = ROWS if rows is None else rows
    nx, npar, nc = len(xs), len(ps), len(cs)

    def body(*refs):
        i = pl.program_id(0)
        xv = [r[...].astype(F32) for r in refs[:nx]]
        pv = [r[...] for r in refs[nx:nx + npar + nc]]
        res = fn(i, *xv, *pv)
        for r, val in zip(refs[nx + npar + nc:], res):
            r[...] = val.astype(r.dtype)

    in_specs = [_x_spec(tm, w, cb0, mv) for (_, w, cb0, mv) in xs]
    in_specs += [_full_spec(p) for p in list(ps) + list(cs)]
    out_specs = [_x_spec(tm, w, 0, 1) for (w, _, _) in outs]
    out_shape = [jax.ShapeDtypeStruct((rows, tot), dt) for (_, tot, dt) in outs]
    return pl.pallas_call(
        body, grid=(rows // tm, ncol), in_specs=in_specs, out_specs=out_specs, out_shape=out_shape,
        name=name, compiler_params=_params(("parallel", "parallel")),
    )(*[x[0] for x in xs], *ps, *cs)


def _row_bwd(fn, xs, ps, cs, cts, dxs, name, adds=None, tm=ROW_TILE, ncol=1, rows=None):
    rows = ROWS if rows is None else rows
    nx, npar, nc, nct = len(xs), len(ps), len(cs), len(cts)
    adds = [None] * nx if adds is None else adds
    add_idx = [k for k in range(nx) if adds[k] is not None]

    def body(*refs):
        i, j = pl.program_id(0), pl.program_id(1)
        pos = 0
        x_refs = refs[pos:pos + nx]; pos += nx
        p_refs = refs[pos:pos + npar]; pos += npar
        c_refs = refs[pos:pos + nc]; pos += nc
        ct_refs = refs[pos:pos + nct]; pos += nct
        add_refs = refs[pos:pos + len(add_idx)]; pos += len(add_idx)
        dx_refs = refs[pos:pos + nx]; pos += nx
        dp_refs = refs[pos:pos + npar]
        xv = [r[...].astype(F32) for r in x_refs]
        pv = [r[...] for r in p_refs]
        cv = [r[...] for r in c_refs]
        _, vjp = jax.vjp(lambda *args: fn(i, *args, *cv), *xv, *pv)
        grads = vjp(tuple(r[...].astype(F32) for r in ct_refs))
        for k in range(nx):
            val = grads[k]
            if adds[k] is not None:
                val = val + add_refs[add_idx.index(k)][...]
            dx_refs[k][...] = val.astype(dx_refs[k].dtype)
        if npar:
            @pl.when((i == 0) & (j == 0))
            def _():
                for r in dp_refs:
                    r[...] = jnp.zeros(r.shape, r.dtype)

            for k in range(npar):
                dp_refs[k][...] += grads[nx + k]

    in_specs = [_x_spec(tm, w, cb0, mv) for (_, w, cb0, mv) in xs]
    in_specs += [_full_spec(p) for p in list(ps) + list(cs)]
    in_specs += [_x_spec(tm, w, cb0, mv) for (_, w, cb0, mv) in cts]
    in_specs += [_x_spec(tm, xs[k][1], 0, 1) for k in add_idx]
    out_specs = [_x_spec(tm, xs[k][1], 0, 1) for k in range(nx)] + [_full_spec(p) for p in ps]
    out_shape = [jax.ShapeDtypeStruct((rows, tot), dt) for (tot, dt) in dxs]
    out_shape += [jax.ShapeDtypeStruct(p.shape, F32) for p in ps]
    sem = ("arbitrary", "arbitrary") if npar else ("parallel", "parallel")
    return pl.pallas_call(
        body, grid=(rows // tm, ncol), in_specs=in_specs, out_specs=out_specs, out_shape=out_shape,
        name=name, compiler_params=_params(sem),
    )(*[x[0] for x in xs], *ps, *cs, *[c[0] for c in cts], *[adds[k] for k in add_idx])


def _rms_fn(i, h, g):
    return (h * lax.rsqrt(jnp.mean(h * h, axis=1, keepdims=True) + NORM_EPS) * g,)


def _swiglu_fn(i, gt, up):
    return (_silu(gt) * up,)


def _fox_pre_fn(i, q, k, f, qg, kg, bf, e_mean, e_expand):
    tm = q.shape[0]

    def headnorm(x, gain, scale):
        msq = _dot_hi(x * x, e_mean, "nn")
        r = _dot_hi(lax.rsqrt(msq + NORM_EPS), e_expand, "nn")
        return x * r * gain * scale

    qn = headnorm(q, qg, FOX_DH ** -0.5)
    kn = headnorm(k, kg, 1.0)
    keep = (_iota((tm, LANE), 1) < FOX_HEADS).astype(F32) * _row_mask(i, tm)
    return qn, kn, _log_sigmoid(f + bf) * keep


def _fox_post_fn(i, o, gate):
    return (o * jax.nn.sigmoid(gate) * _row_mask(i, o.shape[0]),)


def _gla_pre_fn(i, alr, wa, ba):
    z = _dot_ad(alr, wa, "nn") + ba
    return (_log_sigmoid(z) * (1.0 / 16.0) * _row_mask(i, alr.shape[0]),)


def _headwise_norm_gate(o, gate, gain, dv, mask):
    pieces = []
    for h in range(o.shape[1] // dv):
        oh = o[:, h * dv:(h + 1) * dv]
        pieces.append(oh * lax.rsqrt(jnp.mean(oh * oh, axis=1, keepdims=True) + NORM_EPS) * gain)
    return jnp.concatenate(pieces, axis=1) * _silu(gate) * mask


def _gla_post_fn(i, o, r, gain):
    return (_headwise_norm_gate(o, r, gain, GLA_DV, _row_mask(i, o.shape[0])),)


def _gdn_post_fn(i, o, gate, gain):
    return (_headwise_norm_gate(o, gate, gain, GDN_DV, _row_mask(i, o.shape[0])),)


def _gdn_gates_fn(i, sm, alog, dtb):
    tm = sm.shape[0]
    lane = _iota((tm, LANE), 1)
    g = -jnp.exp(alog) * _softplus(sm + dtb)
    beta = jax.nn.sigmoid(sm)
    out = jnp.where(lane < GDN_HEADS, g, jnp.where(lane < 2 * GDN_HEADS, beta, 0.0))
    return (out * _row_mask(i, tm),)


def _gdn_conv_fn(kind):
    def fn(i, prev, cur, w):
        tm = cur.shape[0]
        xw = jnp.concatenate([prev * jnp.where(i > 0, 1.0, 0.0), cur], axis=0)
        t = _iota((tm, 2 * tm), 0)
        s = _iota((tm, 2 * tm), 1)
        y = jnp.zeros(cur.shape, F32)
        for j in range(GDN_CONV):
            sel = (s == t + (tm - (GDN_CONV - 1) + j)).astype(F32)
            y = y + _dot_mid(sel, xw, "nn") * jnp.sum(jnp.where(_iota(w.shape, 0) == j, w, 0.0), axis=0, keepdims=True)
        a = _silu(y * _row_mask(i, tm))
        if kind == "v":
            return (a,)
        scale = GDN_DK ** -0.5 if kind == "q" else 1.0
        pieces = []
        for h in range(GDN_HEADS):
            ah = a[:, h * GDN_DK:(h + 1) * GDN_DK]
            pieces.append(ah * lax.rsqrt(jnp.sum(ah * ah, axis=1, keepdims=True) + NORM_EPS) * scale)
        return (jnp.concatenate(pieces, axis=1),)

    return fn


def _seq_cumsum(x, reverse, name):
    rows, width = x.shape
    nt = rows // ROW_TILE

    def body(x_ref, o_ref, carry):
        @pl.when(pl.program_id(0) == 0)
        def _():
            carry[...] = jnp.zeros(carry.shape, F32)

        r = _iota((ROW_TILE, ROW_TILE), 0)
        c = _iota((ROW_TILE, ROW_TILE), 1)
        tri = ((c >= r) if reverse else (c <= r)).astype(F32)
        acc = _dot_hi(tri, x_ref[...], "nn") + carry[...]
        o_ref[...] = acc
        edge = 0 if reverse else ROW_TILE - 1
        carry[...] = jnp.sum(jnp.where(_iota(acc.shape, 0) == edge, acc, 0.0), axis=0, keepdims=True)

    idx = (lambda i: (nt - 1 - i, 0)) if reverse else (lambda i: (i, 0))
    return pl.pallas_call(
        body, grid=(nt,), in_specs=[pl.BlockSpec((ROW_TILE, width), idx)],
        out_specs=pl.BlockSpec((ROW_TILE, width), idx), out_shape=jax.ShapeDtypeStruct(x.shape, F32),
        scratch_shapes=[pltpu.VMEM((1, width), F32)], name=name, compiler_params=_params(("arbitrary",)),
    )(x)


FOX_PAIRS = FOX_HEADS // 2
FOX_V_BLOCK = 2 * D_MODEL // LANE


def _pair_cols(x):
    return _lane_col(x, 0), _lane_col(x, FOX_DH)


ATT_TILE = 384
assert ROWS % ATT_TILE == 0
N_ATT = ROWS // ATT_TILE
KEY_OFF = 1e30


def _fox_scores(qa, qb, kj, ca, cb, cra, crb, mask):
    sa = _dot(qa, kj, "nt") + (ca - cra)
    sb = _dot(qb, kj, "nt") + (cb - crb)
    if mask is None:
        return sa, sb
    return jnp.where(mask, sa, NEG), jnp.where(mask, sb, NEG)


def _fox_diag_mask():
    return _iota((ATT_TILE, ATT_TILE), 1) <= _iota((ATT_TILE, ATT_TILE), 0)


def _att_off(j):
    return pl.multiple_of(j * ATT_TILE, ATT_TILE)


def _fox_attn_fwd(qn, kn, proj, c2, crow):
    nt = ROWS // ROW_TILE

    def body(q_ref, k_ref, v_ref, c2_ref, crow_ref, o_ref, lse_ref):
        i = pl.program_id(1)
        q = q_ref[...]
        is_a = _iota((ATT_TILE, LANE), 1) < FOX_DH
        qa, qb = jnp.where(is_a, q, 0), jnp.where(is_a, 0, q)
        ca, cb = _pair_cols(c2_ref[0])

        def block(off, size, mask, carry):
            ma, la, acca, mb, lb, accb = carry
            kj = k_ref[pl.ds(off, size), :]
            vj = v_ref[pl.ds(off, size), :].astype(BF16)
            sa, sb = _fox_scores(qa, qb, kj, ca, cb, crow_ref[0, 0:1, pl.ds(off, size)],
                                 crow_ref[0, 1:2, pl.ds(off, size)], mask)

            def online(s, m, l, acc):
                m_new = jnp.maximum(m, jnp.max(s, axis=1, keepdims=True))
                p = jnp.exp(s - m_new)
                alpha = jnp.exp(m - m_new)
                return m_new, alpha * l + jnp.sum(p, axis=1, keepdims=True), alpha * acc + _dot(p, vj, "nn")

            return online(sa, ma, la, acca) + online(sb, mb, lb, accb)

        m0 = jnp.full((ATT_TILE, 1), NEG, F32)
        l0 = jnp.zeros((ATT_TILE, 1), F32)
        a0 = jnp.zeros((ATT_TILE, LANE), F32)
        carry = lax.fori_loop(0, i, lambda j, cr: block(_att_off(j), ATT_TILE, None, cr), (m0, l0, a0, m0, l0, a0))
        ma, la, acca, mb, lb, accb = block(_att_off(i), ATT_TILE, _fox_diag_mask(), carry)
        o_ref[...] = jnp.where(is_a, acca / la, accb / lb)
        lse_ref[0] = jnp.where(is_a, ma + jnp.log(la), mb + jnp.log(lb))

    return pl.pallas_call(
        body, grid=(FOX_PAIRS, N_ATT),
        in_specs=[
            pl.BlockSpec((ATT_TILE, LANE), lambda g, i: (i, g)),
            pl.BlockSpec((ROWS, LANE), lambda g, i: (0, g)),
            pl.BlockSpec((ROWS, LANE), lambda g, i: (0, FOX_V_BLOCK + g)),
            pl.BlockSpec((1, ATT_TILE, LANE), lambda g, i: (g, i, 0)),
            pl.BlockSpec((1, 2, ROWS), lambda g, i: (g, 0, 0)),
        ],
        out_specs=[
            pl.BlockSpec((ATT_TILE, LANE), lambda g, i: (i, g)),
            pl.BlockSpec((1, ATT_TILE, LANE), lambda g, i: (g, i, 0)),
        ],
        out_shape=[jax.ShapeDtypeStruct((ROWS, D_MODEL), F32), jax.ShapeDtypeStruct((FOX_PAIRS, ROWS, LANE), F32)],
        name="fox_attn_fwd", compiler_params=_params(("parallel", "arbitrary")),
    )(qn, kn, proj, c2, crow)


def _fox_attn_dq(qn, kn, proj, c2, crow, lse2, o, do, blockdiag):
    nt = ROWS // ROW_TILE

    def body(q_ref, k_ref, v_ref, c2_ref, crow_ref, lse_ref, o_ref, do_ref, bd_ref, dq_ref, delta_ref, dcq_ref):
        i = pl.program_id(1)
        q = q_ref[...]
        is_a = _iota((ATT_TILE, LANE), 1) < FOX_DH
        qa, qb = jnp.where(is_a, q, 0), jnp.where(is_a, 0, q)
        ca, cb = _pair_cols(c2_ref[0])
        lsa, lsb = _pair_cols(lse_ref[0])
        do_t = do_ref[...]
        delta2 = _dot_hi(do_t * o_ref[...], bd_ref[...], "nn")
        delta_ref[0] = delta2
        dla, dlb = _pair_cols(delta2)
        doa, dob = jnp.where(is_a, do_t, 0.0).astype(BF16), jnp.where(is_a, 0.0, do_t).astype(BF16)

        def block(off, size, mask, carry):
            dqa, dqb, rsa, rsb = carry
            kj = k_ref[pl.ds(off, size), :]
            vj = v_ref[pl.ds(off, size), :].astype(BF16)
            sa, sb = _fox_scores(qa, qb, kj, ca, cb, crow_ref[0, 0:1, pl.ds(off, size)],
                                 crow_ref[0, 1:2, pl.ds(off, size)], mask)
            dsa = jnp.exp(sa - lsa) * (_dot(doa, vj, "nt") - dla)
            dsb = jnp.exp(sb - lsb) * (_dot(dob, vj, "nt") - dlb)
            return (dqa + _dot(dsa, kj, "nn"), dqb + _dot(dsb, kj, "nn"),
                    rsa + jnp.sum(dsa, axis=1, keepdims=True), rsb + jnp.sum(dsb, axis=1, keepdims=True))

        z = jnp.zeros((ATT_TILE, LANE), F32)
        zc = jnp.zeros((ATT_TILE, 1), F32)
        carry = lax.fori_loop(0, i, lambda j, cr: block(_att_off(j), ATT_TILE, None, cr), (z, z, zc, zc))
        dqa, dqb, rsa, rsb = block(_att_off(i), ATT_TILE, _fox_diag_mask(), carry)
        dq_ref[...] = jnp.where(is_a, dqa, dqb)
        dcq_ref[0] = jnp.where(is_a, rsa, rsb)

    tile = pl.BlockSpec((ATT_TILE, LANE), lambda g, i: (i, g))
    pair_tile = pl.BlockSpec((1, ATT_TILE, LANE), lambda g, i: (g, i, 0))
    return pl.pallas_call(
        body, grid=(FOX_PAIRS, N_ATT),
        in_specs=[
            tile,
            pl.BlockSpec((ROWS, LANE), lambda g, i: (0, g)),
            pl.BlockSpec((ROWS, LANE), lambda g, i: (0, FOX_V_BLOCK + g)),
            pair_tile,
            pl.BlockSpec((1, 2, ROWS), lambda g, i: (g, 0, 0)),
            pair_tile, tile, tile,
            pl.BlockSpec((LANE, LANE), lambda g, i: (0, 0)),
        ],
        out_specs=[tile, pair_tile, pair_tile],
        out_shape=[jax.ShapeDtypeStruct((ROWS, D_MODEL), F32), jax.ShapeDtypeStruct((FOX_PAIRS, ROWS, LANE), F32),
                   jax.ShapeDtypeStruct((FOX_PAIRS, ROWS, LANE), F32)],
        name="fox_attn_dq", compiler_params=_params(("parallel", "arbitrary")),
    )(qn, kn, proj, c2, crow, lse2, o, do, blockdiag)


def _fox_attn_dkv(qn, kn, proj, c2, crow, lse2, delta2, do):
    nt = ROWS // ROW_TILE

    def body(q_ref, k_ref, v_ref, c2_ref, crow_ref, lse_ref, delta_ref, do_ref, dk_ref, dv_ref, dc_ref):
        j = pl.program_id(1)
        kj = k_ref[...]
        vj = v_ref[...].astype(BF16)
        cra, crb = crow_ref[0, 0:1, :], crow_ref[0, 1:2, :]

        def block(off, size, mask, carry):
            dk, dv, dca, dcb = carry
            is_a = _iota((size, LANE), 1) < FOX_DH
            q = q_ref[pl.ds(off, size), :]
            qa, qb = jnp.where(is_a, q, 0), jnp.where(is_a, 0, q)
            do_t = do_ref[pl.ds(off, size), :]
            doa, dob = jnp.where(is_a, do_t, 0.0).astype(BF16), jnp.where(is_a, 0.0, do_t).astype(BF16)
            ca, cb = _pair_cols(c2_ref[0, pl.ds(off, size), :])
            lsa, lsb = _pair_cols(lse_ref[0, pl.ds(off, size), :])
            dla, dlb = _pair_cols(delta_ref[0, pl.ds(off, size), :])
            sa, sb = _fox_scores(qa, qb, kj, ca, cb, cra, crb, mask)
            pa, pb = jnp.exp(sa - lsa), jnp.exp(sb - lsb)
            dsa = pa * (_dot(doa, vj, "nt") - dla)
            dsb = pb * (_dot(dob, vj, "nt") - dlb)
            dv = dv + _dot(pa, doa, "tn") + _dot(pb, dob, "tn")
            dk = dk + _dot(dsa, qa, "tn") + _dot(dsb, qb, "tn")
            dca = dca - jnp.sum(dsa, axis=0, keepdims=True)
            dcb = dcb - jnp.sum(dsb, axis=0, keepdims=True)
            return dk, dv, dca, dcb

        z = jnp.zeros((ATT_TILE, LANE), F32)
        zr = jnp.zeros((1, ATT_TILE), F32)
        carry = block(_att_off(j), ATT_TILE, _fox_diag_mask(), (z, z, zr, zr))
        dk, dv, dca, dcb = lax.fori_loop(j + 1, N_ATT, lambda i, cr: block(_att_off(i), ATT_TILE, None, cr), carry)
        dk_ref[...] = dk
        dv_ref[...] = dv.astype(dv_ref.dtype)
        dc_ref[0, 0:1, :] = dca
        dc_ref[0, 1:2, :] = dcb

    whole = pl.BlockSpec((ROWS, LANE), lambda g, j: (0, g))
    pair_whole = pl.BlockSpec((1, ROWS, LANE), lambda g, j: (g, 0, 0))
    tile = pl.BlockSpec((ATT_TILE, LANE), lambda g, j: (j, g))
    return pl.pallas_call(
        body, grid=(FOX_PAIRS, N_ATT),
        in_specs=[
            whole, tile,
            pl.BlockSpec((ATT_TILE, LANE), lambda g, j: (j, FOX_V_BLOCK + g)),
            pair_whole,
            pl.BlockSpec((1, 2, ATT_TILE), lambda g, j: (g, 0, j)),
            pair_whole, pair_whole, whole,
        ],
        out_specs=[tile, tile, pl.BlockSpec((1, 2, ATT_TILE), lambda g, j: (g, 0, j))],
        out_shape=[jax.ShapeDtypeStruct((ROWS, D_MODEL), F32), jax.ShapeDtypeStruct((ROWS, D_MODEL), BF16),
                   jax.ShapeDtypeStruct((FOX_PAIRS, 2, ROWS), F32)],
        name="fox_attn_dkv", compiler_params=_params(("parallel", "arbitrary")),
    )(qn, kn, proj, c2, crow, lse2, delta2, do)


def _gla_chunk(q, k, v, g, st, dot):
    c = q.shape[0]
    incl = _iota((c, c), 1) <= _iota((c, c), 0)
    b = _dot_hi(incl.astype(F32), g, "nn")
    qe = q * (GLA_DK ** -0.5) * jnp.exp(b)
    ke = k * jnp.exp(-b)
    att = jnp.where(incl, dot(qe, ke, "nt"), 0.0)
    o = dot(att, v, "nn") + dot(qe, st, "nt")
    b_last = _last_row(b)
    kd = k * jnp.exp(b_last - b)
    return o, st * jnp.exp(b_last) + dot(v, kd, "tn")


def _tri_inverse(low):
    c = low.shape[0]
    eye = (_iota((c, c), 0) == _iota((c, c), 1)).astype(F32)
    x, p = eye - low, low
    span = 2
    while span < c:
        p = _dot_mid(p, p, "nn")
        x = _dot_mid(x, eye + p, "nn")
        span *= 2
    return x


def _solve(low, rhs):
    return _dot_mid(_tri_inverse(low), rhs, "nn")


@jax.custom_vjp
def _solve_ad(low, rhs):
    return _solve(low, rhs)


def _solve_ad_fwd(low, rhs):
    inv = _tri_inverse(low)
    sol = _dot_mid(inv, rhs, "nn")
    return sol, (inv, sol)


def _solve_ad_bwd(res, ct):
    inv, sol = res
    drhs = _dot_mid(inv, ct, "tn")
    return -_dot_mid(drhs, sol, "nt"), drhs


_solve_ad.defvjp(_solve_ad_fwd, _solve_ad_bwd)


def _gdn_chunk(q, k, v, gates, st, h, dot, solve):
    c = q.shape[0]
    gcol = _lane_col(gates, h)
    beta = _lane_col(gates, h + GDN_HEADS)
    incl = _iota((c, c), 1) <= _iota((c, c), 0)
    strict = _iota((c, c), 1) < _iota((c, c), 0)
    incl_f = incl.astype(F32)
    b = _dot_hi(incl_f, jnp.broadcast_to(gcol, (c, LANE)), "nn")
    g_sq = jnp.broadcast_to(gcol, (c, c))
    b_col = _dot_hi(incl_f, g_sq, "nn")
    b_row = _dot_hi(g_sq, (_iota((c, c), 0) <= _iota((c, c), 1)).astype(F32), "tn")
    diff = b_col - b_row
    kb, vb = k * beta, v * beta
    low = dot(kb, k, "nt") * jnp.exp(jnp.where(strict, diff, NEG))
    eb = jnp.exp(b)
    sol = solve(low, jnp.concatenate([vb, kb * eb], axis=1))
    v_new = sol[:, :GDN_DV] - dot(sol[:, GDN_DV:], st, "nn")
    att = dot(q, k, "nt") * jnp.exp(jnp.where(incl, diff, NEG))
    o = dot(att, v_new, "nn") + dot(q * eb, st, "nn")
    b_last = _last_row(b)
    kd = k * jnp.exp(b_last - b)
    return o, st * jnp.exp(b_last) + dot(kd, v_new, "tn")


N_CHUNKS = ROWS // CHUNK
GDN_GROUP = 4


def _gla_scan_fwd(proj, garr):
    def body(q_ref, k_ref, v_ref, g_ref, o_ref, states_ref, st):
        @pl.when(pl.program_id(1) == 0)
        def _():
            st[...] = jnp.zeros(st.shape, F32)

        s_in = st[...]
        states_ref[0, 0] = s_in
        o, s_out = _gla_chunk(q_ref[...], k_ref[...], v_ref[...], g_ref[...], s_in, _dot)
        o_ref[...] = o
        st[...] = s_out

    qk = GLA_QK // GLA_DK
    return pl.pallas_call(
        body, grid=(GLA_HEADS, N_CHUNKS),
        in_specs=[
            pl.BlockSpec((CHUNK, GLA_DK), lambda h, c: (c, h)),
            pl.BlockSpec((CHUNK, GLA_DK), lambda h, c: (c, qk + h)),
            pl.BlockSpec((CHUNK, GLA_DV), lambda h, c: (c, qk + h)),
            pl.BlockSpec((CHUNK, GLA_DK), lambda h, c: (c, h)),
        ],
        out_specs=[
            pl.BlockSpec((CHUNK, GLA_DV), lambda h, c: (c, h)),
            pl.BlockSpec((1, 1, GLA_DV, GLA_DK), lambda h, c: (h, c, 0, 0)),
        ],
        out_shape=[jax.ShapeDtypeStruct((ROWS, GLA_V), F32),
                   jax.ShapeDtypeStruct((GLA_HEADS, N_CHUNKS, GLA_DV, GLA_DK), F32)],
        scratch_shapes=[pltpu.VMEM((GLA_DV, GLA_DK), F32)],
        name="gla_scan_fwd", compiler_params=_params(("parallel", "arbitrary")),
    )(proj, proj, proj, garr)


def _gla_scan_bwd(proj, garr, states, do):
    last = N_CHUNKS - 1

    def body(q_ref, k_ref, v_ref, g_ref, s_ref, do_ref, dq_ref, dk_ref, dv_ref, dg_ref, dst):
        @pl.when(pl.program_id(1) == 0)
        def _():
            dst[...] = jnp.zeros(dst.shape, F32)

        _, vjp = jax.vjp(lambda q, k, v, g, s: _gla_chunk(q, k, v, g, s, _dot_ad),
                         q_ref[...], k_ref[...], v_ref[...], g_ref[...], s_ref[0, 0])
        dq, dk, dv, dg, ds = vjp((do_ref[...], dst[...]))
        dq_ref[...] = dq.astype(dq_ref.dtype)
        dk_ref[...] = dk.astype(dk_ref.dtype)
        dv_ref[...] = dv.astype(dv_ref.dtype)
        dg_ref[...] = dg
        dst[...] = ds

    qk = GLA_QK // GLA_DK
    kspec = pl.BlockSpec((CHUNK, GLA_DK), lambda h, c: (last - c, h))
    vspec = pl.BlockSpec((CHUNK, GLA_DV), lambda h, c: (last - c, h))
    return pl.pallas_call(
        body, grid=(GLA_HEADS, N_CHUNKS),
        in_specs=[
            kspec,
            pl.BlockSpec((CHUNK, GLA_DK), lambda h, c: (last - c, qk + h)),
            pl.BlockSpec((CHUNK, GLA_DV), lambda h, c: (last - c, qk + h)),
            kspec,
            pl.BlockSpec((1, 1, GLA_DV, GLA_DK), lambda h, c: (h, last - c, 0, 0)),
            vspec,
        ],
        out_specs=[kspec, kspec, vspec, kspec],
        out_shape=[jax.ShapeDtypeStruct((ROWS, GLA_QK), BF16), jax.ShapeDtypeStruct((ROWS, GLA_QK), BF16),
                   jax.ShapeDtypeStruct((ROWS, GLA_V), BF16), jax.ShapeDtypeStruct((ROWS, GLA_QK), F32)],
        scratch_shapes=[pltpu.VMEM((GLA_DV, GLA_DK), F32)],
        name="gla_scan_bwd", compiler_params=_params(("parallel", "arbitrary")),
    )(proj, proj, proj, garr, states, do)


def _gdn_scan_fwd(qn, kn, vn, gates):
    def body(q_ref, k_ref, v_ref, g_ref, o_ref, states_ref, st):
        @pl.when((pl.program_id(0) == 0) & (pl.program_id(1) == 0))
        def _():
            st[...] = jnp.zeros(st.shape, F32)

        gates = g_ref[...]
        heads = [pl.program_id(1) * GDN_GROUP + hh for hh in range(GDN_GROUP)]
        lanes = [slice(hh * GDN_DK, (hh + 1) * GDN_DK) for hh in range(GDN_GROUP)]
        s_in = [st[h] for h in heads]
        outs = [_gdn_chunk(q_ref[:, ln], k_ref[:, ln], v_ref[:, ln], gates, s, h, _dot, _solve)
                for ln, s, h in zip(lanes, s_in, heads)]
        for hh in range(GDN_GROUP):
            states_ref[hh, 0] = s_in[hh]
            o_ref[:, lanes[hh]] = outs[hh][0]
            st[heads[hh]] = outs[hh][1]

    hspec = pl.BlockSpec((CHUNK, GDN_GROUP * GDN_DK), lambda c, h: (c, h))
    return pl.pallas_call(
        body, grid=(N_CHUNKS, GDN_HEADS // GDN_GROUP),
        in_specs=[hspec, hspec, hspec, pl.BlockSpec((CHUNK, LANE), lambda c, h: (c, 0))],
        out_specs=[hspec, pl.BlockSpec((GDN_GROUP, 1, GDN_DK, GDN_DV), lambda c, h: (h, c, 0, 0))],
        out_shape=[jax.ShapeDtypeStruct((ROWS, D_MODEL), F32),
                   jax.ShapeDtypeStruct((GDN_HEADS, N_CHUNKS, GDN_DK, GDN_DV), F32)],
        scratch_shapes=[pltpu.VMEM((GDN_HEADS, GDN_DK, GDN_DV), F32)],
        name="gdn_scan_fwd", compiler_params=_params(("arbitrary", "arbitrary")),
    )(qn, kn, vn, gates)


def _gdn_scan_bwd(qn, kn, vn, gates, states, do):
    last = N_CHUNKS - 1

    def body(q_ref, k_ref, v_ref, g_ref, s_ref, do_ref, dq_ref, dk_ref, dv_ref, dg_ref, dst):
        @pl.when((pl.program_id(0) == 0) & (pl.program_id(1) == 0))
        def _():
            dst[...] = jnp.zeros(dst.shape, F32)

        @pl.when(pl.program_id(1) == 0)
        def _():
            dg_ref[...] = jnp.zeros(dg_ref.shape, F32)

        gates = g_ref[...]
        heads = [pl.program_id(1) * GDN_GROUP + hh for hh in range(GDN_GROUP)]
        lanes = [slice(hh * GDN_DK, (hh + 1) * GDN_DK) for hh in range(GDN_GROUP)]
        ds_in = [dst[h] for h in heads]
        grads = []
        for hh in range(GDN_GROUP):
            ln = lanes[hh]
            _, vjp = jax.vjp(
                functools.partial(lambda hd, q, k, v, g, s: _gdn_chunk(q, k, v, g, s, hd, _dot_ad, _solve_ad),
                                  heads[hh]),
                q_ref[:, ln], k_ref[:, ln], v_ref[:, ln], gates, s_ref[hh, 0])
            grads.append(vjp((do_ref[:, ln], ds_in[hh])))
        dg_sum = grads[0][3]
        for hh in range(1, GDN_GROUP):
            dg_sum = dg_sum + grads[hh][3]
        for hh in range(GDN_GROUP):
            dq_ref[:, lanes[hh]] = grads[hh][0]
            dk_ref[:, lanes[hh]] = grads[hh][1]
            dv_ref[:, lanes[hh]] = grads[hh][2]
            dst[heads[hh]] = grads[hh][4]
        dg_ref[...] += dg_sum

    hspec = pl.BlockSpec((CHUNK, GDN_GROUP * GDN_DK), lambda c, h: (last - c, h))
    gspec = pl.BlockSpec((CHUNK, LANE), lambda c, h: (last - c, 0))
    return pl.pallas_call(
        body, grid=(N_CHUNKS, GDN_HEADS // GDN_GROUP),
        in_specs=[hspec, hspec, hspec, gspec,
                  pl.BlockSpec((GDN_GROUP, 1, GDN_DK, GDN_DV), lambda c, h: (h, last - c, 0, 0)), hspec],
        out_specs=[hspec, hspec, hspec, gspec],
        out_shape=[jax.ShapeDtypeStruct((ROWS, D_MODEL), F32)] * 3 + [jax.ShapeDtypeStruct((ROWS, LANE), F32)],
        scratch_shapes=[pltpu.VMEM((GDN_HEADS, GDN_DK, GDN_DV), F32)],
        name="gdn_scan_bwd", compiler_params=_params(("arbitrary", "arbitrary")),
    )(qn, kn, vn, gates, states, do)


def _gdn_conv_fwd(proj, w, kind, cb):
    fn = _gdn_conv_fn(kind)
    nt = ROWS // ROW_TILE

    def body(prev_ref, cur_ref, w_ref, o_ref):
        (out,) = fn(pl.program_id(0), prev_ref[...], cur_ref[...], w_ref[...])
        o_ref[...] = out

    return pl.pallas_call(
        body, grid=(nt,),
        in_specs=[pl.BlockSpec((ROW_TILE, D_MODEL), lambda i: (jnp.maximum(i - 1, 0), cb)),
                  pl.BlockSpec((ROW_TILE, D_MODEL), lambda i: (i, cb)),
                  pl.BlockSpec(w.shape, lambda i: (0, 0))],
        out_specs=pl.BlockSpec((ROW_TILE, D_MODEL), lambda i: (i, 0)),
        out_shape=jax.ShapeDtypeStruct((ROWS, D_MODEL), F32),
        name="gdn_conv_fwd_" + kind, compiler_params=_params(("parallel",)),
    )(proj, proj, w)


def _gdn_conv_bwd(proj, w, kind, cb, ct):
    fn = _gdn_conv_fn(kind)
    nt = ROWS // ROW_TILE

    def body(prev_ref, cur_ref, w_ref, ct_ref, dx_ref, dw_ref, carry):
        step = pl.program_id(0)
        i = nt - 1 - step

        @pl.when(step == 0)
        def _():
            carry[...] = jnp.zeros(carry.shape, F32)
            dw_ref[...] = jnp.zeros(dw_ref.shape, F32)

        _, vjp = jax.vjp(lambda p, c, ww: fn(i, p, c, ww), prev_ref[...], cur_ref[...], w_ref[...])
        dprev, dcur, dw = vjp((ct_ref[...],))
        dx_ref[...] = (dcur + carry[...]).astype(dx_ref.dtype)
        carry[...] = dprev
        dw_ref[...] += dw

    return pl.pallas_call(
        body, grid=(nt,),
        in_specs=[pl.BlockSpec((ROW_TILE, D_MODEL), lambda s: (jnp.maximum(nt - 2 - s, 0), cb)),
                  pl.BlockSpec((ROW_TILE, D_MODEL), lambda s: (nt - 1 - s, cb)),
                  pl.BlockSpec(w.shape, lambda s: (0, 0)),
                  pl.BlockSpec((ROW_TILE, D_MODEL), lambda s: (nt - 1 - s, 0))],
        out_specs=[pl.BlockSpec((ROW_TILE, D_MODEL), lambda s: (nt - 1 - s, 0)),
                   pl.BlockSpec(w.shape, lambda s: (0, 0))],
        out_shape=[jax.ShapeDtypeStruct((ROWS, D_MODEL), BF16), jax.ShapeDtypeStruct(w.shape, F32)],
        scratch_shapes=[pltpu.VMEM((ROW_TILE, D_MODEL), F32)],
        name="gdn_conv_bwd_" + kind, compiler_params=_params(("arbitrary",)),
    )(proj, proj, w, ct)


def _loss_head(h, target):
    nt = ROWS // ROW_TILE
    lead = PAD_ROWS // ROW_TILE

    def body(h_ref, t_ref, dh_ref, acc_ref):
        i = pl.program_id(0)

        @pl.when(i == 0)
        def _():
            acc_ref[...] = jnp.zeros(acc_ref.shape, F32)

        err = (h_ref[...] - t_ref[...]) * jnp.where(i >= lead, 1.0, 0.0)
        dh_ref[...] = err * (1.0 / D_MODEL)
        acc_ref[...] += jnp.sum(err * err, axis=0, keepdims=True)

    return pl.pallas_call(
        body, grid=(nt,),
        in_specs=[pl.BlockSpec((ROW_TILE, D_MODEL), lambda i: (i, 0)),
                  pl.BlockSpec((ROW_TILE, D_MODEL), lambda i: (jnp.maximum(i - lead, 0), 0))],
        out_specs=[pl.BlockSpec((ROW_TILE, D_MODEL), lambda i: (i, 0)),
                   pl.BlockSpec((1, D_MODEL), lambda i: (0, 0))],
        out_shape=[jax.ShapeDtypeStruct((ROWS, D_MODEL), F32), jax.ShapeDtypeStruct((1, D_MODEL), F32)],
        name="loss_head", compiler_params=_params(("arbitrary",)),
    )(h, target)


MESH = pl.DeviceIdType.MESH
ANY = pl.BlockSpec(memory_space=pl.ANY)


N_PEERS = N_DEV - 1


HBM_SPEC = pl.BlockSpec(memory_space=pltpu.HBM)
SEM_SPEC = pl.BlockSpec(memory_space=pltpu.SEMAPHORE)
SIDE_EFFECT = pltpu.SideEffectType.DATAFLOW_SIDE_EFFECTING


def _me_and_peers():
    x, y, c = lax.axis_index("x"), lax.axis_index("y"), lax.axis_index("c")
    peers = []
    for k in range(1, N_DEV):
        pid = (1 - x if k & 4 else x, 1 - y if k & 2 else y, 1 - c if k & 1 else c)
        peers.append((4 * pid[0] + 2 * pid[1] + pid[2], pid))
    return 4 * x + 2 * y + c, peers


def _split_copies(src_refs, land_refs, send_sems, recv_sems, spread):
    me, peers = _me_and_peers()
    out = []
    for t, (src, land) in enumerate(zip(src_refs, land_refs)):
        for k, (peer, pid) in enumerate(peers):
            sems = dict(send_sem=send_sems.at[N_PEERS * t + k], recv_sem=recv_sems.at[N_PEERS * t + k],
                        device_id=pid, device_id_type=MESH)
            piece = src if spread else src.at[peer]
            out.append((pltpu.make_async_remote_copy(src_ref=piece, dst_ref=land.at[me], **sems),
                        pltpu.make_async_remote_copy(src_ref=piece, dst_ref=land.at[peer], **sems)))
    return out


def _send_start(srcs, spread, name):
    n = len(srcs)
    lands = [lax.empty((N_DEV,) + (s.shape if spread else s.shape[1:]), s.dtype) for s in srcs]

    def body(*refs):
        for out, _ in _split_copies(refs[:n], refs[n:2 * n], refs[2 * n], refs[2 * n + 1], spread):
            out.start()
        refs[-1][...] = jnp.zeros(refs[-1].shape, F32)

    thru = [pltpu.HBM(a.shape, a.dtype) for a in list(srcs) + lands]
    res = pl.pallas_call(
        body, name=name,
        out_shape=(pltpu.SemaphoreType.DMA((N_PEERS * n,)), pltpu.SemaphoreType.DMA((N_PEERS * n,)), *thru,
                   jax.ShapeDtypeStruct((8, LANE), F32)),
        in_specs=[HBM_SPEC] * (2 * n),
        out_specs=(SEM_SPEC, SEM_SPEC, *[HBM_SPEC] * (2 * n), pl.BlockSpec(memory_space=pltpu.VMEM)),
        input_output_aliases={t: 2 + t for t in range(2 * n)},
        compiler_params=pltpu.CompilerParams(has_side_effects=SIDE_EFFECT),
    )(*[pltpu.with_memory_space_constraint(a, pltpu.HBM) for a in list(srcs) + lands])
    return res[0], res[1], list(res[2:2 + n]), list(res[2 + n:2 + 2 * n]), res[-1]


def _send_wait(started, spread, after, name):
    send_sems, recv_sems, srcs, lands, _ = started
    n = len(srcs)

    def body(*refs):
        for out, arrival in _split_copies(refs[:n], refs[n:2 * n], refs[2 * n], refs[2 * n + 1], spread):
            out.wait_send()
            arrival.wait_recv()

    thru = [pltpu.HBM(a.shape, a.dtype) for a in srcs + lands]
    res = pl.pallas_call(
        body, name=name, out_shape=tuple(thru),
        in_specs=[HBM_SPEC] * (2 * n) + [SEM_SPEC, SEM_SPEC, ANY], out_specs=tuple([HBM_SPEC] * (2 * n)),
        input_output_aliases={t: t for t in range(2 * n)},
        compiler_params=pltpu.CompilerParams(has_side_effects=SIDE_EFFECT),
    )(*srcs, *lands, send_sems, recv_sems, after)
    return list(res[n:])


def _adamw(recvs, w, m, v, name):
    n_layers = len(recvs)
    rows, cols = recvs[0].shape[1:]
    tr = _pick(rows, (256, 128, 64, 32, 16))
    nb = rows // tr

    def body(*refs):
        r_refs = refs[:n_layers]
        w_ref, m_ref, v_ref, g_ref, d_ref, m2_ref, v2_ref = refs[n_layers:]

        def update(r_ref):
            g = r_ref[0].astype(F32)
            for k in range(1, N_DEV):
                g = g + r_ref[k].astype(F32)
            m2 = ADAM_B1 * m_ref[...] + (1.0 - ADAM_B1) * g
            v2 = ADAM_B2 * v_ref[...] + (1.0 - ADAM_B2) * (g * g)
            m_hat = m2 / (1.0 - ADAM_B1 ** ADAM_STEP)
            v_hat = v2 / (1.0 - ADAM_B2 ** ADAM_STEP)
            g_ref[...] = g
            d_ref[...] = -ADAM_LR * (m_hat / (jnp.sqrt(v_hat) + ADAM_EPS) + ADAM_WD * w_ref[...])
            m2_ref[...] = m2
            v2_ref[...] = v2

        if n_layers == 1:
            update(r_refs[0])
        else:
            for layer in range(n_layers):
                pl.when(pl.program_id(0) == layer)(functools.partial(update, r_refs[layer]))

    def recv_spec(layer):
        return pl.BlockSpec((N_DEV, tr, cols), lambda a, i: (0, jnp.where(a == layer, i, 0), 0))

    spec = pl.BlockSpec((tr, cols), lambda a, i: (a * nb + i, 0))
    return pl.pallas_call(
        body, grid=(n_layers, nb),
        in_specs=[recv_spec(layer) for layer in range(n_layers)] + [spec, spec, spec],
        out_specs=[spec] * 4, out_shape=[jax.ShapeDtypeStruct(w.shape, F32)] * 4,
        name=name, compiler_params=_params(("arbitrary", "arbitrary")),
    )(*recvs, w, m, v)


def _padded_rows(n, mult):
    rows = -(-n // PACK_COLS)
    return -(-rows // mult) * mult


def _pack(flat_pieces, mult, dtype, lead=()):
    flat = jnp.concatenate([p.astype(dtype) for p in flat_pieces], axis=-1)
    n = flat.shape[-1]
    rows = _padded_rows(n, mult)
    flat = jnp.pad(flat, [(0, 0)] * len(lead) + [(0, rows * PACK_COLS - n)])
    return flat.reshape(lead + (rows, PACK_COLS))


def _unpack(packed, shapes):
    flat = packed.reshape(-1)
    out, off = {}, 0
    for name, shape in shapes:
        size = 1
        for s in shape:
            size *= s
        out[name] = flat[off:off + size].reshape(shape)
        off += size
    return out


def _local_2d(arr):
    return arr.reshape(-1, arr.shape[-1])


SHARD_2D = {n: (s[-2] if len(s) > 2 else s[0], s[-1], a == len(s) - 1) for n, s, a in SHARDED}
SHARD_2D["gdn_conv_w"] = (GDN_CONV, GDN_CONV_DIM // N_DEV, True)


def _wire(name):
    rows, cols, _ = SHARD_2D[name]
    return BF16 if rows * cols >= 1 << 16 else F32


def _layer_tensors(i):
    kind, j = i % 3, i // 3
    pre = ("fox", "gla", "gdn")[kind]
    out = [("w_gate_up", i), ("w_down", i), (pre + "_w_in", j), (pre + "_w_out", j)]
    if kind == 1:
        out.append(("gla_w_alpha2", j))
    if kind == 2:
        out.append(("gdn_conv_w", j))
    if i == 0:
        out.append(("meta_tokens", None))
    return out


def _layer_block(arr, idx):
    return arr if idx is None else _local_2d(arr[idx])


def _to_slabs(full2d, name):
    rows, cols, by_cols = SHARD_2D[name]
    if by_cols:
        return jnp.stack([full2d[:, cols * d:cols * (d + 1)] for d in range(N_DEV)])
    return full2d.reshape(N_DEV, rows, cols)


def _from_gathered(g, name):
    rows, cols, by_cols = SHARD_2D[name]
    if by_cols:
        return jnp.concatenate([g[d] for d in range(N_DEV)], axis=1)
    return g.reshape(N_DEV * rows, cols)


def _pad_cols(w, total):
    return jnp.pad(w, [(0, 0)] * (w.ndim - 1) + [(0, total - w.shape[-1])])


def _pad_lanes(v):
    return _pad_cols(v, LANE)


def _x(arr, width, cb0=0, moves=0):
    return (arr, width, cb0, moves)


def _rms_fwd(h, g):
    return _row_fwd(_rms_fn, [_x(h, D_MODEL)], [g], [], [(D_MODEL, D_MODEL, BF16)], "rms_fwd")[0]


def _rms_bwd(h, g, dy, add):
    dh, dg = _row_bwd(_rms_fn, [_x(h, D_MODEL)], [g], [], [_x(dy, D_MODEL)], [(D_MODEL, F32)], "rms_bwd", adds=[add])
    return dh, dg


SW_TILE, SW_WIDTH = 384, 256
SW_NCOL = D_FF // SW_WIDTH


def _ffn_fwd(h, g, wgu, wd):
    y = _rms_fwd(h, g)
    gu = _matmul(y, wgu, "nn", F32, "ffn_gate_up")
    a = _row_fwd(_swiglu_fn, [_x(gu, SW_WIDTH, 0, 1), _x(gu, SW_WIDTH, SW_NCOL, 1)], [], [],
                 [(SW_WIDTH, D_FF, BF16)], "swiglu_fwd", tm=SW_TILE, ncol=SW_NCOL)[0]
    h2 = _matmul(a, wd, "nn", F32, "ffn_down", add=h)
    return h2, (h, y, gu, a)


def _ffn_bwd(dh2, saved, g, wgu, wd):
    h, y, gu, a = saved
    da = _matmul(dh2, wd, "nt", F32, "ffn_down_dx")
    dwd = _matmul(a, dh2, "tn", F32, "ffn_down_dw")
    dgt, dup = _row_bwd(_swiglu_fn, [_x(gu, SW_WIDTH, 0, 1), _x(gu, SW_WIDTH, SW_NCOL, 1)], [], [],
                        [_x(da, SW_WIDTH, 0, 1)], [(D_FF, BF16), (D_FF, BF16)], "swiglu_bwd",
                        tm=SW_TILE, ncol=SW_NCOL)
    dgu = jnp.concatenate([dgt, dup], axis=1)
    dwgu = _matmul(y, dgu, "tn", F32, "ffn_gate_up_dw")
    dy = _matmul(dgu, wgu, "nt", F32, "ffn_gate_up_dx")
    dh, dg = _rms_bwd(h, g, dy, dh2)
    return dh, dg, dwgu, dwd


def _fox_consts():
    lane = jnp.arange(D_MODEL) // FOX_DH
    e_expand = (jnp.arange(LANE)[:, None] == lane[None, :]).astype(F32)
    e_mean = e_expand.T * (1.0 / FOX_DH)
    half = jnp.arange(LANE) // FOX_DH
    blockdiag = (half[:, None] == half[None, :]).astype(F32)
    return e_mean, e_expand, blockdiag


def _pair_layout(c):
    heads = c[:, :FOX_HEADS].T
    crow = heads.reshape(FOX_PAIRS, 2, ROWS)
    c2 = jnp.broadcast_to(crow.transpose(0, 2, 1)[:, :, :, None], (FOX_PAIRS, ROWS, 2, FOX_DH))
    crow = jnp.where(jnp.arange(ROWS) >= FIRST_ROW, crow, KEY_OFF)
    return c2.reshape(FOX_PAIRS, ROWS, LANE), crow


def _fox_fwd(y, p):
    proj = _matmul(y, p["w_in"], "nn", F32, "fox_in")
    e_mean, e_expand, _ = _fox_consts()
    nb = D_MODEL // LANE
    xs = [_x(proj, D_MODEL, 0), _x(proj, D_MODEL, 1), _x(proj, LANE, 4 * nb)]
    ps = [p["q_gain"], p["k_gain"], p["b_f"]]
    qn, kn, lf = _row_fwd(_fox_pre_fn, xs, ps, [e_mean, e_expand],
                          [(D_MODEL, D_MODEL, BF16), (D_MODEL, D_MODEL, BF16), (LANE, LANE, F32)], "fox_pre_fwd")
    c = _seq_cumsum(lf, False, "fox_cumsum")
    c2, crow = _pair_layout(c)
    o, lse2 = _fox_attn_fwd(qn, kn, proj, c2, crow)
    om = _row_fwd(_fox_post_fn, [_x(o, D_MODEL, 0), _x(proj, D_MODEL, 3)], [], [],
                  [(D_MODEL, D_MODEL, BF16)], "fox_post_fwd")[0]
    return om, (proj, qn, kn, c2, crow, o, lse2)


def _fox_bwd(dom, y, saved, p):
    proj, qn, kn, c2, crow, o, lse2 = saved
    e_mean, e_expand, blockdiag = _fox_consts()
    nb = D_MODEL // LANE
    do, dgate = _row_bwd(_fox_post_fn, [_x(o, D_MODEL, 0), _x(proj, D_MODEL, 3)], [], [], [_x(dom, D_MODEL)],
                         [(D_MODEL, F32), (D_MODEL, BF16)], "fox_post_bwd")
    dq, delta2, dcq2 = _fox_attn_dq(qn, kn, proj, c2, crow, lse2, o, do, blockdiag)
    dk, dv, dcrow = _fox_attn_dkv(qn, kn, proj, c2, crow, lse2, delta2, do)
    dcq = dcq2[:, :, ::FOX_DH].transpose(1, 0, 2).reshape(ROWS, FOX_HEADS)
    dc = _pad_cols(dcq + dcrow.reshape(FOX_HEADS, ROWS).T, LANE)
    dlf = _seq_cumsum(dc, True, "fox_cumsum_rev")
    xs = [_x(proj, D_MODEL, 0), _x(proj, D_MODEL, 1), _x(proj, LANE, 4 * nb)]
    ps = [p["q_gain"], p["k_gain"], p["b_f"]]
    dqr, dkr, df, dqg, dkg, dbf = _row_bwd(
        _fox_pre_fn, xs, ps, [e_mean, e_expand], [_x(dq, D_MODEL), _x(dk, D_MODEL), _x(dlf, LANE)],
        [(D_MODEL, BF16), (D_MODEL, BF16), (LANE, BF16)], "fox_pre_bwd")
    dproj = jnp.concatenate([dqr, dkr, dv, dgate, df], axis=1)
    grads = {
        "w_in": _matmul(y, dproj, "tn", F32, "fox_in_dw")[:, :FOX_IN],
        "q_gain": dqg.reshape(FOX_HEADS, FOX_DH).sum(axis=0),
        "k_gain": dkg.reshape(FOX_HEADS, FOX_DH).sum(axis=0),
        "b_f": dbf[0, :FOX_HEADS],
    }
    return _matmul(dproj, p["w_in"], "nt", F32, "fox_in_dx"), grads


def _gla_fwd(y, p):
    proj = _matmul(y, p["w_in"], "nn", F32, "gla_in")
    small = (2 * GLA_QK + 2 * GLA_V) // LANE
    garr = _row_fwd(_gla_pre_fn, [_x(proj, LANE, small)], [p["w_alpha"], p["b_alpha"]], [],
                    [(GLA_QK, GLA_QK, F32)], "gla_pre_fwd")[0]
    o, states = _gla_scan_fwd(proj, garr)
    om = _row_fwd(_gla_post_fn, [_x(o, GLA_V, 0), _x(proj, GLA_V, 2)], [p["o_gain"]], [],
                  [(GLA_V, GLA_V, BF16)], "gla_post_fwd")[0]
    return om, (proj, garr, states, o)


def _gla_bwd(dom, y, saved, p):
    proj, garr, states, o = saved
    small = (2 * GLA_QK + 2 * GLA_V) // LANE
    do, dr, dgain = _row_bwd(_gla_post_fn, [_x(o, GLA_V, 0), _x(proj, GLA_V, 2)], [p["o_gain"]], [],
                             [_x(dom, GLA_V)], [(GLA_V, F32), (GLA_V, BF16)], "gla_post_bwd")
    dq, dk, dv, dg = _gla_scan_bwd(proj, garr, states, do)
    dalr, dwa, dba = _row_bwd(_gla_pre_fn, [_x(proj, LANE, small)], [p["w_alpha"], p["b_alpha"]], [],
                              [_x(dg, GLA_QK)], [(LANE, BF16)], "gla_pre_bwd")
    dproj = jnp.concatenate([dq, dk, dv, dr, dalr], axis=1)
    grads = {
        "w_in": _matmul(y, dproj, "tn", F32, "gla_in_dw")[:, :GLA_IN],
        "w_alpha": dwa[:GLA_RANK], "b_alpha": dba[0], "o_gain": dgain[0],
    }
    return _matmul(dproj, p["w_in"], "nt", F32, "gla_in_dx"), grads


def _gdn_fwd(y, p):
    proj = _matmul(y, p["w_in"], "nn", F32, "gdn_in")
    nb = D_MODEL // LANE
    qn = _gdn_conv_fwd(proj, p["conv_q"], "q", 0)
    kn = _gdn_conv_fwd(proj, p["conv_k"], "k", 1)
    vn = _gdn_conv_fwd(proj, p["conv_v"], "v", 2)
    gates = _row_fwd(_gdn_gates_fn, [_x(proj, LANE, 4 * nb)], [p["a_log"], p["dt_bias"]], [],
                     [(LANE, LANE, F32)], "gdn_gates_fwd")[0]
    o, states = _gdn_scan_fwd(qn, kn, vn, gates)
    om = _row_fwd(_gdn_post_fn, [_x(o, D_MODEL, 0), _x(proj, D_MODEL, 3)], [p["o_gain"]], [],
                  [(D_MODEL, D_MODEL, BF16)], "gdn_post_fwd")[0]
    return om, (proj, qn, kn, vn, gates, states, o)


def _gdn_bwd(dom, y, saved, p):
    proj, qn, kn, vn, gates, states, o = saved
    nb = D_MODEL // LANE
    do, dgate, dgain = _row_bwd(_gdn_post_fn, [_x(o, D_MODEL, 0), _x(proj, D_MODEL, 3)], [p["o_gain"]], [],
                                [_x(dom, D_MODEL)], [(D_MODEL, F32), (D_MODEL, BF16)], "gdn_post_bwd")
    dqn, dkn, dvn, dgates = _gdn_scan_bwd(qn, kn, vn, gates, states, do)
    dsm, dalog, ddtb = _row_bwd(_gdn_gates_fn, [_x(proj, LANE, 4 * nb)], [p["a_log"], p["dt_bias"]], [],
                                [_x(dgates, LANE)], [(LANE, BF16)], "gdn_gates_bwd")
    dxq, dwq = _gdn_conv_bwd(proj, p["conv_q"], "q", 0, dqn)
    dxk, dwk = _gdn_conv_bwd(proj, p["conv_k"], "k", 1, dkn)
    dxv, dwv = _gdn_conv_bwd(proj, p["conv_v"], "v", 2, dvn)
    dproj = jnp.concatenate([dxq, dxk, dxv, dgate, dsm], axis=1)
    grads = {
        "w_in": _matmul(y, dproj, "tn", F32, "gdn_in_dw")[:, :GDN_IN],
        "conv_w": jnp.concatenate([dwq[:GDN_CONV], dwk[:GDN_CONV], dwv[:GDN_CONV]], axis=1),
        "a_log": dalog[0, :GDN_HEADS], "dt_bias": ddtb[0, :GDN_HEADS], "o_gain": dgain[0],
    }
    return _matmul(dproj, p["w_in"], "nt", F32, "gdn_in_dx"), grads


_MIXERS = ((_fox_fwd, _fox_bwd), (_gla_fwd, _gla_bwd), (_gdn_fwd, _gdn_bwd))


def _mixer_params(i, wts, rep):
    kind, j = i % 3, i // 3
    if kind == 0:
        return {
            "w_in": _pad_cols(wts["fox_w_in"], FOX_IN_PAD), "w_out": wts["fox_w_out"],
            "q_gain": jnp.tile(rep["fox_q_gain"][j], FOX_HEADS)[None],
            "k_gain": jnp.tile(rep["fox_k_gain"][j], FOX_HEADS)[None],
            "b_f": _pad_lanes(rep["fox_b_f"][j][None]),
        }
    if kind == 1:
        return {
            "w_in": _pad_cols(wts["gla_w_in"], GLA_IN_PAD), "w_out": wts["gla_w_out"],
            "w_alpha": jnp.pad(wts["gla_w_alpha2"], ((0, LANE - GLA_RANK), (0, 0))).astype(F32),
            "b_alpha": rep["gla_b_alpha"][j][None], "o_gain": rep["gla_o_gain"][j][None],
        }
    conv = jnp.pad(wts["gdn_conv_w"].astype(F32), ((0, 8 - GDN_CONV), (0, 0)))
    return {
        "w_in": _pad_cols(wts["gdn_w_in"], GDN_IN_PAD), "w_out": wts["gdn_w_out"],
        "conv_q": conv[:, :D_MODEL], "conv_k": conv[:, D_MODEL:2 * D_MODEL], "conv_v": conv[:, 2 * D_MODEL:],
        "a_log": _pad_lanes(rep["gdn_a_log"][j][None]), "dt_bias": _pad_lanes(rep["gdn_dt_bias"][j][None]),
        "o_gain": rep["gdn_o_gain"][j][None],
    }


def kernel(x, meta_tokens, norm_mix, norm_ffn, w_gate_up, w_down, fox_w_in, fox_b_f, fox_q_gain, fox_k_gain, fox_w_out, gla_w_in, gla_w_alpha2, gla_b_alpha, gla_o_gain, gla_w_out, gdn_w_in, gdn_conv_w, gdn_a_log, gdn_dt_bias, gdn_o_gain, gdn_w_out, loss_target, m_meta_tokens, m_norm_mix, m_norm_ffn, m_w_gate_up, m_w_down, m_fox_w_in, m_fox_b_f, m_fox_q_gain, m_fox_k_gain, m_fox_w_out, m_gla_w_in, m_gla_w_alpha2, m_gla_b_alpha, m_gla_o_gain, m_gla_w_out, m_gdn_w_in, m_gdn_conv_w, m_gdn_a_log, m_gdn_dt_bias, m_gdn_o_gain, m_gdn_w_out, v_meta_tokens, v_norm_mix, v_norm_ffn, v_w_gate_up, v_w_down, v_fox_w_in, v_fox_b_f, v_fox_q_gain, v_fox_k_gain, v_fox_w_out, v_gla_w_in, v_gla_w_alpha2, v_gla_b_alpha, v_gla_o_gain, v_gla_w_out, v_gdn_w_in, v_gdn_conv_w, v_gdn_a_log, v_gdn_dt_bias, v_gdn_o_gain, v_gdn_w_out):
    given = dict(locals())
    w_loc = {n: given[n] for n in WEIGHT_ORDER}
    m_loc = {n: given["m_" + n] for n in WEIGHT_ORDER}
    v_loc = {n: given["v_" + n] for n in WEIGHT_ORDER}

    rep = {n: w_loc[n] for n, _ in REPLICATED}
    me = 4 * lax.axis_index("x") + 2 * lax.axis_index("y") + lax.axis_index("c")

    def with_own(lands, owns):
        return [lax.dynamic_update_slice(land, own[None], (me, 0, 0)) for land, own in zip(lands, owns)]

    gathers = []
    order = jnp.zeros((), F32)
    for i in range(DEPTH):
        tensors = _layer_tensors(i)
        blocks = [(_layer_block(w_loc[n], idx) + order).astype(_wire(n)) for n, idx in tensors]
        gathers.append((tensors, blocks, _send_start(blocks, True, "gather_start_%d" % i)))
        order = gathers[i][2][4][0, 0]
    all_started = gathers[DEPTH - 1][2][4]

    def layer_weights(i, after):
        tensors, blocks, started = gathers[i]
        lands = with_own(_send_wait(started, True, after, "gather_wait_%d" % i), blocks)
        return {n: _from_gathered(g, n) for (n, _), g in zip(tensors, lands)}

    h = None
    saved = []
    for i in range(DEPTH):
        wts = layer_weights(i, all_started if i == 0 else h)
        if i == 0:
            h = jnp.concatenate([jnp.zeros((FIRST_ROW, D_MODEL), F32), wts["meta_tokens"], x[0]], axis=0)
        p = _mixer_params(i, wts, rep)
        fwd, _ = _MIXERS[i % 3]
        g_mix = rep["norm_mix"][i][None]
        y = _rms_fwd(h, g_mix)
        om, mix_saved = fwd(y, p)
        h_mid = _matmul(om, p["w_out"], "nn", F32, "mixer_out", add=h)
        g_ffn = rep["norm_ffn"][i][None]
        h_out, ffn_saved = _ffn_fwd(h_mid, g_ffn, wts["w_gate_up"], wts["w_down"])
        saved.append((p, wts, h, y, om, mix_saved, ffn_saved))
        h = h_out

    dh, sq = _loss_head(h, loss_target[0])
    loss = lax.psum(0.5 * jnp.sum(sq) * (1.0 / D_MODEL), ("x", "y", "c"))

    gw = {n: [None] * DEPTH for n in ("norm_mix", "norm_ffn")}
    gm = {}
    reduces = [None] * DEPTH
    for i in reversed(range(DEPTH)):
        p, wts, h_in, y, om, mix_saved, ffn_saved = saved[i]
        kind, j = i % 3, i // 3
        pre = ("fox", "gla", "gdn")[kind]
        _, bwd = _MIXERS[kind]
        dh_mid, gw["norm_ffn"][i], dwgu, dwd = _ffn_bwd(
            dh, ffn_saved, rep["norm_ffn"][i][None], wts["w_gate_up"], wts["w_down"])
        dom = _matmul(dh_mid, p["w_out"], "nt", F32, "mixer_out_dx")
        dwo = _matmul(om, dh_mid, "tn", F32, "mixer_out_dw")
        dy, grads = bwd(dom, y, mix_saved, p)
        gm[(kind, j)] = grads
        dh, gw["norm_mix"][i] = _rms_bwd(h_in, rep["norm_mix"][i][None], dy, dh_mid)
        full = {"w_gate_up": dwgu, "w_down": dwd, pre + "_w_in": grads["w_in"], pre + "_w_out": dwo,
                "gla_w_alpha2": grads.get("w_alpha"), "gdn_conv_w": grads.get("conv_w"),
                "meta_tokens": dh[FIRST_ROW:PAD_ROWS]}
        tensors = _layer_tensors(i)
        slabs = [_to_slabs(full[n], n).astype(_wire(n)) for n, _ in tensors]
        reduces[i] = [tensors, slabs, None]
        if i > 0:
            reduces[i][2] = _send_start(slabs, False, "reduce_start_%d" % i)
            dh = dh + reduces[i][2][4][0, 0]

    partial = {
        "norm_mix": jnp.concatenate(gw["norm_mix"], axis=0), "norm_ffn": jnp.concatenate(gw["norm_ffn"], axis=0),
        "fox_b_f": jnp.stack([gm[(0, 0)]["b_f"], gm[(0, 1)]["b_f"]]),
        "fox_q_gain": jnp.stack([gm[(0, 0)]["q_gain"], gm[(0, 1)]["q_gain"]]),
        "fox_k_gain": jnp.stack([gm[(0, 0)]["k_gain"], gm[(0, 1)]["k_gain"]]),
        "gla_b_alpha": gm[(1, 0)]["b_alpha"][None], "gla_o_gain": gm[(1, 0)]["o_gain"][None],
        "gdn_a_log": gm[(2, 0)]["a_log"][None], "gdn_dt_bias": gm[(2, 0)]["dt_bias"][None],
        "gdn_o_gain": gm[(2, 0)]["o_gain"][None],
    }

    small = _pack([partial[n].reshape(-1) for n, _ in REPLICATED], 8, F32)
    reduces[0][1] = reduces[0][1] + [jnp.broadcast_to(small[None], (N_DEV,) + small.shape)]
    reduces[0][2] = _send_start(reduces[0][1], False, "reduce_start_0")
    recv = {}
    for i in reversed(range(DEPTH)):
        tensors, slabs, started = reduces[i]
        owns = [lax.dynamic_index_in_dim(s, me, 0, keepdims=False) for s in slabs]
        lands = with_own(_send_wait(started, False, dh, "reduce_wait_%d" % i), owns)
        for (n, idx), land in zip(tensors, lands):
            recv[(n, idx)] = land
        if i == 0:
            recv_small = lands[-1]

    res = [{} for _ in range(4)]
    for n, s, _ in SHARDED:
        layers = [idx for i in range(DEPTH) for (nn, idx) in _layer_tensors(i) if nn == n]
        layers = sorted(layers, key=lambda idx: -1 if idx is None else idx)
        outs = _adamw([recv[(n, idx)] for idx in layers], *[_local_2d(d[n]) for d in (w_loc, m_loc, v_loc)],
                      "adamw_" + n)
        for k in range(4):
            res[k][n] = outs[k].reshape(s)
    outs_small = _adamw([recv_small], *[_pack([d[n].reshape(-1) for n, _ in REPLICATED], 8, F32)
                                        for d in (w_loc, m_loc, v_loc)], "adamw_replicated")
    for k in range(4):
        res[k].update(_unpack(outs_small[k], list(REPLICATED)))

    result = [loss, dh[PAD_ROWS:][None]]
    for k in range(4):
        result += [res[k][n] for n in WEIGHT_ORDER]
    return tuple(result)
```

```python
import functools

import jax
import jax.numpy as jnp
from jax import lax
from jax.experimental import pallas as pl
from jax.experimental.pallas import tpu as pltpu

F32, BF16 = jnp.float32, jnp.bfloat16
HIGHEST = lax.Precision.HIGHEST

D_MODEL = 1024
SEQ = 4096
N_META = 16
DEPTH = 4
NORM_EPS = 1e-6
LANE = 128
PAD_ROWS = 128
FIRST_ROW = PAD_ROWS - N_META
ROWS = PAD_ROWS + SEQ
ROW_TILE = 128
CHUNK = 64
NEG = -1e30
N_DEV = 8
VMEM_LIMIT = 56 * 1024 * 1024
VMEM_BLOCK_BUDGET = 36 * 1024 * 1024

FOX_HEADS, FOX_DH = 16, 64
FOX_IN, FOX_IN_PAD = 4 * D_MODEL + FOX_HEADS, 4 * D_MODEL + LANE
GLA_HEADS, GLA_DK, GLA_DV, GLA_RANK = 4, 128, 256, 16
GLA_QK, GLA_V = GLA_HEADS * GLA_DK, GLA_HEADS * GLA_DV
GLA_IN, GLA_IN_PAD = 2 * GLA_QK + 2 * GLA_V + GLA_RANK, 2 * GLA_QK + 2 * GLA_V + LANE
GDN_HEADS, GDN_DK, GDN_DV, GDN_CONV = 8, 128, 128, 4
GDN_CONV_DIM = 3 * GDN_HEADS * GDN_DK
GDN_IN, GDN_IN_PAD = GDN_CONV_DIM + GDN_HEADS * GDN_DV + 2 * GDN_HEADS, GDN_CONV_DIM + GDN_HEADS * GDN_DV + LANE
D_FF = 2816

ADAM_LR, ADAM_B1, ADAM_B2, ADAM_EPS, ADAM_WD, ADAM_STEP = 0.001, 0.9, 0.999, 1e-08, 0.01, 10

PACK_COLS = 1024
PACK_ROW_TILE = 256

SHARDED = (
    ("meta_tokens", (16, 128), 1),
    ("w_gate_up", (4, 1024, 704), 2),
    ("w_down", (4, 352, 1024), 1),
    ("fox_w_in", (2, 1024, 514), 2),
    ("fox_w_out", (2, 128, 1024), 1),
    ("gla_w_in", (1, 1024, 386), 2),
    ("gla_w_alpha2", (1, 16, 64), 2),
    ("gla_w_out", (1, 128, 1024), 1),
    ("gdn_w_in", (1, 1024, 514), 2),
    ("gdn_conv_w", (1, 4, 1, 384), 3),
    ("gdn_w_out", (1, 128, 1024), 1),
)
REPLICATED = (
    ("norm_mix", (4, 1024)),
    ("norm_ffn", (4, 1024)),
    ("fox_b_f", (2, 16)),
    ("fox_q_gain", (2, 64)),
    ("fox_k_gain", (2, 64)),
    ("gla_b_alpha", (1, 512)),
    ("gla_o_gain", (1, 256)),
    ("gdn_a_log", (1, 8)),
    ("gdn_dt_bias", (1, 8)),
    ("gdn_o_gain", (1, 128)),
)
WEIGHT_ORDER = (
    "meta_tokens", "norm_mix", "norm_ffn", "w_gate_up", "w_down", "fox_w_in", "fox_b_f", "fox_q_gain",
    "fox_k_gain", "fox_w_out", "gla_w_in", "gla_w_alpha2", "gla_b_alpha", "gla_o_gain", "gla_w_out",
    "gdn_w_in", "gdn_conv_w", "gdn_a_log", "gdn_dt_bias", "gdn_o_gain", "gdn_w_out",
)

_DN = {
    "nn": (((1,), (0,)), ((), ())),
    "nt": (((1,), (1,)), ((), ())),
    "tn": (((0,), (0,)), ((), ())),
}


def _dot(a, b, kind):
    return lax.dot_general(a.astype(BF16), b.astype(BF16), _DN[kind], preferred_element_type=F32)


@functools.partial(jax.custom_vjp, nondiff_argnums=(2,))
def _dot_ad(a, b, kind):
    return _dot(a, b, kind)


def _dot_ad_fwd(a, b, kind):
    return _dot(a, b, kind), (a, b)


def _dot_ad_bwd(kind, res, ct):
    a, b = res
    if kind == "nn":
        return _dot(ct, b, "nt"), _dot(a, ct, "tn")
    if kind == "nt":
        return _dot(ct, b, "nn"), _dot(ct, a, "tn")
    return _dot(b, ct, "nt"), _dot(a, ct, "nn")


_dot_ad.defvjp(_dot_ad_fwd, _dot_ad_bwd)


def _dot_hi(a, b, kind):
    return lax.dot_general(a, b, _DN[kind], precision=HIGHEST, preferred_element_type=F32)


def _dot_mid(a, b, kind):
    return lax.dot_general(a, b, _DN[kind], precision=lax.Precision.HIGH, preferred_element_type=F32)


def _iota(shape, dim):
    return lax.broadcasted_iota(jnp.int32, shape, dim)


def _row_mask(i, tm):
    return ((i * tm + _iota((tm, 1), 0)) >= FIRST_ROW).astype(F32)


def _log_sigmoid(z):
    return jnp.minimum(z, 0.0) - jnp.log(1.0 + jnp.exp(-jnp.abs(z)))


def _softplus(z):
    return jnp.maximum(z, 0.0) + jnp.log(1.0 + jnp.exp(-jnp.abs(z)))


def _silu(z):
    return z * jax.nn.sigmoid(z)


def _lane_col(x, lane):
    return jnp.sum(jnp.where(_iota(x.shape, 1) == lane, x, 0.0), axis=1, keepdims=True)


def _last_row(x):
    return jnp.sum(jnp.where(_iota(x.shape, 0) == x.shape[0] - 1, x, 0.0), axis=0, keepdims=True)


def _params(sem, limit=VMEM_LIMIT):
    return pltpu.CompilerParams(dimension_semantics=sem, vmem_limit_bytes=limit)


def _pick(n, options):
    for t in options:
        if n % t == 0:
            return t
    return n


def _matmul(a, b, kind, out_dtype, name, add=None, b_kblock=0):
    if kind == "nn":
        (m, k), n = a.shape, b.shape[1]
    elif kind == "nt":
        (m, k), n = a.shape, b.shape[0]
    else:
        (k, m), n = a.shape, b.shape[1]
    tn = _pick(n, (512, 384, 640, 256, 128))
    tm_options = (1408, 384, 128) if kind != "tn" else (256, 128)
    tm = None
    for t in tm_options:
        if m % t:
            continue
        need = 2 * (t * k * a.dtype.itemsize + tn * k * b.dtype.itemsize + t * tn * 8)
        if need <= VMEM_BLOCK_BUDGET:
            tm = t
            break
    assert tm is not None, (name, a.shape, b.shape)

    def body(*refs):
        a_ref, b_ref = refs[0], refs[1]
        o_ref = refs[-1]
        acc = _dot(a_ref[...], b_ref[...], kind)
        if add is not None:
            acc = acc + refs[2][...]
        o_ref[...] = acc.astype(o_ref.dtype)

    if kind == "tn":
        a_spec = pl.BlockSpec((k, tm), lambda i, j: (0, i))
    else:
        a_spec = pl.BlockSpec((tm, k), lambda i, j: (i, 0))
    if kind == "nt":
        b_spec = pl.BlockSpec((tn, k), lambda i, j: (j, b_kblock))
    else:
        b_spec = pl.BlockSpec((k, tn), lambda i, j: (0, j))
    o_spec = pl.BlockSpec((tm, tn), lambda i, j: (i, j))
    operands, in_specs = [a, b], [a_spec, b_spec]
    if add is not None:
        operands.append(add)
        in_specs.append(o_spec)
    return pl.pallas_call(
        body, grid=(m // tm, n // tn), in_specs=in_specs, out_specs=o_spec,
        out_shape=jax.ShapeDtypeStruct((m, n), out_dtype), name=name,
        compiler_params=_params(("parallel", "parallel")),
    )(*operands)


def _x_spec(tm, width, cb0, moves):
    return pl.BlockSpec((tm, width), lambda i, j: (i, cb0 + j * moves))


def _full_spec(arr):
    nd = arr.ndim
    return pl.BlockSpec(arr.shape, lambda i, j: (0,) * nd)


def _row_fwd(fn, xs, ps, cs, outs, name, tm=ROW_TILE, ncol=1, rows=None):
    rows = ROWS if rows is None else rows
    nx, npar, nc = len(xs), len(ps), len(cs)

    def body(*refs):
        i = pl.program_id(0)
        xv = [r[...].astype(F32) for r in refs[:nx]]
        pv = [r[...] for r in refs[nx:nx + npar + nc]]
        res = fn(i, *xv, *pv)
        for r, val in zip(refs[nx + npar + nc:], res):
            r[...] = val.astype(r.dtype)

    in_specs = [_x_spec(tm, w, cb0, mv) for (_, w, cb0, mv) in xs]
    in_specs += [_full_spec(p) for p in list(ps) + list(cs)]
    out_specs = [_x_spec(tm, w, 0, 1) for (w, _, _) in outs]
    out_shape = [jax.ShapeDtypeStruct((rows, tot), dt) for (_, tot, dt) in outs]
    return pl.pallas_call(
        body, grid=(rows // tm, ncol), in_specs=in_specs, out_specs=out_specs, out_shape=out_shape,
        name=name, compiler_params=_params(("parallel", "parallel")),
    )(*[x[0] for x in xs], *ps, *cs)


def _row_bwd(fn, xs, ps, cs, cts, dxs, name, adds=None, tm=ROW_TILE, ncol=1, rows=None):
    rows = ROWS if rows is None else rows
    nx, npar, nc, nct = len(xs), len(ps), len(cs), len(cts)
    adds = [None] * nx if adds is None else adds
    add_idx = [k for k in range(nx) if adds[k] is not None]

    def body(*refs):
        i, j = pl.program_id(0), pl.program_id(1)
        pos = 0
        x_refs = refs[pos:pos + nx]; pos += nx
        p_refs = refs[pos:pos + npar]; pos += npar
        c_refs = refs[pos:pos + nc]; pos += nc
        ct_refs = refs[pos:pos + nct]; pos += nct
        add_refs = refs[pos:pos + len(add_idx)]; pos += len(add_idx)
        dx_refs = refs[pos:pos + nx]; pos += nx
        dp_refs = refs[pos:pos + npar]
        xv = [r[...].astype(F32) for r in x_refs]
        pv = [r[...] for r in p_refs]
        cv = [r[...] for r in c_refs]
        _, vjp = jax.vjp(lambda *args: fn(i, *args, *cv), *xv, *pv)
        grads = vjp(tuple(r[...].astype(F32) for r in ct_refs))
        for k in range(nx):
            val = grads[k]
            if adds[k] is not None:
                val = val + add_refs[add_idx.index(k)][...]
            dx_refs[k][...] = val.astype(dx_refs[k].dtype)
        if npar:
            @pl.when((i == 0) & (j == 0))
            def _():
                for r in dp_refs:
                    r[...] = jnp.zeros(r.shape, r.dtype)

            for k in range(npar):
                dp_refs[k][...] += grads[nx + k]

    in_specs = [_x_spec(tm, w, cb0, mv) for (_, w, cb0, mv) in xs]
    in_specs += [_full_spec(p) for p in list(ps) + list(cs)]
    in_specs += [_x_spec(tm, w, cb0, mv) for (_, w, cb0, mv) in cts]
    in_specs += [_x_spec(tm, xs[k][1], 0, 1) for k in add_idx]
    out_specs = [_x_spec(tm, xs[k][1], 0, 1) for k in range(nx)] + [_full_spec(p) for p in ps]
    out_shape = [jax.ShapeDtypeStruct((rows, tot), dt) for (tot, dt) in dxs]
    out_shape += [jax.ShapeDtypeStruct(p.shape, F32) for p in ps]
    sem = ("arbitrary", "arbitrary") if npar else ("parallel", "parallel")
    return pl.pallas_call(
        body, grid=(rows // tm, ncol), in_specs=in_specs, out_specs=out_specs, out_shape=out_shape,
        name=name, compiler_params=_params(sem),
    )(*[x[0] for x in xs], *ps, *cs, *[c[0] for c in cts], *[adds[k] for k in add_idx])


def _rms_fn(i, h, g):
    return (h * lax.rsqrt(jnp.mean(h * h, axis=1, keepdims=True) + NORM_EPS) * g,)


def _fox_pre_fn(i, q, k, f, qg, kg, bf, e_mean, e_expand):
    tm = q.shape[0]

    def headnorm(x, gain, scale):
        msq = _dot_mid(x * x, e_mean, "nn")
        r = _dot_mid(lax.rsqrt(msq + NORM_EPS), e_expand, "nn")
        return x * r * gain * scale

    qn = headnorm(q, qg, FOX_DH ** -0.5)
    kn = headnorm(k, kg, 1.0)
    keep = (_iota((tm, LANE), 1) < FOX_HEADS).astype(F32) * _row_mask(i, tm)
    return qn, kn, _log_sigmoid(f + bf) * keep


def _fox_post_fn(i, o, gate):
    return (o * jax.nn.sigmoid(gate) * _row_mask(i, o.shape[0]),)


def _gla_pre_fn(i, alr, wa, ba):
    z = _dot_ad(alr, wa, "nn") + ba
    return (_log_sigmoid(z) * (1.0 / 16.0) * _row_mask(i, alr.shape[0]),)


def _headwise_norm_gate(o, gate, gain, dv, mask):
    pieces = []
    for h in range(o.shape[1] // dv):
        oh = o[:, h * dv:(h + 1) * dv]
        pieces.append(oh * lax.rsqrt(jnp.mean(oh * oh, axis=1, keepdims=True) + NORM_EPS) * gain)
    return jnp.concatenate(pieces, axis=1) * _silu(gate) * mask


def _gla_post_fn(i, o, r, gain):
    return (_headwise_norm_gate(o, r, gain, GLA_DV, _row_mask(i, o.shape[0])),)


def _gdn_post_fn(i, o, gate, gain):
    return (_headwise_norm_gate(o, gate, gain, GDN_DV, _row_mask(i, o.shape[0])),)


def _gdn_gates_fn(i, sm, alog, dtb):
    tm = sm.shape[0]
    lane = _iota((tm, LANE), 1)
    g = -jnp.exp(alog) * _softplus(sm + dtb)
    beta = jax.nn.sigmoid(sm)
    out = jnp.where(lane < GDN_HEADS, g, jnp.where(lane < 2 * GDN_HEADS, beta, 0.0))
    return (out * _row_mask(i, tm),)


def _gdn_conv_fn(kind):
    def fn(i, prev, cur, w):
        tm = cur.shape[0]
        xw = jnp.concatenate([prev * jnp.where(i > 0, 1.0, 0.0), cur], axis=0)
        t = _iota((tm, 2 * tm), 0)
        s = _iota((tm, 2 * tm), 1)
        y = jnp.zeros(cur.shape, F32)
        for j in range(GDN_CONV):
            sel = (s == t + (tm - (GDN_CONV - 1) + j)).astype(F32)
            y = y + _dot_mid(sel, xw, "nn") * jnp.sum(jnp.where(_iota(w.shape, 0) == j, w, 0.0), axis=0, keepdims=True)
        a = _silu(y * _row_mask(i, tm))
        if kind == "v":
            return (a,)
        scale = GDN_DK ** -0.5 if kind == "q" else 1.0
        pieces = []
        for h in range(GDN_HEADS):
            ah = a[:, h * GDN_DK:(h + 1) * GDN_DK]
            pieces.append(ah * lax.rsqrt(jnp.sum(ah * ah, axis=1, keepdims=True) + NORM_EPS) * scale)
        return (jnp.concatenate(pieces, axis=1),)

    return fn


def _seq_cumsum(x, reverse, name, pairs=None):
    rows, width = x.shape
    nt = rows // ROW_TILE

    def body(*refs):
        x_ref, o_ref, carry = refs[0], refs[-2], refs[-1]

        @pl.when(pl.program_id(0) == 0)
        def _():
            carry[...] = jnp.zeros(carry.shape, F32)

        xv = x_ref[...]
        if pairs is not None:
            lane = _iota(xv.shape, 1)
            for g in range(pairs.shape[0]):
                ta, tb = _pair_cols(refs[1][g])
                xv = xv + jnp.where(lane == 2 * g, ta, 0.0) + jnp.where(lane == 2 * g + 1, tb, 0.0)
        r = _iota((ROW_TILE, ROW_TILE), 0)
        c = _iota((ROW_TILE, ROW_TILE), 1)
        tri = ((c >= r) if reverse else (c <= r)).astype(F32)
        acc = _dot_hi(tri, xv, "nn") + carry[...]
        o_ref[...] = acc
        edge = 0 if reverse else ROW_TILE - 1
        carry[...] = jnp.sum(jnp.where(_iota(acc.shape, 0) == edge, acc, 0.0), axis=0, keepdims=True)

    idx = (lambda i: (nt - 1 - i, 0)) if reverse else (lambda i: (i, 0))
    idx3 = (lambda i: (0, nt - 1 - i, 0)) if reverse else (lambda i: (0, i, 0))
    operands, in_specs = [x], [pl.BlockSpec((ROW_TILE, width), idx)]
    if pairs is not None:
        operands.append(pairs)
        in_specs.append(pl.BlockSpec((pairs.shape[0], ROW_TILE, LANE), idx3))
    return pl.pallas_call(
        body, grid=(nt,), in_specs=in_specs,
        out_specs=pl.BlockSpec((ROW_TILE, width), idx), out_shape=jax.ShapeDtypeStruct(x.shape, F32),
        scratch_shapes=[pltpu.VMEM((1, width), F32)], name=name, compiler_params=_params(("arbitrary",)),
    )(*operands)


FOX_PAIRS = FOX_HEADS // 2
FOX_V_BLOCK = 2 * D_MODEL // LANE


def _pair_cols(x):
    return _lane_col(x, 0), _lane_col(x, FOX_DH)


ATT_TILE = 384
assert ROWS % ATT_TILE == 0
N_ATT = ROWS // ATT_TILE
KEY_OFF = 1e30


def _fox_scores(qa, qb, kj, ca, cb, cra, crb, mask):
    sa = _dot(qa, kj, "nt") + (ca - cra)
    sb = _dot(qb, kj, "nt") + (cb - crb)
    if mask is None:
        return sa, sb
    return jnp.where(mask, sa, NEG), jnp.where(mask, sb, NEG)


def _fox_diag_mask():
    return _iota((ATT_TILE, ATT_TILE), 1) <= _iota((ATT_TILE, ATT_TILE), 0)


def _att_off(j):
    return pl.multiple_of(j * ATT_TILE, ATT_TILE)


def _fox_attn_fwd(qn, kn, proj, c2, crow):
    nt = ROWS // ROW_TILE

    def body(q_ref, k_ref, v_ref, c2_ref, crow_ref, o_ref, lse_ref):
        i = pl.program_id(1)
        q = q_ref[...]
        is_a = _iota((ATT_TILE, LANE), 1) < FOX_DH
        qa, qb = jnp.where(is_a, q, 0), jnp.where(is_a, 0, q)
        ca, cb = _pair_cols(c2_ref[0])

        def block(off, size, mask, carry):
            ma, la, acca, mb, lb, accb = carry
            kj = k_ref[pl.ds(off, size), :]
            vj = v_ref[pl.ds(off, size), :].astype(BF16)
            sa, sb = _fox_scores(qa, qb, kj, ca, cb, crow_ref[0, 0:1, pl.ds(off, size)],
                                 crow_ref[0, 1:2, pl.ds(off, size)], mask)

            def online(s, m, l, acc):
                m_new = jnp.maximum(m, jnp.max(s, axis=1, keepdims=True))
                p = jnp.exp(s - m_new)
                alpha = jnp.exp(m - m_new)
                return m_new, alpha * l + jnp.sum(p, axis=1, keepdims=True), alpha * acc + _dot(p, vj, "nn")

            return online(sa, ma, la, acca) + online(sb, mb, lb, accb)

        m0 = jnp.full((ATT_TILE, 1), NEG, F32)
        l0 = jnp.zeros((ATT_TILE, 1), F32)
        a0 = jnp.zeros((ATT_TILE, LANE), F32)
        carry = lax.fori_loop(0, i, lambda j, cr: block(_att_off(j), ATT_TILE, None, cr), (m0, l0, a0, m0, l0, a0))
        ma, la, acca, mb, lb, accb = block(_att_off(i), ATT_TILE, _fox_diag_mask(), carry)
        o_ref[...] = jnp.where(is_a, acca / la, accb / lb)
        lse_ref[0] = jnp.where(is_a, ma + jnp.log(la), mb + jnp.log(lb))

    return pl.pallas_call(
        body, grid=(FOX_PAIRS, N_ATT),
        in_specs=[
            pl.BlockSpec((ATT_TILE, LANE), lambda g, i: (i, g)),
            pl.BlockSpec((ROWS, LANE), lambda g, i: (0, g)),
            pl.BlockSpec((ROWS, LANE), lambda g, i: (0, FOX_V_BLOCK + g)),
            pl.BlockSpec((1, ATT_TILE, LANE), lambda g, i: (g, i, 0)),
            pl.BlockSpec((1, 2, ROWS), lambda g, i: (g, 0, 0)),
        ],
        out_specs=[
            pl.BlockSpec((ATT_TILE, LANE), lambda g, i: (i, g)),
            pl.BlockSpec((1, ATT_TILE, LANE), lambda g, i: (g, i, 0)),
        ],
        out_shape=[jax.ShapeDtypeStruct((ROWS, D_MODEL), F32), jax.ShapeDtypeStruct((FOX_PAIRS, ROWS, LANE), F32)],
        name="fox_attn_fwd", compiler_params=_params(("parallel", "arbitrary")),
    )(qn, kn, proj, c2, crow)


def _fox_attn_dq(qn, kn, proj, c2, crow, lse2, o, do, blockdiag):
    nt = ROWS // ROW_TILE

    def body(q_ref, k_ref, v_ref, c2_ref, crow_ref, lse_ref, o_ref, do_ref, bd_ref, dq_ref, delta_ref, dcq_ref):
        i = pl.program_id(1)
        q = q_ref[...]
        is_a = _iota((ATT_TILE, LANE), 1) < FOX_DH
        qa, qb = jnp.where(is_a, q, 0), jnp.where(is_a, 0, q)
        ca, cb = _pair_cols(c2_ref[0])
        lsa, lsb = _pair_cols(lse_ref[0])
        do_t = do_ref[...]
        delta2 = _dot_hi(do_t * o_ref[...], bd_ref[...], "nn")
        delta_ref[0] = delta2
        dla, dlb = _pair_cols(delta2)
        doa, dob = jnp.where(is_a, do_t, 0.0).astype(BF16), jnp.where(is_a, 0.0, do_t).astype(BF16)

        def block(off, size, mask, carry):
            dqa, dqb, rsa, rsb = carry
            kj = k_ref[pl.ds(off, size), :]
            vj = v_ref[pl.ds(off, size), :].astype(BF16)
            sa, sb = _fox_scores(qa, qb, kj, ca, cb, crow_ref[0, 0:1, pl.ds(off, size)],
                                 crow_ref[0, 1:2, pl.ds(off, size)], mask)
            dsa = jnp.exp(sa - lsa) * (_dot(doa, vj, "nt") - dla)
            dsb = jnp.exp(sb - lsb) * (_dot(dob, vj, "nt") - dlb)
            return (dqa + _dot(dsa, kj, "nn"), dqb + _dot(dsb, kj, "nn"),
                    rsa + jnp.sum(dsa, axis=1, keepdims=True), rsb + jnp.sum(dsb, axis=1, keepdims=True))

        z = jnp.zeros((ATT_TILE, LANE), F32)
        zc = jnp.zeros((ATT_TILE, 1), F32)
        carry = lax.fori_loop(0, i, lambda j, cr: block(_att_off(j), ATT_TILE, None, cr), (z, z, zc, zc))
        dqa, dqb, rsa, rsb = block(_att_off(i), ATT_TILE, _fox_diag_mask(), carry)
        dq_ref[...] = jnp.where(is_a, dqa, dqb)
        dcq_ref[0] = jnp.where(is_a, rsa, rsb)

    tile = pl.BlockSpec((ATT_TILE, LANE), lambda g, i: (i, g))
    pair_tile = pl.BlockSpec((1, ATT_TILE, LANE), lambda g, i: (g, i, 0))
    return pl.pallas_call(
        body, grid=(FOX_PAIRS, N_ATT),
        in_specs=[
            tile,
            pl.BlockSpec((ROWS, LANE), lambda g, i: (0, g)),
            pl.BlockSpec((ROWS, LANE), lambda g, i: (0, FOX_V_BLOCK + g)),
            pair_tile,
            pl.BlockSpec((1, 2, ROWS), lambda g, i: (g, 0, 0)),
            pair_tile, tile, tile,
            pl.BlockSpec((LANE, LANE), lambda g, i: (0, 0)),
        ],
        out_specs=[tile, pair_tile, pair_tile],
        out_shape=[jax.ShapeDtypeStruct((ROWS, D_MODEL), F32), jax.ShapeDtypeStruct((FOX_PAIRS, ROWS, LANE), F32),
                   jax.ShapeDtypeStruct((FOX_PAIRS, ROWS, LANE), F32)],
        name="fox_attn_dq", compiler_params=_params(("parallel", "arbitrary")),
    )(qn, kn, proj, c2, crow, lse2, o, do, blockdiag)


def _fox_attn_dkv(qn, kn, proj, c2, crow, lse2, delta2, do):
    nt = ROWS // ROW_TILE

    def body(q_ref, k_ref, v_ref, c2_ref, crow_ref, lse_ref, delta_ref, do_ref, dk_ref, dv_ref, dc_ref):
        j = pl.program_id(1)
        kj = k_ref[...]
        vj = v_ref[...].astype(BF16)
        cra, crb = crow_ref[0, 0:1, :], crow_ref[0, 1:2, :]

        def block(off, size, mask, carry):
            dk, dv, dca, dcb = carry
            is_a = _iota((size, LANE), 1) < FOX_DH
            q = q_ref[pl.ds(off, size), :]
            qa, qb = jnp.where(is_a, q, 0), jnp.where(is_a, 0, q)
            do_t = do_ref[pl.ds(off, size), :]
            doa, dob = jnp.where(is_a, do_t, 0.0).astype(BF16), jnp.where(is_a, 0.0, do_t).astype(BF16)
            ca, cb = _pair_cols(c2_ref[0, pl.ds(off, size), :])
            lsa, lsb = _pair_cols(lse_ref[0, pl.ds(off, size), :])
            dla, dlb = _pair_cols(delta_ref[0, pl.ds(off, size), :])
            sa, sb = _fox_scores(qa, qb, kj, ca, cb, cra, crb, mask)
            pa, pb = jnp.exp(sa - lsa), jnp.exp(sb - lsb)
            dsa = pa * (_dot(doa, vj, "nt") - dla)
            dsb = pb * (_dot(dob, vj, "nt") - dlb)
            dv = dv + _dot(pa, doa, "tn") + _dot(pb, dob, "tn")
            dk = dk + _dot(dsa, qa, "tn") + _dot(dsb, qb, "tn")
            dca = dca - jnp.sum(dsa, axis=0, keepdims=True)
            dcb = dcb - jnp.sum(dsb, axis=0, keepdims=True)
            return dk, dv, dca, dcb

        z = jnp.zeros((ATT_TILE, LANE), F32)
        zr = jnp.zeros((1, ATT_TILE), F32)
        carry = block(_att_off(j), ATT_TILE, _fox_diag_mask(), (z, z, zr, zr))
        dk, dv, dca, dcb = lax.fori_loop(j + 1, N_ATT, lambda i, cr: block(_att_off(i), ATT_TILE, None, cr), carry)
        dk_ref[...] = dk
        dv_ref[...] = dv.astype(dv_ref.dtype)
        dc_ref[0, 0:1, :] = dca
        dc_ref[0, 1:2, :] = dcb

    whole = pl.BlockSpec((ROWS, LANE), lambda g, j: (0, g))
    pair_whole = pl.BlockSpec((1, ROWS, LANE), lambda g, j: (g, 0, 0))
    tile = pl.BlockSpec((ATT_TILE, LANE), lambda g, j: (j, g))
    return pl.pallas_call(
        body, grid=(FOX_PAIRS, N_ATT),
        in_specs=[
            whole, tile,
            pl.BlockSpec((ATT_TILE, LANE), lambda g, j: (j, FOX_V_BLOCK + g)),
            pair_whole,
            pl.BlockSpec((1, 2, ATT_TILE), lambda g, j: (g, 0, j)),
            pair_whole, pair_whole, whole,
        ],
        out_specs=[tile, tile, pl.BlockSpec((1, 2, ATT_TILE), lambda g, j: (g, 0, j))],
        out_shape=[jax.ShapeDtypeStruct((ROWS, D_MODEL), F32), jax.ShapeDtypeStruct((ROWS, D_MODEL), BF16),
                   jax.ShapeDtypeStruct((FOX_PAIRS, 2, ROWS), F32)],
        name="fox_attn_dkv", compiler_params=_params(("parallel", "arbitrary")),
    )(qn, kn, proj, c2, crow, lse2, delta2, do)


def _gla_chunk(q, k, v, g, st, dot):
    c = q.shape[0]
    incl = _iota((c, c), 1) <= _iota((c, c), 0)
    b = _dot_hi(incl.astype(F32), g, "nn")
    qe = q * (GLA_DK ** -0.5) * jnp.exp(b)
    ke = k * jnp.exp(-b)
    att = jnp.where(incl, dot(qe, ke, "nt"), 0.0)
    o = dot(att, v, "nn") + dot(qe, st, "nt")
    b_last = _last_row(b)
    kd = k * jnp.exp(b_last - b)
    return o, st * jnp.exp(b_last) + dot(v, kd, "tn")


def _tri_inverse(low):
    c = low.shape[0]
    eye = (_iota((c, c), 0) == _iota((c, c), 1)).astype(F32)
    x, p = eye - low, low
    span = 2
    while span < c:
        p = _dot_mid(p, p, "nn")
        x = _dot_mid(x, eye + p, "nn")
        span *= 2
    return x


def _solve(low, rhs):
    return _dot_mid(_tri_inverse(low), rhs, "nn")


@jax.custom_vjp
def _solve_ad(low, rhs):
    return _solve(low, rhs)


def _solve_ad_fwd(low, rhs):
    inv = _tri_inverse(low)
    sol = _dot_mid(inv, rhs, "nn")
    return sol, (inv, sol)


def _solve_ad_bwd(res, ct):
    inv, sol = res
    drhs = _dot_mid(inv, ct, "tn")
    return -_dot_mid(drhs, sol, "nt"), drhs


_solve_ad.defvjp(_solve_ad_fwd, _solve_ad_bwd)


def _gdn_chunk(q, k, v, gates, st, h, dot, solve):
    c = q.shape[0]
    gcol = _lane_col(gates, h)
    beta = _lane_col(gates, h + GDN_HEADS)
    incl = _iota((c, c), 1) <= _iota((c, c), 0)
    strict = _iota((c, c), 1) < _iota((c, c), 0)
    incl_f = incl.astype(F32)
    b = _dot_hi(incl_f, jnp.broadcast_to(gcol, (c, LANE)), "nn")
    g_sq = jnp.broadcast_to(gcol, (c, c))
    b_col = _dot_hi(incl_f, g_sq, "nn")
    b_row = _dot_hi(g_sq, (_iota((c, c), 0) <= _iota((c, c), 1)).astype(F32), "tn")
    diff = b_col - b_row
    kb, vb = k * beta, v * beta
    low = dot(kb, k, "nt") * jnp.exp(jnp.where(strict, diff, NEG))
    eb = jnp.exp(b)
    sol = solve(low, jnp.concatenate([vb, kb * eb], axis=1))
    v_new = sol[:, :GDN_DV] - dot(sol[:, GDN_DV:], st, "nn")
    att = dot(q, k, "nt") * jnp.exp(jnp.where(incl, diff, NEG))
    o = dot(att, v_new, "nn") + dot(q * eb, st, "nn")
    b_last = _last_row(b)
    kd = k * jnp.exp(b_last - b)
    return o, st * jnp.exp(b_last) + dot(kd, v_new, "tn")


N_CHUNKS = ROWS // CHUNK
GDN_GROUP = 4


def _gla_scan_fwd(proj, garr):
    def body(q_ref, k_ref, v_ref, g_ref, o_ref, states_ref, st):
        @pl.when(pl.program_id(1) == 0)
        def _():
            st[...] = jnp.zeros(st.shape, F32)

        s_in = st[...]
        states_ref[0, 0] = s_in
        o, s_out = _gla_chunk(q_ref[...], k_ref[...], v_ref[...], g_ref[...], s_in, _dot)
        o_ref[...] = o
        st[...] = s_out

    qk = GLA_QK // GLA_DK
    return pl.pallas_call(
        body, grid=(GLA_HEADS, N_CHUNKS),
        in_specs=[
            pl.BlockSpec((CHUNK, GLA_DK), lambda h, c: (c, h)),
            pl.BlockSpec((CHUNK, GLA_DK), lambda h, c: (c, qk + h)),
            pl.BlockSpec((CHUNK, GLA_DV), lambda h, c: (c, qk + h)),
            pl.BlockSpec((CHUNK, GLA_DK), lambda h, c: (c, h)),
        ],
        out_specs=[
            pl.BlockSpec((CHUNK, GLA_DV), lambda h, c: (c, h)),
            pl.BlockSpec((1, 1, GLA_DV, GLA_DK), lambda h, c: (h, c, 0, 0)),
        ],
        out_shape=[jax.ShapeDtypeStruct((ROWS, GLA_V), F32),
                   jax.ShapeDtypeStruct((GLA_HEADS, N_CHUNKS, GLA_DV, GLA_DK), F32)],
        scratch_shapes=[pltpu.VMEM((GLA_DV, GLA_DK), F32)],
        name="gla_scan_fwd", compiler_params=_params(("parallel", "arbitrary")),
    )(proj, proj, proj, garr)


def _gla_scan_bwd(proj, garr, states, do):
    last = N_CHUNKS - 1

    def body(q_ref, k_ref, v_ref, g_ref, s_ref, do_ref, dq_ref, dk_ref, dv_ref, dg_ref, dst):
        @pl.when(pl.program_id(1) == 0)
        def _():
            dst[...] = jnp.zeros(dst.shape, F32)

        _, vjp = jax.vjp(lambda q, k, v, g, s: _gla_chunk(q, k, v, g, s, _dot_ad),
                         q_ref[...], k_ref[...], v_ref[...], g_ref[...], s_ref[0, 0])
        dq, dk, dv, dg, ds = vjp((do_ref[...], dst[...]))
        dq_ref[...] = dq.astype(dq_ref.dtype)
        dk_ref[...] = dk.astype(dk_ref.dtype)
        dv_ref[...] = dv.astype(dv_ref.dtype)
        dg_ref[...] = dg
        dst[...] = ds

    qk = GLA_QK // GLA_DK
    kspec = pl.BlockSpec((CHUNK, GLA_DK), lambda h, c: (last - c, h))
    vspec = pl.BlockSpec((CHUNK, GLA_DV), lambda h, c: (last - c, h))
    return pl.pallas_call(
        body, grid=(GLA_HEADS, N_CHUNKS),
        in_specs=[
            kspec,
            pl.BlockSpec((CHUNK, GLA_DK), lambda h, c: (last - c, qk + h)),
            pl.BlockSpec((CHUNK, GLA_DV), lambda h, c: (last - c, qk + h)),
            kspec,
            pl.BlockSpec((1, 1, GLA_DV, GLA_DK), lambda h, c: (h, last - c, 0, 0)),
            vspec,
        ],
        out_specs=[kspec, kspec, vspec, kspec],
        out_shape=[jax.ShapeDtypeStruct((ROWS, GLA_QK), BF16), jax.ShapeDtypeStruct((ROWS, GLA_QK), BF16),
                   jax.ShapeDtypeStruct((ROWS, GLA_V), BF16), jax.ShapeDtypeStruct((ROWS, GLA_QK), F32)],
        scratch_shapes=[pltpu.VMEM((GLA_DV, GLA_DK), F32)],
        name="gla_scan_bwd", compiler_params=_params(("parallel", "arbitrary")),
    )(proj, proj, proj, garr, states, do)


def _gdn_scan_fwd(qn, kn, vn, gates):
    def body(q_ref, k_ref, v_ref, g_ref, o_ref, states_ref, st):
        @pl.when((pl.program_id(0) == 0) & (pl.program_id(1) == 0))
        def _():
            st[...] = jnp.zeros(st.shape, F32)

        gates = g_ref[...]
        heads = [pl.program_id(1) * GDN_GROUP + hh for hh in range(GDN_GROUP)]
        lanes = [slice(hh * GDN_DK, (hh + 1) * GDN_DK) for hh in range(GDN_GROUP)]
        s_in = [st[h] for h in heads]
        outs = [_gdn_chunk(q_ref[:, ln], k_ref[:, ln], v_ref[:, ln], gates, s, h, _dot, _solve)
                for ln, s, h in zip(lanes, s_in, heads)]
        for hh in range(GDN_GROUP):
            states_ref[hh, 0] = s_in[hh]
            o_ref[:, lanes[hh]] = outs[hh][0]
            st[heads[hh]] = outs[hh][1]

    hspec = pl.BlockSpec((CHUNK, GDN_GROUP * GDN_DK), lambda c, h: (c, h))
    return pl.pallas_call(
        body, grid=(N_CHUNKS, GDN_HEADS // GDN_GROUP),
        in_specs=[hspec, hspec, hspec, pl.BlockSpec((CHUNK, LANE), lambda c, h: (c, 0))],
        out_specs=[hspec, pl.BlockSpec((GDN_GROUP, 1, GDN_DK, GDN_DV), lambda c, h: (h, c, 0, 0))],
        out_shape=[jax.ShapeDtypeStruct((ROWS, D_MODEL), F32),
                   jax.ShapeDtypeStruct((GDN_HEADS, N_CHUNKS, GDN_DK, GDN_DV), F32)],
        scratch_shapes=[pltpu.VMEM((GDN_HEADS, GDN_DK, GDN_DV), F32)],
        name="gdn_scan_fwd", compiler_params=_params(("arbitrary", "arbitrary")),
    )(qn, kn, vn, gates)


def _gdn_scan_bwd(qn, kn, vn, gates, states, do):
    last = N_CHUNKS - 1

    def body(q_ref, k_ref, v_ref, g_ref, s_ref, do_ref, dq_ref, dk_ref, dv_ref, dg_ref, dst):
        @pl.when((pl.program_id(0) == 0) & (pl.program_id(1) == 0))
        def _():
            dst[...] = jnp.zeros(dst.shape, F32)

        @pl.when(pl.program_id(1) == 0)
        def _():
            dg_ref[...] = jnp.zeros(dg_ref.shape, F32)

        gates = g_ref[...]
        heads = [pl.program_id(1) * GDN_GROUP + hh for hh in range(GDN_GROUP)]
        lanes = [slice(hh * GDN_DK, (hh + 1) * GDN_DK) for hh in range(GDN_GROUP)]
        ds_in = [dst[h] for h in heads]
        grads = []
        for hh in range(GDN_GROUP):
            ln = lanes[hh]
            _, vjp = jax.vjp(
                functools.partial(lambda hd, q, k, v, g, s: _gdn_chunk(q, k, v, g, s, hd, _dot_ad, _solve_ad),
                                  heads[hh]),
                q_ref[:, ln], k_ref[:, ln], v_ref[:, ln], gates, s_ref[hh, 0])
            grads.append(vjp((do_ref[:, ln], ds_in[hh])))
        dg_sum = grads[0][3]
        for hh in range(1, GDN_GROUP):
            dg_sum = dg_sum + grads[hh][3]
        for hh in range(GDN_GROUP):
            dq_ref[:, lanes[hh]] = grads[hh][0]
            dk_ref[:, lanes[hh]] = grads[hh][1]
            dv_ref[:, lanes[hh]] = grads[hh][2]
            dst[heads[hh]] = grads[hh][4]
        dg_ref[...] += dg_sum

    hspec = pl.BlockSpec((CHUNK, GDN_GROUP * GDN_DK), lambda c, h: (last - c, h))
    gspec = pl.BlockSpec((CHUNK, LANE), lambda c, h: (last - c, 0))
    return pl.pallas_call(
        body, grid=(N_CHUNKS, GDN_HEADS // GDN_GROUP),
        in_specs=[hspec, hspec, hspec, gspec,
                  pl.BlockSpec((GDN_GROUP, 1, GDN_DK, GDN_DV), lambda c, h: (h, last - c, 0, 0)), hspec],
        out_specs=[hspec, hspec, hspec, gspec],
        out_shape=[jax.ShapeDtypeStruct((ROWS, D_MODEL), F32)] * 3 + [jax.ShapeDtypeStruct((ROWS, LANE), F32)],
        scratch_shapes=[pltpu.VMEM((GDN_HEADS, GDN_DK, GDN_DV), F32)],
        name="gdn_scan_bwd", compiler_params=_params(("arbitrary", "arbitrary")),
    )(qn, kn, vn, gates, states, do)


def _gdn_conv_fwd(proj, w, kind, cb):
    fn = _gdn_conv_fn(kind)
    nt = ROWS // ROW_TILE

    def body(prev_ref, cur_ref, w_ref, o_ref):
        (out,) = fn(pl.program_id(0), prev_ref[...], cur_ref[...], w_ref[...])
        o_ref[...] = out

    return pl.pallas_call(
        body, grid=(nt,),
        in_specs=[pl.BlockSpec((ROW_TILE, D_MODEL), lambda i: (jnp.maximum(i - 1, 0), cb)),
                  pl.BlockSpec((ROW_TILE, D_MODEL), lambda i: (i, cb)),
                  pl.BlockSpec(w.shape, lambda i: (0, 0))],
        out_specs=pl.BlockSpec((ROW_TILE, D_MODEL), lambda i: (i, 0)),
        out_shape=jax.ShapeDtypeStruct((ROWS, D_MODEL), F32),
        name="gdn_conv_fwd_" + kind, compiler_params=_params(("parallel",)),
    )(proj, proj, w)


def _gdn_conv_bwd(proj, w, kind, cb, ct):
    fn = _gdn_conv_fn(kind)
    nt = ROWS // ROW_TILE

    def body(prev_ref, cur_ref, w_ref, ct_ref, dx_ref, dw_ref, carry):
        step = pl.program_id(0)
        i = nt - 1 - step

        @pl.when(step == 0)
        def _():
            carry[...] = jnp.zeros(carry.shape, F32)
            dw_ref[...] = jnp.zeros(dw_ref.shape, F32)

        _, vjp = jax.vjp(lambda p, c, ww: fn(i, p, c, ww), prev_ref[...], cur_ref[...], w_ref[...])
        dprev, dcur, dw = vjp((ct_ref[...],))
        dx_ref[...] = (dcur + carry[...]).astype(dx_ref.dtype)
        carry[...] = dprev
        dw_ref[...] += dw

    return pl.pallas_call(
        body, grid=(nt,),
        in_specs=[pl.BlockSpec((ROW_TILE, D_MODEL), lambda s: (jnp.maximum(nt - 2 - s, 0), cb)),
                  pl.BlockSpec((ROW_TILE, D_MODEL), lambda s: (nt - 1 - s, cb)),
                  pl.BlockSpec(w.shape, lambda s: (0, 0)),
                  pl.BlockSpec((ROW_TILE, D_MODEL), lambda s: (nt - 1 - s, 0))],
        out_specs=[pl.BlockSpec((ROW_TILE, D_MODEL), lambda s: (nt - 1 - s, 0)),
                   pl.BlockSpec(w.shape, lambda s: (0, 0))],
        out_shape=[jax.ShapeDtypeStruct((ROWS, D_MODEL), BF16), jax.ShapeDtypeStruct(w.shape, F32)],
        scratch_shapes=[pltpu.VMEM((ROW_TILE, D_MODEL), F32)],
        name="gdn_conv_bwd_" + kind, compiler_params=_params(("arbitrary",)),
    )(proj, proj, w, ct)


def _loss_head(h, target):
    nt = ROWS // ROW_TILE
    lead = PAD_ROWS // ROW_TILE

    def body(h_ref, t_ref, dh_ref, acc_ref):
        i = pl.program_id(0)

        @pl.when(i == 0)
        def _():
            acc_ref[...] = jnp.zeros(acc_ref.shape, F32)

        err = (h_ref[...] - t_ref[...]) * jnp.where(i >= lead, 1.0, 0.0)
        dh_ref[...] = err * (1.0 / D_MODEL)
        acc_ref[...] += jnp.sum(err * err, axis=0, keepdims=True)

    return pl.pallas_call(
        body, grid=(nt,),
        in_specs=[pl.BlockSpec((ROW_TILE, D_MODEL), lambda i: (i, 0)),
                  pl.BlockSpec((ROW_TILE, D_MODEL), lambda i: (jnp.maximum(i - lead, 0), 0))],
        out_specs=[pl.BlockSpec((ROW_TILE, D_MODEL), lambda i: (i, 0)),
                   pl.BlockSpec((1, D_MODEL), lambda i: (0, 0))],
        out_shape=[jax.ShapeDtypeStruct((ROWS, D_MODEL), F32), jax.ShapeDtypeStruct((1, D_MODEL), F32)],
        name="loss_head", compiler_params=_params(("arbitrary",)),
    )(h, target)


MESH = pl.DeviceIdType.MESH
ANY = pl.BlockSpec(memory_space=pl.ANY)


N_PEERS = N_DEV - 1


HBM_SPEC = pl.BlockSpec(memory_space=pltpu.HBM)
SEM_SPEC = pl.BlockSpec(memory_space=pltpu.SEMAPHORE)
SIDE_EFFECT = pltpu.SideEffectType.DATAFLOW_SIDE_EFFECTING


def _me_and_peers():
    x, y, c = lax.axis_index("x"), lax.axis_index("y"), lax.axis_index("c")
    peers = []
    for k in range(1, N_DEV):
        pid = (1 - x if k & 4 else x, 1 - y if k & 2 else y, 1 - c if k & 1 else c)
        peers.append((4 * pid[0] + 2 * pid[1] + pid[2], pid))
    return 4 * x + 2 * y + c, peers


def _split_copies(src_refs, land_refs, send_sems, recv_sems, spread):
    me, peers = _me_and_peers()
    out = []
    for t, (src, land) in enumerate(zip(src_refs, land_refs)):
        for k, (peer, pid) in enumerate(peers):
            sems = dict(send_sem=send_sems.at[N_PEERS * t + k], recv_sem=recv_sems.at[N_PEERS * t + k],
                        device_id=pid, device_id_type=MESH)
            piece = src if spread else src.at[peer]
            out.append((pltpu.make_async_remote_copy(src_ref=piece, dst_ref=land.at[me], **sems),
                        pltpu.make_async_remote_copy(src_ref=piece, dst_ref=land.at[peer], **sems)))
    return out


def _send_start(srcs, spread, name):
    n = len(srcs)
    lands = [lax.empty((N_DEV,) + (s.shape if spread else s.shape[1:]), s.dtype) for s in srcs]

    def body(*refs):
        for out, _ in _split_copies(refs[:n], refs[n:2 * n], refs[2 * n], refs[2 * n + 1], spread):
            out.start()
        refs[-1][...] = jnp.zeros(refs[-1].shape, F32)

    thru = [pltpu.HBM(a.shape, a.dtype) for a in list(srcs) + lands]
    res = pl.pallas_call(
        body, name=name,
        out_shape=(pltpu.SemaphoreType.DMA((N_PEERS * n,)), pltpu.SemaphoreType.DMA((N_PEERS * n,)), *thru,
                   jax.ShapeDtypeStruct((8, LANE), F32)),
        in_specs=[HBM_SPEC] * (2 * n),
        out_specs=(SEM_SPEC, SEM_SPEC, *[HBM_SPEC] * (2 * n), pl.BlockSpec(memory_space=pltpu.VMEM)),
        input_output_aliases={t: 2 + t for t in range(2 * n)},
        compiler_params=pltpu.CompilerParams(has_side_effects=SIDE_EFFECT),
    )(*[pltpu.with_memory_space_constraint(a, pltpu.HBM) for a in list(srcs) + lands])
    return res[0], res[1], list(res[2:2 + n]), list(res[2 + n:2 + 2 * n]), res[-1]


def _send_wait(started, spread, after, name):
    send_sems, recv_sems, srcs, lands, _ = started
    n = len(srcs)

    def body(*refs):
        for out, arrival in _split_copies(refs[:n], refs[n:2 * n], refs[2 * n], refs[2 * n + 1], spread):
            out.wait_send()
            arrival.wait_recv()

    thru = [pltpu.HBM(a.shape, a.dtype) for a in srcs + lands]
    res = pl.pallas_call(
        body, name=name, out_shape=tuple(thru),
        in_specs=[HBM_SPEC] * (2 * n) + [SEM_SPEC, SEM_SPEC, ANY], out_specs=tuple([HBM_SPEC] * (2 * n)),
        input_output_aliases={t: t for t in range(2 * n)},
        compiler_params=pltpu.CompilerParams(has_side_effects=SIDE_EFFECT),
    )(*srcs, *lands, send_sems, recv_sems, after)
    return list(res[n:])


def _adamw(recvs, w, m, v, name):
    n_layers = len(recvs)
    rows, cols = recvs[0].shape[1:]
    tr = _pick(rows, (256, 128, 64, 32, 16))
    nb = rows // tr

    def body(*refs):
        r_refs = refs[:n_layers]
        w_ref, m_ref, v_ref, g_ref, d_ref, m2_ref, v2_ref = refs[n_layers:]

        def update(r_ref):
            g = r_ref[0].astype(F32)
            for k in range(1, N_DEV):
                g = g + r_ref[k].astype(F32)
            m2 = ADAM_B1 * m_ref[...] + (1.0 - ADAM_B1) * g
            v2 = ADAM_B2 * v_ref[...] + (1.0 - ADAM_B2) * (g * g)
            m_hat = m2 / (1.0 - ADAM_B1 ** ADAM_STEP)
            v_hat = v2 / (1.0 - ADAM_B2 ** ADAM_STEP)
            g_ref[...] = g
            d_ref[...] = -ADAM_LR * (m_hat / (jnp.sqrt(v_hat) + ADAM_EPS) + ADAM_WD * w_ref[...])
            m2_ref[...] = m2
            v2_ref[...] = v2

        if n_layers == 1:
            update(r_refs[0])
        else:
            for layer in range(n_layers):
                pl.when(pl.program_id(0) == layer)(functools.partial(update, r_refs[layer]))

    def recv_spec(layer):
        return pl.BlockSpec((N_DEV, tr, cols), lambda a, i: (0, jnp.where(a == layer, i, 0), 0))

    spec = pl.BlockSpec((tr, cols), lambda a, i: (a * nb + i, 0))
    return pl.pallas_call(
        body, grid=(n_layers, nb),
        in_specs=[recv_spec(layer) for layer in range(n_layers)] + [spec, spec, spec],
        out_specs=[spec] * 4, out_shape=[jax.ShapeDtypeStruct(w.shape, F32)] * 4,
        name=name, compiler_params=_params(("arbitrary", "arbitrary")),
    )(*recvs, w, m, v)


def _padded_rows(n, mult):
    rows = -(-n // PACK_COLS)
    return -(-rows // mult) * mult


def _pack(flat_pieces, mult, dtype, lead=()):
    flat = jnp.concatenate([p.astype(dtype) for p in flat_pieces], axis=-1)
    n = flat.shape[-1]
    rows = _padded_rows(n, mult)
    flat = jnp.pad(flat, [(0, 0)] * len(lead) + [(0, rows * PACK_COLS - n)])
    return flat.reshape(lead + (rows, PACK_COLS))


def _unpack(packed, shapes):
    flat = packed.reshape(-1)
    out, off = {}, 0
    for name, shape in shapes:
        size = 1
        for s in shape:
            size *= s
        out[name] = flat[off:off + size].reshape(shape)
        off += size
    return out


def _local_2d(arr):
    return arr.reshape(-1, arr.shape[-1])


SHARD_2D = {n: (s[-2] if len(s) > 2 else s[0], s[-1], a == len(s) - 1) for n, s, a in SHARDED}
SHARD_2D["gdn_conv_w"] = (GDN_CONV, GDN_CONV_DIM // N_DEV, True)


def _wire(name):
    rows, cols, _ = SHARD_2D[name]
    return BF16 if rows * cols >= 1 << 16 else F32


def _layer_tensors(i):
    kind, j = i % 3, i // 3
    pre = ("fox", "gla", "gdn")[kind]
    mix = [(pre + "_w_in", j), (pre + "_w_out", j)]
    if kind == 1:
        mix.append(("gla_w_alpha2", j))
    if kind == 2:
        mix.append(("gdn_conv_w", j))
    if i == 0:
        mix.append(("meta_tokens", None))
    return {"mix": mix, "ffn": [("w_gate_up", i), ("w_down", i)]}


PARTS = ("mix", "ffn")


def _layer_block(arr, idx):
    return arr if idx is None else _local_2d(arr[idx])


def _to_slabs(full2d, name):
    rows, cols, by_cols = SHARD_2D[name]
    if by_cols:
        parts = full2d if isinstance(full2d, tuple) else (full2d,)
        per = parts[0].shape[1] // cols
        return jnp.stack([parts[d // per][:, cols * (d % per):cols * (d % per + 1)] for d in range(N_DEV)])
    return full2d.reshape(N_DEV, rows, cols)


def _from_gathered(g, name):
    rows, cols, by_cols = SHARD_2D[name]
    if by_cols:
        return jnp.concatenate([g[d] for d in range(N_DEV)], axis=1)
    return g.reshape(N_DEV * rows, cols)


def _pad_cols(w, total):
    return jnp.pad(w, [(0, 0)] * (w.ndim - 1) + [(0, total - w.shape[-1])])


def _pad_lanes(v):
    return _pad_cols(v, LANE)


def _x(arr, width, cb0=0, moves=0):
    return (arr, width, cb0, moves)


def _rms_fwd(h, g):
    return _row_fwd(_rms_fn, [_x(h, D_MODEL)], [g], [], [(D_MODEL, D_MODEL, BF16)], "rms_fwd")[0]


def _rms_bwd(h, g, dy, add):
    dh, dg = _row_bwd(_rms_fn, [_x(h, D_MODEL)], [g], [], [_x(dy, D_MODEL)], [(D_MODEL, F32)], "rms_bwd", adds=[add])
    return dh, dg


SW_WIDTH = 256
SW_NCOL = D_FF // SW_WIDTH


def _ffn_gate_up(y, wgu):
    m, k = y.shape
    tm = _pick(m, (1408, 384, 128))

    def body(y_ref, wg_ref, wu_ref, a_ref, g_ref, u_ref):
        yv = y_ref[...]
        gt = _dot(yv, wg_ref[...], "nn")
        up = _dot(yv, wu_ref[...], "nn")
        a_ref[...] = (_silu(gt) * up).astype(a_ref.dtype)
        g_ref[...] = gt.astype(g_ref.dtype)
        u_ref[...] = up.astype(u_ref.dtype)

    tile = pl.BlockSpec((tm, SW_WIDTH), lambda i, j: (i, j))
    return pl.pallas_call(
        body, grid=(m // tm, SW_NCOL),
        in_specs=[pl.BlockSpec((tm, k), lambda i, j: (i, 0)),
                  pl.BlockSpec((k, SW_WIDTH), lambda i, j: (0, j)),
                  pl.BlockSpec((k, SW_WIDTH), lambda i, j: (0, SW_NCOL + j))],
        out_specs=[tile] * 3, out_shape=[jax.ShapeDtypeStruct((m, D_FF), BF16)] * 3,
        name="ffn_gate_up", compiler_params=_params(("parallel", "parallel")),
    )(y, wgu, wgu)


def _ffn_down_dx(dh2, wd, gt, up):
    m, k = dh2.shape
    tm = _pick(m, (1408, 384, 128))

    def body(d_ref, w_ref, g_ref, u_ref, dg_ref, du_ref):
        da = _dot(d_ref[...], w_ref[...], "nt")
        gv, uv = g_ref[...].astype(F32), u_ref[...].astype(F32)
        s = jax.nn.sigmoid(gv)
        dg_ref[...] = (da * uv * (s + gv * s * (1.0 - s))).astype(dg_ref.dtype)
        du_ref[...] = (da * gv * s).astype(du_ref.dtype)

    tile = pl.BlockSpec((tm, SW_WIDTH), lambda i, j: (i, j))
    return pl.pallas_call(
        body, grid=(m // tm, SW_NCOL),
        in_specs=[pl.BlockSpec((tm, k), lambda i, j: (i, 0)),
                  pl.BlockSpec((SW_WIDTH, k), lambda i, j: (j, 0)), tile, tile],
        out_specs=[tile] * 2, out_shape=[jax.ShapeDtypeStruct((m, D_FF), BF16)] * 2,
        name="ffn_down_dx", compiler_params=_params(("parallel", "parallel")),
    )(dh2, wd, gt, up)


def _ffn_fwd(h, g, wgu, wd):
    y = _rms_fwd(h, g)
    a, gt, up = _ffn_gate_up(y, wgu)
    h2 = _matmul(a, wd, "nn", F32, "ffn_down", add=h)
    return h2, (h, y, gt, up, a)


def _ffn_bwd(dh2, saved, g, wgu, wd):
    h, y, gt, up, a = saved
    dgt, dup = _ffn_down_dx(dh2, wd, gt, up)
    dwd = _matmul(a, dh2, "tn", F32, "ffn_down_dw")
    dwg = _matmul(y, dgt, "tn", F32, "ffn_gate_dw")
    dwu = _matmul(y, dup, "tn", F32, "ffn_up_dw")
    dy = _matmul(dgt, wgu, "nt", F32, "ffn_gate_dx", b_kblock=0)
    dy = _matmul(dup, wgu, "nt", F32, "ffn_up_dx", add=dy, b_kblock=1)
    dh, dg = _rms_bwd(h, g, dy, dh2)
    return dh, dg, (dwg, dwu), dwd


def _fox_consts():
    lane = jnp.arange(D_MODEL) // FOX_DH
    e_expand = (jnp.arange(LANE)[:, None] == lane[None, :]).astype(F32)
    e_mean = e_expand.T * (1.0 / FOX_DH)
    half = jnp.arange(LANE) // FOX_DH
    blockdiag = (half[:, None] == half[None, :]).astype(F32)
    return e_mean, e_expand, blockdiag


def _pair_layout(c):
    heads = c[:, :FOX_HEADS].T
    crow = heads.reshape(FOX_PAIRS, 2, ROWS)
    c2 = jnp.broadcast_to(crow.transpose(0, 2, 1)[:, :, :, None], (FOX_PAIRS, ROWS, 2, FOX_DH))
    crow = jnp.where(jnp.arange(ROWS) >= FIRST_ROW, crow, KEY_OFF)
    return c2.reshape(FOX_PAIRS, ROWS, LANE), crow


def _fox_fwd(y, p):
    proj = _matmul(y, p["w_in"], "nn", F32, "fox_in")
    e_mean, e_expand, _ = _fox_consts()
    nb = D_MODEL // LANE
    xs = [_x(proj, D_MODEL, 0), _x(proj, D_MODEL, 1), _x(proj, LANE, 4 * nb)]
    ps = [p["q_gain"], p["k_gain"], p["b_f"]]
    qn, kn, lf = _row_fwd(_fox_pre_fn, xs, ps, [e_mean, e_expand],
                          [(D_MODEL, D_MODEL, BF16), (D_MODEL, D_MODEL, BF16), (LANE, LANE, F32)], "fox_pre_fwd")
    c = _seq_cumsum(lf, False, "fox_cumsum")
    c2, crow = _pair_layout(c)
    o, lse2 = _fox_attn_fwd(qn, kn, proj, c2, crow)
    om = _row_fwd(_fox_post_fn, [_x(o, D_MODEL, 0), _x(proj, D_MODEL, 3)], [], [],
                  [(D_MODEL, D_MODEL, BF16)], "fox_post_fwd")[0]
    return om, (proj, qn, kn, c2, crow, o, lse2)


def _fox_bwd(dom, y, saved, p):
    proj, qn, kn, c2, crow, o, lse2 = saved
    e_mean, e_expand, blockdiag = _fox_consts()
    nb = D_MODEL // LANE
    do, dgate = _row_bwd(_fox_post_fn, [_x(o, D_MODEL, 0), _x(proj, D_MODEL, 3)], [], [], [_x(dom, D_MODEL)],
                         [(D_MODEL, F32), (D_MODEL, BF16)], "fox_post_bwd")
    dq, delta2, dcq2 = _fox_attn_dq(qn, kn, proj, c2, crow, lse2, o, do, blockdiag)
    dk, dv, dcrow = _fox_attn_dkv(qn, kn, proj, c2, crow, lse2, delta2, do)
    dlf = _seq_cumsum(_pad_cols(dcrow.reshape(FOX_HEADS, ROWS).T, LANE), True, "fox_cumsum_rev", pairs=dcq2)
    xs = [_x(proj, D_MODEL, 0), _x(proj, D_MODEL, 1), _x(proj, LANE, 4 * nb)]
    ps = [p["q_gain"], p["k_gain"], p["b_f"]]
    dqr, dkr, df, dqg, dkg, dbf = _row_bwd(
        _fox_pre_fn, xs, ps, [e_mean, e_expand], [_x(dq, D_MODEL), _x(dk, D_MODEL), _x(dlf, LANE)],
        [(D_MODEL, BF16), (D_MODEL, BF16), (LANE, BF16)], "fox_pre_bwd")
    dproj = jnp.concatenate([dqr, dkr, dv, dgate, df], axis=1)
    grads = {
        "w_in": _matmul(y, dproj, "tn", F32, "fox_in_dw")[:, :FOX_IN],
        "q_gain": dqg.reshape(FOX_HEADS, FOX_DH).sum(axis=0),
        "k_gain": dkg.reshape(FOX_HEADS, FOX_DH).sum(axis=0),
        "b_f": dbf[0, :FOX_HEADS],
    }
    return _matmul(dproj, p["w_in"], "nt", F32, "fox_in_dx"), grads


def _gla_fwd(y, p):
    proj = _matmul(y, p["w_in"], "nn", F32, "gla_in")
    small = (2 * GLA_QK + 2 * GLA_V) // LANE
    garr = _row_fwd(_gla_pre_fn, [_x(proj, LANE, small)], [p["w_alpha"], p["b_alpha"]], [],
                    [(GLA_QK, GLA_QK, F32)], "gla_pre_fwd")[0]
    o, states = _gla_scan_fwd(proj, garr)
    om = _row_fwd(_gla_post_fn, [_x(o, GLA_V, 0), _x(proj, GLA_V, 2)], [p["o_gain"]], [],
                  [(GLA_V, GLA_V, BF16)], "gla_post_fwd")[0]
    return om, (proj, garr, states, o)


def _gla_bwd(dom, y, saved, p):
    proj, garr, states, o = saved
    small = (2 * GLA_QK + 2 * GLA_V) // LANE
    do, dr, dgain = _row_bwd(_gla_post_fn, [_x(o, GLA_V, 0), _x(proj, GLA_V, 2)], [p["o_gain"]], [],
                             [_x(dom, GLA_V)], [(GLA_V, F32), (GLA_V, BF16)], "gla_post_bwd")
    dq, dk, dv, dg = _gla_scan_bwd(proj, garr, states, do)
    dalr, dwa, dba = _row_bwd(_gla_pre_fn, [_x(proj, LANE, small)], [p["w_alpha"], p["b_alpha"]], [],
                              [_x(dg, GLA_QK)], [(LANE, BF16)], "gla_pre_bwd")
    dproj = jnp.concatenate([dq, dk, dv, dr, dalr], axis=1)
    grads = {
        "w_in": _matmul(y, dproj, "tn", F32, "gla_in_dw")[:, :GLA_IN],
        "w_alpha": dwa[:GLA_RANK], "b_alpha": dba[0], "o_gain": dgain[0],
    }
    return _matmul(dproj, p["w_in"], "nt", F32, "gla_in_dx"), grads


def _gdn_fwd(y, p):
    proj = _matmul(y, p["w_in"], "nn", F32, "gdn_in")
    nb = D_MODEL // LANE
    qn = _gdn_conv_fwd(proj, p["conv_q"], "q", 0)
    kn = _gdn_conv_fwd(proj, p["conv_k"], "k", 1)
    vn = _gdn_conv_fwd(proj, p["conv_v"], "v", 2)
    gates = _row_fwd(_gdn_gates_fn, [_x(proj, LANE, 4 * nb)], [p["a_log"], p["dt_bias"]], [],
                     [(LANE, LANE, F32)], "gdn_gates_fwd")[0]
    o, states = _gdn_scan_fwd(qn, kn, vn, gates)
    om = _row_fwd(_gdn_post_fn, [_x(o, D_MODEL, 0), _x(proj, D_MODEL, 3)], [p["o_gain"]], [],
                  [(D_MODEL, D_MODEL, BF16)], "gdn_post_fwd")[0]
    return om, (proj, qn, kn, vn, gates, states, o)


def _gdn_bwd(dom, y, saved, p):
    proj, qn, kn, vn, gates, states, o = saved
    nb = D_MODEL // LANE
    do, dgate, dgain = _row_bwd(_gdn_post_fn, [_x(o, D_MODEL, 0), _x(proj, D_MODEL, 3)], [p["o_gain"]], [],
                                [_x(dom, D_MODEL)], [(D_MODEL, F32), (D_MODEL, BF16)], "gdn_post_bwd")
    dqn, dkn, dvn, dgates = _gdn_scan_bwd(qn, kn, vn, gates, states, do)
    dsm, dalog, ddtb = _row_bwd(_gdn_gates_fn, [_x(proj, LANE, 4 * nb)], [p["a_log"], p["dt_bias"]], [],
                                [_x(dgates, LANE)], [(LANE, BF16)], "gdn_gates_bwd")
    dxq, dwq = _gdn_conv_bwd(proj, p["conv_q"], "q", 0, dqn)
    dxk, dwk = _gdn_conv_bwd(proj, p["conv_k"], "k", 1, dkn)
    dxv, dwv = _gdn_conv_bwd(proj, p["conv_v"], "v", 2, dvn)
    dproj = jnp.concatenate([dxq, dxk, dxv, dgate, dsm], axis=1)
    grads = {
        "w_in": _matmul(y, dproj, "tn", F32, "gdn_in_dw")[:, :GDN_IN],
        "conv_w": jnp.concatenate([dwq[:GDN_CONV], dwk[:GDN_CONV], dwv[:GDN_CONV]], axis=1),
        "a_log": dalog[0, :GDN_HEADS], "dt_bias": ddtb[0, :GDN_HEADS], "o_gain": dgain[0],
    }
    return _matmul(dproj, p["w_in"], "nt", F32, "gdn_in_dx"), grads


_MIXERS = ((_fox_fwd, _fox_bwd), (_gla_fwd, _gla_bwd), (_gdn_fwd, _gdn_bwd))


def _mixer_params(i, wts, rep):
    kind, j = i % 3, i // 3
    if kind == 0:
        return {
            "w_in": _pad_cols(wts["fox_w_in"], FOX_IN_PAD), "w_out": wts["fox_w_out"],
            "q_gain": jnp.tile(rep["fox_q_gain"][j], FOX_HEADS)[None],
            "k_gain": jnp.tile(rep["fox_k_gain"][j], FOX_HEADS)[None],
            "b_f": _pad_lanes(rep["fox_b_f"][j][None]),
        }
    if kind == 1:
        return {
            "w_in": _pad_cols(wts["gla_w_in"], GLA_IN_PAD), "w_out": wts["gla_w_out"],
            "w_alpha": jnp.pad(wts["gla_w_alpha2"], ((0, LANE - GLA_RANK), (0, 0))).astype(F32),
            "b_alpha": rep["gla_b_alpha"][j][None], "o_gain": rep["gla_o_gain"][j][None],
        }
    conv = jnp.pad(wts["gdn_conv_w"].astype(F32), ((0, 8 - GDN_CONV), (0, 0)))
    return {
        "w_in": _pad_cols(wts["gdn_w_in"], GDN_IN_PAD), "w_out": wts["gdn_w_out"],
        "conv_q": conv[:, :D_MODEL], "conv_k": conv[:, D_MODEL:2 * D_MODEL], "conv_v": conv[:, 2 * D_MODEL:],
        "a_log": _pad_lanes(rep["gdn_a_log"][j][None]), "dt_bias": _pad_lanes(rep["gdn_dt_bias"][j][None]),
        "o_gain": rep["gdn_o_gain"][j][None],
    }


def kernel(x, meta_tokens, norm_mix, norm_ffn, w_gate_up, w_down, fox_w_in, fox_b_f, fox_q_gain, fox_k_gain, fox_w_out, gla_w_in, gla_w_alpha2, gla_b_alpha, gla_o_gain, gla_w_out, gdn_w_in, gdn_conv_w, gdn_a_log, gdn_dt_bias, gdn_o_gain, gdn_w_out, loss_target, m_meta_tokens, m_norm_mix, m_norm_ffn, m_w_gate_up, m_w_down, m_fox_w_in, m_fox_b_f, m_fox_q_gain, m_fox_k_gain, m_fox_w_out, m_gla_w_in, m_gla_w_alpha2, m_gla_b_alpha, m_gla_o_gain, m_gla_w_out, m_gdn_w_in, m_gdn_conv_w, m_gdn_a_log, m_gdn_dt_bias, m_gdn_o_gain, m_gdn_w_out, v_meta_tokens, v_norm_mix, v_norm_ffn, v_w_gate_up, v_w_down, v_fox_w_in, v_fox_b_f, v_fox_q_gain, v_fox_k_gain, v_fox_w_out, v_gla_w_in, v_gla_w_alpha2, v_gla_b_alpha, v_gla_o_gain, v_gla_w_out, v_gdn_w_in, v_gdn_conv_w, v_gdn_a_log, v_gdn_dt_bias, v_gdn_o_gain, v_gdn_w_out):
    given = dict(locals())
    w_loc = {n: given[n] for n in WEIGHT_ORDER}
    m_loc = {n: given["m_" + n] for n in WEIGHT_ORDER}
    v_loc = {n: given["v_" + n] for n in WEIGHT_ORDER}

    rep = {n: w_loc[n] for n, _ in REPLICATED}
    me = 4 * lax.axis_index("x") + 2 * lax.axis_index("y") + lax.axis_index("c")

    def with_own(lands, owns):
        return [lax.dynamic_update_slice(land, own[None], (me, 0, 0)) for land, own in zip(lands, owns)]

    gathers = {}
    order = jnp.zeros((), F32)
    for i in range(DEPTH):
        groups = _layer_tensors(i)
        for part in PARTS:
            blocks = [(_layer_block(w_loc[n], idx) + order).astype(_wire(n)) for n, idx in groups[part]]
            started = _send_start(blocks, True, "gather_start_%d_%s" % (i, part))
            gathers[(i, part)] = (groups[part], blocks, started)
            order = started[4][0, 0]
    all_started = gathers[(DEPTH - 1, PARTS[-1])][2][4]

    def weights(i, part, after):
        tensors, blocks, started = gathers[(i, part)]
        lands = with_own(_send_wait(started, True, after, "gather_wait_%d_%s" % (i, part)), blocks)
        return {n: _from_gathered(g, n) for (n, _), g in zip(tensors, lands)}

    h = None
    saved = []
    for i in range(DEPTH):
        w_mix = weights(i, "mix", all_started if i == 0 else h)
        if i == 0:
            h = jnp.concatenate([jnp.zeros((FIRST_ROW, D_MODEL), F32), w_mix["meta_tokens"], x[0]], axis=0)
        p = _mixer_params(i, w_mix, rep)
        fwd, _ = _MIXERS[i % 3]
        g_mix = rep["norm_mix"][i][None]
        y = _rms_fwd(h, g_mix)
        om, mix_saved = fwd(y, p)
        h_mid = _matmul(om, p["w_out"], "nn", F32, "mixer_out", add=h)
        w_ffn = weights(i, "ffn", h_mid)
        g_ffn = rep["norm_ffn"][i][None]
        h_out, ffn_saved = _ffn_fwd(h_mid, g_ffn, w_ffn["w_gate_up"], w_ffn["w_down"])
        saved.append((p, w_ffn, h, y, om, mix_saved, ffn_saved))
        h = h_out

    dh, sq = _loss_head(h, loss_target[0])
    loss = lax.psum(0.5 * jnp.sum(sq) * (1.0 / D_MODEL), ("x", "y", "c"))

    gw = {n: [None] * DEPTH for n in ("norm_mix", "norm_ffn")}
    gm = {}
    reduces = {}

    def start_reduce(i, part, full, extra=()):
        tensors = _layer_tensors(i)[part]
        slabs = [_to_slabs(full[n], n).astype(_wire(n)) for n, _ in tensors] + list(extra)
        started = _send_start(slabs, False, "reduce_start_%d_%s" % (i, part))
        reduces[(i, part)] = (tensors, slabs, started)
        return started[4][0, 0]

    for i in reversed(range(DEPTH)):
        p, w_ffn, h_in, y, om, mix_saved, ffn_saved = saved[i]
        kind, j = i % 3, i // 3
        pre = ("fox", "gla", "gdn")[kind]
        _, bwd = _MIXERS[kind]
        dh_mid, gw["norm_ffn"][i], dwgu, dwd = _ffn_bwd(
            dh, ffn_saved, rep["norm_ffn"][i][None], w_ffn["w_gate_up"], w_ffn["w_down"])
        dh_mid = dh_mid + start_reduce(i, "ffn", {"w_gate_up": dwgu, "w_down": dwd})
        dom = _matmul(dh_mid, p["w_out"], "nt", F32, "mixer_out_dx")
        dwo = _matmul(om, dh_mid, "tn", F32, "mixer_out_dw")
        dy, grads = bwd(dom, y, mix_saved, p)
        gm[(kind, j)] = grads
        dh, gw["norm_mix"][i] = _rms_bwd(h_in, rep["norm_mix"][i][None], dy, dh_mid)
        mix_full = {pre + "_w_in": grads["w_in"], pre + "_w_out": dwo, "gla_w_alpha2": grads.get("w_alpha"),
                    "gdn_conv_w": grads.get("conv_w"), "meta_tokens": dh[FIRST_ROW:PAD_ROWS]}
        if i > 0:
            dh = dh + start_reduce(i, "mix", mix_full)

    partial = {
        "norm_mix": jnp.concatenate(gw["norm_mix"], axis=0), "norm_ffn": jnp.concatenate(gw["norm_ffn"], axis=0),
        "fox_b_f": jnp.stack([gm[(0, 0)]["b_f"], gm[(0, 1)]["b_f"]]),
        "fox_q_gain": jnp.stack([gm[(0, 0)]["q_gain"], gm[(0, 1)]["q_gain"]]),
        "fox_k_gain": jnp.stack([gm[(0, 0)]["k_gain"], gm[(0, 1)]["k_gain"]]),
        "gla_b_alpha": gm[(1, 0)]["b_alpha"][None], "gla_o_gain": gm[(1, 0)]["o_gain"][None],
        "gdn_a_log": gm[(2, 0)]["a_log"][None], "gdn_dt_bias": gm[(2, 0)]["dt_bias"][None],
        "gdn_o_gain": gm[(2, 0)]["o_gain"][None],
    }

    small = _pack([partial[n].reshape(-1) for n, _ in REPLICATED], 8, F32)
    start_reduce(0, "mix", mix_full, extra=[jnp.broadcast_to(small[None], (N_DEV,) + small.shape)])
    recv = {}
    for (i, part), (tensors, slabs, started) in reduces.items():
        owns = [lax.dynamic_index_in_dim(s, me, 0, keepdims=False) for s in slabs]
        lands = with_own(_send_wait(started, False, dh, "reduce_wait_%d_%s" % (i, part)), owns)
        for (n, idx), land in zip(tensors, lands):
            recv[(n, idx)] = land
        if (i, part) == (0, "mix"):
            recv_small = lands[-1]

    res = [{} for _ in range(4)]
    for n, s, _ in SHARDED:
        layers = [idx for i in range(DEPTH) for part in PARTS for (nn, idx) in _layer_tensors(i)[part] if nn == n]
        layers = sorted(layers, key=lambda idx: -1 if idx is None else idx)
        outs = _adamw([recv[(n, idx)] for idx in layers], *[_local_2d(d[n]) for d in (w_loc, m_loc, v_loc)],
                      "adamw_" + n)
        for k in range(4):
            res[k][n] = outs[k].reshape(s)
    outs_small = _adamw([recv_small], *[_pack([d[n].reshape(-1) for n, _ in REPLICATED], 8, F32)
                                        for d in (w_loc, m_loc, v_loc)], "adamw_replicated")
    for k in range(4):
        res[k].update(_unpack(outs_small[k], list(REPLICATED)))

    result = [loss, dh[PAD_ROWS:][None]]
    for k in range(4):
        result += [res[k][n] for n in WEIGHT_ORDER]
    return tuple(result)
```

```python
import functools

import jax
import jax.numpy as jnp
from jax import lax
from jax.experimental import pallas as pl
from jax.experimental.pallas import tpu as pltpu

F32, BF16 = jnp.float32, jnp.bfloat16
HIGHEST = lax.Precision.HIGHEST

D_MODEL = 1024
SEQ = 4096
N_META = 16
DEPTH = 4
NORM_EPS = 1e-6
LANE = 128
PAD_ROWS = 128
FIRST_ROW = PAD_ROWS - N_META
ROWS = PAD_ROWS + SEQ
ROW_TILE = 128
CHUNK = 64
NEG = -1e30
N_DEV = 8
VMEM_LIMIT = 56 * 1024 * 1024
VMEM_BLOCK_BUDGET = 36 * 1024 * 1024

FOX_HEADS, FOX_DH = 16, 64
FOX_IN, FOX_IN_PAD = 4 * D_MODEL + FOX_HEADS, 4 * D_MODEL + LANE
GLA_HEADS, GLA_DK, GLA_DV, GLA_RANK = 4, 128, 256, 16
GLA_QK, GLA_V = GLA_HEADS * GLA_DK, GLA_HEADS * GLA_DV
GLA_IN, GLA_IN_PAD = 2 * GLA_QK + 2 * GLA_V + GLA_RANK, 2 * GLA_QK + 2 * GLA_V + LANE
GDN_HEADS, GDN_DK, GDN_DV, GDN_CONV = 8, 128, 128, 4
GDN_CONV_DIM = 3 * GDN_HEADS * GDN_DK
GDN_IN, GDN_IN_PAD = GDN_CONV_DIM + GDN_HEADS * GDN_DV + 2 * GDN_HEADS, GDN_CONV_DIM + GDN_HEADS * GDN_DV + LANE
D_FF = 2816

ADAM_LR, ADAM_B1, ADAM_B2, ADAM_EPS, ADAM_WD, ADAM_STEP = 0.001, 0.9, 0.999, 1e-08, 0.01, 10

PACK_COLS = 1024
PACK_ROW_TILE = 256

SHARDED = (
    ("meta_tokens", (16, 128), 1),
    ("w_gate_up", (4, 1024, 704), 2),
    ("w_down", (4, 352, 1024), 1),
    ("fox_w_in", (2, 1024, 514), 2),
    ("fox_w_out", (2, 128, 1024), 1),
    ("gla_w_in", (1, 1024, 386), 2),
    ("gla_w_alpha2", (1, 16, 64), 2),
    ("gla_w_out", (1, 128, 1024), 1),
    ("gdn_w_in", (1, 1024, 514), 2),
    ("gdn_conv_w", (1, 4, 1, 384), 3),
    ("gdn_w_out", (1, 128, 1024), 1),
)
REPLICATED = (
    ("norm_mix", (4, 1024)),
    ("norm_ffn", (4, 1024)),
    ("fox_b_f", (2, 16)),
    ("fox_q_gain", (2, 64)),
    ("fox_k_gain", (2, 64)),
    ("gla_b_alpha", (1, 512)),
    ("gla_o_gain", (1, 256)),
    ("gdn_a_log", (1, 8)),
    ("gdn_dt_bias", (1, 8)),
    ("gdn_o_gain", (1, 128)),
)
WEIGHT_ORDER = (
    "meta_tokens", "norm_mix", "norm_ffn", "w_gate_up", "w_down", "fox_w_in", "fox_b_f", "fox_q_gain",
    "fox_k_gain", "fox_w_out", "gla_w_in", "gla_w_alpha2", "gla_b_alpha", "gla_o_gain", "gla_w_out",
    "gdn_w_in", "gdn_conv_w", "gdn_a_log", "gdn_dt_bias", "gdn_o_gain", "gdn_w_out",
)

_DN = {
    "nn": (((1,), (0,)), ((), ())),
    "nt": (((1,), (1,)), ((), ())),
    "tn": (((0,), (0,)), ((), ())),
}


def _dot(a, b, kind):
    return lax.dot_general(a.astype(BF16), b.astype(BF16), _DN[kind], preferred_element_type=F32)


@functools.partial(jax.custom_vjp, nondiff_argnums=(2,))
def _dot_ad(a, b, kind):
    return _dot(a, b, kind)


def _dot_ad_fwd(a, b, kind):
    return _dot(a, b, kind), (a, b)


def _dot_ad_bwd(kind, res, ct):
    a, b = res
    if kind == "nn":
        return _dot(ct, b, "nt"), _dot(a, ct, "tn")
    if kind == "nt":
        return _dot(ct, b, "nn"), _dot(ct, a, "tn")
    return _dot(b, ct, "nt"), _dot(a, ct, "nn")


_dot_ad.defvjp(_dot_ad_fwd, _dot_ad_bwd)


def _dot_hi(a, b, kind):
    return lax.dot_general(a, b, _DN[kind], precision=HIGHEST, preferred_element_type=F32)


def _dot_mid(a, b, kind):
    return lax.dot_general(a, b, _DN[kind], precision=lax.Precision.HIGH, preferred_element_type=F32)


def _iota(shape, dim):
    return lax.broadcasted_iota(jnp.int32, shape, dim)


def _row_mask(i, tm):
    return ((i * tm + _iota((tm, 1), 0)) >= FIRST_ROW).astype(F32)


def _log_sigmoid(z):
    return jnp.minimum(z, 0.0) - jnp.log(1.0 + jnp.exp(-jnp.abs(z)))


def _softplus(z):
    return jnp.maximum(z, 0.0) + jnp.log(1.0 + jnp.exp(-jnp.abs(z)))


def _silu(z):
    return z * jax.nn.sigmoid(z)


def _lane_col(x, lane):
    return jnp.sum(jnp.where(_iota(x.shape, 1) == lane, x, 0.0), axis=1, keepdims=True)


def _last_row(x):
    return jnp.sum(jnp.where(_iota(x.shape, 0) == x.shape[0] - 1, x, 0.0), axis=0, keepdims=True)


def _params(sem, limit=VMEM_LIMIT):
    return pltpu.CompilerParams(dimension_semantics=sem, vmem_limit_bytes=limit)


def _pick(n, options):
    for t in options:
        if n % t == 0:
            return t
    return n


def _matmul(a, b, kind, out_dtype, name, add=None, b_kblock=0):
    if kind == "nn":
        (m, k), n = a.shape, b.shape[1]
    elif kind == "nt":
        (m, k), n = a.shape, b.shape[0]
    else:
        (k, m), n = a.shape, b.shape[1]
    tn = _pick(n, (512, 384, 640, 256, 128))
    tm_options = (1408, 384, 128) if kind != "tn" else (256, 128)
    tm = None
    for t in tm_options:
        if m % t:
            continue
        need = 2 * (t * k * a.dtype.itemsize + tn * k * b.dtype.itemsize + t * tn * 8)
        if need <= VMEM_BLOCK_BUDGET:
            tm = t
            break
    assert tm is not None, (name, a.shape, b.shape)

    def body(*refs):
        a_ref, b_ref = refs[0], refs[1]
        o_ref = refs[-1]
        acc = _dot(a_ref[...], b_ref[...], kind)
        if add is not None:
            acc = acc + refs[2][...]
        o_ref[...] = acc.astype(o_ref.dtype)

    if kind == "tn":
        a_spec = pl.BlockSpec((k, tm), lambda i, j: (0, i))
    else:
        a_spec = pl.BlockSpec((tm, k), lambda i, j: (i, 0))
    if kind == "nt":
        b_spec = pl.BlockSpec((tn, k), lambda i, j: (j, b_kblock))
    else:
        b_spec = pl.BlockSpec((k, tn), lambda i, j: (0, j))
    o_spec = pl.BlockSpec((tm, tn), lambda i, j: (i, j))
    operands, in_specs = [a, b], [a_spec, b_spec]
    if add is not None:
        operands.append(add)
        in_specs.append(o_spec)
    return pl.pallas_call(
        body, grid=(m // tm, n // tn), in_specs=in_specs, out_specs=o_spec,
        out_shape=jax.ShapeDtypeStruct((m, n), out_dtype), name=name,
        compiler_params=_params(("parallel", "parallel")),
    )(*operands)


def _x_spec(tm, width, cb0, moves):
    return pl.BlockSpec((tm, width), lambda i, j: (i, cb0 + j * moves))


def _full_spec(arr):
    nd = arr.ndim
    return pl.BlockSpec(arr.shape, lambda i, j: (0,) * nd)


def _row_fwd(fn, xs, ps, cs, outs, name, tm=ROW_TILE, ncol=1, rows=None):
    rows = ROWS if rows is None else rows
    nx, npar, nc = len(xs), len(ps), len(cs)

    def body(*refs):
        i = pl.program_id(0)
        xv = [r[...].astype(F32) for r in refs[:nx]]
        pv = [r[...] for r in refs[nx:nx + npar + nc]]
        res = fn(i, *xv, *pv)
        for r, val in zip(refs[nx + npar + nc:], res):
            r[...] = val.astype(r.dtype)

    in_specs = [_x_spec(tm, w, cb0, mv) for (_, w, cb0, mv) in xs]
    in_specs += [_full_spec(p) for p in list(ps) + list(cs)]
    out_specs = [_x_spec(tm, w, 0, 1) for (w, _, _) in outs]
    out_shape = [jax.ShapeDtypeStruct((rows, tot), dt) for (_, tot, dt) in outs]
    return pl.pallas_call(
        body, grid=(rows // tm, ncol), in_specs=in_specs, out_specs=out_specs, out_shape=out_shape,
        name=name, compiler_params=_params(("parallel", "parallel")),
    )(*[x[0] for x in xs], *ps, *cs)


def _row_bwd(fn, xs, ps, cs, cts, dxs, name, adds=None, tm=ROW_TILE, ncol=1, rows=None):
    rows = ROWS if rows is None else rows
    nx, npar, nc, nct = len(xs), len(ps), len(cs), len(cts)
    adds = [None] * nx if adds is None else adds
    add_idx = [k for k in range(nx) if adds[k] is not None]

    def body(*refs):
        i, j = pl.program_id(0), pl.program_id(1)
        pos = 0
        x_refs = refs[pos:pos + nx]; pos += nx
        p_refs = refs[pos:pos + npar]; pos += npar
        c_refs = refs[pos:pos + nc]; pos += nc
        ct_refs = refs[pos:pos + nct]; pos += nct
        add_refs = refs[pos:pos + len(add_idx)]; pos += len(add_idx)
        dx_refs = refs[pos:pos + nx]; pos += nx
        dp_refs = refs[pos:pos + npar]
        xv = [r[...].astype(F32) for r in x_refs]
        pv = [r[...] for r in p_refs]
        cv = [r[...] for r in c_refs]
        _, vjp = jax.vjp(lambda *args: fn(i, *args, *cv), *xv, *pv)
        grads = vjp(tuple(r[...].astype(F32) for r in ct_refs))
        for k in range(nx):
            val = grads[k]
            if adds[k] is not None:
                val = val + add_refs[add_idx.index(k)][...]
            dx_refs[k][...] = val.astype(dx_refs[k].dtype)
        if npar:
            @pl.when((i == 0) & (j == 0))
            def _():
                for r in dp_refs:
                    r[...] = jnp.zeros(r.shape, r.dtype)

            for k in range(npar):
                dp_refs[k][...] += grads[nx + k]

    in_specs = [_x_spec(tm, w, cb0, mv) for (_, w, cb0, mv) in xs]
    in_specs += [_full_spec(p) for p in list(ps) + list(cs)]
    in_specs += [_x_spec(tm, w, cb0, mv) for (_, w, cb0, mv) in cts]
    in_specs += [_x_spec(tm, xs[k][1], 0, 1) for k in add_idx]
    out_specs = [_x_spec(tm, xs[k][1], 0, 1) for k in range(nx)] + [_full_spec(p) for p in ps]
    out_shape = [jax.ShapeDtypeStruct((rows, tot), dt) for (tot, dt) in dxs]
    out_shape += [jax.ShapeDtypeStruct(p.shape, F32) for p in ps]
    sem = ("arbitrary", "arbitrary") if npar else ("parallel", "parallel")
    return pl.pallas_call(
        body, grid=(rows // tm, ncol), in_specs=in_specs, out_specs=out_specs, out_shape=out_shape,
        name=name, compiler_params=_params(sem),
    )(*[x[0] for x in xs], *ps, *cs, *[c[0] for c in cts], *[adds[k] for k in add_idx])


def _rms_fn(i, h, g):
    return (h * lax.rsqrt(jnp.mean(h * h, axis=1, keepdims=True) + NORM_EPS) * g,)


def _fox_pre_fn(i, q, k, f, qg, kg, bf, e_mean, e_expand):
    tm = q.shape[0]

    def headnorm(x, gain, scale):
        msq = _dot_mid(x * x, e_mean, "nn")
        r = _dot_mid(lax.rsqrt(msq + NORM_EPS), e_expand, "nn")
        return x * r * gain * scale

    qn = headnorm(q, qg, FOX_DH ** -0.5)
    kn = headnorm(k, kg, 1.0)
    keep = (_iota((tm, LANE), 1) < FOX_HEADS).astype(F32) * _row_mask(i, tm)
    return qn, kn, _log_sigmoid(f + bf) * keep


def _fox_post_fn(i, o, gate):
    return (o * jax.nn.sigmoid(gate) * _row_mask(i, o.shape[0]),)


def _gla_pre_fn(i, alr, wa, ba):
    z = _dot_ad(alr, wa, "nn") + ba
    return (_log_sigmoid(z) * (1.0 / 16.0) * _row_mask(i, alr.shape[0]),)


def _headwise_norm_gate(o, gate, gain, dv, mask):
    pieces = []
    for h in range(o.shape[1] // dv):
        oh = o[:, h * dv:(h + 1) * dv]
        pieces.append(oh * lax.rsqrt(jnp.mean(oh * oh, axis=1, keepdims=True) + NORM_EPS) * gain)
    return jnp.concatenate(pieces, axis=1) * _silu(gate) * mask


def _gla_post_fn(i, o, r, gain):
    return (_headwise_norm_gate(o, r, gain, GLA_DV, _row_mask(i, o.shape[0])),)


def _gdn_post_fn(i, o, gate, gain):
    return (_headwise_norm_gate(o, gate, gain, GDN_DV, _row_mask(i, o.shape[0])),)


def _gdn_gates_fn(i, sm, alog, dtb):
    tm = sm.shape[0]
    lane = _iota((tm, LANE), 1)
    g = -jnp.exp(alog) * _softplus(sm + dtb)
    beta = jax.nn.sigmoid(sm)
    out = jnp.where(lane < GDN_HEADS, g, jnp.where(lane < 2 * GDN_HEADS, beta, 0.0))
    return (out * _row_mask(i, tm),)


def _gdn_conv_fn(kind):
    def fn(i, prev, cur, w):
        tm = cur.shape[0]
        xw = jnp.concatenate([prev * jnp.where(i > 0, 1.0, 0.0), cur], axis=0)
        t = _iota((tm, 2 * tm), 0)
        s = _iota((tm, 2 * tm), 1)
        y = jnp.zeros(cur.shape, F32)
        for j in range(GDN_CONV):
            sel = (s == t + (tm - (GDN_CONV - 1) + j)).astype(F32)
            y = y + _dot_mid(sel, xw, "nn") * jnp.sum(jnp.where(_iota(w.shape, 0) == j, w, 0.0), axis=0, keepdims=True)
        a = _silu(y * _row_mask(i, tm))
        if kind == "v":
            return (a,)
        scale = GDN_DK ** -0.5 if kind == "q" else 1.0
        pieces = []
        for h in range(GDN_HEADS):
            ah = a[:, h * GDN_DK:(h + 1) * GDN_DK]
            pieces.append(ah * lax.rsqrt(jnp.sum(ah * ah, axis=1, keepdims=True) + NORM_EPS) * scale)
        return (jnp.concatenate(pieces, axis=1),)

    return fn


def _seq_cumsum(x, reverse, name, pairs=None):
    rows, width = x.shape
    nt = rows // ROW_TILE

    def body(*refs):
        x_ref, o_ref, carry = refs[0], refs[-2], refs[-1]

        @pl.when(pl.program_id(0) == 0)
        def _():
            carry[...] = jnp.zeros(carry.shape, F32)

        xv = x_ref[...]
        if pairs is not None:
            lane = _iota(xv.shape, 1)
            for g in range(pairs.shape[0]):
                ta, tb = _pair_cols(refs[1][g])
                xv = xv + jnp.where(lane == 2 * g, ta, 0.0) + jnp.where(lane == 2 * g + 1, tb, 0.0)
        r = _iota((ROW_TILE, ROW_TILE), 0)
        c = _iota((ROW_TILE, ROW_TILE), 1)
        tri = ((c >= r) if reverse else (c <= r)).astype(F32)
        acc = _dot_hi(tri, xv, "nn") + carry[...]
        o_ref[...] = acc
        edge = 0 if reverse else ROW_TILE - 1
        carry[...] = jnp.sum(jnp.where(_iota(acc.shape, 0) == edge, acc, 0.0), axis=0, keepdims=True)

    idx = (lambda i: (nt - 1 - i, 0)) if reverse else (lambda i: (i, 0))
    idx3 = (lambda i: (0, nt - 1 - i, 0)) if reverse else (lambda i: (0, i, 0))
    operands, in_specs = [x], [pl.BlockSpec((ROW_TILE, width), idx)]
    if pairs is not None:
        operands.append(pairs)
        in_specs.append(pl.BlockSpec((pairs.shape[0], ROW_TILE, LANE), idx3))
    return pl.pallas_call(
        body, grid=(nt,), in_specs=in_specs,
        out_specs=pl.BlockSpec((ROW_TILE, width), idx), out_shape=jax.ShapeDtypeStruct(x.shape, F32),
        scratch_shapes=[pltpu.VMEM((1, width), F32)], name=name, compiler_params=_params(("arbitrary",)),
    )(*operands)


FOX_PAIRS = FOX_HEADS // 2
FOX_V_BLOCK = 2 * D_MODEL // LANE


def _pair_cols(x):
    return _lane_col(x, 0), _lane_col(x, FOX_DH)


ATT_TILE = 384
assert ROWS % ATT_TILE == 0
N_ATT = ROWS // ATT_TILE
KEY_OFF = 1e30


def _fox_scores(qa, qb, kj, ca, cb, cra, crb, mask):
    sa = _dot(qa, kj, "nt") + (ca - cra)
    sb = _dot(qb, kj, "nt") + (cb - crb)
    if mask is None:
        return sa, sb
    return jnp.where(mask, sa, NEG), jnp.where(mask, sb, NEG)


def _fox_diag_mask():
    return _iota((ATT_TILE, ATT_TILE), 1) <= _iota((ATT_TILE, ATT_TILE), 0)


def _att_off(j):
    return pl.multiple_of(j * ATT_TILE, ATT_TILE)


def _fox_attn_fwd(qn, kn, proj, c2, crow):
    nt = ROWS // ROW_TILE

    def body(q_ref, k_ref, v_ref, c2_ref, crow_ref, o_ref, lse_ref):
        i = pl.program_id(1)
        q = q_ref[...]
        is_a = _iota((ATT_TILE, LANE), 1) < FOX_DH
        qa, qb = jnp.where(is_a, q, 0), jnp.where(is_a, 0, q)
        ca, cb = _pair_cols(c2_ref[0])

        def block(off, size, mask, carry):
            ma, la, acca, mb, lb, accb = carry
            kj = k_ref[pl.ds(off, size), :]
            vj = v_ref[pl.ds(off, size), :].astype(BF16)
            sa, sb = _fox_scores(qa, qb, kj, ca, cb, crow_ref[0, 0:1, pl.ds(off, size)],
                                 crow_ref[0, 1:2, pl.ds(off, size)], mask)

            def online(s, m, l, acc):
                m_new = jnp.maximum(m, jnp.max(s, axis=1, keepdims=True))
                p = jnp.exp(s - m_new)
                alpha = jnp.exp(m - m_new)
                return m_new, alpha * l + jnp.sum(p, axis=1, keepdims=True), alpha * acc + _dot(p, vj, "nn")

            return online(sa, ma, la, acca) + online(sb, mb, lb, accb)

        m0 = jnp.full((ATT_TILE, 1), NEG, F32)
        l0 = jnp.zeros((ATT_TILE, 1), F32)
        a0 = jnp.zeros((ATT_TILE, LANE), F32)
        carry = lax.fori_loop(0, i, lambda j, cr: block(_att_off(j), ATT_TILE, None, cr), (m0, l0, a0, m0, l0, a0))
        ma, la, acca, mb, lb, accb = block(_att_off(i), ATT_TILE, _fox_diag_mask(), carry)
        o_ref[...] = jnp.where(is_a, acca / la, accb / lb)
        lse_ref[0] = jnp.where(is_a, ma + jnp.log(la), mb + jnp.log(lb))

    return pl.pallas_call(
        body, grid=(FOX_PAIRS, N_ATT),
        in_specs=[
            pl.BlockSpec((ATT_TILE, LANE), lambda g, i: (i, g)),
            pl.BlockSpec((ROWS, LANE), lambda g, i: (0, g)),
            pl.BlockSpec((ROWS, LANE), lambda g, i: (0, FOX_V_BLOCK + g)),
            pl.BlockSpec((1, ATT_TILE, LANE), lambda g, i: (g, i, 0)),
            pl.BlockSpec((1, 2, ROWS), lambda g, i: (g, 0, 0)),
        ],
        out_specs=[
            pl.BlockSpec((ATT_TILE, LANE), lambda g, i: (i, g)),
            pl.BlockSpec((1, ATT_TILE, LANE), lambda g, i: (g, i, 0)),
        ],
        out_shape=[jax.ShapeDtypeStruct((ROWS, D_MODEL), F32), jax.ShapeDtypeStruct((FOX_PAIRS, ROWS, LANE), F32)],
        name="fox_attn_fwd", compiler_params=_params(("parallel", "arbitrary")),
    )(qn, kn, proj, c2, crow)


def _fox_attn_dq(qn, kn, proj, c2, crow, lse2, o, do, blockdiag):
    nt = ROWS // ROW_TILE

    def body(q_ref, k_ref, v_ref, c2_ref, crow_ref, lse_ref, o_ref, do_ref, bd_ref, dq_ref, delta_ref, dcq_ref):
        i = pl.program_id(1)
        q = q_ref[...]
        is_a = _iota((ATT_TILE, LANE), 1) < FOX_DH
        qa, qb = jnp.where(is_a, q, 0), jnp.where(is_a, 0, q)
        ca, cb = _pair_cols(c2_ref[0])
        lsa, lsb = _pair_cols(lse_ref[0])
        do_t = do_ref[...]
        delta2 = _dot_hi(do_t * o_ref[...], bd_ref[...], "nn")
        delta_ref[0] = delta2
        dla, dlb = _pair_cols(delta2)
        doa, dob = jnp.where(is_a, do_t, 0.0).astype(BF16), jnp.where(is_a, 0.0, do_t).astype(BF16)

        def block(off, size, mask, carry):
            dqa, dqb, rsa, rsb = carry
            kj = k_ref[pl.ds(off, size), :]
            vj = v_ref[pl.ds(off, size), :].astype(BF16)
            sa, sb = _fox_scores(qa, qb, kj, ca, cb, crow_ref[0, 0:1, pl.ds(off, size)],
                                 crow_ref[0, 1:2, pl.ds(off, size)], mask)
            dsa = jnp.exp(sa - lsa) * (_dot(doa, vj, "nt") - dla)
            dsb = jnp.exp(sb - lsb) * (_dot(dob, vj, "nt") - dlb)
            return (dqa + _dot(dsa, kj, "nn"), dqb + _dot(dsb, kj, "nn"),
                    rsa + jnp.sum(dsa, axis=1, keepdims=True), rsb + jnp.sum(dsb, axis=1, keepdims=True))

        z = jnp.zeros((ATT_TILE, LANE), F32)
        zc = jnp.zeros((ATT_TILE, 1), F32)
        carry = lax.fori_loop(0, i, lambda j, cr: block(_att_off(j), ATT_TILE, None, cr), (z, z, zc, zc))
        dqa, dqb, rsa, rsb = block(_att_off(i), ATT_TILE, _fox_diag_mask(), carry)
        dq_ref[...] = jnp.where(is_a, dqa, dqb)
        dcq_ref[0] = jnp.where(is_a, rsa, rsb)

    tile = pl.BlockSpec((ATT_TILE, LANE), lambda g, i: (i, g))
    pair_tile = pl.BlockSpec((1, ATT_TILE, LANE), lambda g, i: (g, i, 0))
    return pl.pallas_call(
        body, grid=(FOX_PAIRS, N_ATT),
        in_specs=[
            tile,
            pl.BlockSpec((ROWS, LANE), lambda g, i: (0, g)),
            pl.BlockSpec((ROWS, LANE), lambda g, i: (0, FOX_V_BLOCK + g)),
            pair_tile,
            pl.BlockSpec((1, 2, ROWS), lambda g, i: (g, 0, 0)),
            pair_tile, tile, tile,
            pl.BlockSpec((LANE, LANE), lambda g, i: (0, 0)),
        ],
        out_specs=[tile, pair_tile, pair_tile],
        out_shape=[jax.ShapeDtypeStruct((ROWS, D_MODEL), F32), jax.ShapeDtypeStruct((FOX_PAIRS, ROWS, LANE), F32),
                   jax.ShapeDtypeStruct((FOX_PAIRS, ROWS, LANE), F32)],
        name="fox_attn_dq", compiler_params=_params(("parallel", "arbitrary")),
    )(qn, kn, proj, c2, crow, lse2, o, do, blockdiag)


def _fox_attn_dkv(qn, kn, proj, c2, crow, lse2, delta2, do):
    nt = ROWS // ROW_TILE

    def body(q_ref, k_ref, v_ref, c2_ref, crow_ref, lse_ref, delta_ref, do_ref, dk_ref, dv_ref, dc_ref):
        j = pl.program_id(1)
        kj = k_ref[...]
        vj = v_ref[...].astype(BF16)
        cra, crb = crow_ref[0, 0:1, :], crow_ref[0, 1:2, :]

        def block(off, size, mask, carry):
            dk, dv, dca, dcb = carry
            is_a = _iota((size, LANE), 1) < FOX_DH
            q = q_ref[pl.ds(off, size), :]
            qa, qb = jnp.where(is_a, q, 0), jnp.where(is_a, 0, q)
            do_t = do_ref[pl.ds(off, size), :]
            doa, dob = jnp.where(is_a, do_t, 0.0).astype(BF16), jnp.where(is_a, 0.0, do_t).astype(BF16)
            ca, cb = _pair_cols(c2_ref[0, pl.ds(off, size), :])
            lsa, lsb = _pair_cols(lse_ref[0, pl.ds(off, size), :])
            dla, dlb = _pair_cols(delta_ref[0, pl.ds(off, size), :])
            sa, sb = _fox_scores(qa, qb, kj, ca, cb, cra, crb, mask)
            pa, pb = jnp.exp(sa - lsa), jnp.exp(sb - lsb)
            dsa = pa * (_dot(doa, vj, "nt") - dla)
            dsb = pb * (_dot(dob, vj, "nt") - dlb)
            dv = dv + _dot(pa, doa, "tn") + _dot(pb, dob, "tn")
            dk = dk + _dot(dsa, qa, "tn") + _dot(dsb, qb, "tn")
            dca = dca - jnp.sum(dsa, axis=0, keepdims=True)
            dcb = dcb - jnp.sum(dsb, axis=0, keepdims=True)
            return dk, dv, dca, dcb

        z = jnp.zeros((ATT_TILE, LANE), F32)
        zr = jnp.zeros((1, ATT_TILE), F32)
        carry = block(_att_off(j), ATT_TILE, _fox_diag_mask(), (z, z, zr, zr))
        dk, dv, dca, dcb = lax.fori_loop(j + 1, N_ATT, lambda i, cr: block(_att_off(i), ATT_TILE, None, cr), carry)
        dk_ref[...] = dk
        dv_ref[...] = dv.astype(dv_ref.dtype)
        dc_ref[0, 0:1, :] = dca
        dc_ref[0, 1:2, :] = dcb

    whole = pl.BlockSpec((ROWS, LANE), lambda g, j: (0, g))
    pair_whole = pl.BlockSpec((1, ROWS, LANE), lambda g, j: (g, 0, 0))
    tile = pl.BlockSpec((ATT_TILE, LANE), lambda g, j: (j, g))
    return pl.pallas_call(
        body, grid=(FOX_PAIRS, N_ATT),
        in_specs=[
            whole, tile,
            pl.BlockSpec((ATT_TILE, LANE), lambda g, j: (j, FOX_V_BLOCK + g)),
            pair_whole,
            pl.BlockSpec((1, 2, ATT_TILE), lambda g, j: (g, 0, j)),
            pair_whole, pair_whole, whole,
        ],
        out_specs=[tile, tile, pl.BlockSpec((1, 2, ATT_TILE), lambda g, j: (g, 0, j))],
        out_shape=[jax.ShapeDtypeStruct((ROWS, D_MODEL), F32), jax.ShapeDtypeStruct((ROWS, D_MODEL), BF16),
                   jax.ShapeDtypeStruct((FOX_PAIRS, 2, ROWS), F32)],
        name="fox_attn_dkv", compiler_params=_params(("parallel", "arbitrary")),
    )(qn, kn, proj, c2, crow, lse2, delta2, do)


def _gla_chunk(q, k, v, g, st, dot):
    n = len(q)
    c = q[0].shape[0]
    incl = _iota((c, c), 1) <= _iota((c, c), 0)
    incl_f = incl.astype(F32)
    b = [_dot_hi(incl_f, g[i], "nn") for i in range(n)]
    qe = [q[i] * (GLA_DK ** -0.5) * jnp.exp(b[i]) for i in range(n)]
    ke = [k[i] * jnp.exp(-b[i]) for i in range(n)]
    att = [jnp.where(incl, dot(qe[i], ke[i], "nt"), 0.0) for i in range(n)]
    o_intra = [dot(att[i], v[i], "nn") for i in range(n)]
    o_inter = [dot(qe[i], st[i], "nt") for i in range(n)]
    b_last = [_last_row(b[i]) for i in range(n)]
    kd = [k[i] * jnp.exp(b_last[i] - b[i]) for i in range(n)]
    upd = [dot(v[i], kd[i], "tn") for i in range(n)]
    return ([o_intra[i] + o_inter[i] for i in range(n)],
            [st[i] * jnp.exp(b_last[i]) + upd[i] for i in range(n)])


def _tri_inverse(low):
    c = low[0].shape[0]
    eye = (_iota((c, c), 0) == _iota((c, c), 1)).astype(F32)
    x, p = [eye - l for l in low], list(low)
    span = 2
    while span < c:
        p = [_dot_mid(pi, pi, "nn") for pi in p]
        x = [_dot_mid(xi, eye + pi, "nn") for xi, pi in zip(x, p)]
        span *= 2
    return x


def _solve(low, rhs):
    return [_dot_mid(inv, r, "nn") for inv, r in zip(_tri_inverse(low), rhs)]


@jax.custom_vjp
def _solve_ad(low, rhs):
    return _solve(low, rhs)


def _solve_ad_fwd(low, rhs):
    inv = _tri_inverse(low)
    sol = [_dot_mid(i, r, "nn") for i, r in zip(inv, rhs)]
    return sol, (inv, sol)


def _solve_ad_bwd(res, ct):
    inv, sol = res
    drhs = [_dot_mid(i, c, "tn") for i, c in zip(inv, ct)]
    return [-_dot_mid(d, s, "nt") for d, s in zip(drhs, sol)], drhs


_solve_ad.defvjp(_solve_ad_fwd, _solve_ad_bwd)


def _gdn_chunk(q, k, v, gates, st, heads, dot, solve):
    n = len(q)
    c = q[0].shape[0]
    incl = _iota((c, c), 1) <= _iota((c, c), 0)
    strict = _iota((c, c), 1) < _iota((c, c), 0)
    incl_f = incl.astype(F32)
    upper_f = (_iota((c, c), 0) <= _iota((c, c), 1)).astype(F32)
    gcol = [_lane_col(gates, h) for h in heads]
    beta = [_lane_col(gates, h + GDN_HEADS) for h in heads]
    b = [_dot_hi(incl_f, jnp.broadcast_to(gc, (c, LANE)), "nn") for gc in gcol]
    g_sq = [jnp.broadcast_to(gc, (c, c)) for gc in gcol]
    b_col = [_dot_hi(incl_f, gs, "nn") for gs in g_sq]
    b_row = [_dot_hi(gs, upper_f, "tn") for gs in g_sq]
    diff = [b_col[i] - b_row[i] for i in range(n)]
    kb = [k[i] * beta[i] for i in range(n)]
    vb = [v[i] * beta[i] for i in range(n)]
    low = [dot(kb[i], k[i], "nt") * jnp.exp(jnp.where(strict, diff[i], NEG)) for i in range(n)]
    eb = [jnp.exp(bi) for bi in b]
    sol = solve(low, [jnp.concatenate([vb[i], kb[i] * eb[i]], axis=1) for i in range(n)])
    w_st = [dot(sol[i][:, GDN_DV:], st[i], "nn") for i in range(n)]
    v_new = [sol[i][:, :GDN_DV] - w_st[i] for i in range(n)]
    att = [dot(q[i], k[i], "nt") * jnp.exp(jnp.where(incl, diff[i], NEG)) for i in range(n)]
    o_intra = [dot(att[i], v_new[i], "nn") for i in range(n)]
    o_inter = [dot(q[i] * eb[i], st[i], "nn") for i in range(n)]
    b_last = [_last_row(bi) for bi in b]
    kd = [k[i] * jnp.exp(b_last[i] - b[i]) for i in range(n)]
    upd = [dot(kd[i], v_new[i], "tn") for i in range(n)]
    return ([o_intra[i] + o_inter[i] for i in range(n)],
            [st[i] * jnp.exp(b_last[i]) + upd[i] for i in range(n)])


N_CHUNKS = ROWS // CHUNK
GDN_GROUP = 8


def _head_lanes(width, n):
    return [slice(i * width, (i + 1) * width) for i in range(n)]


def _gla_scan_fwd(proj, garr):
    kl, vl = _head_lanes(GLA_DK, GLA_HEADS), _head_lanes(GLA_DV, GLA_HEADS)

    def body(q_ref, k_ref, v_ref, g_ref, o_ref, states_ref, st):
        @pl.when(pl.program_id(0) == 0)
        def _():
            st[...] = jnp.zeros(st.shape, F32)

        s_in = [st[i] for i in range(GLA_HEADS)]
        o, s_out = _gla_chunk([q_ref[:, ln] for ln in kl], [k_ref[:, ln] for ln in kl], [v_ref[:, ln] for ln in vl],
                              [g_ref[:, ln] for ln in kl], s_in, _dot)
        for i in range(GLA_HEADS):
            states_ref[i, 0] = s_in[i]
            o_ref[:, vl[i]] = o[i]
            st[i] = s_out[i]

    return pl.pallas_call(
        body, grid=(N_CHUNKS,),
        in_specs=[
            pl.BlockSpec((CHUNK, GLA_QK), lambda c: (c, 0)),
            pl.BlockSpec((CHUNK, GLA_QK), lambda c: (c, 1)),
            pl.BlockSpec((CHUNK, GLA_V), lambda c: (c, 1)),
            pl.BlockSpec((CHUNK, GLA_QK), lambda c: (c, 0)),
        ],
        out_specs=[
            pl.BlockSpec((CHUNK, GLA_V), lambda c: (c, 0)),
            pl.BlockSpec((GLA_HEADS, 1, GLA_DV, GLA_DK), lambda c: (0, c, 0, 0)),
        ],
        out_shape=[jax.ShapeDtypeStruct((ROWS, GLA_V), F32),
                   jax.ShapeDtypeStruct((GLA_HEADS, N_CHUNKS, GLA_DV, GLA_DK), F32)],
        scratch_shapes=[pltpu.VMEM((GLA_HEADS, GLA_DV, GLA_DK), F32)],
        name="gla_scan_fwd", compiler_params=_params(("arbitrary",)),
    )(proj, proj, proj, garr)


def _gla_scan_bwd(proj, garr, states, do):
    last = N_CHUNKS - 1
    kl, vl = _head_lanes(GLA_DK, GLA_HEADS), _head_lanes(GLA_DV, GLA_HEADS)

    def body(q_ref, k_ref, v_ref, g_ref, s_ref, do_ref, dq_ref, dk_ref, dv_ref, dg_ref, dst):
        @pl.when(pl.program_id(0) == 0)
        def _():
            dst[...] = jnp.zeros(dst.shape, F32)

        _, vjp = jax.vjp(lambda q, k, v, g, s: _gla_chunk(q, k, v, g, s, _dot_ad),
                         [q_ref[:, ln] for ln in kl], [k_ref[:, ln] for ln in kl], [v_ref[:, ln] for ln in vl],
                         [g_ref[:, ln] for ln in kl], [s_ref[i, 0] for i in range(GLA_HEADS)])
        dq, dk, dv, dg, ds = vjp(([do_ref[:, ln] for ln in vl], [dst[i] for i in range(GLA_HEADS)]))
        for i in range(GLA_HEADS):
            dq_ref[:, kl[i]] = dq[i].astype(dq_ref.dtype)
            dk_ref[:, kl[i]] = dk[i].astype(dk_ref.dtype)
            dv_ref[:, vl[i]] = dv[i].astype(dv_ref.dtype)
            dg_ref[:, kl[i]] = dg[i]
            dst[i] = ds[i]

    kspec = pl.BlockSpec((CHUNK, GLA_QK), lambda c: (last - c, 0))
    vspec = pl.BlockSpec((CHUNK, GLA_V), lambda c: (last - c, 0))
    return pl.pallas_call(
        body, grid=(N_CHUNKS,),
        in_specs=[
            kspec,
            pl.BlockSpec((CHUNK, GLA_QK), lambda c: (last - c, 1)),
            pl.BlockSpec((CHUNK, GLA_V), lambda c: (last - c, 1)),
            kspec,
            pl.BlockSpec((GLA_HEADS, 1, GLA_DV, GLA_DK), lambda c: (0, last - c, 0, 0)),
            vspec,
        ],
        out_specs=[kspec, kspec, vspec, kspec],
        out_shape=[jax.ShapeDtypeStruct((ROWS, GLA_QK), BF16), jax.ShapeDtypeStruct((ROWS, GLA_QK), BF16),
                   jax.ShapeDtypeStruct((ROWS, GLA_V), BF16), jax.ShapeDtypeStruct((ROWS, GLA_QK), F32)],
        scratch_shapes=[pltpu.VMEM((GLA_HEADS, GLA_DV, GLA_DK), F32)],
        name="gla_scan_bwd", compiler_params=_params(("arbitrary",)),
    )(proj, proj, proj, garr, states, do)


def _gdn_scan_fwd(qn, kn, vn, gates):
    def body(q_ref, k_ref, v_ref, g_ref, o_ref, states_ref, st):
        @pl.when((pl.program_id(0) == 0) & (pl.program_id(1) == 0))
        def _():
            st[...] = jnp.zeros(st.shape, F32)

        heads = [pl.program_id(1) * GDN_GROUP + hh for hh in range(GDN_GROUP)]
        lanes = _head_lanes(GDN_DK, GDN_GROUP)
        s_in = [st[h] for h in heads]
        o, s_out = _gdn_chunk([q_ref[:, ln] for ln in lanes], [k_ref[:, ln] for ln in lanes],
                              [v_ref[:, ln] for ln in lanes], g_ref[...], s_in, heads, _dot, _solve)
        for hh in range(GDN_GROUP):
            states_ref[hh, 0] = s_in[hh]
            o_ref[:, lanes[hh]] = o[hh]
            st[heads[hh]] = s_out[hh]

    hspec = pl.BlockSpec((CHUNK, GDN_GROUP * GDN_DK), lambda c, h: (c, h))
    return pl.pallas_call(
        body, grid=(N_CHUNKS, GDN_HEADS // GDN_GROUP),
        in_specs=[hspec, hspec, hspec, pl.BlockSpec((CHUNK, LANE), lambda c, h: (c, 0))],
        out_specs=[hspec, pl.BlockSpec((GDN_GROUP, 1, GDN_DK, GDN_DV), lambda c, h: (h, c, 0, 0))],
        out_shape=[jax.ShapeDtypeStruct((ROWS, D_MODEL), F32),
                   jax.ShapeDtypeStruct((GDN_HEADS, N_CHUNKS, GDN_DK, GDN_DV), F32)],
        scratch_shapes=[pltpu.VMEM((GDN_HEADS, GDN_DK, GDN_DV), F32)],
        name="gdn_scan_fwd", compiler_params=_params(("arbitrary", "arbitrary")),
    )(qn, kn, vn, gates)


def _gdn_scan_bwd(qn, kn, vn, gates, states, do):
    last = N_CHUNKS - 1

    def body(q_ref, k_ref, v_ref, g_ref, s_ref, do_ref, dq_ref, dk_ref, dv_ref, dg_ref, dst):
        @pl.when((pl.program_id(0) == 0) & (pl.program_id(1) == 0))
        def _():
            dst[...] = jnp.zeros(dst.shape, F32)

        @pl.when(pl.program_id(1) == 0)
        def _():
            dg_ref[...] = jnp.zeros(dg_ref.shape, F32)

        heads = [pl.program_id(1) * GDN_GROUP + hh for hh in range(GDN_GROUP)]
        lanes = _head_lanes(GDN_DK, GDN_GROUP)
        _, vjp = jax.vjp(lambda q, k, v, g, s: _gdn_chunk(q, k, v, g, s, heads, _dot_ad, _solve_ad),
                         [q_ref[:, ln] for ln in lanes], [k_ref[:, ln] for ln in lanes],
                         [v_ref[:, ln] for ln in lanes], g_ref[...], [s_ref[hh, 0] for hh in range(GDN_GROUP)])
        dq, dk, dv, dg, ds = vjp(([do_ref[:, ln] for ln in lanes], [dst[h] for h in heads]))
        for hh in range(GDN_GROUP):
            dq_ref[:, lanes[hh]] = dq[hh]
            dk_ref[:, lanes[hh]] = dk[hh]
            dv_ref[:, lanes[hh]] = dv[hh]
            dst[heads[hh]] = ds[hh]
        dg_ref[...] += dg

    hspec = pl.BlockSpec((CHUNK, GDN_GROUP * GDN_DK), lambda c, h: (last - c, h))
    gspec = pl.BlockSpec((CHUNK, LANE), lambda c, h: (last - c, 0))
    return pl.pallas_call(
        body, grid=(N_CHUNKS, GDN_HEADS // GDN_GROUP),
        in_specs=[hspec, hspec, hspec, gspec,
                  pl.BlockSpec((GDN_GROUP, 1, GDN_DK, GDN_DV), lambda c, h: (h, last - c, 0, 0)), hspec],
        out_specs=[hspec, hspec, hspec, gspec],
        out_shape=[jax.ShapeDtypeStruct((ROWS, D_MODEL), F32)] * 3 + [jax.ShapeDtypeStruct((ROWS, LANE), F32)],
        scratch_shapes=[pltpu.VMEM((GDN_HEADS, GDN_DK, GDN_DV), F32)],
        name="gdn_scan_bwd", compiler_params=_params(("arbitrary", "arbitrary")),
    )(qn, kn, vn, gates, states, do)


def _gdn_conv_fwd(proj, w, kind, cb):
    fn = _gdn_conv_fn(kind)
    nt = ROWS // ROW_TILE

    def body(prev_ref, cur_ref, w_ref, o_ref):
        (out,) = fn(pl.program_id(0), prev_ref[...], cur_ref[...], w_ref[...])
        o_ref[...] = out

    return pl.pallas_call(
        body, grid=(nt,),
        in_specs=[pl.BlockSpec((ROW_TILE, D_MODEL), lambda i: (jnp.maximum(i - 1, 0), cb)),
                  pl.BlockSpec((ROW_TILE, D_MODEL), lambda i: (i, cb)),
                  pl.BlockSpec(w.shape, lambda i: (0, 0))],
        out_specs=pl.BlockSpec((ROW_TILE, D_MODEL), lambda i: (i, 0)),
        out_shape=jax.ShapeDtypeStruct((ROWS, D_MODEL), F32),
        name="gdn_conv_fwd_" + kind, compiler_params=_params(("parallel",)),
    )(proj, proj, w)


def _gdn_conv_bwd(proj, w, kind, cb, ct):
    fn = _gdn_conv_fn(kind)
    nt = ROWS // ROW_TILE

    def body(prev_ref, cur_ref, w_ref, ct_ref, dx_ref, dw_ref, carry):
        step = pl.program_id(0)
        i = nt - 1 - step

        @pl.when(step == 0)
        def _():
            carry[...] = jnp.zeros(carry.shape, F32)
            dw_ref[...] = jnp.zeros(dw_ref.shape, F32)

        _, vjp = jax.vjp(lambda p, c, ww: fn(i, p, c, ww), prev_ref[...], cur_ref[...], w_ref[...])
        dprev, dcur, dw = vjp((ct_ref[...],))
        dx_ref[...] = (dcur + carry[...]).astype(dx_ref.dtype)
        carry[...] = dprev
        dw_ref[...] += dw

    return pl.pallas_call(
        body, grid=(nt,),
        in_specs=[pl.BlockSpec((ROW_TILE, D_MODEL), lambda s: (jnp.maximum(nt - 2 - s, 0), cb)),
                  pl.BlockSpec((ROW_TILE, D_MODEL), lambda s: (nt - 1 - s, cb)),
                  pl.BlockSpec(w.shape, lambda s: (0, 0)),
                  pl.BlockSpec((ROW_TILE, D_MODEL), lambda s: (nt - 1 - s, 0))],
        out_specs=[pl.BlockSpec((ROW_TILE, D_MODEL), lambda s: (nt - 1 - s, 0)),
                   pl.BlockSpec(w.shape, lambda s: (0, 0))],
        out_shape=[jax.ShapeDtypeStruct((ROWS, D_MODEL), BF16), jax.ShapeDtypeStruct(w.shape, F32)],
        scratch_shapes=[pltpu.VMEM((ROW_TILE, D_MODEL), F32)],
        name="gdn_conv_bwd_" + kind, compiler_params=_params(("arbitrary",)),
    )(proj, proj, w, ct)


def _loss_head(h, target):
    nt = ROWS // ROW_TILE
    lead = PAD_ROWS // ROW_TILE

    def body(h_ref, t_ref, dh_ref, acc_ref):
        i = pl.program_id(0)

        @pl.when(i == 0)
        def _():
            acc_ref[...] = jnp.zeros(acc_ref.shape, F32)

        err = (h_ref[...] - t_ref[...]) * jnp.where(i >= lead, 1.0, 0.0)
        dh_ref[...] = err * (1.0 / D_MODEL)
        acc_ref[...] += jnp.sum(err * err, axis=0, keepdims=True)

    return pl.pallas_call(
        body, grid=(nt,),
        in_specs=[pl.BlockSpec((ROW_TILE, D_MODEL), lambda i: (i, 0)),
                  pl.BlockSpec((ROW_TILE, D_MODEL), lambda i: (jnp.maximum(i - lead, 0), 0))],
        out_specs=[pl.BlockSpec((ROW_TILE, D_MODEL), lambda i: (i, 0)),
                   pl.BlockSpec((1, D_MODEL), lambda i: (0, 0))],
        out_shape=[jax.ShapeDtypeStruct((ROWS, D_MODEL), F32), jax.ShapeDtypeStruct((1, D_MODEL), F32)],
        name="loss_head", compiler_params=_params(("arbitrary",)),
    )(h, target)


MESH = pl.DeviceIdType.MESH
ANY = pl.BlockSpec(memory_space=pl.ANY)


N_PEERS = N_DEV - 1


HBM_SPEC = pl.BlockSpec(memory_space=pltpu.HBM)
SEM_SPEC = pl.BlockSpec(memory_space=pltpu.SEMAPHORE)
SIDE_EFFECT = pltpu.SideEffectType.DATAFLOW_SIDE_EFFECTING


def _me_and_peers():
    x, y, c = lax.axis_index("x"), lax.axis_index("y"), lax.axis_index("c")
    peers = []
    for k in range(1, N_DEV):
        pid = (1 - x if k & 4 else x, 1 - y if k & 2 else y, 1 - c if k & 1 else c)
        peers.append((4 * pid[0] + 2 * pid[1] + pid[2], pid))
    return 4 * x + 2 * y + c, peers


def _split_copies(src_refs, land_refs, send_sems, recv_sems, spread):
    me, peers = _me_and_peers()
    out = []
    for t, (src, land) in enumerate(zip(src_refs, land_refs)):
        for k, (peer, pid) in enumerate(peers):
            sems = dict(send_sem=send_sems.at[N_PEERS * t + k], recv_sem=recv_sems.at[N_PEERS * t + k],
                        device_id=pid, device_id_type=MESH)
            piece = src if spread else src.at[peer]
            out.append((pltpu.make_async_remote_copy(src_ref=piece, dst_ref=land.at[me], **sems),
                        pltpu.make_async_remote_copy(src_ref=piece, dst_ref=land.at[peer], **sems)))
    return out


def _send_start(srcs, spread, name):
    n = len(srcs)
    lands = [lax.empty((N_DEV,) + (s.shape if spread else s.shape[1:]), s.dtype) for s in srcs]

    def body(*refs):
        for out, _ in _split_copies(refs[:n], refs[n:2 * n], refs[2 * n], refs[2 * n + 1], spread):
            out.start()
        refs[-1][...] = jnp.zeros(refs[-1].shape, F32)

    thru = [pltpu.HBM(a.shape, a.dtype) for a in list(srcs) + lands]
    res = pl.pallas_call(
        body, name=name,
        out_shape=(pltpu.SemaphoreType.DMA((N_PEERS * n,)), pltpu.SemaphoreType.DMA((N_PEERS * n,)), *thru,
                   jax.ShapeDtypeStruct((8, LANE), F32)),
        in_specs=[HBM_SPEC] * (2 * n),
        out_specs=(SEM_SPEC, SEM_SPEC, *[HBM_SPEC] * (2 * n), pl.BlockSpec(memory_space=pltpu.VMEM)),
        input_output_aliases={t: 2 + t for t in range(2 * n)},
        compiler_params=pltpu.CompilerParams(has_side_effects=SIDE_EFFECT),
    )(*[pltpu.with_memory_space_constraint(a, pltpu.HBM) for a in list(srcs) + lands])
    return res[0], res[1], list(res[2:2 + n]), list(res[2 + n:2 + 2 * n]), res[-1]


def _send_wait(started, spread, after, name):
    send_sems, recv_sems, srcs, lands, _ = started
    n = len(srcs)

    def body(*refs):
        for out, arrival in _split_copies(refs[:n], refs[n:2 * n], refs[2 * n], refs[2 * n + 1], spread):
            out.wait_send()
            arrival.wait_recv()

    thru = [pltpu.HBM(a.shape, a.dtype) for a in srcs + lands]
    res = pl.pallas_call(
        body, name=name, out_shape=tuple(thru),
        in_specs=[HBM_SPEC] * (2 * n) + [SEM_SPEC, SEM_SPEC, ANY], out_specs=tuple([HBM_SPEC] * (2 * n)),
        input_output_aliases={t: t for t in range(2 * n)},
        compiler_params=pltpu.CompilerParams(has_side_effects=SIDE_EFFECT),
    )(*srcs, *lands, send_sems, recv_sems, after)
    return list(res[n:])


def _adamw(recvs, w, m, v, name):
    n_layers = len(recvs)
    rows, cols = recvs[0].shape[1:]
    tr = _pick(rows, (256, 128, 64, 32, 16))
    nb = rows // tr

    def body(*refs):
        r_refs = refs[:n_layers]
        w_ref, m_ref, v_ref, g_ref, d_ref, m2_ref, v2_ref = refs[n_layers:]

        def update(r_ref):
            g = r_ref[0].astype(F32)
            for k in range(1, N_DEV):
                g = g + r_ref[k].astype(F32)
            m2 = ADAM_B1 * m_ref[...] + (1.0 - ADAM_B1) * g
            v2 = ADAM_B2 * v_ref[...] + (1.0 - ADAM_B2) * (g * g)
            m_hat = m2 / (1.0 - ADAM_B1 ** ADAM_STEP)
            v_hat = v2 / (1.0 - ADAM_B2 ** ADAM_STEP)
            g_ref[...] = g
            d_ref[...] = -ADAM_LR * (m_hat / (jnp.sqrt(v_hat) + ADAM_EPS) + ADAM_WD * w_ref[...])
            m2_ref[...] = m2
            v2_ref[...] = v2

        if n_layers == 1:
            update(r_refs[0])
        else:
            for layer in range(n_layers):
                pl.when(pl.program_id(0) == layer)(functools.partial(update, r_refs[layer]))

    def recv_spec(layer):
        return pl.BlockSpec((N_DEV, tr, cols), lambda a, i: (0, jnp.where(a == layer, i, 0), 0))

    spec = pl.BlockSpec((tr, cols), lambda a, i: (a * nb + i, 0))
    return pl.pallas_call(
        body, grid=(n_layers, nb),
        in_specs=[recv_spec(layer) for layer in range(n_layers)] + [spec, spec, spec],
        out_specs=[spec] * 4, out_shape=[jax.ShapeDtypeStruct(w.shape, F32)] * 4,
        name=name, compiler_params=_params(("arbitrary", "arbitrary")),
    )(*recvs, w, m, v)


def _padded_rows(n, mult):
    rows = -(-n // PACK_COLS)
    return -(-rows // mult) * mult


def _pack(flat_pieces, mult, dtype, lead=()):
    flat = jnp.concatenate([p.astype(dtype) for p in flat_pieces], axis=-1)
    n = flat.shape[-1]
    rows = _padded_rows(n, mult)
    flat = jnp.pad(flat, [(0, 0)] * len(lead) + [(0, rows * PACK_COLS - n)])
    return flat.reshape(lead + (rows, PACK_COLS))


def _unpack(packed, shapes):
    flat = packed.reshape(-1)
    out, off = {}, 0
    for name, shape in shapes:
        size = 1
        for s in shape:
            size *= s
        out[name] = flat[off:off + size].reshape(shape)
        off += size
    return out


def _local_2d(arr):
    return arr.reshape(-1, arr.shape[-1])


SHARD_2D = {n: (s[-2] if len(s) > 2 else s[0], s[-1], a == len(s) - 1) for n, s, a in SHARDED}
SHARD_2D["gdn_conv_w"] = (GDN_CONV, GDN_CONV_DIM // N_DEV, True)


def _wire(name):
    rows, cols, _ = SHARD_2D[name]
    return BF16 if rows * cols >= 1 << 16 else F32


def _layer_tensors(i):
    kind, j = i % 3, i // 3
    pre = ("fox", "gla", "gdn")[kind]
    mix = [(pre + "_w_in", j), (pre + "_w_out", j)]
    if kind == 1:
        mix.append(("gla_w_alpha2", j))
    if kind == 2:
        mix.append(("gdn_conv_w", j))
    if i == 0:
        mix.append(("meta_tokens", None))
    return {"mix": mix, "ffn": [("w_gate_up", i), ("w_down", i)]}


PARTS = ("mix", "ffn")


def _layer_block(arr, idx):
    return arr if idx is None else _local_2d(arr[idx])


def _to_slabs(full2d, name):
    rows, cols, by_cols = SHARD_2D[name]
    if by_cols:
        parts = full2d if isinstance(full2d, tuple) else (full2d,)
        per = parts[0].shape[1] // cols
        return jnp.stack([parts[d // per][:, cols * (d % per):cols * (d % per + 1)] for d in range(N_DEV)])
    return full2d.reshape(N_DEV, rows, cols)


def _from_gathered(g, name):
    rows, cols, by_cols = SHARD_2D[name]
    if by_cols:
        return jnp.concatenate([g[d] for d in range(N_DEV)], axis=1)
    return g.reshape(N_DEV * rows, cols)


def _pad_cols(w, total):
    return jnp.pad(w, [(0, 0)] * (w.ndim - 1) + [(0, total - w.shape[-1])])


def _pad_lanes(v):
    return _pad_cols(v, LANE)


def _x(arr, width, cb0=0, moves=0):
    return (arr, width, cb0, moves)


def _rms_fwd(h, g):
    return _row_fwd(_rms_fn, [_x(h, D_MODEL)], [g], [], [(D_MODEL, D_MODEL, BF16)], "rms_fwd")[0]


def _rms_bwd(h, g, dy, add):
    dh, dg = _row_bwd(_rms_fn, [_x(h, D_MODEL)], [g], [], [_x(dy, D_MODEL)], [(D_MODEL, F32)], "rms_bwd", adds=[add])
    return dh, dg


SW_WIDTH = 256
SW_NCOL = D_FF // SW_WIDTH


def _ffn_gate_up(y, wgu):
    m, k = y.shape
    tm = _pick(m, (1408, 384, 128))

    def body(y_ref, wg_ref, wu_ref, a_ref, g_ref, u_ref):
        yv = y_ref[...]
        gt = _dot(yv, wg_ref[...], "nn")
        up = _dot(yv, wu_ref[...], "nn")
        a_ref[...] = (_silu(gt) * up).astype(a_ref.dtype)
        g_ref[...] = gt.astype(g_ref.dtype)
        u_ref[...] = up.astype(u_ref.dtype)

    tile = pl.BlockSpec((tm, SW_WIDTH), lambda i, j: (i, j))
    return pl.pallas_call(
        body, grid=(m // tm, SW_NCOL),
        in_specs=[pl.BlockSpec((tm, k), lambda i, j: (i, 0)),
                  pl.BlockSpec((k, SW_WIDTH), lambda i, j: (0, j)),
                  pl.BlockSpec((k, SW_WIDTH), lambda i, j: (0, SW_NCOL + j))],
        out_specs=[tile] * 3, out_shape=[jax.ShapeDtypeStruct((m, D_FF), BF16)] * 3,
        name="ffn_gate_up", compiler_params=_params(("parallel", "parallel")),
    )(y, wgu, wgu)


def _ffn_down_dx(dh2, wd, gt, up):
    m, k = dh2.shape
    tm = _pick(m, (1408, 384, 128))

    def body(d_ref, w_ref, g_ref, u_ref, dg_ref, du_ref):
        da = _dot(d_ref[...], w_ref[...], "nt")
        gv, uv = g_ref[...].astype(F32), u_ref[...].astype(F32)
        s = jax.nn.sigmoid(gv)
        dg_ref[...] = (da * uv * (s + gv * s * (1.0 - s))).astype(dg_ref.dtype)
        du_ref[...] = (da * gv * s).astype(du_ref.dtype)

    tile = pl.BlockSpec((tm, SW_WIDTH), lambda i, j: (i, j))
    return pl.pallas_call(
        body, grid=(m // tm, SW_NCOL),
        in_specs=[pl.BlockSpec((tm, k), lambda i, j: (i, 0)),
                  pl.BlockSpec((SW_WIDTH, k), lambda i, j: (j, 0)), tile, tile],
        out_specs=[tile] * 2, out_shape=[jax.ShapeDtypeStruct((m, D_FF), BF16)] * 2,
        name="ffn_down_dx", compiler_params=_params(("parallel", "parallel")),
    )(dh2, wd, gt, up)


def _ffn_fwd(h, g, wgu, wd):
    y = _rms_fwd(h, g)
    a, gt, up = _ffn_gate_up(y, wgu)
    h2 = _matmul(a, wd, "nn", F32, "ffn_down", add=h)
    return h2, (h, y, gt, up, a)


def _ffn_bwd(dh2, saved, g, wgu, wd):
    h, y, gt, up, a = saved
    dgt, dup = _ffn_down_dx(dh2, wd, gt, up)
    dwd = _matmul(a, dh2, "tn", F32, "ffn_down_dw")
    dwg = _matmul(y, dgt, "tn", F32, "ffn_gate_dw")
    dwu = _matmul(y, dup, "tn", F32, "ffn_up_dw")
    dy = _matmul(dgt, wgu, "nt", F32, "ffn_gate_dx", b_kblock=0)
    dy = _matmul(dup, wgu, "nt", F32, "ffn_up_dx", add=dy, b_kblock=1)
    dh, dg = _rms_bwd(h, g, dy, dh2)
    return dh, dg, (dwg, dwu), dwd


def _fox_consts():
    lane = jnp.arange(D_MODEL) // FOX_DH
    e_expand = (jnp.arange(LANE)[:, None] == lane[None, :]).astype(F32)
    e_mean = e_expand.T * (1.0 / FOX_DH)
    half = jnp.arange(LANE) // FOX_DH
    blockdiag = (half[:, None] == half[None, :]).astype(F32)
    return e_mean, e_expand, blockdiag


def _pair_layout(c):
    heads = c[:, :FOX_HEADS].T
    crow = heads.reshape(FOX_PAIRS, 2, ROWS)
    c2 = jnp.broadcast_to(crow.transpose(0, 2, 1)[:, :, :, None], (FOX_PAIRS, ROWS, 2, FOX_DH))
    crow = jnp.where(jnp.arange(ROWS) >= FIRST_ROW, crow, KEY_OFF)
    return c2.reshape(FOX_PAIRS, ROWS, LANE), crow


def _fox_fwd(y, p):
    proj = _matmul(y, p["w_in"], "nn", F32, "fox_in")
    e_mean, e_expand, _ = _fox_consts()
    nb = D_MODEL // LANE
    xs = [_x(proj, D_MODEL, 0), _x(proj, D_MODEL, 1), _x(proj, LANE, 4 * nb)]
    ps = [p["q_gain"], p["k_gain"], p["b_f"]]
    qn, kn, lf = _row_fwd(_fox_pre_fn, xs, ps, [e_mean, e_expand],
                          [(D_MODEL, D_MODEL, BF16), (D_MODEL, D_MODEL, BF16), (LANE, LANE, F32)], "fox_pre_fwd")
    c = _seq_cumsum(lf, False, "fox_cumsum")
    c2, crow = _pair_layout(c)
    o, lse2 = _fox_attn_fwd(qn, kn, proj, c2, crow)
    om = _row_fwd(_fox_post_fn, [_x(o, D_MODEL, 0), _x(proj, D_MODEL, 3)], [], [],
                  [(D_MODEL, D_MODEL, BF16)], "fox_post_fwd")[0]
    return om, (proj, qn, kn, c2, crow, o, lse2)


def _fox_bwd(dom, y, saved, p):
    proj, qn, kn, c2, crow, o, lse2 = saved
    e_mean, e_expand, blockdiag = _fox_consts()
    nb = D_MODEL // LANE
    do, dgate = _row_bwd(_fox_post_fn, [_x(o, D_MODEL, 0), _x(proj, D_MODEL, 3)], [], [], [_x(dom, D_MODEL)],
                         [(D_MODEL, F32), (D_MODEL, BF16)], "fox_post_bwd")
    dq, delta2, dcq2 = _fox_attn_dq(qn, kn, proj, c2, crow, lse2, o, do, blockdiag)
    dk, dv, dcrow = _fox_attn_dkv(qn, kn, proj, c2, crow, lse2, delta2, do)
    dlf = _seq_cumsum(_pad_cols(dcrow.reshape(FOX_HEADS, ROWS).T, LANE), True, "fox_cumsum_rev", pairs=dcq2)
    xs = [_x(proj, D_MODEL, 0), _x(proj, D_MODEL, 1), _x(proj, LANE, 4 * nb)]
    ps = [p["q_gain"], p["k_gain"], p["b_f"]]
    dqr, dkr, df, dqg, dkg, dbf = _row_bwd(
        _fox_pre_fn, xs, ps, [e_mean, e_expand], [_x(dq, D_MODEL), _x(dk, D_MODEL), _x(dlf, LANE)],
        [(D_MODEL, BF16), (D_MODEL, BF16), (LANE, BF16)], "fox_pre_bwd")
    dproj = jnp.concatenate([dqr, dkr, dv, dgate, df], axis=1)
    grads = {
        "w_in": _matmul(y, dproj, "tn", F32, "fox_in_dw")[:, :FOX_IN],
        "q_gain": dqg.reshape(FOX_HEADS, FOX_DH).sum(axis=0),
        "k_gain": dkg.reshape(FOX_HEADS, FOX_DH).sum(axis=0),
        "b_f": dbf[0, :FOX_HEADS],
    }
    return _matmul(dproj, p["w_in"], "nt", F32, "fox_in_dx"), grads


def _gla_fwd(y, p):
    proj = _matmul(y, p["w_in"], "nn", F32, "gla_in")
    small = (2 * GLA_QK + 2 * GLA_V) // LANE
    garr = _row_fwd(_gla_pre_fn, [_x(proj, LANE, small)], [p["w_alpha"], p["b_alpha"]], [],
                    [(GLA_QK, GLA_QK, F32)], "gla_pre_fwd")[0]
    o, states = _gla_scan_fwd(proj, garr)
    om = _row_fwd(_gla_post_fn, [_x(o, GLA_V, 0), _x(proj, GLA_V, 2)], [p["o_gain"]], [],
                  [(GLA_V, GLA_V, BF16)], "gla_post_fwd")[0]
    return om, (proj, garr, states, o)


def _gla_bwd(dom, y, saved, p):
    proj, garr, states, o = saved
    small = (2 * GLA_QK + 2 * GLA_V) // LANE
    do, dr, dgain = _row_bwd(_gla_post_fn, [_x(o, GLA_V, 0), _x(proj, GLA_V, 2)], [p["o_gain"]], [],
                             [_x(dom, GLA_V)], [(GLA_V, F32), (GLA_V, BF16)], "gla_post_bwd")
    dq, dk, dv, dg = _gla_scan_bwd(proj, garr, states, do)
    dalr, dwa, dba = _row_bwd(_gla_pre_fn, [_x(proj, LANE, small)], [p["w_alpha"], p["b_alpha"]], [],
                              [_x(dg, GLA_QK)], [(LANE, BF16)], "gla_pre_bwd")
    dproj = jnp.concatenate([dq, dk, dv, dr, dalr], axis=1)
    grads = {
        "w_in": _matmul(y, dproj, "tn", F32, "gla_in_dw")[:, :GLA_IN],
        "w_alpha": dwa[:GLA_RANK], "b_alpha": dba[0], "o_gain": dgain[0],
    }
    return _matmul(dproj, p["w_in"], "nt", F32, "gla_in_dx"), grads


def _gdn_fwd(y, p):
    proj = _matmul(y, p["w_in"], "nn", F32, "gdn_in")
    nb = D_MODEL // LANE
    qn = _gdn_conv_fwd(proj, p["conv_q"], "q", 0)
    kn = _gdn_conv_fwd(proj, p["conv_k"], "k", 1)
    vn = _gdn_conv_fwd(proj, p["conv_v"], "v", 2)
    gates = _row_fwd(_gdn_gates_fn, [_x(proj, LANE, 4 * nb)], [p["a_log"], p["dt_bias"]], [],
                     [(LANE, LANE, F32)], "gdn_gates_fwd")[0]
    o, states = _gdn_scan_fwd(qn, kn, vn, gates)
    om = _row_fwd(_gdn_post_fn, [_x(o, D_MODEL, 0), _x(proj, D_MODEL, 3)], [p["o_gain"]], [],
                  [(D_MODEL, D_MODEL, BF16)], "gdn_post_fwd")[0]
    return om, (proj, qn, kn, vn, gates, states, o)


def _gdn_bwd(dom, y, saved, p):
    proj, qn, kn, vn, gates, states, o = saved
    nb = D_MODEL // LANE
    do, dgate, dgain = _row_bwd(_gdn_post_fn, [_x(o, D_MODEL, 0), _x(proj, D_MODEL, 3)], [p["o_gain"]], [],
                                [_x(dom, D_MODEL)], [(D_MODEL, F32), (D_MODEL, BF16)], "gdn_post_bwd")
    dqn, dkn, dvn, dgates = _gdn_scan_bwd(qn, kn, vn, gates, states, do)
    dsm, dalog, ddtb = _row_bwd(_gdn_gates_fn, [_x(proj, LANE, 4 * nb)], [p["a_log"], p["dt_bias"]], [],
                                [_x(dgates, LANE)], [(LANE, BF16)], "gdn_gates_bwd")
    dxq, dwq = _gdn_conv_bwd(proj, p["conv_q"], "q", 0, dqn)
    dxk, dwk = _gdn_conv_bwd(proj, p["conv_k"], "k", 1, dkn)
    dxv, dwv = _gdn_conv_bwd(proj, p["conv_v"], "v", 2, dvn)
    dproj = jnp.concatenate([dxq, dxk, dxv, dgate, dsm], axis=1)
    grads = {
        "w_in": _matmul(y, dproj, "tn", F32, "gdn_in_dw")[:, :GDN_IN],
        "conv_w": jnp.concatenate([dwq[:GDN_CONV], dwk[:GDN_CONV], dwv[:GDN_CONV]], axis=1),
        "a_log": dalog[0, :GDN_HEADS], "dt_bias": ddtb[0, :GDN_HEADS], "o_gain": dgain[0],
    }
    return _matmul(dproj, p["w_in"], "nt", F32, "gdn_in_dx"), grads


_MIXERS = ((_fox_fwd, _fox_bwd), (_gla_fwd, _gla_bwd), (_gdn_fwd, _gdn_bwd))


def _mixer_params(i, wts, rep):
    kind, j = i % 3, i // 3
    if kind == 0:
        return {
            "w_in": _pad_cols(wts["fox_w_in"], FOX_IN_PAD), "w_out": wts["fox_w_out"],
            "q_gain": jnp.tile(rep["fox_q_gain"][j], FOX_HEADS)[None],
            "k_gain": jnp.tile(rep["fox_k_gain"][j], FOX_HEADS)[None],
            "b_f": _pad_lanes(rep["fox_b_f"][j][None]),
        }
    if kind == 1:
        return {
            "w_in": _pad_cols(wts["gla_w_in"], GLA_IN_PAD), "w_out": wts["gla_w_out"],
            "w_alpha": jnp.pad(wts["gla_w_alpha2"], ((0, LANE - GLA_RANK), (0, 0))).astype(F32),
            "b_alpha": rep["gla_b_alpha"][j][None], "o_gain": rep["gla_o_gain"][j][None],
        }
    conv = jnp.pad(wts["gdn_conv_w"].astype(F32), ((0, 8 - GDN_CONV), (0, 0)))
    return {
        "w_in": _pad_cols(wts["gdn_w_in"], GDN_IN_PAD), "w_out": wts["gdn_w_out"],
        "conv_q": conv[:, :D_MODEL], "conv_k": conv[:, D_MODEL:2 * D_MODEL], "conv_v": conv[:, 2 * D_MODEL:],
        "a_log": _pad_lanes(rep["gdn_a_log"][j][None]), "dt_bias": _pad_lanes(rep["gdn_dt_bias"][j][None]),
        "o_gain": rep["gdn_o_gain"][j][None],
    }


def kernel(x, meta_tokens, norm_mix, norm_ffn, w_gate_up, w_down, fox_w_in, fox_b_f, fox_q_gain, fox_k_gain, fox_w_out, gla_w_in, gla_w_alpha2, gla_b_alpha, gla_o_gain, gla_w_out, gdn_w_in, gdn_conv_w, gdn_a_log, gdn_dt_bias, gdn_o_gain, gdn_w_out, loss_target, m_meta_tokens, m_norm_mix, m_norm_ffn, m_w_gate_up, m_w_down, m_fox_w_in, m_fox_b_f, m_fox_q_gain, m_fox_k_gain, m_fox_w_out, m_gla_w_in, m_gla_w_alpha2, m_gla_b_alpha, m_gla_o_gain, m_gla_w_out, m_gdn_w_in, m_gdn_conv_w, m_gdn_a_log, m_gdn_dt_bias, m_gdn_o_gain, m_gdn_w_out, v_meta_tokens, v_norm_mix, v_norm_ffn, v_w_gate_up, v_w_down, v_fox_w_in, v_fox_b_f, v_fox_q_gain, v_fox_k_gain, v_fox_w_out, v_gla_w_in, v_gla_w_alpha2, v_gla_b_alpha, v_gla_o_gain, v_gla_w_out, v_gdn_w_in, v_gdn_conv_w, v_gdn_a_log, v_gdn_dt_bias, v_gdn_o_gain, v_gdn_w_out):
    given = dict(locals())
    w_loc = {n: given[n] for n in WEIGHT_ORDER}
    m_loc = {n: given["m_" + n] for n in WEIGHT_ORDER}
    v_loc = {n: given["v_" + n] for n in WEIGHT_ORDER}

    rep = {n: w_loc[n] for n, _ in REPLICATED}
    me = 4 * lax.axis_index("x") + 2 * lax.axis_index("y") + lax.axis_index("c")

    def with_own(lands, owns):
        return [lax.dynamic_update_slice(land, own[None], (me, 0, 0)) for land, own in zip(lands, owns)]

    gathers = {}
    order = jnp.zeros((), F32)
    for i in range(DEPTH):
        groups = _layer_tensors(i)
        for part in PARTS:
            blocks = [(_layer_block(w_loc[n], idx) + order).astype(_wire(n)) for n, idx in groups[part]]
            started = _send_start(blocks, True, "gather_start_%d_%s" % (i, part))
            gathers[(i, part)] = (groups[part], blocks, started)
            order = started[4][0, 0]
    all_started = gathers[(DEPTH - 1, PARTS[-1])][2][4]

    def weights(i, part, after):
        tensors, blocks, started = gathers[(i, part)]
        lands = with_own(_send_wait(started, True, after, "gather_wait_%d_%s" % (i, part)), blocks)
        return {n: _from_gathered(g, n) for (n, _), g in zip(tensors, lands)}

    h = None
    saved = []
    for i in range(DEPTH):
        w_mix = weights(i, "mix", all_started if i == 0 else h)
        if i == 0:
            h = jnp.concatenate([jnp.zeros((FIRST_ROW, D_MODEL), F32), w_mix["meta_tokens"], x[0]], axis=0)
        p = _mixer_params(i, w_mix, rep)
        fwd, _ = _MIXERS[i % 3]
        g_mix = rep["norm_mix"][i][None]
        y = _rms_fwd(h, g_mix)
        om, mix_saved = fwd(y, p)
        h_mid = _matmul(om, p["w_out"], "nn", F32, "mixer_out", add=h)
        w_ffn = weights(i, "ffn", h_mid)
        g_ffn = rep["norm_ffn"][i][None]
        h_out, ffn_saved = _ffn_fwd(h_mid, g_ffn, w_ffn["w_gate_up"], w_ffn["w_down"])
        saved.append((p, w_ffn, h, y, om, mix_saved, ffn_saved))
        h = h_out

    dh, sq = _loss_head(h, loss_target[0])
    loss = lax.psum(0.5 * jnp.sum(sq) * (1.0 / D_MODEL), ("x", "y", "c"))

    gw = {n: [None] * DEPTH for n in ("norm_mix", "norm_ffn")}
    gm = {}
    reduces = {}

    def start_reduce(i, part, full, extra=()):
        tensors = _layer_tensors(i)[part]
        slabs = [_to_slabs(full[n], n).astype(_wire(n)) for n, _ in tensors] + list(extra)
        started = _send_start(slabs, False, "reduce_start_%d_%s" % (i, part))
        reduces[(i, part)] = (tensors, slabs, started)
        return started[4][0, 0]

    for i in reversed(range(DEPTH)):
        p, w_ffn, h_in, y, om, mix_saved, ffn_saved = saved[i]
        kind, j = i % 3, i // 3
        pre = ("fox", "gla", "gdn")[kind]
        _, bwd = _MIXERS[kind]
        dh_mid, gw["norm_ffn"][i], dwgu, dwd = _ffn_bwd(
            dh, ffn_saved, rep["norm_ffn"][i][None], w_ffn["w_gate_up"], w_ffn["w_down"])
        dh_mid = dh_mid + start_reduce(i, "ffn", {"w_gate_up": dwgu, "w_down": dwd})
        dom = _matmul(dh_mid, p["w_out"], "nt", F32, "mixer_out_dx")
        dwo = _matmul(om, dh_mid, "tn", F32, "mixer_out_dw")
        dy, grads = bwd(dom, y, mix_saved, p)
        gm[(kind, j)] = grads
        dh, gw["norm_mix"][i] = _rms_bwd(h_in, rep["norm_mix"][i][None], dy, dh_mid)
        mix_full = {pre + "_w_in": grads["w_in"], pre + "_w_out": dwo, "gla_w_alpha2": grads.get("w_alpha"),
                    "gdn_conv_w": grads.get("conv_w"), "meta_tokens": dh[FIRST_ROW:PAD_ROWS]}
        if i > 0:
            dh = dh + start_reduce(i, "mix", mix_full)

    partial = {
        "norm_mix": jnp.concatenate(gw["norm_mix"], axis=0), "norm_ffn": jnp.concatenate(gw["norm_ffn"], axis=0),
        "fox_b_f": jnp.stack([gm[(0, 0)]["b_f"], gm[(0, 1)]["b_f"]]),
        "fox_q_gain": jnp.stack([gm[(0, 0)]["q_gain"], gm[(0, 1)]["q_gain"]]),
        "fox_k_gain": jnp.stack([gm[(0, 0)]["k_gain"], gm[(0, 1)]["k_gain"]]),
        "gla_b_alpha": gm[(1, 0)]["b_alpha"][None], "gla_o_gain": gm[(1, 0)]["o_gain"][None],
        "gdn_a_log": gm[(2, 0)]["a_log"][None], "gdn_dt_bias": gm[(2, 0)]["dt_bias"][None],
        "gdn_o_gain": gm[(2, 0)]["o_gain"][None],
    }

    small = _pack([partial[n].reshape(-1) for n, _ in REPLICATED], 8, F32)
    start_reduce(0, "mix", mix_full, extra=[jnp.broadcast_to(small[None], (N_DEV,) + small.shape)])
    recv = {}
    for (i, part), (tensors, slabs, started) in reduces.items():
        owns = [lax.dynamic_index_in_dim(s, me, 0, keepdims=False) for s in slabs]
        lands = with_own(_send_wait(started, False, dh, "reduce_wait_%d_%s" % (i, part)), owns)
        for (n, idx), land in zip(tensors, lands):
            recv[(n, idx)] = land
        if (i, part) == (0, "mix"):
            recv_small = lands[-1]

    res = [{} for _ in range(4)]
    for n, s, _ in SHARDED:
        layers = [idx for i in range(DEPTH) for part in PARTS for (nn, idx) in _layer_tensors(i)[part] if nn == n]
        layers = sorted(layers, key=lambda idx: -1 if idx is None else idx)
        outs = _adamw([recv[(n, idx)] for idx in layers], *[_local_2d(d[n]) for d in (w_loc, m_loc, v_loc)],
                      "adamw_" + n)
        for k in range(4):
            res[k][n] = outs[k].reshape(s)
    outs_small = _adamw([recv_small], *[_pack([d[n].reshape(-1) for n, _ in REPLICATED], 8, F32)
                                        for d in (w_loc, m_loc, v_loc)], "adamw_replicated")
    for k in range(4):
        res[k].update(_unpack(outs_small[k], list(REPLICATED)))

    result = [loss, dh[PAD_ROWS:][None]]
    for k in range(4):
        result += [res[k][n] for n in WEIGHT_ORDER]
    return tuple(result)
```

```python
import functools

import jax
import jax.numpy as jnp
from jax import lax
from jax.experimental import pallas as pl
from jax.experimental.pallas import tpu as pltpu

F32, BF16 = jnp.float32, jnp.bfloat16
HIGHEST = lax.Precision.HIGHEST

D_MODEL = 1024
SEQ = 4096
N_META = 16
DEPTH = 4
NORM_EPS = 1e-6
LANE = 128
PAD_ROWS = 128
FIRST_ROW = PAD_ROWS - N_META
ROWS = PAD_ROWS + SEQ
ROW_TILE = 128
CHUNK = 64
NEG = -1e30
N_DEV = 8
VMEM_LIMIT = 56 * 1024 * 1024
VMEM_BLOCK_BUDGET = 36 * 1024 * 1024

FOX_HEADS, FOX_DH = 16, 64
FOX_IN, FOX_IN_PAD = 4 * D_MODEL + FOX_HEADS, 4 * D_MODEL + LANE
GLA_HEADS, GLA_DK, GLA_DV, GLA_RANK = 4, 128, 256, 16
GLA_QK, GLA_V = GLA_HEADS * GLA_DK, GLA_HEADS * GLA_DV
GLA_IN, GLA_IN_PAD = 2 * GLA_QK + 2 * GLA_V + GLA_RANK, 2 * GLA_QK + 2 * GLA_V + LANE
GDN_HEADS, GDN_DK, GDN_DV, GDN_CONV = 8, 128, 128, 4
GDN_CONV_DIM = 3 * GDN_HEADS * GDN_DK
GDN_IN, GDN_IN_PAD = GDN_CONV_DIM + GDN_HEADS * GDN_DV + 2 * GDN_HEADS, GDN_CONV_DIM + GDN_HEADS * GDN_DV + LANE
D_FF = 2816

ADAM_LR, ADAM_B1, ADAM_B2, ADAM_EPS, ADAM_WD, ADAM_STEP = 0.001, 0.9, 0.999, 1e-08, 0.01, 10

PACK_COLS = 1024
PACK_ROW_TILE = 256

SHARDED = (
    ("meta_tokens", (16, 128), 1),
    ("w_gate_up", (4, 1024, 704), 2),
    ("w_down", (4, 352, 1024), 1),
    ("fox_w_in", (2, 1024, 514), 2),
    ("fox_w_out", (2, 128, 1024), 1),
    ("gla_w_in", (1, 1024, 386), 2),
    ("gla_w_alpha2", (1, 16, 64), 2),
    ("gla_w_out", (1, 128, 1024), 1),
    ("gdn_w_in", (1, 1024, 514), 2),
    ("gdn_conv_w", (1, 4, 1, 384), 3),
    ("gdn_w_out", (1, 128, 1024), 1),
)
REPLICATED = (
    ("norm_mix", (4, 1024)),
    ("norm_ffn", (4, 1024)),
    ("fox_b_f", (2, 16)),
    ("fox_q_gain", (2, 64)),
    ("fox_k_gain", (2, 64)),
    ("gla_b_alpha", (1, 512)),
    ("gla_o_gain", (1, 256)),
    ("gdn_a_log", (1, 8)),
    ("gdn_dt_bias", (1, 8)),
    ("gdn_o_gain", (1, 128)),
)
WEIGHT_ORDER = (
    "meta_tokens", "norm_mix", "norm_ffn", "w_gate_up", "w_down", "fox_w_in", "fox_b_f", "fox_q_gain",
    "fox_k_gain", "fox_w_out", "gla_w_in", "gla_w_alpha2", "gla_b_alpha", "gla_o_gain", "gla_w_out",
    "gdn_w_in", "gdn_conv_w", "gdn_a_log", "gdn_dt_bias", "gdn_o_gain", "gdn_w_out",
)

_DN = {
    "nn": (((1,), (0,)), ((), ())),
    "nt": (((1,), (1,)), ((), ())),
    "tn": (((0,), (0,)), ((), ())),
}


def _dot(a, b, kind):
    return lax.dot_general(a.astype(BF16), b.astype(BF16), _DN[kind], preferred_element_type=F32)


@functools.partial(jax.custom_vjp, nondiff_argnums=(2,))
def _dot_ad(a, b, kind):
    return _dot(a, b, kind)


def _dot_ad_fwd(a, b, kind):
    return _dot(a, b, kind), (a, b)


def _dot_ad_bwd(kind, res, ct):
    a, b = res
    if kind == "nn":
        return _dot(ct, b, "nt"), _dot(a, ct, "tn")
    if kind == "nt":
        return _dot(ct, b, "nn"), _dot(ct, a, "tn")
    return _dot(b, ct, "nt"), _dot(a, ct, "nn")


_dot_ad.defvjp(_dot_ad_fwd, _dot_ad_bwd)


def _dot_hi(a, b, kind):
    return lax.dot_general(a, b, _DN[kind], precision=HIGHEST, preferred_element_type=F32)


def _dot_mid(a, b, kind):
    return lax.dot_general(a, b, _DN[kind], precision=lax.Precision.HIGH, preferred_element_type=F32)


def _iota(shape, dim):
    return lax.broadcasted_iota(jnp.int32, shape, dim)


def _row_mask(i, tm):
    return ((i * tm + _iota((tm, 1), 0)) >= FIRST_ROW).astype(F32)


def _log_sigmoid(z):
    return jnp.minimum(z, 0.0) - jnp.log(1.0 + jnp.exp(-jnp.abs(z)))


def _softplus(z):
    return jnp.maximum(z, 0.0) + jnp.log(1.0 + jnp.exp(-jnp.abs(z)))


def _silu(z):
    return z * jax.nn.sigmoid(z)


def _lane_col(x, lane):
    return jnp.sum(jnp.where(_iota(x.shape, 1) == lane, x, 0.0), axis=1, keepdims=True)


def _last_row(x):
    return jnp.sum(jnp.where(_iota(x.shape, 0) == x.shape[0] - 1, x, 0.0), axis=0, keepdims=True)


def _params(sem, limit=VMEM_LIMIT):
    return pltpu.CompilerParams(dimension_semantics=sem, vmem_limit_bytes=limit)


def _pick(n, options):
    for t in options:
        if n % t == 0:
            return t
    return n


def _matmul(a, b, kind, out_dtype, name, add=None, b_kblock=0):
    if kind == "nn":
        (m, k), n = a.shape, b.shape[1]
    elif kind == "nt":
        (m, k), n = a.shape, b.shape[0]
    else:
        (k, m), n = a.shape, b.shape[1]
    tn = _pick(n, (512, 384, 640, 256, 128))
    tm_options = (1408, 384, 128) if kind != "tn" else (1024, 512, 256, 128)
    tm = None
    for t in tm_options:
        if m % t:
            continue
        need = 2 * (t * k * a.dtype.itemsize + tn * k * b.dtype.itemsize + t * tn * 8)
        if need <= VMEM_BLOCK_BUDGET:
            tm = t
            break
    assert tm is not None, (name, a.shape, b.shape)
    a_bytes, b_bytes = m * k * a.dtype.itemsize, n * k * b.dtype.itemsize
    m_outer = a_bytes + (m // tm) * b_bytes <= b_bytes + (n // tn) * a_bytes

    def ij(fn):
        return (lambda i, j: fn(i, j)) if m_outer else (lambda j, i: fn(i, j))

    def body(*refs):
        a_ref, b_ref = refs[0], refs[1]
        o_ref = refs[-1]
        acc = _dot(a_ref[...], b_ref[...], kind)
        if add is not None:
            acc = acc + refs[2][...]
        o_ref[...] = acc.astype(o_ref.dtype)

    if kind == "tn":
        a_spec = pl.BlockSpec((k, tm), ij(lambda i, j: (0, i)))
    else:
        a_spec = pl.BlockSpec((tm, k), ij(lambda i, j: (i, 0)))
    if kind == "nt":
        b_spec = pl.BlockSpec((tn, k), ij(lambda i, j: (j, b_kblock)))
    else:
        b_spec = pl.BlockSpec((k, tn), ij(lambda i, j: (0, j)))
    o_spec = pl.BlockSpec((tm, tn), ij(lambda i, j: (i, j)))
    operands, in_specs = [a, b], [a_spec, b_spec]
    if add is not None:
        operands.append(add)
        in_specs.append(o_spec)
    grid = (m // tm, n // tn) if m_outer else (n // tn, m // tm)
    return pl.pallas_call(
        body, grid=grid, in_specs=in_specs, out_specs=o_spec,
        out_shape=jax.ShapeDtypeStruct((m, n), out_dtype), name=name,
        compiler_params=_params(("parallel", "parallel")),
    )(*operands)


def _x_spec(tm, width, cb0, moves):
    return pl.BlockSpec((tm, width), lambda i, j: (i, cb0 + j * moves))


def _full_spec(arr):
    nd = arr.ndim
    return pl.BlockSpec(arr.shape, lambda i, j: (0,) * nd)


def _row_fwd(fn, xs, ps, cs, outs, name, tm=ROW_TILE, ncol=1, rows=None):
    rows = ROWS if rows is None else rows
    nx, npar, nc = len(xs), len(ps), len(cs)

    def body(*refs):
        i = pl.program_id(0)
        xv = [r[...].astype(F32) for r in refs[:nx]]
        pv = [r[...] for r in refs[nx:nx + npar + nc]]
        res = fn(i, *xv, *pv)
        for r, val in zip(refs[nx + npar + nc:], res):
            r[...] = val.astype(r.dtype)

    in_specs = [_x_spec(tm, w, cb0, mv) for (_, w, cb0, mv) in xs]
    in_specs += [_full_spec(p) for p in list(ps) + list(cs)]
    out_specs = [_x_spec(tm, w, 0, 1) for (w, _, _) in outs]
    out_shape = [jax.ShapeDtypeStruct((rows, tot), dt) for (_, tot, dt) in outs]
    return pl.pallas_call(
        body, grid=(rows // tm, ncol), in_specs=in_specs, out_specs=out_specs, out_shape=out_shape,
        name=name, compiler_params=_params(("parallel", "parallel")),
    )(*[x[0] for x in xs], *ps, *cs)


def _row_bwd(fn, xs, ps, cs, cts, dxs, name, adds=None, tm=ROW_TILE, ncol=1, rows=None):
    rows = ROWS if rows is None else rows
    nx, npar, nc, nct = len(xs), len(ps), len(cs), len(cts)
    adds = [None] * nx if adds is None else adds
    add_idx = [k for k in range(nx) if adds[k] is not None]

    def body(*refs):
        i, j = pl.program_id(0), pl.program_id(1)
        pos = 0
        x_refs = refs[pos:pos + nx]; pos += nx
        p_refs = refs[pos:pos + npar]; pos += npar
        c_refs = refs[pos:pos + nc]; pos += nc
        ct_refs = refs[pos:pos + nct]; pos += nct
        add_refs = refs[pos:pos + len(add_idx)]; pos += len(add_idx)
        dx_refs = refs[pos:pos + nx]; pos += nx
        dp_refs = refs[pos:pos + npar]
        xv = [r[...].astype(F32) for r in x_refs]
        pv = [r[...] for r in p_refs]
        cv = [r[...] for r in c_refs]
        _, vjp = jax.vjp(lambda *args: fn(i, *args, *cv), *xv, *pv)
        grads = vjp(tuple(r[...].astype(F32) for r in ct_refs))
        for k in range(nx):
            val = grads[k]
            if adds[k] is not None:
                val = val + add_refs[add_idx.index(k)][...]
            dx_refs[k][...] = val.astype(dx_refs[k].dtype)
        if npar:
            @pl.when((i == 0) & (j == 0))
            def _():
                for r in dp_refs:
                    r[...] = jnp.zeros(r.shape, r.dtype)

            for k in range(npar):
                dp_refs[k][...] += grads[nx + k]

    in_specs = [_x_spec(tm, w, cb0, mv) for (_, w, cb0, mv) in xs]
    in_specs += [_full_spec(p) for p in list(ps) + list(cs)]
    in_specs += [_x_spec(tm, w, cb0, mv) for (_, w, cb0, mv) in cts]
    in_specs += [_x_spec(tm, xs[k][1], 0, 1) for k in add_idx]
    out_specs = [_x_spec(tm, xs[k][1], 0, 1) for k in range(nx)] + [_full_spec(p) for p in ps]
    out_shape = [jax.ShapeDtypeStruct((rows, tot), dt) for (tot, dt) in dxs]
    out_shape += [jax.ShapeDtypeStruct(p.shape, F32) for p in ps]
    sem = ("arbitrary", "arbitrary") if npar else ("parallel", "parallel")
    return pl.pallas_call(
        body, grid=(rows // tm, ncol), in_specs=in_specs, out_specs=out_specs, out_shape=out_shape,
        name=name, compiler_params=_params(sem),
    )(*[x[0] for x in xs], *ps, *cs, *[c[0] for c in cts], *[adds[k] for k in add_idx])


def _rms_fn(i, h, g):
    return (h * lax.rsqrt(jnp.mean(h * h, axis=1, keepdims=True) + NORM_EPS) * g,)


def _fox_pre_fn(i, q, k, f, qg, kg, bf, e_mean, e_expand):
    tm = q.shape[0]

    def headnorm(x, gain, scale):
        msq = _dot_mid(x * x, e_mean, "nn")
        r = _dot_mid(lax.rsqrt(msq + NORM_EPS), e_expand, "nn")
        return x * r * gain * scale

    qn = headnorm(q, qg, FOX_DH ** -0.5)
    kn = headnorm(k, kg, 1.0)
    keep = (_iota((tm, LANE), 1) < FOX_HEADS).astype(F32) * _row_mask(i, tm)
    return qn, kn, _log_sigmoid(f + bf) * keep


def _fox_post_fn(i, o, gate):
    return (o * jax.nn.sigmoid(gate) * _row_mask(i, o.shape[0]),)


def _gla_pre_fn(i, alr, wa, ba):
    z = _dot_ad(alr, wa, "nn") + ba
    return (_log_sigmoid(z) * (1.0 / 16.0) * _row_mask(i, alr.shape[0]),)


def _headwise_norm_gate(o, gate, gain, dv, mask):
    pieces = []
    for h in range(o.shape[1] // dv):
        oh = o[:, h * dv:(h + 1) * dv]
        pieces.append(oh * lax.rsqrt(jnp.mean(oh * oh, axis=1, keepdims=True) + NORM_EPS) * gain)
    return jnp.concatenate(pieces, axis=1) * _silu(gate) * mask


def _gla_post_fn(i, o, r, gain):
    return (_headwise_norm_gate(o, r, gain, GLA_DV, _row_mask(i, o.shape[0])),)


def _gdn_post_fn(i, o, gate, gain):
    return (_headwise_norm_gate(o, gate, gain, GDN_DV, _row_mask(i, o.shape[0])),)


def _gdn_gates_fn(i, sm, alog, dtb):
    tm = sm.shape[0]
    lane = _iota((tm, LANE), 1)
    g = -jnp.exp(alog) * _softplus(sm + dtb)
    beta = jax.nn.sigmoid(sm)
    out = jnp.where(lane < GDN_HEADS, g, jnp.where(lane < 2 * GDN_HEADS, beta, 0.0))
    return (out * _row_mask(i, tm),)


def _gdn_conv_fn(kind):
    def fn(i, prev, cur, w):
        tm = cur.shape[0]
        xw = jnp.concatenate([prev * jnp.where(i > 0, 1.0, 0.0), cur], axis=0)
        t = _iota((tm, 2 * tm), 0)
        s = _iota((tm, 2 * tm), 1)
        y = jnp.zeros(cur.shape, F32)
        for j in range(GDN_CONV):
            sel = (s == t + (tm - (GDN_CONV - 1) + j)).astype(F32)
            y = y + _dot_mid(sel, xw, "nn") * jnp.sum(jnp.where(_iota(w.shape, 0) == j, w, 0.0), axis=0, keepdims=True)
        a = _silu(y * _row_mask(i, tm))
        if kind == "v":
            return (a,)
        scale = GDN_DK ** -0.5 if kind == "q" else 1.0
        pieces = []
        for h in range(GDN_HEADS):
            ah = a[:, h * GDN_DK:(h + 1) * GDN_DK]
            pieces.append(ah * lax.rsqrt(jnp.sum(ah * ah, axis=1, keepdims=True) + NORM_EPS) * scale)
        return (jnp.concatenate(pieces, axis=1),)

    return fn


def _seq_cumsum(x, reverse, name, pairs=None):
    rows, width = x.shape
    nt = rows // ROW_TILE

    def body(*refs):
        x_ref, o_ref, carry = refs[0], refs[-2], refs[-1]

        @pl.when(pl.program_id(0) == 0)
        def _():
            carry[...] = jnp.zeros(carry.shape, F32)

        xv = x_ref[...]
        if pairs is not None:
            lane = _iota(xv.shape, 1)
            for g in range(pairs.shape[0]):
                ta, tb = _pair_cols(refs[1][g])
                xv = xv + jnp.where(lane == 2 * g, ta, 0.0) + jnp.where(lane == 2 * g + 1, tb, 0.0)
        r = _iota((ROW_TILE, ROW_TILE), 0)
        c = _iota((ROW_TILE, ROW_TILE), 1)
        tri = ((c >= r) if reverse else (c <= r)).astype(F32)
        acc = _dot_hi(tri, xv, "nn") + carry[...]
        o_ref[...] = acc
        edge = 0 if reverse else ROW_TILE - 1
        carry[...] = jnp.sum(jnp.where(_iota(acc.shape, 0) == edge, acc, 0.0), axis=0, keepdims=True)

    idx = (lambda i: (nt - 1 - i, 0)) if reverse else (lambda i: (i, 0))
    idx3 = (lambda i: (0, nt - 1 - i, 0)) if reverse else (lambda i: (0, i, 0))
    operands, in_specs = [x], [pl.BlockSpec((ROW_TILE, width), idx)]
    if pairs is not None:
        operands.append(pairs)
        in_specs.append(pl.BlockSpec((pairs.shape[0], ROW_TILE, LANE), idx3))
    return pl.pallas_call(
        body, grid=(nt,), in_specs=in_specs,
        out_specs=pl.BlockSpec((ROW_TILE, width), idx), out_shape=jax.ShapeDtypeStruct(x.shape, F32),
        scratch_shapes=[pltpu.VMEM((1, width), F32)], name=name, compiler_params=_params(("arbitrary",)),
    )(*operands)


FOX_PAIRS = FOX_HEADS // 2
FOX_V_BLOCK = 2 * D_MODEL // LANE


def _pair_cols(x):
    return _lane_col(x, 0), _lane_col(x, FOX_DH)


ATT_TILE = 384
assert ROWS % ATT_TILE == 0
N_ATT = ROWS // ATT_TILE
KEY_OFF = 1e30


def _fox_scores(qa, qb, kj, ca, cb, cra, crb, mask):
    sa = _dot(qa, kj, "nt") + (ca - cra)
    sb = _dot(qb, kj, "nt") + (cb - crb)
    if mask is None:
        return sa, sb
    return jnp.where(mask, sa, NEG), jnp.where(mask, sb, NEG)


def _fox_diag_mask():
    return _iota((ATT_TILE, ATT_TILE), 1) <= _iota((ATT_TILE, ATT_TILE), 0)


def _att_off(j):
    return pl.multiple_of(j * ATT_TILE, ATT_TILE)


def _fox_attn_fwd(qn, kn, proj, c2, crow):
    nt = ROWS // ROW_TILE

    def body(q_ref, k_ref, v_ref, c2_ref, crow_ref, o_ref, lse_ref):
        i = pl.program_id(1)
        q = q_ref[...]
        is_a = _iota((ATT_TILE, LANE), 1) < FOX_DH
        qa, qb = jnp.where(is_a, q, 0), jnp.where(is_a, 0, q)
        ca, cb = _pair_cols(c2_ref[0])

        def block(off, size, mask, carry):
            ma, la, acca, mb, lb, accb = carry
            kj = k_ref[pl.ds(off, size), :]
            vj = v_ref[pl.ds(off, size), :].astype(BF16)
            sa, sb = _fox_scores(qa, qb, kj, ca, cb, crow_ref[0, 0:1, pl.ds(off, size)],
                                 crow_ref[0, 1:2, pl.ds(off, size)], mask)

            def online(s, m, l, acc):
                m_new = jnp.maximum(m, jnp.max(s, axis=1, keepdims=True))
                p = jnp.exp(s - m_new)
                alpha = jnp.exp(m - m_new)
                return m_new, alpha * l + jnp.sum(p, axis=1, keepdims=True), alpha * acc + _dot(p, vj, "nn")

            return online(sa, ma, la, acca) + online(sb, mb, lb, accb)

        m0 = jnp.full((ATT_TILE, 1), NEG, F32)
        l0 = jnp.zeros((ATT_TILE, 1), F32)
        a0 = jnp.zeros((ATT_TILE, LANE), F32)
        carry = lax.fori_loop(0, i, lambda j, cr: block(_att_off(j), ATT_TILE, None, cr), (m0, l0, a0, m0, l0, a0))
        ma, la, acca, mb, lb, accb = block(_att_off(i), ATT_TILE, _fox_diag_mask(), carry)
        o_ref[...] = jnp.where(is_a, acca / la, accb / lb)
        lse_ref[0] = jnp.where(is_a, ma + jnp.log(la), mb + jnp.log(lb))

    return pl.pallas_call(
        body, grid=(FOX_PAIRS, N_ATT),
        in_specs=[
            pl.BlockSpec((ATT_TILE, LANE), lambda g, i: (i, g)),
            pl.BlockSpec((ROWS, LANE), lambda g, i: (0, g)),
            pl.BlockSpec((ROWS, LANE), lambda g, i: (0, FOX_V_BLOCK + g)),
            pl.BlockSpec((1, ATT_TILE, LANE), lambda g, i: (g, i, 0)),
            pl.BlockSpec((1, 2, ROWS), lambda g, i: (g, 0, 0)),
        ],
        out_specs=[
            pl.BlockSpec((ATT_TILE, LANE), lambda g, i: (i, g)),
            pl.BlockSpec((1, ATT_TILE, LANE), lambda g, i: (g, i, 0)),
        ],
        out_shape=[jax.ShapeDtypeStruct((ROWS, D_MODEL), F32), jax.ShapeDtypeStruct((FOX_PAIRS, ROWS, LANE), F32)],
        name="fox_attn_fwd", compiler_params=_params(("parallel", "arbitrary")),
    )(qn, kn, proj, c2, crow)


def _fox_attn_dq(qn, kn, proj, c2, crow, lse2, o, do, blockdiag):
    nt = ROWS // ROW_TILE

    def body(q_ref, k_ref, v_ref, c2_ref, crow_ref, lse_ref, o_ref, do_ref, bd_ref, dq_ref, delta_ref, dcq_ref):
        i = pl.program_id(1)
        q = q_ref[...]
        is_a = _iota((ATT_TILE, LANE), 1) < FOX_DH
        qa, qb = jnp.where(is_a, q, 0), jnp.where(is_a, 0, q)
        ca, cb = _pair_cols(c2_ref[0])
        lsa, lsb = _pair_cols(lse_ref[0])
        do_t = do_ref[...]
        delta2 = _dot_hi(do_t * o_ref[...], bd_ref[...], "nn")
        delta_ref[0] = delta2
        dla, dlb = _pair_cols(delta2)
        doa, dob = jnp.where(is_a, do_t, 0.0).astype(BF16), jnp.where(is_a, 0.0, do_t).astype(BF16)

        def block(off, size, mask, carry):
            dqa, dqb, rsa, rsb = carry
            kj = k_ref[pl.ds(off, size), :]
            vj = v_ref[pl.ds(off, size), :].astype(BF16)
            sa, sb = _fox_scores(qa, qb, kj, ca, cb, crow_ref[0, 0:1, pl.ds(off, size)],
                                 crow_ref[0, 1:2, pl.ds(off, size)], mask)
            dsa = jnp.exp(sa - lsa) * (_dot(doa, vj, "nt") - dla)
            dsb = jnp.exp(sb - lsb) * (_dot(dob, vj, "nt") - dlb)
            return (dqa + _dot(dsa, kj, "nn"), dqb + _dot(dsb, kj, "nn"),
                    rsa + jnp.sum(dsa, axis=1, keepdims=True), rsb + jnp.sum(dsb, axis=1, keepdims=True))

        z = jnp.zeros((ATT_TILE, LANE), F32)
        zc = jnp.zeros((ATT_TILE, 1), F32)
        carry = lax.fori_loop(0, i, lambda j, cr: block(_att_off(j), ATT_TILE, None, cr), (z, z, zc, zc))
        dqa, dqb, rsa, rsb = block(_att_off(i), ATT_TILE, _fox_diag_mask(), carry)
        dq_ref[...] = jnp.where(is_a, dqa, dqb)
        dcq_ref[0] = jnp.where(is_a, rsa, rsb)

    tile = pl.BlockSpec((ATT_TILE, LANE), lambda g, i: (i, g))
    pair_tile = pl.BlockSpec((1, ATT_TILE, LANE), lambda g, i: (g, i, 0))
    return pl.pallas_call(
        body, grid=(FOX_PAIRS, N_ATT),
        in_specs=[
            tile,
            pl.BlockSpec((ROWS, LANE), lambda g, i: (0, g)),
            pl.BlockSpec((ROWS, LANE), lambda g, i: (0, FOX_V_BLOCK + g)),
            pair_tile,
            pl.BlockSpec((1, 2, ROWS), lambda g, i: (g, 0, 0)),
            pair_tile, tile, tile,
            pl.BlockSpec((LANE, LANE), lambda g, i: (0, 0)),
        ],
        out_specs=[tile, pair_tile, pair_tile],
        out_shape=[jax.ShapeDtypeStruct((ROWS, D_MODEL), F32), jax.ShapeDtypeStruct((FOX_PAIRS, ROWS, LANE), F32),
                   jax.ShapeDtypeStruct((FOX_PAIRS, ROWS, LANE), F32)],
        name="fox_attn_dq", compiler_params=_params(("parallel", "arbitrary")),
    )(qn, kn, proj, c2, crow, lse2, o, do, blockdiag)


def _fox_attn_dkv(qn, kn, proj, c2, crow, lse2, delta2, do):
    nt = ROWS // ROW_TILE

    def body(q_ref, k_ref, v_ref, c2_ref, crow_ref, lse_ref, delta_ref, do_ref, dk_ref, dv_ref, dc_ref):
        j = pl.program_id(1)
        kj = k_ref[...]
        vj = v_ref[...].astype(BF16)
        cra, crb = crow_ref[0, 0:1, :], crow_ref[0, 1:2, :]

        def block(off, size, mask, carry):
            dk, dv, dca, dcb = carry
            is_a = _iota((size, LANE), 1) < FOX_DH
            q = q_ref[pl.ds(off, size), :]
            qa, qb = jnp.where(is_a, q, 0), jnp.where(is_a, 0, q)
            do_t = do_ref[pl.ds(off, size), :]
            doa, dob = jnp.where(is_a, do_t, 0.0).astype(BF16), jnp.where(is_a, 0.0, do_t).astype(BF16)
            ca, cb = _pair_cols(c2_ref[0, pl.ds(off, size), :])
            lsa, lsb = _pair_cols(lse_ref[0, pl.ds(off, size), :])
            dla, dlb = _pair_cols(delta_ref[0, pl.ds(off, size), :])
            sa, sb = _fox_scores(qa, qb, kj, ca, cb, cra, crb, mask)
            pa, pb = jnp.exp(sa - lsa), jnp.exp(sb - lsb)
            dsa = pa * (_dot(doa, vj, "nt") - dla)
            dsb = pb * (_dot(dob, vj, "nt") - dlb)
            dv = dv + _dot(pa, doa, "tn") + _dot(pb, dob, "tn")
            dk = dk + _dot(dsa, qa, "tn") + _dot(dsb, qb, "tn")
            dca = dca - jnp.sum(dsa, axis=0, keepdims=True)
            dcb = dcb - jnp.sum(dsb, axis=0, keepdims=True)
            return dk, dv, dca, dcb

        z = jnp.zeros((ATT_TILE, LANE), F32)
        zr = jnp.zeros((1, ATT_TILE), F32)
        carry = block(_att_off(j), ATT_TILE, _fox_diag_mask(), (z, z, zr, zr))
        dk, dv, dca, dcb = lax.fori_loop(j + 1, N_ATT, lambda i, cr: block(_att_off(i), ATT_TILE, None, cr), carry)
        dk_ref[...] = dk
        dv_ref[...] = dv.astype(dv_ref.dtype)
        dc_ref[0, 0:1, :] = dca
        dc_ref[0, 1:2, :] = dcb

    whole = pl.BlockSpec((ROWS, LANE), lambda g, j: (0, g))
    pair_whole = pl.BlockSpec((1, ROWS, LANE), lambda g, j: (g, 0, 0))
    tile = pl.BlockSpec((ATT_TILE, LANE), lambda g, j: (j, g))
    return pl.pallas_call(
        body, grid=(FOX_PAIRS, N_ATT),
        in_specs=[
            whole, tile,
            pl.BlockSpec((ATT_TILE, LANE), lambda g, j: (j, FOX_V_BLOCK + g)),
            pair_whole,
            pl.BlockSpec((1, 2, ATT_TILE), lambda g, j: (g, 0, j)),
            pair_whole, pair_whole, whole,
        ],
        out_specs=[tile, tile, pl.BlockSpec((1, 2, ATT_TILE), lambda g, j: (g, 0, j))],
        out_shape=[jax.ShapeDtypeStruct((ROWS, D_MODEL), F32), jax.ShapeDtypeStruct((ROWS, D_MODEL), BF16),
                   jax.ShapeDtypeStruct((FOX_PAIRS, 2, ROWS), F32)],
        name="fox_attn_dkv", compiler_params=_params(("parallel", "arbitrary")),
    )(qn, kn, proj, c2, crow, lse2, delta2, do)


def _gla_chunk(q, k, v, g, st, dot):
    n = len(q)
    c = q[0].shape[0]
    incl = _iota((c, c), 1) <= _iota((c, c), 0)
    incl_f = incl.astype(F32)
    b = [_dot_hi(incl_f, g[i], "nn") for i in range(n)]
    qe = [q[i] * (GLA_DK ** -0.5) * jnp.exp(b[i]) for i in range(n)]
    ke = [k[i] * jnp.exp(-b[i]) for i in range(n)]
    att = [jnp.where(incl, dot(qe[i], ke[i], "nt"), 0.0) for i in range(n)]
    o_intra = [dot(att[i], v[i], "nn") for i in range(n)]
    o_inter = [dot(qe[i], st[i], "nt") for i in range(n)]
    b_last = [_last_row(b[i]) for i in range(n)]
    kd = [k[i] * jnp.exp(b_last[i] - b[i]) for i in range(n)]
    upd = [dot(v[i], kd[i], "tn") for i in range(n)]
    return ([o_intra[i] + o_inter[i] for i in range(n)],
            [st[i] * jnp.exp(b_last[i]) + upd[i] for i in range(n)])


def _tri_inverse(low):
    c = low[0].shape[0]
    eye = (_iota((c, c), 0) == _iota((c, c), 1)).astype(F32)
    x, p = [eye - l for l in low], list(low)
    span = 2
    while span < c:
        p = [_dot_mid(pi, pi, "nn") for pi in p]
        x = [_dot_mid(xi, eye + pi, "nn") for xi, pi in zip(x, p)]
        span *= 2
    return x


def _solve(low, rhs):
    return [_dot_mid(inv, r, "nn") for inv, r in zip(_tri_inverse(low), rhs)]


@jax.custom_vjp
def _solve_ad(low, rhs):
    return _solve(low, rhs)


def _solve_ad_fwd(low, rhs):
    inv = _tri_inverse(low)
    sol = [_dot_mid(i, r, "nn") for i, r in zip(inv, rhs)]
    return sol, (inv, sol)


def _solve_ad_bwd(res, ct):
    inv, sol = res
    drhs = [_dot_mid(i, c, "tn") for i, c in zip(inv, ct)]
    return [-_dot_mid(d, s, "nt") for d, s in zip(drhs, sol)], drhs


_solve_ad.defvjp(_solve_ad_fwd, _solve_ad_bwd)


def _gdn_chunk(q, k, v, gates, st, heads, dot, solve):
    n = len(q)
    c = q[0].shape[0]
    incl = _iota((c, c), 1) <= _iota((c, c), 0)
    strict = _iota((c, c), 1) < _iota((c, c), 0)
    incl_f = incl.astype(F32)
    upper_f = (_iota((c, c), 0) <= _iota((c, c), 1)).astype(F32)
    gcol = [_lane_col(gates, h) for h in heads]
    beta = [_lane_col(gates, h + GDN_HEADS) for h in heads]
    b = [_dot_hi(incl_f, jnp.broadcast_to(gc, (c, LANE)), "nn") for gc in gcol]
    g_sq = [jnp.broadcast_to(gc, (c, c)) for gc in gcol]
    b_col = [_dot_hi(incl_f, gs, "nn") for gs in g_sq]
    b_row = [_dot_hi(gs, upper_f, "tn") for gs in g_sq]
    diff = [b_col[i] - b_row[i] for i in range(n)]
    kb = [k[i] * beta[i] for i in range(n)]
    vb = [v[i] * beta[i] for i in range(n)]
    low = [dot(kb[i], k[i], "nt") * jnp.exp(jnp.where(strict, diff[i], NEG)) for i in range(n)]
    eb = [jnp.exp(bi) for bi in b]
    sol = solve(low, [jnp.concatenate([vb[i], kb[i] * eb[i]], axis=1) for i in range(n)])
    w_st = [dot(sol[i][:, GDN_DV:], st[i], "nn") for i in range(n)]
    v_new = [sol[i][:, :GDN_DV] - w_st[i] for i in range(n)]
    att = [dot(q[i], k[i], "nt") * jnp.exp(jnp.where(incl, diff[i], NEG)) for i in range(n)]
    o_intra = [dot(att[i], v_new[i], "nn") for i in range(n)]
    o_inter = [dot(q[i] * eb[i], st[i], "nn") for i in range(n)]
    b_last = [_last_row(bi) for bi in b]
    kd = [k[i] * jnp.exp(b_last[i] - b[i]) for i in range(n)]
    upd = [dot(kd[i], v_new[i], "tn") for i in range(n)]
    return ([o_intra[i] + o_inter[i] for i in range(n)],
            [st[i] * jnp.exp(b_last[i]) + upd[i] for i in range(n)])


N_CHUNKS = ROWS // CHUNK
GDN_GROUP = 8


def _head_lanes(width, n):
    return [slice(i * width, (i + 1) * width) for i in range(n)]


def _gla_scan_fwd(proj, garr):
    kl, vl = _head_lanes(GLA_DK, GLA_HEADS), _head_lanes(GLA_DV, GLA_HEADS)

    def body(q_ref, k_ref, v_ref, g_ref, o_ref, states_ref, st):
        @pl.when(pl.program_id(0) == 0)
        def _():
            st[...] = jnp.zeros(st.shape, F32)

        s_in = [st[i] for i in range(GLA_HEADS)]
        o, s_out = _gla_chunk([q_ref[:, ln] for ln in kl], [k_ref[:, ln] for ln in kl], [v_ref[:, ln] for ln in vl],
                              [g_ref[:, ln] for ln in kl], s_in, _dot)
        for i in range(GLA_HEADS):
            states_ref[i, 0] = s_in[i]
            o_ref[:, vl[i]] = o[i]
            st[i] = s_out[i]

    return pl.pallas_call(
        body, grid=(N_CHUNKS,),
        in_specs=[
            pl.BlockSpec((CHUNK, GLA_QK), lambda c: (c, 0)),
            pl.BlockSpec((CHUNK, GLA_QK), lambda c: (c, 1)),
            pl.BlockSpec((CHUNK, GLA_V), lambda c: (c, 1)),
            pl.BlockSpec((CHUNK, GLA_QK), lambda c: (c, 0)),
        ],
        out_specs=[
            pl.BlockSpec((CHUNK, GLA_V), lambda c: (c, 0)),
            pl.BlockSpec((GLA_HEADS, 1, GLA_DV, GLA_DK), lambda c: (0, c, 0, 0)),
        ],
        out_shape=[jax.ShapeDtypeStruct((ROWS, GLA_V), F32),
                   jax.ShapeDtypeStruct((GLA_HEADS, N_CHUNKS, GLA_DV, GLA_DK), F32)],
        scratch_shapes=[pltpu.VMEM((GLA_HEADS, GLA_DV, GLA_DK), F32)],
        name="gla_scan_fwd", compiler_params=_params(("arbitrary",)),
    )(proj, proj, proj, garr)


def _gla_scan_bwd(proj, garr, states, do):
    last = N_CHUNKS - 1
    kl, vl = _head_lanes(GLA_DK, GLA_HEADS), _head_lanes(GLA_DV, GLA_HEADS)

    def body(q_ref, k_ref, v_ref, g_ref, s_ref, do_ref, dq_ref, dk_ref, dv_ref, dg_ref, dst):
        @pl.when(pl.program_id(0) == 0)
        def _():
            dst[...] = jnp.zeros(dst.shape, F32)

        _, vjp = jax.vjp(lambda q, k, v, g, s: _gla_chunk(q, k, v, g, s, _dot_ad),
                         [q_ref[:, ln] for ln in kl], [k_ref[:, ln] for ln in kl], [v_ref[:, ln] for ln in vl],
                         [g_ref[:, ln] for ln in kl], [s_ref[i, 0] for i in range(GLA_HEADS)])
        dq, dk, dv, dg, ds = vjp(([do_ref[:, ln] for ln in vl], [dst[i] for i in range(GLA_HEADS)]))
        for i in range(GLA_HEADS):
            dq_ref[:, kl[i]] = dq[i].astype(dq_ref.dtype)
            dk_ref[:, kl[i]] = dk[i].astype(dk_ref.dtype)
            dv_ref[:, vl[i]] = dv[i].astype(dv_ref.dtype)
            dg_ref[:, kl[i]] = dg[i]
            dst[i] = ds[i]

    kspec = pl.BlockSpec((CHUNK, GLA_QK), lambda c: (last - c, 0))
    vspec = pl.BlockSpec((CHUNK, GLA_V), lambda c: (last - c, 0))
    return pl.pallas_call(
        body, grid=(N_CHUNKS,),
        in_specs=[
            kspec,
            pl.BlockSpec((CHUNK, GLA_QK), lambda c: (last - c, 1)),
            pl.BlockSpec((CHUNK, GLA_V), lambda c: (last - c, 1)),
            kspec,
            pl.BlockSpec((GLA_HEADS, 1, GLA_DV, GLA_DK), lambda c: (0, last - c, 0, 0)),
            vspec,
        ],
        out_specs=[kspec, kspec, vspec, kspec],
        out_shape=[jax.ShapeDtypeStruct((ROWS, GLA_QK), BF16), jax.ShapeDtypeStruct((ROWS, GLA_QK), BF16),
                   jax.ShapeDtypeStruct((ROWS, GLA_V), BF16), jax.ShapeDtypeStruct((ROWS, GLA_QK), F32)],
        scratch_shapes=[pltpu.VMEM((GLA_HEADS, GLA_DV, GLA_DK), F32)],
        name="gla_scan_bwd", compiler_params=_params(("arbitrary",)),
    )(proj, proj, proj, garr, states, do)


def _gdn_scan_fwd(qn, kn, vn, gates):
    def body(q_ref, k_ref, v_ref, g_ref, o_ref, states_ref, st):
        @pl.when((pl.program_id(0) == 0) & (pl.program_id(1) == 0))
        def _():
            st[...] = jnp.zeros(st.shape, F32)

        heads = [pl.program_id(1) * GDN_GROUP + hh for hh in range(GDN_GROUP)]
        lanes = _head_lanes(GDN_DK, GDN_GROUP)
        s_in = [st[h] for h in heads]
        o, s_out = _gdn_chunk([q_ref[:, ln] for ln in lanes], [k_ref[:, ln] for ln in lanes],
                              [v_ref[:, ln] for ln in lanes], g_ref[...], s_in, heads, _dot, _solve)
        for hh in range(GDN_GROUP):
            states_ref[hh, 0] = s_in[hh]
            o_ref[:, lanes[hh]] = o[hh]
            st[heads[hh]] = s_out[hh]

    hspec = pl.BlockSpec((CHUNK, GDN_GROUP * GDN_DK), lambda c, h: (c, h))
    return pl.pallas_call(
        body, grid=(N_CHUNKS, GDN_HEADS // GDN_GROUP),
        in_specs=[hspec, hspec, hspec, pl.BlockSpec((CHUNK, LANE), lambda c, h: (c, 0))],
        out_specs=[hspec, pl.BlockSpec((GDN_GROUP, 1, GDN_DK, GDN_DV), lambda c, h: (h, c, 0, 0))],
        out_shape=[jax.ShapeDtypeStruct((ROWS, D_MODEL), F32),
                   jax.ShapeDtypeStruct((GDN_HEADS, N_CHUNKS, GDN_DK, GDN_DV), F32)],
        scratch_shapes=[pltpu.VMEM((GDN_HEADS, GDN_DK, GDN_DV), F32)],
        name="gdn_scan_fwd", compiler_params=_params(("arbitrary", "arbitrary")),
    )(qn, kn, vn, gates)


def _gdn_scan_bwd(qn, kn, vn, gates, states, do):
    last = N_CHUNKS - 1

    def body(q_ref, k_ref, v_ref, g_ref, s_ref, do_ref, dq_ref, dk_ref, dv_ref, dg_ref, dst):
        @pl.when((pl.program_id(0) == 0) & (pl.program_id(1) == 0))
        def _():
            dst[...] = jnp.zeros(dst.shape, F32)

        @pl.when(pl.program_id(1) == 0)
        def _():
            dg_ref[...] = jnp.zeros(dg_ref.shape, F32)

        heads = [pl.program_id(1) * GDN_GROUP + hh for hh in range(GDN_GROUP)]
        lanes = _head_lanes(GDN_DK, GDN_GROUP)
        _, vjp = jax.vjp(lambda q, k, v, g, s: _gdn_chunk(q, k, v, g, s, heads, _dot_ad, _solve_ad),
                         [q_ref[:, ln] for ln in lanes], [k_ref[:, ln] for ln in lanes],
                         [v_ref[:, ln] for ln in lanes], g_ref[...], [s_ref[hh, 0] for hh in range(GDN_GROUP)])
        dq, dk, dv, dg, ds = vjp(([do_ref[:, ln] for ln in lanes], [dst[h] for h in heads]))
        for hh in range(GDN_GROUP):
            dq_ref[:, lanes[hh]] = dq[hh]
            dk_ref[:, lanes[hh]] = dk[hh]
            dv_ref[:, lanes[hh]] = dv[hh]
            dst[heads[hh]] = ds[hh]
        dg_ref[...] += dg

    hspec = pl.BlockSpec((CHUNK, GDN_GROUP * GDN_DK), lambda c, h: (last - c, h))
    gspec = pl.BlockSpec((CHUNK, LANE), lambda c, h: (last - c, 0))
    return pl.pallas_call(
        body, grid=(N_CHUNKS, GDN_HEADS // GDN_GROUP),
        in_specs=[hspec, hspec, hspec, gspec,
                  pl.BlockSpec((GDN_GROUP, 1, GDN_DK, GDN_DV), lambda c, h: (h, last - c, 0, 0)), hspec],
        out_specs=[hspec, hspec, hspec, gspec],
        out_shape=[jax.ShapeDtypeStruct((ROWS, D_MODEL), F32)] * 3 + [jax.ShapeDtypeStruct((ROWS, LANE), F32)],
        scratch_shapes=[pltpu.VMEM((GDN_HEADS, GDN_DK, GDN_DV), F32)],
        name="gdn_scan_bwd", compiler_params=_params(("arbitrary", "arbitrary")),
    )(qn, kn, vn, gates, states, do)


def _gdn_conv_fwd(proj, w, kind, cb):
    fn = _gdn_conv_fn(kind)
    nt = ROWS // ROW_TILE

    def body(prev_ref, cur_ref, w_ref, o_ref):
        (out,) = fn(pl.program_id(0), prev_ref[...], cur_ref[...], w_ref[...])
        o_ref[...] = out

    return pl.pallas_call(
        body, grid=(nt,),
        in_specs=[pl.BlockSpec((ROW_TILE, D_MODEL), lambda i: (jnp.maximum(i - 1, 0), cb)),
                  pl.BlockSpec((ROW_TILE, D_MODEL), lambda i: (i, cb)),
                  pl.BlockSpec(w.shape, lambda i: (0, 0))],
        out_specs=pl.BlockSpec((ROW_TILE, D_MODEL), lambda i: (i, 0)),
        out_shape=jax.ShapeDtypeStruct((ROWS, D_MODEL), F32),
        name="gdn_conv_fwd_" + kind, compiler_params=_params(("parallel",)),
    )(proj, proj, w)


def _gdn_conv_bwd(proj, w, kind, cb, ct):
    fn = _gdn_conv_fn(kind)
    nt = ROWS // ROW_TILE

    def body(prev_ref, cur_ref, w_ref, ct_ref, dx_ref, dw_ref, carry):
        step = pl.program_id(0)
        i = nt - 1 - step

        @pl.when(step == 0)
        def _():
            carry[...] = jnp.zeros(carry.shape, F32)
            dw_ref[...] = jnp.zeros(dw_ref.shape, F32)

        _, vjp = jax.vjp(lambda p, c, ww: fn(i, p, c, ww), prev_ref[...], cur_ref[...], w_ref[...])
        dprev, dcur, dw = vjp((ct_ref[...],))
        dx_ref[...] = (dcur + carry[...]).astype(dx_ref.dtype)
        carry[...] = dprev
        dw_ref[...] += dw

    return pl.pallas_call(
        body, grid=(nt,),
        in_specs=[pl.BlockSpec((ROW_TILE, D_MODEL), lambda s: (jnp.maximum(nt - 2 - s, 0), cb)),
                  pl.BlockSpec((ROW_TILE, D_MODEL), lambda s: (nt - 1 - s, cb)),
                  pl.BlockSpec(w.shape, lambda s: (0, 0)),
                  pl.BlockSpec((ROW_TILE, D_MODEL), lambda s: (nt - 1 - s, 0))],
        out_specs=[pl.BlockSpec((ROW_TILE, D_MODEL), lambda s: (nt - 1 - s, 0)),
                   pl.BlockSpec(w.shape, lambda s: (0, 0))],
        out_shape=[jax.ShapeDtypeStruct((ROWS, D_MODEL), BF16), jax.ShapeDtypeStruct(w.shape, F32)],
        scratch_shapes=[pltpu.VMEM((ROW_TILE, D_MODEL), F32)],
        name="gdn_conv_bwd_" + kind, compiler_params=_params(("arbitrary",)),
    )(proj, proj, w, ct)


def _loss_head(h, target):
    nt = ROWS // ROW_TILE
    lead = PAD_ROWS // ROW_TILE

    def body(h_ref, t_ref, dh_ref, acc_ref):
        i = pl.program_id(0)

        @pl.when(i == 0)
        def _():
            acc_ref[...] = jnp.zeros(acc_ref.shape, F32)

        err = (h_ref[...] - t_ref[...]) * jnp.where(i >= lead, 1.0, 0.0)
        dh_ref[...] = err * (1.0 / D_MODEL)
        acc_ref[...] += jnp.sum(err * err, axis=0, keepdims=True)

    return pl.pallas_call(
        body, grid=(nt,),
        in_specs=[pl.BlockSpec((ROW_TILE, D_MODEL), lambda i: (i, 0)),
                  pl.BlockSpec((ROW_TILE, D_MODEL), lambda i: (jnp.maximum(i - lead, 0), 0))],
        out_specs=[pl.BlockSpec((ROW_TILE, D_MODEL), lambda i: (i, 0)),
                   pl.BlockSpec((1, D_MODEL), lambda i: (0, 0))],
        out_shape=[jax.ShapeDtypeStruct((ROWS, D_MODEL), F32), jax.ShapeDtypeStruct((1, D_MODEL), F32)],
        name="loss_head", compiler_params=_params(("arbitrary",)),
    )(h, target)


MESH = pl.DeviceIdType.MESH
ANY = pl.BlockSpec(memory_space=pl.ANY)


N_PEERS = N_DEV - 1


HBM_SPEC = pl.BlockSpec(memory_space=pltpu.HBM)
SEM_SPEC = pl.BlockSpec(memory_space=pltpu.SEMAPHORE)
SIDE_EFFECT = pltpu.SideEffectType.DATAFLOW_SIDE_EFFECTING


def _me_and_peers():
    x, y, c = lax.axis_index("x"), lax.axis_index("y"), lax.axis_index("c")
    peers = []
    for k in range(1, N_DEV):
        pid = (1 - x if k & 4 else x, 1 - y if k & 2 else y, 1 - c if k & 1 else c)
        peers.append((4 * pid[0] + 2 * pid[1] + pid[2], pid))
    return 4 * x + 2 * y + c, peers


def _split_copies(src_refs, land_refs, send_sems, recv_sems, spread):
    me, peers = _me_and_peers()
    out = []
    for t, (src, land) in enumerate(zip(src_refs, land_refs)):
        for k, (peer, pid) in enumerate(peers):
            sems = dict(send_sem=send_sems.at[N_PEERS * t + k], recv_sem=recv_sems.at[N_PEERS * t + k],
                        device_id=pid, device_id_type=MESH)
            piece = src if spread else src.at[peer]
            out.append((pltpu.make_async_remote_copy(src_ref=piece, dst_ref=land.at[me], **sems),
                        pltpu.make_async_remote_copy(src_ref=piece, dst_ref=land.at[peer], **sems)))
    return out


def _send_start(srcs, spread, name):
    n = len(srcs)
    lands = [lax.empty((N_DEV,) + (s.shape if spread else s.shape[1:]), s.dtype) for s in srcs]

    def body(*refs):
        for out, _ in _split_copies(refs[:n], refs[n:2 * n], refs[2 * n], refs[2 * n + 1], spread):
            out.start()
        refs[-1][...] = jnp.zeros(refs[-1].shape, F32)

    thru = [pltpu.HBM(a.shape, a.dtype) for a in list(srcs) + lands]
    res = pl.pallas_call(
        body, name=name,
        out_shape=(pltpu.SemaphoreType.DMA((N_PEERS * n,)), pltpu.SemaphoreType.DMA((N_PEERS * n,)), *thru,
                   jax.ShapeDtypeStruct((8, LANE), F32)),
        in_specs=[HBM_SPEC] * (2 * n),
        out_specs=(SEM_SPEC, SEM_SPEC, *[HBM_SPEC] * (2 * n), pl.BlockSpec(memory_space=pltpu.VMEM)),
        input_output_aliases={t: 2 + t for t in range(2 * n)},
        compiler_params=pltpu.CompilerParams(has_side_effects=SIDE_EFFECT),
    )(*[pltpu.with_memory_space_constraint(a, pltpu.HBM) for a in list(srcs) + lands])
    return res[0], res[1], list(res[2:2 + n]), list(res[2 + n:2 + 2 * n]), res[-1]


def _send_wait(started, spread, after, name):
    send_sems, recv_sems, srcs, lands, _ = started
    n = len(srcs)

    def body(*refs):
        for out, arrival in _split_copies(refs[:n], refs[n:2 * n], refs[2 * n], refs[2 * n + 1], spread):
            out.wait_send()
            arrival.wait_recv()

    thru = [pltpu.HBM(a.shape, a.dtype) for a in srcs + lands]
    res = pl.pallas_call(
        body, name=name, out_shape=tuple(thru),
        in_specs=[HBM_SPEC] * (2 * n) + [SEM_SPEC, SEM_SPEC, ANY], out_specs=tuple([HBM_SPEC] * (2 * n)),
        input_output_aliases={t: t for t in range(2 * n)},
        compiler_params=pltpu.CompilerParams(has_side_effects=SIDE_EFFECT),
    )(*srcs, *lands, send_sems, recv_sems, after)
    return list(res[n:])


def _adamw(recvs, w, m, v, name):
    n_layers = len(recvs)
    rows, cols = recvs[0].shape[1:]
    tr = _pick(rows, (256, 128, 64, 32, 16))
    nb = rows // tr

    def body(*refs):
        r_refs = refs[:n_layers]
        w_ref, m_ref, v_ref, g_ref, d_ref, m2_ref, v2_ref = refs[n_layers:]

        def update(r_ref):
            g = r_ref[0].astype(F32)
            for k in range(1, N_DEV):
                g = g + r_ref[k].astype(F32)
            m2 = ADAM_B1 * m_ref[...] + (1.0 - ADAM_B1) * g
            v2 = ADAM_B2 * v_ref[...] + (1.0 - ADAM_B2) * (g * g)
            m_hat = m2 / (1.0 - ADAM_B1 ** ADAM_STEP)
            v_hat = v2 / (1.0 - ADAM_B2 ** ADAM_STEP)
            g_ref[...] = g
            d_ref[...] = -ADAM_LR * (m_hat / (jnp.sqrt(v_hat) + ADAM_EPS) + ADAM_WD * w_ref[...])
            m2_ref[...] = m2
            v2_ref[...] = v2

        if n_layers == 1:
            update(r_refs[0])
        else:
            for layer in range(n_layers):
                pl.when(pl.program_id(0) == layer)(functools.partial(update, r_refs[layer]))

    def recv_spec(layer):
        return pl.BlockSpec((N_DEV, tr, cols), lambda a, i: (0, jnp.where(a == layer, i, 0), 0))

    spec = pl.BlockSpec((tr, cols), lambda a, i: (a * nb + i, 0))
    return pl.pallas_call(
        body, grid=(n_layers, nb),
        in_specs=[recv_spec(layer) for layer in range(n_layers)] + [spec, spec, spec],
        out_specs=[spec] * 4, out_shape=[jax.ShapeDtypeStruct(w.shape, F32)] * 4,
        name=name, compiler_params=_params(("arbitrary", "arbitrary")),
    )(*recvs, w, m, v)


def _padded_rows(n, mult):
    rows = -(-n // PACK_COLS)
    return -(-rows // mult) * mult


def _pack(flat_pieces, mult, dtype, lead=()):
    flat = jnp.concatenate([p.astype(dtype) for p in flat_pieces], axis=-1)
    n = flat.shape[-1]
    rows = _padded_rows(n, mult)
    flat = jnp.pad(flat, [(0, 0)] * len(lead) + [(0, rows * PACK_COLS - n)])
    return flat.reshape(lead + (rows, PACK_COLS))


def _unpack(packed, shapes):
    flat = packed.reshape(-1)
    out, off = {}, 0
    for name, shape in shapes:
        size = 1
        for s in shape:
            size *= s
        out[name] = flat[off:off + size].reshape(shape)
        off += size
    return out


def _local_2d(arr):
    return arr.reshape(-1, arr.shape[-1])


SHARD_2D = {n: (s[-2] if len(s) > 2 else s[0], s[-1], a == len(s) - 1) for n, s, a in SHARDED}
SHARD_2D["gdn_conv_w"] = (GDN_CONV, GDN_CONV_DIM // N_DEV, True)


def _wire(name):
    rows, cols, _ = SHARD_2D[name]
    return BF16 if rows * cols >= 1 << 16 else F32


def _layer_tensors(i):
    kind, j = i % 3, i // 3
    pre = ("fox", "gla", "gdn")[kind]
    mix = [(pre + "_w_in", j), (pre + "_w_out", j)]
    if kind == 1:
        mix.append(("gla_w_alpha2", j))
    if kind == 2:
        mix.append(("gdn_conv_w", j))
    if i == 0:
        mix.append(("meta_tokens", None))
    return {"mix": mix, "ffn": [("w_gate_up", i), ("w_down", i)]}


PARTS = ("mix", "ffn")


def _layer_block(arr, idx):
    return arr if idx is None else _local_2d(arr[idx])


def _to_slabs(full2d, name):
    rows, cols, by_cols = SHARD_2D[name]
    if by_cols:
        parts = full2d if isinstance(full2d, tuple) else (full2d,)
        per = parts[0].shape[1] // cols
        return jnp.stack([parts[d // per][:, cols * (d % per):cols * (d % per + 1)] for d in range(N_DEV)])
    return full2d.reshape(N_DEV, rows, cols)


def _from_gathered(g, name):
    rows, cols, by_cols = SHARD_2D[name]
    if by_cols:
        return jnp.concatenate([g[d] for d in range(N_DEV)], axis=1)
    return g.reshape(N_DEV * rows, cols)


def _pad_cols(w, total):
    return jnp.pad(w, [(0, 0)] * (w.ndim - 1) + [(0, total - w.shape[-1])])


def _pad_lanes(v):
    return _pad_cols(v, LANE)


def _x(arr, width, cb0=0, moves=0):
    return (arr, width, cb0, moves)


def _rms_fwd(h, g):
    return _row_fwd(_rms_fn, [_x(h, D_MODEL)], [g], [], [(D_MODEL, D_MODEL, BF16)], "rms_fwd")[0]


def _rms_bwd(h, g, dy, add):
    dh, dg = _row_bwd(_rms_fn, [_x(h, D_MODEL)], [g], [], [_x(dy, D_MODEL)], [(D_MODEL, F32)], "rms_bwd", adds=[add])
    return dh, dg


SW_WIDTH = 256
SW_NCOL = D_FF // SW_WIDTH


def _ffn_gate_up(y, wgu):
    m, k = y.shape
    tm = _pick(m, (1408, 384, 128))

    def body(y_ref, wg_ref, wu_ref, a_ref, g_ref, u_ref):
        yv = y_ref[...]
        gt = _dot(yv, wg_ref[...], "nn")
        up = _dot(yv, wu_ref[...], "nn")
        a_ref[...] = (_silu(gt) * up).astype(a_ref.dtype)
        g_ref[...] = gt.astype(g_ref.dtype)
        u_ref[...] = up.astype(u_ref.dtype)

    tile = pl.BlockSpec((tm, SW_WIDTH), lambda i, j: (i, j))
    return pl.pallas_call(
        body, grid=(m // tm, SW_NCOL),
        in_specs=[pl.BlockSpec((tm, k), lambda i, j: (i, 0)),
                  pl.BlockSpec((k, SW_WIDTH), lambda i, j: (0, j)),
                  pl.BlockSpec((k, SW_WIDTH), lambda i, j: (0, SW_NCOL + j))],
        out_specs=[tile] * 3, out_shape=[jax.ShapeDtypeStruct((m, D_FF), BF16)] * 3,
        name="ffn_gate_up", compiler_params=_params(("parallel", "parallel")),
    )(y, wgu, wgu)


def _ffn_down_dx(dh2, wd, gt, up):
    m, k = dh2.shape
    tm = _pick(m, (1408, 384, 128))

    def body(d_ref, w_ref, g_ref, u_ref, dg_ref, du_ref):
        da = _dot(d_ref[...], w_ref[...], "nt")
        gv, uv = g_ref[...].astype(F32), u_ref[...].astype(F32)
        s = jax.nn.sigmoid(gv)
        dg_ref[...] = (da * uv * (s + gv * s * (1.0 - s))).astype(dg_ref.dtype)
        du_ref[...] = (da * gv * s).astype(du_ref.dtype)

    tile = pl.BlockSpec((tm, SW_WIDTH), lambda i, j: (i, j))
    return pl.pallas_call(
        body, grid=(m // tm, SW_NCOL),
        in_specs=[pl.BlockSpec((tm, k), lambda i, j: (i, 0)),
                  pl.BlockSpec((SW_WIDTH, k), lambda i, j: (j, 0)), tile, tile],
        out_specs=[tile] * 2, out_shape=[jax.ShapeDtypeStruct((m, D_FF), BF16)] * 2,
        name="ffn_down_dx", compiler_params=_params(("parallel", "parallel")),
    )(dh2, wd, gt, up)


def _ffn_fwd(h, g, wgu, wd):
    y = _rms_fwd(h, g)
    a, gt, up = _ffn_gate_up(y, wgu)
    h2 = _matmul(a, wd, "nn", F32, "ffn_down", add=h)
    return h2, (h, y, gt, up, a)


def _ffn_bwd(dh2, saved, g, wgu, wd):
    h, y, gt, up, a = saved
    dgt, dup = _ffn_down_dx(dh2, wd, gt, up)
    dwd = _matmul(a, dh2, "tn", F32, "ffn_down_dw")
    dwg = _matmul(y, dgt, "tn", F32, "ffn_gate_dw")
    dwu = _matmul(y, dup, "tn", F32, "ffn_up_dw")
    dy = _matmul(dgt, wgu, "nt", F32, "ffn_gate_dx", b_kblock=0)
    dy = _matmul(dup, wgu, "nt", F32, "ffn_up_dx", add=dy, b_kblock=1)
    dh, dg = _rms_bwd(h, g, dy, dh2)
    return dh, dg, (dwg, dwu), dwd


def _fox_consts():
    lane = jnp.arange(D_MODEL) // FOX_DH
    e_expand = (jnp.arange(LANE)[:, None] == lane[None, :]).astype(F32)
    e_mean = e_expand.T * (1.0 / FOX_DH)
    half = jnp.arange(LANE) // FOX_DH
    blockdiag = (half[:, None] == half[None, :]).astype(F32)
    return e_mean, e_expand, blockdiag


def _pair_layout(c):
    heads = c[:, :FOX_HEADS].T
    crow = heads.reshape(FOX_PAIRS, 2, ROWS)
    c2 = jnp.broadcast_to(crow.transpose(0, 2, 1)[:, :, :, None], (FOX_PAIRS, ROWS, 2, FOX_DH))
    crow = jnp.where(jnp.arange(ROWS) >= FIRST_ROW, crow, KEY_OFF)
    return c2.reshape(FOX_PAIRS, ROWS, LANE), crow


def _fox_fwd(y, p):
    proj = _matmul(y, p["w_in"], "nn", F32, "fox_in")
    e_mean, e_expand, _ = _fox_consts()
    nb = D_MODEL // LANE
    xs = [_x(proj, D_MODEL, 0), _x(proj, D_MODEL, 1), _x(proj, LANE, 4 * nb)]
    ps = [p["q_gain"], p["k_gain"], p["b_f"]]
    qn, kn, lf = _row_fwd(_fox_pre_fn, xs, ps, [e_mean, e_expand],
                          [(D_MODEL, D_MODEL, BF16), (D_MODEL, D_MODEL, BF16), (LANE, LANE, F32)], "fox_pre_fwd")
    c = _seq_cumsum(lf, False, "fox_cumsum")
    c2, crow = _pair_layout(c)
    o, lse2 = _fox_attn_fwd(qn, kn, proj, c2, crow)
    om = _row_fwd(_fox_post_fn, [_x(o, D_MODEL, 0), _x(proj, D_MODEL, 3)], [], [],
                  [(D_MODEL, D_MODEL, BF16)], "fox_post_fwd")[0]
    return om, (proj, qn, kn, c2, crow, o, lse2)


def _fox_bwd(dom, y, saved, p):
    proj, qn, kn, c2, crow, o, lse2 = saved
    e_mean, e_expand, blockdiag = _fox_consts()
    nb = D_MODEL // LANE
    do, dgate = _row_bwd(_fox_post_fn, [_x(o, D_MODEL, 0), _x(proj, D_MODEL, 3)], [], [], [_x(dom, D_MODEL)],
                         [(D_MODEL, F32), (D_MODEL, BF16)], "fox_post_bwd")
    dq, delta2, dcq2 = _fox_attn_dq(qn, kn, proj, c2, crow, lse2, o, do, blockdiag)
    dk, dv, dcrow = _fox_attn_dkv(qn, kn, proj, c2, crow, lse2, delta2, do)
    dlf = _seq_cumsum(_pad_cols(dcrow.reshape(FOX_HEADS, ROWS).T, LANE), True, "fox_cumsum_rev", pairs=dcq2)
    xs = [_x(proj, D_MODEL, 0), _x(proj, D_MODEL, 1), _x(proj, LANE, 4 * nb)]
    ps = [p["q_gain"], p["k_gain"], p["b_f"]]
    dqr, dkr, df, dqg, dkg, dbf = _row_bwd(
        _fox_pre_fn, xs, ps, [e_mean, e_expand], [_x(dq, D_MODEL), _x(dk, D_MODEL), _x(dlf, LANE)],
        [(D_MODEL, BF16), (D_MODEL, BF16), (LANE, BF16)], "fox_pre_bwd")
    dproj = jnp.concatenate([dqr, dkr, dv, dgate, df], axis=1)
    grads = {
        "w_in": _matmul(y, dproj, "tn", F32, "fox_in_dw")[:, :FOX_IN],
        "q_gain": dqg.reshape(FOX_HEADS, FOX_DH).sum(axis=0),
        "k_gain": dkg.reshape(FOX_HEADS, FOX_DH).sum(axis=0),
        "b_f": dbf[0, :FOX_HEADS],
    }
    return _matmul(dproj, p["w_in"], "nt", F32, "fox_in_dx"), grads


def _gla_fwd(y, p):
    proj = _matmul(y, p["w_in"], "nn", F32, "gla_in")
    small = (2 * GLA_QK + 2 * GLA_V) // LANE
    garr = _row_fwd(_gla_pre_fn, [_x(proj, LANE, small)], [p["w_alpha"], p["b_alpha"]], [],
                    [(GLA_QK, GLA_QK, F32)], "gla_pre_fwd")[0]
    o, states = _gla_scan_fwd(proj, garr)
    om = _row_fwd(_gla_post_fn, [_x(o, GLA_V, 0), _x(proj, GLA_V, 2)], [p["o_gain"]], [],
                  [(GLA_V, GLA_V, BF16)], "gla_post_fwd")[0]
    return om, (proj, garr, states, o)


def _gla_bwd(dom, y, saved, p):
    proj, garr, states, o = saved
    small = (2 * GLA_QK + 2 * GLA_V) // LANE
    do, dr, dgain = _row_bwd(_gla_post_fn, [_x(o, GLA_V, 0), _x(proj, GLA_V, 2)], [p["o_gain"]], [],
                             [_x(dom, GLA_V)], [(GLA_V, F32), (GLA_V, BF16)], "gla_post_bwd")
    dq, dk, dv, dg = _gla_scan_bwd(proj, garr, states, do)
    dalr, dwa, dba = _row_bwd(_gla_pre_fn, [_x(proj, LANE, small)], [p["w_alpha"], p["b_alpha"]], [],
                              [_x(dg, GLA_QK)], [(LANE, BF16)], "gla_pre_bwd")
    dproj = jnp.concatenate([dq, dk, dv, dr, dalr], axis=1)
    grads = {
        "w_in": _matmul(y, dproj, "tn", F32, "gla_in_dw")[:, :GLA_IN],
        "w_alpha": dwa[:GLA_RANK], "b_alpha": dba[0], "o_gain": dgain[0],
    }
    return _matmul(dproj, p["w_in"], "nt", F32, "gla_in_dx"), grads


def _gdn_fwd(y, p):
    proj = _matmul(y, p["w_in"], "nn", F32, "gdn_in")
    nb = D_MODEL // LANE
    qn = _gdn_conv_fwd(proj, p["conv_q"], "q", 0)
    kn = _gdn_conv_fwd(proj, p["conv_k"], "k", 1)
    vn = _gdn_conv_fwd(proj, p["conv_v"], "v", 2)
    gates = _row_fwd(_gdn_gates_fn, [_x(proj, LANE, 4 * nb)], [p["a_log"], p["dt_bias"]], [],
                     [(LANE, LANE, F32)], "gdn_gates_fwd")[0]
    o, states = _gdn_scan_fwd(qn, kn, vn, gates)
    om = _row_fwd(_gdn_post_fn, [_x(o, D_MODEL, 0), _x(proj, D_MODEL, 3)], [p["o_gain"]], [],
                  [(D_MODEL, D_MODEL, BF16)], "gdn_post_fwd")[0]
    return om, (proj, qn, kn, vn, gates, states, o)


def _gdn_bwd(dom, y, saved, p):
    proj, qn, kn, vn, gates, states, o = saved
    nb = D_MODEL // LANE
    do, dgate, dgain = _row_bwd(_gdn_post_fn, [_x(o, D_MODEL, 0), _x(proj, D_MODEL, 3)], [p["o_gain"]], [],
                                [_x(dom, D_MODEL)], [(D_MODEL, F32), (D_MODEL, BF16)], "gdn_post_bwd")
    dqn, dkn, dvn, dgates = _gdn_scan_bwd(qn, kn, vn, gates, states, do)
    dsm, dalog, ddtb = _row_bwd(_gdn_gates_fn, [_x(proj, LANE, 4 * nb)], [p["a_log"], p["dt_bias"]], [],
                                [_x(dgates, LANE)], [(LANE, BF16)], "gdn_gates_bwd")
    dxq, dwq = _gdn_conv_bwd(proj, p["conv_q"], "q", 0, dqn)
    dxk, dwk = _gdn_conv_bwd(proj, p["conv_k"], "k", 1, dkn)
    dxv, dwv = _gdn_conv_bwd(proj, p["conv_v"], "v", 2, dvn)
    dproj = jnp.concatenate([dxq, dxk, dxv, dgate, dsm], axis=1)
    grads = {
        "w_in": _matmul(y, dproj, "tn", F32, "gdn_in_dw")[:, :GDN_IN],
        "conv_w": jnp.concatenate([dwq[:GDN_CONV], dwk[:GDN_CONV], dwv[:GDN_CONV]], axis=1),
        "a_log": dalog[0, :GDN_HEADS], "dt_bias": ddtb[0, :GDN_HEADS], "o_gain": dgain[0],
    }
    return _matmul(dproj, p["w_in"], "nt", F32, "gdn_in_dx"), grads


_MIXERS = ((_fox_fwd, _fox_bwd), (_gla_fwd, _gla_bwd), (_gdn_fwd, _gdn_bwd))


def _mixer_params(i, wts, rep):
    kind, j = i % 3, i // 3
    if kind == 0:
        return {
            "w_in": _pad_cols(wts["fox_w_in"], FOX_IN_PAD), "w_out": wts["fox_w_out"],
            "q_gain": jnp.tile(rep["fox_q_gain"][j], FOX_HEADS)[None],
            "k_gain": jnp.tile(rep["fox_k_gain"][j], FOX_HEADS)[None],
            "b_f": _pad_lanes(rep["fox_b_f"][j][None]),
        }
    if kind == 1:
        return {
            "w_in": _pad_cols(wts["gla_w_in"], GLA_IN_PAD), "w_out": wts["gla_w_out"],
            "w_alpha": jnp.pad(wts["gla_w_alpha2"], ((0, LANE - GLA_RANK), (0, 0))).astype(F32),
            "b_alpha": rep["gla_b_alpha"][j][None], "o_gain": rep["gla_o_gain"][j][None],
        }
    conv = jnp.pad(wts["gdn_conv_w"].astype(F32), ((0, 8 - GDN_CONV), (0, 0)))
    return {
        "w_in": _pad_cols(wts["gdn_w_in"], GDN_IN_PAD), "w_out": wts["gdn_w_out"],
        "conv_q": conv[:, :D_MODEL], "conv_k": conv[:, D_MODEL:2 * D_MODEL], "conv_v": conv[:, 2 * D_MODEL:],
        "a_log": _pad_lanes(rep["gdn_a_log"][j][None]), "dt_bias": _pad_lanes(rep["gdn_dt_bias"][j][None]),
        "o_gain": rep["gdn_o_gain"][j][None],
    }


def kernel(x, meta_tokens, norm_mix, norm_ffn, w_gate_up, w_down, fox_w_in, fox_b_f, fox_q_gain, fox_k_gain, fox_w_out, gla_w_in, gla_w_alpha2, gla_b_alpha, gla_o_gain, gla_w_out, gdn_w_in, gdn_conv_w, gdn_a_log, gdn_dt_bias, gdn_o_gain, gdn_w_out, loss_target, m_meta_tokens, m_norm_mix, m_norm_ffn, m_w_gate_up, m_w_down, m_fox_w_in, m_fox_b_f, m_fox_q_gain, m_fox_k_gain, m_fox_w_out, m_gla_w_in, m_gla_w_alpha2, m_gla_b_alpha, m_gla_o_gain, m_gla_w_out, m_gdn_w_in, m_gdn_conv_w, m_gdn_a_log, m_gdn_dt_bias, m_gdn_o_gain, m_gdn_w_out, v_meta_tokens, v_norm_mix, v_norm_ffn, v_w_gate_up, v_w_down, v_fox_w_in, v_fox_b_f, v_fox_q_gain, v_fox_k_gain, v_fox_w_out, v_gla_w_in, v_gla_w_alpha2, v_gla_b_alpha, v_gla_o_gain, v_gla_w_out, v_gdn_w_in, v_gdn_conv_w, v_gdn_a_log, v_gdn_dt_bias, v_gdn_o_gain, v_gdn_w_out):
    given = dict(locals())
    w_loc = {n: given[n] for n in WEIGHT_ORDER}
    m_loc = {n: given["m_" + n] for n in WEIGHT_ORDER}
    v_loc = {n: given["v_" + n] for n in WEIGHT_ORDER}

    rep = {n: w_loc[n] for n, _ in REPLICATED}
    me = 4 * lax.axis_index("x") + 2 * lax.axis_index("y") + lax.axis_index("c")

    def with_own(lands, owns):
        return [lax.dynamic_update_slice(land, own[None], (me, 0, 0)) for land, own in zip(lands, owns)]

    gathers = {}
    order = jnp.zeros((), F32)
    for i in range(DEPTH):
        groups = _layer_tensors(i)
        for part in PARTS:
            blocks = [(_layer_block(w_loc[n], idx) + order).astype(_wire(n)) for n, idx in groups[part]]
            started = _send_start(blocks, True, "gather_start_%d_%s" % (i, part))
            gathers[(i, part)] = (groups[part], blocks, started)
            order = started[4][0, 0]
    all_started = gathers[(DEPTH - 1, PARTS[-1])][2][4]

    def weights(i, part, after):
        tensors, blocks, started = gathers[(i, part)]
        lands = with_own(_send_wait(started, True, after, "gather_wait_%d_%s" % (i, part)), blocks)
        return {n: _from_gathered(g, n) for (n, _), g in zip(tensors, lands)}

    h = None
    saved = []
    for i in range(DEPTH):
        w_mix = weights(i, "mix", all_started if i == 0 else h)
        if i == 0:
            h = jnp.concatenate([jnp.zeros((FIRST_ROW, D_MODEL), F32), w_mix["meta_tokens"], x[0]], axis=0)
        p = _mixer_params(i, w_mix, rep)
        fwd, _ = _MIXERS[i % 3]
        g_mix = rep["norm_mix"][i][None]
        y = _rms_fwd(h, g_mix)
        om, mix_saved = fwd(y, p)
        h_mid = _matmul(om, p["w_out"], "nn", F32, "mixer_out", add=h)
        w_ffn = weights(i, "ffn", h_mid)
        g_ffn = rep["norm_ffn"][i][None]
        h_out, ffn_saved = _ffn_fwd(h_mid, g_ffn, w_ffn["w_gate_up"], w_ffn["w_down"])
        saved.append((p, w_ffn, h, y, om, mix_saved, ffn_saved))
        h = h_out

    dh, sq = _loss_head(h, loss_target[0])
    loss = lax.psum(0.5 * jnp.sum(sq) * (1.0 / D_MODEL), ("x", "y", "c"))

    gw = {n: [None] * DEPTH for n in ("norm_mix", "norm_ffn")}
    gm = {}
    reduces = {}

    def start_reduce(i, part, full, extra=()):
        tensors = _layer_tensors(i)[part]
        slabs = [_to_slabs(full[n], n).astype(_wire(n)) for n, _ in tensors] + list(extra)
        started = _send_start(slabs, False, "reduce_start_%d_%s" % (i, part))
        reduces[(i, part)] = (tensors, slabs, started)
        return started[4][0, 0]

    order = jnp.zeros((), BF16)
    for i in reversed(range(DEPTH)):
        p, w_ffn, h_in, y, om, mix_saved, ffn_saved = saved[i]
        kind, j = i % 3, i // 3
        pre = ("fox", "gla", "gdn")[kind]
        _, bwd = _MIXERS[kind]
        dh_mid, gw["norm_ffn"][i], dwgu, dwd = _ffn_bwd(
            dh, ffn_saved, rep["norm_ffn"][i][None], w_ffn["w_gate_up"], w_ffn["w_down"] + order)
        order = start_reduce(i, "ffn", {"w_gate_up": dwgu, "w_down": dwd}).astype(BF16)
        dom = _matmul(dh_mid, p["w_out"] + order, "nt", F32, "mixer_out_dx")
        dwo = _matmul(om, dh_mid, "tn", F32, "mixer_out_dw")
        dy, grads = bwd(dom, y, mix_saved, p)
        gm[(kind, j)] = grads
        dh, gw["norm_mix"][i] = _rms_bwd(h_in, rep["norm_mix"][i][None], dy, dh_mid)
        mix_full = {pre + "_w_in": grads["w_in"], pre + "_w_out": dwo, "gla_w_alpha2": grads.get("w_alpha"),
                    "gdn_conv_w": grads.get("conv_w"), "meta_tokens": dh[FIRST_ROW:PAD_ROWS]}
        if i > 0:
            order = start_reduce(i, "mix", mix_full).astype(BF16)

    partial = {
        "norm_mix": jnp.concatenate(gw["norm_mix"], axis=0), "norm_ffn": jnp.concatenate(gw["norm_ffn"], axis=0),
        "fox_b_f": jnp.stack([gm[(0, 0)]["b_f"], gm[(0, 1)]["b_f"]]),
        "fox_q_gain": jnp.stack([gm[(0, 0)]["q_gain"], gm[(0, 1)]["q_gain"]]),
        "fox_k_gain": jnp.stack([gm[(0, 0)]["k_gain"], gm[(0, 1)]["k_gain"]]),
        "gla_b_alpha": gm[(1, 0)]["b_alpha"][None], "gla_o_gain": gm[(1, 0)]["o_gain"][None],
        "gdn_a_log": gm[(2, 0)]["a_log"][None], "gdn_dt_bias": gm[(2, 0)]["dt_bias"][None],
        "gdn_o_gain": gm[(2, 0)]["o_gain"][None],
    }

    small = _pack([partial[n].reshape(-1) for n, _ in REPLICATED], 8, F32)
    start_reduce(0, "mix", mix_full, extra=[jnp.broadcast_to(small[None], (N_DEV,) + small.shape)])
    recv = {}
    for (i, part), (tensors, slabs, started) in reduces.items():
        owns = [lax.dynamic_index_in_dim(s, me, 0, keepdims=False) for s in slabs]
        lands = with_own(_send_wait(started, False, dh, "reduce_wait_%d_%s" % (i, part)), owns)
        for (n, idx), land in zip(tensors, lands):
            recv[(n, idx)] = land
        if (i, part) == (0, "mix"):
            recv_small = lands[-1]

    res = [{} for _ in range(4)]
    for n, s, _ in SHARDED:
        layers = [idx for i in range(DEPTH) for part in PARTS for (nn, idx) in _layer_tensors(i)[part] if nn == n]
        layers = sorted(layers, key=lambda idx: -1 if idx is None else idx)
        outs = _adamw([recv[(n, idx)] for idx in layers], *[_local_2d(d[n]) for d in (w_loc, m_loc, v_loc)],
                      "adamw_" + n)
        for k in range(4):
            res[k][n] = outs[k].reshape(s)
    outs_small = _adamw([recv_small], *[_pack([d[n].reshape(-1) for n, _ in REPLICATED], 8, F32)
                                        for d in (w_loc, m_loc, v_loc)], "adamw_replicated")
    for k in range(4):
        res[k].update(_unpack(outs_small[k], list(REPLICATED)))

    result = [loss, dh[PAD_ROWS:][None]]
    for k in range(4):
        result += [res[k][n] for n in WEIGHT_ORDER]
    return tuple(result)
```

```python
import functools

import jax
import jax.numpy as jnp
from jax import lax
from jax.experimental import pallas as pl
from jax.experimental.pallas import tpu as pltpu

F32, BF16 = jnp.float32, jnp.bfloat16
HIGHEST = lax.Precision.HIGHEST

D_MODEL = 1024
SEQ = 4096
N_META = 16
DEPTH = 4
NORM_EPS = 1e-6
LANE = 128
PAD_ROWS = 128
FIRST_ROW = PAD_ROWS - N_META
ROWS = PAD_ROWS + SEQ
ROW_TILE = 128
CHUNK = 64
NEG = -1e30
N_DEV = 8
VMEM_LIMIT = 56 * 1024 * 1024
VMEM_BLOCK_BUDGET = 36 * 1024 * 1024

FOX_HEADS, FOX_DH = 16, 64
FOX_IN, FOX_IN_PAD = 4 * D_MODEL + FOX_HEADS, 4 * D_MODEL + LANE
GLA_HEADS, GLA_DK, GLA_DV, GLA_RANK = 4, 128, 256, 16
GLA_QK, GLA_V = GLA_HEADS * GLA_DK, GLA_HEADS * GLA_DV
GLA_IN, GLA_IN_PAD = 2 * GLA_QK + 2 * GLA_V + GLA_RANK, 2 * GLA_QK + 2 * GLA_V + LANE
GDN_HEADS, GDN_DK, GDN_DV, GDN_CONV = 8, 128, 128, 4
GDN_CONV_DIM = 3 * GDN_HEADS * GDN_DK
GDN_IN, GDN_IN_PAD = GDN_CONV_DIM + GDN_HEADS * GDN_DV + 2 * GDN_HEADS, GDN_CONV_DIM + GDN_HEADS * GDN_DV + LANE
D_FF = 2816

ADAM_LR, ADAM_B1, ADAM_B2, ADAM_EPS, ADAM_WD, ADAM_STEP = 0.001, 0.9, 0.999, 1e-08, 0.01, 10

PACK_COLS = 1024
PACK_ROW_TILE = 256

SHARDED = (
    ("meta_tokens", (16, 128), 1),
    ("w_gate_up", (4, 1024, 704), 2),
    ("w_down", (4, 352, 1024), 1),
    ("fox_w_in", (2, 1024, 514), 2),
    ("fox_w_out", (2, 128, 1024), 1),
    ("gla_w_in", (1, 1024, 386), 2),
    ("gla_w_alpha2", (1, 16, 64), 2),
    ("gla_w_out", (1, 128, 1024), 1),
    ("gdn_w_in", (1, 1024, 514), 2),
    ("gdn_conv_w", (1, 4, 1, 384), 3),
    ("gdn_w_out", (1, 128, 1024), 1),
)
REPLICATED = (
    ("norm_mix", (4, 1024)),
    ("norm_ffn", (4, 1024)),
    ("fox_b_f", (2, 16)),
    ("fox_q_gain", (2, 64)),
    ("fox_k_gain", (2, 64)),
    ("gla_b_alpha", (1, 512)),
    ("gla_o_gain", (1, 256)),
    ("gdn_a_log", (1, 8)),
    ("gdn_dt_bias", (1, 8)),
    ("gdn_o_gain", (1, 128)),
)
WEIGHT_ORDER = (
    "meta_tokens", "norm_mix", "norm_ffn", "w_gate_up", "w_down", "fox_w_in", "fox_b_f", "fox_q_gain",
    "fox_k_gain", "fox_w_out", "gla_w_in", "gla_w_alpha2", "gla_b_alpha", "gla_o_gain", "gla_w_out",
    "gdn_w_in", "gdn_conv_w", "gdn_a_log", "gdn_dt_bias", "gdn_o_gain", "gdn_w_out",
)

_DN = {
    "nn": (((1,), (0,)), ((), ())),
    "nt": (((1,), (1,)), ((), ())),
    "tn": (((0,), (0,)), ((), ())),
}


def _dot(a, b, kind):
    return lax.dot_general(a.astype(BF16), b.astype(BF16), _DN[kind], preferred_element_type=F32)


@functools.partial(jax.custom_vjp, nondiff_argnums=(2,))
def _dot_ad(a, b, kind):
    return _dot(a, b, kind)


def _dot_ad_fwd(a, b, kind):
    return _dot(a, b, kind), (a, b)


def _dot_ad_bwd(kind, res, ct):
    a, b = res
    if kind == "nn":
        return _dot(ct, b, "nt"), _dot(a, ct, "tn")
    if kind == "nt":
        return _dot(ct, b, "nn"), _dot(ct, a, "tn")
    return _dot(b, ct, "nt"), _dot(a, ct, "nn")


_dot_ad.defvjp(_dot_ad_fwd, _dot_ad_bwd)


def _dot_hi(a, b, kind):
    return lax.dot_general(a, b, _DN[kind], precision=HIGHEST, preferred_element_type=F32)


def _dot_mid(a, b, kind):
    return lax.dot_general(a, b, _DN[kind], precision=lax.Precision.HIGH, preferred_element_type=F32)


def _iota(shape, dim):
    return lax.broadcasted_iota(jnp.int32, shape, dim)


def _row_mask(i, tm):
    return ((i * tm + _iota((tm, 1), 0)) >= FIRST_ROW).astype(F32)


def _log_sigmoid(z):
    return jnp.minimum(z, 0.0) - jnp.log(1.0 + jnp.exp(-jnp.abs(z)))


def _softplus(z):
    return jnp.maximum(z, 0.0) + jnp.log(1.0 + jnp.exp(-jnp.abs(z)))


def _silu(z):
    return z * jax.nn.sigmoid(z)


def _lane_col(x, lane):
    return jnp.sum(jnp.where(_iota(x.shape, 1) == lane, x, 0.0), axis=1, keepdims=True)


def _last_row(x):
    return jnp.sum(jnp.where(_iota(x.shape, 0) == x.shape[0] - 1, x, 0.0), axis=0, keepdims=True)


def _params(sem, limit=VMEM_LIMIT):
    return pltpu.CompilerParams(dimension_semantics=sem, vmem_limit_bytes=limit)


def _pick(n, options):
    for t in options:
        if n % t == 0:
            return t
    return n


def _matmul(a, b, kind, out_dtype, name, add=None, b_kblock=0):
    if kind == "nn":
        (m, k), n = a.shape, b.shape[1]
    elif kind == "nt":
        (m, k), n = a.shape, b.shape[0]
    else:
        (k, m), n = a.shape, b.shape[1]
    tn = _pick(n, (512, 384, 640, 256, 128))
    tm_options = (1408, 384, 128) if kind != "tn" else (1024, 512, 256, 128)
    tm = None
    for t in tm_options:
        if m % t:
            continue
        need = 2 * (t * k * a.dtype.itemsize + tn * k * b.dtype.itemsize + t * tn * 8)
        if need <= VMEM_BLOCK_BUDGET:
            tm = t
            break
    assert tm is not None, (name, a.shape, b.shape)
    a_bytes, b_bytes = m * k * a.dtype.itemsize, n * k * b.dtype.itemsize
    m_outer = a_bytes + (m // tm) * b_bytes <= b_bytes + (n // tn) * a_bytes

    def ij(fn):
        return (lambda i, j: fn(i, j)) if m_outer else (lambda j, i: fn(i, j))

    def body(*refs):
        a_ref, b_ref = refs[0], refs[1]
        o_ref = refs[-1]
        acc = _dot(a_ref[...], b_ref[...], kind)
        if add is not None:
            acc = acc + refs[2][...]
        o_ref[...] = acc.astype(o_ref.dtype)

    if kind == "tn":
        a_spec = pl.BlockSpec((k, tm), ij(lambda i, j: (0, i)))
    else:
        a_spec = pl.BlockSpec((tm, k), ij(lambda i, j: (i, 0)))
    if kind == "nt":
        b_spec = pl.BlockSpec((tn, k), ij(lambda i, j: (j, b_kblock)))
    else:
        b_spec = pl.BlockSpec((k, tn), ij(lambda i, j: (0, j)))
    o_spec = pl.BlockSpec((tm, tn), ij(lambda i, j: (i, j)))
    operands, in_specs = [a, b], [a_spec, b_spec]
    if add is not None:
        operands.append(add)
        in_specs.append(o_spec)
    grid = (m // tm, n // tn) if m_outer else (n // tn, m // tm)
    return pl.pallas_call(
        body, grid=grid, in_specs=in_specs, out_specs=o_spec,
        out_shape=jax.ShapeDtypeStruct((m, n), out_dtype), name=name,
        compiler_params=_params(("parallel", "parallel")),
    )(*operands)


def _x_spec(tm, width, cb0, moves):
    return pl.BlockSpec((tm, width), lambda i, j: (i, cb0 + j * moves))


def _full_spec(arr):
    nd = arr.ndim
    return pl.BlockSpec(arr.shape, lambda i, j: (0,) * nd)


def _row_fwd(fn, xs, ps, cs, outs, name, tm=ROW_TILE, ncol=1, rows=None):
    rows = ROWS if rows is None else rows
    nx, npar, nc = len(xs), len(ps), len(cs)

    def body(*refs):
        i = pl.program_id(0)
        xv = [r[...].astype(F32) for r in refs[:nx]]
        pv = [r[...] for r in refs[nx:nx + npar + nc]]
        res = fn(i, *xv, *pv)
        for r, val in zip(refs[nx + npar + nc:], res):
            r[...] = val.astype(r.dtype)

    in_specs = [_x_spec(tm, w, cb0, mv) for (_, w, cb0, mv) in xs]
    in_specs += [_full_spec(p) for p in list(ps) + list(cs)]
    out_specs = [_x_spec(tm, w, 0, 1) for (w, _, _) in outs]
    out_shape = [jax.ShapeDtypeStruct((rows, tot), dt) for (_, tot, dt) in outs]
    return pl.pallas_call(
        body, grid=(rows // tm, ncol), in_specs=in_specs, out_specs=out_specs, out_shape=out_shape,
        name=name, compiler_params=_params(("parallel", "parallel")),
    )(*[x[0] for x in xs], *ps, *cs)


def _row_bwd(fn, xs, ps, cs, cts, dxs, name, adds=None, tm=ROW_TILE, ncol=1, rows=None):
    rows = ROWS if rows is None else rows
    nx, npar, nc, nct = len(xs), len(ps), len(cs), len(cts)
    adds = [None] * nx if adds is None else adds
    add_idx = [k for k in range(nx) if adds[k] is not None]

    def body(*refs):
        i, j = pl.program_id(0), pl.program_id(1)
        pos = 0
        x_refs = refs[pos:pos + nx]; pos += nx
        p_refs = refs[pos:pos + npar]; pos += npar
        c_refs = refs[pos:pos + nc]; pos += nc
        ct_refs = refs[pos:pos + nct]; pos += nct
        add_refs = refs[pos:pos + len(add_idx)]; pos += len(add_idx)
        dx_refs = refs[pos:pos + nx]; pos += nx
        dp_refs = refs[pos:pos + npar]
        xv = [r[...].astype(F32) for r in x_refs]
        pv = [r[...] for r in p_refs]
        cv = [r[...] for r in c_refs]
        _, vjp = jax.vjp(lambda *args: fn(i, *args, *cv), *xv, *pv)
        grads = vjp(tuple(r[...].astype(F32) for r in ct_refs))
        for k in range(nx):
            val = grads[k]
            if adds[k] is not None:
                val = val + add_refs[add_idx.index(k)][...]
            dx_refs[k][...] = val.astype(dx_refs[k].dtype)
        if npar:
            @pl.when((i == 0) & (j == 0))
            def _():
                for r in dp_refs:
                    r[...] = jnp.zeros(r.shape, r.dtype)

            for k in range(npar):
                dp_refs[k][...] += grads[nx + k]

    in_specs = [_x_spec(tm, w, cb0, mv) for (_, w, cb0, mv) in xs]
    in_specs += [_full_spec(p) for p in list(ps) + list(cs)]
    in_specs += [_x_spec(tm, w, cb0, mv) for (_, w, cb0, mv) in cts]
    in_specs += [_x_spec(tm, xs[k][1], 0, 1) for k in add_idx]
    out_specs = [_x_spec(tm, xs[k][1], 0, 1) for k in range(nx)] + [_full_spec(p) for p in ps]
    out_shape = [jax.ShapeDtypeStruct((rows, tot), dt) for (tot, dt) in dxs]
    out_shape += [jax.ShapeDtypeStruct(p.shape, F32) for p in ps]
    sem = ("arbitrary", "arbitrary") if npar else ("parallel", "parallel")
    return pl.pallas_call(
        body, grid=(rows // tm, ncol), in_specs=in_specs, out_specs=out_specs, out_shape=out_shape,
        name=name, compiler_params=_params(sem),
    )(*[x[0] for x in xs], *ps, *cs, *[c[0] for c in cts], *[adds[k] for k in add_idx])


def _rms_fn(i, h, g):
    return (h * lax.rsqrt(jnp.mean(h * h, axis=1, keepdims=True) + NORM_EPS) * g,)


def _fox_pre_fn(i, q, k, f, qg, kg, bf, e_mean, e_expand):
    tm = q.shape[0]

    def headnorm(x, gain, scale):
        msq = _dot_mid(x * x, e_mean, "nn")
        r = _dot_mid(lax.rsqrt(msq + NORM_EPS), e_expand, "nn")
        return x * r * gain * scale

    qn = headnorm(q, qg, FOX_DH ** -0.5)
    kn = headnorm(k, kg, 1.0)
    keep = (_iota((tm, LANE), 1) < FOX_HEADS).astype(F32) * _row_mask(i, tm)
    return qn, kn, _log_sigmoid(f + bf) * keep


def _fox_post_fn(i, o, gate):
    return (o * jax.nn.sigmoid(gate) * _row_mask(i, o.shape[0]),)


def _gla_pre_fn(i, alr, wa, ba):
    z = _dot_ad(alr, wa, "nn") + ba
    return (_log_sigmoid(z) * (1.0 / 16.0) * _row_mask(i, alr.shape[0]),)


def _headwise_norm_gate(o, gate, gain, dv, mask):
    pieces = []
    for h in range(o.shape[1] // dv):
        oh = o[:, h * dv:(h + 1) * dv]
        pieces.append(oh * lax.rsqrt(jnp.mean(oh * oh, axis=1, keepdims=True) + NORM_EPS) * gain)
    return jnp.concatenate(pieces, axis=1) * _silu(gate) * mask


def _gla_post_fn(i, o, r, gain):
    return (_headwise_norm_gate(o, r, gain, GLA_DV, _row_mask(i, o.shape[0])),)


def _gdn_post_fn(i, o, gate, gain):
    return (_headwise_norm_gate(o, gate, gain, GDN_DV, _row_mask(i, o.shape[0])),)


def _gdn_gates_fn(i, sm, alog, dtb):
    tm = sm.shape[0]
    lane = _iota((tm, LANE), 1)
    g = -jnp.exp(alog) * _softplus(sm + dtb)
    beta = jax.nn.sigmoid(sm)
    out = jnp.where(lane < GDN_HEADS, g, jnp.where(lane < 2 * GDN_HEADS, beta, 0.0))
    return (out * _row_mask(i, tm),)


def _gdn_conv_fn(kind):
    def fn(i, prev, cur, w):
        tm = cur.shape[0]
        xw = jnp.concatenate([prev * jnp.where(i > 0, 1.0, 0.0), cur], axis=0)
        t = _iota((tm, 2 * tm), 0)
        s = _iota((tm, 2 * tm), 1)
        y = jnp.zeros(cur.shape, F32)
        for j in range(GDN_CONV):
            sel = (s == t + (tm - (GDN_CONV - 1) + j)).astype(F32)
            y = y + _dot_mid(sel, xw, "nn") * jnp.sum(jnp.where(_iota(w.shape, 0) == j, w, 0.0), axis=0, keepdims=True)
        a = _silu(y * _row_mask(i, tm))
        if kind == "v":
            return (a,)
        scale = GDN_DK ** -0.5 if kind == "q" else 1.0
        pieces = []
        for h in range(GDN_HEADS):
            ah = a[:, h * GDN_DK:(h + 1) * GDN_DK]
            pieces.append(ah * lax.rsqrt(jnp.sum(ah * ah, axis=1, keepdims=True) + NORM_EPS) * scale)
        return (jnp.concatenate(pieces, axis=1),)

    return fn


def _seq_cumsum(x, reverse, name, pairs=None):
    rows, width = x.shape
    nt = rows // ROW_TILE

    def body(*refs):
        x_ref, o_ref, carry = refs[0], refs[-2], refs[-1]

        @pl.when(pl.program_id(0) == 0)
        def _():
            carry[...] = jnp.zeros(carry.shape, F32)

        xv = x_ref[...]
        if pairs is not None:
            lane = _iota(xv.shape, 1)
            for g in range(pairs.shape[0]):
                ta, tb = _pair_cols(refs[1][g])
                xv = xv + jnp.where(lane == 2 * g, ta, 0.0) + jnp.where(lane == 2 * g + 1, tb, 0.0)
        r = _iota((ROW_TILE, ROW_TILE), 0)
        c = _iota((ROW_TILE, ROW_TILE), 1)
        tri = ((c >= r) if reverse else (c <= r)).astype(F32)
        acc = _dot_hi(tri, xv, "nn") + carry[...]
        o_ref[...] = acc
        edge = 0 if reverse else ROW_TILE - 1
        carry[...] = jnp.sum(jnp.where(_iota(acc.shape, 0) == edge, acc, 0.0), axis=0, keepdims=True)

    idx = (lambda i: (nt - 1 - i, 0)) if reverse else (lambda i: (i, 0))
    idx3 = (lambda i: (0, nt - 1 - i, 0)) if reverse else (lambda i: (0, i, 0))
    operands, in_specs = [x], [pl.BlockSpec((ROW_TILE, width), idx)]
    if pairs is not None:
        operands.append(pairs)
        in_specs.append(pl.BlockSpec((pairs.shape[0], ROW_TILE, LANE), idx3))
    return pl.pallas_call(
        body, grid=(nt,), in_specs=in_specs,
        out_specs=pl.BlockSpec((ROW_TILE, width), idx), out_shape=jax.ShapeDtypeStruct(x.shape, F32),
        scratch_shapes=[pltpu.VMEM((1, width), F32)], name=name, compiler_params=_params(("arbitrary",)),
    )(*operands)


FOX_PAIRS = FOX_HEADS // 2
FOX_V_BLOCK = 2 * D_MODEL // LANE


def _pair_cols(x):
    return _lane_col(x, 0), _lane_col(x, FOX_DH)


ATT_TILE = 384
assert ROWS % ATT_TILE == 0
N_ATT = ROWS // ATT_TILE
KEY_OFF = 1e30


def _fox_scores(qa, qb, kj, ca, cb, cra, crb, mask):
    sa = _dot(qa, kj, "nt") + (ca - cra)
    sb = _dot(qb, kj, "nt") + (cb - crb)
    if mask is None:
        return sa, sb
    return jnp.where(mask, sa, NEG), jnp.where(mask, sb, NEG)


def _fox_diag_mask():
    return _iota((ATT_TILE, ATT_TILE), 1) <= _iota((ATT_TILE, ATT_TILE), 0)


def _att_off(j):
    return pl.multiple_of(j * ATT_TILE, ATT_TILE)


def _fox_attn_fwd(qn, kn, proj, c2, crow):
    nt = ROWS // ROW_TILE

    def body(q_ref, k_ref, v_ref, c2_ref, crow_ref, o_ref, lse_ref):
        i = pl.program_id(1)
        q = q_ref[...]
        is_a = _iota((ATT_TILE, LANE), 1) < FOX_DH
        qa, qb = jnp.where(is_a, q, 0), jnp.where(is_a, 0, q)
        ca, cb = _pair_cols(c2_ref[0])

        def block(off, size, mask, carry):
            ma, la, acca, mb, lb, accb = carry
            kj = k_ref[pl.ds(off, size), :]
            vj = v_ref[pl.ds(off, size), :].astype(BF16)
            sa, sb = _fox_scores(qa, qb, kj, ca, cb, crow_ref[0, 0:1, pl.ds(off, size)],
                                 crow_ref[0, 1:2, pl.ds(off, size)], mask)

            def online(s, m, l, acc):
                m_new = jnp.maximum(m, jnp.max(s, axis=1, keepdims=True))
                p = jnp.exp(s - m_new)
                alpha = jnp.exp(m - m_new)
                return m_new, alpha * l + jnp.sum(p, axis=1, keepdims=True), alpha * acc + _dot(p, vj, "nn")

            return online(sa, ma, la, acca) + online(sb, mb, lb, accb)

        m0 = jnp.full((ATT_TILE, 1), NEG, F32)
        l0 = jnp.zeros((ATT_TILE, 1), F32)
        a0 = jnp.zeros((ATT_TILE, LANE), F32)
        carry = lax.fori_loop(0, i, lambda j, cr: block(_att_off(j), ATT_TILE, None, cr), (m0, l0, a0, m0, l0, a0))
        ma, la, acca, mb, lb, accb = block(_att_off(i), ATT_TILE, _fox_diag_mask(), carry)
        o_ref[...] = jnp.where(is_a, acca / la, accb / lb)
        lse_ref[0] = jnp.where(is_a, ma + jnp.log(la), mb + jnp.log(lb))

    return pl.pallas_call(
        body, grid=(FOX_PAIRS, N_ATT),
        in_specs=[
            pl.BlockSpec((ATT_TILE, LANE), lambda g, i: (i, g)),
            pl.BlockSpec((ROWS, LANE), lambda g, i: (0, g)),
            pl.BlockSpec((ROWS, LANE), lambda g, i: (0, FOX_V_BLOCK + g)),
            pl.BlockSpec((1, ATT_TILE, LANE), lambda g, i: (g, i, 0)),
            pl.BlockSpec((1, 2, ROWS), lambda g, i: (g, 0, 0)),
        ],
        out_specs=[
            pl.BlockSpec((ATT_TILE, LANE), lambda g, i: (i, g)),
            pl.BlockSpec((1, ATT_TILE, LANE), lambda g, i: (g, i, 0)),
        ],
        out_shape=[jax.ShapeDtypeStruct((ROWS, D_MODEL), F32), jax.ShapeDtypeStruct((FOX_PAIRS, ROWS, LANE), F32)],
        name="fox_attn_fwd", compiler_params=_params(("parallel", "arbitrary")),
    )(qn, kn, proj, c2, crow)


def _fox_attn_dq(qn, kn, proj, c2, crow, lse2, o, do, blockdiag):
    nt = ROWS // ROW_TILE

    def body(q_ref, k_ref, v_ref, c2_ref, crow_ref, lse_ref, o_ref, do_ref, bd_ref, dq_ref, delta_ref, dcq_ref):
        i = pl.program_id(1)
        q = q_ref[...]
        is_a = _iota((ATT_TILE, LANE), 1) < FOX_DH
        qa, qb = jnp.where(is_a, q, 0), jnp.where(is_a, 0, q)
        ca, cb = _pair_cols(c2_ref[0])
        lsa, lsb = _pair_cols(lse_ref[0])
        do_t = do_ref[...]
        delta2 = _dot_hi(do_t * o_ref[...], bd_ref[...], "nn")
        delta_ref[0] = delta2
        dla, dlb = _pair_cols(delta2)
        doa, dob = jnp.where(is_a, do_t, 0.0).astype(BF16), jnp.where(is_a, 0.0, do_t).astype(BF16)

        def block(off, size, mask, carry):
            dqa, dqb, rsa, rsb = carry
            kj = k_ref[pl.ds(off, size), :]
            vj = v_ref[pl.ds(off, size), :].astype(BF16)
            sa, sb = _fox_scores(qa, qb, kj, ca, cb, crow_ref[0, 0:1, pl.ds(off, size)],
                                 crow_ref[0, 1:2, pl.ds(off, size)], mask)
            dsa = jnp.exp(sa - lsa) * (_dot(doa, vj, "nt") - dla)
            dsb = jnp.exp(sb - lsb) * (_dot(dob, vj, "nt") - dlb)
            return (dqa + _dot(dsa, kj, "nn"), dqb + _dot(dsb, kj, "nn"),
                    rsa + jnp.sum(dsa, axis=1, keepdims=True), rsb + jnp.sum(dsb, axis=1, keepdims=True))

        z = jnp.zeros((ATT_TILE, LANE), F32)
        zc = jnp.zeros((ATT_TILE, 1), F32)
        carry = lax.fori_loop(0, i, lambda j, cr: block(_att_off(j), ATT_TILE, None, cr), (z, z, zc, zc))
        dqa, dqb, rsa, rsb = block(_att_off(i), ATT_TILE, _fox_diag_mask(), carry)
        dq_ref[...] = jnp.where(is_a, dqa, dqb)
        dcq_ref[0] = jnp.where(is_a, rsa, rsb)

    tile = pl.BlockSpec((ATT_TILE, LANE), lambda g, i: (i, g))
    pair_tile = pl.BlockSpec((1, ATT_TILE, LANE), lambda g, i: (g, i, 0))
    return pl.pallas_call(
        body, grid=(FOX_PAIRS, N_ATT),
        in_specs=[
            tile,
            pl.BlockSpec((ROWS, LANE), lambda g, i: (0, g)),
            pl.BlockSpec((ROWS, LANE), lambda g, i: (0, FOX_V_BLOCK + g)),
            pair_tile,
            pl.BlockSpec((1, 2, ROWS), lambda g, i: (g, 0, 0)),
            pair_tile, tile, tile,
            pl.BlockSpec((LANE, LANE), lambda g, i: (0, 0)),
        ],
        out_specs=[tile, pair_tile, pair_tile],
        out_shape=[jax.ShapeDtypeStruct((ROWS, D_MODEL), F32), jax.ShapeDtypeStruct((FOX_PAIRS, ROWS, LANE), F32),
                   jax.ShapeDtypeStruct((FOX_PAIRS, ROWS, LANE), F32)],
        name="fox_attn_dq", compiler_params=_params(("parallel", "arbitrary")),
    )(qn, kn, proj, c2, crow, lse2, o, do, blockdiag)


def _fox_attn_dkv(qn, kn, proj, c2, crow, lse2, delta2, do):
    nt = ROWS // ROW_TILE

    def body(q_ref, k_ref, v_ref, c2_ref, crow_ref, lse_ref, delta_ref, do_ref, dk_ref, dv_ref, dc_ref):
        j = pl.program_id(1)
        kj = k_ref[...]
        vj = v_ref[...].astype(BF16)
        cra, crb = crow_ref[0, 0:1, :], crow_ref[0, 1:2, :]

        def block(off, size, mask, carry):
            dk, dv, dca, dcb = carry
            is_a = _iota((size, LANE), 1) < FOX_DH
            q = q_ref[pl.ds(off, size), :]
            qa, qb = jnp.where(is_a, q, 0), jnp.where(is_a, 0, q)
            do_t = do_ref[pl.ds(off, size), :]
            doa, dob = jnp.where(is_a, do_t, 0.0).astype(BF16), jnp.where(is_a, 0.0, do_t).astype(BF16)
            ca, cb = _pair_cols(c2_ref[0, pl.ds(off, size), :])
            lsa, lsb = _pair_cols(lse_ref[0, pl.ds(off, size), :])
            dla, dlb = _pair_cols(delta_ref[0, pl.ds(off, size), :])
            sa, sb = _fox_scores(qa, qb, kj, ca, cb, cra, crb, mask)
            pa, pb = jnp.exp(sa - lsa), jnp.exp(sb - lsb)
            dsa = pa * (_dot(doa, vj, "nt") - dla)
            dsb = pb * (_dot(dob, vj, "nt") - dlb)
            dv = dv + _dot(pa, doa, "tn") + _dot(pb, dob, "tn")
            dk = dk + _dot(dsa, qa, "tn") + _dot(dsb, qb, "tn")
            dca = dca - jnp.sum(dsa, axis=0, keepdims=True)
            dcb = dcb - jnp.sum(dsb, axis=0, keepdims=True)
            return dk, dv, dca, dcb

        z = jnp.zeros((ATT_TILE, LANE), F32)
        zr = jnp.zeros((1, ATT_TILE), F32)
        carry = block(_att_off(j), ATT_TILE, _fox_diag_mask(), (z, z, zr, zr))
        dk, dv, dca, dcb = lax.fori_loop(j + 1, N_ATT, lambda i, cr: block(_att_off(i), ATT_TILE, None, cr), carry)
        dk_ref[...] = dk
        dv_ref[...] = dv.astype(dv_ref.dtype)
        dc_ref[0, 0:1, :] = dca
        dc_ref[0, 1:2, :] = dcb

    whole = pl.BlockSpec((ROWS, LANE), lambda g, j: (0, g))
    pair_whole = pl.BlockSpec((1, ROWS, LANE), lambda g, j: (g, 0, 0))
    tile = pl.BlockSpec((ATT_TILE, LANE), lambda g, j: (j, g))
    return pl.pallas_call(
        body, grid=(FOX_PAIRS, N_ATT),
        in_specs=[
            whole, tile,
            pl.BlockSpec((ATT_TILE, LANE), lambda g, j: (j, FOX_V_BLOCK + g)),
            pair_whole,
            pl.BlockSpec((1, 2, ATT_TILE), lambda g, j: (g, 0, j)),
            pair_whole, pair_whole, whole,
        ],
        out_specs=[tile, tile, pl.BlockSpec((1, 2, ATT_TILE), lambda g, j: (g, 0, j))],
        out_shape=[jax.ShapeDtypeStruct((ROWS, D_MODEL), F32), jax.ShapeDtypeStruct((ROWS, D_MODEL), BF16),
                   jax.ShapeDtypeStruct((FOX_PAIRS, 2, ROWS), F32)],
        name="fox_attn_dkv", compiler_params=_params(("parallel", "arbitrary")),
    )(qn, kn, proj, c2, crow, lse2, delta2, do)


def _fox_attn_bwd(qn, kn, proj, c2, crow, lse2, o, do, blockdiag):
    def body(q_ref, k_ref, v_ref, c2_ref, crow_ref, lse_ref, o_ref, do_ref, bd_ref,
             dq_ref, dk_ref, dv_ref, dc_ref, dcq_ref, delta_ref):
        j = pl.program_id(1)

        @pl.when(j == 0)
        def _():
            for t in range(N_ATT):
                rows = pl.ds(t * ATT_TILE, ATT_TILE)
                delta_ref[rows, :] = _dot_hi(do_ref[rows, :] * o_ref[rows, :], bd_ref[...], "nn")
            dq_ref[...] = jnp.zeros(dq_ref.shape, F32)
            dcq_ref[...] = jnp.zeros(dcq_ref.shape, F32)

        kj = k_ref[...]
        vj = v_ref[...].astype(BF16)
        cra, crb = crow_ref[0, 0:1, :], crow_ref[0, 1:2, :]
        is_a = _iota((ATT_TILE, LANE), 1) < FOX_DH

        def block(off, mask, carry):
            dk, dv, dca, dcb = carry
            rows = pl.ds(off, ATT_TILE)
            q = q_ref[rows, :]
            qa, qb = jnp.where(is_a, q, 0), jnp.where(is_a, 0, q)
            do_t = do_ref[rows, :]
            doa, dob = jnp.where(is_a, do_t, 0.0).astype(BF16), jnp.where(is_a, 0.0, do_t).astype(BF16)
            ca, cb = _pair_cols(c2_ref[0, rows, :])
            lsa, lsb = _pair_cols(lse_ref[0, rows, :])
            dla, dlb = _pair_cols(delta_ref[rows, :])
            sa, sb = _fox_scores(qa, qb, kj, ca, cb, cra, crb, mask)
            pa, pb = jnp.exp(sa - lsa), jnp.exp(sb - lsb)
            dsa = pa * (_dot(doa, vj, "nt") - dla)
            dsb = pb * (_dot(dob, vj, "nt") - dlb)
            dq_ref[rows, :] += jnp.where(is_a, _dot(dsa, kj, "nn"), _dot(dsb, kj, "nn"))
            dcq_ref[0, rows, :] += jnp.where(is_a, jnp.sum(dsa, axis=1, keepdims=True),
                                             jnp.sum(dsb, axis=1, keepdims=True))
            dv = dv + _dot(pa, doa, "tn") + _dot(pb, dob, "tn")
            dk = dk + _dot(dsa, qa, "tn") + _dot(dsb, qb, "tn")
            dca = dca - jnp.sum(dsa, axis=0, keepdims=True)
            dcb = dcb - jnp.sum(dsb, axis=0, keepdims=True)
            return dk, dv, dca, dcb

        z = jnp.zeros((ATT_TILE, LANE), F32)
        zr = jnp.zeros((1, ATT_TILE), F32)
        carry = block(_att_off(j), _fox_diag_mask(), (z, z, zr, zr))
        dk, dv, dca, dcb = lax.fori_loop(j + 1, N_ATT, lambda i, cr: block(_att_off(i), None, cr), carry)
        dk_ref[...] = dk
        dv_ref[...] = dv.astype(dv_ref.dtype)
        dc_ref[0, 0:1, :] = dca
        dc_ref[0, 1:2, :] = dcb

    whole = pl.BlockSpec((ROWS, LANE), lambda g, j: (0, g))
    pair_whole = pl.BlockSpec((1, ROWS, LANE), lambda g, j: (g, 0, 0))
    tile = pl.BlockSpec((ATT_TILE, LANE), lambda g, j: (j, g))
    key_rows = pl.BlockSpec((1, 2, ATT_TILE), lambda g, j: (g, 0, j))
    return pl.pallas_call(
        body, grid=(FOX_PAIRS, N_ATT),
        in_specs=[
            whole, tile,
            pl.BlockSpec((ATT_TILE, LANE), lambda g, j: (j, FOX_V_BLOCK + g)),
            pair_whole, key_rows, pair_whole, whole, whole,
            pl.BlockSpec((LANE, LANE), lambda g, j: (0, 0)),
        ],
        out_specs=[whole, tile, tile, key_rows, pair_whole],
        out_shape=[jax.ShapeDtypeStruct((ROWS, D_MODEL), F32), jax.ShapeDtypeStruct((ROWS, D_MODEL), F32),
                   jax.ShapeDtypeStruct((ROWS, D_MODEL), BF16), jax.ShapeDtypeStruct((FOX_PAIRS, 2, ROWS), F32),
                   jax.ShapeDtypeStruct((FOX_PAIRS, ROWS, LANE), F32)],
        scratch_shapes=[pltpu.VMEM((ROWS, LANE), F32)],
        name="fox_attn_bwd", compiler_params=_params(("parallel", "arbitrary")),
    )(qn, kn, proj, c2, crow, lse2, o, do, blockdiag)


def _gla_chunk(q, k, v, g, st, dot):
    n = len(q)
    c = q[0].shape[0]
    incl = _iota((c, c), 1) <= _iota((c, c), 0)
    incl_f = incl.astype(F32)
    b = [_dot_hi(incl_f, g[i], "nn") for i in range(n)]
    qe = [q[i] * (GLA_DK ** -0.5) * jnp.exp(b[i]) for i in range(n)]
    ke = [k[i] * jnp.exp(-b[i]) for i in range(n)]
    att = [jnp.where(incl, dot(qe[i], ke[i], "nt"), 0.0) for i in range(n)]
    o_intra = [dot(att[i], v[i], "nn") for i in range(n)]
    o_inter = [dot(qe[i], st[i], "nt") for i in range(n)]
    b_last = [_last_row(b[i]) for i in range(n)]
    kd = [k[i] * jnp.exp(b_last[i] - b[i]) for i in range(n)]
    upd = [dot(v[i], kd[i], "tn") for i in range(n)]
    return ([o_intra[i] + o_inter[i] for i in range(n)],
            [st[i] * jnp.exp(b_last[i]) + upd[i] for i in range(n)])


def _tri_inverse(low):
    c = low[0].shape[0]
    eye = (_iota((c, c), 0) == _iota((c, c), 1)).astype(F32)
    x, p = [eye - l for l in low], list(low)
    span = 2
    while span < c:
        p = [_dot_mid(pi, pi, "nn") for pi in p]
        x = [_dot_mid(xi, eye + pi, "nn") for xi, pi in zip(x, p)]
        span *= 2
    return x


def _solve(low, rhs):
    return [_dot_mid(inv, r, "nn") for inv, r in zip(_tri_inverse(low), rhs)]


@jax.custom_vjp
def _solve_ad(low, rhs):
    return _solve(low, rhs)


def _solve_ad_fwd(low, rhs):
    inv = _tri_inverse(low)
    sol = [_dot_mid(i, r, "nn") for i, r in zip(inv, rhs)]
    return sol, (inv, sol)


def _solve_ad_bwd(res, ct):
    inv, sol = res
    drhs = [_dot_mid(i, c, "tn") for i, c in zip(inv, ct)]
    return [-_dot_mid(d, s, "nt") for d, s in zip(drhs, sol)], drhs


_solve_ad.defvjp(_solve_ad_fwd, _solve_ad_bwd)


def _gdn_chunk(q, k, v, gates, st, heads, dot, solve):
    n = len(q)
    c = q[0].shape[0]
    incl = _iota((c, c), 1) <= _iota((c, c), 0)
    strict = _iota((c, c), 1) < _iota((c, c), 0)
    incl_f = incl.astype(F32)
    upper_f = (_iota((c, c), 0) <= _iota((c, c), 1)).astype(F32)
    gcol = [_lane_col(gates, h) for h in heads]
    beta = [_lane_col(gates, h + GDN_HEADS) for h in heads]
    b = [_dot_hi(incl_f, jnp.broadcast_to(gc, (c, LANE)), "nn") for gc in gcol]
    g_sq = [jnp.broadcast_to(gc, (c, c)) for gc in gcol]
    b_col = [_dot_hi(incl_f, gs, "nn") for gs in g_sq]
    b_row = [_dot_hi(gs, upper_f, "tn") for gs in g_sq]
    diff = [b_col[i] - b_row[i] for i in range(n)]
    kb = [k[i] * beta[i] for i in range(n)]
    vb = [v[i] * beta[i] for i in range(n)]
    low = [dot(kb[i], k[i], "nt") * jnp.exp(jnp.where(strict, diff[i], NEG)) for i in range(n)]
    eb = [jnp.exp(bi) for bi in b]
    sol = solve(low, [jnp.concatenate([vb[i], kb[i] * eb[i]], axis=1) for i in range(n)])
    w_st = [dot(sol[i][:, GDN_DV:], st[i], "nn") for i in range(n)]
    v_new = [sol[i][:, :GDN_DV] - w_st[i] for i in range(n)]
    att = [dot(q[i], k[i], "nt") * jnp.exp(jnp.where(incl, diff[i], NEG)) for i in range(n)]
    o_intra = [dot(att[i], v_new[i], "nn") for i in range(n)]
    o_inter = [dot(q[i] * eb[i], st[i], "nn") for i in range(n)]
    b_last = [_last_row(bi) for bi in b]
    kd = [k[i] * jnp.exp(b_last[i] - b[i]) for i in range(n)]
    upd = [dot(kd[i], v_new[i], "tn") for i in range(n)]
    return ([o_intra[i] + o_inter[i] for i in range(n)],
            [st[i] * jnp.exp(b_last[i]) + upd[i] for i in range(n)])


N_CHUNKS = ROWS // CHUNK
GDN_GROUP = 8


def _head_lanes(width, n):
    return [slice(i * width, (i + 1) * width) for i in range(n)]


def _gla_scan_fwd(proj, garr):
    kl, vl = _head_lanes(GLA_DK, GLA_HEADS), _head_lanes(GLA_DV, GLA_HEADS)

    def body(q_ref, k_ref, v_ref, g_ref, o_ref, states_ref, st):
        @pl.when(pl.program_id(0) == 0)
        def _():
            st[...] = jnp.zeros(st.shape, F32)

        s_in = [st[i] for i in range(GLA_HEADS)]
        o, s_out = _gla_chunk([q_ref[:, ln] for ln in kl], [k_ref[:, ln] for ln in kl], [v_ref[:, ln] for ln in vl],
                              [g_ref[:, ln] for ln in kl], s_in, _dot)
        for i in range(GLA_HEADS):
            states_ref[i, 0] = s_in[i]
            o_ref[:, vl[i]] = o[i]
            st[i] = s_out[i]

    return pl.pallas_call(
        body, grid=(N_CHUNKS,),
        in_specs=[
            pl.BlockSpec((CHUNK, GLA_QK), lambda c: (c, 0)),
            pl.BlockSpec((CHUNK, GLA_QK), lambda c: (c, 1)),
            pl.BlockSpec((CHUNK, GLA_V), lambda c: (c, 1)),
            pl.BlockSpec((CHUNK, GLA_QK), lambda c: (c, 0)),
        ],
        out_specs=[
            pl.BlockSpec((CHUNK, GLA_V), lambda c: (c, 0)),
            pl.BlockSpec((GLA_HEADS, 1, GLA_DV, GLA_DK), lambda c: (0, c, 0, 0)),
        ],
        out_shape=[jax.ShapeDtypeStruct((ROWS, GLA_V), F32),
                   jax.ShapeDtypeStruct((GLA_HEADS, N_CHUNKS, GLA_DV, GLA_DK), F32)],
        scratch_shapes=[pltpu.VMEM((GLA_HEADS, GLA_DV, GLA_DK), F32)],
        name="gla_scan_fwd", compiler_params=_params(("arbitrary",)),
    )(proj, proj, proj, garr)


def _gla_scan_bwd(proj, garr, states, do):
    last = N_CHUNKS - 1
    kl, vl = _head_lanes(GLA_DK, GLA_HEADS), _head_lanes(GLA_DV, GLA_HEADS)

    def body(q_ref, k_ref, v_ref, g_ref, s_ref, do_ref, dq_ref, dk_ref, dv_ref, dg_ref, dst):
        @pl.when(pl.program_id(0) == 0)
        def _():
            dst[...] = jnp.zeros(dst.shape, F32)

        _, vjp = jax.vjp(lambda q, k, v, g, s: _gla_chunk(q, k, v, g, s, _dot_ad),
                         [q_ref[:, ln] for ln in kl], [k_ref[:, ln] for ln in kl], [v_ref[:, ln] for ln in vl],
                         [g_ref[:, ln] for ln in kl], [s_ref[i, 0] for i in range(GLA_HEADS)])
        dq, dk, dv, dg, ds = vjp(([do_ref[:, ln] for ln in vl], [dst[i] for i in range(GLA_HEADS)]))
        for i in range(GLA_HEADS):
            dq_ref[:, kl[i]] = dq[i].astype(dq_ref.dtype)
            dk_ref[:, kl[i]] = dk[i].astype(dk_ref.dtype)
            dv_ref[:, vl[i]] = dv[i].astype(dv_ref.dtype)
            dg_ref[:, kl[i]] = dg[i]
            dst[i] = ds[i]

    kspec = pl.BlockSpec((CHUNK, GLA_QK), lambda c: (last - c, 0))
    vspec = pl.BlockSpec((CHUNK, GLA_V), lambda c: (last - c, 0))
    return pl.pallas_call(
        body, grid=(N_CHUNKS,),
        in_specs=[
            kspec,
            pl.BlockSpec((CHUNK, GLA_QK), lambda c: (last - c, 1)),
            pl.BlockSpec((CHUNK, GLA_V), lambda c: (last - c, 1)),
            kspec,
            pl.BlockSpec((GLA_HEADS, 1, GLA_DV, GLA_DK), lambda c: (0, last - c, 0, 0)),
            vspec,
        ],
        out_specs=[kspec, kspec, vspec, kspec],
        out_shape=[jax.ShapeDtypeStruct((ROWS, GLA_QK), BF16), jax.ShapeDtypeStruct((ROWS, GLA_QK), BF16),
                   jax.ShapeDtypeStruct((ROWS, GLA_V), BF16), jax.ShapeDtypeStruct((ROWS, GLA_QK), F32)],
        scratch_shapes=[pltpu.VMEM((GLA_HEADS, GLA_DV, GLA_DK), F32)],
        name="gla_scan_bwd", compiler_params=_params(("arbitrary",)),
    )(proj, proj, proj, garr, states, do)


def _gdn_scan_fwd(qn, kn, vn, gates):
    def body(q_ref, k_ref, v_ref, g_ref, o_ref, states_ref, st):
        @pl.when((pl.program_id(0) == 0) & (pl.program_id(1) == 0))
        def _():
            st[...] = jnp.zeros(st.shape, F32)

        heads = [pl.program_id(1) * GDN_GROUP + hh for hh in range(GDN_GROUP)]
        lanes = _head_lanes(GDN_DK, GDN_GROUP)
        s_in = [st[h] for h in heads]
        o, s_out = _gdn_chunk([q_ref[:, ln] for ln in lanes], [k_ref[:, ln] for ln in lanes],
                              [v_ref[:, ln] for ln in lanes], g_ref[...], s_in, heads, _dot, _solve)
        for hh in range(GDN_GROUP):
            states_ref[hh, 0] = s_in[hh]
            o_ref[:, lanes[hh]] = o[hh]
            st[heads[hh]] = s_out[hh]

    hspec = pl.BlockSpec((CHUNK, GDN_GROUP * GDN_DK), lambda c, h: (c, h))
    return pl.pallas_call(
        body, grid=(N_CHUNKS, GDN_HEADS // GDN_GROUP),
        in_specs=[hspec, hspec, hspec, pl.BlockSpec((CHUNK, LANE), lambda c, h: (c, 0))],
        out_specs=[hspec, pl.BlockSpec((GDN_GROUP, 1, GDN_DK, GDN_DV), lambda c, h: (h, c, 0, 0))],
        out_shape=[jax.ShapeDtypeStruct((ROWS, D_MODEL), F32),
                   jax.ShapeDtypeStruct((GDN_HEADS, N_CHUNKS, GDN_DK, GDN_DV), F32)],
        scratch_shapes=[pltpu.VMEM((GDN_HEADS, GDN_DK, GDN_DV), F32)],
        name="gdn_scan_fwd", compiler_params=_params(("arbitrary", "arbitrary")),
    )(qn, kn, vn, gates)


def _gdn_scan_bwd(qn, kn, vn, gates, states, do):
    last = N_CHUNKS - 1

    def body(q_ref, k_ref, v_ref, g_ref, s_ref, do_ref, dq_ref, dk_ref, dv_ref, dg_ref, dst):
        @pl.when((pl.program_id(0) == 0) & (pl.program_id(1) == 0))
        def _():
            dst[...] = jnp.zeros(dst.shape, F32)

        @pl.when(pl.program_id(1) == 0)
        def _():
            dg_ref[...] = jnp.zeros(dg_ref.shape, F32)

        heads = [pl.program_id(1) * GDN_GROUP + hh for hh in range(GDN_GROUP)]
        lanes = _head_lanes(GDN_DK, GDN_GROUP)
        _, vjp = jax.vjp(lambda q, k, v, g, s: _gdn_chunk(q, k, v, g, s, heads, _dot_ad, _solve_ad),
                         [q_ref[:, ln] for ln in lanes], [k_ref[:, ln] for ln in lanes],
                         [v_ref[:, ln] for ln in lanes], g_ref[...], [s_ref[hh, 0] for hh in range(GDN_GROUP)])
        dq, dk, dv, dg, ds = vjp(([do_ref[:, ln] for ln in lanes], [dst[h] for h in heads]))
        for hh in range(GDN_GROUP):
            dq_ref[:, lanes[hh]] = dq[hh]
            dk_ref[:, lanes[hh]] = dk[hh]
            dv_ref[:, lanes[hh]] = dv[hh]
            dst[heads[hh]] = ds[hh]
        dg_ref[...] += dg

    hspec = pl.BlockSpec((CHUNK, GDN_GROUP * GDN_DK), lambda c, h: (last - c, h))
    gspec = pl.BlockSpec((CHUNK, LANE), lambda c, h: (last - c, 0))
    return pl.pallas_call(
        body, grid=(N_CHUNKS, GDN_HEADS // GDN_GROUP),
        in_specs=[hspec, hspec, hspec, gspec,
                  pl.BlockSpec((GDN_GROUP, 1, GDN_DK, GDN_DV), lambda c, h: (h, last - c, 0, 0)), hspec],
        out_specs=[hspec, hspec, hspec, gspec],
        out_shape=[jax.ShapeDtypeStruct((ROWS, D_MODEL), F32)] * 3 + [jax.ShapeDtypeStruct((ROWS, LANE), F32)],
        scratch_shapes=[pltpu.VMEM((GDN_HEADS, GDN_DK, GDN_DV), F32)],
        name="gdn_scan_bwd", compiler_params=_params(("arbitrary", "arbitrary")),
    )(qn, kn, vn, gates, states, do)


def _gdn_conv_fwd(proj, w, kind, cb):
    fn = _gdn_conv_fn(kind)
    nt = ROWS // ROW_TILE

    def body(prev_ref, cur_ref, w_ref, o_ref):
        (out,) = fn(pl.program_id(0), prev_ref[...], cur_ref[...], w_ref[...])
        o_ref[...] = out

    return pl.pallas_call(
        body, grid=(nt,),
        in_specs=[pl.BlockSpec((ROW_TILE, D_MODEL), lambda i: (jnp.maximum(i - 1, 0), cb)),
                  pl.BlockSpec((ROW_TILE, D_MODEL), lambda i: (i, cb)),
                  pl.BlockSpec(w.shape, lambda i: (0, 0))],
        out_specs=pl.BlockSpec((ROW_TILE, D_MODEL), lambda i: (i, 0)),
        out_shape=jax.ShapeDtypeStruct((ROWS, D_MODEL), F32),
        name="gdn_conv_fwd_" + kind, compiler_params=_params(("parallel",)),
    )(proj, proj, w)


def _gdn_conv_bwd(proj, w, kind, cb, ct):
    fn = _gdn_conv_fn(kind)
    nt = ROWS // ROW_TILE

    def body(prev_ref, cur_ref, w_ref, ct_ref, dx_ref, dw_ref, carry):
        step = pl.program_id(0)
        i = nt - 1 - step

        @pl.when(step == 0)
        def _():
            carry[...] = jnp.zeros(carry.shape, F32)
            dw_ref[...] = jnp.zeros(dw_ref.shape, F32)

        _, vjp = jax.vjp(lambda p, c, ww: fn(i, p, c, ww), prev_ref[...], cur_ref[...], w_ref[...])
        dprev, dcur, dw = vjp((ct_ref[...],))
        dx_ref[...] = (dcur + carry[...]).astype(dx_ref.dtype)
        carry[...] = dprev
        dw_ref[...] += dw

    return pl.pallas_call(
        body, grid=(nt,),
        in_specs=[pl.BlockSpec((ROW_TILE, D_MODEL), lambda s: (jnp.maximum(nt - 2 - s, 0), cb)),
                  pl.BlockSpec((ROW_TILE, D_MODEL), lambda s: (nt - 1 - s, cb)),
                  pl.BlockSpec(w.shape, lambda s: (0, 0)),
                  pl.BlockSpec((ROW_TILE, D_MODEL), lambda s: (nt - 1 - s, 0))],
        out_specs=[pl.BlockSpec((ROW_TILE, D_MODEL), lambda s: (nt - 1 - s, 0)),
                   pl.BlockSpec(w.shape, lambda s: (0, 0))],
        out_shape=[jax.ShapeDtypeStruct((ROWS, D_MODEL), BF16), jax.ShapeDtypeStruct(w.shape, F32)],
        scratch_shapes=[pltpu.VMEM((ROW_TILE, D_MODEL), F32)],
        name="gdn_conv_bwd_" + kind, compiler_params=_params(("arbitrary",)),
    )(proj, proj, w, ct)


def _loss_head(h, target):
    nt = ROWS // ROW_TILE
    lead = PAD_ROWS // ROW_TILE

    def body(h_ref, t_ref, dh_ref, acc_ref):
        i = pl.program_id(0)

        @pl.when(i == 0)
        def _():
            acc_ref[...] = jnp.zeros(acc_ref.shape, F32)

        err = (h_ref[...] - t_ref[...]) * jnp.where(i >= lead, 1.0, 0.0)
        dh_ref[...] = err * (1.0 / D_MODEL)
        acc_ref[...] += jnp.sum(err * err, axis=0, keepdims=True)

    return pl.pallas_call(
        body, grid=(nt,),
        in_specs=[pl.BlockSpec((ROW_TILE, D_MODEL), lambda i: (i, 0)),
                  pl.BlockSpec((ROW_TILE, D_MODEL), lambda i: (jnp.maximum(i - lead, 0), 0))],
        out_specs=[pl.BlockSpec((ROW_TILE, D_MODEL), lambda i: (i, 0)),
                   pl.BlockSpec((1, D_MODEL), lambda i: (0, 0))],
        out_shape=[jax.ShapeDtypeStruct((ROWS, D_MODEL), F32), jax.ShapeDtypeStruct((1, D_MODEL), F32)],
        name="loss_head", compiler_params=_params(("arbitrary",)),
    )(h, target)


MESH = pl.DeviceIdType.MESH
ANY = pl.BlockSpec(memory_space=pl.ANY)


N_PEERS = N_DEV - 1


HBM_SPEC = pl.BlockSpec(memory_space=pltpu.HBM)
SEM_SPEC = pl.BlockSpec(memory_space=pltpu.SEMAPHORE)
SIDE_EFFECT = pltpu.SideEffectType.DATAFLOW_SIDE_EFFECTING


def _me_and_peers():
    x, y, c = lax.axis_index("x"), lax.axis_index("y"), lax.axis_index("c")
    peers = []
    for k in range(1, N_DEV):
        pid = (1 - x if k & 4 else x, 1 - y if k & 2 else y, 1 - c if k & 1 else c)
        peers.append((4 * pid[0] + 2 * pid[1] + pid[2], pid))
    return 4 * x + 2 * y + c, peers


def _split_copies(src_refs, land_refs, send_sems, recv_sems, spread):
    me, peers = _me_and_peers()
    out = []
    for t, (src, land) in enumerate(zip(src_refs, land_refs)):
        for k, (peer, pid) in enumerate(peers):
            sems = dict(send_sem=send_sems.at[N_PEERS * t + k], recv_sem=recv_sems.at[N_PEERS * t + k],
                        device_id=pid, device_id_type=MESH)
            piece = src if spread else src.at[peer]
            out.append((pltpu.make_async_remote_copy(src_ref=piece, dst_ref=land.at[me], **sems),
                        pltpu.make_async_remote_copy(src_ref=piece, dst_ref=land.at[peer], **sems)))
    return out


def _send_start(srcs, spread, name):
    n = len(srcs)
    lands = [lax.empty((N_DEV,) + (s.shape if spread else s.shape[1:]), s.dtype) for s in srcs]

    def body(*refs):
        for out, _ in _split_copies(refs[:n], refs[n:2 * n], refs[2 * n], refs[2 * n + 1], spread):
            out.start()
        refs[-1][...] = jnp.zeros(refs[-1].shape, F32)

    thru = [pltpu.HBM(a.shape, a.dtype) for a in list(srcs) + lands]
    res = pl.pallas_call(
        body, name=name,
        out_shape=(pltpu.SemaphoreType.DMA((N_PEERS * n,)), pltpu.SemaphoreType.DMA((N_PEERS * n,)), *thru,
                   jax.ShapeDtypeStruct((8, LANE), F32)),
        in_specs=[HBM_SPEC] * (2 * n),
        out_specs=(SEM_SPEC, SEM_SPEC, *[HBM_SPEC] * (2 * n), pl.BlockSpec(memory_space=pltpu.VMEM)),
        input_output_aliases={t: 2 + t for t in range(2 * n)},
        compiler_params=pltpu.CompilerParams(has_side_effects=SIDE_EFFECT),
    )(*[pltpu.with_memory_space_constraint(a, pltpu.HBM) for a in list(srcs) + lands])
    return res[0], res[1], list(res[2:2 + n]), list(res[2 + n:2 + 2 * n]), res[-1]


def _send_wait(started, spread, after, name):
    send_sems, recv_sems, srcs, lands, _ = started
    n = len(srcs)

    def body(*refs):
        for out, arrival in _split_copies(refs[:n], refs[n:2 * n], refs[2 * n], refs[2 * n + 1], spread):
            out.wait_send()
            arrival.wait_recv()

    thru = [pltpu.HBM(a.shape, a.dtype) for a in srcs + lands]
    res = pl.pallas_call(
        body, name=name, out_shape=tuple(thru),
        in_specs=[HBM_SPEC] * (2 * n) + [SEM_SPEC, SEM_SPEC, ANY], out_specs=tuple([HBM_SPEC] * (2 * n)),
        input_output_aliases={t: t for t in range(2 * n)},
        compiler_params=pltpu.CompilerParams(has_side_effects=SIDE_EFFECT),
    )(*srcs, *lands, send_sems, recv_sems, after)
    return list(res[n:])


def _adamw(recvs, w, m, v, name):
    n_layers = len(recvs)
    rows, cols = recvs[0].shape[1:]
    tr = _pick(rows, (256, 128, 64, 32, 16))
    nb = rows // tr

    def body(*refs):
        r_refs = refs[:n_layers]
        w_ref, m_ref, v_ref, g_ref, d_ref, m2_ref, v2_ref = refs[n_layers:]

        def update(r_ref):
            g = r_ref[0].astype(F32)
            for k in range(1, N_DEV):
                g = g + r_ref[k].astype(F32)
            m2 = ADAM_B1 * m_ref[...] + (1.0 - ADAM_B1) * g
            v2 = ADAM_B2 * v_ref[...] + (1.0 - ADAM_B2) * (g * g)
            m_hat = m2 / (1.0 - ADAM_B1 ** ADAM_STEP)
            v_hat = v2 / (1.0 - ADAM_B2 ** ADAM_STEP)
            g_ref[...] = g
            d_ref[...] = -ADAM_LR * (m_hat / (jnp.sqrt(v_hat) + ADAM_EPS) + ADAM_WD * w_ref[...])
            m2_ref[...] = m2
            v2_ref[...] = v2

        if n_layers == 1:
            update(r_refs[0])
        else:
            for layer in range(n_layers):
                pl.when(pl.program_id(0) == layer)(functools.partial(update, r_refs[layer]))

    def recv_spec(layer):
        return pl.BlockSpec((N_DEV, tr, cols), lambda a, i: (0, jnp.where(a == layer, i, 0), 0))

    spec = pl.BlockSpec((tr, cols), lambda a, i: (a * nb + i, 0))
    return pl.pallas_call(
        body, grid=(n_layers, nb),
        in_specs=[recv_spec(layer) for layer in range(n_layers)] + [spec, spec, spec],
        out_specs=[spec] * 4, out_shape=[jax.ShapeDtypeStruct(w.shape, F32)] * 4,
        name=name, compiler_params=_params(("arbitrary", "arbitrary")),
    )(*recvs, w, m, v)


def _padded_rows(n, mult):
    rows = -(-n // PACK_COLS)
    return -(-rows // mult) * mult


def _pack(flat_pieces, mult, dtype, lead=()):
    flat = jnp.concatenate([p.astype(dtype) for p in flat_pieces], axis=-1)
    n = flat.shape[-1]
    rows = _padded_rows(n, mult)
    flat = jnp.pad(flat, [(0, 0)] * len(lead) + [(0, rows * PACK_COLS - n)])
    return flat.reshape(lead + (rows, PACK_COLS))


def _unpack(packed, shapes):
    flat = packed.reshape(-1)
    out, off = {}, 0
    for name, shape in shapes:
        size = 1
        for s in shape:
            size *= s
        out[name] = flat[off:off + size].reshape(shape)
        off += size
    return out


def _local_2d(arr):
    return arr.reshape(-1, arr.shape[-1])


SHARD_2D = {n: (s[-2] if len(s) > 2 else s[0], s[-1], a == len(s) - 1) for n, s, a in SHARDED}
SHARD_2D["gdn_conv_w"] = (GDN_CONV, GDN_CONV_DIM // N_DEV, True)


def _wire(name):
    rows, cols, _ = SHARD_2D[name]
    return BF16 if rows * cols >= 1 << 16 else F32


def _layer_tensors(i):
    kind, j = i % 3, i // 3
    pre = ("fox", "gla", "gdn")[kind]
    mix = [(pre + "_w_in", j), (pre + "_w_out", j)]
    if kind == 1:
        mix.append(("gla_w_alpha2", j))
    if kind == 2:
        mix.append(("gdn_conv_w", j))
    if i == 0:
        mix.append(("meta_tokens", None))
    return {"mix": mix, "ffn": [("w_gate_up", i), ("w_down", i)]}


PARTS = ("mix", "ffn")


def _layer_block(arr, idx):
    return arr if idx is None else _local_2d(arr[idx])


def _to_slabs(full2d, name):
    rows, cols, by_cols = SHARD_2D[name]
    if by_cols:
        parts = full2d if isinstance(full2d, tuple) else (full2d,)
        per = parts[0].shape[1] // cols
        return jnp.stack([parts[d // per][:, cols * (d % per):cols * (d % per + 1)] for d in range(N_DEV)])
    return full2d.reshape(N_DEV, rows, cols)


def _from_gathered(g, name):
    rows, cols, by_cols = SHARD_2D[name]
    if by_cols:
        return jnp.concatenate([g[d] for d in range(N_DEV)], axis=1)
    return g.reshape(N_DEV * rows, cols)


def _pad_cols(w, total):
    return jnp.pad(w, [(0, 0)] * (w.ndim - 1) + [(0, total - w.shape[-1])])


def _pad_lanes(v):
    return _pad_cols(v, LANE)


def _x(arr, width, cb0=0, moves=0):
    return (arr, width, cb0, moves)


def _rms_fwd(h, g):
    return _row_fwd(_rms_fn, [_x(h, D_MODEL)], [g], [], [(D_MODEL, D_MODEL, BF16)], "rms_fwd")[0]


def _rms_bwd(h, g, dy, add):
    dh, dg = _row_bwd(_rms_fn, [_x(h, D_MODEL)], [g], [], [_x(dy, D_MODEL)], [(D_MODEL, F32)], "rms_bwd", adds=[add])
    return dh, dg


SW_WIDTH = 256
SW_NCOL = D_FF // SW_WIDTH


def _ffn_gate_up(y, wgu):
    m, k = y.shape
    tm = _pick(m, (1408, 384, 128))

    def body(y_ref, wg_ref, wu_ref, a_ref, g_ref, u_ref):
        yv = y_ref[...]
        gt = _dot(yv, wg_ref[...], "nn")
        up = _dot(yv, wu_ref[...], "nn")
        a_ref[...] = (_silu(gt) * up).astype(a_ref.dtype)
        g_ref[...] = gt.astype(g_ref.dtype)
        u_ref[...] = up.astype(u_ref.dtype)

    tile = pl.BlockSpec((tm, SW_WIDTH), lambda i, j: (i, j))
    return pl.pallas_call(
        body, grid=(m // tm, SW_NCOL),
        in_specs=[pl.BlockSpec((tm, k), lambda i, j: (i, 0)),
                  pl.BlockSpec((k, SW_WIDTH), lambda i, j: (0, j)),
                  pl.BlockSpec((k, SW_WIDTH), lambda i, j: (0, SW_NCOL + j))],
        out_specs=[tile] * 3, out_shape=[jax.ShapeDtypeStruct((m, D_FF), BF16)] * 3,
        name="ffn_gate_up", compiler_params=_params(("parallel", "parallel")),
    )(y, wgu, wgu)


def _ffn_down_dx(dh2, wd, gt, up):
    m, k = dh2.shape
    tm = _pick(m, (1408, 384, 128))

    def body(d_ref, w_ref, g_ref, u_ref, dg_ref, du_ref):
        da = _dot(d_ref[...], w_ref[...], "nt")
        gv, uv = g_ref[...].astype(F32), u_ref[...].astype(F32)
        s = jax.nn.sigmoid(gv)
        dg_ref[...] = (da * uv * (s + gv * s * (1.0 - s))).astype(dg_ref.dtype)
        du_ref[...] = (da * gv * s).astype(du_ref.dtype)

    tile = pl.BlockSpec((tm, SW_WIDTH), lambda i, j: (i, j))
    return pl.pallas_call(
        body, grid=(m // tm, SW_NCOL),
        in_specs=[pl.BlockSpec((tm, k), lambda i, j: (i, 0)),
                  pl.BlockSpec((SW_WIDTH, k), lambda i, j: (j, 0)), tile, tile],
        out_specs=[tile] * 2, out_shape=[jax.ShapeDtypeStruct((m, D_FF), BF16)] * 2,
        name="ffn_down_dx", compiler_params=_params(("parallel", "parallel")),
    )(dh2, wd, gt, up)


def _ffn_fwd(h, g, wgu, wd):
    y = _rms_fwd(h, g)
    a, gt, up = _ffn_gate_up(y, wgu)
    h2 = _matmul(a, wd, "nn", F32, "ffn_down", add=h)
    return h2, (h, y, gt, up, a)


def _ffn_bwd(dh2, saved, g, wgu, wd):
    h, y, gt, up, a = saved
    dgt, dup = _ffn_down_dx(dh2, wd, gt, up)
    dwd = _matmul(a, dh2, "tn", F32, "ffn_down_dw")
    dwg = _matmul(y, dgt, "tn", F32, "ffn_gate_dw")
    dwu = _matmul(y, dup, "tn", F32, "ffn_up_dw")
    dy = _matmul(dgt, wgu, "nt", F32, "ffn_gate_dx", b_kblock=0)
    dy = _matmul(dup, wgu, "nt", F32, "ffn_up_dx", add=dy, b_kblock=1)
    dh, dg = _rms_bwd(h, g, dy, dh2)
    return dh, dg, (dwg, dwu), dwd


def _fox_consts():
    lane = jnp.arange(D_MODEL) // FOX_DH
    e_expand = (jnp.arange(LANE)[:, None] == lane[None, :]).astype(F32)
    e_mean = e_expand.T * (1.0 / FOX_DH)
    half = jnp.arange(LANE) // FOX_DH
    blockdiag = (half[:, None] == half[None, :]).astype(F32)
    return e_mean, e_expand, blockdiag


def _pair_layout(c):
    heads = c[:, :FOX_HEADS].T
    crow = heads.reshape(FOX_PAIRS, 2, ROWS)
    c2 = jnp.broadcast_to(crow.transpose(0, 2, 1)[:, :, :, None], (FOX_PAIRS, ROWS, 2, FOX_DH))
    crow = jnp.where(jnp.arange(ROWS) >= FIRST_ROW, crow, KEY_OFF)
    return c2.reshape(FOX_PAIRS, ROWS, LANE), crow


def _fox_fwd(y, p):
    proj = _matmul(y, p["w_in"], "nn", F32, "fox_in")
    e_mean, e_expand, _ = _fox_consts()
    nb = D_MODEL // LANE
    xs = [_x(proj, D_MODEL, 0), _x(proj, D_MODEL, 1), _x(proj, LANE, 4 * nb)]
    ps = [p["q_gain"], p["k_gain"], p["b_f"]]
    qn, kn, lf = _row_fwd(_fox_pre_fn, xs, ps, [e_mean, e_expand],
                          [(D_MODEL, D_MODEL, BF16), (D_MODEL, D_MODEL, BF16), (LANE, LANE, F32)], "fox_pre_fwd")
    c = _seq_cumsum(lf, False, "fox_cumsum")
    c2, crow = _pair_layout(c)
    o, lse2 = _fox_attn_fwd(qn, kn, proj, c2, crow)
    om = _row_fwd(_fox_post_fn, [_x(o, D_MODEL, 0), _x(proj, D_MODEL, 3)], [], [],
                  [(D_MODEL, D_MODEL, BF16)], "fox_post_fwd")[0]
    return om, (proj, qn, kn, c2, crow, o, lse2)


def _fox_bwd(dom, y, saved, p):
    proj, qn, kn, c2, crow, o, lse2 = saved
    e_mean, e_expand, blockdiag = _fox_consts()
    nb = D_MODEL // LANE
    do, dgate = _row_bwd(_fox_post_fn, [_x(o, D_MODEL, 0), _x(proj, D_MODEL, 3)], [], [], [_x(dom, D_MODEL)],
                         [(D_MODEL, F32), (D_MODEL, BF16)], "fox_post_bwd")
    dq, dk, dv, dcrow, dcq2 = _fox_attn_bwd(qn, kn, proj, c2, crow, lse2, o, do, blockdiag)
    dlf = _seq_cumsum(_pad_cols(dcrow.reshape(FOX_HEADS, ROWS).T, LANE), True, "fox_cumsum_rev", pairs=dcq2)
    xs = [_x(proj, D_MODEL, 0), _x(proj, D_MODEL, 1), _x(proj, LANE, 4 * nb)]
    ps = [p["q_gain"], p["k_gain"], p["b_f"]]
    dqr, dkr, df, dqg, dkg, dbf = _row_bwd(
        _fox_pre_fn, xs, ps, [e_mean, e_expand], [_x(dq, D_MODEL), _x(dk, D_MODEL), _x(dlf, LANE)],
        [(D_MODEL, BF16), (D_MODEL, BF16), (LANE, BF16)], "fox_pre_bwd")
    dproj = jnp.concatenate([dqr, dkr, dv, dgate, df], axis=1)
    grads = {
        "w_in": _matmul(y, dproj, "tn", F32, "fox_in_dw")[:, :FOX_IN],
        "q_gain": dqg.reshape(FOX_HEADS, FOX_DH).sum(axis=0),
        "k_gain": dkg.reshape(FOX_HEADS, FOX_DH).sum(axis=0),
        "b_f": dbf[0, :FOX_HEADS],
    }
    return _matmul(dproj, p["w_in"], "nt", F32, "fox_in_dx"), grads


def _gla_fwd(y, p):
    proj = _matmul(y, p["w_in"], "nn", F32, "gla_in")
    small = (2 * GLA_QK + 2 * GLA_V) // LANE
    garr = _row_fwd(_gla_pre_fn, [_x(proj, LANE, small)], [p["w_alpha"], p["b_alpha"]], [],
                    [(GLA_QK, GLA_QK, F32)], "gla_pre_fwd")[0]
    o, states = _gla_scan_fwd(proj, garr)
    om = _row_fwd(_gla_post_fn, [_x(o, GLA_V, 0), _x(proj, GLA_V, 2)], [p["o_gain"]], [],
                  [(GLA_V, GLA_V, BF16)], "gla_post_fwd")[0]
    return om, (proj, garr, states, o)


def _gla_bwd(dom, y, saved, p):
    proj, garr, states, o = saved
    small = (2 * GLA_QK + 2 * GLA_V) // LANE
    do, dr, dgain = _row_bwd(_gla_post_fn, [_x(o, GLA_V, 0), _x(proj, GLA_V, 2)], [p["o_gain"]], [],
                             [_x(dom, GLA_V)], [(GLA_V, F32), (GLA_V, BF16)], "gla_post_bwd")
    dq, dk, dv, dg = _gla_scan_bwd(proj, garr, states, do)
    dalr, dwa, dba = _row_bwd(_gla_pre_fn, [_x(proj, LANE, small)], [p["w_alpha"], p["b_alpha"]], [],
                              [_x(dg, GLA_QK)], [(LANE, BF16)], "gla_pre_bwd")
    dproj = jnp.concatenate([dq, dk, dv, dr, dalr], axis=1)
    grads = {
        "w_in": _matmul(y, dproj, "tn", F32, "gla_in_dw")[:, :GLA_IN],
        "w_alpha": dwa[:GLA_RANK], "b_alpha": dba[0], "o_gain": dgain[0],
    }
    return _matmul(dproj, p["w_in"], "nt", F32, "gla_in_dx"), grads


def _gdn_fwd(y, p):
    proj = _matmul(y, p["w_in"], "nn", F32, "gdn_in")
    nb = D_MODEL // LANE
    qn = _gdn_conv_fwd(proj, p["conv_q"], "q", 0)
    kn = _gdn_conv_fwd(proj, p["conv_k"], "k", 1)
    vn = _gdn_conv_fwd(proj, p["conv_v"], "v", 2)
    gates = _row_fwd(_gdn_gates_fn, [_x(proj, LANE, 4 * nb)], [p["a_log"], p["dt_bias"]], [],
                     [(LANE, LANE, F32)], "gdn_gates_fwd")[0]
    o, states = _gdn_scan_fwd(qn, kn, vn, gates)
    om = _row_fwd(_gdn_post_fn, [_x(o, D_MODEL, 0), _x(proj, D_MODEL, 3)], [p["o_gain"]], [],
                  [(D_MODEL, D_MODEL, BF16)], "gdn_post_fwd")[0]
    return om, (proj, qn, kn, vn, gates, states, o)


def _gdn_bwd(dom, y, saved, p):
    proj, qn, kn, vn, gates, states, o = saved
    nb = D_MODEL // LANE
    do, dgate, dgain = _row_bwd(_gdn_post_fn, [_x(o, D_MODEL, 0), _x(proj, D_MODEL, 3)], [p["o_gain"]], [],
                                [_x(dom, D_MODEL)], [(D_MODEL, F32), (D_MODEL, BF16)], "gdn_post_bwd")
    dqn, dkn, dvn, dgates = _gdn_scan_bwd(qn, kn, vn, gates, states, do)
    dsm, dalog, ddtb = _row_bwd(_gdn_gates_fn, [_x(proj, LANE, 4 * nb)], [p["a_log"], p["dt_bias"]], [],
                                [_x(dgates, LANE)], [(LANE, BF16)], "gdn_gates_bwd")
    dxq, dwq = _gdn_conv_bwd(proj, p["conv_q"], "q", 0, dqn)
    dxk, dwk = _gdn_conv_bwd(proj, p["conv_k"], "k", 1, dkn)
    dxv, dwv = _gdn_conv_bwd(proj, p["conv_v"], "v", 2, dvn)
    dproj = jnp.concatenate([dxq, dxk, dxv, dgate, dsm], axis=1)
    grads = {
        "w_in": _matmul(y, dproj, "tn", F32, "gdn_in_dw")[:, :GDN_IN],
        "conv_w": jnp.concatenate([dwq[:GDN_CONV], dwk[:GDN_CONV], dwv[:GDN_CONV]], axis=1),
        "a_log": dalog[0, :GDN_HEADS], "dt_bias": ddtb[0, :GDN_HEADS], "o_gain": dgain[0],
    }
    return _matmul(dproj, p["w_in"], "nt", F32, "gdn_in_dx"), grads


_MIXERS = ((_fox_fwd, _fox_bwd), (_gla_fwd, _gla_bwd), (_gdn_fwd, _gdn_bwd))


def _mixer_params(i, wts, rep):
    kind, j = i % 3, i // 3
    if kind == 0:
        return {
            "w_in": _pad_cols(wts["fox_w_in"], FOX_IN_PAD), "w_out": wts["fox_w_out"],
            "q_gain": jnp.tile(rep["fox_q_gain"][j], FOX_HEADS)[None],
            "k_gain": jnp.tile(rep["fox_k_gain"][j], FOX_HEADS)[None],
            "b_f": _pad_lanes(rep["fox_b_f"][j][None]),
        }
    if kind == 1:
        return {
            "w_in": _pad_cols(wts["gla_w_in"], GLA_IN_PAD), "w_out": wts["gla_w_out"],
            "w_alpha": jnp.pad(wts["gla_w_alpha2"], ((0, LANE - GLA_RANK), (0, 0))).astype(F32),
            "b_alpha": rep["gla_b_alpha"][j][None], "o_gain": rep["gla_o_gain"][j][None],
        }
    conv = jnp.pad(wts["gdn_conv_w"].astype(F32), ((0, 8 - GDN_CONV), (0, 0)))
    return {
        "w_in": _pad_cols(wts["gdn_w_in"], GDN_IN_PAD), "w_out": wts["gdn_w_out"],
        "conv_q": conv[:, :D_MODEL], "conv_k": conv[:, D_MODEL:2 * D_MODEL], "conv_v": conv[:, 2 * D_MODEL:],
        "a_log": _pad_lanes(rep["gdn_a_log"][j][None]), "dt_bias": _pad_lanes(rep["gdn_dt_bias"][j][None]),
        "o_gain": rep["gdn_o_gain"][j][None],
    }


def kernel(x, meta_tokens, norm_mix, norm_ffn, w_gate_up, w_down, fox_w_in, fox_b_f, fox_q_gain, fox_k_gain, fox_w_out, gla_w_in, gla_w_alpha2, gla_b_alpha, gla_o_gain, gla_w_out, gdn_w_in, gdn_conv_w, gdn_a_log, gdn_dt_bias, gdn_o_gain, gdn_w_out, loss_target, m_meta_tokens, m_norm_mix, m_norm_ffn, m_w_gate_up, m_w_down, m_fox_w_in, m_fox_b_f, m_fox_q_gain, m_fox_k_gain, m_fox_w_out, m_gla_w_in, m_gla_w_alpha2, m_gla_b_alpha, m_gla_o_gain, m_gla_w_out, m_gdn_w_in, m_gdn_conv_w, m_gdn_a_log, m_gdn_dt_bias, m_gdn_o_gain, m_gdn_w_out, v_meta_tokens, v_norm_mix, v_norm_ffn, v_w_gate_up, v_w_down, v_fox_w_in, v_fox_b_f, v_fox_q_gain, v_fox_k_gain, v_fox_w_out, v_gla_w_in, v_gla_w_alpha2, v_gla_b_alpha, v_gla_o_gain, v_gla_w_out, v_gdn_w_in, v_gdn_conv_w, v_gdn_a_log, v_gdn_dt_bias, v_gdn_o_gain, v_gdn_w_out):
    given = dict(locals())
    w_loc = {n: given[n] for n in WEIGHT_ORDER}
    m_loc = {n: given["m_" + n] for n in WEIGHT_ORDER}
    v_loc = {n: given["v_" + n] for n in WEIGHT_ORDER}

    rep = {n: w_loc[n] for n, _ in REPLICATED}
    me = 4 * lax.axis_index("x") + 2 * lax.axis_index("y") + lax.axis_index("c")

    def with_own(lands, owns):
        return [lax.dynamic_update_slice(land, own[None], (me, 0, 0)) for land, own in zip(lands, owns)]

    gathers = {}
    order = jnp.zeros((), F32)
    for i in range(DEPTH):
        groups = _layer_tensors(i)
        for part in PARTS:
            blocks = [(_layer_block(w_loc[n], idx) + order).astype(_wire(n)) for n, idx in groups[part]]
            started = _send_start(blocks, True, "gather_start_%d_%s" % (i, part))
            gathers[(i, part)] = (groups[part], blocks, started)
            order = started[4][0, 0]
    all_started = gathers[(DEPTH - 1, PARTS[-1])][2][4]

    def weights(i, part, after):
        tensors, blocks, started = gathers[(i, part)]
        lands = with_own(_send_wait(started, True, after, "gather_wait_%d_%s" % (i, part)), blocks)
        return {n: _from_gathered(g, n) for (n, _), g in zip(tensors, lands)}

    h = None
    saved = []
    for i in range(DEPTH):
        w_mix = weights(i, "mix", all_started if i == 0 else h)
        if i == 0:
            h = jnp.concatenate([jnp.zeros((FIRST_ROW, D_MODEL), F32), w_mix["meta_tokens"], x[0]], axis=0)
        p = _mixer_params(i, w_mix, rep)
        fwd, _ = _MIXERS[i % 3]
        g_mix = rep["norm_mix"][i][None]
        y = _rms_fwd(h, g_mix)
        om, mix_saved = fwd(y, p)
        h_mid = _matmul(om, p["w_out"], "nn", F32, "mixer_out", add=h)
        w_ffn = weights(i, "ffn", h_mid)
        g_ffn = rep["norm_ffn"][i][None]
        h_out, ffn_saved = _ffn_fwd(h_mid, g_ffn, w_ffn["w_gate_up"], w_ffn["w_down"])
        saved.append((p, w_ffn, h, y, om, mix_saved, ffn_saved))
        h = h_out

    dh, sq = _loss_head(h, loss_target[0])
    loss = lax.psum(0.5 * jnp.sum(sq) * (1.0 / D_MODEL), ("x", "y", "c"))

    gw = {n: [None] * DEPTH for n in ("norm_mix", "norm_ffn")}
    gm = {}
    reduces = {}

    def start_reduce(i, part, full, extra=()):
        tensors = _layer_tensors(i)[part]
        slabs = [_to_slabs(full[n], n).astype(_wire(n)) for n, _ in tensors] + list(extra)
        started = _send_start(slabs, False, "reduce_start_%d_%s" % (i, part))
        reduces[(i, part)] = (tensors, slabs, started)
        return started[4][0, 0]

    order = jnp.zeros((), BF16)
    for i in reversed(range(DEPTH)):
        p, w_ffn, h_in, y, om, mix_saved, ffn_saved = saved[i]
        kind, j = i % 3, i // 3
        pre = ("fox", "gla", "gdn")[kind]
        _, bwd = _MIXERS[kind]
        dh_mid, gw["norm_ffn"][i], dwgu, dwd = _ffn_bwd(
            dh, ffn_saved, rep["norm_ffn"][i][None], w_ffn["w_gate_up"], w_ffn["w_down"] + order)
        order = start_reduce(i, "ffn", {"w_gate_up": dwgu, "w_down": dwd}).astype(BF16)
        dom = _matmul(dh_mid, p["w_out"] + order, "nt", F32, "mixer_out_dx")
        dwo = _matmul(om, dh_mid, "tn", F32, "mixer_out_dw")
        dy, grads = bwd(dom, y, mix_saved, p)
        gm[(kind, j)] = grads
        dh, gw["norm_mix"][i] = _rms_bwd(h_in, rep["norm_mix"][i][None], dy, dh_mid)
        mix_full = {pre + "_w_in": grads["w_in"], pre + "_w_out": dwo, "gla_w_alpha2": grads.get("w_alpha"),
                    "gdn_conv_w": grads.get("conv_w"), "meta_tokens": dh[FIRST_ROW:PAD_ROWS]}
        if i > 0:
            order = start_reduce(i, "mix", mix_full).astype(BF16)

    partial = {
        "norm_mix": jnp.concatenate(gw["norm_mix"], axis=0), "norm_ffn": jnp.concatenate(gw["norm_ffn"], axis=0),
        "fox_b_f": jnp.stack([gm[(0, 0)]["b_f"], gm[(0, 1)]["b_f"]]),
        "fox_q_gain": jnp.stack([gm[(0, 0)]["q_gain"], gm[(0, 1)]["q_gain"]]),
        "fox_k_gain": jnp.stack([gm[(0, 0)]["k_gain"], gm[(0, 1)]["k_gain"]]),
        "gla_b_alpha": gm[(1, 0)]["b_alpha"][None], "gla_o_gain": gm[(1, 0)]["o_gain"][None],
        "gdn_a_log": gm[(2, 0)]["a_log"][None], "gdn_dt_bias": gm[(2, 0)]["dt_bias"][None],
        "gdn_o_gain": gm[(2, 0)]["o_gain"][None],
    }

    small = _pack([partial[n].reshape(-1) for n, _ in REPLICATED], 8, F32)
    start_reduce(0, "mix", mix_full, extra=[jnp.broadcast_to(small[None], (N_DEV,) + small.shape)])
    recv = {}
    for (i, part), (tensors, slabs, started) in reduces.items():
        owns = [lax.dynamic_index_in_dim(s, me, 0, keepdims=False) for s in slabs]
        lands = with_own(_send_wait(started, False, dh, "reduce_wait_%d_%s" % (i, part)), owns)
        for (n, idx), land in zip(tensors, lands):
            recv[(n, idx)] = land
        if (i, part) == (0, "mix"):
            recv_small = lands[-1]

    res = [{} for _ in range(4)]
    for n, s, _ in SHARDED:
        layers = [idx for i in range(DEPTH) for part in PARTS for (nn, idx) in _layer_tensors(i)[part] if nn == n]
        layers = sorted(layers, key=lambda idx: -1 if idx is None else idx)
        outs = _adamw([recv[(n, idx)] for idx in layers], *[_local_2d(d[n]) for d in (w_loc, m_loc, v_loc)],
                      "adamw_" + n)
        for k in range(4):
            res[k][n] = outs[k].reshape(s)
    outs_small = _adamw([recv_small], *[_pack([d[n].reshape(-1) for n, _ in REPLICATED], 8, F32)
                                        for d in (w_loc, m_loc, v_loc)], "adamw_replicated")
    for k in range(4):
        res[k].update(_unpack(outs_small[k], list(REPLICATED)))

    result = [loss, dh[PAD_ROWS:][None]]
    for k in range(4):
        result += [res[k][n] for n in WEIGHT_ORDER]
    return tuple(result)
```

```python
import functools

import jax
import jax.numpy as jnp
from jax import lax
from jax.experimental import pallas as pl
from jax.experimental.pallas import tpu as pltpu

F32, BF16 = jnp.float32, jnp.bfloat16
HIGHEST = lax.Precision.HIGHEST

D_MODEL = 1024
SEQ = 4096
N_META = 16
DEPTH = 4
NORM_EPS = 1e-6
LANE = 128
PAD_ROWS = 128
FIRST_ROW = PAD_ROWS - N_META
ROWS = PAD_ROWS + SEQ
ROW_TILE = 128
CHUNK = 64
NEG = -1e30
N_DEV = 8
VMEM_LIMIT = 56 * 1024 * 1024
VMEM_BLOCK_BUDGET = 36 * 1024 * 1024

FOX_HEADS, FOX_DH = 16, 64
FOX_IN, FOX_IN_PAD = 4 * D_MODEL + FOX_HEADS, 4 * D_MODEL + LANE
GLA_HEADS, GLA_DK, GLA_DV, GLA_RANK = 4, 128, 256, 16
GLA_QK, GLA_V = GLA_HEADS * GLA_DK, GLA_HEADS * GLA_DV
GLA_IN, GLA_IN_PAD = 2 * GLA_QK + 2 * GLA_V + GLA_RANK, 2 * GLA_QK + 2 * GLA_V + LANE
GDN_HEADS, GDN_DK, GDN_DV, GDN_CONV = 8, 128, 128, 4
GDN_CONV_DIM = 3 * GDN_HEADS * GDN_DK
GDN_IN, GDN_IN_PAD = GDN_CONV_DIM + GDN_HEADS * GDN_DV + 2 * GDN_HEADS, GDN_CONV_DIM + GDN_HEADS * GDN_DV + LANE
D_FF = 2816

ADAM_LR, ADAM_B1, ADAM_B2, ADAM_EPS, ADAM_WD, ADAM_STEP = 0.001, 0.9, 0.999, 1e-08, 0.01, 10

PACK_COLS = 1024
PACK_ROW_TILE = 256

SHARDED = (
    ("meta_tokens", (16, 128), 1),
    ("w_gate_up", (4, 1024, 704), 2),
    ("w_down", (4, 352, 1024), 1),
    ("fox_w_in", (2, 1024, 514), 2),
    ("fox_w_out", (2, 128, 1024), 1),
    ("gla_w_in", (1, 1024, 386), 2),
    ("gla_w_alpha2", (1, 16, 64), 2),
    ("gla_w_out", (1, 128, 1024), 1),
    ("gdn_w_in", (1, 1024, 514), 2),
    ("gdn_conv_w", (1, 4, 1, 384), 3),
    ("gdn_w_out", (1, 128, 1024), 1),
)
REPLICATED = (
    ("norm_mix", (4, 1024)),
    ("norm_ffn", (4, 1024)),
    ("fox_b_f", (2, 16)),
    ("fox_q_gain", (2, 64)),
    ("fox_k_gain", (2, 64)),
    ("gla_b_alpha", (1, 512)),
    ("gla_o_gain", (1, 256)),
    ("gdn_a_log", (1, 8)),
    ("gdn_dt_bias", (1, 8)),
    ("gdn_o_gain", (1, 128)),
)
WEIGHT_ORDER = (
    "meta_tokens", "norm_mix", "norm_ffn", "w_gate_up", "w_down", "fox_w_in", "fox_b_f", "fox_q_gain",
    "fox_k_gain", "fox_w_out", "gla_w_in", "gla_w_alpha2", "gla_b_alpha", "gla_o_gain", "gla_w_out",
    "gdn_w_in", "gdn_conv_w", "gdn_a_log", "gdn_dt_bias", "gdn_o_gain", "gdn_w_out",
)

_DN = {
    "nn": (((1,), (0,)), ((), ())),
    "nt": (((1,), (1,)), ((), ())),
    "tn": (((0,), (0,)), ((), ())),
}


def _dot(a, b, kind):
    return lax.dot_general(a.astype(BF16), b.astype(BF16), _DN[kind], preferred_element_type=F32)


@functools.partial(jax.custom_vjp, nondiff_argnums=(2,))
def _dot_ad(a, b, kind):
    return _dot(a, b, kind)


def _dot_ad_fwd(a, b, kind):
    return _dot(a, b, kind), (a, b)


def _dot_ad_bwd(kind, res, ct):
    a, b = res
    if kind == "nn":
        return _dot(ct, b, "nt"), _dot(a, ct, "tn")
    if kind == "nt":
        return _dot(ct, b, "nn"), _dot(ct, a, "tn")
    return _dot(b, ct, "nt"), _dot(a, ct, "nn")


_dot_ad.defvjp(_dot_ad_fwd, _dot_ad_bwd)


def _dot_hi(a, b, kind):
    return lax.dot_general(a, b, _DN[kind], precision=HIGHEST, preferred_element_type=F32)


def _dot_mid(a, b, kind):
    return lax.dot_general(a, b, _DN[kind], precision=lax.Precision.HIGH, preferred_element_type=F32)


def _iota(shape, dim):
    return lax.broadcasted_iota(jnp.int32, shape, dim)


def _row_mask(i, tm):
    return ((i * tm + _iota((tm, 1), 0)) >= FIRST_ROW).astype(F32)


def _log_sigmoid(z):
    return jnp.minimum(z, 0.0) - jnp.log(1.0 + jnp.exp(-jnp.abs(z)))


def _softplus(z):
    return jnp.maximum(z, 0.0) + jnp.log(1.0 + jnp.exp(-jnp.abs(z)))


def _silu(z):
    return z * jax.nn.sigmoid(z)


def _lane_col(x, lane):
    return jnp.sum(jnp.where(_iota(x.shape, 1) == lane, x, 0.0), axis=1, keepdims=True)


def _last_row(x):
    return jnp.sum(jnp.where(_iota(x.shape, 0) == x.shape[0] - 1, x, 0.0), axis=0, keepdims=True)


def _params(sem, limit=VMEM_LIMIT):
    return pltpu.CompilerParams(dimension_semantics=sem, vmem_limit_bytes=limit)


def _pick(n, options):
    for t in options:
        if n % t == 0:
            return t
    return n


def _matmul(a, b, kind, out_dtype, name, add=None, b_kblock=0):
    if kind == "nn":
        (m, k), n = a.shape, b.shape[1]
    elif kind == "nt":
        (m, k), n = a.shape, b.shape[0]
    else:
        (k, m), n = a.shape, b.shape[1]
    tn = _pick(n, (512, 384, 640, 256, 128))
    tm_options = (1408, 384, 128) if kind != "tn" else (1024, 512, 256, 128)
    tm = None
    for t in tm_options:
        if m % t:
            continue
        need = 2 * (t * k * a.dtype.itemsize + tn * k * b.dtype.itemsize + t * tn * 8)
        if need <= VMEM_BLOCK_BUDGET:
            tm = t
            break
    assert tm is not None, (name, a.shape, b.shape)
    a_bytes, b_bytes = m * k * a.dtype.itemsize, n * k * b.dtype.itemsize
    m_outer = a_bytes + (m // tm) * b_bytes <= b_bytes + (n // tn) * a_bytes

    def ij(fn):
        return (lambda i, j: fn(i, j)) if m_outer else (lambda j, i: fn(i, j))

    def body(*refs):
        a_ref, b_ref = refs[0], refs[1]
        o_ref = refs[-1]
        acc = _dot(a_ref[...], b_ref[...], kind)
        if add is not None:
            acc = acc + refs[2][...]
        o_ref[...] = acc.astype(o_ref.dtype)

    if kind == "tn":
        a_spec = pl.BlockSpec((k, tm), ij(lambda i, j: (0, i)))
    else:
        a_spec = pl.BlockSpec((tm, k), ij(lambda i, j: (i, 0)))
    if kind == "nt":
        b_spec = pl.BlockSpec((tn, k), ij(lambda i, j: (j, b_kblock)))
    else:
        b_spec = pl.BlockSpec((k, tn), ij(lambda i, j: (0, j)))
    o_spec = pl.BlockSpec((tm, tn), ij(lambda i, j: (i, j)))
    operands, in_specs = [a, b], [a_spec, b_spec]
    if add is not None:
        operands.append(add)
        in_specs.append(o_spec)
    grid = (m // tm, n // tn) if m_outer else (n // tn, m // tm)
    return pl.pallas_call(
        body, grid=grid, in_specs=in_specs, out_specs=o_spec,
        out_shape=jax.ShapeDtypeStruct((m, n), out_dtype), name=name,
        compiler_params=_params(("parallel", "parallel")),
    )(*operands)


def _x_spec(tm, width, cb0, moves):
    return pl.BlockSpec((tm, width), lambda i, j: (i, cb0 + j * moves))


def _full_spec(arr):
    nd = arr.ndim
    return pl.BlockSpec(arr.shape, lambda i, j: (0,) * nd)


def _row_fwd(fn, xs, ps, cs, outs, name, tm=ROW_TILE, ncol=1, rows=None):
    rows = ROWS if rows is None else rows
    nx, npar, nc = len(xs), len(ps), len(cs)

    def body(*refs):
        i = pl.program_id(0)
        xv = [r[...].astype(F32) for r in refs[:nx]]
        pv = [r[...] for r in refs[nx:nx + npar + nc]]
        res = fn(i, *xv, *pv)
        for r, val in zip(refs[nx + npar + nc:], res):
            r[...] = val.astype(r.dtype)

    in_specs = [_x_spec(tm, w, cb0, mv) for (_, w, cb0, mv) in xs]
    in_specs += [_full_spec(p) for p in list(ps) + list(cs)]
    out_specs = [_x_spec(tm, w, 0, 1) for (w, _, _) in outs]
    out_shape = [jax.ShapeDtypeStruct((rows, tot), dt) for (_, tot, dt) in outs]
    return pl.pallas_call(
        body, grid=(rows // tm, ncol), in_specs=in_specs, out_specs=out_specs, out_shape=out_shape,
        name=name, compiler_params=_params(("parallel", "parallel")),
    )(*[x[0] for x in xs], *ps, *cs)


def _row_bwd(fn, xs, ps, cs, cts, dxs, name, adds=None, tm=ROW_TILE, ncol=1, rows=None):
    rows = ROWS if rows is None else rows
    nx, npar, nc, nct = len(xs), len(ps), len(cs), len(cts)
    adds = [None] * nx if adds is None else adds
    add_idx = [k for k in range(nx) if adds[k] is not None]

    def body(*refs):
        i, j = pl.program_id(0), pl.program_id(1)
        pos = 0
        x_refs = refs[pos:pos + nx]; pos += nx
        p_refs = refs[pos:pos + npar]; pos += npar
        c_refs = refs[pos:pos + nc]; pos += nc
        ct_refs = refs[pos:pos + nct]; pos += nct
        add_refs = refs[pos:pos + len(add_idx)]; pos += len(add_idx)
        dx_refs = refs[pos:pos + nx]; pos += nx
        dp_refs = refs[pos:pos + npar]
        xv = [r[...].astype(F32) for r in x_refs]
        pv = [r[...] for r in p_refs]
        cv = [r[...] for r in c_refs]
        _, vjp = jax.vjp(lambda *args: fn(i, *args, *cv), *xv, *pv)
        grads = vjp(tuple(r[...].astype(F32) for r in ct_refs))
        for k in range(nx):
            val = grads[k]
            if adds[k] is not None:
                val = val + add_refs[add_idx.index(k)][...]
            dx_refs[k][...] = val.astype(dx_refs[k].dtype)
        if npar:
            @pl.when((i == 0) & (j == 0))
            def _():
                for r in dp_refs:
                    r[...] = jnp.zeros(r.shape, r.dtype)

            for k in range(npar):
                dp_refs[k][...] += grads[nx + k]

    in_specs = [_x_spec(tm, w, cb0, mv) for (_, w, cb0, mv) in xs]
    in_specs += [_full_spec(p) for p in list(ps) + list(cs)]
    in_specs += [_x_spec(tm, w, cb0, mv) for (_, w, cb0, mv) in cts]
    in_specs += [_x_spec(tm, xs[k][1], 0, 1) for k in add_idx]
    out_specs = [_x_spec(tm, xs[k][1], 0, 1) for k in range(nx)] + [_full_spec(p) for p in ps]
    out_shape = [jax.ShapeDtypeStruct((rows, tot), dt) for (tot, dt) in dxs]
    out_shape += [jax.ShapeDtypeStruct(p.shape, F32) for p in ps]
    sem = ("arbitrary", "arbitrary") if npar else ("parallel", "parallel")
    return pl.pallas_call(
        body, grid=(rows // tm, ncol), in_specs=in_specs, out_specs=out_specs, out_shape=out_shape,
        name=name, compiler_params=_params(sem),
    )(*[x[0] for x in xs], *ps, *cs, *[c[0] for c in cts], *[adds[k] for k in add_idx])


def _rms_fn(i, h, g):
    return (h * lax.rsqrt(jnp.mean(h * h, axis=1, keepdims=True) + NORM_EPS) * g,)


def _fox_pre_fn(i, q, k, f, qg, kg, bf, e_mean, e_expand):
    tm = q.shape[0]

    def headnorm(x, gain, scale):
        msq = _dot_mid(x * x, e_mean, "nn")
        r = _dot_mid(lax.rsqrt(msq + NORM_EPS), e_expand, "nn")
        return x * r * gain * scale

    qn = headnorm(q, qg, FOX_DH ** -0.5)
    kn = headnorm(k, kg, 1.0)
    keep = (_iota((tm, LANE), 1) < FOX_HEADS).astype(F32) * _row_mask(i, tm)
    return qn, kn, _log_sigmoid(f + bf) * keep


def _fox_post_fn(i, o, gate):
    return (o * jax.nn.sigmoid(gate) * _row_mask(i, o.shape[0]),)


def _gla_pre_fn(i, alr, wa, ba):
    z = _dot_ad(alr, wa, "nn") + ba
    return (_log_sigmoid(z) * (1.0 / 16.0) * _row_mask(i, alr.shape[0]),)


def _headwise_norm_gate(o, gate, gain, dv, mask):
    pieces = []
    for h in range(o.shape[1] // dv):
        oh = o[:, h * dv:(h + 1) * dv]
        pieces.append(oh * lax.rsqrt(jnp.mean(oh * oh, axis=1, keepdims=True) + NORM_EPS) * gain)
    return jnp.concatenate(pieces, axis=1) * _silu(gate) * mask


def _gla_post_fn(i, o, r, gain):
    return (_headwise_norm_gate(o, r, gain, GLA_DV, _row_mask(i, o.shape[0])),)


def _gdn_post_fn(i, o, gate, gain):
    return (_headwise_norm_gate(o, gate, gain, GDN_DV, _row_mask(i, o.shape[0])),)


def _gdn_gates_fn(i, sm, alog, dtb):
    tm = sm.shape[0]
    lane = _iota((tm, LANE), 1)
    g = -jnp.exp(alog) * _softplus(sm + dtb)
    beta = jax.nn.sigmoid(sm)
    out = jnp.where(lane < GDN_HEADS, g, jnp.where(lane < 2 * GDN_HEADS, beta, 0.0))
    return (out * _row_mask(i, tm),)


def _gdn_conv_fn(kind):
    def fn(i, prev, cur, w):
        tm = cur.shape[0]
        xw = jnp.concatenate([prev * jnp.where(i > 0, 1.0, 0.0), cur], axis=0)
        t = _iota((tm, 2 * tm), 0)
        s = _iota((tm, 2 * tm), 1)
        y = jnp.zeros(cur.shape, F32)
        for j in range(GDN_CONV):
            sel = (s == t + (tm - (GDN_CONV - 1) + j)).astype(F32)
            y = y + _dot_mid(sel, xw, "nn") * jnp.sum(jnp.where(_iota(w.shape, 0) == j, w, 0.0), axis=0, keepdims=True)
        a = _silu(y * _row_mask(i, tm))
        if kind == "v":
            return (a,)
        scale = GDN_DK ** -0.5 if kind == "q" else 1.0
        pieces = []
        for h in range(GDN_HEADS):
            ah = a[:, h * GDN_DK:(h + 1) * GDN_DK]
            pieces.append(ah * lax.rsqrt(jnp.sum(ah * ah, axis=1, keepdims=True) + NORM_EPS) * scale)
        return (jnp.concatenate(pieces, axis=1),)

    return fn


def _seq_cumsum(x, reverse, name, pairs=None):
    rows, width = x.shape
    nt = rows // ROW_TILE

    def body(*refs):
        x_ref, o_ref, carry = refs[0], refs[-2], refs[-1]

        @pl.when(pl.program_id(0) == 0)
        def _():
            carry[...] = jnp.zeros(carry.shape, F32)

        xv = x_ref[...]
        if pairs is not None:
            lane = _iota(xv.shape, 1)
            for g in range(pairs.shape[0]):
                ta, tb = _pair_cols(refs[1][g])
                xv = xv + jnp.where(lane == 2 * g, ta, 0.0) + jnp.where(lane == 2 * g + 1, tb, 0.0)
        r = _iota((ROW_TILE, ROW_TILE), 0)
        c = _iota((ROW_TILE, ROW_TILE), 1)
        tri = ((c >= r) if reverse else (c <= r)).astype(F32)
        acc = _dot_hi(tri, xv, "nn") + carry[...]
        o_ref[...] = acc
        edge = 0 if reverse else ROW_TILE - 1
        carry[...] = jnp.sum(jnp.where(_iota(acc.shape, 0) == edge, acc, 0.0), axis=0, keepdims=True)

    idx = (lambda i: (nt - 1 - i, 0)) if reverse else (lambda i: (i, 0))
    idx3 = (lambda i: (0, nt - 1 - i, 0)) if reverse else (lambda i: (0, i, 0))
    operands, in_specs = [x], [pl.BlockSpec((ROW_TILE, width), idx)]
    if pairs is not None:
        operands.append(pairs)
        in_specs.append(pl.BlockSpec((pairs.shape[0], ROW_TILE, LANE), idx3))
    return pl.pallas_call(
        body, grid=(nt,), in_specs=in_specs,
        out_specs=pl.BlockSpec((ROW_TILE, width), idx), out_shape=jax.ShapeDtypeStruct(x.shape, F32),
        scratch_shapes=[pltpu.VMEM((1, width), F32)], name=name, compiler_params=_params(("arbitrary",)),
    )(*operands)


FOX_PAIRS = FOX_HEADS // 2
FOX_V_BLOCK = 2 * D_MODEL // LANE


def _pair_cols(x):
    return _lane_col(x, 0), _lane_col(x, FOX_DH)


ATT_TILE = 384
assert ROWS % ATT_TILE == 0
N_ATT = ROWS // ATT_TILE
KEY_OFF = 1e30


def _fox_scores(qa, qb, kj, ca, cb, cra, crb, mask):
    sa = _dot(qa, kj, "nt") + (ca - cra)
    sb = _dot(qb, kj, "nt") + (cb - crb)
    if mask is None:
        return sa, sb
    return jnp.where(mask, sa, NEG), jnp.where(mask, sb, NEG)


def _fox_diag_mask():
    return _iota((ATT_TILE, ATT_TILE), 1) <= _iota((ATT_TILE, ATT_TILE), 0)


def _att_off(j):
    return pl.multiple_of(j * ATT_TILE, ATT_TILE)


def _fox_attn_fwd(qn, kn, proj, c2, crow):
    nt = ROWS // ROW_TILE

    def body(q_ref, k_ref, v_ref, c2_ref, crow_ref, o_ref, lse_ref):
        i = pl.program_id(1)
        q = q_ref[...]
        is_a = _iota((ATT_TILE, LANE), 1) < FOX_DH
        qa, qb = jnp.where(is_a, q, 0), jnp.where(is_a, 0, q)
        ca, cb = _pair_cols(c2_ref[0])

        def block(off, size, mask, carry):
            ma, la, acca, mb, lb, accb = carry
            kj = k_ref[pl.ds(off, size), :]
            vj = v_ref[pl.ds(off, size), :].astype(BF16)
            sa, sb = _fox_scores(qa, qb, kj, ca, cb, crow_ref[0, 0:1, pl.ds(off, size)],
                                 crow_ref[0, 1:2, pl.ds(off, size)], mask)

            def online(s, m, l, acc):
                m_new = jnp.maximum(m, jnp.max(s, axis=1, keepdims=True))
                p = jnp.exp(s - m_new)
                alpha = jnp.exp(m - m_new)
                return m_new, alpha * l + jnp.sum(p, axis=1, keepdims=True), alpha * acc + _dot(p, vj, "nn")

            return online(sa, ma, la, acca) + online(sb, mb, lb, accb)

        m0 = jnp.full((ATT_TILE, 1), NEG, F32)
        l0 = jnp.zeros((ATT_TILE, 1), F32)
        a0 = jnp.zeros((ATT_TILE, LANE), F32)
        carry = lax.fori_loop(0, i, lambda j, cr: block(_att_off(j), ATT_TILE, None, cr), (m0, l0, a0, m0, l0, a0))
        ma, la, acca, mb, lb, accb = block(_att_off(i), ATT_TILE, _fox_diag_mask(), carry)
        o_ref[...] = jnp.where(is_a, acca / la, accb / lb)
        lse_ref[0] = jnp.where(is_a, ma + jnp.log(la), mb + jnp.log(lb))

    return pl.pallas_call(
        body, grid=(FOX_PAIRS, N_ATT),
        in_specs=[
            pl.BlockSpec((ATT_TILE, LANE), lambda g, i: (i, g)),
            pl.BlockSpec((ROWS, LANE), lambda g, i: (0, g)),
            pl.BlockSpec((ROWS, LANE), lambda g, i: (0, FOX_V_BLOCK + g)),
            pl.BlockSpec((1, ATT_TILE, LANE), lambda g, i: (g, i, 0)),
            pl.BlockSpec((1, 2, ROWS), lambda g, i: (g, 0, 0)),
        ],
        out_specs=[
            pl.BlockSpec((ATT_TILE, LANE), lambda g, i: (i, g)),
            pl.BlockSpec((1, ATT_TILE, LANE), lambda g, i: (g, i, 0)),
        ],
        out_shape=[jax.ShapeDtypeStruct((ROWS, D_MODEL), F32), jax.ShapeDtypeStruct((FOX_PAIRS, ROWS, LANE), F32)],
        name="fox_attn_fwd", compiler_params=_params(("parallel", "arbitrary")),
    )(qn, kn, proj, c2, crow)


def _fox_attn_dq(qn, kn, proj, c2, crow, lse2, o, do, blockdiag):
    nt = ROWS // ROW_TILE

    def body(q_ref, k_ref, v_ref, c2_ref, crow_ref, lse_ref, o_ref, do_ref, bd_ref, dq_ref, delta_ref, dcq_ref):
        i = pl.program_id(1)
        q = q_ref[...]
        is_a = _iota((ATT_TILE, LANE), 1) < FOX_DH
        qa, qb = jnp.where(is_a, q, 0), jnp.where(is_a, 0, q)
        ca, cb = _pair_cols(c2_ref[0])
        lsa, lsb = _pair_cols(lse_ref[0])
        do_t = do_ref[...]
        delta2 = _dot_hi(do_t * o_ref[...], bd_ref[...], "nn")
        delta_ref[0] = delta2
        dla, dlb = _pair_cols(delta2)
        doa, dob = jnp.where(is_a, do_t, 0.0).astype(BF16), jnp.where(is_a, 0.0, do_t).astype(BF16)

        def block(off, size, mask, carry):
            dqa, dqb, rsa, rsb = carry
            kj = k_ref[pl.ds(off, size), :]
            vj = v_ref[pl.ds(off, size), :].astype(BF16)
            sa, sb = _fox_scores(qa, qb, kj, ca, cb, crow_ref[0, 0:1, pl.ds(off, size)],
                                 crow_ref[0, 1:2, pl.ds(off, size)], mask)
            dsa = jnp.exp(sa - lsa) * (_dot(doa, vj, "nt") - dla)
            dsb = jnp.exp(sb - lsb) * (_dot(dob, vj, "nt") - dlb)
            return (dqa + _dot(dsa, kj, "nn"), dqb + _dot(dsb, kj, "nn"),
                    rsa + jnp.sum(dsa, axis=1, keepdims=True), rsb + jnp.sum(dsb, axis=1, keepdims=True))

        z = jnp.zeros((ATT_TILE, LANE), F32)
        zc = jnp.zeros((ATT_TILE, 1), F32)
        carry = lax.fori_loop(0, i, lambda j, cr: block(_att_off(j), ATT_TILE, None, cr), (z, z, zc, zc))
        dqa, dqb, rsa, rsb = block(_att_off(i), ATT_TILE, _fox_diag_mask(), carry)
        dq_ref[...] = jnp.where(is_a, dqa, dqb)
        dcq_ref[0] = jnp.where(is_a, rsa, rsb)

    tile = pl.BlockSpec((ATT_TILE, LANE), lambda g, i: (i, g))
    pair_tile = pl.BlockSpec((1, ATT_TILE, LANE), lambda g, i: (g, i, 0))
    return pl.pallas_call(
        body, grid=(FOX_PAIRS, N_ATT),
        in_specs=[
            tile,
            pl.BlockSpec((ROWS, LANE), lambda g, i: (0, g)),
            pl.BlockSpec((ROWS, LANE), lambda g, i: (0, FOX_V_BLOCK + g)),
            pair_tile,
            pl.BlockSpec((1, 2, ROWS), lambda g, i: (g, 0, 0)),
            pair_tile, tile, tile,
            pl.BlockSpec((LANE, LANE), lambda g, i: (0, 0)),
        ],
        out_specs=[tile, pair_tile, pair_tile],
        out_shape=[jax.ShapeDtypeStruct((ROWS, D_MODEL), F32), jax.ShapeDtypeStruct((FOX_PAIRS, ROWS, LANE), F32),
                   jax.ShapeDtypeStruct((FOX_PAIRS, ROWS, LANE), F32)],
        name="fox_attn_dq", compiler_params=_params(("parallel", "arbitrary")),
    )(qn, kn, proj, c2, crow, lse2, o, do, blockdiag)


def _fox_attn_dkv(qn, kn, proj, c2, crow, lse2, delta2, do):
    nt = ROWS // ROW_TILE

    def body(q_ref, k_ref, v_ref, c2_ref, crow_ref, lse_ref, delta_ref, do_ref, dk_ref, dv_ref, dc_ref):
        j = pl.program_id(1)
        kj = k_ref[...]
        vj = v_ref[...].astype(BF16)
        cra, crb = crow_ref[0, 0:1, :], crow_ref[0, 1:2, :]

        def block(off, size, mask, carry):
            dk, dv, dca, dcb = carry
            is_a = _iota((size, LANE), 1) < FOX_DH
            q = q_ref[pl.ds(off, size), :]
            qa, qb = jnp.where(is_a, q, 0), jnp.where(is_a, 0, q)
            do_t = do_ref[pl.ds(off, size), :]
            doa, dob = jnp.where(is_a, do_t, 0.0).astype(BF16), jnp.where(is_a, 0.0, do_t).astype(BF16)
            ca, cb = _pair_cols(c2_ref[0, pl.ds(off, size), :])
            lsa, lsb = _pair_cols(lse_ref[0, pl.ds(off, size), :])
            dla, dlb = _pair_cols(delta_ref[0, pl.ds(off, size), :])
            sa, sb = _fox_scores(qa, qb, kj, ca, cb, cra, crb, mask)
            pa, pb = jnp.exp(sa - lsa), jnp.exp(sb - lsb)
            dsa = pa * (_dot(doa, vj, "nt") - dla)
            dsb = pb * (_dot(dob, vj, "nt") - dlb)
            dv = dv + _dot(pa, doa, "tn") + _dot(pb, dob, "tn")
            dk = dk + _dot(dsa, qa, "tn") + _dot(dsb, qb, "tn")
            dca = dca - jnp.sum(dsa, axis=0, keepdims=True)
            dcb = dcb - jnp.sum(dsb, axis=0, keepdims=True)
            return dk, dv, dca, dcb

        z = jnp.zeros((ATT_TILE, LANE), F32)
        zr = jnp.zeros((1, ATT_TILE), F32)
        carry = block(_att_off(j), ATT_TILE, _fox_diag_mask(), (z, z, zr, zr))
        dk, dv, dca, dcb = lax.fori_loop(j + 1, N_ATT, lambda i, cr: block(_att_off(i), ATT_TILE, None, cr), carry)
        dk_ref[...] = dk
        dv_ref[...] = dv.astype(dv_ref.dtype)
        dc_ref[0, 0:1, :] = dca
        dc_ref[0, 1:2, :] = dcb

    whole = pl.BlockSpec((ROWS, LANE), lambda g, j: (0, g))
    pair_whole = pl.BlockSpec((1, ROWS, LANE), lambda g, j: (g, 0, 0))
    tile = pl.BlockSpec((ATT_TILE, LANE), lambda g, j: (j, g))
    return pl.pallas_call(
        body, grid=(FOX_PAIRS, N_ATT),
        in_specs=[
            whole, tile,
            pl.BlockSpec((ATT_TILE, LANE), lambda g, j: (j, FOX_V_BLOCK + g)),
            pair_whole,
            pl.BlockSpec((1, 2, ATT_TILE), lambda g, j: (g, 0, j)),
            pair_whole, pair_whole, whole,
        ],
        out_specs=[tile, tile, pl.BlockSpec((1, 2, ATT_TILE), lambda g, j: (g, 0, j))],
        out_shape=[jax.ShapeDtypeStruct((ROWS, D_MODEL), F32), jax.ShapeDtypeStruct((ROWS, D_MODEL), BF16),
                   jax.ShapeDtypeStruct((FOX_PAIRS, 2, ROWS), F32)],
        name="fox_attn_dkv", compiler_params=_params(("parallel", "arbitrary")),
    )(qn, kn, proj, c2, crow, lse2, delta2, do)


def _fox_attn_bwd(qn, kn, proj, c2, crow, lse2, o, do, blockdiag):
    def body(q_ref, k_ref, v_ref, c2_ref, crow_ref, lse_ref, o_ref, do_ref, bd_ref,
             dq_ref, dk_ref, dv_ref, dc_ref, dcq_ref, delta_ref):
        j = pl.program_id(1)

        @pl.when(j == 0)
        def _():
            for t in range(N_ATT):
                rows = pl.ds(t * ATT_TILE, ATT_TILE)
                delta_ref[rows, :] = _dot_hi(do_ref[rows, :] * o_ref[rows, :], bd_ref[...], "nn")
            dq_ref[...] = jnp.zeros(dq_ref.shape, F32)
            dcq_ref[...] = jnp.zeros(dcq_ref.shape, F32)

        kj = k_ref[...]
        vj = v_ref[...].astype(BF16)
        cra, crb = crow_ref[0, 0:1, :], crow_ref[0, 1:2, :]
        is_a = _iota((ATT_TILE, LANE), 1) < FOX_DH

        def block(off, mask, carry):
            dk, dv, dca, dcb = carry
            rows = pl.ds(off, ATT_TILE)
            q = q_ref[rows, :]
            qa, qb = jnp.where(is_a, q, 0), jnp.where(is_a, 0, q)
            do_t = do_ref[rows, :]
            doa, dob = jnp.where(is_a, do_t, 0.0).astype(BF16), jnp.where(is_a, 0.0, do_t).astype(BF16)
            ca, cb = _pair_cols(c2_ref[0, rows, :])
            lsa, lsb = _pair_cols(lse_ref[0, rows, :])
            dla, dlb = _pair_cols(delta_ref[rows, :])
            sa, sb = _fox_scores(qa, qb, kj, ca, cb, cra, crb, mask)
            pa, pb = jnp.exp(sa - lsa), jnp.exp(sb - lsb)
            dsa = pa * (_dot(doa, vj, "nt") - dla)
            dsb = pb * (_dot(dob, vj, "nt") - dlb)
            dq_ref[rows, :] += jnp.where(is_a, _dot(dsa, kj, "nn"), _dot(dsb, kj, "nn"))
            dcq_ref[0, rows, :] += jnp.where(is_a, jnp.sum(dsa, axis=1, keepdims=True),
                                             jnp.sum(dsb, axis=1, keepdims=True))
            dv = dv + _dot(pa, doa, "tn") + _dot(pb, dob, "tn")
            dk = dk + _dot(dsa, qa, "tn") + _dot(dsb, qb, "tn")
            dca = dca - jnp.sum(dsa, axis=0, keepdims=True)
            dcb = dcb - jnp.sum(dsb, axis=0, keepdims=True)
            return dk, dv, dca, dcb

        z = jnp.zeros((ATT_TILE, LANE), F32)
        zr = jnp.zeros((1, ATT_TILE), F32)
        carry = block(_att_off(j), _fox_diag_mask(), (z, z, zr, zr))
        dk, dv, dca, dcb = lax.fori_loop(j + 1, N_ATT, lambda i, cr: block(_att_off(i), None, cr), carry)
        dk_ref[...] = dk
        dv_ref[...] = dv.astype(dv_ref.dtype)
        dc_ref[0, 0:1, :] = dca
        dc_ref[0, 1:2, :] = dcb

    whole = pl.BlockSpec((ROWS, LANE), lambda g, j: (0, g))
    pair_whole = pl.BlockSpec((1, ROWS, LANE), lambda g, j: (g, 0, 0))
    tile = pl.BlockSpec((ATT_TILE, LANE), lambda g, j: (j, g))
    key_rows = pl.BlockSpec((1, 2, ATT_TILE), lambda g, j: (g, 0, j))
    return pl.pallas_call(
        body, grid=(FOX_PAIRS, N_ATT),
        in_specs=[
            whole, tile,
            pl.BlockSpec((ATT_TILE, LANE), lambda g, j: (j, FOX_V_BLOCK + g)),
            pair_whole, key_rows, pair_whole, whole, whole,
            pl.BlockSpec((LANE, LANE), lambda g, j: (0, 0)),
        ],
        out_specs=[whole, tile, tile, key_rows, pair_whole],
        out_shape=[jax.ShapeDtypeStruct((ROWS, D_MODEL), F32), jax.ShapeDtypeStruct((ROWS, D_MODEL), F32),
                   jax.ShapeDtypeStruct((ROWS, D_MODEL), BF16), jax.ShapeDtypeStruct((FOX_PAIRS, 2, ROWS), F32),
                   jax.ShapeDtypeStruct((FOX_PAIRS, ROWS, LANE), F32)],
        scratch_shapes=[pltpu.VMEM((ROWS, LANE), F32)],
        name="fox_attn_bwd", compiler_params=_params(("parallel", "arbitrary")),
    )(qn, kn, proj, c2, crow, lse2, o, do, blockdiag)


def _gla_chunk(q, k, v, g, st, dot):
    n = len(q)
    c = q[0].shape[0]
    incl = _iota((c, c), 1) <= _iota((c, c), 0)
    incl_f = incl.astype(F32)
    b = [_dot_hi(incl_f, g[i], "nn") for i in range(n)]
    qe = [q[i] * (GLA_DK ** -0.5) * jnp.exp(b[i]) for i in range(n)]
    ke = [k[i] * jnp.exp(-b[i]) for i in range(n)]
    att = [jnp.where(incl, dot(qe[i], ke[i], "nt"), 0.0) for i in range(n)]
    o_intra = [dot(att[i], v[i], "nn") for i in range(n)]
    o_inter = [dot(qe[i], st[i], "nt") for i in range(n)]
    b_last = [_last_row(b[i]) for i in range(n)]
    kd = [k[i] * jnp.exp(b_last[i] - b[i]) for i in range(n)]
    upd = [dot(v[i], kd[i], "tn") for i in range(n)]
    return ([o_intra[i] + o_inter[i] for i in range(n)],
            [st[i] * jnp.exp(b_last[i]) + upd[i] for i in range(n)])


def _tri_inverse(low):
    c = low[0].shape[0]
    eye = (_iota((c, c), 0) == _iota((c, c), 1)).astype(F32)
    x, p = [eye - l for l in low], list(low)
    span = 2
    while span < c:
        p = [_dot_mid(pi, pi, "nn") for pi in p]
        x = [_dot_mid(xi, eye + pi, "nn") for xi, pi in zip(x, p)]
        span *= 2
    return x


def _solve(low, rhs):
    return [_dot_mid(inv, r, "nn") for inv, r in zip(_tri_inverse(low), rhs)]


@jax.custom_vjp
def _solve_ad(low, rhs):
    return _solve(low, rhs)


def _solve_ad_fwd(low, rhs):
    inv = _tri_inverse(low)
    sol = [_dot_mid(i, r, "nn") for i, r in zip(inv, rhs)]
    return sol, (inv, sol)


def _solve_ad_bwd(res, ct):
    inv, sol = res
    drhs = [_dot_mid(i, c, "tn") for i, c in zip(inv, ct)]
    return [-_dot_mid(d, s, "nt") for d, s in zip(drhs, sol)], drhs


_solve_ad.defvjp(_solve_ad_fwd, _solve_ad_bwd)


def _gdn_chunk(q, k, v, gates, st, heads, dot, solve):
    n = len(q)
    c = q[0].shape[0]
    incl = _iota((c, c), 1) <= _iota((c, c), 0)
    strict = _iota((c, c), 1) < _iota((c, c), 0)
    incl_f = incl.astype(F32)
    upper_f = (_iota((c, c), 0) <= _iota((c, c), 1)).astype(F32)
    gcol = [_lane_col(gates, h) for h in heads]
    beta = [_lane_col(gates, h + GDN_HEADS) for h in heads]
    b = [_dot_hi(incl_f, jnp.broadcast_to(gc, (c, LANE)), "nn") for gc in gcol]
    g_sq = [jnp.broadcast_to(gc, (c, c)) for gc in gcol]
    b_col = [_dot_hi(incl_f, gs, "nn") for gs in g_sq]
    b_row = [_dot_hi(gs, upper_f, "tn") for gs in g_sq]
    diff = [b_col[i] - b_row[i] for i in range(n)]
    kb = [k[i] * beta[i] for i in range(n)]
    vb = [v[i] * beta[i] for i in range(n)]
    low = [dot(kb[i], k[i], "nt") * jnp.exp(jnp.where(strict, diff[i], NEG)) for i in range(n)]
    eb = [jnp.exp(bi) for bi in b]
    sol = solve(low, [jnp.concatenate([vb[i], kb[i] * eb[i]], axis=1) for i in range(n)])
    w_st = [dot(sol[i][:, GDN_DV:], st[i], "nn") for i in range(n)]
    v_new = [sol[i][:, :GDN_DV] - w_st[i] for i in range(n)]
    att = [dot(q[i], k[i], "nt") * jnp.exp(jnp.where(incl, diff[i], NEG)) for i in range(n)]
    o_intra = [dot(att[i], v_new[i], "nn") for i in range(n)]
    o_inter = [dot(q[i] * eb[i], st[i], "nn") for i in range(n)]
    b_last = [_last_row(bi) for bi in b]
    kd = [k[i] * jnp.exp(b_last[i] - b[i]) for i in range(n)]
    upd = [dot(kd[i], v_new[i], "tn") for i in range(n)]
    return ([o_intra[i] + o_inter[i] for i in range(n)],
            [st[i] * jnp.exp(b_last[i]) + upd[i] for i in range(n)])


N_CHUNKS = ROWS // CHUNK
GDN_GROUP = 8


def _head_lanes(width, n):
    return [slice(i * width, (i + 1) * width) for i in range(n)]


def _gla_scan_fwd(proj, garr):
    kl, vl = _head_lanes(GLA_DK, GLA_HEADS), _head_lanes(GLA_DV, GLA_HEADS)

    def body(q_ref, k_ref, v_ref, g_ref, o_ref, states_ref, st):
        @pl.when(pl.program_id(0) == 0)
        def _():
            st[...] = jnp.zeros(st.shape, F32)

        s_in = [st[i] for i in range(GLA_HEADS)]
        o, s_out = _gla_chunk([q_ref[:, ln] for ln in kl], [k_ref[:, ln] for ln in kl], [v_ref[:, ln] for ln in vl],
                              [g_ref[:, ln] for ln in kl], s_in, _dot)
        for i in range(GLA_HEADS):
            states_ref[i, 0] = s_in[i]
            o_ref[:, vl[i]] = o[i]
            st[i] = s_out[i]

    return pl.pallas_call(
        body, grid=(N_CHUNKS,),
        in_specs=[
            pl.BlockSpec((CHUNK, GLA_QK), lambda c: (c, 0)),
            pl.BlockSpec((CHUNK, GLA_QK), lambda c: (c, 1)),
            pl.BlockSpec((CHUNK, GLA_V), lambda c: (c, 1)),
            pl.BlockSpec((CHUNK, GLA_QK), lambda c: (c, 0)),
        ],
        out_specs=[
            pl.BlockSpec((CHUNK, GLA_V), lambda c: (c, 0)),
            pl.BlockSpec((GLA_HEADS, 1, GLA_DV, GLA_DK), lambda c: (0, c, 0, 0)),
        ],
        out_shape=[jax.ShapeDtypeStruct((ROWS, GLA_V), F32),
                   jax.ShapeDtypeStruct((GLA_HEADS, N_CHUNKS, GLA_DV, GLA_DK), F32)],
        scratch_shapes=[pltpu.VMEM((GLA_HEADS, GLA_DV, GLA_DK), F32)],
        name="gla_scan_fwd", compiler_params=_params(("arbitrary",)),
    )(proj, proj, proj, garr)


def _gla_scan_bwd(proj, garr, states, do):
    last = N_CHUNKS - 1
    kl, vl = _head_lanes(GLA_DK, GLA_HEADS), _head_lanes(GLA_DV, GLA_HEADS)

    def body(q_ref, k_ref, v_ref, g_ref, s_ref, do_ref, dq_ref, dk_ref, dv_ref, dg_ref, dst):
        @pl.when(pl.program_id(0) == 0)
        def _():
            dst[...] = jnp.zeros(dst.shape, F32)

        _, vjp = jax.vjp(lambda q, k, v, g, s: _gla_chunk(q, k, v, g, s, _dot_ad),
                         [q_ref[:, ln] for ln in kl], [k_ref[:, ln] for ln in kl], [v_ref[:, ln] for ln in vl],
                         [g_ref[:, ln] for ln in kl], [s_ref[i, 0] for i in range(GLA_HEADS)])
        dq, dk, dv, dg, ds = vjp(([do_ref[:, ln] for ln in vl], [dst[i] for i in range(GLA_HEADS)]))
        for i in range(GLA_HEADS):
            dq_ref[:, kl[i]] = dq[i].astype(dq_ref.dtype)
            dk_ref[:, kl[i]] = dk[i].astype(dk_ref.dtype)
            dv_ref[:, vl[i]] = dv[i].astype(dv_ref.dtype)
            dg_ref[:, kl[i]] = dg[i]
            dst[i] = ds[i]

    kspec = pl.BlockSpec((CHUNK, GLA_QK), lambda c: (last - c, 0))
    vspec = pl.BlockSpec((CHUNK, GLA_V), lambda c: (last - c, 0))
    return pl.pallas_call(
        body, grid=(N_CHUNKS,),
        in_specs=[
            kspec,
            pl.BlockSpec((CHUNK, GLA_QK), lambda c: (last - c, 1)),
            pl.BlockSpec((CHUNK, GLA_V), lambda c: (last - c, 1)),
            kspec,
            pl.BlockSpec((GLA_HEADS, 1, GLA_DV, GLA_DK), lambda c: (0, last - c, 0, 0)),
            vspec,
        ],
        out_specs=[kspec, kspec, vspec, kspec],
        out_shape=[jax.ShapeDtypeStruct((ROWS, GLA_QK), BF16), jax.ShapeDtypeStruct((ROWS, GLA_QK), BF16),
                   jax.ShapeDtypeStruct((ROWS, GLA_V), BF16), jax.ShapeDtypeStruct((ROWS, GLA_QK), F32)],
        scratch_shapes=[pltpu.VMEM((GLA_HEADS, GLA_DV, GLA_DK), F32)],
        name="gla_scan_bwd", compiler_params=_params(("arbitrary",)),
    )(proj, proj, proj, garr, states, do)


def _gdn_scan_fwd(qn, kn, vn, gates):
    def body(q_ref, k_ref, v_ref, g_ref, o_ref, states_ref, st):
        @pl.when((pl.program_id(0) == 0) & (pl.program_id(1) == 0))
        def _():
            st[...] = jnp.zeros(st.shape, F32)

        heads = [pl.program_id(1) * GDN_GROUP + hh for hh in range(GDN_GROUP)]
        lanes = _head_lanes(GDN_DK, GDN_GROUP)
        s_in = [st[h] for h in heads]
        o, s_out = _gdn_chunk([q_ref[:, ln] for ln in lanes], [k_ref[:, ln] for ln in lanes],
                              [v_ref[:, ln] for ln in lanes], g_ref[...], s_in, heads, _dot, _solve)
        for hh in range(GDN_GROUP):
            states_ref[hh, 0] = s_in[hh]
            o_ref[:, lanes[hh]] = o[hh]
            st[heads[hh]] = s_out[hh]

    hspec = pl.BlockSpec((CHUNK, GDN_GROUP * GDN_DK), lambda c, h: (c, h))
    return pl.pallas_call(
        body, grid=(N_CHUNKS, GDN_HEADS // GDN_GROUP),
        in_specs=[hspec, hspec, hspec, pl.BlockSpec((CHUNK, LANE), lambda c, h: (c, 0))],
        out_specs=[hspec, pl.BlockSpec((GDN_GROUP, 1, GDN_DK, GDN_DV), lambda c, h: (h, c, 0, 0))],
        out_shape=[jax.ShapeDtypeStruct((ROWS, D_MODEL), F32),
                   jax.ShapeDtypeStruct((GDN_HEADS, N_CHUNKS, GDN_DK, GDN_DV), F32)],
        scratch_shapes=[pltpu.VMEM((GDN_HEADS, GDN_DK, GDN_DV), F32)],
        name="gdn_scan_fwd", compiler_params=_params(("arbitrary", "arbitrary")),
    )(qn, kn, vn, gates)


def _gdn_scan_bwd(qn, kn, vn, gates, states, do):
    last = N_CHUNKS - 1

    def body(q_ref, k_ref, v_ref, g_ref, s_ref, do_ref, dq_ref, dk_ref, dv_ref, dg_ref, dst):
        @pl.when((pl.program_id(0) == 0) & (pl.program_id(1) == 0))
        def _():
            dst[...] = jnp.zeros(dst.shape, F32)

        @pl.when(pl.program_id(1) == 0)
        def _():
            dg_ref[...] = jnp.zeros(dg_ref.shape, F32)

        heads = [pl.program_id(1) * GDN_GROUP + hh for hh in range(GDN_GROUP)]
        lanes = _head_lanes(GDN_DK, GDN_GROUP)
        _, vjp = jax.vjp(lambda q, k, v, g, s: _gdn_chunk(q, k, v, g, s, heads, _dot_ad, _solve_ad),
                         [q_ref[:, ln] for ln in lanes], [k_ref[:, ln] for ln in lanes],
                         [v_ref[:, ln] for ln in lanes], g_ref[...], [s_ref[hh, 0] for hh in range(GDN_GROUP)])
        dq, dk, dv, dg, ds = vjp(([do_ref[:, ln] for ln in lanes], [dst[h] for h in heads]))
        for hh in range(GDN_GROUP):
            dq_ref[:, lanes[hh]] = dq[hh]
            dk_ref[:, lanes[hh]] = dk[hh]
            dv_ref[:, lanes[hh]] = dv[hh]
            dst[heads[hh]] = ds[hh]
        dg_ref[...] += dg

    hspec = pl.BlockSpec((CHUNK, GDN_GROUP * GDN_DK), lambda c, h: (last - c, h))
    gspec = pl.BlockSpec((CHUNK, LANE), lambda c, h: (last - c, 0))
    return pl.pallas_call(
        body, grid=(N_CHUNKS, GDN_HEADS // GDN_GROUP),
        in_specs=[hspec, hspec, hspec, gspec,
                  pl.BlockSpec((GDN_GROUP, 1, GDN_DK, GDN_DV), lambda c, h: (h, last - c, 0, 0)), hspec],
        out_specs=[hspec, hspec, hspec, gspec],
        out_shape=[jax.ShapeDtypeStruct((ROWS, D_MODEL), F32)] * 3 + [jax.ShapeDtypeStruct((ROWS, LANE), F32)],
        scratch_shapes=[pltpu.VMEM((GDN_HEADS, GDN_DK, GDN_DV), F32)],
        name="gdn_scan_bwd", compiler_params=_params(("arbitrary", "arbitrary")),
    )(qn, kn, vn, gates, states, do)


def _gdn_conv_fwd(proj, w, kind, cb):
    fn = _gdn_conv_fn(kind)
    nt = ROWS // ROW_TILE

    def body(prev_ref, cur_ref, w_ref, o_ref):
        (out,) = fn(pl.program_id(0), prev_ref[...], cur_ref[...], w_ref[...])
        o_ref[...] = out

    return pl.pallas_call(
        body, grid=(nt,),
        in_specs=[pl.BlockSpec((ROW_TILE, D_MODEL), lambda i: (jnp.maximum(i - 1, 0), cb)),
                  pl.BlockSpec((ROW_TILE, D_MODEL), lambda i: (i, cb)),
                  pl.BlockSpec(w.shape, lambda i: (0, 0))],
        out_specs=pl.BlockSpec((ROW_TILE, D_MODEL), lambda i: (i, 0)),
        out_shape=jax.ShapeDtypeStruct((ROWS, D_MODEL), F32),
        name="gdn_conv_fwd_" + kind, compiler_params=_params(("parallel",)),
    )(proj, proj, w)


def _gdn_conv_bwd(proj, w, kind, cb, ct):
    fn = _gdn_conv_fn(kind)
    nt = ROWS // ROW_TILE

    def body(prev_ref, cur_ref, w_ref, ct_ref, dx_ref, dw_ref, carry):
        step = pl.program_id(0)
        i = nt - 1 - step

        @pl.when(step == 0)
        def _():
            carry[...] = jnp.zeros(carry.shape, F32)
            dw_ref[...] = jnp.zeros(dw_ref.shape, F32)

        _, vjp = jax.vjp(lambda p, c, ww: fn(i, p, c, ww), prev_ref[...], cur_ref[...], w_ref[...])
        dprev, dcur, dw = vjp((ct_ref[...],))
        dx_ref[...] = (dcur + carry[...]).astype(dx_ref.dtype)
        carry[...] = dprev
        dw_ref[...] += dw

    return pl.pallas_call(
        body, grid=(nt,),
        in_specs=[pl.BlockSpec((ROW_TILE, D_MODEL), lambda s: (jnp.maximum(nt - 2 - s, 0), cb)),
                  pl.BlockSpec((ROW_TILE, D_MODEL), lambda s: (nt - 1 - s, cb)),
                  pl.BlockSpec(w.shape, lambda s: (0, 0)),
                  pl.BlockSpec((ROW_TILE, D_MODEL), lambda s: (nt - 1 - s, 0))],
        out_specs=[pl.BlockSpec((ROW_TILE, D_MODEL), lambda s: (nt - 1 - s, 0)),
                   pl.BlockSpec(w.shape, lambda s: (0, 0))],
        out_shape=[jax.ShapeDtypeStruct((ROWS, D_MODEL), BF16), jax.ShapeDtypeStruct(w.shape, F32)],
        scratch_shapes=[pltpu.VMEM((ROW_TILE, D_MODEL), F32)],
        name="gdn_conv_bwd_" + kind, compiler_params=_params(("arbitrary",)),
    )(proj, proj, w, ct)


def _loss_head(h, target):
    nt = ROWS // ROW_TILE
    lead = PAD_ROWS // ROW_TILE

    def body(h_ref, t_ref, dh_ref, acc_ref):
        i = pl.program_id(0)

        @pl.when(i == 0)
        def _():
            acc_ref[...] = jnp.zeros(acc_ref.shape, F32)

        err = (h_ref[...] - t_ref[...]) * jnp.where(i >= lead, 1.0, 0.0)
        dh_ref[...] = err * (1.0 / D_MODEL)
        acc_ref[...] += jnp.sum(err * err, axis=0, keepdims=True)

    return pl.pallas_call(
        body, grid=(nt,),
        in_specs=[pl.BlockSpec((ROW_TILE, D_MODEL), lambda i: (i, 0)),
                  pl.BlockSpec((ROW_TILE, D_MODEL), lambda i: (jnp.maximum(i - lead, 0), 0))],
        out_specs=[pl.BlockSpec((ROW_TILE, D_MODEL), lambda i: (i, 0)),
                   pl.BlockSpec((1, D_MODEL), lambda i: (0, 0))],
        out_shape=[jax.ShapeDtypeStruct((ROWS, D_MODEL), F32), jax.ShapeDtypeStruct((1, D_MODEL), F32)],
        name="loss_head", compiler_params=_params(("arbitrary",)),
    )(h, target)


MESH = pl.DeviceIdType.MESH
ANY = pl.BlockSpec(memory_space=pl.ANY)


N_PEERS = N_DEV - 1


HBM_SPEC = pl.BlockSpec(memory_space=pltpu.HBM)
SEM_SPEC = pl.BlockSpec(memory_space=pltpu.SEMAPHORE)
SIDE_EFFECT = pltpu.SideEffectType.DATAFLOW_SIDE_EFFECTING


def _me_and_peers():
    x, y, c = lax.axis_index("x"), lax.axis_index("y"), lax.axis_index("c")
    peers = []
    for k in range(1, N_DEV):
        pid = (1 - x if k & 4 else x, 1 - y if k & 2 else y, 1 - c if k & 1 else c)
        peers.append((4 * pid[0] + 2 * pid[1] + pid[2], pid))
    return 4 * x + 2 * y + c, peers


def _split_copies(src_refs, land_refs, send_sems, recv_sems, spread):
    me, peers = _me_and_peers()
    out = []
    for t, (src, land) in enumerate(zip(src_refs, land_refs)):
        for k, (peer, pid) in enumerate(peers):
            sems = dict(send_sem=send_sems.at[N_PEERS * t + k], recv_sem=recv_sems.at[N_PEERS * t + k],
                        device_id=pid, device_id_type=MESH)
            piece = src if spread else src.at[peer]
            out.append((pltpu.make_async_remote_copy(src_ref=piece, dst_ref=land.at[me], **sems),
                        pltpu.make_async_remote_copy(src_ref=piece, dst_ref=land.at[peer], **sems)))
    return out


def _send_start(srcs, spread, name):
    n = len(srcs)
    lands = [lax.empty((N_DEV,) + (s.shape if spread else s.shape[1:]), s.dtype) for s in srcs]

    def body(*refs):
        for out, _ in _split_copies(refs[:n], refs[n:2 * n], refs[2 * n], refs[2 * n + 1], spread):
            out.start()
        refs[-1][...] = jnp.zeros(refs[-1].shape, F32)

    thru = [pltpu.HBM(a.shape, a.dtype) for a in list(srcs) + lands]
    res = pl.pallas_call(
        body, name=name,
        out_shape=(pltpu.SemaphoreType.DMA((N_PEERS * n,)), pltpu.SemaphoreType.DMA((N_PEERS * n,)), *thru,
                   jax.ShapeDtypeStruct((8, LANE), F32)),
        in_specs=[HBM_SPEC] * (2 * n),
        out_specs=(SEM_SPEC, SEM_SPEC, *[HBM_SPEC] * (2 * n), pl.BlockSpec(memory_space=pltpu.VMEM)),
        input_output_aliases={t: 2 + t for t in range(2 * n)},
        compiler_params=pltpu.CompilerParams(has_side_effects=SIDE_EFFECT),
    )(*[pltpu.with_memory_space_constraint(a, pltpu.HBM) for a in list(srcs) + lands])
    return res[0], res[1], list(res[2:2 + n]), list(res[2 + n:2 + 2 * n]), res[-1]


def _send_wait(started, spread, after, name):
    send_sems, recv_sems, srcs, lands, _ = started
    n = len(srcs)

    def body(*refs):
        for out, arrival in _split_copies(refs[:n], refs[n:2 * n], refs[2 * n], refs[2 * n + 1], spread):
            out.wait_send()
            arrival.wait_recv()

    thru = [pltpu.HBM(a.shape, a.dtype) for a in srcs + lands]
    res = pl.pallas_call(
        body, name=name, out_shape=tuple(thru),
        in_specs=[HBM_SPEC] * (2 * n) + [SEM_SPEC, SEM_SPEC, ANY], out_specs=tuple([HBM_SPEC] * (2 * n)),
        input_output_aliases={t: t for t in range(2 * n)},
        compiler_params=pltpu.CompilerParams(has_side_effects=SIDE_EFFECT),
    )(*srcs, *lands, send_sems, recv_sems, after)
    return list(res[n:])


def _adamw(recvs, w, m, v, name):
    n_layers = len(recvs)
    rows, cols = recvs[0].shape[1:]
    tr = _pick(rows, (256, 128, 64, 32, 16))
    nb = rows // tr

    def body(*refs):
        r_refs = refs[:n_layers]
        w_ref, m_ref, v_ref, g_ref, d_ref, m2_ref, v2_ref = refs[n_layers:]

        def update(r_ref):
            g = r_ref[0].astype(F32)
            for k in range(1, N_DEV):
                g = g + r_ref[k].astype(F32)
            m2 = ADAM_B1 * m_ref[...] + (1.0 - ADAM_B1) * g
            v2 = ADAM_B2 * v_ref[...] + (1.0 - ADAM_B2) * (g * g)
            m_hat = m2 / (1.0 - ADAM_B1 ** ADAM_STEP)
            v_hat = v2 / (1.0 - ADAM_B2 ** ADAM_STEP)
            g_ref[...] = g
            d_ref[...] = -ADAM_LR * (m_hat / (jnp.sqrt(v_hat) + ADAM_EPS) + ADAM_WD * w_ref[...])
            m2_ref[...] = m2
            v2_ref[...] = v2

        if n_layers == 1:
            update(r_refs[0])
        else:
            for layer in range(n_layers):
                pl.when(pl.program_id(0) == layer)(functools.partial(update, r_refs[layer]))

    def recv_spec(layer):
        return pl.BlockSpec((N_DEV, tr, cols), lambda a, i: (0, jnp.where(a == layer, i, 0), 0))

    spec = pl.BlockSpec((tr, cols), lambda a, i: (a * nb + i, 0))
    return pl.pallas_call(
        body, grid=(n_layers, nb),
        in_specs=[recv_spec(layer) for layer in range(n_layers)] + [spec, spec, spec],
        out_specs=[spec] * 4, out_shape=[jax.ShapeDtypeStruct(w.shape, F32)] * 4,
        name=name, compiler_params=_params(("arbitrary", "arbitrary")),
    )(*recvs, w, m, v)


def _padded_rows(n, mult):
    rows = -(-n // PACK_COLS)
    return -(-rows // mult) * mult


def _pack(flat_pieces, mult, dtype, lead=()):
    flat = jnp.concatenate([p.astype(dtype) for p in flat_pieces], axis=-1)
    n = flat.shape[-1]
    rows = _padded_rows(n, mult)
    flat = jnp.pad(flat, [(0, 0)] * len(lead) + [(0, rows * PACK_COLS - n)])
    return flat.reshape(lead + (rows, PACK_COLS))


def _unpack(packed, shapes):
    flat = packed.reshape(-1)
    out, off = {}, 0
    for name, shape in shapes:
        size = 1
        for s in shape:
            size *= s
        out[name] = flat[off:off + size].reshape(shape)
        off += size
    return out


def _local_2d(arr):
    return arr.reshape(-1, arr.shape[-1])


SHARD_2D = {n: (s[-2] if len(s) > 2 else s[0], s[-1], a == len(s) - 1) for n, s, a in SHARDED}
SHARD_2D["gdn_conv_w"] = (GDN_CONV, GDN_CONV_DIM // N_DEV, True)


def _wire(name):
    rows, cols, _ = SHARD_2D[name]
    return BF16 if rows * cols >= 1 << 16 else F32


def _layer_tensors(i):
    kind, j = i % 3, i // 3
    pre = ("fox", "gla", "gdn")[kind]
    mix = [(pre + "_w_in", j), (pre + "_w_out", j)]
    if kind == 1:
        mix.append(("gla_w_alpha2", j))
    if kind == 2:
        mix.append(("gdn_conv_w", j))
    if i == 0:
        mix.append(("meta_tokens", None))
    return {"mix": mix, "ffn": [("w_gate_up", i), ("w_down", i)]}


PARTS = ("mix", "ffn")


def _layer_block(arr, idx):
    return arr if idx is None else _local_2d(arr[idx])


def _to_slabs(full2d, name):
    rows, cols, by_cols = SHARD_2D[name]
    if by_cols:
        parts = full2d if isinstance(full2d, tuple) else (full2d,)
        per = parts[0].shape[1] // cols
        return jnp.stack([parts[d // per][:, cols * (d % per):cols * (d % per + 1)] for d in range(N_DEV)])
    return full2d.reshape(N_DEV, rows, cols)


def _from_gathered(g, name):
    rows, cols, by_cols = SHARD_2D[name]
    if by_cols:
        return jnp.concatenate([g[d] for d in range(N_DEV)], axis=1)
    return g.reshape(N_DEV * rows, cols)


def _pad_cols(w, total):
    return jnp.pad(w, [(0, 0)] * (w.ndim - 1) + [(0, total - w.shape[-1])])


def _pad_lanes(v):
    return _pad_cols(v, LANE)


def _x(arr, width, cb0=0, moves=0):
    return (arr, width, cb0, moves)


def _rms_fwd(h, g):
    return _row_fwd(_rms_fn, [_x(h, D_MODEL)], [g], [], [(D_MODEL, D_MODEL, BF16)], "rms_fwd")[0]


FUSED_ROWS = 384


def _matmul_add_norm(a, b, add, g, name):
    m, k = a.shape
    n = b.shape[1]

    def body(a_ref, b_ref, add_ref, g_ref, h_ref, y_ref):
        h2 = _dot(a_ref[...], b_ref[...], "nn") + add_ref[...]
        h_ref[...] = h2
        y_ref[...] = _rms_fn(0, h2, g_ref[...])[0].astype(y_ref.dtype)

    rows = pl.BlockSpec((FUSED_ROWS, n), lambda i: (i, 0))
    return pl.pallas_call(
        body, grid=(m // FUSED_ROWS,),
        in_specs=[pl.BlockSpec((FUSED_ROWS, k), lambda i: (i, 0)), pl.BlockSpec((k, n), lambda i: (0, 0)), rows,
                  pl.BlockSpec((1, n), lambda i: (0, 0))],
        out_specs=[rows, rows],
        out_shape=[jax.ShapeDtypeStruct((m, n), F32), jax.ShapeDtypeStruct((m, n), BF16)],
        name=name, compiler_params=_params(("parallel",)),
    )(a, b, add, g)


def _matmul_norm_bwd(a, b, b_kblock, h, g, dres, name, prev=None):
    m, k = a.shape
    n = b.shape[0]

    def body(*refs):
        a_ref, b_ref, h_ref, g_ref, dres_ref = refs[:5]
        dh_ref, dg_ref = refs[-2:]
        dy = _dot(a_ref[...], b_ref[...], "nt")
        if prev is not None:
            dy = dy + refs[5][...]
        _, vjp = jax.vjp(lambda hh, gg: _rms_fn(0, hh, gg), h_ref[...], g_ref[...])
        dh, dg = vjp((dy,))
        dh_ref[...] = dres_ref[...] + dh

        @pl.when(pl.program_id(0) == 0)
        def _():
            dg_ref[...] = jnp.zeros(dg_ref.shape, F32)

        dg_ref[...] += dg

    rows = pl.BlockSpec((FUSED_ROWS, n), lambda i: (i, 0))
    gain = pl.BlockSpec((1, n), lambda i: (0, 0))
    operands = [a, b, h, g, dres] + ([] if prev is None else [prev])
    in_specs = [pl.BlockSpec((FUSED_ROWS, k), lambda i: (i, 0)), pl.BlockSpec((n, k), lambda i: (0, b_kblock)),
                rows, gain, rows] + ([] if prev is None else [rows])
    return pl.pallas_call(
        body, grid=(m // FUSED_ROWS,), in_specs=in_specs, out_specs=[rows, gain],
        out_shape=[jax.ShapeDtypeStruct((m, n), F32), jax.ShapeDtypeStruct((1, n), F32)],
        name=name, compiler_params=_params(("arbitrary",)),
    )(*operands)


SW_WIDTH = 256
SW_NCOL = D_FF // SW_WIDTH


def _ffn_gate_up(y, wgu):
    m, k = y.shape
    tm = _pick(m, (1408, 384, 128))

    def body(y_ref, wg_ref, wu_ref, a_ref, g_ref, u_ref):
        yv = y_ref[...]
        gt = _dot(yv, wg_ref[...], "nn")
        up = _dot(yv, wu_ref[...], "nn")
        a_ref[...] = (_silu(gt) * up).astype(a_ref.dtype)
        g_ref[...] = gt.astype(g_ref.dtype)
        u_ref[...] = up.astype(u_ref.dtype)

    tile = pl.BlockSpec((tm, SW_WIDTH), lambda i, j: (i, j))
    return pl.pallas_call(
        body, grid=(m // tm, SW_NCOL),
        in_specs=[pl.BlockSpec((tm, k), lambda i, j: (i, 0)),
                  pl.BlockSpec((k, SW_WIDTH), lambda i, j: (0, j)),
                  pl.BlockSpec((k, SW_WIDTH), lambda i, j: (0, SW_NCOL + j))],
        out_specs=[tile] * 3, out_shape=[jax.ShapeDtypeStruct((m, D_FF), BF16)] * 3,
        name="ffn_gate_up", compiler_params=_params(("parallel", "parallel")),
    )(y, wgu, wgu)


def _ffn_down_dx(dh2, wd, gt, up):
    m, k = dh2.shape
    tm = _pick(m, (1408, 384, 128))

    def body(d_ref, w_ref, g_ref, u_ref, dg_ref, du_ref):
        da = _dot(d_ref[...], w_ref[...], "nt")
        gv, uv = g_ref[...].astype(F32), u_ref[...].astype(F32)
        s = jax.nn.sigmoid(gv)
        dg_ref[...] = (da * uv * (s + gv * s * (1.0 - s))).astype(dg_ref.dtype)
        du_ref[...] = (da * gv * s).astype(du_ref.dtype)

    tile = pl.BlockSpec((tm, SW_WIDTH), lambda i, j: (i, j))
    return pl.pallas_call(
        body, grid=(m // tm, SW_NCOL),
        in_specs=[pl.BlockSpec((tm, k), lambda i, j: (i, 0)),
                  pl.BlockSpec((SW_WIDTH, k), lambda i, j: (j, 0)), tile, tile],
        out_specs=[tile] * 2, out_shape=[jax.ShapeDtypeStruct((m, D_FF), BF16)] * 2,
        name="ffn_down_dx", compiler_params=_params(("parallel", "parallel")),
    )(dh2, wd, gt, up)


def _ffn_fwd(h, y, wgu, wd, g_next):
    a, gt, up = _ffn_gate_up(y, wgu)
    if g_next is None:
        h2, y_next = _matmul(a, wd, "nn", F32, "ffn_down", add=h), None
    else:
        h2, y_next = _matmul_add_norm(a, wd, h, g_next, "ffn_down_norm")
    return h2, y_next, (h, y, gt, up, a)


def _ffn_bwd(dh2, saved, g, wgu, wd):
    h, y, gt, up, a = saved
    dgt, dup = _ffn_down_dx(dh2, wd, gt, up)
    dwd = _matmul(a, dh2, "tn", F32, "ffn_down_dw")
    dwg = _matmul(y, dgt, "tn", F32, "ffn_gate_dw")
    dwu = _matmul(y, dup, "tn", F32, "ffn_up_dw")
    dy = _matmul(dgt, wgu, "nt", F32, "ffn_gate_dx", b_kblock=0)
    dh, dg = _matmul_norm_bwd(dup, wgu, 1, h, g, dh2, "ffn_up_dx_norm", prev=dy)
    return dh, dg, (dwg, dwu), dwd


def _fox_consts():
    lane = jnp.arange(D_MODEL) // FOX_DH
    e_expand = (jnp.arange(LANE)[:, None] == lane[None, :]).astype(F32)
    e_mean = e_expand.T * (1.0 / FOX_DH)
    half = jnp.arange(LANE) // FOX_DH
    blockdiag = (half[:, None] == half[None, :]).astype(F32)
    return e_mean, e_expand, blockdiag


def _pair_layout(c):
    heads = c[:, :FOX_HEADS].T
    crow = heads.reshape(FOX_PAIRS, 2, ROWS)
    c2 = jnp.broadcast_to(crow.transpose(0, 2, 1)[:, :, :, None], (FOX_PAIRS, ROWS, 2, FOX_DH))
    crow = jnp.where(jnp.arange(ROWS) >= FIRST_ROW, crow, KEY_OFF)
    return c2.reshape(FOX_PAIRS, ROWS, LANE), crow


def _fox_fwd(y, p):
    proj = _matmul(y, p["w_in"], "nn", F32, "fox_in")
    e_mean, e_expand, _ = _fox_consts()
    nb = D_MODEL // LANE
    xs = [_x(proj, D_MODEL, 0), _x(proj, D_MODEL, 1), _x(proj, LANE, 4 * nb)]
    ps = [p["q_gain"], p["k_gain"], p["b_f"]]
    qn, kn, lf = _row_fwd(_fox_pre_fn, xs, ps, [e_mean, e_expand],
                          [(D_MODEL, D_MODEL, BF16), (D_MODEL, D_MODEL, BF16), (LANE, LANE, F32)], "fox_pre_fwd")
    c = _seq_cumsum(lf, False, "fox_cumsum")
    c2, crow = _pair_layout(c)
    o, lse2 = _fox_attn_fwd(qn, kn, proj, c2, crow)
    om = _row_fwd(_fox_post_fn, [_x(o, D_MODEL, 0), _x(proj, D_MODEL, 3)], [], [],
                  [(D_MODEL, D_MODEL, BF16)], "fox_post_fwd")[0]
    return om, (proj, qn, kn, c2, crow, o, lse2)


def _fox_bwd(dom, y, saved, p):
    proj, qn, kn, c2, crow, o, lse2 = saved
    e_mean, e_expand, blockdiag = _fox_consts()
    nb = D_MODEL // LANE
    do, dgate = _row_bwd(_fox_post_fn, [_x(o, D_MODEL, 0), _x(proj, D_MODEL, 3)], [], [], [_x(dom, D_MODEL)],
                         [(D_MODEL, F32), (D_MODEL, BF16)], "fox_post_bwd")
    dq, dk, dv, dcrow, dcq2 = _fox_attn_bwd(qn, kn, proj, c2, crow, lse2, o, do, blockdiag)
    dlf = _seq_cumsum(_pad_cols(dcrow.reshape(FOX_HEADS, ROWS).T, LANE), True, "fox_cumsum_rev", pairs=dcq2)
    xs = [_x(proj, D_MODEL, 0), _x(proj, D_MODEL, 1), _x(proj, LANE, 4 * nb)]
    ps = [p["q_gain"], p["k_gain"], p["b_f"]]
    dqr, dkr, df, dqg, dkg, dbf = _row_bwd(
        _fox_pre_fn, xs, ps, [e_mean, e_expand], [_x(dq, D_MODEL), _x(dk, D_MODEL), _x(dlf, LANE)],
        [(D_MODEL, BF16), (D_MODEL, BF16), (LANE, BF16)], "fox_pre_bwd")
    dproj = jnp.concatenate([dqr, dkr, dv, dgate, df], axis=1)
    grads = {
        "w_in": _matmul(y, dproj, "tn", F32, "fox_in_dw")[:, :FOX_IN],
        "q_gain": dqg.reshape(FOX_HEADS, FOX_DH).sum(axis=0),
        "k_gain": dkg.reshape(FOX_HEADS, FOX_DH).sum(axis=0),
        "b_f": dbf[0, :FOX_HEADS],
    }
    return dproj, grads


def _gla_fwd(y, p):
    proj = _matmul(y, p["w_in"], "nn", F32, "gla_in")
    small = (2 * GLA_QK + 2 * GLA_V) // LANE
    garr = _row_fwd(_gla_pre_fn, [_x(proj, LANE, small)], [p["w_alpha"], p["b_alpha"]], [],
                    [(GLA_QK, GLA_QK, F32)], "gla_pre_fwd")[0]
    o, states = _gla_scan_fwd(proj, garr)
    om = _row_fwd(_gla_post_fn, [_x(o, GLA_V, 0), _x(proj, GLA_V, 2)], [p["o_gain"]], [],
                  [(GLA_V, GLA_V, BF16)], "gla_post_fwd")[0]
    return om, (proj, garr, states, o)


def _gla_bwd(dom, y, saved, p):
    proj, garr, states, o = saved
    small = (2 * GLA_QK + 2 * GLA_V) // LANE
    do, dr, dgain = _row_bwd(_gla_post_fn, [_x(o, GLA_V, 0), _x(proj, GLA_V, 2)], [p["o_gain"]], [],
                             [_x(dom, GLA_V)], [(GLA_V, F32), (GLA_V, BF16)], "gla_post_bwd")
    dq, dk, dv, dg = _gla_scan_bwd(proj, garr, states, do)
    dalr, dwa, dba = _row_bwd(_gla_pre_fn, [_x(proj, LANE, small)], [p["w_alpha"], p["b_alpha"]], [],
                              [_x(dg, GLA_QK)], [(LANE, BF16)], "gla_pre_bwd")
    dproj = jnp.concatenate([dq, dk, dv, dr, dalr], axis=1)
    grads = {
        "w_in": _matmul(y, dproj, "tn", F32, "gla_in_dw")[:, :GLA_IN],
        "w_alpha": dwa[:GLA_RANK], "b_alpha": dba[0], "o_gain": dgain[0],
    }
    return dproj, grads


def _gdn_fwd(y, p):
    proj = _matmul(y, p["w_in"], "nn", F32, "gdn_in")
    nb = D_MODEL // LANE
    qn = _gdn_conv_fwd(proj, p["conv_q"], "q", 0)
    kn = _gdn_conv_fwd(proj, p["conv_k"], "k", 1)
    vn = _gdn_conv_fwd(proj, p["conv_v"], "v", 2)
    gates = _row_fwd(_gdn_gates_fn, [_x(proj, LANE, 4 * nb)], [p["a_log"], p["dt_bias"]], [],
                     [(LANE, LANE, F32)], "gdn_gates_fwd")[0]
    o, states = _gdn_scan_fwd(qn, kn, vn, gates)
    om = _row_fwd(_gdn_post_fn, [_x(o, D_MODEL, 0), _x(proj, D_MODEL, 3)], [p["o_gain"]], [],
                  [(D_MODEL, D_MODEL, BF16)], "gdn_post_fwd")[0]
    return om, (proj, qn, kn, vn, gates, states, o)


def _gdn_bwd(dom, y, saved, p):
    proj, qn, kn, vn, gates, states, o = saved
    nb = D_MODEL // LANE
    do, dgate, dgain = _row_bwd(_gdn_post_fn, [_x(o, D_MODEL, 0), _x(proj, D_MODEL, 3)], [p["o_gain"]], [],
                                [_x(dom, D_MODEL)], [(D_MODEL, F32), (D_MODEL, BF16)], "gdn_post_bwd")
    dqn, dkn, dvn, dgates = _gdn_scan_bwd(qn, kn, vn, gates, states, do)
    dsm, dalog, ddtb = _row_bwd(_gdn_gates_fn, [_x(proj, LANE, 4 * nb)], [p["a_log"], p["dt_bias"]], [],
                                [_x(dgates, LANE)], [(LANE, BF16)], "gdn_gates_bwd")
    dxq, dwq = _gdn_conv_bwd(proj, p["conv_q"], "q", 0, dqn)
    dxk, dwk = _gdn_conv_bwd(proj, p["conv_k"], "k", 1, dkn)
    dxv, dwv = _gdn_conv_bwd(proj, p["conv_v"], "v", 2, dvn)
    dproj = jnp.concatenate([dxq, dxk, dxv, dgate, dsm], axis=1)
    grads = {
        "w_in": _matmul(y, dproj, "tn", F32, "gdn_in_dw")[:, :GDN_IN],
        "conv_w": jnp.concatenate([dwq[:GDN_CONV], dwk[:GDN_CONV], dwv[:GDN_CONV]], axis=1),
        "a_log": dalog[0, :GDN_HEADS], "dt_bias": ddtb[0, :GDN_HEADS], "o_gain": dgain[0],
    }
    return dproj, grads


_MIXERS = ((_fox_fwd, _fox_bwd), (_gla_fwd, _gla_bwd), (_gdn_fwd, _gdn_bwd))


def _mixer_params(i, wts, rep):
    kind, j = i % 3, i // 3
    if kind == 0:
        return {
            "w_in": _pad_cols(wts["fox_w_in"], FOX_IN_PAD), "w_out": wts["fox_w_out"],
            "q_gain": jnp.tile(rep["fox_q_gain"][j], FOX_HEADS)[None],
            "k_gain": jnp.tile(rep["fox_k_gain"][j], FOX_HEADS)[None],
            "b_f": _pad_lanes(rep["fox_b_f"][j][None]),
        }
    if kind == 1:
        return {
            "w_in": _pad_cols(wts["gla_w_in"], GLA_IN_PAD), "w_out": wts["gla_w_out"],
            "w_alpha": jnp.pad(wts["gla_w_alpha2"], ((0, LANE - GLA_RANK), (0, 0))).astype(F32),
            "b_alpha": rep["gla_b_alpha"][j][None], "o_gain": rep["gla_o_gain"][j][None],
        }
    conv = jnp.pad(wts["gdn_conv_w"].astype(F32), ((0, 8 - GDN_CONV), (0, 0)))
    return {
        "w_in": _pad_cols(wts["gdn_w_in"], GDN_IN_PAD), "w_out": wts["gdn_w_out"],
        "conv_q": conv[:, :D_MODEL], "conv_k": conv[:, D_MODEL:2 * D_MODEL], "conv_v": conv[:, 2 * D_MODEL:],
        "a_log": _pad_lanes(rep["gdn_a_log"][j][None]), "dt_bias": _pad_lanes(rep["gdn_dt_bias"][j][None]),
        "o_gain": rep["gdn_o_gain"][j][None],
    }


def kernel(x, meta_tokens, norm_mix, norm_ffn, w_gate_up, w_down, fox_w_in, fox_b_f, fox_q_gain, fox_k_gain, fox_w_out, gla_w_in, gla_w_alpha2, gla_b_alpha, gla_o_gain, gla_w_out, gdn_w_in, gdn_conv_w, gdn_a_log, gdn_dt_bias, gdn_o_gain, gdn_w_out, loss_target, m_meta_tokens, m_norm_mix, m_norm_ffn, m_w_gate_up, m_w_down, m_fox_w_in, m_fox_b_f, m_fox_q_gain, m_fox_k_gain, m_fox_w_out, m_gla_w_in, m_gla_w_alpha2, m_gla_b_alpha, m_gla_o_gain, m_gla_w_out, m_gdn_w_in, m_gdn_conv_w, m_gdn_a_log, m_gdn_dt_bias, m_gdn_o_gain, m_gdn_w_out, v_meta_tokens, v_norm_mix, v_norm_ffn, v_w_gate_up, v_w_down, v_fox_w_in, v_fox_b_f, v_fox_q_gain, v_fox_k_gain, v_fox_w_out, v_gla_w_in, v_gla_w_alpha2, v_gla_b_alpha, v_gla_o_gain, v_gla_w_out, v_gdn_w_in, v_gdn_conv_w, v_gdn_a_log, v_gdn_dt_bias, v_gdn_o_gain, v_gdn_w_out):
    given = dict(locals())
    w_loc = {n: given[n] for n in WEIGHT_ORDER}
    m_loc = {n: given["m_" + n] for n in WEIGHT_ORDER}
    v_loc = {n: given["v_" + n] for n in WEIGHT_ORDER}

    rep = {n: w_loc[n] for n, _ in REPLICATED}
    me = 4 * lax.axis_index("x") + 2 * lax.axis_index("y") + lax.axis_index("c")

    def with_own(lands, owns):
        return [lax.dynamic_update_slice(land, own[None], (me, 0, 0)) for land, own in zip(lands, owns)]

    gathers = {}
    order = jnp.zeros((), F32)
    for i in range(DEPTH):
        groups = _layer_tensors(i)
        for part in PARTS:
            blocks = [(_layer_block(w_loc[n], idx) + order).astype(_wire(n)) for n, idx in groups[part]]
            started = _send_start(blocks, True, "gather_start_%d_%s" % (i, part))
            gathers[(i, part)] = (groups[part], blocks, started)
            order = started[4][0, 0]
    all_started = gathers[(DEPTH - 1, PARTS[-1])][2][4]

    def weights(i, part, after):
        tensors, blocks, started = gathers[(i, part)]
        lands = with_own(_send_wait(started, True, after, "gather_wait_%d_%s" % (i, part)), blocks)
        return {n: _from_gathered(g, n) for (n, _), g in zip(tensors, lands)}

    h = y = None
    saved = []
    for i in range(DEPTH):
        w_mix = weights(i, "mix", all_started if i == 0 else h)
        if i == 0:
            h = jnp.concatenate([jnp.zeros((FIRST_ROW, D_MODEL), F32), w_mix["meta_tokens"], x[0]], axis=0)
            y = _rms_fwd(h, rep["norm_mix"][0][None])
        p = _mixer_params(i, w_mix, rep)
        fwd, _ = _MIXERS[i % 3]
        om, mix_saved = fwd(y, p)
        h_mid, y_mid = _matmul_add_norm(om, p["w_out"], h, rep["norm_ffn"][i][None], "mixer_out_norm")
        w_ffn = weights(i, "ffn", h_mid)
        g_next = rep["norm_mix"][i + 1][None] if i + 1 < DEPTH else None
        h_out, y_next, ffn_saved = _ffn_fwd(h_mid, y_mid, w_ffn["w_gate_up"], w_ffn["w_down"], g_next)
        saved.append((p, w_ffn, h, y, om, mix_saved, ffn_saved))
        h, y = h_out, y_next

    dh, sq = _loss_head(h, loss_target[0])
    loss = lax.psum(0.5 * jnp.sum(sq) * (1.0 / D_MODEL), ("x", "y", "c"))

    gw = {n: [None] * DEPTH for n in ("norm_mix", "norm_ffn")}
    gm = {}
    reduces = {}

    def start_reduce(i, part, full, extra=()):
        tensors = _layer_tensors(i)[part]
        slabs = [_to_slabs(full[n], n).astype(_wire(n)) for n, _ in tensors] + list(extra)
        started = _send_start(slabs, False, "reduce_start_%d_%s" % (i, part))
        reduces[(i, part)] = (tensors, slabs, started)
        return started[4][0, 0]

    order = jnp.zeros((), BF16)
    for i in reversed(range(DEPTH)):
        p, w_ffn, h_in, y, om, mix_saved, ffn_saved = saved[i]
        kind, j = i % 3, i // 3
        pre = ("fox", "gla", "gdn")[kind]
        _, bwd = _MIXERS[kind]
        dh_mid, gw["norm_ffn"][i], dwgu, dwd = _ffn_bwd(
            dh, ffn_saved, rep["norm_ffn"][i][None], w_ffn["w_gate_up"], w_ffn["w_down"] + order)
        order = start_reduce(i, "ffn", {"w_gate_up": dwgu, "w_down": dwd}).astype(BF16)
        dom = _matmul(dh_mid, p["w_out"] + order, "nt", F32, "mixer_out_dx")
        dwo = _matmul(om, dh_mid, "tn", F32, "mixer_out_dw")
        dproj, grads = bwd(dom, y, mix_saved, p)
        gm[(kind, j)] = grads
        dh, gw["norm_mix"][i] = _matmul_norm_bwd(dproj, p["w_in"], 0, h_in, rep["norm_mix"][i][None], dh_mid,
                                                 "mixer_in_dx_norm")
        mix_full = {pre + "_w_in": grads["w_in"], pre + "_w_out": dwo, "gla_w_alpha2": grads.get("w_alpha"),
                    "gdn_conv_w": grads.get("conv_w"), "meta_tokens": dh[FIRST_ROW:PAD_ROWS]}
        if i > 0:
            order = start_reduce(i, "mix", mix_full).astype(BF16)

    partial = {
        "norm_mix": jnp.concatenate(gw["norm_mix"], axis=0), "norm_ffn": jnp.concatenate(gw["norm_ffn"], axis=0),
        "fox_b_f": jnp.stack([gm[(0, 0)]["b_f"], gm[(0, 1)]["b_f"]]),
        "fox_q_gain": jnp.stack([gm[(0, 0)]["q_gain"], gm[(0, 1)]["q_gain"]]),
        "fox_k_gain": jnp.stack([gm[(0, 0)]["k_gain"], gm[(0, 1)]["k_gain"]]),
        "gla_b_alpha": gm[(1, 0)]["b_alpha"][None], "gla_o_gain": gm[(1, 0)]["o_gain"][None],
        "gdn_a_log": gm[(2, 0)]["a_log"][None], "gdn_dt_bias": gm[(2, 0)]["dt_bias"][None],
        "gdn_o_gain": gm[(2, 0)]["o_gain"][None],
    }

    small = _pack([partial[n].reshape(-1) for n, _ in REPLICATED], 8, F32)
    start_reduce(0, "mix", mix_full, extra=[jnp.broadcast_to(small[None], (N_DEV,) + small.shape)])
    recv = {}
    for (i, part), (tensors, slabs, started) in reduces.items():
        owns = [lax.dynamic_index_in_dim(s, me, 0, keepdims=False) for s in slabs]
        lands = with_own(_send_wait(started, False, dh, "reduce_wait_%d_%s" % (i, part)), owns)
        for (n, idx), land in zip(tensors, lands):
            recv[(n, idx)] = land
        if (i, part) == (0, "mix"):
            recv_small = lands[-1]

    res = [{} for _ in range(4)]
    for n, s, _ in SHARDED:
        layers = [idx for i in range(DEPTH) for part in PARTS for (nn, idx) in _layer_tensors(i)[part] if nn == n]
        layers = sorted(layers, key=lambda idx: -1 if idx is None else idx)
        outs = _adamw([recv[(n, idx)] for idx in layers], *[_local_2d(d[n]) for d in (w_loc, m_loc, v_loc)],
                      "adamw_" + n)
        for k in range(4):
            res[k][n] = outs[k].reshape(s)
    outs_small = _adamw([recv_small], *[_pack([d[n].reshape(-1) for n, _ in REPLICATED], 8, F32)
                                        for d in (w_loc, m_loc, v_loc)], "adamw_replicated")
    for k in range(4):
        res[k].update(_unpack(outs_small[k], list(REPLICATED)))

    result = [loss, dh[PAD_ROWS:][None]]
    for k in range(4):
        result += [res[k][n] for n in WEIGHT_ORDER]
    return tuple(result)
```

```python
import functools

import jax
import jax.numpy as jnp
from jax import lax
from jax.experimental import pallas as pl
from jax.experimental.pallas import tpu as pltpu

F32, BF16 = jnp.float32, jnp.bfloat16
HIGHEST = lax.Precision.HIGHEST

D_MODEL = 1024
SEQ = 4096
N_META = 16
DEPTH = 4
NORM_EPS = 1e-6
LANE = 128
PAD_ROWS = 128
FIRST_ROW = PAD_ROWS - N_META
ROWS = PAD_ROWS + SEQ
ROW_TILE = 128
CHUNK = 64
NEG = -1e30
N_DEV = 8
VMEM_LIMIT = 56 * 1024 * 1024
VMEM_BLOCK_BUDGET = 36 * 1024 * 1024

FOX_HEADS, FOX_DH = 16, 64
FOX_IN, FOX_IN_PAD = 4 * D_MODEL + FOX_HEADS, 4 * D_MODEL + LANE
GLA_HEADS, GLA_DK, GLA_DV, GLA_RANK = 4, 128, 256, 16
GLA_QK, GLA_V = GLA_HEADS * GLA_DK, GLA_HEADS * GLA_DV
GLA_IN, GLA_IN_PAD = 2 * GLA_QK + 2 * GLA_V + GLA_RANK, 2 * GLA_QK + 2 * GLA_V + LANE
GDN_HEADS, GDN_DK, GDN_DV, GDN_CONV = 8, 128, 128, 4
GDN_CONV_DIM = 3 * GDN_HEADS * GDN_DK
GDN_IN, GDN_IN_PAD = GDN_CONV_DIM + GDN_HEADS * GDN_DV + 2 * GDN_HEADS, GDN_CONV_DIM + GDN_HEADS * GDN_DV + LANE
D_FF = 2816

ADAM_LR, ADAM_B1, ADAM_B2, ADAM_EPS, ADAM_WD, ADAM_STEP = 0.001, 0.9, 0.999, 1e-08, 0.01, 10

PACK_COLS = 1024
PACK_ROW_TILE = 256

SHARDED = (
    ("meta_tokens", (16, 128), 1),
    ("w_gate_up", (4, 1024, 704), 2),
    ("w_down", (4, 352, 1024), 1),
    ("fox_w_in", (2, 1024, 514), 2),
    ("fox_w_out", (2, 128, 1024), 1),
    ("gla_w_in", (1, 1024, 386), 2),
    ("gla_w_alpha2", (1, 16, 64), 2),
    ("gla_w_out", (1, 128, 1024), 1),
    ("gdn_w_in", (1, 1024, 514), 2),
    ("gdn_conv_w", (1, 4, 1, 384), 3),
    ("gdn_w_out", (1, 128, 1024), 1),
)
REPLICATED = (
    ("norm_mix", (4, 1024)),
    ("norm_ffn", (4, 1024)),
    ("fox_b_f", (2, 16)),
    ("fox_q_gain", (2, 64)),
    ("fox_k_gain", (2, 64)),
    ("gla_b_alpha", (1, 512)),
    ("gla_o_gain", (1, 256)),
    ("gdn_a_log", (1, 8)),
    ("gdn_dt_bias", (1, 8)),
    ("gdn_o_gain", (1, 128)),
)
WEIGHT_ORDER = (
    "meta_tokens", "norm_mix", "norm_ffn", "w_gate_up", "w_down", "fox_w_in", "fox_b_f", "fox_q_gain",
    "fox_k_gain", "fox_w_out", "gla_w_in", "gla_w_alpha2", "gla_b_alpha", "gla_o_gain", "gla_w_out",
    "gdn_w_in", "gdn_conv_w", "gdn_a_log", "gdn_dt_bias", "gdn_o_gain", "gdn_w_out",
)

_DN = {
    "nn": (((1,), (0,)), ((), ())),
    "nt": (((1,), (1,)), ((), ())),
    "tn": (((0,), (0,)), ((), ())),
}


def _dot(a, b, kind):
    return lax.dot_general(a.astype(BF16), b.astype(BF16), _DN[kind], preferred_element_type=F32)


@functools.partial(jax.custom_vjp, nondiff_argnums=(2,))
def _dot_ad(a, b, kind):
    return _dot(a, b, kind)


def _dot_ad_fwd(a, b, kind):
    return _dot(a, b, kind), (a, b)


def _dot_ad_bwd(kind, res, ct):
    a, b = res
    if kind == "nn":
        return _dot(ct, b, "nt"), _dot(a, ct, "tn")
    if kind == "nt":
        return _dot(ct, b, "nn"), _dot(ct, a, "tn")
    return _dot(b, ct, "nt"), _dot(a, ct, "nn")


_dot_ad.defvjp(_dot_ad_fwd, _dot_ad_bwd)


def _dot_hi(a, b, kind):
    return lax.dot_general(a, b, _DN[kind], precision=HIGHEST, preferred_element_type=F32)


def _dot_mid(a, b, kind):
    return lax.dot_general(a, b, _DN[kind], precision=lax.Precision.HIGH, preferred_element_type=F32)


def _iota(shape, dim):
    return lax.broadcasted_iota(jnp.int32, shape, dim)


def _row_mask(i, tm):
    return ((i * tm + _iota((tm, 1), 0)) >= FIRST_ROW).astype(F32)


def _log_sigmoid(z):
    return jnp.minimum(z, 0.0) - jnp.log(1.0 + jnp.exp(-jnp.abs(z)))


def _softplus(z):
    return jnp.maximum(z, 0.0) + jnp.log(1.0 + jnp.exp(-jnp.abs(z)))


def _silu(z):
    return z * jax.nn.sigmoid(z)


def _lane_col(x, lane):
    return jnp.sum(jnp.where(_iota(x.shape, 1) == lane, x, 0.0), axis=1, keepdims=True)


def _last_row(x):
    return jnp.sum(jnp.where(_iota(x.shape, 0) == x.shape[0] - 1, x, 0.0), axis=0, keepdims=True)


def _params(sem, limit=VMEM_LIMIT):
    return pltpu.CompilerParams(dimension_semantics=sem, vmem_limit_bytes=limit)


def _pick(n, options):
    for t in options:
        if n % t == 0:
            return t
    return n


def _matmul(a, b, kind, out_dtype, name, add=None, b_kblock=0):
    if kind == "nn":
        (m, k), n = a.shape, b.shape[1]
    elif kind == "nt":
        (m, k), n = a.shape, b.shape[0]
    else:
        (k, m), n = a.shape, b.shape[1]
    tn = _pick(n, (512, 384, 640, 256, 128))
    tm_options = (1408, 384, 128) if kind != "tn" else (1024, 512, 256, 128)
    tm = None
    for t in tm_options:
        if m % t:
            continue
        need = 2 * (t * k * a.dtype.itemsize + tn * k * b.dtype.itemsize + t * tn * 8)
        if need <= VMEM_BLOCK_BUDGET:
            tm = t
            break
    assert tm is not None, (name, a.shape, b.shape)
    a_bytes, b_bytes = m * k * a.dtype.itemsize, n * k * b.dtype.itemsize
    m_outer = a_bytes + (m // tm) * b_bytes <= b_bytes + (n // tn) * a_bytes

    def ij(fn):
        return (lambda i, j: fn(i, j)) if m_outer else (lambda j, i: fn(i, j))

    def body(*refs):
        a_ref, b_ref = refs[0], refs[1]
        o_ref = refs[-1]
        acc = _dot(a_ref[...], b_ref[...], kind)
        if add is not None:
            acc = acc + refs[2][...]
        o_ref[...] = acc.astype(o_ref.dtype)

    if kind == "tn":
        a_spec = pl.BlockSpec((k, tm), ij(lambda i, j: (0, i)))
    else:
        a_spec = pl.BlockSpec((tm, k), ij(lambda i, j: (i, 0)))
    if kind == "nt":
        b_spec = pl.BlockSpec((tn, k), ij(lambda i, j: (j, b_kblock)))
    else:
        b_spec = pl.BlockSpec((k, tn), ij(lambda i, j: (0, j)))
    o_spec = pl.BlockSpec((tm, tn), ij(lambda i, j: (i, j)))
    operands, in_specs = [a, b], [a_spec, b_spec]
    if add is not None:
        operands.append(add)
        in_specs.append(o_spec)
    grid = (m // tm, n // tn) if m_outer else (n // tn, m // tm)
    return pl.pallas_call(
        body, grid=grid, in_specs=in_specs, out_specs=o_spec,
        out_shape=jax.ShapeDtypeStruct((m, n), out_dtype), name=name,
        compiler_params=_params(("parallel", "parallel")),
    )(*operands)


def _x_spec(tm, width, cb0, moves):
    return pl.BlockSpec((tm, width), lambda i, j: (i, cb0 + j * moves))


def _full_spec(arr):
    nd = arr.ndim
    return pl.BlockSpec(arr.shape, lambda i, j: (0,) * nd)


def _row_fwd(fn, xs, ps, cs, outs, name, tm=ROW_TILE, ncol=1, rows=None):
    rows = ROWS if rows is None else rows
    nx, npar, nc = len(xs), len(ps), len(cs)

    def body(*refs):
        i = pl.program_id(0)
        xv = [r[...].astype(F32) for r in refs[:nx]]
        pv = [r[...] for r in refs[nx:nx + npar + nc]]
        res = fn(i, *xv, *pv)
        for r, val in zip(refs[nx + npar + nc:], res):
            r[...] = val.astype(r.dtype)

    in_specs = [_x_spec(tm, w, cb0, mv) for (_, w, cb0, mv) in xs]
    in_specs += [_full_spec(p) for p in list(ps) + list(cs)]
    out_specs = [_x_spec(tm, w, 0, 1) for (w, _, _) in outs]
    out_shape = [jax.ShapeDtypeStruct((rows, tot), dt) for (_, tot, dt) in outs]
    return pl.pallas_call(
        body, grid=(rows // tm, ncol), in_specs=in_specs, out_specs=out_specs, out_shape=out_shape,
        name=name, compiler_params=_params(("parallel", "parallel")),
    )(*[x[0] for x in xs], *ps, *cs)


def _row_bwd(fn, xs, ps, cs, cts, dxs, name, adds=None, tm=ROW_TILE, ncol=1, rows=None):
    rows = ROWS if rows is None else rows
    nx, npar, nc, nct = len(xs), len(ps), len(cs), len(cts)
    adds = [None] * nx if adds is None else adds
    add_idx = [k for k in range(nx) if adds[k] is not None]

    def body(*refs):
        i, j = pl.program_id(0), pl.program_id(1)
        pos = 0
        x_refs = refs[pos:pos + nx]; pos += nx
        p_refs = refs[pos:pos + npar]; pos += npar
        c_refs = refs[pos:pos + nc]; pos += nc
        ct_refs = refs[pos:pos + nct]; pos += nct
        add_refs = refs[pos:pos + len(add_idx)]; pos += len(add_idx)
        dx_refs = refs[pos:pos + nx]; pos += nx
        dp_refs = refs[pos:pos + npar]
        xv = [r[...].astype(F32) for r in x_refs]
        pv = [r[...] for r in p_refs]
        cv = [r[...] for r in c_refs]
        _, vjp = jax.vjp(lambda *args: fn(i, *args, *cv), *xv, *pv)
        grads = vjp(tuple(r[...].astype(F32) for r in ct_refs))
        for k in range(nx):
            val = grads[k]
            if adds[k] is not None:
                val = val + add_refs[add_idx.index(k)][...]
            dx_refs[k][...] = val.astype(dx_refs[k].dtype)
        if npar:
            @pl.when((i == 0) & (j == 0))
            def _():
                for r in dp_refs:
                    r[...] = jnp.zeros(r.shape, r.dtype)

            for k in range(npar):
                dp_refs[k][...] += grads[nx + k]

    in_specs = [_x_spec(tm, w, cb0, mv) for (_, w, cb0, mv) in xs]
    in_specs += [_full_spec(p) for p in list(ps) + list(cs)]
    in_specs += [_x_spec(tm, w, cb0, mv) for (_, w, cb0, mv) in cts]
    in_specs += [_x_spec(tm, xs[k][1], 0, 1) for k in add_idx]
    out_specs = [_x_spec(tm, xs[k][1], 0, 1) for k in range(nx)] + [_full_spec(p) for p in ps]
    out_shape = [jax.ShapeDtypeStruct((rows, tot), dt) for (tot, dt) in dxs]
    out_shape += [jax.ShapeDtypeStruct(p.shape, F32) for p in ps]
    sem = ("arbitrary", "arbitrary") if npar else ("parallel", "parallel")
    return pl.pallas_call(
        body, grid=(rows // tm, ncol), in_specs=in_specs, out_specs=out_specs, out_shape=out_shape,
        name=name, compiler_params=_params(sem),
    )(*[x[0] for x in xs], *ps, *cs, *[c[0] for c in cts], *[adds[k] for k in add_idx])


def _rms_fn(i, h, g):
    return (h * lax.rsqrt(jnp.mean(h * h, axis=1, keepdims=True) + NORM_EPS) * g,)


def _fox_pre_fn(i, q, k, f, qg, kg, bf, e_mean, e_expand):
    tm = q.shape[0]

    def headnorm(x, gain, scale):
        msq = _dot_mid(x * x, e_mean, "nn")
        r = _dot_mid(lax.rsqrt(msq + NORM_EPS), e_expand, "nn")
        return x * r * gain * scale

    qn = headnorm(q, qg, FOX_DH ** -0.5)
    kn = headnorm(k, kg, 1.0)
    keep = (_iota((tm, LANE), 1) < FOX_HEADS).astype(F32) * _row_mask(i, tm)
    return qn, kn, _log_sigmoid(f + bf) * keep


def _fox_post_fn(i, o, gate):
    return (o * jax.nn.sigmoid(gate) * _row_mask(i, o.shape[0]),)


def _gla_pre_fn(i, alr, wa, ba):
    z = _dot_ad(alr, wa, "nn") + ba
    return (_log_sigmoid(z) * (1.0 / 16.0) * _row_mask(i, alr.shape[0]),)


def _headwise_norm_gate(o, gate, gain, dv, mask):
    pieces = []
    for h in range(o.shape[1] // dv):
        oh = o[:, h * dv:(h + 1) * dv]
        pieces.append(oh * lax.rsqrt(jnp.mean(oh * oh, axis=1, keepdims=True) + NORM_EPS) * gain)
    return jnp.concatenate(pieces, axis=1) * _silu(gate) * mask


def _gla_post_fn(i, o, r, gain):
    return (_headwise_norm_gate(o, r, gain, GLA_DV, _row_mask(i, o.shape[0])),)


def _gdn_post_fn(i, o, gate, gain):
    return (_headwise_norm_gate(o, gate, gain, GDN_DV, _row_mask(i, o.shape[0])),)


def _gdn_gates_fn(i, sm, alog, dtb):
    tm = sm.shape[0]
    lane = _iota((tm, LANE), 1)
    g = -jnp.exp(alog) * _softplus(sm + dtb)
    beta = jax.nn.sigmoid(sm)
    out = jnp.where(lane < GDN_HEADS, g, jnp.where(lane < 2 * GDN_HEADS, beta, 0.0))
    return (out * _row_mask(i, tm),)


@functools.partial(jax.custom_vjp, nondiff_argnums=(1,))
def _window_rows(xw, start):
    n = xw.shape[0]
    return pltpu.roll(xw, n - start, 0)[:n // 2]


def _window_rows_fwd(xw, start):
    return _window_rows(xw, start), None


def _window_rows_bwd(start, _, ct):
    return (pltpu.roll(jnp.concatenate([ct, jnp.zeros_like(ct)], axis=0), start, 0),)


_window_rows.defvjp(_window_rows_fwd, _window_rows_bwd)


def _gdn_conv_fn(kind):
    def fn(i, prev, cur, w):
        tm = cur.shape[0]
        xw = jnp.concatenate([prev * jnp.where(i > 0, 1.0, 0.0), cur], axis=0)
        y = jnp.zeros(cur.shape, F32)
        for j in range(GDN_CONV):
            tap = jnp.sum(jnp.where(_iota(w.shape, 0) == j, w, 0.0), axis=0, keepdims=True)
            y = y + _window_rows(xw, tm - (GDN_CONV - 1) + j) * tap
        a = _silu(y * _row_mask(i, tm))
        if kind == "v":
            return (a,)
        scale = GDN_DK ** -0.5 if kind == "q" else 1.0
        pieces = []
        for h in range(GDN_HEADS):
            ah = a[:, h * GDN_DK:(h + 1) * GDN_DK]
            pieces.append(ah * lax.rsqrt(jnp.sum(ah * ah, axis=1, keepdims=True) + NORM_EPS) * scale)
        return (jnp.concatenate(pieces, axis=1),)

    return fn


def _seq_cumsum(x, reverse, name, pairs=None):
    rows, width = x.shape
    nt = rows // ROW_TILE

    def body(*refs):
        x_ref, o_ref, carry = refs[0], refs[-2], refs[-1]

        @pl.when(pl.program_id(0) == 0)
        def _():
            carry[...] = jnp.zeros(carry.shape, F32)

        xv = x_ref[...]
        if pairs is not None:
            lane = _iota(xv.shape, 1)
            for g in range(pairs.shape[0]):
                ta, tb = _pair_cols(refs[1][g])
                xv = xv + jnp.where(lane == 2 * g, ta, 0.0) + jnp.where(lane == 2 * g + 1, tb, 0.0)
        r = _iota((ROW_TILE, ROW_TILE), 0)
        c = _iota((ROW_TILE, ROW_TILE), 1)
        tri = ((c >= r) if reverse else (c <= r)).astype(F32)
        acc = _dot_hi(tri, xv, "nn") + carry[...]
        o_ref[...] = acc
        edge = 0 if reverse else ROW_TILE - 1
        carry[...] = jnp.sum(jnp.where(_iota(acc.shape, 0) == edge, acc, 0.0), axis=0, keepdims=True)

    idx = (lambda i: (nt - 1 - i, 0)) if reverse else (lambda i: (i, 0))
    idx3 = (lambda i: (0, nt - 1 - i, 0)) if reverse else (lambda i: (0, i, 0))
    operands, in_specs = [x], [pl.BlockSpec((ROW_TILE, width), idx)]
    if pairs is not None:
        operands.append(pairs)
        in_specs.append(pl.BlockSpec((pairs.shape[0], ROW_TILE, LANE), idx3))
    return pl.pallas_call(
        body, grid=(nt,), in_specs=in_specs,
        out_specs=pl.BlockSpec((ROW_TILE, width), idx), out_shape=jax.ShapeDtypeStruct(x.shape, F32),
        scratch_shapes=[pltpu.VMEM((1, width), F32)], name=name, compiler_params=_params(("arbitrary",)),
    )(*operands)


FOX_PAIRS = FOX_HEADS // 2
FOX_V_BLOCK = 2 * D_MODEL // LANE


def _pair_cols(x):
    return _lane_col(x, 0), _lane_col(x, FOX_DH)


ATT_TILE = 384
assert ROWS % ATT_TILE == 0
N_ATT = ROWS // ATT_TILE
KEY_OFF = 1e30


def _fox_scores(qa, qb, kj, ca, cb, cra, crb, mask):
    sa = _dot(qa, kj, "nt") + (ca - cra)
    sb = _dot(qb, kj, "nt") + (cb - crb)
    if mask is None:
        return sa, sb
    return jnp.where(mask, sa, NEG), jnp.where(mask, sb, NEG)


def _fox_diag_mask():
    return _iota((ATT_TILE, ATT_TILE), 1) <= _iota((ATT_TILE, ATT_TILE), 0)


def _att_off(j):
    return pl.multiple_of(j * ATT_TILE, ATT_TILE)


def _fox_attn_fwd(qn, kn, proj, c2, crow):
    nt = ROWS // ROW_TILE

    def body(q_ref, k_ref, v_ref, c2_ref, crow_ref, o_ref, lse_ref):
        i = pl.program_id(1)
        q = q_ref[...]
        is_a = _iota((ATT_TILE, LANE), 1) < FOX_DH
        qa, qb = jnp.where(is_a, q, 0), jnp.where(is_a, 0, q)
        ca, cb = _pair_cols(c2_ref[0])

        def block(off, size, mask, carry):
            ma, la, acca, mb, lb, accb = carry
            kj = k_ref[pl.ds(off, size), :]
            vj = v_ref[pl.ds(off, size), :].astype(BF16)
            sa, sb = _fox_scores(qa, qb, kj, ca, cb, crow_ref[0, 0:1, pl.ds(off, size)],
                                 crow_ref[0, 1:2, pl.ds(off, size)], mask)

            def online(s, m, l, acc):
                m_new = jnp.maximum(m, jnp.max(s, axis=1, keepdims=True))
                p = jnp.exp(s - m_new)
                alpha = jnp.exp(m - m_new)
                return m_new, alpha * l + jnp.sum(p, axis=1, keepdims=True), alpha * acc + _dot(p, vj, "nn")

            return online(sa, ma, la, acca) + online(sb, mb, lb, accb)

        m0 = jnp.full((ATT_TILE, 1), NEG, F32)
        l0 = jnp.zeros((ATT_TILE, 1), F32)
        a0 = jnp.zeros((ATT_TILE, LANE), F32)
        carry = lax.fori_loop(0, i, lambda j, cr: block(_att_off(j), ATT_TILE, None, cr), (m0, l0, a0, m0, l0, a0))
        ma, la, acca, mb, lb, accb = block(_att_off(i), ATT_TILE, _fox_diag_mask(), carry)
        o_ref[...] = jnp.where(is_a, acca / la, accb / lb)
        lse_ref[0] = jnp.where(is_a, ma + jnp.log(la), mb + jnp.log(lb))

    return pl.pallas_call(
        body, grid=(FOX_PAIRS, N_ATT),
        in_specs=[
            pl.BlockSpec((ATT_TILE, LANE), lambda g, i: (i, g)),
            pl.BlockSpec((ROWS, LANE), lambda g, i: (0, g)),
            pl.BlockSpec((ROWS, LANE), lambda g, i: (0, FOX_V_BLOCK + g)),
            pl.BlockSpec((1, ATT_TILE, LANE), lambda g, i: (g, i, 0)),
            pl.BlockSpec((1, 2, ROWS), lambda g, i: (g, 0, 0)),
        ],
        out_specs=[
            pl.BlockSpec((ATT_TILE, LANE), lambda g, i: (i, g)),
            pl.BlockSpec((1, ATT_TILE, LANE), lambda g, i: (g, i, 0)),
        ],
        out_shape=[jax.ShapeDtypeStruct((ROWS, D_MODEL), F32), jax.ShapeDtypeStruct((FOX_PAIRS, ROWS, LANE), F32)],
        name="fox_attn_fwd", compiler_params=_params(("parallel", "arbitrary")),
    )(qn, kn, proj, c2, crow)


def _fox_attn_dq(qn, kn, proj, c2, crow, lse2, o, do, blockdiag):
    nt = ROWS // ROW_TILE

    def body(q_ref, k_ref, v_ref, c2_ref, crow_ref, lse_ref, o_ref, do_ref, bd_ref, dq_ref, delta_ref, dcq_ref):
        i = pl.program_id(1)
        q = q_ref[...]
        is_a = _iota((ATT_TILE, LANE), 1) < FOX_DH
        qa, qb = jnp.where(is_a, q, 0), jnp.where(is_a, 0, q)
        ca, cb = _pair_cols(c2_ref[0])
        lsa, lsb = _pair_cols(lse_ref[0])
        do_t = do_ref[...]
        delta2 = _dot_hi(do_t * o_ref[...], bd_ref[...], "nn")
        delta_ref[0] = delta2
        dla, dlb = _pair_cols(delta2)
        doa, dob = jnp.where(is_a, do_t, 0.0).astype(BF16), jnp.where(is_a, 0.0, do_t).astype(BF16)

        def block(off, size, mask, carry):
            dqa, dqb, rsa, rsb = carry
            kj = k_ref[pl.ds(off, size), :]
            vj = v_ref[pl.ds(off, size), :].astype(BF16)
            sa, sb = _fox_scores(qa, qb, kj, ca, cb, crow_ref[0, 0:1, pl.ds(off, size)],
                                 crow_ref[0, 1:2, pl.ds(off, size)], mask)
            dsa = jnp.exp(sa - lsa) * (_dot(doa, vj, "nt") - dla)
            dsb = jnp.exp(sb - lsb) * (_dot(dob, vj, "nt") - dlb)
            return (dqa + _dot(dsa, kj, "nn"), dqb + _dot(dsb, kj, "nn"),
                    rsa + jnp.sum(dsa, axis=1, keepdims=True), rsb + jnp.sum(dsb, axis=1, keepdims=True))

        z = jnp.zeros((ATT_TILE, LANE), F32)
        zc = jnp.zeros((ATT_TILE, 1), F32)
        carry = lax.fori_loop(0, i, lambda j, cr: block(_att_off(j), ATT_TILE, None, cr), (z, z, zc, zc))
        dqa, dqb, rsa, rsb = block(_att_off(i), ATT_TILE, _fox_diag_mask(), carry)
        dq_ref[...] = jnp.where(is_a, dqa, dqb)
        dcq_ref[0] = jnp.where(is_a, rsa, rsb)

    tile = pl.BlockSpec((ATT_TILE, LANE), lambda g, i: (i, g))
    pair_tile = pl.BlockSpec((1, ATT_TILE, LANE), lambda g, i: (g, i, 0))
    return pl.pallas_call(
        body, grid=(FOX_PAIRS, N_ATT),
        in_specs=[
            tile,
            pl.BlockSpec((ROWS, LANE), lambda g, i: (0, g)),
            pl.BlockSpec((ROWS, LANE), lambda g, i: (0, FOX_V_BLOCK + g)),
            pair_tile,
            pl.BlockSpec((1, 2, ROWS), lambda g, i: (g, 0, 0)),
            pair_tile, tile, tile,
            pl.BlockSpec((LANE, LANE), lambda g, i: (0, 0)),
        ],
        out_specs=[tile, pair_tile, pair_tile],
        out_shape=[jax.ShapeDtypeStruct((ROWS, D_MODEL), F32), jax.ShapeDtypeStruct((FOX_PAIRS, ROWS, LANE), F32),
                   jax.ShapeDtypeStruct((FOX_PAIRS, ROWS, LANE), F32)],
        name="fox_attn_dq", compiler_params=_params(("parallel", "arbitrary")),
    )(qn, kn, proj, c2, crow, lse2, o, do, blockdiag)


def _fox_attn_dkv(qn, kn, proj, c2, crow, lse2, delta2, do):
    nt = ROWS // ROW_TILE

    def body(q_ref, k_ref, v_ref, c2_ref, crow_ref, lse_ref, delta_ref, do_ref, dk_ref, dv_ref, dc_ref):
        j = pl.program_id(1)
        kj = k_ref[...]
        vj = v_ref[...].astype(BF16)
        cra, crb = crow_ref[0, 0:1, :], crow_ref[0, 1:2, :]

        def block(off, size, mask, carry):
            dk, dv, dca, dcb = carry
            is_a = _iota((size, LANE), 1) < FOX_DH
            q = q_ref[pl.ds(off, size), :]
            qa, qb = jnp.where(is_a, q, 0), jnp.where(is_a, 0, q)
            do_t = do_ref[pl.ds(off, size), :]
            doa, dob = jnp.where(is_a, do_t, 0.0).astype(BF16), jnp.where(is_a, 0.0, do_t).astype(BF16)
            ca, cb = _pair_cols(c2_ref[0, pl.ds(off, size), :])
            lsa, lsb = _pair_cols(lse_ref[0, pl.ds(off, size), :])
            dla, dlb = _pair_cols(delta_ref[0, pl.ds(off, size), :])
            sa, sb = _fox_scores(qa, qb, kj, ca, cb, cra, crb, mask)
            pa, pb = jnp.exp(sa - lsa), jnp.exp(sb - lsb)
            dsa = pa * (_dot(doa, vj, "nt") - dla)
            dsb = pb * (_dot(dob, vj, "nt") - dlb)
            dv = dv + _dot(pa, doa, "tn") + _dot(pb, dob, "tn")
            dk = dk + _dot(dsa, qa, "tn") + _dot(dsb, qb, "tn")
            dca = dca - jnp.sum(dsa, axis=0, keepdims=True)
            dcb = dcb - jnp.sum(dsb, axis=0, keepdims=True)
            return dk, dv, dca, dcb

        z = jnp.zeros((ATT_TILE, LANE), F32)
        zr = jnp.zeros((1, ATT_TILE), F32)
        carry = block(_att_off(j), ATT_TILE, _fox_diag_mask(), (z, z, zr, zr))
        dk, dv, dca, dcb = lax.fori_loop(j + 1, N_ATT, lambda i, cr: block(_att_off(i), ATT_TILE, None, cr), carry)
        dk_ref[...] = dk
        dv_ref[...] = dv.astype(dv_ref.dtype)
        dc_ref[0, 0:1, :] = dca
        dc_ref[0, 1:2, :] = dcb

    whole = pl.BlockSpec((ROWS, LANE), lambda g, j: (0, g))
    pair_whole = pl.BlockSpec((1, ROWS, LANE), lambda g, j: (g, 0, 0))
    tile = pl.BlockSpec((ATT_TILE, LANE), lambda g, j: (j, g))
    return pl.pallas_call(
        body, grid=(FOX_PAIRS, N_ATT),
        in_specs=[
            whole, tile,
            pl.BlockSpec((ATT_TILE, LANE), lambda g, j: (j, FOX_V_BLOCK + g)),
            pair_whole,
            pl.BlockSpec((1, 2, ATT_TILE), lambda g, j: (g, 0, j)),
            pair_whole, pair_whole, whole,
        ],
        out_specs=[tile, tile, pl.BlockSpec((1, 2, ATT_TILE), lambda g, j: (g, 0, j))],
        out_shape=[jax.ShapeDtypeStruct((ROWS, D_MODEL), F32), jax.ShapeDtypeStruct((ROWS, D_MODEL), BF16),
                   jax.ShapeDtypeStruct((FOX_PAIRS, 2, ROWS), F32)],
        name="fox_attn_dkv", compiler_params=_params(("parallel", "arbitrary")),
    )(qn, kn, proj, c2, crow, lse2, delta2, do)


def _fox_attn_bwd(qn, kn, proj, c2, crow, lse2, o, do, blockdiag):
    def body(q_ref, k_ref, v_ref, c2_ref, crow_ref, lse_ref, o_ref, do_ref, bd_ref,
             dq_ref, dk_ref, dv_ref, dc_ref, dcq_ref, delta_ref):
        j = pl.program_id(1)

        @pl.when(j == 0)
        def _():
            for t in range(N_ATT):
                rows = pl.ds(t * ATT_TILE, ATT_TILE)
                delta_ref[rows, :] = _dot_hi(do_ref[rows, :] * o_ref[rows, :], bd_ref[...], "nn")
            dq_ref[...] = jnp.zeros(dq_ref.shape, F32)
            dcq_ref[...] = jnp.zeros(dcq_ref.shape, F32)

        kj = k_ref[...]
        vj = v_ref[...].astype(BF16)
        cra, crb = crow_ref[0, 0:1, :], crow_ref[0, 1:2, :]
        is_a = _iota((ATT_TILE, LANE), 1) < FOX_DH

        def block(off, mask, carry):
            dk, dv, dca, dcb = carry
            rows = pl.ds(off, ATT_TILE)
            q = q_ref[rows, :]
            qa, qb = jnp.where(is_a, q, 0), jnp.where(is_a, 0, q)
            do_t = do_ref[rows, :]
            doa, dob = jnp.where(is_a, do_t, 0.0).astype(BF16), jnp.where(is_a, 0.0, do_t).astype(BF16)
            ca, cb = _pair_cols(c2_ref[0, rows, :])
            lsa, lsb = _pair_cols(lse_ref[0, rows, :])
            dla, dlb = _pair_cols(delta_ref[rows, :])
            sa, sb = _fox_scores(qa, qb, kj, ca, cb, cra, crb, mask)
            pa, pb = jnp.exp(sa - lsa), jnp.exp(sb - lsb)
            dsa = pa * (_dot(doa, vj, "nt") - dla)
            dsb = pb * (_dot(dob, vj, "nt") - dlb)
            dq_ref[rows, :] += jnp.where(is_a, _dot(dsa, kj, "nn"), _dot(dsb, kj, "nn"))
            dcq_ref[0, rows, :] += jnp.where(is_a, jnp.sum(dsa, axis=1, keepdims=True),
                                             jnp.sum(dsb, axis=1, keepdims=True))
            dv = dv + _dot(pa, doa, "tn") + _dot(pb, dob, "tn")
            dk = dk + _dot(dsa, qa, "tn") + _dot(dsb, qb, "tn")
            dca = dca - jnp.sum(dsa, axis=0, keepdims=True)
            dcb = dcb - jnp.sum(dsb, axis=0, keepdims=True)
            return dk, dv, dca, dcb

        z = jnp.zeros((ATT_TILE, LANE), F32)
        zr = jnp.zeros((1, ATT_TILE), F32)
        carry = block(_att_off(j), _fox_diag_mask(), (z, z, zr, zr))
        dk, dv, dca, dcb = lax.fori_loop(j + 1, N_ATT, lambda i, cr: block(_att_off(i), None, cr), carry)
        dk_ref[...] = dk
        dv_ref[...] = dv.astype(dv_ref.dtype)
        dc_ref[0, 0:1, :] = dca
        dc_ref[0, 1:2, :] = dcb

    whole = pl.BlockSpec((ROWS, LANE), lambda g, j: (0, g))
    pair_whole = pl.BlockSpec((1, ROWS, LANE), lambda g, j: (g, 0, 0))
    tile = pl.BlockSpec((ATT_TILE, LANE), lambda g, j: (j, g))
    key_rows = pl.BlockSpec((1, 2, ATT_TILE), lambda g, j: (g, 0, j))
    return pl.pallas_call(
        body, grid=(FOX_PAIRS, N_ATT),
        in_specs=[
            whole, tile,
            pl.BlockSpec((ATT_TILE, LANE), lambda g, j: (j, FOX_V_BLOCK + g)),
            pair_whole, key_rows, pair_whole, whole, whole,
            pl.BlockSpec((LANE, LANE), lambda g, j: (0, 0)),
        ],
        out_specs=[whole, tile, tile, key_rows, pair_whole],
        out_shape=[jax.ShapeDtypeStruct((ROWS, D_MODEL), F32), jax.ShapeDtypeStruct((ROWS, D_MODEL), F32),
                   jax.ShapeDtypeStruct((ROWS, D_MODEL), BF16), jax.ShapeDtypeStruct((FOX_PAIRS, 2, ROWS), F32),
                   jax.ShapeDtypeStruct((FOX_PAIRS, ROWS, LANE), F32)],
        scratch_shapes=[pltpu.VMEM((ROWS, LANE), F32)],
        name="fox_attn_bwd", compiler_params=_params(("parallel", "arbitrary")),
    )(qn, kn, proj, c2, crow, lse2, o, do, blockdiag)


def _gla_chunk(q, k, v, g, st, dot):
    n = len(q)
    c = q[0].shape[0]
    incl = _iota((c, c), 1) <= _iota((c, c), 0)
    incl_f = incl.astype(F32)
    b = [_dot_hi(incl_f, g[i], "nn") for i in range(n)]
    qe = [q[i] * (GLA_DK ** -0.5) * jnp.exp(b[i]) for i in range(n)]
    ke = [k[i] * jnp.exp(-b[i]) for i in range(n)]
    att = [jnp.where(incl, dot(qe[i], ke[i], "nt"), 0.0) for i in range(n)]
    o_intra = [dot(att[i], v[i], "nn") for i in range(n)]
    o_inter = [dot(qe[i], st[i], "nt") for i in range(n)]
    b_last = [_last_row(b[i]) for i in range(n)]
    kd = [k[i] * jnp.exp(b_last[i] - b[i]) for i in range(n)]
    upd = [dot(v[i], kd[i], "tn") for i in range(n)]
    return ([o_intra[i] + o_inter[i] for i in range(n)],
            [st[i] * jnp.exp(b_last[i]) + upd[i] for i in range(n)])


def _tri_inverse(low):
    c = low[0].shape[0]
    eye = (_iota((c, c), 0) == _iota((c, c), 1)).astype(F32)
    x, p = [eye - l for l in low], list(low)
    span = 2
    while span < c:
        p = [_dot_mid(pi, pi, "nn") for pi in p]
        x = [_dot_mid(xi, eye + pi, "nn") for xi, pi in zip(x, p)]
        span *= 2
    return x


def _solve(low, rhs):
    return [_dot_mid(inv, r, "nn") for inv, r in zip(_tri_inverse(low), rhs)]


@jax.custom_vjp
def _solve_ad(low, rhs):
    return _solve(low, rhs)


def _solve_ad_fwd(low, rhs):
    inv = _tri_inverse(low)
    sol = [_dot_mid(i, r, "nn") for i, r in zip(inv, rhs)]
    return sol, (inv, sol)


def _solve_ad_bwd(res, ct):
    inv, sol = res
    drhs = [_dot_mid(i, c, "tn") for i, c in zip(inv, ct)]
    return [-_dot_mid(d, s, "nt") for d, s in zip(drhs, sol)], drhs


_solve_ad.defvjp(_solve_ad_fwd, _solve_ad_bwd)


def _gdn_chunk(q, k, v, gates, st, heads, dot, solve):
    n = len(q)
    c = q[0].shape[0]
    incl = _iota((c, c), 1) <= _iota((c, c), 0)
    strict = _iota((c, c), 1) < _iota((c, c), 0)
    incl_f = incl.astype(F32)
    upper_f = (_iota((c, c), 0) <= _iota((c, c), 1)).astype(F32)
    gcol = [_lane_col(gates, h) for h in heads]
    beta = [_lane_col(gates, h + GDN_HEADS) for h in heads]
    b = [_dot_hi(incl_f, jnp.broadcast_to(gc, (c, LANE)), "nn") for gc in gcol]
    g_sq = [jnp.broadcast_to(gc, (c, c)) for gc in gcol]
    b_col = [_dot_hi(incl_f, gs, "nn") for gs in g_sq]
    b_row = [_dot_hi(gs, upper_f, "tn") for gs in g_sq]
    diff = [b_col[i] - b_row[i] for i in range(n)]
    kb = [k[i] * beta[i] for i in range(n)]
    vb = [v[i] * beta[i] for i in range(n)]
    low = [dot(kb[i], k[i], "nt") * jnp.exp(jnp.where(strict, diff[i], NEG)) for i in range(n)]
    eb = [jnp.exp(bi) for bi in b]
    sol = solve(low, [jnp.concatenate([vb[i], kb[i] * eb[i]], axis=1) for i in range(n)])
    w_st = [dot(sol[i][:, GDN_DV:], st[i], "nn") for i in range(n)]
    v_new = [sol[i][:, :GDN_DV] - w_st[i] for i in range(n)]
    att = [dot(q[i], k[i], "nt") * jnp.exp(jnp.where(incl, diff[i], NEG)) for i in range(n)]
    o_intra = [dot(att[i], v_new[i], "nn") for i in range(n)]
    o_inter = [dot(q[i] * eb[i], st[i], "nn") for i in range(n)]
    b_last = [_last_row(bi) for bi in b]
    kd = [k[i] * jnp.exp(b_last[i] - b[i]) for i in range(n)]
    upd = [dot(kd[i], v_new[i], "tn") for i in range(n)]
    return ([o_intra[i] + o_inter[i] for i in range(n)],
            [st[i] * jnp.exp(b_last[i]) + upd[i] for i in range(n)])


N_CHUNKS = ROWS // CHUNK
GDN_GROUP = 8


def _head_lanes(width, n):
    return [slice(i * width, (i + 1) * width) for i in range(n)]


def _gla_scan_fwd(proj, garr):
    kl, vl = _head_lanes(GLA_DK, GLA_HEADS), _head_lanes(GLA_DV, GLA_HEADS)

    def body(q_ref, k_ref, v_ref, g_ref, o_ref, states_ref, st):
        @pl.when(pl.program_id(0) == 0)
        def _():
            st[...] = jnp.zeros(st.shape, F32)

        s_in = [st[i] for i in range(GLA_HEADS)]
        o, s_out = _gla_chunk([q_ref[:, ln] for ln in kl], [k_ref[:, ln] for ln in kl], [v_ref[:, ln] for ln in vl],
                              [g_ref[:, ln] for ln in kl], s_in, _dot)
        for i in range(GLA_HEADS):
            states_ref[i, 0] = s_in[i]
            o_ref[:, vl[i]] = o[i]
            st[i] = s_out[i]

    return pl.pallas_call(
        body, grid=(N_CHUNKS,),
        in_specs=[
            pl.BlockSpec((CHUNK, GLA_QK), lambda c: (c, 0)),
            pl.BlockSpec((CHUNK, GLA_QK), lambda c: (c, 1)),
            pl.BlockSpec((CHUNK, GLA_V), lambda c: (c, 1)),
            pl.BlockSpec((CHUNK, GLA_QK), lambda c: (c, 0)),
        ],
        out_specs=[
            pl.BlockSpec((CHUNK, GLA_V), lambda c: (c, 0)),
            pl.BlockSpec((GLA_HEADS, 1, GLA_DV, GLA_DK), lambda c: (0, c, 0, 0)),
        ],
        out_shape=[jax.ShapeDtypeStruct((ROWS, GLA_V), F32),
                   jax.ShapeDtypeStruct((GLA_HEADS, N_CHUNKS, GLA_DV, GLA_DK), F32)],
        scratch_shapes=[pltpu.VMEM((GLA_HEADS, GLA_DV, GLA_DK), F32)],
        name="gla_scan_fwd", compiler_params=_params(("arbitrary",)),
    )(proj, proj, proj, garr)


def _gla_scan_bwd(proj, garr, states, do):
    last = N_CHUNKS - 1
    kl, vl = _head_lanes(GLA_DK, GLA_HEADS), _head_lanes(GLA_DV, GLA_HEADS)

    def body(q_ref, k_ref, v_ref, g_ref, s_ref, do_ref, dq_ref, dk_ref, dv_ref, dg_ref, dst):
        @pl.when(pl.program_id(0) == 0)
        def _():
            dst[...] = jnp.zeros(dst.shape, F32)

        _, vjp = jax.vjp(lambda q, k, v, g, s: _gla_chunk(q, k, v, g, s, _dot_ad),
                         [q_ref[:, ln] for ln in kl], [k_ref[:, ln] for ln in kl], [v_ref[:, ln] for ln in vl],
                         [g_ref[:, ln] for ln in kl], [s_ref[i, 0] for i in range(GLA_HEADS)])
        dq, dk, dv, dg, ds = vjp(([do_ref[:, ln] for ln in vl], [dst[i] for i in range(GLA_HEADS)]))
        for i in range(GLA_HEADS):
            dq_ref[:, kl[i]] = dq[i].astype(dq_ref.dtype)
            dk_ref[:, kl[i]] = dk[i].astype(dk_ref.dtype)
            dv_ref[:, vl[i]] = dv[i].astype(dv_ref.dtype)
            dg_ref[:, kl[i]] = dg[i]
            dst[i] = ds[i]

    kspec = pl.BlockSpec((CHUNK, GLA_QK), lambda c: (last - c, 0))
    vspec = pl.BlockSpec((CHUNK, GLA_V), lambda c: (last - c, 0))
    return pl.pallas_call(
        body, grid=(N_CHUNKS,),
        in_specs=[
            kspec,
            pl.BlockSpec((CHUNK, GLA_QK), lambda c: (last - c, 1)),
            pl.BlockSpec((CHUNK, GLA_V), lambda c: (last - c, 1)),
            kspec,
            pl.BlockSpec((GLA_HEADS, 1, GLA_DV, GLA_DK), lambda c: (0, last - c, 0, 0)),
            vspec,
        ],
        out_specs=[kspec, kspec, vspec, kspec],
        out_shape=[jax.ShapeDtypeStruct((ROWS, GLA_QK), BF16), jax.ShapeDtypeStruct((ROWS, GLA_QK), BF16),
                   jax.ShapeDtypeStruct((ROWS, GLA_V), BF16), jax.ShapeDtypeStruct((ROWS, GLA_QK), F32)],
        scratch_shapes=[pltpu.VMEM((GLA_HEADS, GLA_DV, GLA_DK), F32)],
        name="gla_scan_bwd", compiler_params=_params(("arbitrary",)),
    )(proj, proj, proj, garr, states, do)


def _gdn_scan_fwd(qn, kn, vn, gates):
    def body(q_ref, k_ref, v_ref, g_ref, o_ref, states_ref, st):
        @pl.when((pl.program_id(0) == 0) & (pl.program_id(1) == 0))
        def _():
            st[...] = jnp.zeros(st.shape, F32)

        heads = [pl.program_id(1) * GDN_GROUP + hh for hh in range(GDN_GROUP)]
        lanes = _head_lanes(GDN_DK, GDN_GROUP)
        s_in = [st[h] for h in heads]
        o, s_out = _gdn_chunk([q_ref[:, ln] for ln in lanes], [k_ref[:, ln] for ln in lanes],
                              [v_ref[:, ln] for ln in lanes], g_ref[...], s_in, heads, _dot, _solve)
        for hh in range(GDN_GROUP):
            states_ref[hh, 0] = s_in[hh]
            o_ref[:, lanes[hh]] = o[hh]
            st[heads[hh]] = s_out[hh]

    hspec = pl.BlockSpec((CHUNK, GDN_GROUP * GDN_DK), lambda c, h: (c, h))
    return pl.pallas_call(
        body, grid=(N_CHUNKS, GDN_HEADS // GDN_GROUP),
        in_specs=[hspec, hspec, hspec, pl.BlockSpec((CHUNK, LANE), lambda c, h: (c, 0))],
        out_specs=[hspec, pl.BlockSpec((GDN_GROUP, 1, GDN_DK, GDN_DV), lambda c, h: (h, c, 0, 0))],
        out_shape=[jax.ShapeDtypeStruct((ROWS, D_MODEL), F32),
                   jax.ShapeDtypeStruct((GDN_HEADS, N_CHUNKS, GDN_DK, GDN_DV), F32)],
        scratch_shapes=[pltpu.VMEM((GDN_HEADS, GDN_DK, GDN_DV), F32)],
        name="gdn_scan_fwd", compiler_params=_params(("arbitrary", "arbitrary")),
    )(qn, kn, vn, gates)


def _gdn_scan_bwd(qn, kn, vn, gates, states, do):
    last = N_CHUNKS - 1

    def body(q_ref, k_ref, v_ref, g_ref, s_ref, do_ref, dq_ref, dk_ref, dv_ref, dg_ref, dst):
        @pl.when((pl.program_id(0) == 0) & (pl.program_id(1) == 0))
        def _():
            dst[...] = jnp.zeros(dst.shape, F32)

        @pl.when(pl.program_id(1) == 0)
        def _():
            dg_ref[...] = jnp.zeros(dg_ref.shape, F32)

        heads = [pl.program_id(1) * GDN_GROUP + hh for hh in range(GDN_GROUP)]
        lanes = _head_lanes(GDN_DK, GDN_GROUP)
        _, vjp = jax.vjp(lambda q, k, v, g, s: _gdn_chunk(q, k, v, g, s, heads, _dot_ad, _solve_ad),
                         [q_ref[:, ln] for ln in lanes], [k_ref[:, ln] for ln in lanes],
                         [v_ref[:, ln] for ln in lanes], g_ref[...], [s_ref[hh, 0] for hh in range(GDN_GROUP)])
        dq, dk, dv, dg, ds = vjp(([do_ref[:, ln] for ln in lanes], [dst[h] for h in heads]))
        for hh in range(GDN_GROUP):
            dq_ref[:, lanes[hh]] = dq[hh]
            dk_ref[:, lanes[hh]] = dk[hh]
            dv_ref[:, lanes[hh]] = dv[hh]
            dst[heads[hh]] = ds[hh]
        dg_ref[...] += dg

    hspec = pl.BlockSpec((CHUNK, GDN_GROUP * GDN_DK), lambda c, h: (last - c, h))
    gspec = pl.BlockSpec((CHUNK, LANE), lambda c, h: (last - c, 0))
    return pl.pallas_call(
        body, grid=(N_CHUNKS, GDN_HEADS // GDN_GROUP),
        in_specs=[hspec, hspec, hspec, gspec,
                  pl.BlockSpec((GDN_GROUP, 1, GDN_DK, GDN_DV), lambda c, h: (h, last - c, 0, 0)), hspec],
        out_specs=[hspec, hspec, hspec, gspec],
        out_shape=[jax.ShapeDtypeStruct((ROWS, D_MODEL), F32)] * 3 + [jax.ShapeDtypeStruct((ROWS, LANE), F32)],
        scratch_shapes=[pltpu.VMEM((GDN_HEADS, GDN_DK, GDN_DV), F32)],
        name="gdn_scan_bwd", compiler_params=_params(("arbitrary", "arbitrary")),
    )(qn, kn, vn, gates, states, do)


def _gdn_conv_fwd(proj, w, kind, cb):
    fn = _gdn_conv_fn(kind)
    nt = ROWS // ROW_TILE

    def body(prev_ref, cur_ref, w_ref, o_ref):
        (out,) = fn(pl.program_id(0), prev_ref[...], cur_ref[...], w_ref[...])
        o_ref[...] = out

    return pl.pallas_call(
        body, grid=(nt,),
        in_specs=[pl.BlockSpec((ROW_TILE, D_MODEL), lambda i: (jnp.maximum(i - 1, 0), cb)),
                  pl.BlockSpec((ROW_TILE, D_MODEL), lambda i: (i, cb)),
                  pl.BlockSpec(w.shape, lambda i: (0, 0))],
        out_specs=pl.BlockSpec((ROW_TILE, D_MODEL), lambda i: (i, 0)),
        out_shape=jax.ShapeDtypeStruct((ROWS, D_MODEL), F32),
        name="gdn_conv_fwd_" + kind, compiler_params=_params(("parallel",)),
    )(proj, proj, w)


def _gdn_conv_bwd(proj, w, kind, cb, ct):
    fn = _gdn_conv_fn(kind)
    nt = ROWS // ROW_TILE

    def body(prev_ref, cur_ref, w_ref, ct_ref, dx_ref, dw_ref, carry):
        step = pl.program_id(0)
        i = nt - 1 - step

        @pl.when(step == 0)
        def _():
            carry[...] = jnp.zeros(carry.shape, F32)
            dw_ref[...] = jnp.zeros(dw_ref.shape, F32)

        _, vjp = jax.vjp(lambda p, c, ww: fn(i, p, c, ww), prev_ref[...], cur_ref[...], w_ref[...])
        dprev, dcur, dw = vjp((ct_ref[...],))
        dx_ref[...] = (dcur + carry[...]).astype(dx_ref.dtype)
        carry[...] = dprev
        dw_ref[...] += dw

    return pl.pallas_call(
        body, grid=(nt,),
        in_specs=[pl.BlockSpec((ROW_TILE, D_MODEL), lambda s: (jnp.maximum(nt - 2 - s, 0), cb)),
                  pl.BlockSpec((ROW_TILE, D_MODEL), lambda s: (nt - 1 - s, cb)),
                  pl.BlockSpec(w.shape, lambda s: (0, 0)),
                  pl.BlockSpec((ROW_TILE, D_MODEL), lambda s: (nt - 1 - s, 0))],
        out_specs=[pl.BlockSpec((ROW_TILE, D_MODEL), lambda s: (nt - 1 - s, 0)),
                   pl.BlockSpec(w.shape, lambda s: (0, 0))],
        out_shape=[jax.ShapeDtypeStruct((ROWS, D_MODEL), BF16), jax.ShapeDtypeStruct(w.shape, F32)],
        scratch_shapes=[pltpu.VMEM((ROW_TILE, D_MODEL), F32)],
        name="gdn_conv_bwd_" + kind, compiler_params=_params(("arbitrary",)),
    )(proj, proj, w, ct)


def _loss_head(h, target):
    nt = ROWS // ROW_TILE
    lead = PAD_ROWS // ROW_TILE

    def body(h_ref, t_ref, dh_ref, acc_ref):
        i = pl.program_id(0)

        @pl.when(i == 0)
        def _():
            acc_ref[...] = jnp.zeros(acc_ref.shape, F32)

        err = (h_ref[...] - t_ref[...]) * jnp.where(i >= lead, 1.0, 0.0)
        dh_ref[...] = err * (1.0 / D_MODEL)
        acc_ref[...] += jnp.sum(err * err, axis=0, keepdims=True)

    return pl.pallas_call(
        body, grid=(nt,),
        in_specs=[pl.BlockSpec((ROW_TILE, D_MODEL), lambda i: (i, 0)),
                  pl.BlockSpec((ROW_TILE, D_MODEL), lambda i: (jnp.maximum(i - lead, 0), 0))],
        out_specs=[pl.BlockSpec((ROW_TILE, D_MODEL), lambda i: (i, 0)),
                   pl.BlockSpec((1, D_MODEL), lambda i: (0, 0))],
        out_shape=[jax.ShapeDtypeStruct((ROWS, D_MODEL), F32), jax.ShapeDtypeStruct((1, D_MODEL), F32)],
        name="loss_head", compiler_params=_params(("arbitrary",)),
    )(h, target)


MESH = pl.DeviceIdType.MESH
ANY = pl.BlockSpec(memory_space=pl.ANY)


N_PEERS = N_DEV - 1


HBM_SPEC = pl.BlockSpec(memory_space=pltpu.HBM)
SEM_SPEC = pl.BlockSpec(memory_space=pltpu.SEMAPHORE)
SIDE_EFFECT = pltpu.SideEffectType.DATAFLOW_SIDE_EFFECTING


def _me_and_peers():
    x, y, c = lax.axis_index("x"), lax.axis_index("y"), lax.axis_index("c")
    peers = []
    for k in range(1, N_DEV):
        pid = (1 - x if k & 4 else x, 1 - y if k & 2 else y, 1 - c if k & 1 else c)
        peers.append((4 * pid[0] + 2 * pid[1] + pid[2], pid))
    return 4 * x + 2 * y + c, peers


def _split_copies(src_refs, land_refs, send_sems, recv_sems, spread):
    me, peers = _me_and_peers()
    out = []
    for t, (src, land) in enumerate(zip(src_refs, land_refs)):
        for k, (peer, pid) in enumerate(peers):
            sems = dict(send_sem=send_sems.at[N_PEERS * t + k], recv_sem=recv_sems.at[N_PEERS * t + k],
                        device_id=pid, device_id_type=MESH)
            piece = src if spread else src.at[peer]
            out.append((pltpu.make_async_remote_copy(src_ref=piece, dst_ref=land.at[me], **sems),
                        pltpu.make_async_remote_copy(src_ref=piece, dst_ref=land.at[peer], **sems)))
    return out


def _send_start(srcs, spread, name):
    n = len(srcs)
    lands = [lax.empty((N_DEV,) + (s.shape if spread else s.shape[1:]), s.dtype) for s in srcs]

    def body(*refs):
        for out, _ in _split_copies(refs[:n], refs[n:2 * n], refs[2 * n], refs[2 * n + 1], spread):
            out.start()
        refs[-1][...] = jnp.zeros(refs[-1].shape, F32)

    thru = [pltpu.HBM(a.shape, a.dtype) for a in list(srcs) + lands]
    res = pl.pallas_call(
        body, name=name,
        out_shape=(pltpu.SemaphoreType.DMA((N_PEERS * n,)), pltpu.SemaphoreType.DMA((N_PEERS * n,)), *thru,
                   jax.ShapeDtypeStruct((8, LANE), F32)),
        in_specs=[HBM_SPEC] * (2 * n),
        out_specs=(SEM_SPEC, SEM_SPEC, *[HBM_SPEC] * (2 * n), pl.BlockSpec(memory_space=pltpu.VMEM)),
        input_output_aliases={t: 2 + t for t in range(2 * n)},
        compiler_params=pltpu.CompilerParams(has_side_effects=SIDE_EFFECT),
    )(*[pltpu.with_memory_space_constraint(a, pltpu.HBM) for a in list(srcs) + lands])
    return res[0], res[1], list(res[2:2 + n]), list(res[2 + n:2 + 2 * n]), res[-1]


def _send_wait(started, spread, after, name):
    send_sems, recv_sems, srcs, lands, _ = started
    n = len(srcs)

    def body(*refs):
        for out, arrival in _split_copies(refs[:n], refs[n:2 * n], refs[2 * n], refs[2 * n + 1], spread):
            out.wait_send()
            arrival.wait_recv()

    thru = [pltpu.HBM(a.shape, a.dtype) for a in srcs + lands]
    res = pl.pallas_call(
        body, name=name, out_shape=tuple(thru),
        in_specs=[HBM_SPEC] * (2 * n) + [SEM_SPEC, SEM_SPEC, ANY], out_specs=tuple([HBM_SPEC] * (2 * n)),
        input_output_aliases={t: t for t in range(2 * n)},
        compiler_params=pltpu.CompilerParams(has_side_effects=SIDE_EFFECT),
    )(*srcs, *lands, send_sems, recv_sems, after)
    return list(res[n:])


def _adamw(recvs, w, m, v, name):
    n_layers = len(recvs)
    rows, cols = recvs[0].shape[1:]
    tr = _pick(rows, (256, 128, 64, 32, 16))
    nb = rows // tr

    def body(*refs):
        r_refs = refs[:n_layers]
        w_ref, m_ref, v_ref, g_ref, d_ref, m2_ref, v2_ref = refs[n_layers:]

        def update(r_ref):
            g = r_ref[0].astype(F32)
            for k in range(1, N_DEV):
                g = g + r_ref[k].astype(F32)
            m2 = ADAM_B1 * m_ref[...] + (1.0 - ADAM_B1) * g
            v2 = ADAM_B2 * v_ref[...] + (1.0 - ADAM_B2) * (g * g)
            m_hat = m2 / (1.0 - ADAM_B1 ** ADAM_STEP)
            v_hat = v2 / (1.0 - ADAM_B2 ** ADAM_STEP)
            g_ref[...] = g
            d_ref[...] = -ADAM_LR * (m_hat / (jnp.sqrt(v_hat) + ADAM_EPS) + ADAM_WD * w_ref[...])
            m2_ref[...] = m2
            v2_ref[...] = v2

        if n_layers == 1:
            update(r_refs[0])
        else:
            for layer in range(n_layers):
                pl.when(pl.program_id(0) == layer)(functools.partial(update, r_refs[layer]))

    def recv_spec(layer):
        return pl.BlockSpec((N_DEV, tr, cols), lambda a, i: (0, jnp.where(a == layer, i, 0), 0))

    spec = pl.BlockSpec((tr, cols), lambda a, i: (a * nb + i, 0))
    return pl.pallas_call(
        body, grid=(n_layers, nb),
        in_specs=[recv_spec(layer) for layer in range(n_layers)] + [spec, spec, spec],
        out_specs=[spec] * 4, out_shape=[jax.ShapeDtypeStruct(w.shape, F32)] * 4,
        name=name, compiler_params=_params(("arbitrary", "arbitrary")),
    )(*recvs, w, m, v)


def _padded_rows(n, mult):
    rows = -(-n // PACK_COLS)
    return -(-rows // mult) * mult


def _pack(flat_pieces, mult, dtype, lead=()):
    flat = jnp.concatenate([p.astype(dtype) for p in flat_pieces], axis=-1)
    n = flat.shape[-1]
    rows = _padded_rows(n, mult)
    flat = jnp.pad(flat, [(0, 0)] * len(lead) + [(0, rows * PACK_COLS - n)])
    return flat.reshape(lead + (rows, PACK_COLS))


def _unpack(packed, shapes):
    flat = packed.reshape(-1)
    out, off = {}, 0
    for name, shape in shapes:
        size = 1
        for s in shape:
            size *= s
        out[name] = flat[off:off + size].reshape(shape)
        off += size
    return out


def _local_2d(arr):
    return arr.reshape(-1, arr.shape[-1])


SHARD_2D = {n: (s[-2] if len(s) > 2 else s[0], s[-1], a == len(s) - 1) for n, s, a in SHARDED}
SHARD_2D["gdn_conv_w"] = (GDN_CONV, GDN_CONV_DIM // N_DEV, True)


def _wire(name):
    rows, cols, _ = SHARD_2D[name]
    return BF16 if rows * cols >= 1 << 16 else F32


def _layer_tensors(i):
    kind, j = i % 3, i // 3
    pre = ("fox", "gla", "gdn")[kind]
    mix = [(pre + "_w_in", j), (pre + "_w_out", j)]
    if kind == 1:
        mix.append(("gla_w_alpha2", j))
    if kind == 2:
        mix.append(("gdn_conv_w", j))
    if i == 0:
        mix.append(("meta_tokens", None))
    return {"mix": mix, "ffn": [("w_gate_up", i), ("w_down", i)]}


PARTS = ("mix", "ffn")


def _layer_block(arr, idx):
    return arr if idx is None else _local_2d(arr[idx])


def _to_slabs(full2d, name):
    rows, cols, by_cols = SHARD_2D[name]
    if by_cols:
        parts = full2d if isinstance(full2d, tuple) else (full2d,)
        per = parts[0].shape[1] // cols
        return jnp.stack([parts[d // per][:, cols * (d % per):cols * (d % per + 1)] for d in range(N_DEV)])
    return full2d.reshape(N_DEV, rows, cols)


def _from_gathered(g, name):
    rows, cols, by_cols = SHARD_2D[name]
    if by_cols:
        return jnp.concatenate([g[d] for d in range(N_DEV)], axis=1)
    return g.reshape(N_DEV * rows, cols)


def _pad_cols(w, total):
    return jnp.pad(w, [(0, 0)] * (w.ndim - 1) + [(0, total - w.shape[-1])])


def _pad_lanes(v):
    return _pad_cols(v, LANE)


def _x(arr, width, cb0=0, moves=0):
    return (arr, width, cb0, moves)


def _rms_fwd(h, g):
    return _row_fwd(_rms_fn, [_x(h, D_MODEL)], [g], [], [(D_MODEL, D_MODEL, BF16)], "rms_fwd")[0]


FUSED_ROWS = 384


def _matmul_add_norm(a, b, add, g, name):
    m, k = a.shape
    n = b.shape[1]

    def body(a_ref, b_ref, add_ref, g_ref, h_ref, y_ref):
        h2 = _dot(a_ref[...], b_ref[...], "nn") + add_ref[...]
        h_ref[...] = h2
        y_ref[...] = _rms_fn(0, h2, g_ref[...])[0].astype(y_ref.dtype)

    rows = pl.BlockSpec((FUSED_ROWS, n), lambda i: (i, 0))
    return pl.pallas_call(
        body, grid=(m // FUSED_ROWS,),
        in_specs=[pl.BlockSpec((FUSED_ROWS, k), lambda i: (i, 0)), pl.BlockSpec((k, n), lambda i: (0, 0)), rows,
                  pl.BlockSpec((1, n), lambda i: (0, 0))],
        out_specs=[rows, rows],
        out_shape=[jax.ShapeDtypeStruct((m, n), F32), jax.ShapeDtypeStruct((m, n), BF16)],
        name=name, compiler_params=_params(("parallel",)),
    )(a, b, add, g)


def _matmul_norm_bwd(a, b, b_kblock, h, g, dres, name, prev=None):
    m, k = a.shape
    n = b.shape[0]

    def body(*refs):
        a_ref, b_ref, h_ref, g_ref, dres_ref = refs[:5]
        dh_ref, dg_ref = refs[-2:]
        dy = _dot(a_ref[...], b_ref[...], "nt")
        if prev is not None:
            dy = dy + refs[5][...]
        _, vjp = jax.vjp(lambda hh, gg: _rms_fn(0, hh, gg), h_ref[...], g_ref[...])
        dh, dg = vjp((dy,))
        dh_ref[...] = dres_ref[...] + dh

        @pl.when(pl.program_id(0) == 0)
        def _():
            dg_ref[...] = jnp.zeros(dg_ref.shape, F32)

        dg_ref[...] += dg

    rows = pl.BlockSpec((FUSED_ROWS, n), lambda i: (i, 0))
    gain = pl.BlockSpec((1, n), lambda i: (0, 0))
    operands = [a, b, h, g, dres] + ([] if prev is None else [prev])
    in_specs = [pl.BlockSpec((FUSED_ROWS, k), lambda i: (i, 0)), pl.BlockSpec((n, k), lambda i: (0, b_kblock)),
                rows, gain, rows] + ([] if prev is None else [rows])
    return pl.pallas_call(
        body, grid=(m // FUSED_ROWS,), in_specs=in_specs, out_specs=[rows, gain],
        out_shape=[jax.ShapeDtypeStruct((m, n), F32), jax.ShapeDtypeStruct((1, n), F32)],
        name=name, compiler_params=_params(("arbitrary",)),
    )(*operands)


SW_WIDTH = 256
SW_NCOL = D_FF // SW_WIDTH


def _ffn_gate_up(y, wgu):
    m, k = y.shape
    tm = _pick(m, (1408, 384, 128))

    def body(y_ref, wg_ref, wu_ref, a_ref, g_ref, u_ref):
        yv = y_ref[...]
        gt = _dot(yv, wg_ref[...], "nn")
        up = _dot(yv, wu_ref[...], "nn")
        a_ref[...] = (_silu(gt) * up).astype(a_ref.dtype)
        g_ref[...] = gt.astype(g_ref.dtype)
        u_ref[...] = up.astype(u_ref.dtype)

    tile = pl.BlockSpec((tm, SW_WIDTH), lambda i, j: (i, j))
    return pl.pallas_call(
        body, grid=(m // tm, SW_NCOL),
        in_specs=[pl.BlockSpec((tm, k), lambda i, j: (i, 0)),
                  pl.BlockSpec((k, SW_WIDTH), lambda i, j: (0, j)),
                  pl.BlockSpec((k, SW_WIDTH), lambda i, j: (0, SW_NCOL + j))],
        out_specs=[tile] * 3, out_shape=[jax.ShapeDtypeStruct((m, D_FF), BF16)] * 3,
        name="ffn_gate_up", compiler_params=_params(("parallel", "parallel")),
    )(y, wgu, wgu)


def _ffn_down_dx(dh2, wd, gt, up):
    m, k = dh2.shape
    tm = _pick(m, (1408, 384, 128))

    def body(d_ref, w_ref, g_ref, u_ref, dg_ref, du_ref):
        da = _dot(d_ref[...], w_ref[...], "nt")
        gv, uv = g_ref[...].astype(F32), u_ref[...].astype(F32)
        s = jax.nn.sigmoid(gv)
        dg_ref[...] = (da * uv * (s + gv * s * (1.0 - s))).astype(dg_ref.dtype)
        du_ref[...] = (da * gv * s).astype(du_ref.dtype)

    tile = pl.BlockSpec((tm, SW_WIDTH), lambda i, j: (i, j))
    return pl.pallas_call(
        body, grid=(m // tm, SW_NCOL),
        in_specs=[pl.BlockSpec((tm, k), lambda i, j: (i, 0)),
                  pl.BlockSpec((SW_WIDTH, k), lambda i, j: (j, 0)), tile, tile],
        out_specs=[tile] * 2, out_shape=[jax.ShapeDtypeStruct((m, D_FF), BF16)] * 2,
        name="ffn_down_dx", compiler_params=_params(("parallel", "parallel")),
    )(dh2, wd, gt, up)


def _ffn_fwd(h, y, wgu, wd, g_next):
    a, gt, up = _ffn_gate_up(y, wgu)
    if g_next is None:
        h2, y_next = _matmul(a, wd, "nn", F32, "ffn_down", add=h), None
    else:
        h2, y_next = _matmul_add_norm(a, wd, h, g_next, "ffn_down_norm")
    return h2, y_next, (h, y, gt, up, a)


def _ffn_bwd(dh2, saved, g, wgu, wd):
    h, y, gt, up, a = saved
    dgt, dup = _ffn_down_dx(dh2, wd, gt, up)
    dwd = _matmul(a, dh2, "tn", F32, "ffn_down_dw")
    dwg = _matmul(y, dgt, "tn", F32, "ffn_gate_dw")
    dwu = _matmul(y, dup, "tn", F32, "ffn_up_dw")
    dy = _matmul(dgt, wgu, "nt", F32, "ffn_gate_dx", b_kblock=0)
    dh, dg = _matmul_norm_bwd(dup, wgu, 1, h, g, dh2, "ffn_up_dx_norm", prev=dy)
    return dh, dg, (dwg, dwu), dwd


def _fox_consts():
    lane = jnp.arange(D_MODEL) // FOX_DH
    e_expand = (jnp.arange(LANE)[:, None] == lane[None, :]).astype(F32)
    e_mean = e_expand.T * (1.0 / FOX_DH)
    half = jnp.arange(LANE) // FOX_DH
    blockdiag = (half[:, None] == half[None, :]).astype(F32)
    return e_mean, e_expand, blockdiag


def _pair_layout(c):
    heads = c[:, :FOX_HEADS].T
    crow = heads.reshape(FOX_PAIRS, 2, ROWS)
    c2 = jnp.broadcast_to(crow.transpose(0, 2, 1)[:, :, :, None], (FOX_PAIRS, ROWS, 2, FOX_DH))
    crow = jnp.where(jnp.arange(ROWS) >= FIRST_ROW, crow, KEY_OFF)
    return c2.reshape(FOX_PAIRS, ROWS, LANE), crow


def _fox_fwd(y, p):
    proj = _matmul(y, p["w_in"], "nn", F32, "fox_in")
    e_mean, e_expand, _ = _fox_consts()
    nb = D_MODEL // LANE
    xs = [_x(proj, D_MODEL, 0), _x(proj, D_MODEL, 1), _x(proj, LANE, 4 * nb)]
    ps = [p["q_gain"], p["k_gain"], p["b_f"]]
    qn, kn, lf = _row_fwd(_fox_pre_fn, xs, ps, [e_mean, e_expand],
                          [(D_MODEL, D_MODEL, BF16), (D_MODEL, D_MODEL, BF16), (LANE, LANE, F32)], "fox_pre_fwd")
    c = _seq_cumsum(lf, False, "fox_cumsum")
    c2, crow = _pair_layout(c)
    o, lse2 = _fox_attn_fwd(qn, kn, proj, c2, crow)
    om = _row_fwd(_fox_post_fn, [_x(o, D_MODEL, 0), _x(proj, D_MODEL, 3)], [], [],
                  [(D_MODEL, D_MODEL, BF16)], "fox_post_fwd")[0]
    return om, (proj, qn, kn, c2, crow, o, lse2)


def _fox_bwd(dom, y, saved, p):
    proj, qn, kn, c2, crow, o, lse2 = saved
    e_mean, e_expand, blockdiag = _fox_consts()
    nb = D_MODEL // LANE
    do, dgate = _row_bwd(_fox_post_fn, [_x(o, D_MODEL, 0), _x(proj, D_MODEL, 3)], [], [], [_x(dom, D_MODEL)],
                         [(D_MODEL, F32), (D_MODEL, BF16)], "fox_post_bwd")
    dq, dk, dv, dcrow, dcq2 = _fox_attn_bwd(qn, kn, proj, c2, crow, lse2, o, do, blockdiag)
    dlf = _seq_cumsum(_pad_cols(dcrow.reshape(FOX_HEADS, ROWS).T, LANE), True, "fox_cumsum_rev", pairs=dcq2)
    xs = [_x(proj, D_MODEL, 0), _x(proj, D_MODEL, 1), _x(proj, LANE, 4 * nb)]
    ps = [p["q_gain"], p["k_gain"], p["b_f"]]
    dqr, dkr, df, dqg, dkg, dbf = _row_bwd(
        _fox_pre_fn, xs, ps, [e_mean, e_expand], [_x(dq, D_MODEL), _x(dk, D_MODEL), _x(dlf, LANE)],
        [(D_MODEL, BF16), (D_MODEL, BF16), (LANE, BF16)], "fox_pre_bwd")
    dproj = jnp.concatenate([dqr, dkr, dv, dgate, df], axis=1)
    grads = {
        "w_in": _matmul(y, dproj, "tn", F32, "fox_in_dw")[:, :FOX_IN],
        "q_gain": dqg.reshape(FOX_HEADS, FOX_DH).sum(axis=0),
        "k_gain": dkg.reshape(FOX_HEADS, FOX_DH).sum(axis=0),
        "b_f": dbf[0, :FOX_HEADS],
    }
    return dproj, grads


def _gla_fwd(y, p):
    proj = _matmul(y, p["w_in"], "nn", F32, "gla_in")
    small = (2 * GLA_QK + 2 * GLA_V) // LANE
    garr = _row_fwd(_gla_pre_fn, [_x(proj, LANE, small)], [p["w_alpha"], p["b_alpha"]], [],
                    [(GLA_QK, GLA_QK, F32)], "gla_pre_fwd")[0]
    o, states = _gla_scan_fwd(proj, garr)
    om = _row_fwd(_gla_post_fn, [_x(o, GLA_V, 0), _x(proj, GLA_V, 2)], [p["o_gain"]], [],
                  [(GLA_V, GLA_V, BF16)], "gla_post_fwd")[0]
    return om, (proj, garr, states, o)


def _gla_bwd(dom, y, saved, p):
    proj, garr, states, o = saved
    small = (2 * GLA_QK + 2 * GLA_V) // LANE
    do, dr, dgain = _row_bwd(_gla_post_fn, [_x(o, GLA_V, 0), _x(proj, GLA_V, 2)], [p["o_gain"]], [],
                             [_x(dom, GLA_V)], [(GLA_V, F32), (GLA_V, BF16)], "gla_post_bwd")
    dq, dk, dv, dg = _gla_scan_bwd(proj, garr, states, do)
    dalr, dwa, dba = _row_bwd(_gla_pre_fn, [_x(proj, LANE, small)], [p["w_alpha"], p["b_alpha"]], [],
                              [_x(dg, GLA_QK)], [(LANE, BF16)], "gla_pre_bwd")
    dproj = jnp.concatenate([dq, dk, dv, dr, dalr], axis=1)
    grads = {
        "w_in": _matmul(y, dproj, "tn", F32, "gla_in_dw")[:, :GLA_IN],
        "w_alpha": dwa[:GLA_RANK], "b_alpha": dba[0], "o_gain": dgain[0],
    }
    return dproj, grads


def _gdn_fwd(y, p):
    proj = _matmul(y, p["w_in"], "nn", F32, "gdn_in")
    nb = D_MODEL // LANE
    qn = _gdn_conv_fwd(proj, p["conv_q"], "q", 0)
    kn = _gdn_conv_fwd(proj, p["conv_k"], "k", 1)
    vn = _gdn_conv_fwd(proj, p["conv_v"], "v", 2)
    gates = _row_fwd(_gdn_gates_fn, [_x(proj, LANE, 4 * nb)], [p["a_log"], p["dt_bias"]], [],
                     [(LANE, LANE, F32)], "gdn_gates_fwd")[0]
    o, states = _gdn_scan_fwd(qn, kn, vn, gates)
    om = _row_fwd(_gdn_post_fn, [_x(o, D_MODEL, 0), _x(proj, D_MODEL, 3)], [p["o_gain"]], [],
                  [(D_MODEL, D_MODEL, BF16)], "gdn_post_fwd")[0]
    return om, (proj, qn, kn, vn, gates, states, o)


def _gdn_bwd(dom, y, saved, p):
    proj, qn, kn, vn, gates, states, o = saved
    nb = D_MODEL // LANE
    do, dgate, dgain = _row_bwd(_gdn_post_fn, [_x(o, D_MODEL, 0), _x(proj, D_MODEL, 3)], [p["o_gain"]], [],
                                [_x(dom, D_MODEL)], [(D_MODEL, F32), (D_MODEL, BF16)], "gdn_post_bwd")
    dqn, dkn, dvn, dgates = _gdn_scan_bwd(qn, kn, vn, gates, states, do)
    dsm, dalog, ddtb = _row_bwd(_gdn_gates_fn, [_x(proj, LANE, 4 * nb)], [p["a_log"], p["dt_bias"]], [],
                                [_x(dgates, LANE)], [(LANE, BF16)], "gdn_gates_bwd")
    dxq, dwq = _gdn_conv_bwd(proj, p["conv_q"], "q", 0, dqn)
    dxk, dwk = _gdn_conv_bwd(proj, p["conv_k"], "k", 1, dkn)
    dxv, dwv = _gdn_conv_bwd(proj, p["conv_v"], "v", 2, dvn)
    dproj = jnp.concatenate([dxq, dxk, dxv, dgate, dsm], axis=1)
    grads = {
        "w_in": _matmul(y, dproj, "tn", F32, "gdn_in_dw")[:, :GDN_IN],
        "conv_w": jnp.concatenate([dwq[:GDN_CONV], dwk[:GDN_CONV], dwv[:GDN_CONV]], axis=1),
        "a_log": dalog[0, :GDN_HEADS], "dt_bias": ddtb[0, :GDN_HEADS], "o_gain": dgain[0],
    }
    return dproj, grads


_MIXERS = ((_fox_fwd, _fox_bwd), (_gla_fwd, _gla_bwd), (_gdn_fwd, _gdn_bwd))


def _mixer_params(i, wts, rep):
    kind, j = i % 3, i // 3
    if kind == 0:
        return {
            "w_in": _pad_cols(wts["fox_w_in"], FOX_IN_PAD), "w_out": wts["fox_w_out"],
            "q_gain": jnp.tile(rep["fox_q_gain"][j], FOX_HEADS)[None],
            "k_gain": jnp.tile(rep["fox_k_gain"][j], FOX_HEADS)[None],
            "b_f": _pad_lanes(rep["fox_b_f"][j][None]),
        }
    if kind == 1:
        return {
            "w_in": _pad_cols(wts["gla_w_in"], GLA_IN_PAD), "w_out": wts["gla_w_out"],
            "w_alpha": jnp.pad(wts["gla_w_alpha2"], ((0, LANE - GLA_RANK), (0, 0))).astype(F32),
            "b_alpha": rep["gla_b_alpha"][j][None], "o_gain": rep["gla_o_gain"][j][None],
        }
    conv = jnp.pad(wts["gdn_conv_w"].astype(F32), ((0, 8 - GDN_CONV), (0, 0)))
    return {
        "w_in": _pad_cols(wts["gdn_w_in"], GDN_IN_PAD), "w_out": wts["gdn_w_out"],
        "conv_q": conv[:, :D_MODEL], "conv_k": conv[:, D_MODEL:2 * D_MODEL], "conv_v": conv[:, 2 * D_MODEL:],
        "a_log": _pad_lanes(rep["gdn_a_log"][j][None]), "dt_bias": _pad_lanes(rep["gdn_dt_bias"][j][None]),
        "o_gain": rep["gdn_o_gain"][j][None],
    }


def kernel(x, meta_tokens, norm_mix, norm_ffn, w_gate_up, w_down, fox_w_in, fox_b_f, fox_q_gain, fox_k_gain, fox_w_out, gla_w_in, gla_w_alpha2, gla_b_alpha, gla_o_gain, gla_w_out, gdn_w_in, gdn_conv_w, gdn_a_log, gdn_dt_bias, gdn_o_gain, gdn_w_out, loss_target, m_meta_tokens, m_norm_mix, m_norm_ffn, m_w_gate_up, m_w_down, m_fox_w_in, m_fox_b_f, m_fox_q_gain, m_fox_k_gain, m_fox_w_out, m_gla_w_in, m_gla_w_alpha2, m_gla_b_alpha, m_gla_o_gain, m_gla_w_out, m_gdn_w_in, m_gdn_conv_w, m_gdn_a_log, m_gdn_dt_bias, m_gdn_o_gain, m_gdn_w_out, v_meta_tokens, v_norm_mix, v_norm_ffn, v_w_gate_up, v_w_down, v_fox_w_in, v_fox_b_f, v_fox_q_gain, v_fox_k_gain, v_fox_w_out, v_gla_w_in, v_gla_w_alpha2, v_gla_b_alpha, v_gla_o_gain, v_gla_w_out, v_gdn_w_in, v_gdn_conv_w, v_gdn_a_log, v_gdn_dt_bias, v_gdn_o_gain, v_gdn_w_out):
    given = dict(locals())
    w_loc = {n: given[n] for n in WEIGHT_ORDER}
    m_loc = {n: given["m_" + n] for n in WEIGHT_ORDER}
    v_loc = {n: given["v_" + n] for n in WEIGHT_ORDER}

    rep = {n: w_loc[n] for n, _ in REPLICATED}
    me = 4 * lax.axis_index("x") + 2 * lax.axis_index("y") + lax.axis_index("c")

    def with_own(lands, owns):
        return [lax.dynamic_update_slice(land, own[None], (me, 0, 0)) for land, own in zip(lands, owns)]

    gathers = {}
    order = jnp.zeros((), F32)
    for i in range(DEPTH):
        groups = _layer_tensors(i)
        for part in PARTS:
            blocks = [(_layer_block(w_loc[n], idx) + order).astype(_wire(n)) for n, idx in groups[part]]
            started = _send_start(blocks, True, "gather_start_%d_%s" % (i, part))
            gathers[(i, part)] = (groups[part], blocks, started)
            order = started[4][0, 0]
    all_started = gathers[(DEPTH - 1, PARTS[-1])][2][4]

    def weights(i, part, after):
        tensors, blocks, started = gathers[(i, part)]
        lands = with_own(_send_wait(started, True, after, "gather_wait_%d_%s" % (i, part)), blocks)
        return {n: _from_gathered(g, n) for (n, _), g in zip(tensors, lands)}

    h = y = None
    saved = []
    for i in range(DEPTH):
        w_mix = weights(i, "mix", all_started if i == 0 else h)
        if i == 0:
            h = jnp.concatenate([jnp.zeros((FIRST_ROW, D_MODEL), F32), w_mix["meta_tokens"], x[0]], axis=0)
            y = _rms_fwd(h, rep["norm_mix"][0][None])
        p = _mixer_params(i, w_mix, rep)
        fwd, _ = _MIXERS[i % 3]
        om, mix_saved = fwd(y, p)
        h_mid, y_mid = _matmul_add_norm(om, p["w_out"], h, rep["norm_ffn"][i][None], "mixer_out_norm")
        w_ffn = weights(i, "ffn", h_mid)
        g_next = rep["norm_mix"][i + 1][None] if i + 1 < DEPTH else None
        h_out, y_next, ffn_saved = _ffn_fwd(h_mid, y_mid, w_ffn["w_gate_up"], w_ffn["w_down"], g_next)
        saved.append((p, w_ffn, h, y, om, mix_saved, ffn_saved))
        h, y = h_out, y_next

    dh, sq = _loss_head(h, loss_target[0])
    loss = lax.psum(0.5 * jnp.sum(sq) * (1.0 / D_MODEL), ("x", "y", "c"))

    gw = {n: [None] * DEPTH for n in ("norm_mix", "norm_ffn")}
    gm = {}
    reduces = {}

    def start_reduce(i, part, full, extra=()):
        tensors = _layer_tensors(i)[part]
        slabs = [_to_slabs(full[n], n).astype(_wire(n)) for n, _ in tensors] + list(extra)
        started = _send_start(slabs, False, "reduce_start_%d_%s" % (i, part))
        reduces[(i, part)] = (tensors, slabs, started)
        return started[4][0, 0]

    order = jnp.zeros((), BF16)
    for i in reversed(range(DEPTH)):
        p, w_ffn, h_in, y, om, mix_saved, ffn_saved = saved[i]
        kind, j = i % 3, i // 3
        pre = ("fox", "gla", "gdn")[kind]
        _, bwd = _MIXERS[kind]
        dh_mid, gw["norm_ffn"][i], dwgu, dwd = _ffn_bwd(
            dh, ffn_saved, rep["norm_ffn"][i][None], w_ffn["w_gate_up"], w_ffn["w_down"] + order)
        order = start_reduce(i, "ffn", {"w_gate_up": dwgu, "w_down": dwd}).astype(BF16)
        dom = _matmul(dh_mid, p["w_out"] + order, "nt", F32, "mixer_out_dx")
        dwo = _matmul(om, dh_mid, "tn", F32, "mixer_out_dw")
        dproj, grads = bwd(dom, y, mix_saved, p)
        gm[(kind, j)] = grads
        dh, gw["norm_mix"][i] = _matmul_norm_bwd(dproj, p["w_in"], 0, h_in, rep["norm_mix"][i][None], dh_mid,
                                                 "mixer_in_dx_norm")
        mix_full = {pre + "_w_in": grads["w_in"], pre + "_w_out": dwo, "gla_w_alpha2": grads.get("w_alpha"),
                    "gdn_conv_w": grads.get("conv_w"), "meta_tokens": dh[FIRST_ROW:PAD_ROWS]}
        if i > 0:
            order = start_reduce(i, "mix", mix_full).astype(BF16)

    partial = {
        "norm_mix": jnp.concatenate(gw["norm_mix"], axis=0), "norm_ffn": jnp.concatenate(gw["norm_ffn"], axis=0),
        "fox_b_f": jnp.stack([gm[(0, 0)]["b_f"], gm[(0, 1)]["b_f"]]),
        "fox_q_gain": jnp.stack([gm[(0, 0)]["q_gain"], gm[(0, 1)]["q_gain"]]),
        "fox_k_gain": jnp.stack([gm[(0, 0)]["k_gain"], gm[(0, 1)]["k_gain"]]),
        "gla_b_alpha": gm[(1, 0)]["b_alpha"][None], "gla_o_gain": gm[(1, 0)]["o_gain"][None],
        "gdn_a_log": gm[(2, 0)]["a_log"][None], "gdn_dt_bias": gm[(2, 0)]["dt_bias"][None],
        "gdn_o_gain": gm[(2, 0)]["o_gain"][None],
    }

    small = _pack([partial[n].reshape(-1) for n, _ in REPLICATED], 8, F32)
    start_reduce(0, "mix", mix_full, extra=[jnp.broadcast_to(small[None], (N_DEV,) + small.shape)])
    recv = {}
    for (i, part), (tensors, slabs, started) in reduces.items():
        owns = [lax.dynamic_index_in_dim(s, me, 0, keepdims=False) for s in slabs]
        lands = with_own(_send_wait(started, False, dh, "reduce_wait_%d_%s" % (i, part)), owns)
        for (n, idx), land in zip(tensors, lands):
            recv[(n, idx)] = land
        if (i, part) == (0, "mix"):
            recv_small = lands[-1]

    res = [{} for _ in range(4)]
    for n, s, _ in SHARDED:
        layers = [idx for i in range(DEPTH) for part in PARTS for (nn, idx) in _layer_tensors(i)[part] if nn == n]
        layers = sorted(layers, key=lambda idx: -1 if idx is None else idx)
        outs = _adamw([recv[(n, idx)] for idx in layers], *[_local_2d(d[n]) for d in (w_loc, m_loc, v_loc)],
                      "adamw_" + n)
        for k in range(4):
            res[k][n] = outs[k].reshape(s)
    outs_small = _adamw([recv_small], *[_pack([d[n].reshape(-1) for n, _ in REPLICATED], 8, F32)
                                        for d in (w_loc, m_loc, v_loc)], "adamw_replicated")
    for k in range(4):
        res[k].update(_unpack(outs_small[k], list(REPLICATED)))

    result = [loss, dh[PAD_ROWS:][None]]
    for k in range(4):
        result += [res[k][n] for n in WEIGHT_ORDER]
    return tuple(result)
```

```python
import functools

import jax
import jax.numpy as jnp
from jax import lax
from jax.experimental import pallas as pl
from jax.experimental.pallas import tpu as pltpu

F32, BF16 = jnp.float32, jnp.bfloat16
HIGHEST = lax.Precision.HIGHEST

D_MODEL = 1024
SEQ = 4096
N_META = 16
DEPTH = 4
NORM_EPS = 1e-6
LANE = 128
PAD_ROWS = 128
FIRST_ROW = PAD_ROWS - N_META
ROWS = PAD_ROWS + SEQ
ROW_TILE = 128
CHUNK = 64
NEG = -1e30
N_DEV = 8
VMEM_LIMIT = 56 * 1024 * 1024
VMEM_BLOCK_BUDGET = 36 * 1024 * 1024

FOX_HEADS, FOX_DH = 16, 64
FOX_IN, FOX_IN_PAD = 4 * D_MODEL + FOX_HEADS, 4 * D_MODEL + LANE
GLA_HEADS, GLA_DK, GLA_DV, GLA_RANK = 4, 128, 256, 16
GLA_QK, GLA_V = GLA_HEADS * GLA_DK, GLA_HEADS * GLA_DV
GLA_IN, GLA_IN_PAD = 2 * GLA_QK + 2 * GLA_V + GLA_RANK, 2 * GLA_QK + 2 * GLA_V + LANE
GDN_HEADS, GDN_DK, GDN_DV, GDN_CONV = 8, 128, 128, 4
GDN_CONV_DIM = 3 * GDN_HEADS * GDN_DK
GDN_IN, GDN_IN_PAD = GDN_CONV_DIM + GDN_HEADS * GDN_DV + 2 * GDN_HEADS, GDN_CONV_DIM + GDN_HEADS * GDN_DV + LANE
D_FF = 2816

ADAM_LR, ADAM_B1, ADAM_B2, ADAM_EPS, ADAM_WD, ADAM_STEP = 0.001, 0.9, 0.999, 1e-08, 0.01, 10

PACK_COLS = 1024

SHARDED = (
    ("meta_tokens", (16, 128), 1),
    ("w_gate_up", (4, 1024, 704), 2),
    ("w_down", (4, 352, 1024), 1),
    ("fox_w_in", (2, 1024, 514), 2),
    ("fox_w_out", (2, 128, 1024), 1),
    ("gla_w_in", (1, 1024, 386), 2),
    ("gla_w_alpha2", (1, 16, 64), 2),
    ("gla_w_out", (1, 128, 1024), 1),
    ("gdn_w_in", (1, 1024, 514), 2),
    ("gdn_conv_w", (1, 4, 1, 384), 3),
    ("gdn_w_out", (1, 128, 1024), 1),
)
REPLICATED = (
    ("norm_mix", (4, 1024)),
    ("norm_ffn", (4, 1024)),
    ("fox_b_f", (2, 16)),
    ("fox_q_gain", (2, 64)),
    ("fox_k_gain", (2, 64)),
    ("gla_b_alpha", (1, 512)),
    ("gla_o_gain", (1, 256)),
    ("gdn_a_log", (1, 8)),
    ("gdn_dt_bias", (1, 8)),
    ("gdn_o_gain", (1, 128)),
)
WEIGHT_ORDER = (
    "meta_tokens", "norm_mix", "norm_ffn", "w_gate_up", "w_down", "fox_w_in", "fox_b_f", "fox_q_gain",
    "fox_k_gain", "fox_w_out", "gla_w_in", "gla_w_alpha2", "gla_b_alpha", "gla_o_gain", "gla_w_out",
    "gdn_w_in", "gdn_conv_w", "gdn_a_log", "gdn_dt_bias", "gdn_o_gain", "gdn_w_out",
)

_DN = {
    "nn": (((1,), (0,)), ((), ())),
    "nt": (((1,), (1,)), ((), ())),
    "tn": (((0,), (0,)), ((), ())),
}


def _dot(a, b, kind):
    return lax.dot_general(a.astype(BF16), b.astype(BF16), _DN[kind], preferred_element_type=F32)


@functools.partial(jax.custom_vjp, nondiff_argnums=(2,))
def _dot_ad(a, b, kind):
    return _dot(a, b, kind)


def _dot_ad_fwd(a, b, kind):
    return _dot(a, b, kind), (a, b)


def _dot_ad_bwd(kind, res, ct):
    a, b = res
    if kind == "nn":
        return _dot(ct, b, "nt"), _dot(a, ct, "tn")
    if kind == "nt":
        return _dot(ct, b, "nn"), _dot(ct, a, "tn")
    return _dot(b, ct, "nt"), _dot(a, ct, "nn")


_dot_ad.defvjp(_dot_ad_fwd, _dot_ad_bwd)


def _dot_hi(a, b, kind):
    return lax.dot_general(a, b, _DN[kind], precision=HIGHEST, preferred_element_type=F32)


def _dot_mid(a, b, kind):
    return lax.dot_general(a, b, _DN[kind], precision=lax.Precision.HIGH, preferred_element_type=F32)


def _iota(shape, dim):
    return lax.broadcasted_iota(jnp.int32, shape, dim)


def _row_mask(i, tm):
    return ((i * tm + _iota((tm, 1), 0)) >= FIRST_ROW).astype(F32)


def _log_sigmoid(z):
    return jnp.minimum(z, 0.0) - jnp.log(1.0 + jnp.exp(-jnp.abs(z)))


def _softplus(z):
    return jnp.maximum(z, 0.0) + jnp.log(1.0 + jnp.exp(-jnp.abs(z)))


def _silu(z):
    return z * jax.nn.sigmoid(z)


def _lane_col(x, lane):
    return jnp.sum(jnp.where(_iota(x.shape, 1) == lane, x, 0.0), axis=1, keepdims=True)


def _last_row(x):
    return jnp.sum(jnp.where(_iota(x.shape, 0) == x.shape[0] - 1, x, 0.0), axis=0, keepdims=True)


def _params(sem, limit=VMEM_LIMIT):
    return pltpu.CompilerParams(dimension_semantics=sem, vmem_limit_bytes=limit)


def _pick(n, options):
    for t in options:
        if n % t == 0:
            return t
    return n


def _matmul(a, b, kind, out_dtype, name, add=None, b_kblock=0):
    if kind == "nn":
        (m, k), n = a.shape, b.shape[1]
    elif kind == "nt":
        (m, k), n = a.shape, b.shape[0]
    else:
        (k, m), n = a.shape, b.shape[1]
    tn = _pick(n, (512, 384, 640, 256, 128))
    tm_options = (1408, 384, 128) if kind != "tn" else (1024, 512, 256, 128)
    tm = None
    for t in tm_options:
        if m % t:
            continue
        need = 2 * (t * k * a.dtype.itemsize + tn * k * b.dtype.itemsize + t * tn * 8)
        if need <= VMEM_BLOCK_BUDGET:
            tm = t
            break
    assert tm is not None, (name, a.shape, b.shape)
    a_bytes, b_bytes = m * k * a.dtype.itemsize, n * k * b.dtype.itemsize
    m_outer = a_bytes + (m // tm) * b_bytes <= b_bytes + (n // tn) * a_bytes

    def ij(fn):
        return (lambda i, j: fn(i, j)) if m_outer else (lambda j, i: fn(i, j))

    def body(*refs):
        a_ref, b_ref = refs[0], refs[1]
        o_ref = refs[-1]
        acc = _dot(a_ref[...], b_ref[...], kind)
        if add is not None:
            acc = acc + refs[2][...]
        o_ref[...] = acc.astype(o_ref.dtype)

    if kind == "tn":
        a_spec = pl.BlockSpec((k, tm), ij(lambda i, j: (0, i)))
    else:
        a_spec = pl.BlockSpec((tm, k), ij(lambda i, j: (i, 0)))
    if kind == "nt":
        b_spec = pl.BlockSpec((tn, k), ij(lambda i, j: (j, b_kblock)))
    else:
        b_spec = pl.BlockSpec((k, tn), ij(lambda i, j: (0, j)))
    o_spec = pl.BlockSpec((tm, tn), ij(lambda i, j: (i, j)))
    operands, in_specs = [a, b], [a_spec, b_spec]
    if add is not None:
        operands.append(add)
        in_specs.append(o_spec)
    grid = (m // tm, n // tn) if m_outer else (n // tn, m // tm)
    return pl.pallas_call(
        body, grid=grid, in_specs=in_specs, out_specs=o_spec,
        out_shape=jax.ShapeDtypeStruct((m, n), out_dtype), name=name,
        compiler_params=_params(("parallel", "parallel")),
    )(*operands)


def _x_spec(tm, width, cb0, moves):
    return pl.BlockSpec((tm, width), lambda i, j: (i, cb0 + j * moves))


def _full_spec(arr):
    nd = arr.ndim
    return pl.BlockSpec(arr.shape, lambda i, j: (0,) * nd)


def _row_fwd(fn, xs, ps, cs, outs, name, tm=ROW_TILE, ncol=1, rows=None):
    rows = ROWS if rows is None else rows
    nx, npar, nc = len(xs), len(ps), len(cs)

    def body(*refs):
        i = pl.program_id(0)
        xv = [r[...].astype(F32) for r in refs[:nx]]
        pv = [r[...] for r in refs[nx:nx + npar + nc]]
        res = fn(i, *xv, *pv)
        for r, val in zip(refs[nx + npar + nc:], res):
            r[...] = val.astype(r.dtype)

    in_specs = [_x_spec(tm, w, cb0, mv) for (_, w, cb0, mv) in xs]
    in_specs += [_full_spec(p) for p in list(ps) + list(cs)]
    out_specs = [_x_spec(tm, w, 0, 1) for (w, _, _) in outs]
    out_shape = [jax.ShapeDtypeStruct((rows, tot), dt) for (_, tot, dt) in outs]
    return pl.pallas_call(
        body, grid=(rows // tm, ncol), in_specs=in_specs, out_specs=out_specs, out_shape=out_shape,
        name=name, compiler_params=_params(("parallel", "parallel")),
    )(*[x[0] for x in xs], *ps, *cs)


def _row_bwd(fn, xs, ps, cs, cts, dxs, name, adds=None, tm=ROW_TILE, ncol=1, rows=None):
    rows = ROWS if rows is None else rows
    nx, npar, nc, nct = len(xs), len(ps), len(cs), len(cts)
    adds = [None] * nx if adds is None else adds
    add_idx = [k for k in range(nx) if adds[k] is not None]

    def body(*refs):
        i, j = pl.program_id(0), pl.program_id(1)
        pos = 0
        x_refs = refs[pos:pos + nx]; pos += nx
        p_refs = refs[pos:pos + npar]; pos += npar
        c_refs = refs[pos:pos + nc]; pos += nc
        ct_refs = refs[pos:pos + nct]; pos += nct
        add_refs = refs[pos:pos + len(add_idx)]; pos += len(add_idx)
        dx_refs = refs[pos:pos + nx]; pos += nx
        dp_refs = refs[pos:pos + npar]
        xv = [r[...].astype(F32) for r in x_refs]
        pv = [r[...] for r in p_refs]
        cv = [r[...] for r in c_refs]
        _, vjp = jax.vjp(lambda *args: fn(i, *args, *cv), *xv, *pv)
        grads = vjp(tuple(r[...].astype(F32) for r in ct_refs))
        for k in range(nx):
            val = grads[k]
            if adds[k] is not None:
                val = val + add_refs[add_idx.index(k)][...]
            dx_refs[k][...] = val.astype(dx_refs[k].dtype)
        if npar:
            @pl.when((i == 0) & (j == 0))
            def _():
                for r in dp_refs:
                    r[...] = jnp.zeros(r.shape, r.dtype)

            for k in range(npar):
                dp_refs[k][...] += grads[nx + k]

    in_specs = [_x_spec(tm, w, cb0, mv) for (_, w, cb0, mv) in xs]
    in_specs += [_full_spec(p) for p in list(ps) + list(cs)]
    in_specs += [_x_spec(tm, w, cb0, mv) for (_, w, cb0, mv) in cts]
    in_specs += [_x_spec(tm, xs[k][1], 0, 1) for k in add_idx]
    out_specs = [_x_spec(tm, xs[k][1], 0, 1) for k in range(nx)] + [_full_spec(p) for p in ps]
    out_shape = [jax.ShapeDtypeStruct((rows, tot), dt) for (tot, dt) in dxs]
    out_shape += [jax.ShapeDtypeStruct(p.shape, F32) for p in ps]
    sem = ("arbitrary", "arbitrary") if npar else ("parallel", "parallel")
    return pl.pallas_call(
        body, grid=(rows // tm, ncol), in_specs=in_specs, out_specs=out_specs, out_shape=out_shape,
        name=name, compiler_params=_params(sem),
    )(*[x[0] for x in xs], *ps, *cs, *[c[0] for c in cts], *[adds[k] for k in add_idx])


def _rms_fn(i, h, g):
    return (h * lax.rsqrt(jnp.mean(h * h, axis=1, keepdims=True) + NORM_EPS) * g,)


def _fox_pre_fn(i, q, k, f, qg, kg, bf, e_mean, e_expand):
    tm = q.shape[0]

    def headnorm(x, gain, scale):
        msq = _dot_mid(x * x, e_mean, "nn")
        r = _dot_mid(lax.rsqrt(msq + NORM_EPS), e_expand, "nn")
        return x * r * gain * scale

    qn = headnorm(q, qg, FOX_DH ** -0.5)
    kn = headnorm(k, kg, 1.0)
    keep = (_iota((tm, LANE), 1) < FOX_HEADS).astype(F32) * _row_mask(i, tm)
    return qn, kn, _log_sigmoid(f + bf) * keep


def _fox_post_fn(i, o, gate):
    return (o * jax.nn.sigmoid(gate) * _row_mask(i, o.shape[0]),)


def _gla_pre_fn(i, alr, wa, ba):
    z = _dot_ad(alr, wa, "nn") + ba
    return (_log_sigmoid(z) * (1.0 / 16.0) * _row_mask(i, alr.shape[0]),)


def _headwise_norm_gate(o, gate, gain, dv, mask):
    pieces = []
    for h in range(o.shape[1] // dv):
        oh = o[:, h * dv:(h + 1) * dv]
        pieces.append(oh * lax.rsqrt(jnp.mean(oh * oh, axis=1, keepdims=True) + NORM_EPS) * gain)
    return jnp.concatenate(pieces, axis=1) * _silu(gate) * mask


def _gla_post_fn(i, o, r, gain):
    return (_headwise_norm_gate(o, r, gain, GLA_DV, _row_mask(i, o.shape[0])),)


def _gdn_post_fn(i, o, gate, gain):
    return (_headwise_norm_gate(o, gate, gain, GDN_DV, _row_mask(i, o.shape[0])),)


def _gdn_gates_fn(i, sm, alog, dtb):
    tm = sm.shape[0]
    lane = _iota((tm, LANE), 1)
    g = -jnp.exp(alog) * _softplus(sm + dtb)
    beta = jax.nn.sigmoid(sm)
    out = jnp.where(lane < GDN_HEADS, g, jnp.where(lane < 2 * GDN_HEADS, beta, 0.0))
    return (out * _row_mask(i, tm),)


@functools.partial(jax.custom_vjp, nondiff_argnums=(1,))
def _window_rows(xw, start):
    n = xw.shape[0]
    return pltpu.roll(xw, n - start, 0)[:n // 2]


def _window_rows_fwd(xw, start):
    return _window_rows(xw, start), None


def _window_rows_bwd(start, _, ct):
    return (pltpu.roll(jnp.concatenate([ct, jnp.zeros_like(ct)], axis=0), start, 0),)


_window_rows.defvjp(_window_rows_fwd, _window_rows_bwd)


def _gdn_conv_fn(kind):
    def fn(i, prev, cur, w):
        tm = cur.shape[0]
        xw = jnp.concatenate([prev * jnp.where(i > 0, 1.0, 0.0), cur], axis=0)
        y = jnp.zeros(cur.shape, F32)
        for j in range(GDN_CONV):
            tap = jnp.sum(jnp.where(_iota(w.shape, 0) == j, w, 0.0), axis=0, keepdims=True)
            y = y + _window_rows(xw, tm - (GDN_CONV - 1) + j) * tap
        a = _silu(y * _row_mask(i, tm))
        if kind == "v":
            return (a,)
        scale = GDN_DK ** -0.5 if kind == "q" else 1.0
        pieces = []
        for h in range(GDN_HEADS):
            ah = a[:, h * GDN_DK:(h + 1) * GDN_DK]
            pieces.append(ah * lax.rsqrt(jnp.sum(ah * ah, axis=1, keepdims=True) + NORM_EPS) * scale)
        return (jnp.concatenate(pieces, axis=1),)

    return fn


def _seq_cumsum(x, reverse, name, pairs=None):
    rows, width = x.shape
    nt = rows // ROW_TILE

    def body(*refs):
        x_ref, o_ref, carry = refs[0], refs[-2], refs[-1]

        @pl.when(pl.program_id(0) == 0)
        def _():
            carry[...] = jnp.zeros(carry.shape, F32)

        xv = x_ref[...]
        if pairs is not None:
            lane = _iota(xv.shape, 1)
            for g in range(pairs.shape[0]):
                ta, tb = _pair_cols(refs[1][g])
                xv = xv + jnp.where(lane == 2 * g, ta, 0.0) + jnp.where(lane == 2 * g + 1, tb, 0.0)
        r = _iota((ROW_TILE, ROW_TILE), 0)
        c = _iota((ROW_TILE, ROW_TILE), 1)
        tri = ((c >= r) if reverse else (c <= r)).astype(F32)
        acc = _dot_hi(tri, xv, "nn") + carry[...]
        o_ref[...] = acc
        edge = 0 if reverse else ROW_TILE - 1
        carry[...] = jnp.sum(jnp.where(_iota(acc.shape, 0) == edge, acc, 0.0), axis=0, keepdims=True)

    idx = (lambda i: (nt - 1 - i, 0)) if reverse else (lambda i: (i, 0))
    idx3 = (lambda i: (0, nt - 1 - i, 0)) if reverse else (lambda i: (0, i, 0))
    operands, in_specs = [x], [pl.BlockSpec((ROW_TILE, width), idx)]
    if pairs is not None:
        operands.append(pairs)
        in_specs.append(pl.BlockSpec((pairs.shape[0], ROW_TILE, LANE), idx3))
    return pl.pallas_call(
        body, grid=(nt,), in_specs=in_specs,
        out_specs=pl.BlockSpec((ROW_TILE, width), idx), out_shape=jax.ShapeDtypeStruct(x.shape, F32),
        scratch_shapes=[pltpu.VMEM((1, width), F32)], name=name, compiler_params=_params(("arbitrary",)),
    )(*operands)


FOX_PAIRS = FOX_HEADS // 2
FOX_V_BLOCK = 2 * D_MODEL // LANE


def _pair_cols(x):
    return _lane_col(x, 0), _lane_col(x, FOX_DH)


ATT_TILE = 384
assert ROWS % ATT_TILE == 0
N_ATT = ROWS // ATT_TILE
KEY_OFF = 1e30


def _fox_scores(qa, qb, kj, ca, cb, cra, crb, mask):
    sa = _dot(qa, kj, "nt") + (ca - cra)
    sb = _dot(qb, kj, "nt") + (cb - crb)
    if mask is None:
        return sa, sb
    return jnp.where(mask, sa, NEG), jnp.where(mask, sb, NEG)


def _fox_diag_mask():
    return _iota((ATT_TILE, ATT_TILE), 1) <= _iota((ATT_TILE, ATT_TILE), 0)


def _att_off(j):
    return pl.multiple_of(j * ATT_TILE, ATT_TILE)


def _fox_attn_fwd(qn, kn, proj, c2, crow):
    def body(q_ref, k_ref, v_ref, c2_ref, crow_ref, o_ref, lse_ref):
        i = pl.program_id(1)
        q = q_ref[...]
        is_a = _iota((ATT_TILE, LANE), 1) < FOX_DH
        qa, qb = jnp.where(is_a, q, 0), jnp.where(is_a, 0, q)
        ca, cb = _pair_cols(c2_ref[0])

        def block(off, size, mask, carry):
            ma, la, acca, mb, lb, accb = carry
            kj = k_ref[pl.ds(off, size), :]
            vj = v_ref[pl.ds(off, size), :].astype(BF16)
            sa, sb = _fox_scores(qa, qb, kj, ca, cb, crow_ref[0, 0:1, pl.ds(off, size)],
                                 crow_ref[0, 1:2, pl.ds(off, size)], mask)

            def online(s, m, l, acc):
                m_new = jnp.maximum(m, jnp.max(s, axis=1, keepdims=True))
                p = jnp.exp(s - m_new)
                alpha = jnp.exp(m - m_new)
                return m_new, alpha * l + jnp.sum(p, axis=1, keepdims=True), alpha * acc + _dot(p, vj, "nn")

            return online(sa, ma, la, acca) + online(sb, mb, lb, accb)

        m0 = jnp.full((ATT_TILE, 1), NEG, F32)
        l0 = jnp.zeros((ATT_TILE, 1), F32)
        a0 = jnp.zeros((ATT_TILE, LANE), F32)
        carry = lax.fori_loop(0, i, lambda j, cr: block(_att_off(j), ATT_TILE, None, cr), (m0, l0, a0, m0, l0, a0))
        ma, la, acca, mb, lb, accb = block(_att_off(i), ATT_TILE, _fox_diag_mask(), carry)
        o_ref[...] = jnp.where(is_a, acca / la, accb / lb)
        lse_ref[0] = jnp.where(is_a, ma + jnp.log(la), mb + jnp.log(lb))

    return pl.pallas_call(
        body, grid=(FOX_PAIRS, N_ATT),
        in_specs=[
            pl.BlockSpec((ATT_TILE, LANE), lambda g, i: (i, g)),
            pl.BlockSpec((ROWS, LANE), lambda g, i: (0, g)),
            pl.BlockSpec((ROWS, LANE), lambda g, i: (0, FOX_V_BLOCK + g)),
            pl.BlockSpec((1, ATT_TILE, LANE), lambda g, i: (g, i, 0)),
            pl.BlockSpec((1, 2, ROWS), lambda g, i: (g, 0, 0)),
        ],
        out_specs=[
            pl.BlockSpec((ATT_TILE, LANE), lambda g, i: (i, g)),
            pl.BlockSpec((1, ATT_TILE, LANE), lambda g, i: (g, i, 0)),
        ],
        out_shape=[jax.ShapeDtypeStruct((ROWS, D_MODEL), F32), jax.ShapeDtypeStruct((FOX_PAIRS, ROWS, LANE), F32)],
        name="fox_attn_fwd", compiler_params=_params(("parallel", "arbitrary")),
    )(qn, kn, proj, c2, crow)


def _fox_attn_bwd(qn, kn, proj, c2, crow, lse2, o, do, blockdiag):
    def body(q_ref, k_ref, v_ref, c2_ref, crow_ref, lse_ref, o_ref, do_ref, bd_ref,
             dq_ref, dk_ref, dv_ref, dc_ref, dcq_ref, delta_ref):
        j = pl.program_id(1)

        @pl.when(j == 0)
        def _():
            for t in range(N_ATT):
                rows = pl.ds(t * ATT_TILE, ATT_TILE)
                delta_ref[rows, :] = _dot_hi(do_ref[rows, :] * o_ref[rows, :], bd_ref[...], "nn")
            dq_ref[...] = jnp.zeros(dq_ref.shape, F32)
            dcq_ref[...] = jnp.zeros(dcq_ref.shape, F32)

        kj = k_ref[...]
        vj = v_ref[...].astype(BF16)
        cra, crb = crow_ref[0, 0:1, :], crow_ref[0, 1:2, :]
        is_a = _iota((ATT_TILE, LANE), 1) < FOX_DH

        def block(off, mask, carry):
            dk, dv, dca, dcb = carry
            rows = pl.ds(off, ATT_TILE)
            q = q_ref[rows, :]
            qa, qb = jnp.where(is_a, q, 0), jnp.where(is_a, 0, q)
            do_t = do_ref[rows, :]
            doa, dob = jnp.where(is_a, do_t, 0.0).astype(BF16), jnp.where(is_a, 0.0, do_t).astype(BF16)
            ca, cb = _pair_cols(c2_ref[0, rows, :])
            lsa, lsb = _pair_cols(lse_ref[0, rows, :])
            dla, dlb = _pair_cols(delta_ref[rows, :])
            sa, sb = _fox_scores(qa, qb, kj, ca, cb, cra, crb, mask)
            pa, pb = jnp.exp(sa - lsa), jnp.exp(sb - lsb)
            dsa = pa * (_dot(doa, vj, "nt") - dla)
            dsb = pb * (_dot(dob, vj, "nt") - dlb)
            dq_ref[rows, :] += jnp.where(is_a, _dot(dsa, kj, "nn"), _dot(dsb, kj, "nn"))
            dcq_ref[0, rows, :] += jnp.where(is_a, jnp.sum(dsa, axis=1, keepdims=True),
                                             jnp.sum(dsb, axis=1, keepdims=True))
            dv = dv + _dot(pa, doa, "tn") + _dot(pb, dob, "tn")
            dk = dk + _dot(dsa, qa, "tn") + _dot(dsb, qb, "tn")
            dca = dca - jnp.sum(dsa, axis=0, keepdims=True)
            dcb = dcb - jnp.sum(dsb, axis=0, keepdims=True)
            return dk, dv, dca, dcb

        z = jnp.zeros((ATT_TILE, LANE), F32)
        zr = jnp.zeros((1, ATT_TILE), F32)
        carry = block(_att_off(j), _fox_diag_mask(), (z, z, zr, zr))
        dk, dv, dca, dcb = lax.fori_loop(j + 1, N_ATT, lambda i, cr: block(_att_off(i), None, cr), carry)
        dk_ref[...] = dk
        dv_ref[...] = dv.astype(dv_ref.dtype)
        dc_ref[0, 0:1, :] = dca
        dc_ref[0, 1:2, :] = dcb

    whole = pl.BlockSpec((ROWS, LANE), lambda g, j: (0, g))
    pair_whole = pl.BlockSpec((1, ROWS, LANE), lambda g, j: (g, 0, 0))
    tile = pl.BlockSpec((ATT_TILE, LANE), lambda g, j: (j, g))
    key_rows = pl.BlockSpec((1, 2, ATT_TILE), lambda g, j: (g, 0, j))
    return pl.pallas_call(
        body, grid=(FOX_PAIRS, N_ATT),
        in_specs=[
            whole, tile,
            pl.BlockSpec((ATT_TILE, LANE), lambda g, j: (j, FOX_V_BLOCK + g)),
            pair_whole, key_rows, pair_whole, whole, whole,
            pl.BlockSpec((LANE, LANE), lambda g, j: (0, 0)),
        ],
        out_specs=[whole, tile, tile, key_rows, pair_whole],
        out_shape=[jax.ShapeDtypeStruct((ROWS, D_MODEL), F32), jax.ShapeDtypeStruct((ROWS, D_MODEL), F32),
                   jax.ShapeDtypeStruct((ROWS, D_MODEL), BF16), jax.ShapeDtypeStruct((FOX_PAIRS, 2, ROWS), F32),
                   jax.ShapeDtypeStruct((FOX_PAIRS, ROWS, LANE), F32)],
        scratch_shapes=[pltpu.VMEM((ROWS, LANE), F32)],
        name="fox_attn_bwd", compiler_params=_params(("parallel", "arbitrary")),
    )(qn, kn, proj, c2, crow, lse2, o, do, blockdiag)


def _gla_chunk(q, k, v, g, st, dot):
    n = len(q)
    c = q[0].shape[0]
    incl = _iota((c, c), 1) <= _iota((c, c), 0)
    incl_f = incl.astype(F32)
    b = [_dot_hi(incl_f, g[i], "nn") for i in range(n)]
    qe = [q[i] * (GLA_DK ** -0.5) * jnp.exp(b[i]) for i in range(n)]
    ke = [k[i] * jnp.exp(-b[i]) for i in range(n)]
    att = [jnp.where(incl, dot(qe[i], ke[i], "nt"), 0.0) for i in range(n)]
    o_intra = [dot(att[i], v[i], "nn") for i in range(n)]
    o_inter = [dot(qe[i], st[i], "nt") for i in range(n)]
    b_last = [_last_row(b[i]) for i in range(n)]
    kd = [k[i] * jnp.exp(b_last[i] - b[i]) for i in range(n)]
    upd = [dot(v[i], kd[i], "tn") for i in range(n)]
    return ([o_intra[i] + o_inter[i] for i in range(n)],
            [st[i] * jnp.exp(b_last[i]) + upd[i] for i in range(n)])


def _tri_inverse(low):
    c = low[0].shape[0]
    eye = (_iota((c, c), 0) == _iota((c, c), 1)).astype(F32)
    x, p = [eye - l for l in low], list(low)
    span = 2
    while span < c:
        p = [_dot_mid(pi, pi, "nn") for pi in p]
        x = [_dot_mid(xi, eye + pi, "nn") for xi, pi in zip(x, p)]
        span *= 2
    return x


def _solve(low, rhs):
    return [_dot_mid(inv, r, "nn") for inv, r in zip(_tri_inverse(low), rhs)]


@jax.custom_vjp
def _solve_ad(low, rhs):
    return _solve(low, rhs)


def _solve_ad_fwd(low, rhs):
    inv = _tri_inverse(low)
    sol = [_dot_mid(i, r, "nn") for i, r in zip(inv, rhs)]
    return sol, (inv, sol)


def _solve_ad_bwd(res, ct):
    inv, sol = res
    drhs = [_dot_mid(i, c, "tn") for i, c in zip(inv, ct)]
    return [-_dot_mid(d, s, "nt") for d, s in zip(drhs, sol)], drhs


_solve_ad.defvjp(_solve_ad_fwd, _solve_ad_bwd)


def _gdn_chunk(q, k, v, gates, st, heads, dot, solve):
    n = len(q)
    c = q[0].shape[0]
    incl = _iota((c, c), 1) <= _iota((c, c), 0)
    strict = _iota((c, c), 1) < _iota((c, c), 0)
    incl_f = incl.astype(F32)
    upper_f = (_iota((c, c), 0) <= _iota((c, c), 1)).astype(F32)
    gcol = [_lane_col(gates, h) for h in heads]
    beta = [_lane_col(gates, h + GDN_HEADS) for h in heads]
    b = [_dot_hi(incl_f, jnp.broadcast_to(gc, (c, LANE)), "nn") for gc in gcol]
    g_sq = [jnp.broadcast_to(gc, (c, c)) for gc in gcol]
    b_col = [_dot_hi(incl_f, gs, "nn") for gs in g_sq]
    b_row = [_dot_hi(gs, upper_f, "tn") for gs in g_sq]
    diff = [b_col[i] - b_row[i] for i in range(n)]
    kb = [k[i] * beta[i] for i in range(n)]
    vb = [v[i] * beta[i] for i in range(n)]
    low = [dot(kb[i], k[i], "nt") * jnp.exp(jnp.where(strict, diff[i], NEG)) for i in range(n)]
    eb = [jnp.exp(bi) for bi in b]
    sol = solve(low, [jnp.concatenate([vb[i], kb[i] * eb[i]], axis=1) for i in range(n)])
    w_st = [dot(sol[i][:, GDN_DV:], st[i], "nn") for i in range(n)]
    v_new = [sol[i][:, :GDN_DV] - w_st[i] for i in range(n)]
    att = [dot(q[i], k[i], "nt") * jnp.exp(jnp.where(incl, diff[i], NEG)) for i in range(n)]
    o_intra = [dot(att[i], v_new[i], "nn") for i in range(n)]
    o_inter = [dot(q[i] * eb[i], st[i], "nn") for i in range(n)]
    b_last = [_last_row(bi) for bi in b]
    kd = [k[i] * jnp.exp(b_last[i] - b[i]) for i in range(n)]
    upd = [dot(kd[i], v_new[i], "tn") for i in range(n)]
    return ([o_intra[i] + o_inter[i] for i in range(n)],
            [st[i] * jnp.exp(b_last[i]) + upd[i] for i in range(n)])


N_CHUNKS = ROWS // CHUNK
GDN_GROUP = 8


def _head_lanes(width, n):
    return [slice(i * width, (i + 1) * width) for i in range(n)]


def _gla_scan_fwd(proj, garr):
    kl, vl = _head_lanes(GLA_DK, GLA_HEADS), _head_lanes(GLA_DV, GLA_HEADS)

    def body(q_ref, k_ref, v_ref, g_ref, o_ref, states_ref, st):
        @pl.when(pl.program_id(0) == 0)
        def _():
            st[...] = jnp.zeros(st.shape, F32)

        s_in = [st[i] for i in range(GLA_HEADS)]
        o, s_out = _gla_chunk([q_ref[:, ln] for ln in kl], [k_ref[:, ln] for ln in kl], [v_ref[:, ln] for ln in vl],
                              [g_ref[:, ln] for ln in kl], s_in, _dot)
        for i in range(GLA_HEADS):
            states_ref[i, 0] = s_in[i]
            o_ref[:, vl[i]] = o[i]
            st[i] = s_out[i]

    return pl.pallas_call(
        body, grid=(N_CHUNKS,),
        in_specs=[
            pl.BlockSpec((CHUNK, GLA_QK), lambda c: (c, 0)),
            pl.BlockSpec((CHUNK, GLA_QK), lambda c: (c, 1)),
            pl.BlockSpec((CHUNK, GLA_V), lambda c: (c, 1)),
            pl.BlockSpec((CHUNK, GLA_QK), lambda c: (c, 0)),
        ],
        out_specs=[
            pl.BlockSpec((CHUNK, GLA_V), lambda c: (c, 0)),
            pl.BlockSpec((GLA_HEADS, 1, GLA_DV, GLA_DK), lambda c: (0, c, 0, 0)),
        ],
        out_shape=[jax.ShapeDtypeStruct((ROWS, GLA_V), F32),
                   jax.ShapeDtypeStruct((GLA_HEADS, N_CHUNKS, GLA_DV, GLA_DK), F32)],
        scratch_shapes=[pltpu.VMEM((GLA_HEADS, GLA_DV, GLA_DK), F32)],
        name="gla_scan_fwd", compiler_params=_params(("arbitrary",)),
    )(proj, proj, proj, garr)


def _gla_scan_bwd(proj, garr, states, do):
    last = N_CHUNKS - 1
    kl, vl = _head_lanes(GLA_DK, GLA_HEADS), _head_lanes(GLA_DV, GLA_HEADS)

    def body(q_ref, k_ref, v_ref, g_ref, s_ref, do_ref, dq_ref, dk_ref, dv_ref, dg_ref, dst):
        @pl.when(pl.program_id(0) == 0)
        def _():
            dst[...] = jnp.zeros(dst.shape, F32)

        _, vjp = jax.vjp(lambda q, k, v, g, s: _gla_chunk(q, k, v, g, s, _dot_ad),
                         [q_ref[:, ln] for ln in kl], [k_ref[:, ln] for ln in kl], [v_ref[:, ln] for ln in vl],
                         [g_ref[:, ln] for ln in kl], [s_ref[i, 0] for i in range(GLA_HEADS)])
        dq, dk, dv, dg, ds = vjp(([do_ref[:, ln] for ln in vl], [dst[i] for i in range(GLA_HEADS)]))
        for i in range(GLA_HEADS):
            dq_ref[:, kl[i]] = dq[i].astype(dq_ref.dtype)
            dk_ref[:, kl[i]] = dk[i].astype(dk_ref.dtype)
            dv_ref[:, vl[i]] = dv[i].astype(dv_ref.dtype)
            dg_ref[:, kl[i]] = dg[i]
            dst[i] = ds[i]

    kspec = pl.BlockSpec((CHUNK, GLA_QK), lambda c: (last - c, 0))
    vspec = pl.BlockSpec((CHUNK, GLA_V), lambda c: (last - c, 0))
    return pl.pallas_call(
        body, grid=(N_CHUNKS,),
        in_specs=[
            kspec,
            pl.BlockSpec((CHUNK, GLA_QK), lambda c: (last - c, 1)),
            pl.BlockSpec((CHUNK, GLA_V), lambda c: (last - c, 1)),
            kspec,
            pl.BlockSpec((GLA_HEADS, 1, GLA_DV, GLA_DK), lambda c: (0, last - c, 0, 0)),
            vspec,
        ],
        out_specs=[kspec, kspec, vspec, kspec],
        out_shape=[jax.ShapeDtypeStruct((ROWS, GLA_QK), BF16), jax.ShapeDtypeStruct((ROWS, GLA_QK), BF16),
                   jax.ShapeDtypeStruct((ROWS, GLA_V), BF16), jax.ShapeDtypeStruct((ROWS, GLA_QK), F32)],
        scratch_shapes=[pltpu.VMEM((GLA_HEADS, GLA_DV, GLA_DK), F32)],
        name="gla_scan_bwd", compiler_params=_params(("arbitrary",)),
    )(proj, proj, proj, garr, states, do)


def _gdn_scan_fwd(qn, kn, vn, gates):
    def body(q_ref, k_ref, v_ref, g_ref, o_ref, states_ref, st):
        @pl.when((pl.program_id(0) == 0) & (pl.program_id(1) == 0))
        def _():
            st[...] = jnp.zeros(st.shape, F32)

        heads = [pl.program_id(1) * GDN_GROUP + hh for hh in range(GDN_GROUP)]
        lanes = _head_lanes(GDN_DK, GDN_GROUP)
        s_in = [st[h] for h in heads]
        o, s_out = _gdn_chunk([q_ref[:, ln] for ln in lanes], [k_ref[:, ln] for ln in lanes],
                              [v_ref[:, ln] for ln in lanes], g_ref[...], s_in, heads, _dot, _solve)
        for hh in range(GDN_GROUP):
            states_ref[hh, 0] = s_in[hh]
            o_ref[:, lanes[hh]] = o[hh]
            st[heads[hh]] = s_out[hh]

    hspec = pl.BlockSpec((CHUNK, GDN_GROUP * GDN_DK), lambda c, h: (c, h))
    return pl.pallas_call(
        body, grid=(N_CHUNKS, GDN_HEADS // GDN_GROUP),
        in_specs=[hspec, hspec, hspec, pl.BlockSpec((CHUNK, LANE), lambda c, h: (c, 0))],
        out_specs=[hspec, pl.BlockSpec((GDN_GROUP, 1, GDN_DK, GDN_DV), lambda c, h: (h, c, 0, 0))],
        out_shape=[jax.ShapeDtypeStruct((ROWS, D_MODEL), F32),
                   jax.ShapeDtypeStruct((GDN_HEADS, N_CHUNKS, GDN_DK, GDN_DV), F32)],
        scratch_shapes=[pltpu.VMEM((GDN_HEADS, GDN_DK, GDN_DV), F32)],
        name="gdn_scan_fwd", compiler_params=_params(("arbitrary", "arbitrary")),
    )(qn, kn, vn, gates)


def _gdn_scan_bwd(qn, kn, vn, gates, states, do):
    last = N_CHUNKS - 1

    def body(q_ref, k_ref, v_ref, g_ref, s_ref, do_ref, dq_ref, dk_ref, dv_ref, dg_ref, dst):
        @pl.when((pl.program_id(0) == 0) & (pl.program_id(1) == 0))
        def _():
            dst[...] = jnp.zeros(dst.shape, F32)

        @pl.when(pl.program_id(1) == 0)
        def _():
            dg_ref[...] = jnp.zeros(dg_ref.shape, F32)

        heads = [pl.program_id(1) * GDN_GROUP + hh for hh in range(GDN_GROUP)]
        lanes = _head_lanes(GDN_DK, GDN_GROUP)
        _, vjp = jax.vjp(lambda q, k, v, g, s: _gdn_chunk(q, k, v, g, s, heads, _dot_ad, _solve_ad),
                         [q_ref[:, ln] for ln in lanes], [k_ref[:, ln] for ln in lanes],
                         [v_ref[:, ln] for ln in lanes], g_ref[...], [s_ref[hh, 0] for hh in range(GDN_GROUP)])
        dq, dk, dv, dg, ds = vjp(([do_ref[:, ln] for ln in lanes], [dst[h] for h in heads]))
        for hh in range(GDN_GROUP):
            dq_ref[:, lanes[hh]] = dq[hh]
            dk_ref[:, lanes[hh]] = dk[hh]
            dv_ref[:, lanes[hh]] = dv[hh]
            dst[heads[hh]] = ds[hh]
        dg_ref[...] += dg

    hspec = pl.BlockSpec((CHUNK, GDN_GROUP * GDN_DK), lambda c, h: (last - c, h))
    gspec = pl.BlockSpec((CHUNK, LANE), lambda c, h: (last - c, 0))
    return pl.pallas_call(
        body, grid=(N_CHUNKS, GDN_HEADS // GDN_GROUP),
        in_specs=[hspec, hspec, hspec, gspec,
                  pl.BlockSpec((GDN_GROUP, 1, GDN_DK, GDN_DV), lambda c, h: (h, last - c, 0, 0)), hspec],
        out_specs=[hspec, hspec, hspec, gspec],
        out_shape=[jax.ShapeDtypeStruct((ROWS, D_MODEL), F32)] * 3 + [jax.ShapeDtypeStruct((ROWS, LANE), F32)],
        scratch_shapes=[pltpu.VMEM((GDN_HEADS, GDN_DK, GDN_DV), F32)],
        name="gdn_scan_bwd", compiler_params=_params(("arbitrary", "arbitrary")),
    )(qn, kn, vn, gates, states, do)


def _gdn_conv_fwd(proj, w, kind, cb):
    fn = _gdn_conv_fn(kind)
    nt = ROWS // ROW_TILE

    def body(prev_ref, cur_ref, w_ref, o_ref):
        (out,) = fn(pl.program_id(0), prev_ref[...], cur_ref[...], w_ref[...])
        o_ref[...] = out

    return pl.pallas_call(
        body, grid=(nt,),
        in_specs=[pl.BlockSpec((ROW_TILE, D_MODEL), lambda i: (jnp.maximum(i - 1, 0), cb)),
                  pl.BlockSpec((ROW_TILE, D_MODEL), lambda i: (i, cb)),
                  pl.BlockSpec(w.shape, lambda i: (0, 0))],
        out_specs=pl.BlockSpec((ROW_TILE, D_MODEL), lambda i: (i, 0)),
        out_shape=jax.ShapeDtypeStruct((ROWS, D_MODEL), F32),
        name="gdn_conv_fwd_" + kind, compiler_params=_params(("parallel",)),
    )(proj, proj, w)


def _gdn_conv_bwd(proj, w, kind, cb, ct):
    fn = _gdn_conv_fn(kind)
    nt = ROWS // ROW_TILE

    def body(prev_ref, cur_ref, w_ref, ct_ref, dx_ref, dw_ref, carry):
        step = pl.program_id(0)
        i = nt - 1 - step

        @pl.when(step == 0)
        def _():
            carry[...] = jnp.zeros(carry.shape, F32)
            dw_ref[...] = jnp.zeros(dw_ref.shape, F32)

        _, vjp = jax.vjp(lambda p, c, ww: fn(i, p, c, ww), prev_ref[...], cur_ref[...], w_ref[...])
        dprev, dcur, dw = vjp((ct_ref[...],))
        dx_ref[...] = (dcur + carry[...]).astype(dx_ref.dtype)
        carry[...] = dprev
        dw_ref[...] += dw

    return pl.pallas_call(
        body, grid=(nt,),
        in_specs=[pl.BlockSpec((ROW_TILE, D_MODEL), lambda s: (jnp.maximum(nt - 2 - s, 0), cb)),
                  pl.BlockSpec((ROW_TILE, D_MODEL), lambda s: (nt - 1 - s, cb)),
                  pl.BlockSpec(w.shape, lambda s: (0, 0)),
                  pl.BlockSpec((ROW_TILE, D_MODEL), lambda s: (nt - 1 - s, 0))],
        out_specs=[pl.BlockSpec((ROW_TILE, D_MODEL), lambda s: (nt - 1 - s, 0)),
                   pl.BlockSpec(w.shape, lambda s: (0, 0))],
        out_shape=[jax.ShapeDtypeStruct((ROWS, D_MODEL), BF16), jax.ShapeDtypeStruct(w.shape, F32)],
        scratch_shapes=[pltpu.VMEM((ROW_TILE, D_MODEL), F32)],
        name="gdn_conv_bwd_" + kind, compiler_params=_params(("arbitrary",)),
    )(proj, proj, w, ct)


def _loss_head(h, target):
    nt = ROWS // ROW_TILE
    lead = PAD_ROWS // ROW_TILE

    def body(h_ref, t_ref, dh_ref, acc_ref):
        i = pl.program_id(0)

        @pl.when(i == 0)
        def _():
            acc_ref[...] = jnp.zeros(acc_ref.shape, F32)

        err = (h_ref[...] - t_ref[...]) * jnp.where(i >= lead, 1.0, 0.0)
        dh_ref[...] = err * (1.0 / D_MODEL)
        acc_ref[...] += jnp.sum(err * err, axis=0, keepdims=True)

    return pl.pallas_call(
        body, grid=(nt,),
        in_specs=[pl.BlockSpec((ROW_TILE, D_MODEL), lambda i: (i, 0)),
                  pl.BlockSpec((ROW_TILE, D_MODEL), lambda i: (jnp.maximum(i - lead, 0), 0))],
        out_specs=[pl.BlockSpec((ROW_TILE, D_MODEL), lambda i: (i, 0)),
                   pl.BlockSpec((1, D_MODEL), lambda i: (0, 0))],
        out_shape=[jax.ShapeDtypeStruct((ROWS, D_MODEL), F32), jax.ShapeDtypeStruct((1, D_MODEL), F32)],
        name="loss_head", compiler_params=_params(("arbitrary",)),
    )(h, target)


MESH = pl.DeviceIdType.MESH
ANY = pl.BlockSpec(memory_space=pl.ANY)


N_PEERS = N_DEV - 1


HBM_SPEC = pl.BlockSpec(memory_space=pltpu.HBM)
SEM_SPEC = pl.BlockSpec(memory_space=pltpu.SEMAPHORE)
SIDE_EFFECT = pltpu.SideEffectType.DATAFLOW_SIDE_EFFECTING


def _me_and_peers():
    x, y, c = lax.axis_index("x"), lax.axis_index("y"), lax.axis_index("c")
    peers = []
    for k in range(1, N_DEV):
        pid = (1 - x if k & 4 else x, 1 - y if k & 2 else y, 1 - c if k & 1 else c)
        peers.append((4 * pid[0] + 2 * pid[1] + pid[2], pid))
    return 4 * x + 2 * y + c, peers


def _split_copies(src_refs, land_refs, send_sems, recv_sems, spread):
    me, peers = _me_and_peers()
    out = []
    for t, (src, land) in enumerate(zip(src_refs, land_refs)):
        for k, (peer, pid) in enumerate(peers):
            sems = dict(send_sem=send_sems.at[N_PEERS * t + k], recv_sem=recv_sems.at[N_PEERS * t + k],
                        device_id=pid, device_id_type=MESH)
            piece = src if spread else src.at[peer]
            out.append((pltpu.make_async_remote_copy(src_ref=piece, dst_ref=land.at[me], **sems),
                        pltpu.make_async_remote_copy(src_ref=piece, dst_ref=land.at[peer], **sems)))
    return out


def _send_start(srcs, spread, name):
    n = len(srcs)
    lands = [lax.empty((N_DEV,) + (s.shape if spread else s.shape[1:]), s.dtype) for s in srcs]

    def body(*refs):
        for out, _ in _split_copies(refs[:n], refs[n:2 * n], refs[2 * n], refs[2 * n + 1], spread):
            out.start()
        refs[-1][...] = jnp.zeros(refs[-1].shape, F32)

    thru = [pltpu.HBM(a.shape, a.dtype) for a in list(srcs) + lands]
    res = pl.pallas_call(
        body, name=name,
        out_shape=(pltpu.SemaphoreType.DMA((N_PEERS * n,)), pltpu.SemaphoreType.DMA((N_PEERS * n,)), *thru,
                   jax.ShapeDtypeStruct((8, LANE), F32)),
        in_specs=[HBM_SPEC] * (2 * n),
        out_specs=(SEM_SPEC, SEM_SPEC, *[HBM_SPEC] * (2 * n), pl.BlockSpec(memory_space=pltpu.VMEM)),
        input_output_aliases={t: 2 + t for t in range(2 * n)},
        compiler_params=pltpu.CompilerParams(has_side_effects=SIDE_EFFECT),
    )(*[pltpu.with_memory_space_constraint(a, pltpu.HBM) for a in list(srcs) + lands])
    return res[0], res[1], list(res[2:2 + n]), list(res[2 + n:2 + 2 * n]), res[-1]


def _send_wait(started, spread, after, name):
    send_sems, recv_sems, srcs, lands, _ = started
    n = len(srcs)

    def body(*refs):
        for out, arrival in _split_copies(refs[:n], refs[n:2 * n], refs[2 * n], refs[2 * n + 1], spread):
            out.wait_send()
            arrival.wait_recv()

    thru = [pltpu.HBM(a.shape, a.dtype) for a in srcs + lands]
    res = pl.pallas_call(
        body, name=name, out_shape=tuple(thru),
        in_specs=[HBM_SPEC] * (2 * n) + [SEM_SPEC, SEM_SPEC, ANY], out_specs=tuple([HBM_SPEC] * (2 * n)),
        input_output_aliases={t: t for t in range(2 * n)},
        compiler_params=pltpu.CompilerParams(has_side_effects=SIDE_EFFECT),
    )(*srcs, *lands, send_sems, recv_sems, after)
    return list(res[n:])


def _adamw(recvs, w, m, v, name):
    n_layers = len(recvs)
    rows, cols = recvs[0].shape[1:]
    tr = _pick(rows, (256, 128, 64, 32, 16))
    nb = rows // tr

    def body(*refs):
        r_refs = refs[:n_layers]
        w_ref, m_ref, v_ref, g_ref, d_ref, m2_ref, v2_ref = refs[n_layers:]

        def update(r_ref):
            g = r_ref[0].astype(F32)
            for k in range(1, N_DEV):
                g = g + r_ref[k].astype(F32)
            m2 = ADAM_B1 * m_ref[...] + (1.0 - ADAM_B1) * g
            v2 = ADAM_B2 * v_ref[...] + (1.0 - ADAM_B2) * (g * g)
            m_hat = m2 / (1.0 - ADAM_B1 ** ADAM_STEP)
            v_hat = v2 / (1.0 - ADAM_B2 ** ADAM_STEP)
            g_ref[...] = g
            d_ref[...] = -ADAM_LR * (m_hat / (jnp.sqrt(v_hat) + ADAM_EPS) + ADAM_WD * w_ref[...])
            m2_ref[...] = m2
            v2_ref[...] = v2

        if n_layers == 1:
            update(r_refs[0])
        else:
            for layer in range(n_layers):
                pl.when(pl.program_id(0) == layer)(functools.partial(update, r_refs[layer]))

    def recv_spec(layer):
        return pl.BlockSpec((N_DEV, tr, cols), lambda a, i: (0, jnp.where(a == layer, i, 0), 0))

    spec = pl.BlockSpec((tr, cols), lambda a, i: (a * nb + i, 0))
    return pl.pallas_call(
        body, grid=(n_layers, nb),
        in_specs=[recv_spec(layer) for layer in range(n_layers)] + [spec, spec, spec],
        out_specs=[spec] * 4, out_shape=[jax.ShapeDtypeStruct(w.shape, F32)] * 4,
        name=name, compiler_params=_params(("arbitrary", "arbitrary")),
    )(*recvs, w, m, v)


def _padded_rows(n, mult):
    rows = -(-n // PACK_COLS)
    return -(-rows // mult) * mult


def _pack(flat_pieces, mult, dtype, lead=()):
    flat = jnp.concatenate([p.astype(dtype) for p in flat_pieces], axis=-1)
    n = flat.shape[-1]
    rows = _padded_rows(n, mult)
    flat = jnp.pad(flat, [(0, 0)] * len(lead) + [(0, rows * PACK_COLS - n)])
    return flat.reshape(lead + (rows, PACK_COLS))


def _unpack(packed, shapes):
    flat = packed.reshape(-1)
    out, off = {}, 0
    for name, shape in shapes:
        size = 1
        for s in shape:
            size *= s
        out[name] = flat[off:off + size].reshape(shape)
        off += size
    return out


def _local_2d(arr):
    return arr.reshape(-1, arr.shape[-1])


SHARD_2D = {n: (s[-2] if len(s) > 2 else s[0], s[-1], a == len(s) - 1) for n, s, a in SHARDED}
SHARD_2D["gdn_conv_w"] = (GDN_CONV, GDN_CONV_DIM // N_DEV, True)


def _wire(name):
    rows, cols, _ = SHARD_2D[name]
    return BF16 if rows * cols >= 1 << 16 else F32


def _layer_tensors(i):
    kind, j = i % 3, i // 3
    pre = ("fox", "gla", "gdn")[kind]
    mix = [(pre + "_w_in", j), (pre + "_w_out", j)]
    if kind == 1:
        mix.append(("gla_w_alpha2", j))
    if kind == 2:
        mix.append(("gdn_conv_w", j))
    if i == 0:
        mix.append(("meta_tokens", None))
    return {"mix": mix, "ffn": [("w_gate_up", i), ("w_down", i)]}


PARTS = ("mix", "ffn")


def _layer_block(arr, idx):
    return arr if idx is None else _local_2d(arr[idx])


def _to_slabs(full2d, name):
    rows, cols, by_cols = SHARD_2D[name]
    if by_cols:
        parts = full2d if isinstance(full2d, tuple) else (full2d,)
        per = parts[0].shape[1] // cols
        return jnp.stack([parts[d // per][:, cols * (d % per):cols * (d % per + 1)] for d in range(N_DEV)])
    return full2d.reshape(N_DEV, rows, cols)


def _from_gathered(g, name):
    rows, cols, by_cols = SHARD_2D[name]
    if by_cols:
        return jnp.concatenate([g[d] for d in range(N_DEV)], axis=1)
    return g.reshape(N_DEV * rows, cols)


def _pad_cols(w, total):
    return jnp.pad(w, [(0, 0)] * (w.ndim - 1) + [(0, total - w.shape[-1])])


def _pad_lanes(v):
    return _pad_cols(v, LANE)


def _x(arr, width, cb0=0, moves=0):
    return (arr, width, cb0, moves)


def _rms_fwd(h, g):
    return _row_fwd(_rms_fn, [_x(h, D_MODEL)], [g], [], [(D_MODEL, D_MODEL, BF16)], "rms_fwd")[0]


FUSED_ROWS = 384


def _matmul_add_norm(a, b, add, g, name):
    m, k = a.shape
    n = b.shape[1]

    def body(a_ref, b_ref, add_ref, g_ref, h_ref, y_ref):
        h2 = _dot(a_ref[...], b_ref[...], "nn") + add_ref[...]
        h_ref[...] = h2
        y_ref[...] = _rms_fn(0, h2, g_ref[...])[0].astype(y_ref.dtype)

    rows = pl.BlockSpec((FUSED_ROWS, n), lambda i: (i, 0))
    return pl.pallas_call(
        body, grid=(m // FUSED_ROWS,),
        in_specs=[pl.BlockSpec((FUSED_ROWS, k), lambda i: (i, 0)), pl.BlockSpec((k, n), lambda i: (0, 0)), rows,
                  pl.BlockSpec((1, n), lambda i: (0, 0))],
        out_specs=[rows, rows],
        out_shape=[jax.ShapeDtypeStruct((m, n), F32), jax.ShapeDtypeStruct((m, n), BF16)],
        name=name, compiler_params=_params(("parallel",)),
    )(a, b, add, g)


def _matmul_norm_bwd(a, b, b_kblock, h, g, dres, name, prev=None):
    m, k = a.shape
    n = b.shape[0]

    def body(*refs):
        a_ref, b_ref, h_ref, g_ref, dres_ref = refs[:5]
        dh_ref, dg_ref = refs[-2:]
        dy = _dot(a_ref[...], b_ref[...], "nt")
        if prev is not None:
            dy = dy + refs[5][...]
        _, vjp = jax.vjp(lambda hh, gg: _rms_fn(0, hh, gg), h_ref[...], g_ref[...])
        dh, dg = vjp((dy,))
        dh_ref[...] = dres_ref[...] + dh

        @pl.when(pl.program_id(0) == 0)
        def _():
            dg_ref[...] = jnp.zeros(dg_ref.shape, F32)

        dg_ref[...] += dg

    rows = pl.BlockSpec((FUSED_ROWS, n), lambda i: (i, 0))
    gain = pl.BlockSpec((1, n), lambda i: (0, 0))
    operands = [a, b, h, g, dres] + ([] if prev is None else [prev])
    in_specs = [pl.BlockSpec((FUSED_ROWS, k), lambda i: (i, 0)), pl.BlockSpec((n, k), lambda i: (0, b_kblock)),
                rows, gain, rows] + ([] if prev is None else [rows])
    return pl.pallas_call(
        body, grid=(m // FUSED_ROWS,), in_specs=in_specs, out_specs=[rows, gain],
        out_shape=[jax.ShapeDtypeStruct((m, n), F32), jax.ShapeDtypeStruct((1, n), F32)],
        name=name, compiler_params=_params(("arbitrary",)),
    )(*operands)


SW_WIDTH = 256
SW_NCOL = D_FF // SW_WIDTH


def _ffn_gate_up(y, wgu):
    m, k = y.shape
    tm = _pick(m, (1408, 384, 128))

    def body(y_ref, wg_ref, wu_ref, a_ref, g_ref, u_ref):
        yv = y_ref[...]
        gt = _dot(yv, wg_ref[...], "nn")
        up = _dot(yv, wu_ref[...], "nn")
        a_ref[...] = (_silu(gt) * up).astype(a_ref.dtype)
        g_ref[...] = gt.astype(g_ref.dtype)
        u_ref[...] = up.astype(u_ref.dtype)

    tile = pl.BlockSpec((tm, SW_WIDTH), lambda i, j: (i, j))
    return pl.pallas_call(
        body, grid=(m // tm, SW_NCOL),
        in_specs=[pl.BlockSpec((tm, k), lambda i, j: (i, 0)),
                  pl.BlockSpec((k, SW_WIDTH), lambda i, j: (0, j)),
                  pl.BlockSpec((k, SW_WIDTH), lambda i, j: (0, SW_NCOL + j))],
        out_specs=[tile] * 3, out_shape=[jax.ShapeDtypeStruct((m, D_FF), BF16)] * 3,
        name="ffn_gate_up", compiler_params=_params(("parallel", "parallel")),
    )(y, wgu, wgu)


def _ffn_down_dx(dh2, wd, gt, up):
    m, k = dh2.shape
    tm = _pick(m, (1408, 384, 128))

    def body(d_ref, w_ref, g_ref, u_ref, dg_ref, du_ref):
        da = _dot(d_ref[...], w_ref[...], "nt")
        gv, uv = g_ref[...].astype(F32), u_ref[...].astype(F32)
        s = jax.nn.sigmoid(gv)
        dg_ref[...] = (da * uv * (s + gv * s * (1.0 - s))).astype(dg_ref.dtype)
        du_ref[...] = (da * gv * s).astype(du_ref.dtype)

    tile = pl.BlockSpec((tm, SW_WIDTH), lambda i, j: (i, j))
    return pl.pallas_call(
        body, grid=(m // tm, SW_NCOL),
        in_specs=[pl.BlockSpec((tm, k), lambda i, j: (i, 0)),
                  pl.BlockSpec((SW_WIDTH, k), lambda i, j: (j, 0)), tile, tile],
        out_specs=[tile] * 2, out_shape=[jax.ShapeDtypeStruct((m, D_FF), BF16)] * 2,
        name="ffn_down_dx", compiler_params=_params(("parallel", "parallel")),
    )(dh2, wd, gt, up)


def _ffn_fwd(h, y, wgu, wd, g_next):
    a, gt, up = _ffn_gate_up(y, wgu)
    if g_next is None:
        h2, y_next = _matmul(a, wd, "nn", F32, "ffn_down", add=h), None
    else:
        h2, y_next = _matmul_add_norm(a, wd, h, g_next, "ffn_down_norm")
    return h2, y_next, (h, y, gt, up, a)


def _ffn_bwd(dh2, saved, g, wgu, wd):
    h, y, gt, up, a = saved
    dgt, dup = _ffn_down_dx(dh2, wd, gt, up)
    dwd = _matmul(a, dh2, "tn", F32, "ffn_down_dw")
    dwg = _matmul(y, dgt, "tn", F32, "ffn_gate_dw")
    dwu = _matmul(y, dup, "tn", F32, "ffn_up_dw")
    dy = _matmul(dgt, wgu, "nt", F32, "ffn_gate_dx", b_kblock=0)
    dh, dg = _matmul_norm_bwd(dup, wgu, 1, h, g, dh2, "ffn_up_dx_norm", prev=dy)
    return dh, dg, (dwg, dwu), dwd


def _fox_consts():
    lane = jnp.arange(D_MODEL) // FOX_DH
    e_expand = (jnp.arange(LANE)[:, None] == lane[None, :]).astype(F32)
    e_mean = e_expand.T * (1.0 / FOX_DH)
    half = jnp.arange(LANE) // FOX_DH
    blockdiag = (half[:, None] == half[None, :]).astype(F32)
    return e_mean, e_expand, blockdiag


def _pair_layout(c):
    heads = c[:, :FOX_HEADS].T
    crow = heads.reshape(FOX_PAIRS, 2, ROWS)
    c2 = jnp.broadcast_to(crow.transpose(0, 2, 1)[:, :, :, None], (FOX_PAIRS, ROWS, 2, FOX_DH))
    crow = jnp.where(jnp.arange(ROWS) >= FIRST_ROW, crow, KEY_OFF)
    return c2.reshape(FOX_PAIRS, ROWS, LANE), crow


def _fox_fwd(y, p):
    proj = _matmul(y, p["w_in"], "nn", F32, "fox_in")
    e_mean, e_expand, _ = _fox_consts()
    nb = D_MODEL // LANE
    xs = [_x(proj, D_MODEL, 0), _x(proj, D_MODEL, 1), _x(proj, LANE, 4 * nb)]
    ps = [p["q_gain"], p["k_gain"], p["b_f"]]
    qn, kn, lf = _row_fwd(_fox_pre_fn, xs, ps, [e_mean, e_expand],
                          [(D_MODEL, D_MODEL, BF16), (D_MODEL, D_MODEL, BF16), (LANE, LANE, F32)], "fox_pre_fwd")
    c = _seq_cumsum(lf, False, "fox_cumsum")
    c2, crow = _pair_layout(c)
    o, lse2 = _fox_attn_fwd(qn, kn, proj, c2, crow)
    om = _row_fwd(_fox_post_fn, [_x(o, D_MODEL, 0), _x(proj, D_MODEL, 3)], [], [],
                  [(D_MODEL, D_MODEL, BF16)], "fox_post_fwd")[0]
    return om, (proj, qn, kn, c2, crow, o, lse2)


def _fox_bwd(dom, y, saved, p):
    proj, qn, kn, c2, crow, o, lse2 = saved
    e_mean, e_expand, blockdiag = _fox_consts()
    nb = D_MODEL // LANE
    do, dgate = _row_bwd(_fox_post_fn, [_x(o, D_MODEL, 0), _x(proj, D_MODEL, 3)], [], [], [_x(dom, D_MODEL)],
                         [(D_MODEL, F32), (D_MODEL, BF16)], "fox_post_bwd")
    dq, dk, dv, dcrow, dcq2 = _fox_attn_bwd(qn, kn, proj, c2, crow, lse2, o, do, blockdiag)
    dlf = _seq_cumsum(_pad_cols(dcrow.reshape(FOX_HEADS, ROWS).T, LANE), True, "fox_cumsum_rev", pairs=dcq2)
    xs = [_x(proj, D_MODEL, 0), _x(proj, D_MODEL, 1), _x(proj, LANE, 4 * nb)]
    ps = [p["q_gain"], p["k_gain"], p["b_f"]]
    dqr, dkr, df, dqg, dkg, dbf = _row_bwd(
        _fox_pre_fn, xs, ps, [e_mean, e_expand], [_x(dq, D_MODEL), _x(dk, D_MODEL), _x(dlf, LANE)],
        [(D_MODEL, BF16), (D_MODEL, BF16), (LANE, BF16)], "fox_pre_bwd")
    dproj = jnp.concatenate([dqr, dkr, dv, dgate, df], axis=1)
    grads = {
        "w_in": _matmul(y, dproj, "tn", F32, "fox_in_dw")[:, :FOX_IN],
        "q_gain": dqg.reshape(FOX_HEADS, FOX_DH).sum(axis=0),
        "k_gain": dkg.reshape(FOX_HEADS, FOX_DH).sum(axis=0),
        "b_f": dbf[0, :FOX_HEADS],
    }
    return dproj, grads


def _gla_fwd(y, p):
    proj = _matmul(y, p["w_in"], "nn", F32, "gla_in")
    small = (2 * GLA_QK + 2 * GLA_V) // LANE
    garr = _row_fwd(_gla_pre_fn, [_x(proj, LANE, small)], [p["w_alpha"], p["b_alpha"]], [],
                    [(GLA_QK, GLA_QK, F32)], "gla_pre_fwd")[0]
    o, states = _gla_scan_fwd(proj, garr)
    om = _row_fwd(_gla_post_fn, [_x(o, GLA_V, 0), _x(proj, GLA_V, 2)], [p["o_gain"]], [],
                  [(GLA_V, GLA_V, BF16)], "gla_post_fwd")[0]
    return om, (proj, garr, states, o)


def _gla_bwd(dom, y, saved, p):
    proj, garr, states, o = saved
    small = (2 * GLA_QK + 2 * GLA_V) // LANE
    do, dr, dgain = _row_bwd(_gla_post_fn, [_x(o, GLA_V, 0), _x(proj, GLA_V, 2)], [p["o_gain"]], [],
                             [_x(dom, GLA_V)], [(GLA_V, F32), (GLA_V, BF16)], "gla_post_bwd")
    dq, dk, dv, dg = _gla_scan_bwd(proj, garr, states, do)
    dalr, dwa, dba = _row_bwd(_gla_pre_fn, [_x(proj, LANE, small)], [p["w_alpha"], p["b_alpha"]], [],
                              [_x(dg, GLA_QK)], [(LANE, BF16)], "gla_pre_bwd")
    dproj = jnp.concatenate([dq, dk, dv, dr, dalr], axis=1)
    grads = {
        "w_in": _matmul(y, dproj, "tn", F32, "gla_in_dw")[:, :GLA_IN],
        "w_alpha": dwa[:GLA_RANK], "b_alpha": dba[0], "o_gain": dgain[0],
    }
    return dproj, grads


def _gdn_fwd(y, p):
    proj = _matmul(y, p["w_in"], "nn", F32, "gdn_in")
    nb = D_MODEL // LANE
    qn = _gdn_conv_fwd(proj, p["conv_q"], "q", 0)
    kn = _gdn_conv_fwd(proj, p["conv_k"], "k", 1)
    vn = _gdn_conv_fwd(proj, p["conv_v"], "v", 2)
    gates = _row_fwd(_gdn_gates_fn, [_x(proj, LANE, 4 * nb)], [p["a_log"], p["dt_bias"]], [],
                     [(LANE, LANE, F32)], "gdn_gates_fwd")[0]
    o, states = _gdn_scan_fwd(qn, kn, vn, gates)
    om = _row_fwd(_gdn_post_fn, [_x(o, D_MODEL, 0), _x(proj, D_MODEL, 3)], [p["o_gain"]], [],
                  [(D_MODEL, D_MODEL, BF16)], "gdn_post_fwd")[0]
    return om, (proj, qn, kn, vn, gates, states, o)


def _gdn_bwd(dom, y, saved, p):
    proj, qn, kn, vn, gates, states, o = saved
    nb = D_MODEL // LANE
    do, dgate, dgain = _row_bwd(_gdn_post_fn, [_x(o, D_MODEL, 0), _x(proj, D_MODEL, 3)], [p["o_gain"]], [],
                                [_x(dom, D_MODEL)], [(D_MODEL, F32), (D_MODEL, BF16)], "gdn_post_bwd")
    dqn, dkn, dvn, dgates = _gdn_scan_bwd(qn, kn, vn, gates, states, do)
    dsm, dalog, ddtb = _row_bwd(_gdn_gates_fn, [_x(proj, LANE, 4 * nb)], [p["a_log"], p["dt_bias"]], [],
                                [_x(dgates, LANE)], [(LANE, BF16)], "gdn_gates_bwd")
    dxq, dwq = _gdn_conv_bwd(proj, p["conv_q"], "q", 0, dqn)
    dxk, dwk = _gdn_conv_bwd(proj, p["conv_k"], "k", 1, dkn)
    dxv, dwv = _gdn_conv_bwd(proj, p["conv_v"], "v", 2, dvn)
    dproj = jnp.concatenate([dxq, dxk, dxv, dgate, dsm], axis=1)
    grads = {
        "w_in": _matmul(y, dproj, "tn", F32, "gdn_in_dw")[:, :GDN_IN],
        "conv_w": jnp.concatenate([dwq[:GDN_CONV], dwk[:GDN_CONV], dwv[:GDN_CONV]], axis=1),
        "a_log": dalog[0, :GDN_HEADS], "dt_bias": ddtb[0, :GDN_HEADS], "o_gain": dgain[0],
    }
    return dproj, grads


_MIXERS = ((_fox_fwd, _fox_bwd), (_gla_fwd, _gla_bwd), (_gdn_fwd, _gdn_bwd))


def _mixer_params(i, wts, rep):
    kind, j = i % 3, i // 3
    if kind == 0:
        return {
            "w_in": _pad_cols(wts["fox_w_in"], FOX_IN_PAD), "w_out": wts["fox_w_out"],
            "q_gain": jnp.tile(rep["fox_q_gain"][j], FOX_HEADS)[None],
            "k_gain": jnp.tile(rep["fox_k_gain"][j], FOX_HEADS)[None],
            "b_f": _pad_lanes(rep["fox_b_f"][j][None]),
        }
    if kind == 1:
        return {
            "w_in": _pad_cols(wts["gla_w_in"], GLA_IN_PAD), "w_out": wts["gla_w_out"],
            "w_alpha": jnp.pad(wts["gla_w_alpha2"], ((0, LANE - GLA_RANK), (0, 0))).astype(F32),
            "b_alpha": rep["gla_b_alpha"][j][None], "o_gain": rep["gla_o_gain"][j][None],
        }
    conv = jnp.pad(wts["gdn_conv_w"].astype(F32), ((0, 8 - GDN_CONV), (0, 0)))
    return {
        "w_in": _pad_cols(wts["gdn_w_in"], GDN_IN_PAD), "w_out": wts["gdn_w_out"],
        "conv_q": conv[:, :D_MODEL], "conv_k": conv[:, D_MODEL:2 * D_MODEL], "conv_v": conv[:, 2 * D_MODEL:],
        "a_log": _pad_lanes(rep["gdn_a_log"][j][None]), "dt_bias": _pad_lanes(rep["gdn_dt_bias"][j][None]),
        "o_gain": rep["gdn_o_gain"][j][None],
    }


def kernel(x, meta_tokens, norm_mix, norm_ffn, w_gate_up, w_down, fox_w_in, fox_b_f, fox_q_gain, fox_k_gain, fox_w_out, gla_w_in, gla_w_alpha2, gla_b_alpha, gla_o_gain, gla_w_out, gdn_w_in, gdn_conv_w, gdn_a_log, gdn_dt_bias, gdn_o_gain, gdn_w_out, loss_target, m_meta_tokens, m_norm_mix, m_norm_ffn, m_w_gate_up, m_w_down, m_fox_w_in, m_fox_b_f, m_fox_q_gain, m_fox_k_gain, m_fox_w_out, m_gla_w_in, m_gla_w_alpha2, m_gla_b_alpha, m_gla_o_gain, m_gla_w_out, m_gdn_w_in, m_gdn_conv_w, m_gdn_a_log, m_gdn_dt_bias, m_gdn_o_gain, m_gdn_w_out, v_meta_tokens, v_norm_mix, v_norm_ffn, v_w_gate_up, v_w_down, v_fox_w_in, v_fox_b_f, v_fox_q_gain, v_fox_k_gain, v_fox_w_out, v_gla_w_in, v_gla_w_alpha2, v_gla_b_alpha, v_gla_o_gain, v_gla_w_out, v_gdn_w_in, v_gdn_conv_w, v_gdn_a_log, v_gdn_dt_bias, v_gdn_o_gain, v_gdn_w_out):
    given = dict(locals())
    w_loc = {n: given[n] for n in WEIGHT_ORDER}
    m_loc = {n: given["m_" + n] for n in WEIGHT_ORDER}
    v_loc = {n: given["v_" + n] for n in WEIGHT_ORDER}

    rep = {n: w_loc[n] for n, _ in REPLICATED}
    me = 4 * lax.axis_index("x") + 2 * lax.axis_index("y") + lax.axis_index("c")

    def with_own(lands, owns):
        return [lax.dynamic_update_slice(land, own[None], (me, 0, 0)) for land, own in zip(lands, owns)]

    gathers = {}
    order = jnp.zeros((), F32)
    for i in range(DEPTH):
        groups = _layer_tensors(i)
        for part in PARTS:
            blocks = [(_layer_block(w_loc[n], idx) + order).astype(_wire(n)) for n, idx in groups[part]]
            started = _send_start(blocks, True, "gather_start_%d_%s" % (i, part))
            gathers[(i, part)] = (groups[part], blocks, started)
            order = started[4][0, 0]
    all_started = gathers[(DEPTH - 1, PARTS[-1])][2][4]

    def weights(i, part, after):
        tensors, blocks, started = gathers[(i, part)]
        lands = with_own(_send_wait(started, True, after, "gather_wait_%d_%s" % (i, part)), blocks)
        return {n: _from_gathered(g, n) for (n, _), g in zip(tensors, lands)}

    h = y = None
    saved = []
    for i in range(DEPTH):
        w_mix = weights(i, "mix", all_started if i == 0 else h)
        if i == 0:
            h = jnp.concatenate([jnp.zeros((FIRST_ROW, D_MODEL), F32), w_mix["meta_tokens"], x[0]], axis=0)
            y = _rms_fwd(h, rep["norm_mix"][0][None])
        p = _mixer_params(i, w_mix, rep)
        fwd, _ = _MIXERS[i % 3]
        om, mix_saved = fwd(y, p)
        h_mid, y_mid = _matmul_add_norm(om, p["w_out"], h, rep["norm_ffn"][i][None], "mixer_out_norm")
        w_ffn = weights(i, "ffn", h_mid)
        g_next = rep["norm_mix"][i + 1][None] if i + 1 < DEPTH else None
        h_out, y_next, ffn_saved = _ffn_fwd(h_mid, y_mid, w_ffn["w_gate_up"], w_ffn["w_down"], g_next)
        saved.append((p, w_ffn, h, y, om, mix_saved, ffn_saved))
        h, y = h_out, y_next

    dh, sq = _loss_head(h, loss_target[0])
    loss = lax.psum(0.5 * jnp.sum(sq) * (1.0 / D_MODEL), ("x", "y", "c"))

    gw = {n: [None] * DEPTH for n in ("norm_mix", "norm_ffn")}
    gm = {}
    reduces = {}

    def start_reduce(i, part, full, extra=()):
        tensors = _layer_tensors(i)[part]
        slabs = [_to_slabs(full[n], n).astype(_wire(n)) for n, _ in tensors] + list(extra)
        started = _send_start(slabs, False, "reduce_start_%d_%s" % (i, part))
        reduces[(i, part)] = (tensors, slabs, started)
        return started[4][0, 0]

    order = jnp.zeros((), BF16)
    for i in reversed(range(DEPTH)):
        p, w_ffn, h_in, y, om, mix_saved, ffn_saved = saved[i]
        kind, j = i % 3, i // 3
        pre = ("fox", "gla", "gdn")[kind]
        _, bwd = _MIXERS[kind]
        dh_mid, gw["norm_ffn"][i], dwgu, dwd = _ffn_bwd(
            dh, ffn_saved, rep["norm_ffn"][i][None], w_ffn["w_gate_up"], w_ffn["w_down"] + order)
        order = start_reduce(i, "ffn", {"w_gate_up": dwgu, "w_down": dwd}).astype(BF16)
        dom = _matmul(dh_mid, p["w_out"] + order, "nt", F32, "mixer_out_dx")
        dwo = _matmul(om, dh_mid, "tn", F32, "mixer_out_dw")
        dproj, grads = bwd(dom, y, mix_saved, p)
        gm[(kind, j)] = grads
        dh, gw["norm_mix"][i] = _matmul_norm_bwd(dproj, p["w_in"], 0, h_in, rep["norm_mix"][i][None], dh_mid,
                                                 "mixer_in_dx_norm")
        mix_full = {pre + "_w_in": grads["w_in"], pre + "_w_out": dwo, "gla_w_alpha2": grads.get("w_alpha"),
                    "gdn_conv_w": grads.get("conv_w"), "meta_tokens": dh[FIRST_ROW:PAD_ROWS]}
        if i > 0:
            order = start_reduce(i, "mix", mix_full).astype(BF16)

    partial = {
        "norm_mix": jnp.concatenate(gw["norm_mix"], axis=0), "norm_ffn": jnp.concatenate(gw["norm_ffn"], axis=0),
        "fox_b_f": jnp.stack([gm[(0, 0)]["b_f"], gm[(0, 1)]["b_f"]]),
        "fox_q_gain": jnp.stack([gm[(0, 0)]["q_gain"], gm[(0, 1)]["q_gain"]]),
        "fox_k_gain": jnp.stack([gm[(0, 0)]["k_gain"], gm[(0, 1)]["k_gain"]]),
        "gla_b_alpha": gm[(1, 0)]["b_alpha"][None], "gla_o_gain": gm[(1, 0)]["o_gain"][None],
        "gdn_a_log": gm[(2, 0)]["a_log"][None], "gdn_dt_bias": gm[(2, 0)]["dt_bias"][None],
        "gdn_o_gain": gm[(2, 0)]["o_gain"][None],
    }

    small = _pack([partial[n].reshape(-1) for n, _ in REPLICATED], 8, F32)
    start_reduce(0, "mix", mix_full, extra=[jnp.broadcast_to(small[None], (N_DEV,) + small.shape)])
    recv = {}

    def wait_reduce(key, after):
        tensors, slabs, started = reduces[key]
        owns = [lax.dynamic_index_in_dim(s, me, 0, keepdims=False) for s in slabs]
        lands = with_own(_send_wait(started, False, after, "reduce_wait_%d_%s" % key), owns)
        for (n, idx), land in zip(tensors, lands):
            recv[(n, idx)] = land
        return lands

    def adamw_tensor(n, s):
        layers = [idx for i in range(DEPTH) for part in PARTS for (nn, idx) in _layer_tensors(i)[part] if nn == n]
        layers = sorted(layers, key=lambda idx: -1 if idx is None else idx)
        outs = _adamw([recv[(n, idx)] for idx in layers], *[_local_2d(d[n]) for d in (w_loc, m_loc, v_loc)],
                      "adamw_" + n)
        for k in range(4):
            res[k][n] = outs[k].reshape(s)
        return outs[0]

    last = (0, "mix")
    for key in reduces:
        if key != last:
            wait_reduce(key, dh)
    res = [{} for _ in range(4)]
    late = {n for n, _ in reduces[last][0]}
    done = dh
    for n, s, _ in SHARDED:
        if n not in late:
            done = adamw_tensor(n, s)
    recv_small = wait_reduce(last, done)[-1]
    for n, s, _ in SHARDED:
        if n in late:
            adamw_tensor(n, s)
    outs_small = _adamw([recv_small], *[_pack([d[n].reshape(-1) for n, _ in REPLICATED], 8, F32)
                                        for d in (w_loc, m_loc, v_loc)], "adamw_replicated")
    for k in range(4):
        res[k].update(_unpack(outs_small[k], list(REPLICATED)))

    result = [loss, dh[PAD_ROWS:][None]]
    for k in range(4):
        result += [res[k][n] for n in WEIGHT_ORDER]
    return tuple(result)
```

```python
import functools

import jax
import jax.numpy as jnp
from jax import lax
from jax.experimental import pallas as pl
from jax.experimental.pallas import tpu as pltpu

F32, BF16 = jnp.float32, jnp.bfloat16
HIGHEST = lax.Precision.HIGHEST

D_MODEL = 1024
SEQ = 4096
N_META = 16
DEPTH = 4
NORM_EPS = 1e-6
LANE = 128
PAD_ROWS = 128
FIRST_ROW = PAD_ROWS - N_META
ROWS = PAD_ROWS + SEQ
ROW_TILE = 128
CHUNK = 64
NEG = -1e30
N_DEV = 8
VMEM_LIMIT = 56 * 1024 * 1024
VMEM_BLOCK_BUDGET = 36 * 1024 * 1024

FOX_HEADS, FOX_DH = 16, 64
FOX_IN, FOX_IN_PAD = 4 * D_MODEL + FOX_HEADS, 4 * D_MODEL + LANE
GLA_HEADS, GLA_DK, GLA_DV, GLA_RANK = 4, 128, 256, 16
GLA_QK, GLA_V = GLA_HEADS * GLA_DK, GLA_HEADS * GLA_DV
GLA_IN, GLA_IN_PAD = 2 * GLA_QK + 2 * GLA_V + GLA_RANK, 2 * GLA_QK + 2 * GLA_V + LANE
GDN_HEADS, GDN_DK, GDN_DV, GDN_CONV = 8, 128, 128, 4
GDN_CONV_DIM = 3 * GDN_HEADS * GDN_DK
GDN_IN, GDN_IN_PAD = GDN_CONV_DIM + GDN_HEADS * GDN_DV + 2 * GDN_HEADS, GDN_CONV_DIM + GDN_HEADS * GDN_DV + LANE
D_FF = 2816

ADAM_LR, ADAM_B1, ADAM_B2, ADAM_EPS, ADAM_WD, ADAM_STEP = 0.001, 0.9, 0.999, 1e-08, 0.01, 10

PACK_COLS = 1024

SHARDED = (
    ("meta_tokens", (16, 128), 1),
    ("w_gate_up", (4, 1024, 704), 2),
    ("w_down", (4, 352, 1024), 1),
    ("fox_w_in", (2, 1024, 514), 2),
    ("fox_w_out", (2, 128, 1024), 1),
    ("gla_w_in", (1, 1024, 386), 2),
    ("gla_w_alpha2", (1, 16, 64), 2),
    ("gla_w_out", (1, 128, 1024), 1),
    ("gdn_w_in", (1, 1024, 514), 2),
    ("gdn_conv_w", (1, 4, 1, 384), 3),
    ("gdn_w_out", (1, 128, 1024), 1),
)
REPLICATED = (
    ("norm_mix", (4, 1024)),
    ("norm_ffn", (4, 1024)),
    ("fox_b_f", (2, 16)),
    ("fox_q_gain", (2, 64)),
    ("fox_k_gain", (2, 64)),
    ("gla_b_alpha", (1, 512)),
    ("gla_o_gain", (1, 256)),
    ("gdn_a_log", (1, 8)),
    ("gdn_dt_bias", (1, 8)),
    ("gdn_o_gain", (1, 128)),
)
WEIGHT_ORDER = (
    "meta_tokens", "norm_mix", "norm_ffn", "w_gate_up", "w_down", "fox_w_in", "fox_b_f", "fox_q_gain",
    "fox_k_gain", "fox_w_out", "gla_w_in", "gla_w_alpha2", "gla_b_alpha", "gla_o_gain", "gla_w_out",
    "gdn_w_in", "gdn_conv_w", "gdn_a_log", "gdn_dt_bias", "gdn_o_gain", "gdn_w_out",
)

_DN = {
    "nn": (((1,), (0,)), ((), ())),
    "nt": (((1,), (1,)), ((), ())),
    "tn": (((0,), (0,)), ((), ())),
}


def _dot(a, b, kind):
    return lax.dot_general(a.astype(BF16), b.astype(BF16), _DN[kind], preferred_element_type=F32)


@functools.partial(jax.custom_vjp, nondiff_argnums=(2,))
def _dot_ad(a, b, kind):
    return _dot(a, b, kind)


def _dot_ad_fwd(a, b, kind):
    return _dot(a, b, kind), (a, b)


def _dot_ad_bwd(kind, res, ct):
    a, b = res
    if kind == "nn":
        return _dot(ct, b, "nt"), _dot(a, ct, "tn")
    if kind == "nt":
        return _dot(ct, b, "nn"), _dot(ct, a, "tn")
    return _dot(b, ct, "nt"), _dot(a, ct, "nn")


_dot_ad.defvjp(_dot_ad_fwd, _dot_ad_bwd)


def _dot_hi(a, b, kind):
    return lax.dot_general(a, b, _DN[kind], precision=HIGHEST, preferred_element_type=F32)


def _dot_mid(a, b, kind):
    return lax.dot_general(a, b, _DN[kind], precision=lax.Precision.HIGH, preferred_element_type=F32)


def _iota(shape, dim):
    return lax.broadcasted_iota(jnp.int32, shape, dim)


def _row_mask(i, tm):
    return ((i * tm + _iota((tm, 1), 0)) >= FIRST_ROW).astype(F32)


def _log_sigmoid(z):
    return jnp.minimum(z, 0.0) - jnp.log(1.0 + jnp.exp(-jnp.abs(z)))


def _softplus(z):
    return jnp.maximum(z, 0.0) + jnp.log(1.0 + jnp.exp(-jnp.abs(z)))


def _silu(z):
    return z * jax.nn.sigmoid(z)


def _lane_col(x, lane):
    return jnp.sum(jnp.where(_iota(x.shape, 1) == lane, x, 0.0), axis=1, keepdims=True)


def _last_row(x):
    return jnp.sum(jnp.where(_iota(x.shape, 0) == x.shape[0] - 1, x, 0.0), axis=0, keepdims=True)


def _params(sem, limit=VMEM_LIMIT):
    return pltpu.CompilerParams(dimension_semantics=sem, vmem_limit_bytes=limit)


def _pick(n, options):
    for t in options:
        if n % t == 0:
            return t
    return n


def _matmul(a, b, kind, out_dtype, name, add=None, b_kblock=0):
    if kind == "nn":
        (m, k), n = a.shape, b.shape[1]
    elif kind == "nt":
        (m, k), n = a.shape, b.shape[0]
    else:
        (k, m), n = a.shape, b.shape[1]
    tn = _pick(n, (512, 384, 640, 256, 128))
    tm_options = (1408, 384, 128) if kind != "tn" else (1024, 512, 256, 128)
    tm = None
    for t in tm_options:
        if m % t:
            continue
        need = 2 * (t * k * a.dtype.itemsize + tn * k * b.dtype.itemsize + t * tn * 8)
        if need <= VMEM_BLOCK_BUDGET:
            tm = t
            break
    assert tm is not None, (name, a.shape, b.shape)
    a_bytes, b_bytes = m * k * a.dtype.itemsize, n * k * b.dtype.itemsize
    m_outer = a_bytes + (m // tm) * b_bytes <= b_bytes + (n // tn) * a_bytes

    def ij(fn):
        return (lambda i, j: fn(i, j)) if m_outer else (lambda j, i: fn(i, j))

    def body(*refs):
        a_ref, b_ref = refs[0], refs[1]
        o_ref = refs[-1]
        acc = _dot(a_ref[...], b_ref[...], kind)
        if add is not None:
            acc = acc + refs[2][...]
        o_ref[...] = acc.astype(o_ref.dtype)

    if kind == "tn":
        a_spec = pl.BlockSpec((k, tm), ij(lambda i, j: (0, i)))
    else:
        a_spec = pl.BlockSpec((tm, k), ij(lambda i, j: (i, 0)))
    if kind == "nt":
        b_spec = pl.BlockSpec((tn, k), ij(lambda i, j: (j, b_kblock)))
    else:
        b_spec = pl.BlockSpec((k, tn), ij(lambda i, j: (0, j)))
    o_spec = pl.BlockSpec((tm, tn), ij(lambda i, j: (i, j)))
    operands, in_specs = [a, b], [a_spec, b_spec]
    if add is not None:
        operands.append(add)
        in_specs.append(o_spec)
    grid = (m // tm, n // tn) if m_outer else (n // tn, m // tm)
    return pl.pallas_call(
        body, grid=grid, in_specs=in_specs, out_specs=o_spec,
        out_shape=jax.ShapeDtypeStruct((m, n), out_dtype), name=name,
        compiler_params=_params(("parallel", "parallel")),
    )(*operands)


def _x_spec(tm, width, cb0, moves):
    return pl.BlockSpec((tm, width), lambda i, j: (i, cb0 + j * moves))


def _full_spec(arr):
    nd = arr.ndim
    return pl.BlockSpec(arr.shape, lambda i, j: (0,) * nd)


def _row_fwd(fn, xs, ps, cs, outs, name, tm=ROW_TILE, ncol=1, rows=None):
    rows = ROWS if rows is None else rows
    nx, npar, nc = len(xs), len(ps), len(cs)

    def body(*refs):
        i = pl.program_id(0)
        xv = [r[...].astype(F32) for r in refs[:nx]]
        pv = [r[...] for r in refs[nx:nx + npar + nc]]
        res = fn(i, *xv, *pv)
        for r, val in zip(refs[nx + npar + nc:], res):
            r[...] = val.astype(r.dtype)

    in_specs = [_x_spec(tm, w, cb0, mv) for (_, w, cb0, mv) in xs]
    in_specs += [_full_spec(p) for p in list(ps) + list(cs)]
    out_specs = [_x_spec(tm, w, 0, 1) for (w, _, _) in outs]
    out_shape = [jax.ShapeDtypeStruct((rows, tot), dt) for (_, tot, dt) in outs]
    return pl.pallas_call(
        body, grid=(rows // tm, ncol), in_specs=in_specs, out_specs=out_specs, out_shape=out_shape,
        name=name, compiler_params=_params(("parallel", "parallel")),
    )(*[x[0] for x in xs], *ps, *cs)


def _row_bwd(fn, xs, ps, cs, cts, dxs, name, adds=None, tm=ROW_TILE, ncol=1, rows=None):
    rows = ROWS if rows is None else rows
    nx, npar, nc, nct = len(xs), len(ps), len(cs), len(cts)
    adds = [None] * nx if adds is None else adds
    add_idx = [k for k in range(nx) if adds[k] is not None]

    def body(*refs):
        i, j = pl.program_id(0), pl.program_id(1)
        pos = 0
        x_refs = refs[pos:pos + nx]; pos += nx
        p_refs = refs[pos:pos + npar]; pos += npar
        c_refs = refs[pos:pos + nc]; pos += nc
        ct_refs = refs[pos:pos + nct]; pos += nct
        add_refs = refs[pos:pos + len(add_idx)]; pos += len(add_idx)
        dx_refs = refs[pos:pos + nx]; pos += nx
        dp_refs = refs[pos:pos + npar]
        xv = [r[...].astype(F32) for r in x_refs]
        pv = [r[...] for r in p_refs]
        cv = [r[...] for r in c_refs]
        _, vjp = jax.vjp(lambda *args: fn(i, *args, *cv), *xv, *pv)
        grads = vjp(tuple(r[...].astype(F32) for r in ct_refs))
        for k in range(nx):
            val = grads[k]
            if adds[k] is not None:
                val = val + add_refs[add_idx.index(k)][...]
            dx_refs[k][...] = val.astype(dx_refs[k].dtype)
        if npar:
            @pl.when((i == 0) & (j == 0))
            def _():
                for r in dp_refs:
                    r[...] = jnp.zeros(r.shape, r.dtype)

            for k in range(npar):
                dp_refs[k][...] += grads[nx + k]

    in_specs = [_x_spec(tm, w, cb0, mv) for (_, w, cb0, mv) in xs]
    in_specs += [_full_spec(p) for p in list(ps) + list(cs)]
    in_specs += [_x_spec(tm, w, cb0, mv) for (_, w, cb0, mv) in cts]
    in_specs += [_x_spec(tm, xs[k][1], 0, 1) for k in add_idx]
    out_specs = [_x_spec(tm, xs[k][1], 0, 1) for k in range(nx)] + [_full_spec(p) for p in ps]
    out_shape = [jax.ShapeDtypeStruct((rows, tot), dt) for (tot, dt) in dxs]
    out_shape += [jax.ShapeDtypeStruct(p.shape, F32) for p in ps]
    sem = ("arbitrary", "arbitrary") if npar else ("parallel", "parallel")
    return pl.pallas_call(
        body, grid=(rows // tm, ncol), in_specs=in_specs, out_specs=out_specs, out_shape=out_shape,
        name=name, compiler_params=_params(sem),
    )(*[x[0] for x in xs], *ps, *cs, *[c[0] for c in cts], *[adds[k] for k in add_idx])


def _rms_fn(i, h, g):
    return (h * lax.rsqrt(jnp.mean(h * h, axis=1, keepdims=True) + NORM_EPS) * g,)


def _fox_pre_fn(i, q, k, f, qg, kg, bf, e_mean, e_expand):
    tm = q.shape[0]

    def headnorm(x, gain, scale):
        msq = _dot_mid(x * x, e_mean, "nn")
        r = _dot_mid(lax.rsqrt(msq + NORM_EPS), e_expand, "nn")
        return x * r * gain * scale

    qn = headnorm(q, qg, FOX_DH ** -0.5)
    kn = headnorm(k, kg, 1.0)
    keep = (_iota((tm, LANE), 1) < FOX_HEADS).astype(F32) * _row_mask(i, tm)
    return qn, kn, _log_sigmoid(f + bf) * keep


def _fox_post_fn(i, o, gate):
    return (o * jax.nn.sigmoid(gate) * _row_mask(i, o.shape[0]),)


def _gla_pre_fn(i, alr, wa, ba):
    z = _dot_ad(alr, wa, "nn") + ba
    return (_log_sigmoid(z) * (1.0 / 16.0) * _row_mask(i, alr.shape[0]),)


def _headwise_norm_gate(o, gate, gain, dv, mask):
    pieces = []
    for h in range(o.shape[1] // dv):
        oh = o[:, h * dv:(h + 1) * dv]
        pieces.append(oh * lax.rsqrt(jnp.mean(oh * oh, axis=1, keepdims=True) + NORM_EPS) * gain)
    return jnp.concatenate(pieces, axis=1) * _silu(gate) * mask


def _gla_post_fn(i, o, r, gain):
    return (_headwise_norm_gate(o, r, gain, GLA_DV, _row_mask(i, o.shape[0])),)


def _gdn_post_fn(i, o, gate, gain):
    return (_headwise_norm_gate(o, gate, gain, GDN_DV, _row_mask(i, o.shape[0])),)


def _gdn_gates_fn(i, sm, alog, dtb):
    tm = sm.shape[0]
    lane = _iota((tm, LANE), 1)
    g = -jnp.exp(alog) * _softplus(sm + dtb)
    beta = jax.nn.sigmoid(sm)
    out = jnp.where(lane < GDN_HEADS, g, jnp.where(lane < 2 * GDN_HEADS, beta, 0.0))
    return (out * _row_mask(i, tm),)


@functools.partial(jax.custom_vjp, nondiff_argnums=(1,))
def _window_rows(xw, start):
    n = xw.shape[0]
    return pltpu.roll(xw, n - start, 0)[:n // 2]


def _window_rows_fwd(xw, start):
    return _window_rows(xw, start), None


def _window_rows_bwd(start, _, ct):
    return (pltpu.roll(jnp.concatenate([ct, jnp.zeros_like(ct)], axis=0), start, 0),)


_window_rows.defvjp(_window_rows_fwd, _window_rows_bwd)


def _gdn_conv_fn(kind):
    def fn(i, prev, cur, w):
        tm = cur.shape[0]
        xw = jnp.concatenate([prev * jnp.where(i > 0, 1.0, 0.0), cur], axis=0)
        y = jnp.zeros(cur.shape, F32)
        for j in range(GDN_CONV):
            tap = jnp.sum(jnp.where(_iota(w.shape, 0) == j, w, 0.0), axis=0, keepdims=True)
            y = y + _window_rows(xw, tm - (GDN_CONV - 1) + j) * tap
        a = _silu(y * _row_mask(i, tm))
        if kind == "v":
            return (a,)
        scale = GDN_DK ** -0.5 if kind == "q" else 1.0
        pieces = []
        for h in range(GDN_HEADS):
            ah = a[:, h * GDN_DK:(h + 1) * GDN_DK]
            pieces.append(ah * lax.rsqrt(jnp.sum(ah * ah, axis=1, keepdims=True) + NORM_EPS) * scale)
        return (jnp.concatenate(pieces, axis=1),)

    return fn


def _seq_cumsum(x, reverse, name, pairs=None):
    rows, width = x.shape
    nt = rows // ROW_TILE

    def body(*refs):
        x_ref, o_ref, carry = refs[0], refs[-2], refs[-1]

        @pl.when(pl.program_id(0) == 0)
        def _():
            carry[...] = jnp.zeros(carry.shape, F32)

        xv = x_ref[...]
        if pairs is not None:
            lane = _iota(xv.shape, 1)
            for g in range(pairs.shape[0]):
                ta, tb = _pair_cols(refs[1][g])
                xv = xv + jnp.where(lane == 2 * g, ta, 0.0) + jnp.where(lane == 2 * g + 1, tb, 0.0)
        r = _iota((ROW_TILE, ROW_TILE), 0)
        c = _iota((ROW_TILE, ROW_TILE), 1)
        tri = ((c >= r) if reverse else (c <= r)).astype(F32)
        acc = _dot_hi(tri, xv, "nn") + carry[...]
        o_ref[...] = acc
        edge = 0 if reverse else ROW_TILE - 1
        carry[...] = jnp.sum(jnp.where(_iota(acc.shape, 0) == edge, acc, 0.0), axis=0, keepdims=True)

    idx = (lambda i: (nt - 1 - i, 0)) if reverse else (lambda i: (i, 0))
    idx3 = (lambda i: (0, nt - 1 - i, 0)) if reverse else (lambda i: (0, i, 0))
    operands, in_specs = [x], [pl.BlockSpec((ROW_TILE, width), idx)]
    if pairs is not None:
        operands.append(pairs)
        in_specs.append(pl.BlockSpec((pairs.shape[0], ROW_TILE, LANE), idx3))
    return pl.pallas_call(
        body, grid=(nt,), in_specs=in_specs,
        out_specs=pl.BlockSpec((ROW_TILE, width), idx), out_shape=jax.ShapeDtypeStruct(x.shape, F32),
        scratch_shapes=[pltpu.VMEM((1, width), F32)], name=name, compiler_params=_params(("arbitrary",)),
    )(*operands)


FOX_PAIRS = FOX_HEADS // 2
FOX_V_BLOCK = 2 * D_MODEL // LANE


def _pair_cols(x):
    return _lane_col(x, 0), _lane_col(x, FOX_DH)


ATT_TILE = 384
assert ROWS % ATT_TILE == 0
N_ATT = ROWS // ATT_TILE
KEY_OFF = 1e30


def _fox_scores(qa, qb, kj, ca, cb, cra, crb, mask):
    sa = _dot(qa, kj, "nt") + (ca - cra)
    sb = _dot(qb, kj, "nt") + (cb - crb)
    if mask is None:
        return sa, sb
    return jnp.where(mask, sa, NEG), jnp.where(mask, sb, NEG)


def _fox_diag_mask():
    return _iota((ATT_TILE, ATT_TILE), 1) <= _iota((ATT_TILE, ATT_TILE), 0)


def _att_off(j):
    return pl.multiple_of(j * ATT_TILE, ATT_TILE)


def _fox_attn_fwd(qn, kn, proj, c2, crow):
    def body(q_ref, k_ref, v_ref, c2_ref, crow_ref, o_ref, lse_ref):
        i = pl.program_id(1)
        q = q_ref[...]
        is_a = _iota((ATT_TILE, LANE), 1) < FOX_DH
        qa, qb = jnp.where(is_a, q, 0), jnp.where(is_a, 0, q)
        ca, cb = _pair_cols(c2_ref[0])

        def block(off, size, mask, carry):
            ma, la, acca, mb, lb, accb = carry
            kj = k_ref[pl.ds(off, size), :]
            vj = v_ref[pl.ds(off, size), :].astype(BF16)
            sa, sb = _fox_scores(qa, qb, kj, ca, cb, crow_ref[0, 0:1, pl.ds(off, size)],
                                 crow_ref[0, 1:2, pl.ds(off, size)], mask)

            def online(s, m, l, acc):
                m_new = jnp.maximum(m, jnp.max(s, axis=1, keepdims=True))
                p = jnp.exp(s - m_new)
                alpha = jnp.exp(m - m_new)
                return m_new, alpha * l + jnp.sum(p, axis=1, keepdims=True), alpha * acc + _dot(p, vj, "nn")

            return online(sa, ma, la, acca) + online(sb, mb, lb, accb)

        m0 = jnp.full((ATT_TILE, 1), NEG, F32)
        l0 = jnp.zeros((ATT_TILE, 1), F32)
        a0 = jnp.zeros((ATT_TILE, LANE), F32)
        carry = lax.fori_loop(0, i, lambda j, cr: block(_att_off(j), ATT_TILE, None, cr), (m0, l0, a0, m0, l0, a0))
        ma, la, acca, mb, lb, accb = block(_att_off(i), ATT_TILE, _fox_diag_mask(), carry)
        o_ref[...] = jnp.where(is_a, acca / la, accb / lb)
        lse_ref[0] = jnp.where(is_a, ma + jnp.log(la), mb + jnp.log(lb))

    return pl.pallas_call(
        body, grid=(FOX_PAIRS, N_ATT),
        in_specs=[
            pl.BlockSpec((ATT_TILE, LANE), lambda g, i: (i, g)),
            pl.BlockSpec((ROWS, LANE), lambda g, i: (0, g)),
            pl.BlockSpec((ROWS, LANE), lambda g, i: (0, FOX_V_BLOCK + g)),
            pl.BlockSpec((1, ATT_TILE, LANE), lambda g, i: (g, i, 0)),
            pl.BlockSpec((1, 2, ROWS), lambda g, i: (g, 0, 0)),
        ],
        out_specs=[
            pl.BlockSpec((ATT_TILE, LANE), lambda g, i: (i, g)),
            pl.BlockSpec((1, ATT_TILE, LANE), lambda g, i: (g, i, 0)),
        ],
        out_shape=[jax.ShapeDtypeStruct((ROWS, D_MODEL), F32), jax.ShapeDtypeStruct((FOX_PAIRS, ROWS, LANE), F32)],
        name="fox_attn_fwd", compiler_params=_params(("parallel", "arbitrary")),
    )(qn, kn, proj, c2, crow)


def _fox_attn_bwd(qn, kn, proj, c2, crow, lse2, o, do, blockdiag):
    def body(q_ref, k_ref, v_ref, c2_ref, crow_ref, lse_ref, o_ref, do_ref, bd_ref,
             dq_ref, dk_ref, dv_ref, dc_ref, dcq_ref, delta_ref):
        j = pl.program_id(1)

        @pl.when(j == 0)
        def _():
            for t in range(N_ATT):
                rows = pl.ds(t * ATT_TILE, ATT_TILE)
                delta_ref[rows, :] = _dot_hi(do_ref[rows, :] * o_ref[rows, :], bd_ref[...], "nn")
            dq_ref[...] = jnp.zeros(dq_ref.shape, F32)
            dcq_ref[...] = jnp.zeros(dcq_ref.shape, F32)

        kj = k_ref[...]
        vj = v_ref[...].astype(BF16)
        cra, crb = crow_ref[0, 0:1, :], crow_ref[0, 1:2, :]
        is_a = _iota((ATT_TILE, LANE), 1) < FOX_DH

        def block(off, mask, carry):
            dk, dv, dca, dcb = carry
            rows = pl.ds(off, ATT_TILE)
            q = q_ref[rows, :]
            qa, qb = jnp.where(is_a, q, 0), jnp.where(is_a, 0, q)
            do_t = do_ref[rows, :]
            doa, dob = jnp.where(is_a, do_t, 0.0).astype(BF16), jnp.where(is_a, 0.0, do_t).astype(BF16)
            ca, cb = _pair_cols(c2_ref[0, rows, :])
            lsa, lsb = _pair_cols(lse_ref[0, rows, :])
            dla, dlb = _pair_cols(delta_ref[rows, :])
            sa, sb = _fox_scores(qa, qb, kj, ca, cb, cra, crb, mask)
            pa, pb = jnp.exp(sa - lsa), jnp.exp(sb - lsb)
            dsa = pa * (_dot(doa, vj, "nt") - dla)
            dsb = pb * (_dot(dob, vj, "nt") - dlb)
            dq_ref[rows, :] += jnp.where(is_a, _dot(dsa, kj, "nn"), _dot(dsb, kj, "nn"))
            dcq_ref[0, rows, :] += jnp.where(is_a, jnp.sum(dsa, axis=1, keepdims=True),
                                             jnp.sum(dsb, axis=1, keepdims=True))
            dv = dv + _dot(pa, doa, "tn") + _dot(pb, dob, "tn")
            dk = dk + _dot(dsa, qa, "tn") + _dot(dsb, qb, "tn")
            dca = dca - jnp.sum(dsa, axis=0, keepdims=True)
            dcb = dcb - jnp.sum(dsb, axis=0, keepdims=True)
            return dk, dv, dca, dcb

        z = jnp.zeros((ATT_TILE, LANE), F32)
        zr = jnp.zeros((1, ATT_TILE), F32)
        carry = block(_att_off(j), _fox_diag_mask(), (z, z, zr, zr))
        dk, dv, dca, dcb = lax.fori_loop(j + 1, N_ATT, lambda i, cr: block(_att_off(i), None, cr), carry)
        dk_ref[...] = dk
        dv_ref[...] = dv.astype(dv_ref.dtype)
        dc_ref[0, 0:1, :] = dca
        dc_ref[0, 1:2, :] = dcb

    whole = pl.BlockSpec((ROWS, LANE), lambda g, j: (0, g))
    pair_whole = pl.BlockSpec((1, ROWS, LANE), lambda g, j: (g, 0, 0))
    tile = pl.BlockSpec((ATT_TILE, LANE), lambda g, j: (j, g))
    key_rows = pl.BlockSpec((1, 2, ATT_TILE), lambda g, j: (g, 0, j))
    return pl.pallas_call(
        body, grid=(FOX_PAIRS, N_ATT),
        in_specs=[
            whole, tile,
            pl.BlockSpec((ATT_TILE, LANE), lambda g, j: (j, FOX_V_BLOCK + g)),
            pair_whole, key_rows, pair_whole, whole, whole,
            pl.BlockSpec((LANE, LANE), lambda g, j: (0, 0)),
        ],
        out_specs=[whole, tile, tile, key_rows, pair_whole],
        out_shape=[jax.ShapeDtypeStruct((ROWS, D_MODEL), F32), jax.ShapeDtypeStruct((ROWS, D_MODEL), F32),
                   jax.ShapeDtypeStruct((ROWS, D_MODEL), BF16), jax.ShapeDtypeStruct((FOX_PAIRS, 2, ROWS), F32),
                   jax.ShapeDtypeStruct((FOX_PAIRS, ROWS, LANE), F32)],
        scratch_shapes=[pltpu.VMEM((ROWS, LANE), F32)],
        name="fox_attn_bwd", compiler_params=_params(("parallel", "arbitrary")),
    )(qn, kn, proj, c2, crow, lse2, o, do, blockdiag)


def _gla_chunk(q, k, v, g, st, dot):
    n = len(q)
    c = q[0].shape[0]
    incl = _iota((c, c), 1) <= _iota((c, c), 0)
    incl_f = incl.astype(F32)
    b = [_dot_hi(incl_f, g[i], "nn") for i in range(n)]
    qe = [q[i] * (GLA_DK ** -0.5) * jnp.exp(b[i]) for i in range(n)]
    ke = [k[i] * jnp.exp(-b[i]) for i in range(n)]
    att = [jnp.where(incl, dot(qe[i], ke[i], "nt"), 0.0) for i in range(n)]
    o_intra = [dot(att[i], v[i], "nn") for i in range(n)]
    o_inter = [dot(qe[i], st[i], "nt") for i in range(n)]
    b_last = [_last_row(b[i]) for i in range(n)]
    kd = [k[i] * jnp.exp(b_last[i] - b[i]) for i in range(n)]
    upd = [dot(v[i], kd[i], "tn") for i in range(n)]
    return ([o_intra[i] + o_inter[i] for i in range(n)],
            [st[i] * jnp.exp(b_last[i]) + upd[i] for i in range(n)])


def _tri_inverse(low):
    c = low[0].shape[0]
    eye = (_iota((c, c), 0) == _iota((c, c), 1)).astype(F32)
    x, p = [eye - l for l in low], list(low)
    span = 2
    while span < c:
        p = [_dot_mid(pi, pi, "nn") for pi in p]
        x = [_dot_mid(xi, eye + pi, "nn") for xi, pi in zip(x, p)]
        span *= 2
    return x


def _solve(low, rhs):
    return [_dot_mid(inv, r, "nn") for inv, r in zip(_tri_inverse(low), rhs)]


@jax.custom_vjp
def _solve_ad(low, rhs):
    return _solve(low, rhs)


def _solve_ad_fwd(low, rhs):
    inv = _tri_inverse(low)
    sol = [_dot_mid(i, r, "nn") for i, r in zip(inv, rhs)]
    return sol, (inv, sol)


def _solve_ad_bwd(res, ct):
    inv, sol = res
    drhs = [_dot_mid(i, c, "tn") for i, c in zip(inv, ct)]
    return [-_dot_mid(d, s, "nt") for d, s in zip(drhs, sol)], drhs


_solve_ad.defvjp(_solve_ad_fwd, _solve_ad_bwd)


def _gdn_chunk(q, k, v, gates, st, heads, dot, solve):
    n = len(q)
    c = q[0].shape[0]
    incl = _iota((c, c), 1) <= _iota((c, c), 0)
    strict = _iota((c, c), 1) < _iota((c, c), 0)
    incl_f = incl.astype(F32)
    upper_f = (_iota((c, c), 0) <= _iota((c, c), 1)).astype(F32)
    gcol = [_lane_col(gates, h) for h in heads]
    beta = [_lane_col(gates, h + GDN_HEADS) for h in heads]
    b = [_dot_hi(incl_f, jnp.broadcast_to(gc, (c, LANE)), "nn") for gc in gcol]
    g_sq = [jnp.broadcast_to(gc, (c, c)) for gc in gcol]
    b_col = [_dot_hi(incl_f, gs, "nn") for gs in g_sq]
    b_row = [_dot_hi(gs, upper_f, "tn") for gs in g_sq]
    diff = [b_col[i] - b_row[i] for i in range(n)]
    kb = [k[i] * beta[i] for i in range(n)]
    vb = [v[i] * beta[i] for i in range(n)]
    low = [dot(kb[i], k[i], "nt") * jnp.exp(jnp.where(strict, diff[i], NEG)) for i in range(n)]
    eb = [jnp.exp(bi) for bi in b]
    sol = solve(low, [jnp.concatenate([vb[i], kb[i] * eb[i]], axis=1) for i in range(n)])
    w_st = [dot(sol[i][:, GDN_DV:], st[i], "nn") for i in range(n)]
    v_new = [sol[i][:, :GDN_DV] - w_st[i] for i in range(n)]
    att = [dot(q[i], k[i], "nt") * jnp.exp(jnp.where(incl, diff[i], NEG)) for i in range(n)]
    o_intra = [dot(att[i], v_new[i], "nn") for i in range(n)]
    o_inter = [dot(q[i] * eb[i], st[i], "nn") for i in range(n)]
    b_last = [_last_row(bi) for bi in b]
    kd = [k[i] * jnp.exp(b_last[i] - b[i]) for i in range(n)]
    upd = [dot(kd[i], v_new[i], "tn") for i in range(n)]
    return ([o_intra[i] + o_inter[i] for i in range(n)],
            [st[i] * jnp.exp(b_last[i]) + upd[i] for i in range(n)])


N_CHUNKS = ROWS // CHUNK
GDN_GROUP = 8


def _head_lanes(width, n):
    return [slice(i * width, (i + 1) * width) for i in range(n)]


def _gla_scan_fwd(proj, garr):
    kl, vl = _head_lanes(GLA_DK, GLA_HEADS), _head_lanes(GLA_DV, GLA_HEADS)

    def body(q_ref, k_ref, v_ref, g_ref, o_ref, states_ref, st):
        @pl.when(pl.program_id(0) == 0)
        def _():
            st[...] = jnp.zeros(st.shape, F32)

        s_in = [st[i] for i in range(GLA_HEADS)]
        o, s_out = _gla_chunk([q_ref[:, ln] for ln in kl], [k_ref[:, ln] for ln in kl], [v_ref[:, ln] for ln in vl],
                              [g_ref[:, ln] for ln in kl], s_in, _dot)
        for i in range(GLA_HEADS):
            states_ref[i, 0] = s_in[i]
            o_ref[:, vl[i]] = o[i]
            st[i] = s_out[i]

    return pl.pallas_call(
        body, grid=(N_CHUNKS,),
        in_specs=[
            pl.BlockSpec((CHUNK, GLA_QK), lambda c: (c, 0)),
            pl.BlockSpec((CHUNK, GLA_QK), lambda c: (c, 1)),
            pl.BlockSpec((CHUNK, GLA_V), lambda c: (c, 1)),
            pl.BlockSpec((CHUNK, GLA_QK), lambda c: (c, 0)),
        ],
        out_specs=[
            pl.BlockSpec((CHUNK, GLA_V), lambda c: (c, 0)),
            pl.BlockSpec((GLA_HEADS, 1, GLA_DV, GLA_DK), lambda c: (0, c, 0, 0)),
        ],
        out_shape=[jax.ShapeDtypeStruct((ROWS, GLA_V), F32),
                   jax.ShapeDtypeStruct((GLA_HEADS, N_CHUNKS, GLA_DV, GLA_DK), F32)],
        scratch_shapes=[pltpu.VMEM((GLA_HEADS, GLA_DV, GLA_DK), F32)],
        name="gla_scan_fwd", compiler_params=_params(("arbitrary",)),
    )(proj, proj, proj, garr)


def _gla_scan_bwd(proj, garr, states, do):
    last = N_CHUNKS - 1
    kl, vl = _head_lanes(GLA_DK, GLA_HEADS), _head_lanes(GLA_DV, GLA_HEADS)

    def body(q_ref, k_ref, v_ref, g_ref, s_ref, do_ref, dq_ref, dk_ref, dv_ref, dg_ref, dst):
        @pl.when(pl.program_id(0) == 0)
        def _():
            dst[...] = jnp.zeros(dst.shape, F32)

        _, vjp = jax.vjp(lambda q, k, v, g, s: _gla_chunk(q, k, v, g, s, _dot_ad),
                         [q_ref[:, ln] for ln in kl], [k_ref[:, ln] for ln in kl], [v_ref[:, ln] for ln in vl],
                         [g_ref[:, ln] for ln in kl], [s_ref[i, 0] for i in range(GLA_HEADS)])
        dq, dk, dv, dg, ds = vjp(([do_ref[:, ln] for ln in vl], [dst[i] for i in range(GLA_HEADS)]))
        for i in range(GLA_HEADS):
            dq_ref[:, kl[i]] = dq[i].astype(dq_ref.dtype)
            dk_ref[:, kl[i]] = dk[i].astype(dk_ref.dtype)
            dv_ref[:, vl[i]] = dv[i].astype(dv_ref.dtype)
            dg_ref[:, kl[i]] = dg[i]
            dst[i] = ds[i]

    kspec = pl.BlockSpec((CHUNK, GLA_QK), lambda c: (last - c, 0))
    vspec = pl.BlockSpec((CHUNK, GLA_V), lambda c: (last - c, 0))
    return pl.pallas_call(
        body, grid=(N_CHUNKS,),
        in_specs=[
            kspec,
            pl.BlockSpec((CHUNK, GLA_QK), lambda c: (last - c, 1)),
            pl.BlockSpec((CHUNK, GLA_V), lambda c: (last - c, 1)),
            kspec,
            pl.BlockSpec((GLA_HEADS, 1, GLA_DV, GLA_DK), lambda c: (0, last - c, 0, 0)),
            vspec,
        ],
        out_specs=[kspec, kspec, vspec, kspec],
        out_shape=[jax.ShapeDtypeStruct((ROWS, GLA_QK), BF16), jax.ShapeDtypeStruct((ROWS, GLA_QK), BF16),
                   jax.ShapeDtypeStruct((ROWS, GLA_V), BF16), jax.ShapeDtypeStruct((ROWS, GLA_QK), F32)],
        scratch_shapes=[pltpu.VMEM((GLA_HEADS, GLA_DV, GLA_DK), F32)],
        name="gla_scan_bwd", compiler_params=_params(("arbitrary",)),
    )(proj, proj, proj, garr, states, do)


def _gdn_scan_fwd(qn, kn, vn, gates):
    def body(q_ref, k_ref, v_ref, g_ref, o_ref, states_ref, st):
        @pl.when((pl.program_id(0) == 0) & (pl.program_id(1) == 0))
        def _():
            st[...] = jnp.zeros(st.shape, F32)

        heads = [pl.program_id(1) * GDN_GROUP + hh for hh in range(GDN_GROUP)]
        lanes = _head_lanes(GDN_DK, GDN_GROUP)
        s_in = [st[h] for h in heads]
        o, s_out = _gdn_chunk([q_ref[:, ln] for ln in lanes], [k_ref[:, ln] for ln in lanes],
                              [v_ref[:, ln] for ln in lanes], g_ref[...], s_in, heads, _dot, _solve)
        for hh in range(GDN_GROUP):
            states_ref[hh, 0] = s_in[hh]
            o_ref[:, lanes[hh]] = o[hh]
            st[heads[hh]] = s_out[hh]

    hspec = pl.BlockSpec((CHUNK, GDN_GROUP * GDN_DK), lambda c, h: (c, h))
    return pl.pallas_call(
        body, grid=(N_CHUNKS, GDN_HEADS // GDN_GROUP),
        in_specs=[hspec, hspec, hspec, pl.BlockSpec((CHUNK, LANE), lambda c, h: (c, 0))],
        out_specs=[hspec, pl.BlockSpec((GDN_GROUP, 1, GDN_DK, GDN_DV), lambda c, h: (h, c, 0, 0))],
        out_shape=[jax.ShapeDtypeStruct((ROWS, D_MODEL), F32),
                   jax.ShapeDtypeStruct((GDN_HEADS, N_CHUNKS, GDN_DK, GDN_DV), F32)],
        scratch_shapes=[pltpu.VMEM((GDN_HEADS, GDN_DK, GDN_DV), F32)],
        name="gdn_scan_fwd", compiler_params=_params(("arbitrary", "arbitrary")),
    )(qn, kn, vn, gates)


def _gdn_scan_bwd(qn, kn, vn, gates, states, do):
    last = N_CHUNKS - 1

    def body(q_ref, k_ref, v_ref, g_ref, s_ref, do_ref, dq_ref, dk_ref, dv_ref, dg_ref, dst):
        @pl.when((pl.program_id(0) == 0) & (pl.program_id(1) == 0))
        def _():
            dst[...] = jnp.zeros(dst.shape, F32)

        @pl.when(pl.program_id(1) == 0)
        def _():
            dg_ref[...] = jnp.zeros(dg_ref.shape, F32)

        heads = [pl.program_id(1) * GDN_GROUP + hh for hh in range(GDN_GROUP)]
        lanes = _head_lanes(GDN_DK, GDN_GROUP)
        _, vjp = jax.vjp(lambda q, k, v, g, s: _gdn_chunk(q, k, v, g, s, heads, _dot_ad, _solve_ad),
                         [q_ref[:, ln] for ln in lanes], [k_ref[:, ln] for ln in lanes],
                         [v_ref[:, ln] for ln in lanes], g_ref[...], [s_ref[hh, 0] for hh in range(GDN_GROUP)])
        dq, dk, dv, dg, ds = vjp(([do_ref[:, ln] for ln in lanes], [dst[h] for h in heads]))
        for hh in range(GDN_GROUP):
            dq_ref[:, lanes[hh]] = dq[hh]
            dk_ref[:, lanes[hh]] = dk[hh]
            dv_ref[:, lanes[hh]] = dv[hh]
            dst[heads[hh]] = ds[hh]
        dg_ref[...] += dg

    hspec = pl.BlockSpec((CHUNK, GDN_GROUP * GDN_DK), lambda c, h: (last - c, h))
    gspec = pl.BlockSpec((CHUNK, LANE), lambda c, h: (last - c, 0))
    return pl.pallas_call(
        body, grid=(N_CHUNKS, GDN_HEADS // GDN_GROUP),
        in_specs=[hspec, hspec, hspec, gspec,
                  pl.BlockSpec((GDN_GROUP, 1, GDN_DK, GDN_DV), lambda c, h: (h, last - c, 0, 0)), hspec],
        out_specs=[hspec, hspec, hspec, gspec],
        out_shape=[jax.ShapeDtypeStruct((ROWS, D_MODEL), F32)] * 3 + [jax.ShapeDtypeStruct((ROWS, LANE), F32)],
        scratch_shapes=[pltpu.VMEM((GDN_HEADS, GDN_DK, GDN_DV), F32)],
        name="gdn_scan_bwd", compiler_params=_params(("arbitrary", "arbitrary")),
    )(qn, kn, vn, gates, states, do)


def _gdn_conv_fwd(proj, w, kind, cb):
    fn = _gdn_conv_fn(kind)
    nt = ROWS // ROW_TILE

    def body(prev_ref, cur_ref, w_ref, o_ref):
        (out,) = fn(pl.program_id(0), prev_ref[...], cur_ref[...], w_ref[...])
        o_ref[...] = out

    return pl.pallas_call(
        body, grid=(nt,),
        in_specs=[pl.BlockSpec((ROW_TILE, D_MODEL), lambda i: (jnp.maximum(i - 1, 0), cb)),
                  pl.BlockSpec((ROW_TILE, D_MODEL), lambda i: (i, cb)),
                  pl.BlockSpec(w.shape, lambda i: (0, 0))],
        out_specs=pl.BlockSpec((ROW_TILE, D_MODEL), lambda i: (i, 0)),
        out_shape=jax.ShapeDtypeStruct((ROWS, D_MODEL), F32),
        name="gdn_conv_fwd_" + kind, compiler_params=_params(("parallel",)),
    )(proj, proj, w)


def _gdn_conv_bwd(proj, w, kind, cb, ct):
    fn = _gdn_conv_fn(kind)
    nt = ROWS // ROW_TILE

    def body(prev_ref, cur_ref, w_ref, ct_ref, dx_ref, dw_ref, carry):
        step = pl.program_id(0)
        i = nt - 1 - step

        @pl.when(step == 0)
        def _():
            carry[...] = jnp.zeros(carry.shape, F32)
            dw_ref[...] = jnp.zeros(dw_ref.shape, F32)

        _, vjp = jax.vjp(lambda p, c, ww: fn(i, p, c, ww), prev_ref[...], cur_ref[...], w_ref[...])
        dprev, dcur, dw = vjp((ct_ref[...],))
        dx_ref[...] = (dcur + carry[...]).astype(dx_ref.dtype)
        carry[...] = dprev
        dw_ref[...] += dw

    return pl.pallas_call(
        body, grid=(nt,),
        in_specs=[pl.BlockSpec((ROW_TILE, D_MODEL), lambda s: (jnp.maximum(nt - 2 - s, 0), cb)),
                  pl.BlockSpec((ROW_TILE, D_MODEL), lambda s: (nt - 1 - s, cb)),
                  pl.BlockSpec(w.shape, lambda s: (0, 0)),
                  pl.BlockSpec((ROW_TILE, D_MODEL), lambda s: (nt - 1 - s, 0))],
        out_specs=[pl.BlockSpec((ROW_TILE, D_MODEL), lambda s: (nt - 1 - s, 0)),
                   pl.BlockSpec(w.shape, lambda s: (0, 0))],
        out_shape=[jax.ShapeDtypeStruct((ROWS, D_MODEL), BF16), jax.ShapeDtypeStruct(w.shape, F32)],
        scratch_shapes=[pltpu.VMEM((ROW_TILE, D_MODEL), F32)],
        name="gdn_conv_bwd_" + kind, compiler_params=_params(("arbitrary",)),
    )(proj, proj, w, ct)


def _loss_head(h, target):
    nt = ROWS // ROW_TILE
    lead = PAD_ROWS // ROW_TILE

    def body(h_ref, t_ref, dh_ref, acc_ref):
        i = pl.program_id(0)

        @pl.when(i == 0)
        def _():
            acc_ref[...] = jnp.zeros(acc_ref.shape, F32)

        err = (h_ref[...] - t_ref[...]) * jnp.where(i >= lead, 1.0, 0.0)
        dh_ref[...] = err * (1.0 / D_MODEL)
        acc_ref[...] += jnp.sum(err * err, axis=0, keepdims=True)

    return pl.pallas_call(
        body, grid=(nt,),
        in_specs=[pl.BlockSpec((ROW_TILE, D_MODEL), lambda i: (i, 0)),
                  pl.BlockSpec((ROW_TILE, D_MODEL), lambda i: (jnp.maximum(i - lead, 0), 0))],
        out_specs=[pl.BlockSpec((ROW_TILE, D_MODEL), lambda i: (i, 0)),
                   pl.BlockSpec((1, D_MODEL), lambda i: (0, 0))],
        out_shape=[jax.ShapeDtypeStruct((ROWS, D_MODEL), F32), jax.ShapeDtypeStruct((1, D_MODEL), F32)],
        name="loss_head", compiler_params=_params(("arbitrary",)),
    )(h, target)


MESH = pl.DeviceIdType.MESH
ANY = pl.BlockSpec(memory_space=pl.ANY)


N_PEERS = N_DEV - 1


HBM_SPEC = pl.BlockSpec(memory_space=pltpu.HBM)
SEM_SPEC = pl.BlockSpec(memory_space=pltpu.SEMAPHORE)
SIDE_EFFECT = pltpu.SideEffectType.DATAFLOW_SIDE_EFFECTING


def _me_and_peers():
    x, y, c = lax.axis_index("x"), lax.axis_index("y"), lax.axis_index("c")
    peers = []
    for k in range(1, N_DEV):
        pid = (1 - x if k & 4 else x, 1 - y if k & 2 else y, 1 - c if k & 1 else c)
        peers.append((4 * pid[0] + 2 * pid[1] + pid[2], pid))
    return 4 * x + 2 * y + c, peers


def _split_copies(src_refs, land_refs, send_sems, recv_sems, spread):
    me, peers = _me_and_peers()
    out = []
    for t, (src, land) in enumerate(zip(src_refs, land_refs)):
        for k, (peer, pid) in enumerate(peers):
            sems = dict(send_sem=send_sems.at[N_PEERS * t + k], recv_sem=recv_sems.at[N_PEERS * t + k],
                        device_id=pid, device_id_type=MESH)
            piece = src if spread else src.at[peer]
            out.append((pltpu.make_async_remote_copy(src_ref=piece, dst_ref=land.at[me], **sems),
                        pltpu.make_async_remote_copy(src_ref=piece, dst_ref=land.at[peer], **sems)))
    return out


def _send_start(srcs, spread, name):
    n = len(srcs)
    lands = [lax.empty((N_DEV,) + (s.shape if spread else s.shape[1:]), s.dtype) for s in srcs]

    def body(*refs):
        for out, _ in _split_copies(refs[:n], refs[n:2 * n], refs[2 * n], refs[2 * n + 1], spread):
            out.start()
        refs[-1][...] = jnp.zeros(refs[-1].shape, F32)

    thru = [pltpu.HBM(a.shape, a.dtype) for a in list(srcs) + lands]
    res = pl.pallas_call(
        body, name=name,
        out_shape=(pltpu.SemaphoreType.DMA((N_PEERS * n,)), pltpu.SemaphoreType.DMA((N_PEERS * n,)), *thru,
                   jax.ShapeDtypeStruct((8, LANE), F32)),
        in_specs=[HBM_SPEC] * (2 * n),
        out_specs=(SEM_SPEC, SEM_SPEC, *[HBM_SPEC] * (2 * n), pl.BlockSpec(memory_space=pltpu.VMEM)),
        input_output_aliases={t: 2 + t for t in range(2 * n)},
        compiler_params=pltpu.CompilerParams(has_side_effects=SIDE_EFFECT),
    )(*[pltpu.with_memory_space_constraint(a, pltpu.HBM) for a in list(srcs) + lands])
    return res[0], res[1], list(res[2:2 + n]), list(res[2 + n:2 + 2 * n]), res[-1]


def _send_wait(started, spread, after, name):
    send_sems, recv_sems, srcs, lands, _ = started
    n = len(srcs)

    def body(*refs):
        for out, arrival in _split_copies(refs[:n], refs[n:2 * n], refs[2 * n], refs[2 * n + 1], spread):
            out.wait_send()
            arrival.wait_recv()

    thru = [pltpu.HBM(a.shape, a.dtype) for a in srcs + lands]
    res = pl.pallas_call(
        body, name=name, out_shape=tuple(thru),
        in_specs=[HBM_SPEC] * (2 * n) + [SEM_SPEC, SEM_SPEC, ANY], out_specs=tuple([HBM_SPEC] * (2 * n)),
        input_output_aliases={t: t for t in range(2 * n)},
        compiler_params=pltpu.CompilerParams(has_side_effects=SIDE_EFFECT),
    )(*srcs, *lands, send_sems, recv_sems, after)
    return list(res[n:])


def _adamw(recvs, w, m, v, name):
    n_layers = len(recvs)
    rows, cols = recvs[0].shape[1:]
    tr = _pick(rows, (256, 128, 64, 32, 16))
    nb = rows // tr

    def body(*refs):
        r_refs = refs[:n_layers]
        w_ref, m_ref, v_ref, g_ref, d_ref, m2_ref, v2_ref = refs[n_layers:]

        def update(r_ref):
            g = r_ref[0].astype(F32)
            for k in range(1, N_DEV):
                g = g + r_ref[k].astype(F32)
            m2 = ADAM_B1 * m_ref[...] + (1.0 - ADAM_B1) * g
            v2 = ADAM_B2 * v_ref[...] + (1.0 - ADAM_B2) * (g * g)
            m_hat = m2 / (1.0 - ADAM_B1 ** ADAM_STEP)
            v_hat = v2 / (1.0 - ADAM_B2 ** ADAM_STEP)
            g_ref[...] = g
            d_ref[...] = -ADAM_LR * (m_hat / (jnp.sqrt(v_hat) + ADAM_EPS) + ADAM_WD * w_ref[...])
            m2_ref[...] = m2
            v2_ref[...] = v2

        if n_layers == 1:
            update(r_refs[0])
        else:
            for layer in range(n_layers):
                pl.when(pl.program_id(0) == layer)(functools.partial(update, r_refs[layer]))

    def recv_spec(layer):
        return pl.BlockSpec((N_DEV, tr, cols), lambda a, i: (0, jnp.where(a == layer, i, 0), 0))

    spec = pl.BlockSpec((tr, cols), lambda a, i: (a * nb + i, 0))
    return pl.pallas_call(
        body, grid=(n_layers, nb),
        in_specs=[recv_spec(layer) for layer in range(n_layers)] + [spec, spec, spec],
        out_specs=[spec] * 4, out_shape=[jax.ShapeDtypeStruct(w.shape, F32)] * 4,
        name=name, compiler_params=_params(("arbitrary", "arbitrary")),
    )(*recvs, w, m, v)


def _padded_rows(n, mult):
    rows = -(-n // PACK_COLS)
    return -(-rows // mult) * mult


def _pack(flat_pieces, mult, dtype, lead=()):
    flat = jnp.concatenate([p.astype(dtype) for p in flat_pieces], axis=-1)
    n = flat.shape[-1]
    rows = _padded_rows(n, mult)
    flat = jnp.pad(flat, [(0, 0)] * len(lead) + [(0, rows * PACK_COLS - n)])
    return flat.reshape(lead + (rows, PACK_COLS))


def _unpack(packed, shapes):
    flat = packed.reshape(-1)
    out, off = {}, 0
    for name, shape in shapes:
        size = 1
        for s in shape:
            size *= s
        out[name] = flat[off:off + size].reshape(shape)
        off += size
    return out


def _local_2d(arr):
    return arr.reshape(-1, arr.shape[-1])


SHARD_2D = {n: (s[-2] if len(s) > 2 else s[0], s[-1], a == len(s) - 1) for n, s, a in SHARDED}
SHARD_2D["gdn_conv_w"] = (GDN_CONV, GDN_CONV_DIM // N_DEV, True)


def _wire(name):
    rows, cols, _ = SHARD_2D[name]
    return BF16 if rows * cols >= 1 << 16 else F32


def _layer_tensors(i):
    kind, j = i % 3, i // 3
    pre = ("fox", "gla", "gdn")[kind]
    mix = [(pre + "_w_in", j), (pre + "_w_out", j)]
    if kind == 1:
        mix.append(("gla_w_alpha2", j))
    if kind == 2:
        mix.append(("gdn_conv_w", j))
    if i == 0:
        mix.append(("meta_tokens", None))
    return {"mix": mix, "ffn": [("w_gate_up", i), ("w_down", i)]}


PARTS = ("mix", "ffn")


def _layer_block(arr, idx):
    return arr if idx is None else _local_2d(arr[idx])


def _to_slabs(full2d, name):
    rows, cols, by_cols = SHARD_2D[name]
    if by_cols:
        parts = full2d if isinstance(full2d, tuple) else (full2d,)
        per = parts[0].shape[1] // cols
        return jnp.stack([parts[d // per][:, cols * (d % per):cols * (d % per + 1)] for d in range(N_DEV)])
    return full2d.reshape(N_DEV, rows, cols)


def _from_gathered(g, name):
    rows, cols, by_cols = SHARD_2D[name]
    if by_cols:
        return jnp.concatenate([g[d] for d in range(N_DEV)], axis=1)
    return g.reshape(N_DEV * rows, cols)


def _pad_cols(w, total):
    return jnp.pad(w, [(0, 0)] * (w.ndim - 1) + [(0, total - w.shape[-1])])


def _pad_lanes(v):
    return _pad_cols(v, LANE)


def _x(arr, width, cb0=0, moves=0):
    return (arr, width, cb0, moves)


def _rms_fwd(h, g):
    return _row_fwd(_rms_fn, [_x(h, D_MODEL)], [g], [], [(D_MODEL, D_MODEL, BF16)], "rms_fwd")[0]


FUSED_ROWS = 384


def _matmul_add_norm(a, b, add, g, name):
    m, k = a.shape
    n = b.shape[1]

    def body(a_ref, b_ref, add_ref, g_ref, h_ref, y_ref):
        h2 = _dot(a_ref[...], b_ref[...], "nn") + add_ref[...]
        h_ref[...] = h2
        y_ref[...] = _rms_fn(0, h2, g_ref[...])[0].astype(y_ref.dtype)

    rows = pl.BlockSpec((FUSED_ROWS, n), lambda i: (i, 0))
    return pl.pallas_call(
        body, grid=(m // FUSED_ROWS,),
        in_specs=[pl.BlockSpec((FUSED_ROWS, k), lambda i: (i, 0)), pl.BlockSpec((k, n), lambda i: (0, 0)), rows,
                  pl.BlockSpec((1, n), lambda i: (0, 0))],
        out_specs=[rows, rows],
        out_shape=[jax.ShapeDtypeStruct((m, n), F32), jax.ShapeDtypeStruct((m, n), BF16)],
        name=name, compiler_params=_params(("parallel",)),
    )(a, b, add, g)


def _matmul_norm_bwd(a, b, b_kblock, h, g, dres, name, prev=None):
    m, k = a.shape
    n = b.shape[0]

    def body(*refs):
        a_ref, b_ref, h_ref, g_ref, dres_ref = refs[:5]
        dh_ref, dg_ref = refs[-2:]
        dy = _dot(a_ref[...], b_ref[...], "nt")
        if prev is not None:
            dy = dy + refs[5][...]
        _, vjp = jax.vjp(lambda hh, gg: _rms_fn(0, hh, gg), h_ref[...], g_ref[...])
        dh, dg = vjp((dy,))
        dh_ref[...] = dres_ref[...] + dh

        @pl.when(pl.program_id(0) == 0)
        def _():
            dg_ref[...] = jnp.zeros(dg_ref.shape, F32)

        dg_ref[...] += dg

    rows = pl.BlockSpec((FUSED_ROWS, n), lambda i: (i, 0))
    gain = pl.BlockSpec((1, n), lambda i: (0, 0))
    operands = [a, b, h, g, dres] + ([] if prev is None else [prev])
    in_specs = [pl.BlockSpec((FUSED_ROWS, k), lambda i: (i, 0)), pl.BlockSpec((n, k), lambda i: (0, b_kblock)),
                rows, gain, rows] + ([] if prev is None else [rows])
    return pl.pallas_call(
        body, grid=(m // FUSED_ROWS,), in_specs=in_specs, out_specs=[rows, gain],
        out_shape=[jax.ShapeDtypeStruct((m, n), F32), jax.ShapeDtypeStruct((1, n), F32)],
        name=name, compiler_params=_params(("arbitrary",)),
    )(*operands)


SW_WIDTH = 256
SW_NCOL = D_FF // SW_WIDTH


def _ffn_gate_up(y, wgu):
    m, k = y.shape
    tm = _pick(m, (1408, 384, 128))

    def body(y_ref, wg_ref, wu_ref, a_ref, g_ref, u_ref):
        yv = y_ref[...]
        gt = _dot(yv, wg_ref[...], "nn")
        up = _dot(yv, wu_ref[...], "nn")
        a_ref[...] = (_silu(gt) * up).astype(a_ref.dtype)
        g_ref[...] = gt.astype(g_ref.dtype)
        u_ref[...] = up.astype(u_ref.dtype)

    tile = pl.BlockSpec((tm, SW_WIDTH), lambda i, j: (i, j))
    return pl.pallas_call(
        body, grid=(m // tm, SW_NCOL),
        in_specs=[pl.BlockSpec((tm, k), lambda i, j: (i, 0)),
                  pl.BlockSpec((k, SW_WIDTH), lambda i, j: (0, j)),
                  pl.BlockSpec((k, SW_WIDTH), lambda i, j: (0, SW_NCOL + j))],
        out_specs=[tile] * 3, out_shape=[jax.ShapeDtypeStruct((m, D_FF), BF16)] * 3,
        name="ffn_gate_up", compiler_params=_params(("parallel", "parallel")),
    )(y, wgu, wgu)


def _ffn_down_dx(dh2, wd, gt, up):
    m, k = dh2.shape
    tm = _pick(m, (1408, 384, 128))

    def body(d_ref, w_ref, g_ref, u_ref, dg_ref, du_ref):
        da = _dot(d_ref[...], w_ref[...], "nt")
        gv, uv = g_ref[...].astype(F32), u_ref[...].astype(F32)
        s = jax.nn.sigmoid(gv)
        dg_ref[...] = (da * uv * (s + gv * s * (1.0 - s))).astype(dg_ref.dtype)
        du_ref[...] = (da * gv * s).astype(du_ref.dtype)

    tile = pl.BlockSpec((tm, SW_WIDTH), lambda i, j: (i, j))
    return pl.pallas_call(
        body, grid=(m // tm, SW_NCOL),
        in_specs=[pl.BlockSpec((tm, k), lambda i, j: (i, 0)),
                  pl.BlockSpec((SW_WIDTH, k), lambda i, j: (j, 0)), tile, tile],
        out_specs=[tile] * 2, out_shape=[jax.ShapeDtypeStruct((m, D_FF), BF16)] * 2,
        name="ffn_down_dx", compiler_params=_params(("parallel", "parallel")),
    )(dh2, wd, gt, up)


def _ffn_fwd(h, y, wgu, wd, g_next):
    a, gt, up = _ffn_gate_up(y, wgu)
    if g_next is None:
        h2, y_next = _matmul(a, wd, "nn", F32, "ffn_down", add=h), None
    else:
        h2, y_next = _matmul_add_norm(a, wd, h, g_next, "ffn_down_norm")
    return h2, y_next, (h, y, gt, up, a)


def _ffn_bwd(dh2, saved, g, wgu, wd):
    h, y, gt, up, a = saved
    dgt, dup = _ffn_down_dx(dh2, wd, gt, up)
    dwd = _matmul(a, dh2, "tn", F32, "ffn_down_dw")
    dwg = _matmul(y, dgt, "tn", F32, "ffn_gate_dw")
    dwu = _matmul(y, dup, "tn", F32, "ffn_up_dw")
    dy = _matmul(dgt, wgu, "nt", F32, "ffn_gate_dx", b_kblock=0)
    dh, dg = _matmul_norm_bwd(dup, wgu, 1, h, g, dh2, "ffn_up_dx_norm", prev=dy)
    return dh, dg, (dwg, dwu), dwd


def _fox_consts():
    lane = jnp.arange(D_MODEL) // FOX_DH
    e_expand = (jnp.arange(LANE)[:, None] == lane[None, :]).astype(F32)
    e_mean = e_expand.T * (1.0 / FOX_DH)
    half = jnp.arange(LANE) // FOX_DH
    blockdiag = (half[:, None] == half[None, :]).astype(F32)
    return e_mean, e_expand, blockdiag


def _pair_layout(c):
    heads = c[:, :FOX_HEADS].T
    crow = heads.reshape(FOX_PAIRS, 2, ROWS)
    c2 = jnp.broadcast_to(crow.transpose(0, 2, 1)[:, :, :, None], (FOX_PAIRS, ROWS, 2, FOX_DH))
    crow = jnp.where(jnp.arange(ROWS) >= FIRST_ROW, crow, KEY_OFF)
    return c2.reshape(FOX_PAIRS, ROWS, LANE), crow


def _fox_fwd(y, p):
    proj = _matmul(y, p["w_in"], "nn", F32, "fox_in")
    e_mean, e_expand, _ = _fox_consts()
    nb = D_MODEL // LANE
    xs = [_x(proj, D_MODEL, 0), _x(proj, D_MODEL, 1), _x(proj, LANE, 4 * nb)]
    ps = [p["q_gain"], p["k_gain"], p["b_f"]]
    qn, kn, lf = _row_fwd(_fox_pre_fn, xs, ps, [e_mean, e_expand],
                          [(D_MODEL, D_MODEL, BF16), (D_MODEL, D_MODEL, BF16), (LANE, LANE, F32)], "fox_pre_fwd")
    c = _seq_cumsum(lf, False, "fox_cumsum")
    c2, crow = _pair_layout(c)
    o, lse2 = _fox_attn_fwd(qn, kn, proj, c2, crow)
    om = _row_fwd(_fox_post_fn, [_x(o, D_MODEL, 0), _x(proj, D_MODEL, 3)], [], [],
                  [(D_MODEL, D_MODEL, BF16)], "fox_post_fwd")[0]
    return om, (proj, qn, kn, c2, crow, o, lse2)


def _fox_bwd(dom, y, saved, p):
    proj, qn, kn, c2, crow, o, lse2 = saved
    e_mean, e_expand, blockdiag = _fox_consts()
    nb = D_MODEL // LANE
    do, dgate = _row_bwd(_fox_post_fn, [_x(o, D_MODEL, 0), _x(proj, D_MODEL, 3)], [], [], [_x(dom, D_MODEL)],
                         [(D_MODEL, F32), (D_MODEL, BF16)], "fox_post_bwd")
    dq, dk, dv, dcrow, dcq2 = _fox_attn_bwd(qn, kn, proj, c2, crow, lse2, o, do, blockdiag)
    dlf = _seq_cumsum(_pad_cols(dcrow.reshape(FOX_HEADS, ROWS).T, LANE), True, "fox_cumsum_rev", pairs=dcq2)
    xs = [_x(proj, D_MODEL, 0), _x(proj, D_MODEL, 1), _x(proj, LANE, 4 * nb)]
    ps = [p["q_gain"], p["k_gain"], p["b_f"]]
    dqr, dkr, df, dqg, dkg, dbf = _row_bwd(
        _fox_pre_fn, xs, ps, [e_mean, e_expand], [_x(dq, D_MODEL), _x(dk, D_MODEL), _x(dlf, LANE)],
        [(D_MODEL, BF16), (D_MODEL, BF16), (LANE, BF16)], "fox_pre_bwd")
    dproj = jnp.concatenate([dqr, dkr, dv, dgate, df], axis=1)
    grads = {
        "w_in": _matmul(y, dproj, "tn", F32, "fox_in_dw")[:, :FOX_IN],
        "q_gain": dqg.reshape(FOX_HEADS, FOX_DH).sum(axis=0),
        "k_gain": dkg.reshape(FOX_HEADS, FOX_DH).sum(axis=0),
        "b_f": dbf[0, :FOX_HEADS],
    }
    return dproj, grads


def _gla_fwd(y, p):
    proj = _matmul(y, p["w_in"], "nn", F32, "gla_in")
    small = (2 * GLA_QK + 2 * GLA_V) // LANE
    garr = _row_fwd(_gla_pre_fn, [_x(proj, LANE, small)], [p["w_alpha"], p["b_alpha"]], [],
                    [(GLA_QK, GLA_QK, F32)], "gla_pre_fwd")[0]
    o, states = _gla_scan_fwd(proj, garr)
    om = _row_fwd(_gla_post_fn, [_x(o, GLA_V, 0), _x(proj, GLA_V, 2)], [p["o_gain"]], [],
                  [(GLA_V, GLA_V, BF16)], "gla_post_fwd")[0]
    return om, (proj, garr, states, o)


def _gla_bwd(dom, y, saved, p):
    proj, garr, states, o = saved
    small = (2 * GLA_QK + 2 * GLA_V) // LANE
    do, dr, dgain = _row_bwd(_gla_post_fn, [_x(o, GLA_V, 0), _x(proj, GLA_V, 2)], [p["o_gain"]], [],
                             [_x(dom, GLA_V)], [(GLA_V, F32), (GLA_V, BF16)], "gla_post_bwd")
    dq, dk, dv, dg = _gla_scan_bwd(proj, garr, states, do)
    dalr, dwa, dba = _row_bwd(_gla_pre_fn, [_x(proj, LANE, small)], [p["w_alpha"], p["b_alpha"]], [],
                              [_x(dg, GLA_QK)], [(LANE, BF16)], "gla_pre_bwd")
    dproj = jnp.concatenate([dq, dk, dv, dr, dalr], axis=1)
    grads = {
        "w_in": _matmul(y, dproj, "tn", F32, "gla_in_dw")[:, :GLA_IN],
        "w_alpha": dwa[:GLA_RANK], "b_alpha": dba[0], "o_gain": dgain[0],
    }
    return dproj, grads


def _gdn_fwd(y, p):
    proj = _matmul(y, p["w_in"], "nn", F32, "gdn_in")
    nb = D_MODEL // LANE
    qn = _gdn_conv_fwd(proj, p["conv_q"], "q", 0)
    kn = _gdn_conv_fwd(proj, p["conv_k"], "k", 1)
    vn = _gdn_conv_fwd(proj, p["conv_v"], "v", 2)
    gates = _row_fwd(_gdn_gates_fn, [_x(proj, LANE, 4 * nb)], [p["a_log"], p["dt_bias"]], [],
                     [(LANE, LANE, F32)], "gdn_gates_fwd")[0]
    o, states = _gdn_scan_fwd(qn, kn, vn, gates)
    om = _row_fwd(_gdn_post_fn, [_x(o, D_MODEL, 0), _x(proj, D_MODEL, 3)], [p["o_gain"]], [],
                  [(D_MODEL, D_MODEL, BF16)], "gdn_post_fwd")[0]
    return om, (proj, qn, kn, vn, gates, states, o)


def _gdn_bwd(dom, y, saved, p):
    proj, qn, kn, vn, gates, states, o = saved
    nb = D_MODEL // LANE
    do, dgate, dgain = _row_bwd(_gdn_post_fn, [_x(o, D_MODEL, 0), _x(proj, D_MODEL, 3)], [p["o_gain"]], [],
                                [_x(dom, D_MODEL)], [(D_MODEL, F32), (D_MODEL, BF16)], "gdn_post_bwd")
    dqn, dkn, dvn, dgates = _gdn_scan_bwd(qn, kn, vn, gates, states, do)
    dsm, dalog, ddtb = _row_bwd(_gdn_gates_fn, [_x(proj, LANE, 4 * nb)], [p["a_log"], p["dt_bias"]], [],
                                [_x(dgates, LANE)], [(LANE, BF16)], "gdn_gates_bwd")
    dxq, dwq = _gdn_conv_bwd(proj, p["conv_q"], "q", 0, dqn)
    dxk, dwk = _gdn_conv_bwd(proj, p["conv_k"], "k", 1, dkn)
    dxv, dwv = _gdn_conv_bwd(proj, p["conv_v"], "v", 2, dvn)
    dproj = jnp.concatenate([dxq, dxk, dxv, dgate, dsm], axis=1)
    grads = {
        "w_in": _matmul(y, dproj, "tn", F32, "gdn_in_dw")[:, :GDN_IN],
        "conv_w": jnp.concatenate([dwq[:GDN_CONV], dwk[:GDN_CONV], dwv[:GDN_CONV]], axis=1),
        "a_log": dalog[0, :GDN_HEADS], "dt_bias": ddtb[0, :GDN_HEADS], "o_gain": dgain[0],
    }
    return dproj, grads


_MIXERS = ((_fox_fwd, _fox_bwd), (_gla_fwd, _gla_bwd), (_gdn_fwd, _gdn_bwd))


def _mixer_params(i, wts, rep):
    kind, j = i % 3, i // 3
    if kind == 0:
        return {
            "w_in": _pad_cols(wts["fox_w_in"], FOX_IN_PAD), "w_out": wts["fox_w_out"],
            "q_gain": jnp.tile(rep["fox_q_gain"][j], FOX_HEADS)[None],
            "k_gain": jnp.tile(rep["fox_k_gain"][j], FOX_HEADS)[None],
            "b_f": _pad_lanes(rep["fox_b_f"][j][None]),
        }
    if kind == 1:
        return {
            "w_in": _pad_cols(wts["gla_w_in"], GLA_IN_PAD), "w_out": wts["gla_w_out"],
            "w_alpha": jnp.pad(wts["gla_w_alpha2"], ((0, LANE - GLA_RANK), (0, 0))).astype(F32),
            "b_alpha": rep["gla_b_alpha"][j][None], "o_gain": rep["gla_o_gain"][j][None],
        }
    conv = jnp.pad(wts["gdn_conv_w"].astype(F32), ((0, 8 - GDN_CONV), (0, 0)))
    return {
        "w_in": _pad_cols(wts["gdn_w_in"], GDN_IN_PAD), "w_out": wts["gdn_w_out"],
        "conv_q": conv[:, :D_MODEL], "conv_k": conv[:, D_MODEL:2 * D_MODEL], "conv_v": conv[:, 2 * D_MODEL:],
        "a_log": _pad_lanes(rep["gdn_a_log"][j][None]), "dt_bias": _pad_lanes(rep["gdn_dt_bias"][j][None]),
        "o_gain": rep["gdn_o_gain"][j][None],
    }


def kernel(x, meta_tokens, norm_mix, norm_ffn, w_gate_up, w_down, fox_w_in, fox_b_f, fox_q_gain, fox_k_gain, fox_w_out, gla_w_in, gla_w_alpha2, gla_b_alpha, gla_o_gain, gla_w_out, gdn_w_in, gdn_conv_w, gdn_a_log, gdn_dt_bias, gdn_o_gain, gdn_w_out, loss_target, m_meta_tokens, m_norm_mix, m_norm_ffn, m_w_gate_up, m_w_down, m_fox_w_in, m_fox_b_f, m_fox_q_gain, m_fox_k_gain, m_fox_w_out, m_gla_w_in, m_gla_w_alpha2, m_gla_b_alpha, m_gla_o_gain, m_gla_w_out, m_gdn_w_in, m_gdn_conv_w, m_gdn_a_log, m_gdn_dt_bias, m_gdn_o_gain, m_gdn_w_out, v_meta_tokens, v_norm_mix, v_norm_ffn, v_w_gate_up, v_w_down, v_fox_w_in, v_fox_b_f, v_fox_q_gain, v_fox_k_gain, v_fox_w_out, v_gla_w_in, v_gla_w_alpha2, v_gla_b_alpha, v_gla_o_gain, v_gla_w_out, v_gdn_w_in, v_gdn_conv_w, v_gdn_a_log, v_gdn_dt_bias, v_gdn_o_gain, v_gdn_w_out):
    given = dict(locals())
    w_loc = {n: given[n] for n in WEIGHT_ORDER}
    m_loc = {n: given["m_" + n] for n in WEIGHT_ORDER}
    v_loc = {n: given["v_" + n] for n in WEIGHT_ORDER}

    rep = {n: w_loc[n] for n, _ in REPLICATED}
    me = 4 * lax.axis_index("x") + 2 * lax.axis_index("y") + lax.axis_index("c")

    def with_own(lands, owns):
        return [lax.dynamic_update_slice(land, own[None], (me, 0, 0)) for land, own in zip(lands, owns)]

    gathers = {}
    order = jnp.zeros((), F32)
    for i in range(DEPTH):
        groups = _layer_tensors(i)
        for part in PARTS:
            blocks = [(_layer_block(w_loc[n], idx) + order).astype(_wire(n)) for n, idx in groups[part]]
            started = _send_start(blocks, True, "gather_start_%d_%s" % (i, part))
            gathers[(i, part)] = (groups[part], blocks, started)
            order = started[4][0, 0]
    all_started = gathers[(DEPTH - 1, PARTS[-1])][2][4]

    def weights(i, part, after):
        tensors, blocks, started = gathers[(i, part)]
        lands = with_own(_send_wait(started, True, after, "gather_wait_%d_%s" % (i, part)), blocks)
        return {n: _from_gathered(g, n) for (n, _), g in zip(tensors, lands)}

    h = y = None
    saved = []
    for i in range(DEPTH):
        w_mix = weights(i, "mix", all_started if i == 0 else h)
        if i == 0:
            h = jnp.concatenate([jnp.zeros((FIRST_ROW, D_MODEL), F32), w_mix["meta_tokens"], x[0]], axis=0)
            y = _rms_fwd(h, rep["norm_mix"][0][None])
        p = _mixer_params(i, w_mix, rep)
        fwd, _ = _MIXERS[i % 3]
        om, mix_saved = fwd(y, p)
        h_mid, y_mid = _matmul_add_norm(om, p["w_out"], h, rep["norm_ffn"][i][None], "mixer_out_norm")
        w_ffn = weights(i, "ffn", h_mid)
        g_next = rep["norm_mix"][i + 1][None] if i + 1 < DEPTH else None
        h_out, y_next, ffn_saved = _ffn_fwd(h_mid, y_mid, w_ffn["w_gate_up"], w_ffn["w_down"], g_next)
        saved.append((p, w_ffn, h, y, om, mix_saved, ffn_saved))
        h, y = h_out, y_next

    dh, sq = _loss_head(h, loss_target[0])
    loss = lax.psum(0.5 * jnp.sum(sq) * (1.0 / D_MODEL), ("x", "y", "c"))

    gw = {n: [None] * DEPTH for n in ("norm_mix", "norm_ffn")}
    gm = {}
    reduces = {}

    def start_reduce(i, part, full, extra=()):
        tensors = _layer_tensors(i)[part]
        slabs = [_to_slabs(full[n], n).astype(_wire(n)) for n, _ in tensors] + list(extra)
        started = _send_start(slabs, False, "reduce_start_%d_%s" % (i, part))
        reduces[(i, part)] = (tensors, slabs, started)
        return started[4][0, 0]

    order = jnp.zeros((), BF16)
    for i in reversed(range(DEPTH)):
        p, w_ffn, h_in, y, om, mix_saved, ffn_saved = saved[i]
        kind, j = i % 3, i // 3
        pre = ("fox", "gla", "gdn")[kind]
        _, bwd = _MIXERS[kind]
        dh_mid, gw["norm_ffn"][i], dwgu, dwd = _ffn_bwd(
            dh, ffn_saved, rep["norm_ffn"][i][None], w_ffn["w_gate_up"], w_ffn["w_down"] + order)
        order = start_reduce(i, "ffn", {"w_gate_up": dwgu, "w_down": dwd}).astype(BF16)
        dom = _matmul(dh_mid, p["w_out"] + order, "nt", F32, "mixer_out_dx")
        dwo = _matmul(om, dh_mid, "tn", F32, "mixer_out_dw")
        dproj, grads = bwd(dom, y, mix_saved, p)
        gm[(kind, j)] = grads
        dh, gw["norm_mix"][i] = _matmul_norm_bwd(dproj, p["w_in"], 0, h_in, rep["norm_mix"][i][None], dh_mid,
                                                 "mixer_in_dx_norm")
        mix_full = {pre + "_w_in": grads["w_in"], pre + "_w_out": dwo, "gla_w_alpha2": grads.get("w_alpha"),
                    "gdn_conv_w": grads.get("conv_w"), "meta_tokens": dh[FIRST_ROW:PAD_ROWS]}
        if i > 0:
            order = start_reduce(i, "mix", mix_full).astype(BF16)

    partial = {
        "norm_mix": jnp.concatenate(gw["norm_mix"], axis=0), "norm_ffn": jnp.concatenate(gw["norm_ffn"], axis=0),
        "fox_b_f": jnp.stack([gm[(0, 0)]["b_f"], gm[(0, 1)]["b_f"]]),
        "fox_q_gain": jnp.stack([gm[(0, 0)]["q_gain"], gm[(0, 1)]["q_gain"]]),
        "fox_k_gain": jnp.stack([gm[(0, 0)]["k_gain"], gm[(0, 1)]["k_gain"]]),
        "gla_b_alpha": gm[(1, 0)]["b_alpha"][None], "gla_o_gain": gm[(1, 0)]["o_gain"][None],
        "gdn_a_log": gm[(2, 0)]["a_log"][None], "gdn_dt_bias": gm[(2, 0)]["dt_bias"][None],
        "gdn_o_gain": gm[(2, 0)]["o_gain"][None],
    }

    small = _pack([partial[n].reshape(-1) for n, _ in REPLICATED], 8, F32)
    last_started = start_reduce(0, "mix", mix_full, extra=[jnp.broadcast_to(small[None], (N_DEV,) + small.shape)])
    recv = {}

    def wait_reduce(key, after):
        tensors, slabs, started = reduces[key]
        owns = [lax.dynamic_index_in_dim(s, me, 0, keepdims=False) for s in slabs]
        lands = with_own(_send_wait(started, False, after, "reduce_wait_%d_%s" % key), owns)
        for (n, idx), land in zip(tensors, lands):
            recv[(n, idx)] = land
        return lands

    def adamw_tensor(n, s, order=0.0):
        layers = [idx for i in range(DEPTH) for part in PARTS for (nn, idx) in _layer_tensors(i)[part] if nn == n]
        layers = sorted(layers, key=lambda idx: -1 if idx is None else idx)
        outs = _adamw([recv[(n, idx)] for idx in layers], _local_2d(w_loc[n]) + order, _local_2d(m_loc[n]),
                      _local_2d(v_loc[n]), "adamw_" + n)
        for k in range(4):
            res[k][n] = outs[k].reshape(s)
        return outs[0][0, 0]

    last = (0, "mix")
    for key in reduces:
        if key != last:
            wait_reduce(key, dh)
    res = [{} for _ in range(4)]
    late = {n for n, _ in reduces[last][0]}
    done = jnp.zeros((), F32)
    for n, s, _ in SHARDED:
        if n not in late:
            done = done + adamw_tensor(n, s, last_started)
    recv_small = wait_reduce(last, done.reshape(1, 1))[-1]
    for n, s, _ in SHARDED:
        if n in late:
            adamw_tensor(n, s)
    outs_small = _adamw([recv_small], *[_pack([d[n].reshape(-1) for n, _ in REPLICATED], 8, F32)
                                        for d in (w_loc, m_loc, v_loc)], "adamw_replicated")
    for k in range(4):
        res[k].update(_unpack(outs_small[k], list(REPLICATED)))

    result = [loss, dh[PAD_ROWS:][None]]
    for k in range(4):
        result += [res[k][n] for n in WEIGHT_ORDER]
    return tuple(result)
```

```python
import functools

import jax
import jax.numpy as jnp
from jax import lax
from jax.experimental import pallas as pl
from jax.experimental.pallas import tpu as pltpu

F32, BF16 = jnp.float32, jnp.bfloat16
HIGHEST = lax.Precision.HIGHEST

D_MODEL = 1024
SEQ = 4096
N_META = 16
DEPTH = 4
NORM_EPS = 1e-6
LANE = 128
PAD_ROWS = 128
FIRST_ROW = PAD_ROWS - N_META
ROWS = PAD_ROWS + SEQ
ROW_TILE = 128
CHUNK = 64
NEG = -1e30
N_DEV = 8
VMEM_LIMIT = 56 * 1024 * 1024
VMEM_BLOCK_BUDGET = 36 * 1024 * 1024

FOX_HEADS, FOX_DH = 16, 64
FOX_IN, FOX_IN_PAD = 4 * D_MODEL + FOX_HEADS, 4 * D_MODEL + LANE
GLA_HEADS, GLA_DK, GLA_DV, GLA_RANK = 4, 128, 256, 16
GLA_QK, GLA_V = GLA_HEADS * GLA_DK, GLA_HEADS * GLA_DV
GLA_IN, GLA_IN_PAD = 2 * GLA_QK + 2 * GLA_V + GLA_RANK, 2 * GLA_QK + 2 * GLA_V + LANE
GDN_HEADS, GDN_DK, GDN_DV, GDN_CONV = 8, 128, 128, 4
GDN_CONV_DIM = 3 * GDN_HEADS * GDN_DK
GDN_IN, GDN_IN_PAD = GDN_CONV_DIM + GDN_HEADS * GDN_DV + 2 * GDN_HEADS, GDN_CONV_DIM + GDN_HEADS * GDN_DV + LANE
D_FF = 2816

ADAM_LR, ADAM_B1, ADAM_B2, ADAM_EPS, ADAM_WD, ADAM_STEP = 0.001, 0.9, 0.999, 1e-08, 0.01, 10

PACK_COLS = 1024

SHARDED = (
    ("meta_tokens", (16, 128), 1),
    ("w_gate_up", (4, 1024, 704), 2),
    ("w_down", (4, 352, 1024), 1),
    ("fox_w_in", (2, 1024, 514), 2),
    ("fox_w_out", (2, 128, 1024), 1),
    ("gla_w_in", (1, 1024, 386), 2),
    ("gla_w_alpha2", (1, 16, 64), 2),
    ("gla_w_out", (1, 128, 1024), 1),
    ("gdn_w_in", (1, 1024, 514), 2),
    ("gdn_conv_w", (1, 4, 1, 384), 3),
    ("gdn_w_out", (1, 128, 1024), 1),
)
REPLICATED = (
    ("norm_mix", (4, 1024)),
    ("norm_ffn", (4, 1024)),
    ("fox_b_f", (2, 16)),
    ("fox_q_gain", (2, 64)),
    ("fox_k_gain", (2, 64)),
    ("gla_b_alpha", (1, 512)),
    ("gla_o_gain", (1, 256)),
    ("gdn_a_log", (1, 8)),
    ("gdn_dt_bias", (1, 8)),
    ("gdn_o_gain", (1, 128)),
)
WEIGHT_ORDER = (
    "meta_tokens", "norm_mix", "norm_ffn", "w_gate_up", "w_down", "fox_w_in", "fox_b_f", "fox_q_gain",
    "fox_k_gain", "fox_w_out", "gla_w_in", "gla_w_alpha2", "gla_b_alpha", "gla_o_gain", "gla_w_out",
    "gdn_w_in", "gdn_conv_w", "gdn_a_log", "gdn_dt_bias", "gdn_o_gain", "gdn_w_out",
)

_DN = {
    "nn": (((1,), (0,)), ((), ())),
    "nt": (((1,), (1,)), ((), ())),
    "tn": (((0,), (0,)), ((), ())),
}


def _dot(a, b, kind):
    return lax.dot_general(a.astype(BF16), b.astype(BF16), _DN[kind], preferred_element_type=F32)


@functools.partial(jax.custom_vjp, nondiff_argnums=(2,))
def _dot_ad(a, b, kind):
    return _dot(a, b, kind)


def _dot_ad_fwd(a, b, kind):
    return _dot(a, b, kind), (a, b)


def _dot_ad_bwd(kind, res, ct):
    a, b = res
    if kind == "nn":
        return _dot(ct, b, "nt"), _dot(a, ct, "tn")
    if kind == "nt":
        return _dot(ct, b, "nn"), _dot(ct, a, "tn")
    return _dot(b, ct, "nt"), _dot(a, ct, "nn")


_dot_ad.defvjp(_dot_ad_fwd, _dot_ad_bwd)


def _dot_hi(a, b, kind):
    return lax.dot_general(a, b, _DN[kind], precision=HIGHEST, preferred_element_type=F32)


def _dot_mid(a, b, kind):
    return lax.dot_general(a, b, _DN[kind], precision=lax.Precision.HIGH, preferred_element_type=F32)


def _iota(shape, dim):
    return lax.broadcasted_iota(jnp.int32, shape, dim)


def _row_mask(i, tm):
    return ((i * tm + _iota((tm, 1), 0)) >= FIRST_ROW).astype(F32)


def _log_sigmoid(z):
    return jnp.minimum(z, 0.0) - jnp.log(1.0 + jnp.exp(-jnp.abs(z)))


def _softplus(z):
    return jnp.maximum(z, 0.0) + jnp.log(1.0 + jnp.exp(-jnp.abs(z)))


def _silu(z):
    return z * jax.nn.sigmoid(z)


def _lane_col(x, lane):
    return jnp.sum(jnp.where(_iota(x.shape, 1) == lane, x, 0.0), axis=1, keepdims=True)


def _last_row(x):
    return jnp.sum(jnp.where(_iota(x.shape, 0) == x.shape[0] - 1, x, 0.0), axis=0, keepdims=True)


def _params(sem, limit=VMEM_LIMIT):
    return pltpu.CompilerParams(dimension_semantics=sem, vmem_limit_bytes=limit)


def _pick(n, options):
    for t in options:
        if n % t == 0:
            return t
    return n


def _matmul(a, b, kind, out_dtype, name, add=None, b_kblock=0):
    if kind == "nn":
        (m, k), n = a.shape, b.shape[1]
    elif kind == "nt":
        (m, k), n = a.shape, b.shape[0]
    else:
        (k, m), n = a.shape, b.shape[1]
    tn = _pick(n, (512, 384, 640, 256, 128))
    tm_options = (1408, 384, 128) if kind != "tn" else (1024, 512, 256, 128)
    tm = None
    for t in tm_options:
        if m % t:
            continue
        need = 2 * (t * k * a.dtype.itemsize + tn * k * b.dtype.itemsize + t * tn * 8)
        if need <= VMEM_BLOCK_BUDGET:
            tm = t
            break
    assert tm is not None, (name, a.shape, b.shape)
    a_bytes, b_bytes = m * k * a.dtype.itemsize, n * k * b.dtype.itemsize
    m_outer = a_bytes + (m // tm) * b_bytes <= b_bytes + (n // tn) * a_bytes

    def ij(fn):
        return (lambda i, j: fn(i, j)) if m_outer else (lambda j, i: fn(i, j))

    def body(*refs):
        a_ref, b_ref = refs[0], refs[1]
        o_ref = refs[-1]
        acc = _dot(a_ref[...], b_ref[...], kind)
        if add is not None:
            acc = acc + refs[2][...]
        o_ref[...] = acc.astype(o_ref.dtype)

    if kind == "tn":
        a_spec = pl.BlockSpec((k, tm), ij(lambda i, j: (0, i)))
    else:
        a_spec = pl.BlockSpec((tm, k), ij(lambda i, j: (i, 0)))
    if kind == "nt":
        b_spec = pl.BlockSpec((tn, k), ij(lambda i, j: (j, b_kblock)))
    else:
        b_spec = pl.BlockSpec((k, tn), ij(lambda i, j: (0, j)))
    o_spec = pl.BlockSpec((tm, tn), ij(lambda i, j: (i, j)))
    operands, in_specs = [a, b], [a_spec, b_spec]
    if add is not None:
        operands.append(add)
        in_specs.append(o_spec)
    grid = (m // tm, n // tn) if m_outer else (n // tn, m // tm)
    return pl.pallas_call(
        body, grid=grid, in_specs=in_specs, out_specs=o_spec,
        out_shape=jax.ShapeDtypeStruct((m, n), out_dtype), name=name,
        compiler_params=_params(("parallel", "parallel")),
    )(*operands)


def _x_spec(tm, width, cb0, moves):
    return pl.BlockSpec((tm, width), lambda i, j: (i, cb0 + j * moves))


def _full_spec(arr):
    nd = arr.ndim
    return pl.BlockSpec(arr.shape, lambda i, j: (0,) * nd)


def _row_fwd(fn, xs, ps, cs, outs, name, tm=ROW_TILE, ncol=1, rows=None):
    rows = ROWS if rows is None else rows
    nx, npar, nc = len(xs), len(ps), len(cs)

    def body(*refs):
        i = pl.program_id(0)
        xv = [r[...].astype(F32) for r in refs[:nx]]
        pv = [r[...] for r in refs[nx:nx + npar + nc]]
        res = fn(i, *xv, *pv)
        for r, val in zip(refs[nx + npar + nc:], res):
            r[...] = val.astype(r.dtype)

    in_specs = [_x_spec(tm, w, cb0, mv) for (_, w, cb0, mv) in xs]
    in_specs += [_full_spec(p) for p in list(ps) + list(cs)]
    out_specs = [_x_spec(tm, w, 0, 1) for (w, _, _) in outs]
    out_shape = [jax.ShapeDtypeStruct((rows, tot), dt) for (_, tot, dt) in outs]
    return pl.pallas_call(
        body, grid=(rows // tm, ncol), in_specs=in_specs, out_specs=out_specs, out_shape=out_shape,
        name=name, compiler_params=_params(("parallel", "parallel")),
    )(*[x[0] for x in xs], *ps, *cs)


def _row_bwd(fn, xs, ps, cs, cts, dxs, name, adds=None, tm=ROW_TILE, ncol=1, rows=None):
    rows = ROWS if rows is None else rows
    nx, npar, nc, nct = len(xs), len(ps), len(cs), len(cts)
    adds = [None] * nx if adds is None else adds
    add_idx = [k for k in range(nx) if adds[k] is not None]

    def body(*refs):
        i, j = pl.program_id(0), pl.program_id(1)
        pos = 0
        x_refs = refs[pos:pos + nx]; pos += nx
        p_refs = refs[pos:pos + npar]; pos += npar
        c_refs = refs[pos:pos + nc]; pos += nc
        ct_refs = refs[pos:pos + nct]; pos += nct
        add_refs = refs[pos:pos + len(add_idx)]; pos += len(add_idx)
        dx_refs = refs[pos:pos + nx]; pos += nx
        dp_refs = refs[pos:pos + npar]
        xv = [r[...].astype(F32) for r in x_refs]
        pv = [r[...] for r in p_refs]
        cv = [r[...] for r in c_refs]
        _, vjp = jax.vjp(lambda *args: fn(i, *args, *cv), *xv, *pv)
        grads = vjp(tuple(r[...].astype(F32) for r in ct_refs))
        for k in range(nx):
            val = grads[k]
            if adds[k] is not None:
                val = val + add_refs[add_idx.index(k)][...]
            dx_refs[k][...] = val.astype(dx_refs[k].dtype)
        if npar:
            @pl.when((i == 0) & (j == 0))
            def _():
                for r in dp_refs:
                    r[...] = jnp.zeros(r.shape, r.dtype)

            for k in range(npar):
                dp_refs[k][...] += grads[nx + k]

    in_specs = [_x_spec(tm, w, cb0, mv) for (_, w, cb0, mv) in xs]
    in_specs += [_full_spec(p) for p in list(ps) + list(cs)]
    in_specs += [_x_spec(tm, w, cb0, mv) for (_, w, cb0, mv) in cts]
    in_specs += [_x_spec(tm, xs[k][1], 0, 1) for k in add_idx]
    out_specs = [_x_spec(tm, xs[k][1], 0, 1) for k in range(nx)] + [_full_spec(p) for p in ps]
    out_shape = [jax.ShapeDtypeStruct((rows, tot), dt) for (tot, dt) in dxs]
    out_shape += [jax.ShapeDtypeStruct(p.shape, F32) for p in ps]
    sem = ("arbitrary", "arbitrary") if npar else ("parallel", "parallel")
    return pl.pallas_call(
        body, grid=(rows // tm, ncol), in_specs=in_specs, out_specs=out_specs, out_shape=out_shape,
        name=name, compiler_params=_params(sem),
    )(*[x[0] for x in xs], *ps, *cs, *[c[0] for c in cts], *[adds[k] for k in add_idx])


def _rms_fn(i, h, g):
    return (h * lax.rsqrt(jnp.mean(h * h, axis=1, keepdims=True) + NORM_EPS) * g,)


def _fox_pre_fn(i, q, k, f, qg, kg, bf, e_mean, e_expand):
    tm = q.shape[0]

    def headnorm(x, gain, scale):
        msq = _dot_mid(x * x, e_mean, "nn")
        r = _dot_mid(lax.rsqrt(msq + NORM_EPS), e_expand, "nn")
        return x * r * gain * scale

    qn = headnorm(q, qg, FOX_DH ** -0.5)
    kn = headnorm(k, kg, 1.0)
    keep = (_iota((tm, LANE), 1) < FOX_HEADS).astype(F32) * _row_mask(i, tm)
    return qn, kn, _log_sigmoid(f + bf) * keep


def _fox_post_fn(i, o, gate):
    return (o * jax.nn.sigmoid(gate) * _row_mask(i, o.shape[0]),)


def _gla_pre_fn(i, alr, wa, ba):
    z = _dot_ad(alr, wa, "nn") + ba
    return (_log_sigmoid(z) * (1.0 / 16.0) * _row_mask(i, alr.shape[0]),)


def _headwise_norm_gate(o, gate, gain, dv, mask):
    pieces = []
    for h in range(o.shape[1] // dv):
        oh = o[:, h * dv:(h + 1) * dv]
        pieces.append(oh * lax.rsqrt(jnp.mean(oh * oh, axis=1, keepdims=True) + NORM_EPS) * gain)
    return jnp.concatenate(pieces, axis=1) * _silu(gate) * mask


def _gla_post_fn(i, o, r, gain):
    return (_headwise_norm_gate(o, r, gain, GLA_DV, _row_mask(i, o.shape[0])),)


def _gdn_post_fn(i, o, gate, gain):
    return (_headwise_norm_gate(o, gate, gain, GDN_DV, _row_mask(i, o.shape[0])),)


def _gdn_gates_fn(i, sm, alog, dtb):
    tm = sm.shape[0]
    lane = _iota((tm, LANE), 1)
    g = -jnp.exp(alog) * _softplus(sm + dtb)
    beta = jax.nn.sigmoid(sm)
    out = jnp.where(lane < GDN_HEADS, g, jnp.where(lane < 2 * GDN_HEADS, beta, 0.0))
    return (out * _row_mask(i, tm),)


@functools.partial(jax.custom_vjp, nondiff_argnums=(1,))
def _window_rows(xw, start):
    n = xw.shape[0]
    return pltpu.roll(xw, n - start, 0)[:n // 2]


def _window_rows_fwd(xw, start):
    return _window_rows(xw, start), None


def _window_rows_bwd(start, _, ct):
    return (pltpu.roll(jnp.concatenate([ct, jnp.zeros_like(ct)], axis=0), start, 0),)


_window_rows.defvjp(_window_rows_fwd, _window_rows_bwd)


def _gdn_conv_fn(kind):
    def fn(i, prev, cur, w):
        tm = cur.shape[0]
        xw = jnp.concatenate([prev * jnp.where(i > 0, 1.0, 0.0), cur], axis=0)
        y = jnp.zeros(cur.shape, F32)
        for j in range(GDN_CONV):
            tap = jnp.sum(jnp.where(_iota(w.shape, 0) == j, w, 0.0), axis=0, keepdims=True)
            y = y + _window_rows(xw, tm - (GDN_CONV - 1) + j) * tap
        a = _silu(y * _row_mask(i, tm))
        if kind == "v":
            return (a,)
        scale = GDN_DK ** -0.5 if kind == "q" else 1.0
        pieces = []
        for h in range(GDN_HEADS):
            ah = a[:, h * GDN_DK:(h + 1) * GDN_DK]
            pieces.append(ah * lax.rsqrt(jnp.sum(ah * ah, axis=1, keepdims=True) + NORM_EPS) * scale)
        return (jnp.concatenate(pieces, axis=1),)

    return fn


def _seq_cumsum(x, reverse, name, pairs=None):
    rows, width = x.shape
    nt = rows // ROW_TILE

    def body(*refs):
        x_ref, o_ref, carry = refs[0], refs[-2], refs[-1]

        @pl.when(pl.program_id(0) == 0)
        def _():
            carry[...] = jnp.zeros(carry.shape, F32)

        xv = x_ref[...]
        if pairs is not None:
            lane = _iota(xv.shape, 1)
            for g in range(pairs.shape[0]):
                ta, tb = _pair_cols(refs[1][g])
                xv = xv + jnp.where(lane == 2 * g, ta, 0.0) + jnp.where(lane == 2 * g + 1, tb, 0.0)
        r = _iota((ROW_TILE, ROW_TILE), 0)
        c = _iota((ROW_TILE, ROW_TILE), 1)
        tri = ((c >= r) if reverse else (c <= r)).astype(F32)
        acc = _dot_hi(tri, xv, "nn") + carry[...]
        o_ref[...] = acc
        edge = 0 if reverse else ROW_TILE - 1
        carry[...] = jnp.sum(jnp.where(_iota(acc.shape, 0) == edge, acc, 0.0), axis=0, keepdims=True)

    idx = (lambda i: (nt - 1 - i, 0)) if reverse else (lambda i: (i, 0))
    idx3 = (lambda i: (0, nt - 1 - i, 0)) if reverse else (lambda i: (0, i, 0))
    operands, in_specs = [x], [pl.BlockSpec((ROW_TILE, width), idx)]
    if pairs is not None:
        operands.append(pairs)
        in_specs.append(pl.BlockSpec((pairs.shape[0], ROW_TILE, LANE), idx3))
    return pl.pallas_call(
        body, grid=(nt,), in_specs=in_specs,
        out_specs=pl.BlockSpec((ROW_TILE, width), idx), out_shape=jax.ShapeDtypeStruct(x.shape, F32),
        scratch_shapes=[pltpu.VMEM((1, width), F32)], name=name, compiler_params=_params(("arbitrary",)),
    )(*operands)


FOX_PAIRS = FOX_HEADS // 2
FOX_V_BLOCK = 2 * D_MODEL // LANE


def _pair_cols(x):
    return _lane_col(x, 0), _lane_col(x, FOX_DH)


ATT_TILE = 384
assert ROWS % ATT_TILE == 0
N_ATT = ROWS // ATT_TILE
KEY_OFF = 1e30


def _fox_scores(qa, qb, kj, ca, cb, cra, crb, mask):
    sa = _dot(qa, kj, "nt") + (ca - cra)
    sb = _dot(qb, kj, "nt") + (cb - crb)
    if mask is None:
        return sa, sb
    return jnp.where(mask, sa, NEG), jnp.where(mask, sb, NEG)


def _fox_diag_mask():
    return _iota((ATT_TILE, ATT_TILE), 1) <= _iota((ATT_TILE, ATT_TILE), 0)


def _att_off(j):
    return pl.multiple_of(j * ATT_TILE, ATT_TILE)


def _fox_attn_fwd(qn, kn, proj, c2, crow):
    def body(q_ref, k_ref, v_ref, c2_ref, crow_ref, o_ref, lse_ref):
        i = pl.program_id(1)
        q = q_ref[...]
        is_a = _iota((ATT_TILE, LANE), 1) < FOX_DH
        qa, qb = jnp.where(is_a, q, 0), jnp.where(is_a, 0, q)
        ca, cb = _pair_cols(c2_ref[0])

        def block(off, size, mask, carry):
            ma, la, acca, mb, lb, accb = carry
            kj = k_ref[pl.ds(off, size), :]
            vj = v_ref[pl.ds(off, size), :].astype(BF16)
            sa, sb = _fox_scores(qa, qb, kj, ca, cb, crow_ref[0, 0:1, pl.ds(off, size)],
                                 crow_ref[0, 1:2, pl.ds(off, size)], mask)

            def online(s, m, l, acc):
                m_new = jnp.maximum(m, jnp.max(s, axis=1, keepdims=True))
                p = jnp.exp(s - m_new)
                alpha = jnp.exp(m - m_new)
                return m_new, alpha * l + jnp.sum(p, axis=1, keepdims=True), alpha * acc + _dot(p, vj, "nn")

            return online(sa, ma, la, acca) + online(sb, mb, lb, accb)

        m0 = jnp.full((ATT_TILE, 1), NEG, F32)
        l0 = jnp.zeros((ATT_TILE, 1), F32)
        a0 = jnp.zeros((ATT_TILE, LANE), F32)
        carry = lax.fori_loop(0, i, lambda j, cr: block(_att_off(j), ATT_TILE, None, cr), (m0, l0, a0, m0, l0, a0))
        ma, la, acca, mb, lb, accb = block(_att_off(i), ATT_TILE, _fox_diag_mask(), carry)
        o_ref[...] = jnp.where(is_a, acca / la, accb / lb)
        lse_ref[0] = jnp.where(is_a, ma + jnp.log(la), mb + jnp.log(lb))

    return pl.pallas_call(
        body, grid=(FOX_PAIRS, N_ATT),
        in_specs=[
            pl.BlockSpec((ATT_TILE, LANE), lambda g, i: (i, g)),
            pl.BlockSpec((ROWS, LANE), lambda g, i: (0, g)),
            pl.BlockSpec((ROWS, LANE), lambda g, i: (0, FOX_V_BLOCK + g)),
            pl.BlockSpec((1, ATT_TILE, LANE), lambda g, i: (g, i, 0)),
            pl.BlockSpec((1, 2, ROWS), lambda g, i: (g, 0, 0)),
        ],
        out_specs=[
            pl.BlockSpec((ATT_TILE, LANE), lambda g, i: (i, g)),
            pl.BlockSpec((1, ATT_TILE, LANE), lambda g, i: (g, i, 0)),
        ],
        out_shape=[jax.ShapeDtypeStruct((ROWS, D_MODEL), F32), jax.ShapeDtypeStruct((FOX_PAIRS, ROWS, LANE), F32)],
        name="fox_attn_fwd", compiler_params=_params(("parallel", "arbitrary")),
    )(qn, kn, proj, c2, crow)


def _fox_attn_bwd(qn, kn, proj, c2, crow, lse2, o, do, blockdiag):
    def body(q_ref, k_ref, v_ref, c2_ref, crow_ref, lse_ref, o_ref, do_ref, bd_ref,
             dq_ref, dk_ref, dv_ref, dc_ref, dcq_ref, delta_ref):
        j = pl.program_id(1)

        @pl.when(j == 0)
        def _():
            for t in range(N_ATT):
                rows = pl.ds(t * ATT_TILE, ATT_TILE)
                delta_ref[rows, :] = _dot_hi(do_ref[rows, :] * o_ref[rows, :], bd_ref[...], "nn")
            dq_ref[...] = jnp.zeros(dq_ref.shape, F32)
            dcq_ref[...] = jnp.zeros(dcq_ref.shape, F32)

        kj = k_ref[...]
        vj = v_ref[...].astype(BF16)
        cra, crb = crow_ref[0, 0:1, :], crow_ref[0, 1:2, :]
        is_a = _iota((ATT_TILE, LANE), 1) < FOX_DH

        def block(off, mask, carry):
            dk, dv, dca, dcb = carry
            rows = pl.ds(off, ATT_TILE)
            q = q_ref[rows, :]
            qa, qb = jnp.where(is_a, q, 0), jnp.where(is_a, 0, q)
            do_t = do_ref[rows, :]
            doa, dob = jnp.where(is_a, do_t, 0.0).astype(BF16), jnp.where(is_a, 0.0, do_t).astype(BF16)
            ca, cb = _pair_cols(c2_ref[0, rows, :])
            lsa, lsb = _pair_cols(lse_ref[0, rows, :])
            dla, dlb = _pair_cols(delta_ref[rows, :])
            sa, sb = _fox_scores(qa, qb, kj, ca, cb, cra, crb, mask)
            pa, pb = jnp.exp(sa - lsa), jnp.exp(sb - lsb)
            dsa = pa * (_dot(doa, vj, "nt") - dla)
            dsb = pb * (_dot(dob, vj, "nt") - dlb)
            dq_ref[rows, :] += jnp.where(is_a, _dot(dsa, kj, "nn"), _dot(dsb, kj, "nn"))
            dcq_ref[0, rows, :] += jnp.where(is_a, jnp.sum(dsa, axis=1, keepdims=True),
                                             jnp.sum(dsb, axis=1, keepdims=True))
            dv = dv + _dot(pa, doa, "tn") + _dot(pb, dob, "tn")
            dk = dk + _dot(dsa, qa, "tn") + _dot(dsb, qb, "tn")
            dca = dca - jnp.sum(dsa, axis=0, keepdims=True)
            dcb = dcb - jnp.sum(dsb, axis=0, keepdims=True)
            return dk, dv, dca, dcb

        z = jnp.zeros((ATT_TILE, LANE), F32)
        zr = jnp.zeros((1, ATT_TILE), F32)
        carry = block(_att_off(j), _fox_diag_mask(), (z, z, zr, zr))
        dk, dv, dca, dcb = lax.fori_loop(j + 1, N_ATT, lambda i, cr: block(_att_off(i), None, cr), carry)
        dk_ref[...] = dk
        dv_ref[...] = dv.astype(dv_ref.dtype)
        dc_ref[0, 0:1, :] = dca
        dc_ref[0, 1:2, :] = dcb

    whole = pl.BlockSpec((ROWS, LANE), lambda g, j: (0, g))
    pair_whole = pl.BlockSpec((1, ROWS, LANE), lambda g, j: (g, 0, 0))
    tile = pl.BlockSpec((ATT_TILE, LANE), lambda g, j: (j, g))
    key_rows = pl.BlockSpec((1, 2, ATT_TILE), lambda g, j: (g, 0, j))
    return pl.pallas_call(
        body, grid=(FOX_PAIRS, N_ATT),
        in_specs=[
            whole, tile,
            pl.BlockSpec((ATT_TILE, LANE), lambda g, j: (j, FOX_V_BLOCK + g)),
            pair_whole, key_rows, pair_whole, whole, whole,
            pl.BlockSpec((LANE, LANE), lambda g, j: (0, 0)),
        ],
        out_specs=[whole, tile, tile, key_rows, pair_whole],
        out_shape=[jax.ShapeDtypeStruct((ROWS, D_MODEL), F32), jax.ShapeDtypeStruct((ROWS, D_MODEL), F32),
                   jax.ShapeDtypeStruct((ROWS, D_MODEL), BF16), jax.ShapeDtypeStruct((FOX_PAIRS, 2, ROWS), F32),
                   jax.ShapeDtypeStruct((FOX_PAIRS, ROWS, LANE), F32)],
        scratch_shapes=[pltpu.VMEM((ROWS, LANE), F32)],
        name="fox_attn_bwd", compiler_params=_params(("parallel", "arbitrary")),
    )(qn, kn, proj, c2, crow, lse2, o, do, blockdiag)


def _gla_chunk(q, k, v, g, st, dot):
    n = len(q)
    c = q[0].shape[0]
    incl = _iota((c, c), 1) <= _iota((c, c), 0)
    incl_f = incl.astype(F32)
    b = [_dot_hi(incl_f, g[i], "nn") for i in range(n)]
    qe = [q[i] * (GLA_DK ** -0.5) * jnp.exp(b[i]) for i in range(n)]
    ke = [k[i] * jnp.exp(-b[i]) for i in range(n)]
    att = [jnp.where(incl, dot(qe[i], ke[i], "nt"), 0.0) for i in range(n)]
    o_intra = [dot(att[i], v[i], "nn") for i in range(n)]
    o_inter = [dot(qe[i], st[i], "nt") for i in range(n)]
    b_last = [_last_row(b[i]) for i in range(n)]
    kd = [k[i] * jnp.exp(b_last[i] - b[i]) for i in range(n)]
    upd = [dot(v[i], kd[i], "tn") for i in range(n)]
    return ([o_intra[i] + o_inter[i] for i in range(n)],
            [st[i] * jnp.exp(b_last[i]) + upd[i] for i in range(n)])


def _tri_inverse(low):
    c = low[0].shape[0]
    eye = (_iota((c, c), 0) == _iota((c, c), 1)).astype(F32)
    x, p = [eye - l for l in low], list(low)
    span = 2
    while span < c:
        p = [_dot_mid(pi, pi, "nn") for pi in p]
        x = [_dot_mid(xi, eye + pi, "nn") for xi, pi in zip(x, p)]
        span *= 2
    return x


def _solve(low, rhs):
    return [_dot_mid(inv, r, "nn") for inv, r in zip(_tri_inverse(low), rhs)]


@jax.custom_vjp
def _solve_ad(low, rhs):
    return _solve(low, rhs)


def _solve_ad_fwd(low, rhs):
    inv = _tri_inverse(low)
    sol = [_dot_mid(i, r, "nn") for i, r in zip(inv, rhs)]
    return sol, (inv, sol)


def _solve_ad_bwd(res, ct):
    inv, sol = res
    drhs = [_dot_mid(i, c, "tn") for i, c in zip(inv, ct)]
    return [-_dot_mid(d, s, "nt") for d, s in zip(drhs, sol)], drhs


_solve_ad.defvjp(_solve_ad_fwd, _solve_ad_bwd)


def _gdn_chunk(q, k, v, gates, st, heads, dot, solve):
    n = len(q)
    c = q[0].shape[0]
    incl = _iota((c, c), 1) <= _iota((c, c), 0)
    strict = _iota((c, c), 1) < _iota((c, c), 0)
    incl_f = incl.astype(F32)
    upper_f = (_iota((c, c), 0) <= _iota((c, c), 1)).astype(F32)
    gcol = [_lane_col(gates, h) for h in heads]
    beta = [_lane_col(gates, h + GDN_HEADS) for h in heads]
    b = [_dot_hi(incl_f, jnp.broadcast_to(gc, (c, LANE)), "nn") for gc in gcol]
    g_sq = [jnp.broadcast_to(gc, (c, c)) for gc in gcol]
    b_col = [_dot_hi(incl_f, gs, "nn") for gs in g_sq]
    b_row = [_dot_hi(gs, upper_f, "tn") for gs in g_sq]
    diff = [b_col[i] - b_row[i] for i in range(n)]
    kb = [k[i] * beta[i] for i in range(n)]
    vb = [v[i] * beta[i] for i in range(n)]
    low = [dot(kb[i], k[i], "nt") * jnp.exp(jnp.where(strict, diff[i], NEG)) for i in range(n)]
    eb = [jnp.exp(bi) for bi in b]
    sol = solve(low, [jnp.concatenate([vb[i], kb[i] * eb[i]], axis=1) for i in range(n)])
    w_st = [dot(sol[i][:, GDN_DV:], st[i], "nn") for i in range(n)]
    v_new = [sol[i][:, :GDN_DV] - w_st[i] for i in range(n)]
    att = [dot(q[i], k[i], "nt") * jnp.exp(jnp.where(incl, diff[i], NEG)) for i in range(n)]
    o_intra = [dot(att[i], v_new[i], "nn") for i in range(n)]
    o_inter = [dot(q[i] * eb[i], st[i], "nn") for i in range(n)]
    b_last = [_last_row(bi) for bi in b]
    kd = [k[i] * jnp.exp(b_last[i] - b[i]) for i in range(n)]
    upd = [dot(kd[i], v_new[i], "tn") for i in range(n)]
    return ([o_intra[i] + o_inter[i] for i in range(n)],
            [st[i] * jnp.exp(b_last[i]) + upd[i] for i in range(n)])


N_CHUNKS = ROWS // CHUNK
GDN_GROUP = 8


def _head_lanes(width, n):
    return [slice(i * width, (i + 1) * width) for i in range(n)]


def _gla_scan_fwd(proj, garr):
    kl, vl = _head_lanes(GLA_DK, GLA_HEADS), _head_lanes(GLA_DV, GLA_HEADS)

    def body(q_ref, k_ref, v_ref, g_ref, o_ref, states_ref, st):
        @pl.when(pl.program_id(0) == 0)
        def _():
            st[...] = jnp.zeros(st.shape, F32)

        s_in = [st[i] for i in range(GLA_HEADS)]
        o, s_out = _gla_chunk([q_ref[:, ln] for ln in kl], [k_ref[:, ln] for ln in kl], [v_ref[:, ln] for ln in vl],
                              [g_ref[:, ln] for ln in kl], s_in, _dot)
        for i in range(GLA_HEADS):
            states_ref[i, 0] = s_in[i]
            o_ref[:, vl[i]] = o[i]
            st[i] = s_out[i]

    return pl.pallas_call(
        body, grid=(N_CHUNKS,),
        in_specs=[
            pl.BlockSpec((CHUNK, GLA_QK), lambda c: (c, 0)),
            pl.BlockSpec((CHUNK, GLA_QK), lambda c: (c, 1)),
            pl.BlockSpec((CHUNK, GLA_V), lambda c: (c, 1)),
            pl.BlockSpec((CHUNK, GLA_QK), lambda c: (c, 0)),
        ],
        out_specs=[
            pl.BlockSpec((CHUNK, GLA_V), lambda c: (c, 0)),
            pl.BlockSpec((GLA_HEADS, 1, GLA_DV, GLA_DK), lambda c: (0, c, 0, 0)),
        ],
        out_shape=[jax.ShapeDtypeStruct((ROWS, GLA_V), F32),
                   jax.ShapeDtypeStruct((GLA_HEADS, N_CHUNKS, GLA_DV, GLA_DK), F32)],
        scratch_shapes=[pltpu.VMEM((GLA_HEADS, GLA_DV, GLA_DK), F32)],
        name="gla_scan_fwd", compiler_params=_params(("arbitrary",)),
    )(proj, proj, proj, garr)


def _gla_scan_bwd(proj, garr, states, do):
    last = N_CHUNKS - 1
    kl, vl = _head_lanes(GLA_DK, GLA_HEADS), _head_lanes(GLA_DV, GLA_HEADS)

    def body(q_ref, k_ref, v_ref, g_ref, s_ref, do_ref, dq_ref, dk_ref, dv_ref, dg_ref, dst):
        @pl.when(pl.program_id(0) == 0)
        def _():
            dst[...] = jnp.zeros(dst.shape, F32)

        _, vjp = jax.vjp(lambda q, k, v, g, s: _gla_chunk(q, k, v, g, s, _dot_ad),
                         [q_ref[:, ln] for ln in kl], [k_ref[:, ln] for ln in kl], [v_ref[:, ln] for ln in vl],
                         [g_ref[:, ln] for ln in kl], [s_ref[i, 0] for i in range(GLA_HEADS)])
        dq, dk, dv, dg, ds = vjp(([do_ref[:, ln] for ln in vl], [dst[i] for i in range(GLA_HEADS)]))
        for i in range(GLA_HEADS):
            dq_ref[:, kl[i]] = dq[i].astype(dq_ref.dtype)
            dk_ref[:, kl[i]] = dk[i].astype(dk_ref.dtype)
            dv_ref[:, vl[i]] = dv[i].astype(dv_ref.dtype)
            dg_ref[:, kl[i]] = dg[i]
            dst[i] = ds[i]

    kspec = pl.BlockSpec((CHUNK, GLA_QK), lambda c: (last - c, 0))
    vspec = pl.BlockSpec((CHUNK, GLA_V), lambda c: (last - c, 0))
    return pl.pallas_call(
        body, grid=(N_CHUNKS,),
        in_specs=[
            kspec,
            pl.BlockSpec((CHUNK, GLA_QK), lambda c: (last - c, 1)),
            pl.BlockSpec((CHUNK, GLA_V), lambda c: (last - c, 1)),
            kspec,
            pl.BlockSpec((GLA_HEADS, 1, GLA_DV, GLA_DK), lambda c: (0, last - c, 0, 0)),
            vspec,
        ],
        out_specs=[kspec, kspec, vspec, kspec],
        out_shape=[jax.ShapeDtypeStruct((ROWS, GLA_QK), BF16), jax.ShapeDtypeStruct((ROWS, GLA_QK), BF16),
                   jax.ShapeDtypeStruct((ROWS, GLA_V), BF16), jax.ShapeDtypeStruct((ROWS, GLA_QK), F32)],
        scratch_shapes=[pltpu.VMEM((GLA_HEADS, GLA_DV, GLA_DK), F32)],
        name="gla_scan_bwd", compiler_params=_params(("arbitrary",)),
    )(proj, proj, proj, garr, states, do)


def _gdn_scan_fwd(qn, kn, vn, gates):
    def body(q_ref, k_ref, v_ref, g_ref, o_ref, states_ref, st):
        @pl.when((pl.program_id(0) == 0) & (pl.program_id(1) == 0))
        def _():
            st[...] = jnp.zeros(st.shape, F32)

        heads = [pl.program_id(1) * GDN_GROUP + hh for hh in range(GDN_GROUP)]
        lanes = _head_lanes(GDN_DK, GDN_GROUP)
        s_in = [st[h] for h in heads]
        o, s_out = _gdn_chunk([q_ref[:, ln] for ln in lanes], [k_ref[:, ln] for ln in lanes],
                              [v_ref[:, ln] for ln in lanes], g_ref[...], s_in, heads, _dot, _solve)
        for hh in range(GDN_GROUP):
            states_ref[hh, 0] = s_in[hh]
            o_ref[:, lanes[hh]] = o[hh]
            st[heads[hh]] = s_out[hh]

    hspec = pl.BlockSpec((CHUNK, GDN_GROUP * GDN_DK), lambda c, h: (c, h))
    return pl.pallas_call(
        body, grid=(N_CHUNKS, GDN_HEADS // GDN_GROUP),
        in_specs=[hspec, hspec, hspec, pl.BlockSpec((CHUNK, LANE), lambda c, h: (c, 0))],
        out_specs=[hspec, pl.BlockSpec((GDN_GROUP, 1, GDN_DK, GDN_DV), lambda c, h: (h, c, 0, 0))],
        out_shape=[jax.ShapeDtypeStruct((ROWS, D_MODEL), F32),
                   jax.ShapeDtypeStruct((GDN_HEADS, N_CHUNKS, GDN_DK, GDN_DV), F32)],
        scratch_shapes=[pltpu.VMEM((GDN_HEADS, GDN_DK, GDN_DV), F32)],
        name="gdn_scan_fwd", compiler_params=_params(("arbitrary", "arbitrary")),
    )(qn, kn, vn, gates)


def _gdn_scan_bwd(qn, kn, vn, gates, states, do):
    last = N_CHUNKS - 1

    def body(q_ref, k_ref, v_ref, g_ref, s_ref, do_ref, dq_ref, dk_ref, dv_ref, dg_ref, dst):
        @pl.when((pl.program_id(0) == 0) & (pl.program_id(1) == 0))
        def _():
            dst[...] = jnp.zeros(dst.shape, F32)

        @pl.when(pl.program_id(1) == 0)
        def _():
            dg_ref[...] = jnp.zeros(dg_ref.shape, F32)

        heads = [pl.program_id(1) * GDN_GROUP + hh for hh in range(GDN_GROUP)]
        lanes = _head_lanes(GDN_DK, GDN_GROUP)
        _, vjp = jax.vjp(lambda q, k, v, g, s: _gdn_chunk(q, k, v, g, s, heads, _dot_ad, _solve_ad),
                         [q_ref[:, ln] for ln in lanes], [k_ref[:, ln] for ln in lanes],
                         [v_ref[:, ln] for ln in lanes], g_ref[...], [s_ref[hh, 0] for hh in range(GDN_GROUP)])
        dq, dk, dv, dg, ds = vjp(([do_ref[:, ln] for ln in lanes], [dst[h] for h in heads]))
        for hh in range(GDN_GROUP):
            dq_ref[:, lanes[hh]] = dq[hh]
            dk_ref[:, lanes[hh]] = dk[hh]
            dv_ref[:, lanes[hh]] = dv[hh]
            dst[heads[hh]] = ds[hh]
        dg_ref[...] += dg

    hspec = pl.BlockSpec((CHUNK, GDN_GROUP * GDN_DK), lambda c, h: (last - c, h))
    gspec = pl.BlockSpec((CHUNK, LANE), lambda c, h: (last - c, 0))
    return pl.pallas_call(
        body, grid=(N_CHUNKS, GDN_HEADS // GDN_GROUP),
        in_specs=[hspec, hspec, hspec, gspec,
                  pl.BlockSpec((GDN_GROUP, 1, GDN_DK, GDN_DV), lambda c, h: (h, last - c, 0, 0)), hspec],
        out_specs=[hspec, hspec, hspec, gspec],
        out_shape=[jax.ShapeDtypeStruct((ROWS, D_MODEL), F32)] * 3 + [jax.ShapeDtypeStruct((ROWS, LANE), F32)],
        scratch_shapes=[pltpu.VMEM((GDN_HEADS, GDN_DK, GDN_DV), F32)],
        name="gdn_scan_bwd", compiler_params=_params(("arbitrary", "arbitrary")),
    )(qn, kn, vn, gates, states, do)


def _gdn_conv_fwd(proj, w, kind, cb):
    fn = _gdn_conv_fn(kind)
    nt = ROWS // ROW_TILE

    def body(prev_ref, cur_ref, w_ref, o_ref):
        (out,) = fn(pl.program_id(0), prev_ref[...], cur_ref[...], w_ref[...])
        o_ref[...] = out

    return pl.pallas_call(
        body, grid=(nt,),
        in_specs=[pl.BlockSpec((ROW_TILE, D_MODEL), lambda i: (jnp.maximum(i - 1, 0), cb)),
                  pl.BlockSpec((ROW_TILE, D_MODEL), lambda i: (i, cb)),
                  pl.BlockSpec(w.shape, lambda i: (0, 0))],
        out_specs=pl.BlockSpec((ROW_TILE, D_MODEL), lambda i: (i, 0)),
        out_shape=jax.ShapeDtypeStruct((ROWS, D_MODEL), F32),
        name="gdn_conv_fwd_" + kind, compiler_params=_params(("parallel",)),
    )(proj, proj, w)


def _gdn_conv_bwd(proj, w, kind, cb, ct):
    fn = _gdn_conv_fn(kind)
    nt = ROWS // ROW_TILE

    def body(prev_ref, cur_ref, w_ref, ct_ref, dx_ref, dw_ref, carry):
        step = pl.program_id(0)
        i = nt - 1 - step

        @pl.when(step == 0)
        def _():
            carry[...] = jnp.zeros(carry.shape, F32)
            dw_ref[...] = jnp.zeros(dw_ref.shape, F32)

        _, vjp = jax.vjp(lambda p, c, ww: fn(i, p, c, ww), prev_ref[...], cur_ref[...], w_ref[...])
        dprev, dcur, dw = vjp((ct_ref[...],))
        dx_ref[...] = (dcur + carry[...]).astype(dx_ref.dtype)
        carry[...] = dprev
        dw_ref[...] += dw

    return pl.pallas_call(
        body, grid=(nt,),
        in_specs=[pl.BlockSpec((ROW_TILE, D_MODEL), lambda s: (jnp.maximum(nt - 2 - s, 0), cb)),
                  pl.BlockSpec((ROW_TILE, D_MODEL), lambda s: (nt - 1 - s, cb)),
                  pl.BlockSpec(w.shape, lambda s: (0, 0)),
                  pl.BlockSpec((ROW_TILE, D_MODEL), lambda s: (nt - 1 - s, 0))],
        out_specs=[pl.BlockSpec((ROW_TILE, D_MODEL), lambda s: (nt - 1 - s, 0)),
                   pl.BlockSpec(w.shape, lambda s: (0, 0))],
        out_shape=[jax.ShapeDtypeStruct((ROWS, D_MODEL), BF16), jax.ShapeDtypeStruct(w.shape, F32)],
        scratch_shapes=[pltpu.VMEM((ROW_TILE, D_MODEL), F32)],
        name="gdn_conv_bwd_" + kind, compiler_params=_params(("arbitrary",)),
    )(proj, proj, w, ct)


def _loss_head(h, target):
    nt = ROWS // ROW_TILE
    lead = PAD_ROWS // ROW_TILE

    def body(h_ref, t_ref, dh_ref, acc_ref):
        i = pl.program_id(0)

        @pl.when(i == 0)
        def _():
            acc_ref[...] = jnp.zeros(acc_ref.shape, F32)

        err = (h_ref[...] - t_ref[...]) * jnp.where(i >= lead, 1.0, 0.0)
        dh_ref[...] = err * (1.0 / D_MODEL)
        acc_ref[...] += jnp.sum(err * err, axis=0, keepdims=True)

    return pl.pallas_call(
        body, grid=(nt,),
        in_specs=[pl.BlockSpec((ROW_TILE, D_MODEL), lambda i: (i, 0)),
                  pl.BlockSpec((ROW_TILE, D_MODEL), lambda i: (jnp.maximum(i - lead, 0), 0))],
        out_specs=[pl.BlockSpec((ROW_TILE, D_MODEL), lambda i: (i, 0)),
                   pl.BlockSpec((1, D_MODEL), lambda i: (0, 0))],
        out_shape=[jax.ShapeDtypeStruct((ROWS, D_MODEL), F32), jax.ShapeDtypeStruct((1, D_MODEL), F32)],
        name="loss_head", compiler_params=_params(("arbitrary",)),
    )(h, target)


MESH = pl.DeviceIdType.MESH
ANY = pl.BlockSpec(memory_space=pl.ANY)


N_PEERS = N_DEV - 1


HBM_SPEC = pl.BlockSpec(memory_space=pltpu.HBM)
SEM_SPEC = pl.BlockSpec(memory_space=pltpu.SEMAPHORE)
SIDE_EFFECT = pltpu.SideEffectType.DATAFLOW_SIDE_EFFECTING


def _me_and_peers():
    x, y, c = lax.axis_index("x"), lax.axis_index("y"), lax.axis_index("c")
    peers = []
    for k in range(1, N_DEV):
        pid = (1 - x if k & 4 else x, 1 - y if k & 2 else y, 1 - c if k & 1 else c)
        peers.append((4 * pid[0] + 2 * pid[1] + pid[2], pid))
    return 4 * x + 2 * y + c, peers


def _split_copies(src_refs, land_refs, send_sems, recv_sems, spread):
    me, peers = _me_and_peers()
    out = []
    for t, (src, land) in enumerate(zip(src_refs, land_refs)):
        for k, (peer, pid) in enumerate(peers):
            sems = dict(send_sem=send_sems.at[N_PEERS * t + k], recv_sem=recv_sems.at[N_PEERS * t + k],
                        device_id=pid, device_id_type=MESH)
            piece = src if spread else src.at[peer]
            out.append((pltpu.make_async_remote_copy(src_ref=piece, dst_ref=land.at[me], **sems),
                        pltpu.make_async_remote_copy(src_ref=piece, dst_ref=land.at[peer], **sems)))
    return out


def _send_start(srcs, spread, name):
    n = len(srcs)
    lands = [lax.empty((N_DEV,) + (s.shape if spread else s.shape[1:]), s.dtype) for s in srcs]

    def body(*refs):
        for out, _ in _split_copies(refs[:n], refs[n:2 * n], refs[2 * n], refs[2 * n + 1], spread):
            out.start()
        refs[-1][...] = jnp.zeros(refs[-1].shape, F32)

    thru = [pltpu.HBM(a.shape, a.dtype) for a in list(srcs) + lands]
    res = pl.pallas_call(
        body, name=name,
        out_shape=(pltpu.SemaphoreType.DMA((N_PEERS * n,)), pltpu.SemaphoreType.DMA((N_PEERS * n,)), *thru,
                   jax.ShapeDtypeStruct((8, LANE), F32)),
        in_specs=[HBM_SPEC] * (2 * n),
        out_specs=(SEM_SPEC, SEM_SPEC, *[HBM_SPEC] * (2 * n), pl.BlockSpec(memory_space=pltpu.VMEM)),
        input_output_aliases={t: 2 + t for t in range(2 * n)},
        compiler_params=pltpu.CompilerParams(has_side_effects=SIDE_EFFECT),
    )(*[pltpu.with_memory_space_constraint(a, pltpu.HBM) for a in list(srcs) + lands])
    return res[0], res[1], list(res[2:2 + n]), list(res[2 + n:2 + 2 * n]), res[-1]


def _send_wait(started, spread, after, name):
    send_sems, recv_sems, srcs, lands, _ = started
    n = len(srcs)

    def body(*refs):
        for out, arrival in _split_copies(refs[:n], refs[n:2 * n], refs[2 * n], refs[2 * n + 1], spread):
            out.wait_send()
            arrival.wait_recv()

    thru = [pltpu.HBM(a.shape, a.dtype) for a in srcs + lands]
    res = pl.pallas_call(
        body, name=name, out_shape=tuple(thru),
        in_specs=[HBM_SPEC] * (2 * n) + [SEM_SPEC, SEM_SPEC, ANY], out_specs=tuple([HBM_SPEC] * (2 * n)),
        input_output_aliases={t: t for t in range(2 * n)},
        compiler_params=pltpu.CompilerParams(has_side_effects=SIDE_EFFECT),
    )(*srcs, *lands, send_sems, recv_sems, after)
    return list(res[n:])


def _adamw(recvs, w, m, v, name):
    n_layers = len(recvs)
    rows, cols = recvs[0].shape[1:]
    tr = _pick(rows, (256, 128, 64, 32, 16))
    nb = rows // tr

    def body(*refs):
        r_refs = refs[:n_layers]
        w_ref, m_ref, v_ref, g_ref, d_ref, m2_ref, v2_ref = refs[n_layers:]

        def update(r_ref):
            g = r_ref[0].astype(F32)
            for k in range(1, N_DEV):
                g = g + r_ref[k].astype(F32)
            m2 = ADAM_B1 * m_ref[...] + (1.0 - ADAM_B1) * g
            v2 = ADAM_B2 * v_ref[...] + (1.0 - ADAM_B2) * (g * g)
            m_hat = m2 / (1.0 - ADAM_B1 ** ADAM_STEP)
            v_hat = v2 / (1.0 - ADAM_B2 ** ADAM_STEP)
            g_ref[...] = g
            d_ref[...] = -ADAM_LR * (m_hat / (jnp.sqrt(v_hat) + ADAM_EPS) + ADAM_WD * w_ref[...])
            m2_ref[...] = m2
            v2_ref[...] = v2

        if n_layers == 1:
            update(r_refs[0])
        else:
            for layer in range(n_layers):
                pl.when(pl.program_id(0) == layer)(functools.partial(update, r_refs[layer]))

    def recv_spec(layer):
        return pl.BlockSpec((N_DEV, tr, cols), lambda a, i: (0, jnp.where(a == layer, i, 0), 0))

    spec = pl.BlockSpec((tr, cols), lambda a, i: (a * nb + i, 0))
    return pl.pallas_call(
        body, grid=(n_layers, nb),
        in_specs=[recv_spec(layer) for layer in range(n_layers)] + [spec, spec, spec],
        out_specs=[spec] * 4, out_shape=[jax.ShapeDtypeStruct(w.shape, F32)] * 4,
        name=name, compiler_params=_params(("arbitrary", "arbitrary")),
    )(*recvs, w, m, v)


def _padded_rows(n, mult):
    rows = -(-n // PACK_COLS)
    return -(-rows // mult) * mult


def _pack(flat_pieces, mult, dtype, lead=()):
    flat = jnp.concatenate([p.astype(dtype) for p in flat_pieces], axis=-1)
    n = flat.shape[-1]
    rows = _padded_rows(n, mult)
    flat = jnp.pad(flat, [(0, 0)] * len(lead) + [(0, rows * PACK_COLS - n)])
    return flat.reshape(lead + (rows, PACK_COLS))


def _unpack(packed, shapes):
    flat = packed.reshape(-1)
    out, off = {}, 0
    for name, shape in shapes:
        size = 1
        for s in shape:
            size *= s
        out[name] = flat[off:off + size].reshape(shape)
        off += size
    return out


def _local_2d(arr):
    return arr.reshape(-1, arr.shape[-1])


SHARD_2D = {n: (s[-2] if len(s) > 2 else s[0], s[-1], a == len(s) - 1) for n, s, a in SHARDED}
SHARD_2D["gdn_conv_w"] = (GDN_CONV, GDN_CONV_DIM // N_DEV, True)


def _wire(name):
    rows, cols, _ = SHARD_2D[name]
    return BF16 if rows * cols >= 1 << 16 else F32


def _layer_tensors(i):
    kind, j = i % 3, i // 3
    pre = ("fox", "gla", "gdn")[kind]
    mix = [(pre + "_w_in", j), (pre + "_w_out", j)]
    if kind == 1:
        mix.append(("gla_w_alpha2", j))
    if kind == 2:
        mix.append(("gdn_conv_w", j))
    if i == 0:
        mix.append(("meta_tokens", None))
    return {"mix": mix, "ffn": [("w_gate_up", i), ("w_down", i)]}


PARTS = ("mix", "ffn")


def _layer_block(arr, idx):
    return arr if idx is None else _local_2d(arr[idx])


def _to_slabs(full2d, name):
    rows, cols, by_cols = SHARD_2D[name]
    if by_cols:
        parts = full2d if isinstance(full2d, tuple) else (full2d,)
        per = parts[0].shape[1] // cols
        return jnp.stack([parts[d // per][:, cols * (d % per):cols * (d % per + 1)] for d in range(N_DEV)])
    return full2d.reshape(N_DEV, rows, cols)


def _from_gathered(g, name):
    rows, cols, by_cols = SHARD_2D[name]
    if by_cols:
        return jnp.concatenate([g[d] for d in range(N_DEV)], axis=1)
    return g.reshape(N_DEV * rows, cols)


def _pad_cols(w, total):
    return jnp.pad(w, [(0, 0)] * (w.ndim - 1) + [(0, total - w.shape[-1])])


def _pad_lanes(v):
    return _pad_cols(v, LANE)


def _x(arr, width, cb0=0, moves=0):
    return (arr, width, cb0, moves)


def _rms_fwd(h, g):
    return _row_fwd(_rms_fn, [_x(h, D_MODEL)], [g], [], [(D_MODEL, D_MODEL, BF16)], "rms_fwd")[0]


FUSED_ROWS = 384


def _matmul_add_norm(a, b, add, g, name):
    m, k = a.shape
    n = b.shape[1]

    def body(a_ref, b_ref, add_ref, g_ref, h_ref, y_ref):
        h2 = _dot(a_ref[...], b_ref[...], "nn") + add_ref[...]
        h_ref[...] = h2
        y_ref[...] = _rms_fn(0, h2, g_ref[...])[0].astype(y_ref.dtype)

    rows = pl.BlockSpec((FUSED_ROWS, n), lambda i: (i, 0))
    return pl.pallas_call(
        body, grid=(m // FUSED_ROWS,),
        in_specs=[pl.BlockSpec((FUSED_ROWS, k), lambda i: (i, 0)), pl.BlockSpec((k, n), lambda i: (0, 0)), rows,
                  pl.BlockSpec((1, n), lambda i: (0, 0))],
        out_specs=[rows, rows],
        out_shape=[jax.ShapeDtypeStruct((m, n), F32), jax.ShapeDtypeStruct((m, n), BF16)],
        name=name, compiler_params=_params(("parallel",)),
    )(a, b, add, g)


def _matmul_norm_bwd(a, b, b_kblock, h, g, dres, name, prev=None):
    m, k = a.shape
    n = b.shape[0]

    def body(*refs):
        a_ref, b_ref, h_ref, g_ref, dres_ref = refs[:5]
        dh_ref, dg_ref = refs[-2:]
        dy = _dot(a_ref[...], b_ref[...], "nt")
        if prev is not None:
            dy = dy + refs[5][...]
        _, vjp = jax.vjp(lambda hh, gg: _rms_fn(0, hh, gg), h_ref[...], g_ref[...])
        dh, dg = vjp((dy,))
        dh_ref[...] = dres_ref[...] + dh

        @pl.when(pl.program_id(0) == 0)
        def _():
            dg_ref[...] = jnp.zeros(dg_ref.shape, F32)

        dg_ref[...] += dg

    rows = pl.BlockSpec((FUSED_ROWS, n), lambda i: (i, 0))
    gain = pl.BlockSpec((1, n), lambda i: (0, 0))
    operands = [a, b, h, g, dres] + ([] if prev is None else [prev])
    in_specs = [pl.BlockSpec((FUSED_ROWS, k), lambda i: (i, 0)), pl.BlockSpec((n, k), lambda i: (0, b_kblock)),
                rows, gain, rows] + ([] if prev is None else [rows])
    return pl.pallas_call(
        body, grid=(m // FUSED_ROWS,), in_specs=in_specs, out_specs=[rows, gain],
        out_shape=[jax.ShapeDtypeStruct((m, n), F32), jax.ShapeDtypeStruct((1, n), F32)],
        name=name, compiler_params=_params(("arbitrary",)),
    )(*operands)


SW_WIDTH = 256
SW_NCOL = D_FF // SW_WIDTH


def _ffn_gate_up(y, wgu):
    m, k = y.shape
    tm = _pick(m, (1408, 384, 128))

    def body(y_ref, wg_ref, wu_ref, a_ref, g_ref, u_ref):
        yv = y_ref[...]
        gt = _dot(yv, wg_ref[...], "nn")
        up = _dot(yv, wu_ref[...], "nn")
        a_ref[...] = (_silu(gt) * up).astype(a_ref.dtype)
        g_ref[...] = gt.astype(g_ref.dtype)
        u_ref[...] = up.astype(u_ref.dtype)

    tile = pl.BlockSpec((tm, SW_WIDTH), lambda i, j: (i, j))
    return pl.pallas_call(
        body, grid=(m // tm, SW_NCOL),
        in_specs=[pl.BlockSpec((tm, k), lambda i, j: (i, 0)),
                  pl.BlockSpec((k, SW_WIDTH), lambda i, j: (0, j)),
                  pl.BlockSpec((k, SW_WIDTH), lambda i, j: (0, SW_NCOL + j))],
        out_specs=[tile] * 3, out_shape=[jax.ShapeDtypeStruct((m, D_FF), BF16)] * 3,
        name="ffn_gate_up", compiler_params=_params(("parallel", "parallel")),
    )(y, wgu, wgu)


def _ffn_down_dx(dh2, wd, gt, up):
    m, k = dh2.shape
    tm = _pick(m, (1408, 384, 128))

    def body(d_ref, w_ref, g_ref, u_ref, dg_ref, du_ref):
        da = _dot(d_ref[...], w_ref[...], "nt")
        gv, uv = g_ref[...].astype(F32), u_ref[...].astype(F32)
        s = jax.nn.sigmoid(gv)
        dg_ref[...] = (da * uv * (s + gv * s * (1.0 - s))).astype(dg_ref.dtype)
        du_ref[...] = (da * gv * s).astype(du_ref.dtype)

    tile = pl.BlockSpec((tm, SW_WIDTH), lambda i, j: (i, j))
    return pl.pallas_call(
        body, grid=(m // tm, SW_NCOL),
        in_specs=[pl.BlockSpec((tm, k), lambda i, j: (i, 0)),
                  pl.BlockSpec((SW_WIDTH, k), lambda i, j: (j, 0)), tile, tile],
        out_specs=[tile] * 2, out_shape=[jax.ShapeDtypeStruct((m, D_FF), BF16)] * 2,
        name="ffn_down_dx", compiler_params=_params(("parallel", "parallel")),
    )(dh2, wd, gt, up)


def _ffn_fwd(h, y, wgu, wd, g_next):
    a, gt, up = _ffn_gate_up(y, wgu)
    if g_next is None:
        h2, y_next = _matmul(a, wd, "nn", F32, "ffn_down", add=h), None
    else:
        h2, y_next = _matmul_add_norm(a, wd, h, g_next, "ffn_down_norm")
    return h2, y_next, (h, y, gt, up, a)


def _ffn_bwd(dh2, saved, g, wgu, wd):
    h, y, gt, up, a = saved
    dgt, dup = _ffn_down_dx(dh2, wd, gt, up)
    dwd = _matmul(a, dh2, "tn", F32, "ffn_down_dw")
    dwg = _matmul(y, dgt, "tn", F32, "ffn_gate_dw")
    dwu = _matmul(y, dup, "tn", F32, "ffn_up_dw")
    dy = _matmul(dgt, wgu, "nt", F32, "ffn_gate_dx", b_kblock=0)
    dh, dg = _matmul_norm_bwd(dup, wgu, 1, h, g, dh2, "ffn_up_dx_norm", prev=dy)
    return dh, dg, (dwg, dwu), dwd


def _fox_consts():
    lane = jnp.arange(D_MODEL) // FOX_DH
    e_expand = (jnp.arange(LANE)[:, None] == lane[None, :]).astype(F32)
    e_mean = e_expand.T * (1.0 / FOX_DH)
    half = jnp.arange(LANE) // FOX_DH
    blockdiag = (half[:, None] == half[None, :]).astype(F32)
    return e_mean, e_expand, blockdiag


def _pair_layout(c):
    heads = c[:, :FOX_HEADS].T
    crow = heads.reshape(FOX_PAIRS, 2, ROWS)
    c2 = jnp.broadcast_to(crow.transpose(0, 2, 1)[:, :, :, None], (FOX_PAIRS, ROWS, 2, FOX_DH))
    crow = jnp.where(jnp.arange(ROWS) >= FIRST_ROW, crow, KEY_OFF)
    return c2.reshape(FOX_PAIRS, ROWS, LANE), crow


def _fox_fwd(y, p):
    proj = _matmul(y, p["w_in"], "nn", F32, "fox_in")
    e_mean, e_expand, _ = _fox_consts()
    nb = D_MODEL // LANE
    xs = [_x(proj, D_MODEL, 0), _x(proj, D_MODEL, 1), _x(proj, LANE, 4 * nb)]
    ps = [p["q_gain"], p["k_gain"], p["b_f"]]
    qn, kn, lf = _row_fwd(_fox_pre_fn, xs, ps, [e_mean, e_expand],
                          [(D_MODEL, D_MODEL, BF16), (D_MODEL, D_MODEL, BF16), (LANE, LANE, F32)], "fox_pre_fwd")
    c = _seq_cumsum(lf, False, "fox_cumsum")
    c2, crow = _pair_layout(c)
    o, lse2 = _fox_attn_fwd(qn, kn, proj, c2, crow)
    om = _row_fwd(_fox_post_fn, [_x(o, D_MODEL, 0), _x(proj, D_MODEL, 3)], [], [],
                  [(D_MODEL, D_MODEL, BF16)], "fox_post_fwd")[0]
    return om, (proj, qn, kn, c2, crow, o, lse2)


def _fox_bwd(dom, y, saved, p):
    proj, qn, kn, c2, crow, o, lse2 = saved
    e_mean, e_expand, blockdiag = _fox_consts()
    nb = D_MODEL // LANE
    do, dgate = _row_bwd(_fox_post_fn, [_x(o, D_MODEL, 0), _x(proj, D_MODEL, 3)], [], [], [_x(dom, D_MODEL)],
                         [(D_MODEL, F32), (D_MODEL, BF16)], "fox_post_bwd")
    dq, dk, dv, dcrow, dcq2 = _fox_attn_bwd(qn, kn, proj, c2, crow, lse2, o, do, blockdiag)
    dlf = _seq_cumsum(_pad_cols(dcrow.reshape(FOX_HEADS, ROWS).T, LANE), True, "fox_cumsum_rev", pairs=dcq2)
    xs = [_x(proj, D_MODEL, 0), _x(proj, D_MODEL, 1), _x(proj, LANE, 4 * nb)]
    ps = [p["q_gain"], p["k_gain"], p["b_f"]]
    dqr, dkr, df, dqg, dkg, dbf = _row_bwd(
        _fox_pre_fn, xs, ps, [e_mean, e_expand], [_x(dq, D_MODEL), _x(dk, D_MODEL), _x(dlf, LANE)],
        [(D_MODEL, BF16), (D_MODEL, BF16), (LANE, BF16)], "fox_pre_bwd")
    dproj = jnp.concatenate([dqr, dkr, dv, dgate, df], axis=1)
    grads = {
        "w_in": _matmul(y, dproj, "tn", F32, "fox_in_dw")[:, :FOX_IN],
        "q_gain": dqg.reshape(FOX_HEADS, FOX_DH).sum(axis=0),
        "k_gain": dkg.reshape(FOX_HEADS, FOX_DH).sum(axis=0),
        "b_f": dbf[0, :FOX_HEADS],
    }
    return dproj, grads


def _gla_fwd(y, p):
    proj = _matmul(y, p["w_in"], "nn", F32, "gla_in")
    small = (2 * GLA_QK + 2 * GLA_V) // LANE
    garr = _row_fwd(_gla_pre_fn, [_x(proj, LANE, small)], [p["w_alpha"], p["b_alpha"]], [],
                    [(GLA_QK, GLA_QK, F32)], "gla_pre_fwd")[0]
    o, states = _gla_scan_fwd(proj, garr)
    om = _row_fwd(_gla_post_fn, [_x(o, GLA_V, 0), _x(proj, GLA_V, 2)], [p["o_gain"]], [],
                  [(GLA_V, GLA_V, BF16)], "gla_post_fwd")[0]
    return om, (proj, garr, states, o)


def _gla_bwd(dom, y, saved, p):
    proj, garr, states, o = saved
    small = (2 * GLA_QK + 2 * GLA_V) // LANE
    do, dr, dgain = _row_bwd(_gla_post_fn, [_x(o, GLA_V, 0), _x(proj, GLA_V, 2)], [p["o_gain"]], [],
                             [_x(dom, GLA_V)], [(GLA_V, F32), (GLA_V, BF16)], "gla_post_bwd")
    dq, dk, dv, dg = _gla_scan_bwd(proj, garr, states, do)
    dalr, dwa, dba = _row_bwd(_gla_pre_fn, [_x(proj, LANE, small)], [p["w_alpha"], p["b_alpha"]], [],
                              [_x(dg, GLA_QK)], [(LANE, BF16)], "gla_pre_bwd")
    dproj = jnp.concatenate([dq, dk, dv, dr, dalr], axis=1)
    grads = {
        "w_in": _matmul(y, dproj, "tn", F32, "gla_in_dw")[:, :GLA_IN],
        "w_alpha": dwa[:GLA_RANK], "b_alpha": dba[0], "o_gain": dgain[0],
    }
    return dproj, grads


def _gdn_fwd(y, p):
    proj = _matmul(y, p["w_in"], "nn", F32, "gdn_in")
    nb = D_MODEL // LANE
    qn = _gdn_conv_fwd(proj, p["conv_q"], "q", 0)
    kn = _gdn_conv_fwd(proj, p["conv_k"], "k", 1)
    vn = _gdn_conv_fwd(proj, p["conv_v"], "v", 2)
    gates = _row_fwd(_gdn_gates_fn, [_x(proj, LANE, 4 * nb)], [p["a_log"], p["dt_bias"]], [],
                     [(LANE, LANE, F32)], "gdn_gates_fwd")[0]
    o, states = _gdn_scan_fwd(qn, kn, vn, gates)
    om = _row_fwd(_gdn_post_fn, [_x(o, D_MODEL, 0), _x(proj, D_MODEL, 3)], [p["o_gain"]], [],
                  [(D_MODEL, D_MODEL, BF16)], "gdn_post_fwd")[0]
    return om, (proj, qn, kn, vn, gates, states, o)


def _gdn_bwd(dom, y, saved, p):
    proj, qn, kn, vn, gates, states, o = saved
    nb = D_MODEL // LANE
    do, dgate, dgain = _row_bwd(_gdn_post_fn, [_x(o, D_MODEL, 0), _x(proj, D_MODEL, 3)], [p["o_gain"]], [],
                                [_x(dom, D_MODEL)], [(D_MODEL, F32), (D_MODEL, BF16)], "gdn_post_bwd")
    dqn, dkn, dvn, dgates = _gdn_scan_bwd(qn, kn, vn, gates, states, do)
    dsm, dalog, ddtb = _row_bwd(_gdn_gates_fn, [_x(proj, LANE, 4 * nb)], [p["a_log"], p["dt_bias"]], [],
                                [_x(dgates, LANE)], [(LANE, BF16)], "gdn_gates_bwd")
    dxq, dwq = _gdn_conv_bwd(proj, p["conv_q"], "q", 0, dqn)
    dxk, dwk = _gdn_conv_bwd(proj, p["conv_k"], "k", 1, dkn)
    dxv, dwv = _gdn_conv_bwd(proj, p["conv_v"], "v", 2, dvn)
    dproj = jnp.concatenate([dxq, dxk, dxv, dgate, dsm], axis=1)
    grads = {
        "w_in": _matmul(y, dproj, "tn", F32, "gdn_in_dw")[:, :GDN_IN],
        "conv_w": jnp.concatenate([dwq[:GDN_CONV], dwk[:GDN_CONV], dwv[:GDN_CONV]], axis=1),
        "a_log": dalog[0, :GDN_HEADS], "dt_bias": ddtb[0, :GDN_HEADS], "o_gain": dgain[0],
    }
    return dproj, grads


_MIXERS = ((_fox_fwd, _fox_bwd), (_gla_fwd, _gla_bwd), (_gdn_fwd, _gdn_bwd))


def _mixer_params(i, wts, rep):
    kind, j = i % 3, i // 3
    if kind == 0:
        return {
            "w_in": _pad_cols(wts["fox_w_in"], FOX_IN_PAD), "w_out": wts["fox_w_out"],
            "q_gain": jnp.tile(rep["fox_q_gain"][j], FOX_HEADS)[None],
            "k_gain": jnp.tile(rep["fox_k_gain"][j], FOX_HEADS)[None],
            "b_f": _pad_lanes(rep["fox_b_f"][j][None]),
        }
    if kind == 1:
        return {
            "w_in": _pad_cols(wts["gla_w_in"], GLA_IN_PAD), "w_out": wts["gla_w_out"],
            "w_alpha": jnp.pad(wts["gla_w_alpha2"], ((0, LANE - GLA_RANK), (0, 0))).astype(F32),
            "b_alpha": rep["gla_b_alpha"][j][None], "o_gain": rep["gla_o_gain"][j][None],
        }
    conv = jnp.pad(wts["gdn_conv_w"].astype(F32), ((0, 8 - GDN_CONV), (0, 0)))
    return {
        "w_in": _pad_cols(wts["gdn_w_in"], GDN_IN_PAD), "w_out": wts["gdn_w_out"],
        "conv_q": conv[:, :D_MODEL], "conv_k": conv[:, D_MODEL:2 * D_MODEL], "conv_v": conv[:, 2 * D_MODEL:],
        "a_log": _pad_lanes(rep["gdn_a_log"][j][None]), "dt_bias": _pad_lanes(rep["gdn_dt_bias"][j][None]),
        "o_gain": rep["gdn_o_gain"][j][None],
    }


def kernel(x, meta_tokens, norm_mix, norm_ffn, w_gate_up, w_down, fox_w_in, fox_b_f, fox_q_gain, fox_k_gain, fox_w_out, gla_w_in, gla_w_alpha2, gla_b_alpha, gla_o_gain, gla_w_out, gdn_w_in, gdn_conv_w, gdn_a_log, gdn_dt_bias, gdn_o_gain, gdn_w_out, loss_target, m_meta_tokens, m_norm_mix, m_norm_ffn, m_w_gate_up, m_w_down, m_fox_w_in, m_fox_b_f, m_fox_q_gain, m_fox_k_gain, m_fox_w_out, m_gla_w_in, m_gla_w_alpha2, m_gla_b_alpha, m_gla_o_gain, m_gla_w_out, m_gdn_w_in, m_gdn_conv_w, m_gdn_a_log, m_gdn_dt_bias, m_gdn_o_gain, m_gdn_w_out, v_meta_tokens, v_norm_mix, v_norm_ffn, v_w_gate_up, v_w_down, v_fox_w_in, v_fox_b_f, v_fox_q_gain, v_fox_k_gain, v_fox_w_out, v_gla_w_in, v_gla_w_alpha2, v_gla_b_alpha, v_gla_o_gain, v_gla_w_out, v_gdn_w_in, v_gdn_conv_w, v_gdn_a_log, v_gdn_dt_bias, v_gdn_o_gain, v_gdn_w_out):
    given = dict(locals())
    w_loc = {n: given[n] for n in WEIGHT_ORDER}
    m_loc = {n: given["m_" + n] for n in WEIGHT_ORDER}
    v_loc = {n: given["v_" + n] for n in WEIGHT_ORDER}

    rep = {n: w_loc[n] for n, _ in REPLICATED}
    me = 4 * lax.axis_index("x") + 2 * lax.axis_index("y") + lax.axis_index("c")

    def with_own(lands, owns):
        return [lax.dynamic_update_slice(land, own[None], (me, 0, 0)) for land, own in zip(lands, owns)]

    gathers = {}
    order = jnp.zeros((), F32)
    for i in range(DEPTH):
        groups = _layer_tensors(i)
        for part in PARTS:
            blocks = [(_layer_block(w_loc[n], idx) + order).astype(_wire(n)) for n, idx in groups[part]]
            started = _send_start(blocks, True, "gather_start_%d_%s" % (i, part))
            gathers[(i, part)] = (groups[part], blocks, started)
            order = started[4][0, 0]
    all_started = gathers[(DEPTH - 1, PARTS[-1])][2][4]

    def weights(i, part, after):
        tensors, blocks, started = gathers[(i, part)]
        lands = with_own(_send_wait(started, True, after, "gather_wait_%d_%s" % (i, part)), blocks)
        return {n: _from_gathered(g, n) for (n, _), g in zip(tensors, lands)}

    h = y = None
    saved = []
    for i in range(DEPTH):
        w_mix = weights(i, "mix", all_started if i == 0 else h)
        if i == 0:
            h = jnp.concatenate([jnp.zeros((FIRST_ROW, D_MODEL), F32), w_mix["meta_tokens"], x[0]], axis=0)
            y = _rms_fwd(h, rep["norm_mix"][0][None])
        p = _mixer_params(i, w_mix, rep)
        fwd, _ = _MIXERS[i % 3]
        om, mix_saved = fwd(y, p)
        h_mid, y_mid = _matmul_add_norm(om, p["w_out"], h, rep["norm_ffn"][i][None], "mixer_out_norm")
        w_ffn = weights(i, "ffn", h_mid)
        g_next = rep["norm_mix"][i + 1][None] if i + 1 < DEPTH else None
        h_out, y_next, ffn_saved = _ffn_fwd(h_mid, y_mid, w_ffn["w_gate_up"], w_ffn["w_down"], g_next)
        saved.append((p, w_ffn, h, y, om, mix_saved, ffn_saved))
        h, y = h_out, y_next

    dh, sq = _loss_head(h, loss_target[0])
    loss = lax.psum(0.5 * jnp.sum(sq) * (1.0 / D_MODEL), ("x", "y", "c"))

    gw = {n: [None] * DEPTH for n in ("norm_mix", "norm_ffn")}
    gm = {}
    reduces = {}

    def start_reduce(i, part, full, extra=()):
        tensors = _layer_tensors(i)[part]
        slabs = [_to_slabs(full[n], n).astype(_wire(n)) for n, _ in tensors] + list(extra)
        started = _send_start(slabs, False, "reduce_start_%d_%s" % (i, part))
        reduces[(i, part)] = (tensors, slabs, started)
        return started[4][0, 0]

    order = jnp.zeros((), BF16)
    for i in reversed(range(DEPTH)):
        p, w_ffn, h_in, y, om, mix_saved, ffn_saved = saved[i]
        kind, j = i % 3, i // 3
        pre = ("fox", "gla", "gdn")[kind]
        _, bwd = _MIXERS[kind]
        dh_mid, gw["norm_ffn"][i], dwgu, dwd = _ffn_bwd(
            dh, ffn_saved, rep["norm_ffn"][i][None], w_ffn["w_gate_up"], w_ffn["w_down"] + order)
        order = start_reduce(i, "ffn", {"w_gate_up": dwgu, "w_down": dwd}).astype(BF16)
        dom = _matmul(dh_mid, p["w_out"] + order, "nt", F32, "mixer_out_dx")
        dwo = _matmul(om, dh_mid, "tn", F32, "mixer_out_dw")
        dproj, grads = bwd(dom, y, mix_saved, p)
        gm[(kind, j)] = grads
        dh, gw["norm_mix"][i] = _matmul_norm_bwd(dproj, p["w_in"], 0, h_in, rep["norm_mix"][i][None], dh_mid,
                                                 "mixer_in_dx_norm")
        mix_full = {pre + "_w_in": grads["w_in"], pre + "_w_out": dwo, "gla_w_alpha2": grads.get("w_alpha"),
                    "gdn_conv_w": grads.get("conv_w"), "meta_tokens": dh[FIRST_ROW:PAD_ROWS]}
        if i > 0:
            order = start_reduce(i, "mix", mix_full).astype(BF16)

    partial = {
        "norm_mix": jnp.concatenate(gw["norm_mix"], axis=0), "norm_ffn": jnp.concatenate(gw["norm_ffn"], axis=0),
        "fox_b_f": jnp.stack([gm[(0, 0)]["b_f"], gm[(0, 1)]["b_f"]]),
        "fox_q_gain": jnp.stack([gm[(0, 0)]["q_gain"], gm[(0, 1)]["q_gain"]]),
        "fox_k_gain": jnp.stack([gm[(0, 0)]["k_gain"], gm[(0, 1)]["k_gain"]]),
        "gla_b_alpha": gm[(1, 0)]["b_alpha"][None], "gla_o_gain": gm[(1, 0)]["o_gain"][None],
        "gdn_a_log": gm[(2, 0)]["a_log"][None], "gdn_dt_bias": gm[(2, 0)]["dt_bias"][None],
        "gdn_o_gain": gm[(2, 0)]["o_gain"][None],
    }

    small = _pack([partial[n].reshape(-1) for n, _ in REPLICATED], 8, F32)
    last_started = start_reduce(0, "mix", mix_full, extra=[jnp.broadcast_to(small[None], (N_DEV,) + small.shape)])
    recv = {}

    def wait_reduce(key, after):
        tensors, slabs, started = reduces[key]
        owns = [lax.dynamic_index_in_dim(s, me, 0, keepdims=False) for s in slabs]
        lands = with_own(_send_wait(started, False, after, "reduce_wait_%d_%s" % key), owns)
        for (n, idx), land in zip(tensors, lands):
            recv[(n, idx)] = land
        return lands

    def adamw_tensor(n, s, order=0.0):
        layers = [idx for i in range(DEPTH) for part in PARTS for (nn, idx) in _layer_tensors(i)[part] if nn == n]
        layers = sorted(layers, key=lambda idx: -1 if idx is None else idx)
        outs = _adamw([recv[(n, idx)] for idx in layers], _local_2d(w_loc[n]) + order, _local_2d(m_loc[n]),
                      _local_2d(v_loc[n]), "adamw_" + n)
        for k in range(4):
            res[k][n] = outs[k].reshape(s)
        return outs[0][0, 0]

    last = (0, "mix")
    for key in reduces:
        if key != last:
            wait_reduce(key, dh)
    res = [{} for _ in range(4)]
    late = {n for n, _ in reduces[last][0]}
    done = jnp.zeros((), F32)
    for n, s, _ in SHARDED:
        if n not in late:
            done = done + adamw_tensor(n, s, last_started if _wire(n) == F32 else 0.0)
    recv_small = wait_reduce(last, done.reshape(1, 1))[-1]
    for n, s, _ in SHARDED:
        if n in late:
            adamw_tensor(n, s)
    outs_small = _adamw([recv_small], *[_pack([d[n].reshape(-1) for n, _ in REPLICATED], 8, F32)
                                        for d in (w_loc, m_loc, v_loc)], "adamw_replicated")
    for k in range(4):
        res[k].update(_unpack(outs_small[k], list(REPLICATED)))

    result = [loss, dh[PAD_ROWS:][None]]
    for k in range(4):
        result += [res[k][n] for n in WEIGHT_ORDER]
    return tuple(result)
```
